```python
import math
import jax, jax.numpy as jnp
from jax import lax
import numpy as np

D_MODEL = 1024
BATCH = 8
SEQ = 8192
DEPTH = 1

CHUNK = 64
MIX_WIDTH = D_MODEL
POOL_WIDTH = MIX_WIDTH // 2
ATTN_WIDTH = MIX_WIDTH - POOL_WIDTH
POOL_WINDOWS = (2, 4, 8, 16)
N_POOL_GROUPS = len(POOL_WINDOWS)
POOL_GROUP_DIM = POOL_WIDTH // N_POOL_GROUPS
HEAD_DIM = 64
N_HEADS = ATTN_WIDTH // HEAD_DIM
LEFT_CHUNKS = 8
BAND = (LEFT_CHUNKS + 1) * CHUNK
MAX_REL = 64
N_REL = 2 * MAX_REL + 1
IN_PROJ_WIDTH = 2 * POOL_WIDTH + 4 * ATTN_WIDTH
EPS = 1e-6
MASK_VALUE = -1e30

kernel_name = "hybrid_pool_chunkattn_block"


def rms_norm(x, g):
    xf = x.astype(jnp.float32)
    y = xf * lax.rsqrt(jnp.mean(xf * xf, axis=-1, keepdims=True) + EPS)
    return (y * g.astype(jnp.float32)).astype(x.dtype)


def multiscale_pool(v, pool_w, pool_scale):
    S = v.shape[1]
    vf = v.astype(jnp.float32)
    cs = jnp.pad(jnp.cumsum(vf, axis=1), ((0, 0), (1, 0), (0, 0)))
    t = jnp.arange(S)
    diffs = []
    for gi, w in enumerate(POOL_WINDOWS):
        sl = slice(gi * POOL_GROUP_DIM, (gi + 1) * POOL_GROUP_DIM)
        cs_g = cs[..., sl]
        lo = jnp.maximum(t + 1 - w, 0)
        win_sum = cs_g[:, 1:] - jnp.take(cs_g, lo, axis=1)
        count = (t + 1 - lo).astype(jnp.float32)[None, :, None]
        diffs.append(win_sum / count - vf[..., sl])
    d = jnp.stack(diffs, axis=2)
    y = jnp.einsum('bsgc,gcd->bsgd', d, pool_w.astype(jnp.float32))
    y = y.reshape(y.shape[0], S, POOL_WIDTH) * pool_scale.astype(jnp.float32)
    return y.astype(v.dtype)


def chunked_rel_attention(q, k, v, rel_bias):
    B, S, H, Dh = q.shape
    n_chunks = S // CHUNK
    pad = LEFT_CHUNKS * CHUNK
    kp = jnp.pad(k, ((0, 0), (pad, 0), (0, 0), (0, 0)))
    vp = jnp.pad(v, ((0, 0), (pad, 0), (0, 0), (0, 0)))
    qc = jnp.moveaxis(q.reshape(B, n_chunks, CHUNK, H, Dh), 1, 0)
    i = jnp.arange(CHUNK)
    j = jnp.arange(BAND)
    rel = j[None, :] - pad - i[:, None]
    rel_idx = jnp.clip(rel, -MAX_REL, MAX_REL) + MAX_REL
    bias = rel_bias.astype(jnp.float32)[:, rel_idx]
    scale = 1.0 / math.sqrt(Dh)

    def one_chunk(args):
        c, qb = args
        start = c * CHUNK
        kb = lax.dynamic_slice_in_dim(kp, start, BAND, axis=1)
        vb = lax.dynamic_slice_in_dim(vp, start, BAND, axis=1)
        s = jnp.einsum('bqhd,bkhd->bhqk', qb, kb).astype(jnp.float32) * scale + bias[None]
        valid = (start + j - pad) >= 0
        s = jnp.where(valid[None, None, None, :], s, MASK_VALUE)
        p = jax.nn.softmax(s, axis=-1)
        return jnp.einsum('bhqk,bkhd->bqhd', p.astype(vb.dtype), vb)

    out = lax.map(one_chunk, (jnp.arange(n_chunks), qc))
    return jnp.moveaxis(out, 0, 1).reshape(B, S, H * Dh)


def _fwd_setup_inputs(seed: int = 0) -> dict:
    key = jax.random.key(seed)
    ks = jax.random.split(key, 9)
    x = jax.random.normal(ks[0], (BATCH, SEQ, D_MODEL), jnp.float32)
    norm_gain = 1.0 + 0.02 * jax.random.normal(ks[1], (DEPTH, D_MODEL), jnp.float32)
    w_in = jax.random.normal(ks[2], (DEPTH, D_MODEL, IN_PROJ_WIDTH), jnp.float32) * D_MODEL ** -0.5
    pool_w = jax.random.normal(ks[3], (DEPTH, N_POOL_GROUPS, POOL_GROUP_DIM, POOL_GROUP_DIM), jnp.float32) * POOL_GROUP_DIM ** -0.5
    pool_scale = 1.0 + 0.02 * jax.random.normal(ks[4], (DEPTH, POOL_WIDTH), jnp.float32)
    rel_bias = 0.5 * jax.random.normal(ks[5], (DEPTH, N_HEADS, N_REL), jnp.float32)
    w_out = jax.random.normal(ks[6], (DEPTH, MIX_WIDTH, D_MODEL), jnp.float32) * MIX_WIDTH ** -0.5
    final_norm_gain = 1.0 + 0.02 * jax.random.normal(ks[7], (D_MODEL,), jnp.float32)
    return {"x": x, "norm_gain": norm_gain, "w_in": w_in, "pool_w": pool_w,
            "pool_scale": pool_scale, "rel_bias": rel_bias, "w_out": w_out,
            "final_norm_gain": final_norm_gain}


def _fwd_reference(x, norm_gain, w_in, pool_w, pool_scale, rel_bias, w_out, final_norm_gain):
    B, S, _ = x.shape
    for layer in range(DEPTH):
        h = rms_norm(x, norm_gain[layer])
        proj = jnp.einsum('bsd,de->bse', h, w_in[layer])
        o = 0
        pool_v = proj[..., o:o + POOL_WIDTH]; o += POOL_WIDTH
        pool_g = proj[..., o:o + POOL_WIDTH]; o += POOL_WIDTH
        q = proj[..., o:o + ATTN_WIDTH]; o += ATTN_WIDTH
        k = proj[..., o:o + ATTN_WIDTH]; o += ATTN_WIDTH
        v = proj[..., o:o + ATTN_WIDTH]; o += ATTN_WIDTH
        attn_g = proj[..., o:o + ATTN_WIDTH]
        y_pool = multiscale_pool(pool_v, pool_w[layer], pool_scale[layer]) * jax.nn.silu(pool_g)
        qh = q.reshape(B, S, N_HEADS, HEAD_DIM)
        kh = k.reshape(B, S, N_HEADS, HEAD_DIM)
        vh = v.reshape(B, S, N_HEADS, HEAD_DIM)
        y_attn = chunked_rel_attention(qh, kh, vh, rel_bias[layer]) * jax.nn.silu(attn_g)
        y = jnp.concatenate([y_pool, y_attn], axis=-1)
        x = x + jnp.einsum('bse,ed->bsd', y, w_out[layer])
    return rms_norm(x, final_norm_gain)


import jax as _jax
import jax.numpy as _jnp

TWIN_FORMAT = 'train_step'
FWD_PARAMS = ['x', 'norm_gain', 'w_in', 'pool_w', 'pool_scale', 'rel_bias', 'w_out', 'final_norm_gain']
TWIN_WEIGHTS = ['norm_gain', 'w_in', 'pool_w', 'pool_scale', 'rel_bias', 'w_out', 'final_norm_gain']
TWIN_DIFF_INPUT = 'x'
TWIN_INPUTS = ['x', 'norm_gain', 'w_in', 'pool_w', 'pool_scale', 'rel_bias', 'w_out', 'final_norm_gain', 'loss_target', 'm_norm_gain', 'm_w_in', 'm_pool_w', 'm_pool_scale', 'm_rel_bias', 'm_w_out', 'm_final_norm_gain', 'v_norm_gain', 'v_w_in', 'v_pool_w', 'v_pool_scale', 'v_rel_bias', 'v_w_out', 'v_final_norm_gain']
TWIN_OUTPUTS = ['loss', 'grad_x', 'grad_norm_gain', 'grad_w_in', 'grad_pool_w', 'grad_pool_scale', 'grad_rel_bias', 'grad_w_out', 'grad_final_norm_gain', 'delta_norm_gain', 'delta_w_in', 'delta_pool_w', 'delta_pool_scale', 'delta_rel_bias', 'delta_w_out', 'delta_final_norm_gain', 'new_m_norm_gain', 'new_m_w_in', 'new_m_pool_w', 'new_m_pool_scale', 'new_m_rel_bias', 'new_m_w_out', 'new_m_final_norm_gain', 'new_v_norm_gain', 'new_v_w_in', 'new_v_pool_w', 'new_v_pool_scale', 'new_v_rel_bias', 'new_v_w_out', 'new_v_final_norm_gain']
TWIN_LEAF_KINDS = {'loss': 'loss', 'grad_x': 'grad_x', 'grad_norm_gain': 'grad_w', 'grad_w_in': 'grad_w', 'grad_pool_w': 'grad_w', 'grad_pool_scale': 'grad_w', 'grad_rel_bias': 'grad_w', 'grad_w_out': 'grad_w', 'grad_final_norm_gain': 'grad_w', 'delta_norm_gain': 'delta_w', 'delta_w_in': 'delta_w', 'delta_pool_w': 'delta_w', 'delta_pool_scale': 'delta_w', 'delta_rel_bias': 'delta_w', 'delta_w_out': 'delta_w', 'delta_final_norm_gain': 'delta_w', 'new_m_norm_gain': 'new_m', 'new_m_w_in': 'new_m', 'new_m_pool_w': 'new_m', 'new_m_pool_scale': 'new_m', 'new_m_rel_bias': 'new_m', 'new_m_w_out': 'new_m', 'new_m_final_norm_gain': 'new_m', 'new_v_norm_gain': 'new_v', 'new_v_w_in': 'new_v', 'new_v_pool_w': 'new_v', 'new_v_pool_scale': 'new_v', 'new_v_rel_bias': 'new_v', 'new_v_w_out': 'new_v', 'new_v_final_norm_gain': 'new_v'}


def _forward(args):
    return _fwd_reference(*[args[k] for k in FWD_PARAMS])


def _output_shape():
    out = _jax.eval_shape(lambda: _forward(_fwd_setup_inputs(0)))
    return out.shape, out.dtype

N_MICROBATCH = 1
ADAM_LR = 0.001
ADAM_B1 = 0.9
ADAM_B2 = 0.999
ADAM_EPS = 1e-08
ADAM_WD = 0.01
ADAM_STEP = 10
PER_EXAMPLE_BATCH_AXIS = {'x': 0, 'loss_target': 0}
SHARED_INPUTS = []
_WEIGHT_DTYPES = {'norm_gain': _jnp.float32, 'w_in': _jnp.float32, 'pool_w': _jnp.float32, 'pool_scale': _jnp.float32, 'rel_bias': _jnp.float32, 'w_out': _jnp.float32, 'final_norm_gain': _jnp.float32}
MOMENT_SCALE = {'norm_gain': 1.290056e-01, 'w_in': 7.335656e-02, 'pool_w': 1.230765e-01, 'pool_scale': 1.225298e-01, 'rel_bias': 1.439884e-02, 'w_out': 8.709044e-02, 'final_norm_gain': 6.388162e+01}


def _to_microbatches(a, axis):
    t = _jnp.moveaxis(a, axis, 0)
    t = t.reshape((N_MICROBATCH, t.shape[0] // N_MICROBATCH) + t.shape[1:])
    return _jnp.moveaxis(t, 1, axis + 1)


def setup_inputs(seed: int = 0) -> dict:
    inp = _fwd_setup_inputs(seed)
    key = _jax.random.fold_in(_jax.random.key(seed), 7919)
    shape, _ = _output_shape()
    out = dict(inp)
    out["loss_target"] = _jax.random.normal(_jax.random.fold_in(key, 0), shape, _jnp.float32)
    for i, name in enumerate(TWIN_WEIGHTS):
        w = inp[name].astype(_jnp.float32)
        if MOMENT_SCALE is None:
            s = _jnp.sqrt(_jnp.mean(_jnp.square(w)) + 1e-30)
        else:
            s = MOMENT_SCALE[name]
        km, kv = _jax.random.split(_jax.random.fold_in(key, i + 1))
        out[name] = w
        out["m_" + name] = s * _jax.random.normal(km, w.shape, _jnp.float32)
        out["v_" + name] = (s * s) * _jax.random.uniform(kv, w.shape, _jnp.float32, 0.5, 1.5)
    if N_MICROBATCH > 1:
        for name, axis in PER_EXAMPLE_BATCH_AXIS.items():
            out[name] = _to_microbatches(out[name], axis)
    return {'x': out['x'], 'norm_gain': out['norm_gain'], 'w_in': out['w_in'], 'pool_w': out['pool_w'], 'pool_scale': out['pool_scale'], 'rel_bias': out['rel_bias'], 'w_out': out['w_out'], 'final_norm_gain': out['final_norm_gain'], 'loss_target': out['loss_target'], 'm_norm_gain': out['m_norm_gain'], 'm_w_in': out['m_w_in'], 'm_pool_w': out['m_pool_w'], 'm_pool_scale': out['m_pool_scale'], 'm_rel_bias': out['m_rel_bias'], 'm_w_out': out['m_w_out'], 'm_final_norm_gain': out['m_final_norm_gain'], 'v_norm_gain': out['v_norm_gain'], 'v_w_in': out['v_w_in'], 'v_pool_w': out['v_pool_w'], 'v_pool_scale': out['v_pool_scale'], 'v_rel_bias': out['v_rel_bias'], 'v_w_out': out['v_w_out'], 'v_final_norm_gain': out['v_final_norm_gain']}


def _loss(weights, diff, rest, loss_target):
    with _jax.named_scope("forward"):
        args = {**rest, TWIN_DIFF_INPUT: diff, **{k: w.astype(_WEIGHT_DTYPES[k]) for k, w in weights.items()}}
        y = _forward(args)
    with _jax.named_scope("loss_head"):
        err = _jnp.square(y.astype(_jnp.float32) - loss_target)
        return 0.5 * _jnp.sum(_jnp.mean(err, axis=-1)) if err.ndim else 0.5 * err


def _adamw(w, g, m, v):
    m = ADAM_B1 * m + (1.0 - ADAM_B1) * g
    v = ADAM_B2 * v + (1.0 - ADAM_B2) * _jnp.square(g)
    m_hat = m / (1.0 - ADAM_B1 ** ADAM_STEP)
    v_hat = v / (1.0 - ADAM_B2 ** ADAM_STEP)
    delta = -ADAM_LR * (m_hat / (_jnp.sqrt(v_hat) + ADAM_EPS) + ADAM_WD * w)
    return delta, m, v


def reference(x, norm_gain, w_in, pool_w, pool_scale, rel_bias, w_out, final_norm_gain, loss_target, m_norm_gain, m_w_in, m_pool_w, m_pool_scale, m_rel_bias, m_w_out, m_final_norm_gain, v_norm_gain, v_w_in, v_pool_w, v_pool_scale, v_rel_bias, v_w_out, v_final_norm_gain):
    given = dict(x=x, norm_gain=norm_gain, w_in=w_in, pool_w=pool_w, pool_scale=pool_scale, rel_bias=rel_bias, w_out=w_out, final_norm_gain=final_norm_gain, loss_target=loss_target, m_norm_gain=m_norm_gain, m_w_in=m_w_in, m_pool_w=m_pool_w, m_pool_scale=m_pool_scale, m_rel_bias=m_rel_bias, m_w_out=m_w_out, m_final_norm_gain=m_final_norm_gain, v_norm_gain=v_norm_gain, v_w_in=v_w_in, v_pool_w=v_pool_w, v_pool_scale=v_pool_scale, v_rel_bias=v_rel_bias, v_w_out=v_w_out, v_final_norm_gain=v_final_norm_gain)
    weights = {n: given[n] for n in TWIN_WEIGHTS}
    shared = {n: given[n] for n in SHARED_INPUTS}
    per_example = {n: given[n] for n in ['x']}
    grad_fn = _jax.value_and_grad(_loss, argnums=(0, 1))

    def one_microbatch(ex, loss_target):
        ex = dict(ex)
        diff = ex.pop(TWIN_DIFF_INPUT)
        return grad_fn(weights, diff, {**shared, **ex}, loss_target)

    if N_MICROBATCH == 1:
        loss, (grad_w, grad_x) = one_microbatch(per_example, given["loss_target"])
    else:
        def body(carry, xs):
            loss_sum, grad_sum = carry
            l_k, (gw_k, gx_k) = one_microbatch(xs[0], xs[1])
            with _jax.named_scope("update"):
                return (loss_sum + l_k, _jax.tree.map(_jnp.add, grad_sum, gw_k)), gx_k

        init = (_jnp.zeros((), _jnp.float32), _jax.tree.map(_jnp.zeros_like, weights))
        (loss, grad_w), grad_x = _jax.lax.scan(body, init, (per_example, given["loss_target"]))
    with _jax.named_scope("update"):
        delta_w, new_m, new_v = {}, {}, {}
        for n in TWIN_WEIGHTS:
            delta_w[n], new_m[n], new_v[n] = _adamw(weights[n], grad_w[n], given["m_" + n], given["v_" + n])
    return (loss, grad_x, *[grad_w[n] for n in TWIN_WEIGHTS], *[delta_w[n] for n in TWIN_WEIGHTS],
            *[new_m[n] for n in TWIN_WEIGHTS], *[new_v[n] for n in TWIN_WEIGHTS])
```

```python
import functools
import math

import jax
import jax.numpy as jnp
import numpy as np
from jax import lax
from jax.experimental import pallas as pl
from jax.experimental.pallas import tpu as pltpu

F32 = jnp.float32
BF16 = jnp.bfloat16
MESH_ID = pl.DeviceIdType.MESH

D_MODEL = 1024
POOL_WIDTH = 512
ATTN_WIDTH = 512
POOL_WINDOWS = (2, 4, 8, 16)
N_GROUPS = 4
GROUP_DIM = 128
HEAD_DIM = 64
N_HEADS = 8
CHUNK = 64
LEFT_CHUNKS = 8
MAX_REL = 64
N_REL = 2 * MAX_REL + 1
IN_WIDTH = 2 * POOL_WIDTH + 4 * ATTN_WIDTH
EPS = 1e-6
MASK_VALUE = -1e30
ATTN_SCALE = 1.0 / math.sqrt(HEAD_DIM)
ADAM_LR = 0.001
ADAM_B1 = 0.9
ADAM_B2 = 0.999
ADAM_EPS = 1e-08
ADAM_WD = 0.01
ADAM_STEP = 10

N_DEV = 8
IN_SHARD = IN_WIDTH // N_DEV
OUT_SHARD = D_MODEL // N_DEV

LANES = 128
TOKEN_TILE = 512
HALO = 16
Q_BLOCK = 256
KV_BLOCKS = 3
KV_WINDOW = KV_BLOCKS * Q_BLOCK
PAIR = 2 * HEAD_DIM
N_PAIRS = N_HEADS // 2
TOEPLITZ = 1024
VMEM_LIMIT = 56 * 1024 * 1024


def _params(sem=None, vmem=VMEM_LIMIT):
    return pltpu.CompilerParams(dimension_semantics=sem, vmem_limit_bytes=vmem)


def _sigmoid(x):
    return 1.0 / (1.0 + jnp.exp(-x))


def _nt(a, b):
    return lax.dot_general(a, b, (((1,), (1,)), ((), ())), preferred_element_type=F32)


def _tn(a, b):
    return lax.dot_general(a, b, (((0,), (0,)), ((), ())), preferred_element_type=F32)


def _nn(a, b):
    return jnp.dot(a, b, preferred_element_type=F32)


def _mesh_pos():
    return lax.axis_index("x"), lax.axis_index("y"), lax.axis_index("c")


def _gather_weights(w_in_shard, w_out_shard):
    def body(win_ref, wout_ref, gin_ref, gout_ref, sin_ref, sout_ref, send_sems, recv_sems):
        x, y, c = _mesh_pos()
        me, sibling = (x, y, c), (x, y, 1 - c)
        chips = [(1 - x, y), (x, 1 - y), (1 - x, 1 - y)]

        def idx(p):
            return 4 * p[0] + 2 * p[1] + p[2]

        sin_ref[...] = win_ref[...].astype(BF16)
        sout_ref[...] = wout_ref[...].astype(BF16)
        gin_ref[idx(me)] = sin_ref[...]
        gout_ref[idx(me)] = sout_ref[...]

        def copy(k, which, block, to, from_shard):
            out_ref, shard_ref = (gin_ref, sin_ref) if which == 0 else (gout_ref, sout_ref)
            return pltpu.make_async_remote_copy(
                src_ref=shard_ref if from_shard else out_ref.at[idx(block)],
                dst_ref=out_ref.at[idx(block)],
                send_sem=send_sems.at[7 * which + k],
                recv_sem=recv_sems.at[7 * which + k],
                device_id=to,
                device_id_type=MESH_ID,
            )

        first, passed = [], []
        for w in (0, 1):
            first.append(copy(0, w, me, sibling, True))
            first += [copy(1 + j, w, me, (*chip, c), True) for j, chip in enumerate(chips)]
        for cp in first:
            cp.start()
        for w in (0, 1):
            passed += [copy(4 + j, w, (*chip, c), sibling, False) for j, chip in enumerate(chips)]
        for w in (0, 1):
            for j, chip in enumerate(chips):
                copy(1 + j, w, (*chip, c), me, False).wait_recv()
                passed[3 * w + j].start()
        for w in (0, 1):
            copy(0, w, sibling, me, False).wait_recv()
            for j, chip in enumerate(chips):
                copy(4 + j, w, (*chip, 1 - c), me, False).wait_recv()
        for cp in first + passed:
            cp.wait_send()

    vm = pl.BlockSpec(memory_space=pltpu.VMEM)
    return pl.pallas_call(
        body,
        name="gather_weights",
        out_shape=(
            jax.ShapeDtypeStruct((N_DEV, D_MODEL, IN_SHARD), BF16),
            jax.ShapeDtypeStruct((N_DEV, OUT_SHARD, D_MODEL), BF16),
        ),
        in_specs=[vm, vm],
        out_specs=(vm, vm),
        scratch_shapes=[
            pltpu.VMEM((D_MODEL, IN_SHARD), BF16),
            pltpu.VMEM((OUT_SHARD, D_MODEL), BF16),
            pltpu.SemaphoreType.DMA((14,)),
            pltpu.SemaphoreType.DMA((14,)),
        ],
        compiler_params=_params(),
    )(w_in_shard, w_out_shard)


def _load_w_in(wg_hbm, wfull_ref, sem):
    copies = [
        pltpu.make_async_copy(wg_hbm.at[d], wfull_ref.at[:, d * IN_SHARD:(d + 1) * IN_SHARD], sem.at[d])
        for d in range(N_DEV)
    ]
    for cp in copies:
        cp.start()
    for cp in copies:
        cp.wait()


def _norm_inproj(x2d, g1, wg):
    t = x2d.shape[0]
    n_chunks = IN_WIDTH // POOL_WIDTH

    def body(x_ref, g_ref, wg_hbm, pvg_ref, qkv_ref, ag_ref, wfull_ref, sem):
        @pl.when(pl.program_id(0) == 0)
        def _():
            _load_w_in(wg_hbm, wfull_ref, sem)

        xf = x_ref[...]
        r = lax.rsqrt(jnp.mean(xf * xf, axis=-1, keepdims=True) + EPS)
        h = ((xf * r) * g_ref[...]).astype(BF16)
        for ci in range(n_chunks):
            res = _nn(h, wfull_ref[:, ci * POOL_WIDTH:(ci + 1) * POOL_WIDTH])
            if ci < 2:
                pvg_ref[:, ci * POOL_WIDTH:(ci + 1) * POOL_WIDTH] = res
            elif ci < 5:
                qkv_ref[:, (ci - 2) * POOL_WIDTH:(ci - 1) * POOL_WIDTH] = res.astype(BF16)
            else:
                ag_ref[...] = res

    return pl.pallas_call(
        body,
        name="norm_inproj",
        grid=(t // TOKEN_TILE,),
        out_shape=(
            jax.ShapeDtypeStruct((t, 2 * POOL_WIDTH), F32),
            jax.ShapeDtypeStruct((t, 3 * ATTN_WIDTH), BF16),
            jax.ShapeDtypeStruct((t, ATTN_WIDTH), F32),
        ),
        in_specs=[
            pl.BlockSpec((TOKEN_TILE, D_MODEL), lambda i: (i, 0)),
            pl.BlockSpec((1, D_MODEL), lambda i: (0, 0)),
            pl.BlockSpec(memory_space=pl.ANY),
        ],
        out_specs=(
            pl.BlockSpec((TOKEN_TILE, 2 * POOL_WIDTH), lambda i: (i, 0)),
            pl.BlockSpec((TOKEN_TILE, 3 * ATTN_WIDTH), lambda i: (i, 0)),
            pl.BlockSpec((TOKEN_TILE, ATTN_WIDTH), lambda i: (i, 0)),
        ),
        scratch_shapes=[pltpu.VMEM((D_MODEL, IN_WIDTH), BF16), pltpu.SemaphoreType.DMA((N_DEV,))],
        compiler_params=_params(("arbitrary",)),
    )(x2d, g1, wg)


def _inv_count(first_row, rows, window):
    tpos = first_row + lax.broadcasted_iota(jnp.int32, (rows, 1), 0)
    return 1.0 / jnp.minimum(tpos + 1, window).astype(F32)


def _causal_window_sum(ext, window):
    s, k = ext, 1
    while k < window:
        s = s + pltpu.roll(s, k, 0)
        k *= 2
    return s


def _pool_diffs(pv, halo, first_row, gi):
    w = POOL_WINDOWS[gi]
    sl = slice(gi * GROUP_DIM, (gi + 1) * GROUP_DIM)
    ext = jnp.concatenate([halo[:, sl], pv[:, sl]], axis=0)
    s = _causal_window_sum(ext, w)[HALO:]
    return s * _inv_count(first_row, pv.shape[0], w) - pv[:, sl]


def _pool_fwd(pvg, pool_w, pool_scale):
    t = pvg.shape[0]
    halo_per_tile = TOKEN_TILE // HALO

    def body(cur_ref, halo_ref, pw_ref, ps_ref, y_ref):
        i = pl.program_id(0)
        pv = cur_ref[:, :POOL_WIDTH]
        pg = cur_ref[:, POOL_WIDTH:]
        halo = jnp.where(i > 0, halo_ref[...], 0.0)
        for gi in range(N_GROUPS):
            sl = slice(gi * GROUP_DIM, (gi + 1) * GROUP_DIM)
            d = _pool_diffs(pv, halo, i * TOKEN_TILE, gi)
            z = _nn(d.astype(BF16), pw_ref[gi].astype(BF16))
            g = pg[:, sl]
            y_ref[:, sl] = ((z * ps_ref[:, sl]) * (g * _sigmoid(g))).astype(BF16)

    return pl.pallas_call(
        body,
        name="pool_fwd",
        grid=(t // TOKEN_TILE,),
        out_shape=jax.ShapeDtypeStruct((t, POOL_WIDTH), BF16),
        in_specs=[
            pl.BlockSpec((TOKEN_TILE, 2 * POOL_WIDTH), lambda i: (i, 0)),
            pl.BlockSpec((HALO, POOL_WIDTH), lambda i: (jnp.maximum(i * halo_per_tile - 1, 0), 0)),
            pl.BlockSpec((N_GROUPS, GROUP_DIM, GROUP_DIM), lambda i: (0, 0, 0)),
            pl.BlockSpec((1, POOL_WIDTH), lambda i: (0, 0)),
        ],
        out_specs=pl.BlockSpec((TOKEN_TILE, POOL_WIDTH), lambda i: (i, 0)),
        compiler_params=_params(("arbitrary",)),
    )(pvg, pvg, pool_w, pool_scale)


def _pool_bwd(pvg, dy_pool, pool_w, pool_scale):
    t = pvg.shape[0]
    n_tiles = t // TOKEN_TILE
    halo_per_tile = TOKEN_TILE // HALO
    last_halo = t // HALO - 1

    def body(cur_ref, prev_ref, pgn_ref, dy_ref, dyn_ref, pw_ref, ps_ref, dp_ref, dpw_ref, dps_ref):
        i = pl.program_id(0)

        @pl.when(i == 0)
        def _():
            dpw_ref[...] = jnp.zeros_like(dpw_ref)
            dps_ref[...] = jnp.zeros_like(dps_ref)

        pv = cur_ref[:, :POOL_WIDTH]
        pg = cur_ref[:, POOL_WIDTH:]
        prev = jnp.where(i > 0, prev_ref[...], 0.0)
        has_next = i < n_tiles - 1
        rows = TOKEN_TILE + HALO
        for gi in range(N_GROUPS):
            w = POOL_WINDOWS[gi]
            sl = slice(gi * GROUP_DIM, (gi + 1) * GROUP_DIM)
            pw = pw_ref[gi].astype(BF16)
            ps = ps_ref[:, sl]
            d = _pool_diffs(pv, prev, i * TOKEN_TILE, gi).astype(BF16)
            z = _nn(d, pw)
            g_ext = jnp.concatenate([pg[:, sl], pgn_ref[:, sl]], axis=0)
            dy_ext = jnp.concatenate([dy_ref[:, sl], dyn_ref[:, sl]], axis=0).astype(F32)
            sig = _sigmoid(g_ext)
            gate = g_ext * sig
            dz_ext = ((dy_ext * gate) * ps).astype(BF16)
            dd_ext = _nt(dz_ext, pw)
            e = dd_ext * _inv_count(i * TOKEN_TILE, rows, w)
            row = lax.broadcasted_iota(jnp.int32, (rows, 1), 0)
            e = jnp.where(jnp.logical_or(row < TOKEN_TILE, has_next), e, 0.0)
            s, k = e, 1
            while k < w:
                s = s + pltpu.roll(s, rows - k, 0)
                k *= 2
            dp_ref[:, sl] = (s[:TOKEN_TILE] - dd_ext[:TOKEN_TILE]).astype(BF16)
            dy = dy_ext[:TOKEN_TILE]
            g = g_ext[:TOKEN_TILE]
            sg = sig[:TOKEN_TILE]
            dgate = sg * (1.0 + g * (1.0 - sg))
            dp_ref[:, POOL_WIDTH + gi * GROUP_DIM:POOL_WIDTH + (gi + 1) * GROUP_DIM] = (
                (dy * (z * ps)) * dgate).astype(BF16)
            dps_ref[:, sl] += jnp.sum((dy * gate[:TOKEN_TILE]) * z, axis=0, keepdims=True)
            dpw_ref[gi] += _tn(d, dz_ext[:TOKEN_TILE])

    return pl.pallas_call(
        body,
        name="pool_bwd",
        grid=(n_tiles,),
        out_shape=(
            jax.ShapeDtypeStruct((t, 2 * POOL_WIDTH), BF16),
            jax.ShapeDtypeStruct((N_GROUPS, GROUP_DIM, GROUP_DIM), F32),
            jax.ShapeDtypeStruct((1, POOL_WIDTH), F32),
        ),
        in_specs=[
            pl.BlockSpec((TOKEN_TILE, 2 * POOL_WIDTH), lambda i: (i, 0)),
            pl.BlockSpec((HALO, POOL_WIDTH), lambda i: (jnp.maximum(i * halo_per_tile - 1, 0), 0)),
            pl.BlockSpec((HALO, POOL_WIDTH), lambda i: (jnp.minimum((i + 1) * halo_per_tile, last_halo), 1)),
            pl.BlockSpec((TOKEN_TILE, POOL_WIDTH), lambda i: (i, 0)),
            pl.BlockSpec((HALO, POOL_WIDTH), lambda i: (jnp.minimum((i + 1) * halo_per_tile, last_halo), 0)),
            pl.BlockSpec((N_GROUPS, GROUP_DIM, GROUP_DIM), lambda i: (0, 0, 0)),
            pl.BlockSpec((1, POOL_WIDTH), lambda i: (0, 0)),
        ],
        out_specs=(
            pl.BlockSpec((TOKEN_TILE, 2 * POOL_WIDTH), lambda i: (i, 0)),
            pl.BlockSpec((N_GROUPS, GROUP_DIM, GROUP_DIM), lambda i: (0, 0, 0)),
            pl.BlockSpec((1, POOL_WIDTH), lambda i: (0, 0)),
        ),
        compiler_params=_params(("arbitrary",)),
    )(pvg, pvg, pvg, dy_pool, dy_pool, pool_w, pool_scale)


_REL_FIRST = KV_WINDOW - 1 - MAX_REL
_DIAG_FIRST = _REL_FIRST - (Q_BLOCK - 1)


def _skew_rows(a, right):
    row = lax.broadcasted_iota(jnp.int32, a.shape, 0)
    for b in range(8):
        shift = (1 << b) if right else TOEPLITZ - (1 << b)
        a = jnp.where((row >> b) & 1 == 1, pltpu.roll(a, shift, 1), a)
    return a


def _bias_tile(rel_line):
    def body(line_ref, out_ref):
        a = jnp.broadcast_to(line_ref[0], (Q_BLOCK, TOEPLITZ))
        a = _skew_rows(a, True)
        a = pltpu.roll(a, TOEPLITZ - (Q_BLOCK - 1), 1)
        a = a[:, :KV_WINDOW]
        qc = lax.broadcasted_iota(jnp.int32, a.shape, 0) // CHUNK
        kc = lax.broadcasted_iota(jnp.int32, a.shape, 1) // CHUNK
        visible = jnp.logical_and(kc >= qc, kc <= qc + LEFT_CHUNKS)
        out_ref[0] = jnp.where(visible, a, MASK_VALUE)

    return pl.pallas_call(
        body,
        name="bias_tile",
        grid=(N_HEADS,),
        out_shape=jax.ShapeDtypeStruct((N_HEADS, Q_BLOCK, KV_WINDOW), F32),
        in_specs=[pl.BlockSpec((1, 1, TOEPLITZ), lambda h: (h, 0, 0))],
        out_specs=pl.BlockSpec((1, Q_BLOCK, KV_WINDOW), lambda h: (h, 0, 0)),
        compiler_params=_params(("arbitrary",)),
    )(rel_line)


def _bias_grad(ds_sum):
    def body(ds_ref, out_ref):
        a = jnp.concatenate([ds_ref[0], jnp.zeros((Q_BLOCK, TOEPLITZ - KV_WINDOW), F32)], axis=1)
        a = _skew_rows(a, False)
        diag = jnp.sum(a, axis=0, keepdims=True)
        c = lax.broadcasted_iota(jnp.int32, diag.shape, 1)
        far = jnp.logical_or(c <= _DIAG_FIRST, c > KV_WINDOW)
        near = jnp.logical_and(c >= _DIAG_FIRST + 2 * MAX_REL, c <= KV_WINDOW)
        low = jnp.sum(jnp.where(far, diag, 0.0), axis=1, keepdims=True)
        high = jnp.sum(jnp.where(near, diag, 0.0), axis=1, keepdims=True)
        line = pltpu.roll(diag, TOEPLITZ - _DIAG_FIRST, 1)[:, :2 * LANES]
        r = lax.broadcasted_iota(jnp.int32, line.shape, 1)
        line = jnp.where(r == 0, low, jnp.where(r == 2 * MAX_REL, high, jnp.where(r < 2 * MAX_REL, line, 0.0)))
        out_ref[0] = line

    return pl.pallas_call(
        body,
        name="bias_grad",
        grid=(N_HEADS,),
        out_shape=jax.ShapeDtypeStruct((N_HEADS, 1, 2 * LANES), F32),
        in_specs=[pl.BlockSpec((1, Q_BLOCK, KV_WINDOW), lambda h: (h, 0, 0))],
        out_specs=pl.BlockSpec((1, 1, 2 * LANES), lambda h: (h, 0, 0)),
        compiler_params=_params(("arbitrary",)),
    )(ds_sum)


def _head_lanes(hh):
    lane = lax.broadcasted_iota(jnp.int32, (1, PAIR), 1)
    return (lane < HEAD_DIM) if hh == 0 else (lane >= HEAD_DIM)


def _window_valid(i):
    col = lax.broadcasted_iota(jnp.int32, (1, KV_WINDOW), 1)
    return col >= (KV_BLOCKS - 1 - i) * Q_BLOCK


def _probs(q_h, kw, bias, valid):
    s = _nt(q_h, kw) * ATTN_SCALE + bias
    s = jnp.where(valid, s, MASK_VALUE)
    e = jnp.exp(s - jnp.max(s, axis=-1, keepdims=True))
    return e * (1.0 / jnp.sum(e, axis=-1, keepdims=True))


def _kv_specs(col0, n_blocks, query_block):
    def spec(back):
        def index(j, i):
            return (jnp.clip(query_block(i) - back, 0, n_blocks - 1), col0 + j)
        return pl.BlockSpec((Q_BLOCK, PAIR), index)
    return [spec(2), spec(1), spec(0)]


def _attn_fwd(qkv, ag, bias_tile):
    t = qkv.shape[0]
    nb = t // Q_BLOCK

    def body(q_ref, k0, k1, k2, v0, v1, v2, ag_ref, bias_ref, o_ref, y_ref):
        i = pl.program_id(1)
        q = q_ref[...]
        kw = jnp.concatenate([k0[...], k1[...], k2[...]], axis=0)
        vw = jnp.concatenate([v0[...], v1[...], v2[...]], axis=0)
        valid = _window_valid(i)
        outs = []
        for hh in range(2):
            q_h = jnp.where(_head_lanes(hh), q, jnp.zeros_like(q))
            p = _probs(q_h, kw, bias_ref[hh], valid)
            outs.append(_nn(p.astype(BF16), vw))
        o = jnp.where(_head_lanes(0), outs[0], outs[1])
        g = ag_ref[...]
        o_ref[...] = o.astype(BF16)
        y_ref[...] = (o * (g * _sigmoid(g))).astype(BF16)

    same = lambda i: i
    blk = pl.BlockSpec((Q_BLOCK, PAIR), lambda j, i: (i, j))
    return pl.pallas_call(
        body,
        name="attn_fwd",
        grid=(N_PAIRS, nb),
        out_shape=(jax.ShapeDtypeStruct((t, ATTN_WIDTH), BF16), jax.ShapeDtypeStruct((t, ATTN_WIDTH), BF16)),
        in_specs=[blk] + _kv_specs(N_PAIRS, nb, same) + _kv_specs(2 * N_PAIRS, nb, same) + [
            blk,
            pl.BlockSpec((2, Q_BLOCK, KV_WINDOW), lambda j, i: (j, 0, 0)),
        ],
        out_specs=(blk, blk),
        compiler_params=_params(("arbitrary", "arbitrary")),
    )(qkv, qkv, qkv, qkv, qkv, qkv, qkv, ag, bias_tile)


def _attn_bwd(qkv, ag, o, dy_attn, bias_tile):
    t = qkv.shape[0]
    nb = t // Q_BLOCK
    n_steps = nb + KV_BLOCKS - 1

    def body(q_ref, k0, k1, k2, v0, v1, v2, ag_ref, o_ref, dy_ref, bias_ref,
             dq_ref, dk_ref, dv_ref, dag_ref, ds_ref, dk_acc, dv_acc):
        i = pl.program_id(1)

        @pl.when(i == 0)
        def _():
            dk_acc[...] = jnp.zeros_like(dk_acc)
            dv_acc[...] = jnp.zeros_like(dv_acc)
            ds_ref[...] = jnp.zeros_like(ds_ref)

        @pl.when(jnp.logical_and(i > 0, i < nb))
        def _():
            dk_acc[lax.rem(i, KV_BLOCKS)] = jnp.zeros((Q_BLOCK, PAIR), F32)
            dv_acc[lax.rem(i, KV_BLOCKS)] = jnp.zeros((Q_BLOCK, PAIR), F32)

        @pl.when(i < nb)
        def _():
            q = q_ref[...]
            kw = jnp.concatenate([k0[...], k1[...], k2[...]], axis=0)
            vw = jnp.concatenate([v0[...], v1[...], v2[...]], axis=0)
            valid = _window_valid(i)
            g = ag_ref[...]
            sig = _sigmoid(g)
            dy = dy_ref[...].astype(F32)
            of = o_ref[...].astype(F32)
            do = dy * (g * sig)
            dag_ref[...] = ((dy * of) * (sig * (1.0 + g * (1.0 - sig)))).astype(BF16)
            dqs = []
            dkw = jnp.zeros((KV_WINDOW, PAIR), F32)
            dvw = jnp.zeros((KV_WINDOW, PAIR), F32)
            for hh in range(2):
                lanes = _head_lanes(hh)
                q_h = jnp.where(lanes, q, jnp.zeros_like(q))
                do_h = jnp.where(lanes, do, 0.0).astype(BF16)
                p = _probs(q_h, kw, bias_ref[hh], valid)
                dp = _nt(do_h, vw)
                ds = p * (dp - jnp.sum(p * dp, axis=-1, keepdims=True))
                ds_ref[hh] += ds
                ds_b = ds.astype(BF16)
                dqs.append(_nn(ds_b, kw))
                dkw = dkw + _tn(ds_b, q_h)
                dvw = dvw + _tn(p.astype(BF16), do_h)
            dq_ref[...] = (jnp.where(_head_lanes(0), dqs[0], dqs[1]) * ATTN_SCALE).astype(BF16)
            dkw = dkw * ATTN_SCALE
            for b in range(KV_BLOCKS):
                slot = lax.rem(i + b + 1, KV_BLOCKS)
                dk_acc[slot] += dkw[b * Q_BLOCK:(b + 1) * Q_BLOCK]
                dv_acc[slot] += dvw[b * Q_BLOCK:(b + 1) * Q_BLOCK]

        done = lax.rem(i + 1, KV_BLOCKS)
        dk_ref[...] = dk_acc[done].astype(BF16)
        dv_ref[...] = dv_acc[done].astype(BF16)

    query = lambda i: jnp.minimum(i, nb - 1)
    blk = pl.BlockSpec((Q_BLOCK, PAIR), lambda j, i: (query(i), j))
    done_blk = pl.BlockSpec((Q_BLOCK, PAIR), lambda j, i: (jnp.maximum(i - (KV_BLOCKS - 1), 0), j))
    out = jax.ShapeDtypeStruct((t, ATTN_WIDTH), BF16)
    return pl.pallas_call(
        body,
        name="attn_bwd",
        grid=(N_PAIRS, n_steps),
        out_shape=(out, out, out, out, jax.ShapeDtypeStruct((N_HEADS, Q_BLOCK, KV_WINDOW), F32)),
        in_specs=[blk] + _kv_specs(N_PAIRS, nb, query) + _kv_specs(2 * N_PAIRS, nb, query) + [
            blk, blk, blk,
            pl.BlockSpec((2, Q_BLOCK, KV_WINDOW), lambda j, i: (j, 0, 0)),
        ],
        out_specs=(blk, done_blk, done_blk, blk, pl.BlockSpec((2, Q_BLOCK, KV_WINDOW), lambda j, i: (j, 0, 0))),
        scratch_shapes=[pltpu.VMEM((KV_BLOCKS, Q_BLOCK, PAIR), F32), pltpu.VMEM((KV_BLOCKS, Q_BLOCK, PAIR), F32)],
        compiler_params=_params(("arbitrary", "arbitrary")),
    )(qkv, qkv, qkv, qkv, qkv, qkv, qkv, ag, o, dy_attn, bias_tile)


def _outproj_loss(x2d, tgt2d, y_pool, y_attn, wout_g, g2):
    t = x2d.shape[0]
    n_tiles = t // TOKEN_TILE

    def body(x_ref, tgt_ref, yp_ref, ya_ref, w_ref, g_ref,
             dx2_ref, dyp_ref, dya_ref, dw_ref, dg_ref, loss_ref, acc_ref):
        i = pl.program_id(0)

        @pl.when(i == 0)
        def _():
            acc_ref[...] = jnp.zeros_like(acc_ref)
            dg_ref[...] = jnp.zeros_like(dg_ref)
            loss_ref[...] = jnp.zeros_like(loss_ref)

        w = w_ref[...].reshape(D_MODEL, D_MODEL)
        y = jnp.concatenate([yp_ref[...], ya_ref[...]], axis=1)
        x2 = x_ref[...] + _nn(y, w)
        r = lax.rsqrt(jnp.mean(x2 * x2, axis=-1, keepdims=True) + EPS)
        xh = x2 * r
        g = g_ref[...]
        diff = xh * g - tgt_ref[...]
        tok = jnp.sum(diff * diff, axis=-1, keepdims=True) * (1.0 / D_MODEL)
        loss_ref[...] += jnp.sum(tok, axis=0, keepdims=True)
        dout = diff * (1.0 / D_MODEL)
        dg_ref[...] += jnp.sum(dout * xh, axis=0, keepdims=True)
        u = dout * g
        dx2 = r * (u - xh * jnp.mean(u * xh, axis=-1, keepdims=True))
        dx2_ref[...] = dx2
        dx2_b = dx2.astype(BF16)
        dy = _nt(dx2_b, w)
        dyp_ref[...] = dy[:, :POOL_WIDTH].astype(BF16)
        dya_ref[...] = dy[:, POOL_WIDTH:].astype(BF16)
        acc_ref[...] += _tn(y, dx2_b)

        @pl.when(i == n_tiles - 1)
        def _():
            dw_ref[...] = acc_ref[...].reshape(N_DEV, OUT_SHARD, D_MODEL).astype(BF16)

    tile = lambda width: pl.BlockSpec((TOKEN_TILE, width), lambda i: (i, 0))
    return pl.pallas_call(
        body,
        name="outproj_loss",
        grid=(n_tiles,),
        out_shape=(
            jax.ShapeDtypeStruct((t, D_MODEL), F32),
            jax.ShapeDtypeStruct((t, POOL_WIDTH), BF16),
            jax.ShapeDtypeStruct((t, ATTN_WIDTH), BF16),
            jax.ShapeDtypeStruct((N_DEV, OUT_SHARD, D_MODEL), BF16),
            jax.ShapeDtypeStruct((1, D_MODEL), F32),
            jax.ShapeDtypeStruct((1, 1), F32),
        ),
        in_specs=[
            tile(D_MODEL), tile(D_MODEL), tile(POOL_WIDTH), tile(ATTN_WIDTH),
            pl.BlockSpec((N_DEV, OUT_SHARD, D_MODEL), lambda i: (0, 0, 0)),
            pl.BlockSpec((1, D_MODEL), lambda i: (0, 0)),
        ],
        out_specs=(
            tile(D_MODEL), tile(POOL_WIDTH), tile(ATTN_WIDTH),
            pl.BlockSpec((N_DEV, OUT_SHARD, D_MODEL), lambda i: (0, 0, 0)),
            pl.BlockSpec((1, D_MODEL), lambda i: (0, 0)),
            pl.BlockSpec((1, 1), lambda i: (0, 0)),
        ),
        scratch_shapes=[pltpu.VMEM((D_MODEL, D_MODEL), F32)],
        compiler_params=_params(("arbitrary",)),
    )(x2d, tgt2d, y_pool, y_attn, wout_g, g2)


def _dproj_specs():
    tile = lambda width: pl.BlockSpec((TOKEN_TILE, width), lambda i: (i, 0))
    return [tile(2 * POOL_WIDTH)] + [tile(ATTN_WIDTH)] * 4


def _inproj_bwd_dx(x2d, dx2, dproj, g1, wg):
    t = x2d.shape[0]

    def body(x_ref, dx2_ref, dp_ref, dq_ref, dk_ref, dv_ref, dag_ref, g_ref, wg_hbm, gx_ref, dg_ref, wfull_ref, sem):
        @pl.when(pl.program_id(0) == 0)
        def _():
            _load_w_in(wg_hbm, wfull_ref, sem)
            dg_ref[...] = jnp.zeros_like(dg_ref)

        dproj_t = jnp.concatenate([dp_ref[...], dq_ref[...], dk_ref[...], dv_ref[...], dag_ref[...]], axis=1)
        dh = _nt(dproj_t, wfull_ref[...])
        xf = x_ref[...]
        r = lax.rsqrt(jnp.mean(xf * xf, axis=-1, keepdims=True) + EPS)
        xh = xf * r
        dg_ref[...] += jnp.sum(dh * xh, axis=0, keepdims=True)
        u = dh * g_ref[...]
        gx_ref[...] = dx2_ref[...] + r * (u - xh * jnp.mean(u * xh, axis=-1, keepdims=True))

    tile = pl.BlockSpec((TOKEN_TILE, D_MODEL), lambda i: (i, 0))
    return pl.pallas_call(
        body,
        name="inproj_bwd_dx",
        grid=(t // TOKEN_TILE,),
        out_shape=(jax.ShapeDtypeStruct((t, D_MODEL), F32), jax.ShapeDtypeStruct((1, D_MODEL), F32)),
        in_specs=[tile, tile] + _dproj_specs() + [
            pl.BlockSpec((1, D_MODEL), lambda i: (0, 0)),
            pl.BlockSpec(memory_space=pl.ANY),
        ],
        out_specs=(tile, pl.BlockSpec((1, D_MODEL), lambda i: (0, 0))),
        scratch_shapes=[pltpu.VMEM((D_MODEL, IN_WIDTH), BF16), pltpu.SemaphoreType.DMA((N_DEV,))],
        compiler_params=_params(("arbitrary",)),
    )(x2d, dx2, *dproj, g1, wg)


def _inproj_bwd_dw(x2d, dproj, g1):
    t = x2d.shape[0]
    n_tiles = t // TOKEN_TILE
    widths = [2 * POOL_WIDTH] + [ATTN_WIDTH] * 4

    def body(x_ref, dp_ref, dq_ref, dk_ref, dv_ref, dag_ref, g_ref, out_ref, acc_ref):
        i = pl.program_id(0)

        @pl.when(i == 0)
        def _():
            acc_ref[...] = jnp.zeros_like(acc_ref)

        xf = x_ref[...]
        r = lax.rsqrt(jnp.mean(xf * xf, axis=-1, keepdims=True) + EPS)
        h = ((xf * r) * g_ref[...]).astype(BF16)
        col = 0
        for ref, width in zip((dp_ref, dq_ref, dk_ref, dv_ref, dag_ref), widths):
            for c0 in range(0, width, POOL_WIDTH):
                acc_ref[:, col:col + POOL_WIDTH] += _tn(h, ref[:, c0:c0 + POOL_WIDTH])
                col += POOL_WIDTH

        @pl.when(i == n_tiles - 1)
        def _():
            for d in range(N_DEV):
                out_ref[d] = acc_ref[:, d * IN_SHARD:(d + 1) * IN_SHARD].astype(BF16)

    return pl.pallas_call(
        body,
        name="inproj_bwd_dw",
        grid=(n_tiles,),
        out_shape=jax.ShapeDtypeStruct((N_DEV, D_MODEL, IN_SHARD), BF16),
        in_specs=[pl.BlockSpec((TOKEN_TILE, D_MODEL), lambda i: (i, 0))] + _dproj_specs() + [
            pl.BlockSpec((1, D_MODEL), lambda i: (0, 0)),
        ],
        out_specs=pl.BlockSpec((N_DEV, D_MODEL, IN_SHARD), lambda i: (0, 0, 0)),
        scratch_shapes=[pltpu.VMEM((D_MODEL, IN_WIDTH), F32)],
        compiler_params=_params(("arbitrary",)),
    )(x2d, *dproj, g1)


def _adamw(w, g, m, v):
    m = ADAM_B1 * m + (1.0 - ADAM_B1) * g
    v = ADAM_B2 * v + (1.0 - ADAM_B2) * (g * g)
    m_hat = m / (1.0 - ADAM_B1 ** ADAM_STEP)
    v_hat = v / (1.0 - ADAM_B2 ** ADAM_STEP)
    delta = -ADAM_LR * (m_hat / (jnp.sqrt(v_hat) + ADAM_EPS) + ADAM_WD * w)
    return delta, m, v


def _reduce_adamw(dwin_g, dwout_g, small_g, w_in, m_in, v_in, w_out, m_out, v_out, small_w, small_m, small_v):
    rows = small_g.shape[0]

    def body(dwin_ref, dwout_ref, sg_ref, win_ref, min_ref, vin_ref, wout_ref, mout_ref, vout_ref,
             sw_ref, sm_ref, sv_ref,
             gin_ref, din_ref, nmin_ref, nvin_ref, gout_ref, dout_ref, nmout_ref, nvout_ref,
             gs_ref, ds_ref, nms_ref, nvs_ref,
             rin_ref, rout_ref, rs_ref, send_sems, recv_sems):
        x, y, c = _mesh_pos()
        me = 4 * x + 2 * y + c

        def peer(r):
            return (x ^ ((r >> 2) & 1), y ^ ((r >> 1) & 1), c ^ (r & 1))

        def peer_idx(r):
            p = peer(r)
            return 4 * p[0] + 2 * p[1] + p[2]

        def copies(r):
            to = peer(r)
            args = dict(device_id=to, device_id_type=MESH_ID)
            return [
                pltpu.make_async_remote_copy(src_ref=dwin_ref.at[peer_idx(r)], dst_ref=rin_ref.at[r],
                                             send_sem=send_sems.at[r], recv_sem=recv_sems.at[r], **args),
                pltpu.make_async_remote_copy(src_ref=dwout_ref.at[peer_idx(r)], dst_ref=rout_ref.at[r],
                                             send_sem=send_sems.at[8 + r], recv_sem=recv_sems.at[8 + r], **args),
                pltpu.make_async_remote_copy(src_ref=sg_ref, dst_ref=rs_ref.at[r],
                                             send_sem=send_sems.at[16 + r], recv_sem=recv_sems.at[16 + r], **args),
            ]

        started = [cp for r in range(1, N_DEV) for cp in copies(r)]
        for cp in started:
            cp.start()
        rin_ref[0] = dwin_ref[me]
        rout_ref[0] = dwout_ref[me]
        rs_ref[0] = sg_ref[...]
        for cp in started:
            cp.wait_recv()
        for cp in started:
            cp.wait_send()

        g_in = rin_ref[0].astype(F32)
        g_out = rout_ref[0].astype(F32)
        for r in range(1, N_DEV):
            g_in = g_in + rin_ref[r].astype(F32)
            g_out = g_out + rout_ref[r].astype(F32)
        g_small = rs_ref[me]
        for s in range(1, N_DEV):
            g_small = g_small + rs_ref[me ^ s]

        for g, (w_ref, m_ref, v_ref), (g_out_ref, d_ref, nm_ref, nv_ref) in (
            (g_in, (win_ref, min_ref, vin_ref), (gin_ref, din_ref, nmin_ref, nvin_ref)),
            (g_out, (wout_ref, mout_ref, vout_ref), (gout_ref, dout_ref, nmout_ref, nvout_ref)),
            (g_small, (sw_ref, sm_ref, sv_ref), (gs_ref, ds_ref, nms_ref, nvs_ref)),
        ):
            delta, m_new, v_new = _adamw(w_ref[...], g, m_ref[...], v_ref[...])
            g_out_ref[...] = g
            d_ref[...] = delta
            nm_ref[...] = m_new
            nv_ref[...] = v_new

    vm = pl.BlockSpec(memory_space=pltpu.VMEM)
    shapes = [jax.ShapeDtypeStruct((D_MODEL, IN_SHARD), F32)] * 4
    shapes += [jax.ShapeDtypeStruct((OUT_SHARD, D_MODEL), F32)] * 4
    shapes += [jax.ShapeDtypeStruct((rows, LANES), F32)] * 4
    return pl.pallas_call(
        body,
        name="reduce_adamw",
        out_shape=tuple(shapes),
        in_specs=[vm] * 12,
        out_specs=tuple([vm] * 12),
        scratch_shapes=[
            pltpu.VMEM((N_DEV, D_MODEL, IN_SHARD), BF16),
            pltpu.VMEM((N_DEV, OUT_SHARD, D_MODEL), BF16),
            pltpu.VMEM((N_DEV, rows, LANES), F32),
            pltpu.SemaphoreType.DMA((3 * N_DEV,)),
            pltpu.SemaphoreType.DMA((3 * N_DEV,)),
        ],
        compiler_params=_params(),
    )(dwin_g, dwout_g, small_g, w_in, m_in, v_in, w_out, m_out, v_out, small_w, small_m, small_v)


_SMALL_ROWS = (8, 8, 8, 16, 512, 8)


def _pack_small(norm_gain, final_gain, pool_scale, rel_bias, pool_w, loss_row):
    rb = jnp.pad(rel_bias.reshape(N_HEADS, -1), ((0, 0), (0, 2 * LANES - rel_bias.shape[-1])))
    parts = [
        norm_gain.reshape(8, LANES), final_gain.reshape(8, LANES),
        jnp.pad(pool_scale.reshape(4, LANES), ((0, 4), (0, 0))),
        rb.reshape(16, LANES), pool_w.reshape(N_GROUPS * GROUP_DIM, GROUP_DIM), loss_row,
    ]
    return jnp.concatenate(parts, axis=0)


def _unpack_small(packed):
    offs = np.concatenate([[0], np.cumsum(_SMALL_ROWS)])
    seg = [packed[offs[k]:offs[k + 1]] for k in range(len(_SMALL_ROWS))]
    norm_gain = seg[0].reshape(1, D_MODEL)
    final_gain = seg[1].reshape(D_MODEL)
    pool_scale = seg[2][:4].reshape(1, POOL_WIDTH)
    rel_bias = seg[3].reshape(N_HEADS, 2 * LANES)[:, :N_REL].reshape(1, N_HEADS, N_REL)
    pool_w = seg[4].reshape(1, N_GROUPS, GROUP_DIM, GROUP_DIM)
    return norm_gain, pool_w, pool_scale, rel_bias, final_gain, seg[5]


def kernel(x, norm_gain, w_in, pool_w, pool_scale, rel_bias, w_out, final_norm_gain, loss_target, m_norm_gain, m_w_in, m_pool_w, m_pool_scale, m_rel_bias, m_w_out, m_final_norm_gain, v_norm_gain, v_w_in, v_pool_w, v_pool_scale, v_rel_bias, v_w_out, v_final_norm_gain):
    t = x.shape[1]
    assert x.shape[0] == 1 and t % TOKEN_TILE == 0 and t // Q_BLOCK >= KV_BLOCKS
    x2d = x[0]
    tgt2d = loss_target[0]
    g2 = final_norm_gain.reshape(1, D_MODEL)

    wg_in, wg_out = _gather_weights(w_in[0], w_out[0])

    rb = rel_bias[0]
    rel_line = jnp.concatenate([
        jnp.broadcast_to(rb[:, :1], (N_HEADS, _REL_FIRST)), rb,
        jnp.broadcast_to(rb[:, N_REL - 1:], (N_HEADS, TOEPLITZ - _REL_FIRST - N_REL)),
    ], axis=1).reshape(N_HEADS, 1, TOEPLITZ)
    bias_tile = _bias_tile(rel_line)

    pvg, qkv, ag = _norm_inproj(x2d, norm_gain, wg_in)
    y_pool = _pool_fwd(pvg, pool_w[0], pool_scale)
    o, y_attn = _attn_fwd(qkv, ag, bias_tile)
    dx2, dy_pool, dy_attn, dwout_g, d_g2, loss_sum = _outproj_loss(x2d, tgt2d, y_pool, y_attn, wg_out, g2)
    d_pool, d_pw, d_ps = _pool_bwd(pvg, dy_pool, pool_w[0], pool_scale)
    dq, dk, dv, dag, ds_sum = _attn_bwd(qkv, ag, o, dy_attn, bias_tile)
    d_rb = _bias_grad(ds_sum).reshape(N_HEADS, 2 * LANES)[:, :N_REL]
    dproj = (d_pool, dq, dk, dv, dag)
    grad_x, d_g1 = _inproj_bwd_dx(x2d, dx2, dproj, norm_gain, wg_in)
    dwin_g = _inproj_bwd_dw(x2d, dproj, norm_gain)

    loss_row = jnp.broadcast_to(0.5 * loss_sum, (8, LANES))
    small_g = _pack_small(d_g1, d_g2, d_ps, d_rb, d_pw, loss_row)
    zeros = jnp.zeros((8, LANES), F32)
    small_w = _pack_small(norm_gain, final_norm_gain, pool_scale, rel_bias, pool_w, zeros)
    small_m = _pack_small(m_norm_gain, m_final_norm_gain, m_pool_scale, m_rel_bias, m_pool_w, zeros)
    small_v = _pack_small(v_norm_gain, v_final_norm_gain, v_pool_scale, v_rel_bias, v_pool_w, zeros)

    res = _reduce_adamw(dwin_g, dwout_g, small_g, w_in[0], m_w_in[0], v_w_in[0],
                        w_out[0], m_w_out[0], v_w_out[0], small_w, small_m, small_v)
    big_in = [a[None] for a in res[0:4]]
    big_out = [a[None] for a in res[4:8]]
    small = [_unpack_small(a) for a in res[8:12]]
    loss = small[0][5][0, 0]

    def leaves(k):
        g1_, pw_, ps_, rb_, g2_, _ = small[k]
        return [g1_, big_in[k], pw_, ps_, rb_, big_out[k], g2_]

    return (loss, grad_x[None], *leaves(0), *leaves(1), *leaves(2), *leaves(3))
```

```python
import functools
import math

import jax
import jax.numpy as jnp
import numpy as np
from jax import lax
from jax.experimental import pallas as pl
from jax.experimental.pallas import tpu as pltpu

F32 = jnp.float32
BF16 = jnp.bfloat16
MESH_ID = pl.DeviceIdType.MESH

D_MODEL = 1024
POOL_WIDTH = 512
ATTN_WIDTH = 512
POOL_WINDOWS = (2, 4, 8, 16)
N_GROUPS = 4
GROUP_DIM = 128
HEAD_DIM = 64
N_HEADS = 8
CHUNK = 64
LEFT_CHUNKS = 8
MAX_REL = 64
N_REL = 2 * MAX_REL + 1
IN_WIDTH = 2 * POOL_WIDTH + 4 * ATTN_WIDTH
EPS = 1e-6
MASK_VALUE = -1e30
ATTN_SCALE = 1.0 / math.sqrt(HEAD_DIM)
ADAM_LR = 0.001
ADAM_B1 = 0.9
ADAM_B2 = 0.999
ADAM_EPS = 1e-08
ADAM_WD = 0.01
ADAM_STEP = 10

N_DEV = 8
IN_SHARD = IN_WIDTH // N_DEV
OUT_SHARD = D_MODEL // N_DEV

LANES = 128
TOKEN_TILE = 512
HALO = 16
Q_BLOCK = 256
KV_BLOCKS = 3
KV_WINDOW = KV_BLOCKS * Q_BLOCK
PAIR = 2 * HEAD_DIM
N_PAIRS = N_HEADS // 2
TOEPLITZ = 1024
VMEM_LIMIT = 56 * 1024 * 1024


def _params(sem=None, vmem=VMEM_LIMIT):
    return pltpu.CompilerParams(dimension_semantics=sem, vmem_limit_bytes=vmem)


def _sigmoid(x):
    return 1.0 / (1.0 + jnp.exp(-x))


def _nt(a, b):
    return lax.dot_general(a, b, (((1,), (1,)), ((), ())), preferred_element_type=F32)


def _tn(a, b):
    return lax.dot_general(a, b, (((0,), (0,)), ((), ())), preferred_element_type=F32)


def _nn(a, b):
    return jnp.dot(a, b, preferred_element_type=F32)


def _mesh_pos():
    return lax.axis_index("x"), lax.axis_index("y"), lax.axis_index("c")


def _gather_weights(w_in_shard, w_out_shard):
    def body(win_ref, wout_ref, gin_ref, gout_ref, sin_ref, sout_ref, send_sems, recv_sems):
        x, y, c = _mesh_pos()
        me, sibling = (x, y, c), (x, y, 1 - c)
        chips = [(1 - x, y), (x, 1 - y), (1 - x, 1 - y)]

        def idx(p):
            return 4 * p[0] + 2 * p[1] + p[2]

        sin_ref[...] = win_ref[...].astype(BF16)
        sout_ref[...] = wout_ref[...].astype(BF16)
        gin_ref[idx(me)] = sin_ref[...]
        gout_ref[idx(me)] = sout_ref[...]

        def copy(k, which, block, to, from_shard):
            out_ref, shard_ref = (gin_ref, sin_ref) if which == 0 else (gout_ref, sout_ref)
            return pltpu.make_async_remote_copy(
                src_ref=shard_ref if from_shard else out_ref.at[idx(block)],
                dst_ref=out_ref.at[idx(block)],
                send_sem=send_sems.at[7 * which + k],
                recv_sem=recv_sems.at[7 * which + k],
                device_id=to,
                device_id_type=MESH_ID,
            )

        first, passed = [], []
        for w in (0, 1):
            first.append(copy(0, w, me, sibling, True))
            first += [copy(1 + j, w, me, (*chip, c), True) for j, chip in enumerate(chips)]
        for cp in first:
            cp.start()
        for w in (0, 1):
            passed += [copy(4 + j, w, (*chip, c), sibling, False) for j, chip in enumerate(chips)]
        for w in (0, 1):
            for j, chip in enumerate(chips):
                copy(1 + j, w, (*chip, c), me, False).wait_recv()
                passed[3 * w + j].start()
        for w in (0, 1):
            copy(0, w, sibling, me, False).wait_recv()
            for j, chip in enumerate(chips):
                copy(4 + j, w, (*chip, 1 - c), me, False).wait_recv()
        for cp in first + passed:
            cp.wait_send()

    vm = pl.BlockSpec(memory_space=pltpu.VMEM)
    return pl.pallas_call(
        body,
        name="gather_weights",
        out_shape=(
            jax.ShapeDtypeStruct((N_DEV, D_MODEL, IN_SHARD), BF16),
            jax.ShapeDtypeStruct((N_DEV, OUT_SHARD, D_MODEL), BF16),
        ),
        in_specs=[vm, vm],
        out_specs=(vm, vm),
        scratch_shapes=[
            pltpu.VMEM((D_MODEL, IN_SHARD), BF16),
            pltpu.VMEM((OUT_SHARD, D_MODEL), BF16),
            pltpu.SemaphoreType.DMA((14,)),
            pltpu.SemaphoreType.DMA((14,)),
        ],
        compiler_params=_params(),
    )(w_in_shard, w_out_shard)


def _load_w_in(wg_hbm, wfull_ref, sem):
    copies = [
        pltpu.make_async_copy(wg_hbm.at[d], wfull_ref.at[:, d * IN_SHARD:(d + 1) * IN_SHARD], sem.at[d])
        for d in range(N_DEV)
    ]
    for cp in copies:
        cp.start()
    for cp in copies:
        cp.wait()


def _norm_inproj(x2d, g1, wg):
    t = x2d.shape[0]
    n_chunks = IN_WIDTH // POOL_WIDTH

    def body(x_ref, g_ref, wg_hbm, pvg_ref, qkv_ref, ag_ref, wfull_ref, sem):
        @pl.when(pl.program_id(0) == 0)
        def _():
            _load_w_in(wg_hbm, wfull_ref, sem)

        xf = x_ref[...]
        r = lax.rsqrt(jnp.mean(xf * xf, axis=-1, keepdims=True) + EPS)
        h = ((xf * r) * g_ref[...]).astype(BF16)
        for ci in range(n_chunks):
            res = _nn(h, wfull_ref[:, ci * POOL_WIDTH:(ci + 1) * POOL_WIDTH])
            if ci < 2:
                pvg_ref[:, ci * POOL_WIDTH:(ci + 1) * POOL_WIDTH] = res
            elif ci < 5:
                qkv_ref[:, (ci - 2) * POOL_WIDTH:(ci - 1) * POOL_WIDTH] = res.astype(BF16)
            else:
                ag_ref[...] = res

    return pl.pallas_call(
        body,
        name="norm_inproj",
        grid=(t // TOKEN_TILE,),
        out_shape=(
            jax.ShapeDtypeStruct((t, 2 * POOL_WIDTH), F32),
            jax.ShapeDtypeStruct((t, 3 * ATTN_WIDTH), BF16),
            jax.ShapeDtypeStruct((t, ATTN_WIDTH), F32),
        ),
        in_specs=[
            pl.BlockSpec((TOKEN_TILE, D_MODEL), lambda i: (i, 0)),
            pl.BlockSpec((1, D_MODEL), lambda i: (0, 0)),
            pl.BlockSpec(memory_space=pl.ANY),
        ],
        out_specs=(
            pl.BlockSpec((TOKEN_TILE, 2 * POOL_WIDTH), lambda i: (i, 0)),
            pl.BlockSpec((TOKEN_TILE, 3 * ATTN_WIDTH), lambda i: (i, 0)),
            pl.BlockSpec((TOKEN_TILE, ATTN_WIDTH), lambda i: (i, 0)),
        ),
        scratch_shapes=[pltpu.VMEM((D_MODEL, IN_WIDTH), BF16), pltpu.SemaphoreType.DMA((N_DEV,))],
        compiler_params=_params(("arbitrary",)),
    )(x2d, g1, wg)


def _inv_count(first_row, rows, window):
    tpos = first_row + lax.broadcasted_iota(jnp.int32, (rows, 1), 0)
    return 1.0 / jnp.minimum(tpos + 1, window).astype(F32)


def _causal_window_sum(ext, window):
    s, k = ext, 1
    while k < window:
        s = s + pltpu.roll(s, k, 0)
        k *= 2
    return s


def _pool_diffs(pv, halo, first_row, gi):
    w = POOL_WINDOWS[gi]
    sl = slice(gi * GROUP_DIM, (gi + 1) * GROUP_DIM)
    ext = jnp.concatenate([halo[:, sl], pv[:, sl]], axis=0)
    s = _causal_window_sum(ext, w)[HALO:]
    return s * _inv_count(first_row, pv.shape[0], w) - pv[:, sl]


def _pool_fwd(pvg, pool_w, pool_scale):
    t = pvg.shape[0]
    halo_per_tile = TOKEN_TILE // HALO

    def body(cur_ref, halo_ref, pw_ref, ps_ref, y_ref):
        i = pl.program_id(0)
        pv = cur_ref[:, :POOL_WIDTH]
        pg = cur_ref[:, POOL_WIDTH:]
        halo = jnp.where(i > 0, halo_ref[...], 0.0)
        for gi in range(N_GROUPS):
            sl = slice(gi * GROUP_DIM, (gi + 1) * GROUP_DIM)
            d = _pool_diffs(pv, halo, i * TOKEN_TILE, gi)
            z = _nn(d.astype(BF16), pw_ref[gi].astype(BF16))
            g = pg[:, sl]
            y_ref[:, sl] = ((z * ps_ref[:, sl]) * (g * _sigmoid(g))).astype(BF16)

    return pl.pallas_call(
        body,
        name="pool_fwd",
        grid=(t // TOKEN_TILE,),
        out_shape=jax.ShapeDtypeStruct((t, POOL_WIDTH), BF16),
        in_specs=[
            pl.BlockSpec((TOKEN_TILE, 2 * POOL_WIDTH), lambda i: (i, 0)),
            pl.BlockSpec((HALO, POOL_WIDTH), lambda i: (jnp.maximum(i * halo_per_tile - 1, 0), 0)),
            pl.BlockSpec((N_GROUPS, GROUP_DIM, GROUP_DIM), lambda i: (0, 0, 0)),
            pl.BlockSpec((1, POOL_WIDTH), lambda i: (0, 0)),
        ],
        out_specs=pl.BlockSpec((TOKEN_TILE, POOL_WIDTH), lambda i: (i, 0)),
        compiler_params=_params(("arbitrary",)),
    )(pvg, pvg, pool_w, pool_scale)


def _pool_bwd(pvg, dy_pool, pool_w, pool_scale):
    t = pvg.shape[0]
    n_tiles = t // TOKEN_TILE
    halo_per_tile = TOKEN_TILE // HALO
    last_halo = t // HALO - 1

    def body(cur_ref, prev_ref, pgn_ref, dy_ref, dyn_ref, pw_ref, ps_ref, dp_ref, dpw_ref, dps_ref):
        i = pl.program_id(0)

        @pl.when(i == 0)
        def _():
            dpw_ref[...] = jnp.zeros_like(dpw_ref)
            dps_ref[...] = jnp.zeros_like(dps_ref)

        pv = cur_ref[:, :POOL_WIDTH]
        pg = cur_ref[:, POOL_WIDTH:]
        prev = jnp.where(i > 0, prev_ref[...], 0.0)
        has_next = i < n_tiles - 1
        rows = TOKEN_TILE + HALO
        for gi in range(N_GROUPS):
            w = POOL_WINDOWS[gi]
            sl = slice(gi * GROUP_DIM, (gi + 1) * GROUP_DIM)
            pw = pw_ref[gi].astype(BF16)
            ps = ps_ref[:, sl]
            d = _pool_diffs(pv, prev, i * TOKEN_TILE, gi).astype(BF16)
            z = _nn(d, pw)
            g_ext = jnp.concatenate([pg[:, sl], pgn_ref[:, sl]], axis=0)
            dy_ext = jnp.concatenate([dy_ref[:, sl], dyn_ref[:, sl]], axis=0).astype(F32)
            sig = _sigmoid(g_ext)
            gate = g_ext * sig
            dz_ext = ((dy_ext * gate) * ps).astype(BF16)
            dd_ext = _nt(dz_ext, pw)
            e = dd_ext * _inv_count(i * TOKEN_TILE, rows, w)
            row = lax.broadcasted_iota(jnp.int32, (rows, 1), 0)
            e = jnp.where(jnp.logical_or(row < TOKEN_TILE, has_next), e, 0.0)
            s, k = e, 1
            while k < w:
                s = s + pltpu.roll(s, rows - k, 0)
                k *= 2
            dp_ref[:, sl] = (s[:TOKEN_TILE] - dd_ext[:TOKEN_TILE]).astype(BF16)
            dy = dy_ext[:TOKEN_TILE]
            g = g_ext[:TOKEN_TILE]
            sg = sig[:TOKEN_TILE]
            dgate = sg * (1.0 + g * (1.0 - sg))
            dp_ref[:, POOL_WIDTH + gi * GROUP_DIM:POOL_WIDTH + (gi + 1) * GROUP_DIM] = (
                (dy * (z * ps)) * dgate).astype(BF16)
            dps_ref[:, sl] += jnp.sum((dy * gate[:TOKEN_TILE]) * z, axis=0, keepdims=True)
            dpw_ref[gi] += _tn(d, dz_ext[:TOKEN_TILE])

    return pl.pallas_call(
        body,
        name="pool_bwd",
        grid=(n_tiles,),
        out_shape=(
            jax.ShapeDtypeStruct((t, 2 * POOL_WIDTH), BF16),
            jax.ShapeDtypeStruct((N_GROUPS, GROUP_DIM, GROUP_DIM), F32),
            jax.ShapeDtypeStruct((1, POOL_WIDTH), F32),
        ),
        in_specs=[
            pl.BlockSpec((TOKEN_TILE, 2 * POOL_WIDTH), lambda i: (i, 0)),
            pl.BlockSpec((HALO, POOL_WIDTH), lambda i: (jnp.maximum(i * halo_per_tile - 1, 0), 0)),
            pl.BlockSpec((HALO, POOL_WIDTH), lambda i: (jnp.minimum((i + 1) * halo_per_tile, last_halo), 1)),
            pl.BlockSpec((TOKEN_TILE, POOL_WIDTH), lambda i: (i, 0)),
            pl.BlockSpec((HALO, POOL_WIDTH), lambda i: (jnp.minimum((i + 1) * halo_per_tile, last_halo), 0)),
            pl.BlockSpec((N_GROUPS, GROUP_DIM, GROUP_DIM), lambda i: (0, 0, 0)),
            pl.BlockSpec((1, POOL_WIDTH), lambda i: (0, 0)),
        ],
        out_specs=(
            pl.BlockSpec((TOKEN_TILE, 2 * POOL_WIDTH), lambda i: (i, 0)),
            pl.BlockSpec((N_GROUPS, GROUP_DIM, GROUP_DIM), lambda i: (0, 0, 0)),
            pl.BlockSpec((1, POOL_WIDTH), lambda i: (0, 0)),
        ),
        compiler_params=_params(("arbitrary",)),
    )(pvg, pvg, pvg, dy_pool, dy_pool, pool_w, pool_scale)


_REL_FIRST = KV_WINDOW - 1 - MAX_REL
_DIAG_FIRST = _REL_FIRST - (Q_BLOCK - 1)


def _skew_rows(a, right):
    row = lax.broadcasted_iota(jnp.int32, a.shape, 0)
    for b in range(8):
        shift = (1 << b) if right else TOEPLITZ - (1 << b)
        a = jnp.where((row >> b) & 1 == 1, pltpu.roll(a, shift, 1), a)
    return a


def _bias_tile(rel_line):
    def body(line_ref, out_ref):
        a = jnp.broadcast_to(line_ref[0], (Q_BLOCK, TOEPLITZ))
        a = _skew_rows(a, True)
        a = pltpu.roll(a, TOEPLITZ - (Q_BLOCK - 1), 1)
        a = a[:, :KV_WINDOW]
        qc = lax.broadcasted_iota(jnp.int32, a.shape, 0) // CHUNK
        kc = lax.broadcasted_iota(jnp.int32, a.shape, 1) // CHUNK
        visible = jnp.logical_and(kc >= qc, kc <= qc + LEFT_CHUNKS)
        out_ref[0] = jnp.where(visible, a, MASK_VALUE)

    return pl.pallas_call(
        body,
        name="bias_tile",
        grid=(N_HEADS,),
        out_shape=jax.ShapeDtypeStruct((N_HEADS, Q_BLOCK, KV_WINDOW), F32),
        in_specs=[pl.BlockSpec((1, 1, TOEPLITZ), lambda h: (h, 0, 0))],
        out_specs=pl.BlockSpec((1, Q_BLOCK, KV_WINDOW), lambda h: (h, 0, 0)),
        compiler_params=_params(("arbitrary",)),
    )(rel_line)


def _bias_grad(ds_sum):
    def body(ds_ref, out_ref):
        a = jnp.concatenate([ds_ref[0], jnp.zeros((Q_BLOCK, TOEPLITZ - KV_WINDOW), F32)], axis=1)
        a = _skew_rows(a, False)
        diag = jnp.sum(a, axis=0, keepdims=True)
        c = lax.broadcasted_iota(jnp.int32, diag.shape, 1)
        far = jnp.logical_or(c <= _DIAG_FIRST, c > KV_WINDOW)
        near = jnp.logical_and(c >= _DIAG_FIRST + 2 * MAX_REL, c <= KV_WINDOW)
        low = jnp.sum(jnp.where(far, diag, 0.0), axis=1, keepdims=True)
        high = jnp.sum(jnp.where(near, diag, 0.0), axis=1, keepdims=True)
        line = pltpu.roll(diag, TOEPLITZ - _DIAG_FIRST, 1)[:, :2 * LANES]
        r = lax.broadcasted_iota(jnp.int32, line.shape, 1)
        line = jnp.where(r == 0, low, jnp.where(r == 2 * MAX_REL, high, jnp.where(r < 2 * MAX_REL, line, 0.0)))
        out_ref[0] = line

    return pl.pallas_call(
        body,
        name="bias_grad",
        grid=(N_HEADS,),
        out_shape=jax.ShapeDtypeStruct((N_HEADS, 1, 2 * LANES), F32),
        in_specs=[pl.BlockSpec((1, Q_BLOCK, KV_WINDOW), lambda h: (h, 0, 0))],
        out_specs=pl.BlockSpec((1, 1, 2 * LANES), lambda h: (h, 0, 0)),
        compiler_params=_params(("arbitrary",)),
    )(ds_sum)


def _head_lanes(hh):
    lane = lax.broadcasted_iota(jnp.int32, (1, PAIR), 1)
    return (lane < HEAD_DIM) if hh == 0 else (lane >= HEAD_DIM)


def _block_rows(first_block, n_blocks):
    if isinstance(first_block, int):
        return pl.ds(first_block * Q_BLOCK, n_blocks * Q_BLOCK)
    return pl.ds(pl.multiple_of(first_block * Q_BLOCK, Q_BLOCK), n_blocks * Q_BLOCK)


def _window_valid(i):
    col = lax.broadcasted_iota(jnp.int32, (1, KV_WINDOW), 1)
    return col >= (KV_BLOCKS - 1 - i) * Q_BLOCK


def _probs(q_h, kw, bias, valid):
    s = _nt(q_h, kw) * ATTN_SCALE + bias
    s = jnp.where(valid, s, MASK_VALUE)
    e = jnp.exp(s - jnp.max(s, axis=-1, keepdims=True))
    return e * (1.0 / jnp.sum(e, axis=-1, keepdims=True))


def _kv_specs(col0, n_blocks, query_block):
    def spec(back):
        def index(j, i):
            return (jnp.clip(query_block(i) - back, 0, n_blocks - 1), col0 + j)
        return pl.BlockSpec((Q_BLOCK, PAIR), index)
    return [spec(2), spec(1), spec(0)]


def _attn_fwd(qkv, ag, bias_tile):
    t = qkv.shape[0]
    nb = t // Q_BLOCK

    def body(q_ref, k_ref, v_ref, ag_ref, bias_ref, o_ref, y_ref, s_scr, p_scr, linv_scr):
        def scores(i, slot, nwin):
            q = q_ref[_block_rows(i, 1), :]
            kw = k_ref[_block_rows(i + 1 - nwin, nwin), :]
            for hh in range(2):
                q_h = jnp.where(_head_lanes(hh), q, jnp.zeros_like(q)) * ATTN_SCALE
                s_scr[slot, hh, :, :nwin * Q_BLOCK] = _nt(q_h, kw)

        def softmax(slot, nwin):
            for hh in range(2):
                s = s_scr[slot, hh, :, :nwin * Q_BLOCK] + bias_ref[hh, :, (KV_BLOCKS - nwin) * Q_BLOCK:]
                e = jnp.exp(s - jnp.max(s, axis=-1, keepdims=True))
                linv_scr[slot, hh] = 1.0 / jnp.sum(e, axis=-1, keepdims=True)
                p_scr[slot, hh, :, :nwin * Q_BLOCK] = e.astype(BF16)

        def output(i, slot, nwin):
            vw = v_ref[_block_rows(i + 1 - nwin, nwin), :]
            outs = [_nn(p_scr[slot, hh, :, :nwin * Q_BLOCK], vw) * linv_scr[slot, hh] for hh in range(2)]
            o = jnp.where(_head_lanes(0), outs[0], outs[1])
            g = ag_ref[_block_rows(i, 1), :]
            o_ref[_block_rows(i, 1), :] = o.astype(BF16)
            y_ref[_block_rows(i, 1), :] = (o * (g * _sigmoid(g))).astype(BF16)

        scores(0, 0, 1)
        scores(1, 1, 2)
        softmax(0, 1)
        scores(2, 0, 3)
        softmax(1, 2)
        output(0, 0, 1)
        scores(3, 1, 3)
        softmax(0, 3)
        output(1, 1, 2)

        def two_steps(k, carry):
            i = 3 + 2 * k
            scores(i + 1, 0, KV_BLOCKS)
            softmax(1, KV_BLOCKS)
            output(i - 1, 0, KV_BLOCKS)
            scores(i + 2, 1, KV_BLOCKS)
            softmax(0, KV_BLOCKS)
            output(i, 1, KV_BLOCKS)
            return carry

        lax.fori_loop(0, (nb - 4) // 2, two_steps, 0)
        last = (nb - 1) % 2
        softmax(last, KV_BLOCKS)
        output(nb - 2, 1 - last, KV_BLOCKS)
        output(nb - 1, last, KV_BLOCKS)

    col = lambda c0: pl.BlockSpec((t, PAIR), lambda j: (0, c0 + j))
    return pl.pallas_call(
        body,
        name="attn_fwd",
        grid=(N_PAIRS,),
        out_shape=(jax.ShapeDtypeStruct((t, ATTN_WIDTH), BF16), jax.ShapeDtypeStruct((t, ATTN_WIDTH), BF16)),
        in_specs=[col(0), col(N_PAIRS), col(2 * N_PAIRS), col(0),
                  pl.BlockSpec((2, Q_BLOCK, KV_WINDOW), lambda j: (j, 0, 0))],
        out_specs=(col(0), col(0)),
        scratch_shapes=[
            pltpu.VMEM((2, 2, Q_BLOCK, KV_WINDOW), F32),
            pltpu.VMEM((2, 2, Q_BLOCK, KV_WINDOW), BF16),
            pltpu.VMEM((2, 2, Q_BLOCK, 1), F32),
        ],
        compiler_params=_params(("arbitrary",)),
    )(qkv, qkv, qkv, ag, bias_tile)


def _attn_bwd(qkv, ag, o, dy_attn, bias_tile):
    t = qkv.shape[0]
    nb = t // Q_BLOCK

    def body(q_ref, k_ref, v_ref, ag_ref, o_ref, dy_ref, bias_ref, dq_ref, dk_ref, dv_ref, dag_ref, ds_ref,
             do_scr, s_scr, dp_scr, p_scr, dsb_scr, dq_scr, dk_acc, dv_acc):
        def gates(i, carry):
            rows = _block_rows(i, 1)
            g = ag_ref[rows, :]
            sig = _sigmoid(g)
            dy = dy_ref[rows, :].astype(F32)
            do_scr[rows, :] = (dy * (g * sig)).astype(BF16)
            dag_ref[rows, :] = ((dy * o_ref[rows, :].astype(F32)) * (sig * (1.0 + g * (1.0 - sig)))).astype(BF16)
            return carry

        lax.fori_loop(0, nb, gates, 0)
        ds_ref[...] = jnp.zeros_like(ds_ref)

        def nwin_of(i):
            return min(i + 1, KV_BLOCKS) if isinstance(i, int) else KV_BLOCKS

        def operands(i, hh):
            lanes = _head_lanes(hh)
            q = q_ref[_block_rows(i, 1), :]
            do = do_scr[_block_rows(i, 1), :]
            return (jnp.where(lanes, q, jnp.zeros_like(q)) * ATTN_SCALE, jnp.where(lanes, do, jnp.zeros_like(do)))

        def products(i, hh):
            nwin = nwin_of(i)
            win = _block_rows(i + 1 - nwin, nwin)
            q_h, do_h = operands(i, hh)
            s_scr[hh, :, :nwin * Q_BLOCK] = _nt(q_h, k_ref[win, :])
            dp_scr[hh, :, :nwin * Q_BLOCK] = _nt(do_h, v_ref[win, :])

        def grads(i, hh):
            nwin = nwin_of(i)
            w, off = nwin * Q_BLOCK, (KV_BLOCKS - nwin) * Q_BLOCK
            s = s_scr[hh, :, :w] + bias_ref[hh, :, off:]
            e = jnp.exp(s - jnp.max(s, axis=-1, keepdims=True))
            p = e * (1.0 / jnp.sum(e, axis=-1, keepdims=True))
            dp = dp_scr[hh, :, :w]
            ds = p * (dp - jnp.sum(p * dp, axis=-1, keepdims=True))
            ds_ref[hh, :, off:] += ds
            p_scr[hh, :, :w] = p.astype(BF16)
            dsb_scr[hh, :, :w] = ds.astype(BF16)

        def ring(block):
            return block % KV_BLOCKS if isinstance(block, int) else lax.rem(block, KV_BLOCKS)

        def accumulate(i, hh):
            nwin = nwin_of(i)
            w = nwin * Q_BLOCK
            win = _block_rows(i + 1 - nwin, nwin)
            q_h, do_h = operands(i, hh)
            ds_b = dsb_scr[hh, :, :w]
            dq_h = _nn(ds_b, k_ref[win, :]) * ATTN_SCALE
            dkw = _tn(ds_b, q_h)
            dvw = _tn(p_scr[hh, :, :w], do_h)
            for b in range(nwin):
                slot = ring(i + 1 - nwin + b)
                part = slice(b * Q_BLOCK, (b + 1) * Q_BLOCK)
                if hh == 0 and b == nwin - 1:
                    dk_acc[slot] = dkw[part]
                    dv_acc[slot] = dvw[part]
                else:
                    dk_acc[slot] += dkw[part]
                    dv_acc[slot] += dvw[part]
            if hh == 0:
                dq_scr[...] = dq_h
            else:
                dq_ref[_block_rows(i, 1), :] = jnp.where(_head_lanes(0), dq_scr[...], dq_h).astype(BF16)
                if not (isinstance(i, int) and i < KV_BLOCKS - 1):
                    flush(i - (KV_BLOCKS - 1))

        def flush(block):
            dk_ref[_block_rows(block, 1), :] = dk_acc[ring(block)].astype(BF16)
            dv_ref[_block_rows(block, 1), :] = dv_acc[ring(block)].astype(BF16)

        def tile(n):
            return n // 2, n % 2

        def step(n):
            if n + 1 < 2 * nb:
                products(*tile(n + 1))
            grads(*tile(n))
            if n >= 1:
                accumulate(*tile(n - 1))

        products(0, 0)
        for n in range(2 * KV_BLOCKS):
            step(n)

        def two_steps(i, carry):
            products(i, 1)
            grads(i, 0)
            accumulate(i - 1, 1)
            products(i + 1, 0)
            grads(i, 1)
            accumulate(i, 0)
            return carry

        lax.fori_loop(KV_BLOCKS, nb - 1, two_steps, 0)
        step(2 * nb - 2)
        step(2 * nb - 1)
        accumulate(nb - 1, 1)
        flush(nb - 2)
        flush(nb - 1)

    col = lambda c0: pl.BlockSpec((t, PAIR), lambda j: (0, c0 + j))
    once = pl.BlockSpec((t, PAIR), lambda j: (0, j), pipeline_mode=pl.Buffered(1))
    tile_spec = pl.BlockSpec((2, Q_BLOCK, KV_WINDOW), lambda j: (j, 0, 0))
    out = jax.ShapeDtypeStruct((t, ATTN_WIDTH), BF16)
    return pl.pallas_call(
        body,
        name="attn_bwd",
        grid=(N_PAIRS,),
        out_shape=(out, out, out, out, jax.ShapeDtypeStruct((N_HEADS, Q_BLOCK, KV_WINDOW), F32)),
        in_specs=[col(0), col(N_PAIRS), col(2 * N_PAIRS), once, col(0), col(0), tile_spec],
        out_specs=(col(0), col(0), col(0), col(0), tile_spec),
        scratch_shapes=[
            pltpu.VMEM((t, PAIR), BF16),
            pltpu.VMEM((2, Q_BLOCK, KV_WINDOW), F32),
            pltpu.VMEM((2, Q_BLOCK, KV_WINDOW), F32),
            pltpu.VMEM((2, Q_BLOCK, KV_WINDOW), BF16),
            pltpu.VMEM((2, Q_BLOCK, KV_WINDOW), BF16),
            pltpu.VMEM((Q_BLOCK, PAIR), F32),
            pltpu.VMEM((KV_BLOCKS, Q_BLOCK, PAIR), F32),
            pltpu.VMEM((KV_BLOCKS, Q_BLOCK, PAIR), F32),
        ],
        compiler_params=_params(("arbitrary",), vmem=60 * 1024 * 1024),
    )(qkv, qkv, qkv, ag, o, dy_attn, bias_tile)


def _outproj_loss(x2d, tgt2d, y_pool, y_attn, wout_g, g2):
    t = x2d.shape[0]
    n_tiles = t // TOKEN_TILE

    def body(x_ref, tgt_ref, yp_ref, ya_ref, w_ref, g_ref,
             dx2_ref, dyp_ref, dya_ref, dw_ref, dg_ref, loss_ref, acc_ref):
        i = pl.program_id(0)

        @pl.when(i == 0)
        def _():
            acc_ref[...] = jnp.zeros_like(acc_ref)
            dg_ref[...] = jnp.zeros_like(dg_ref)
            loss_ref[...] = jnp.zeros_like(loss_ref)

        w = w_ref[...].reshape(D_MODEL, D_MODEL)
        y = jnp.concatenate([yp_ref[...], ya_ref[...]], axis=1)
        x2 = x_ref[...] + _nn(y, w)
        r = lax.rsqrt(jnp.mean(x2 * x2, axis=-1, keepdims=True) + EPS)
        xh = x2 * r
        g = g_ref[...]
        diff = xh * g - tgt_ref[...]
        tok = jnp.sum(diff * diff, axis=-1, keepdims=True) * (1.0 / D_MODEL)
        loss_ref[...] += jnp.sum(tok, axis=0, keepdims=True)
        dout = diff * (1.0 / D_MODEL)
        dg_ref[...] += jnp.sum(dout * xh, axis=0, keepdims=True)
        u = dout * g
        dx2 = r * (u - xh * jnp.mean(u * xh, axis=-1, keepdims=True))
        dx2_ref[...] = dx2
        dx2_b = dx2.astype(BF16)
        dy = _nt(dx2_b, w)
        dyp_ref[...] = dy[:, :POOL_WIDTH].astype(BF16)
        dya_ref[...] = dy[:, POOL_WIDTH:].astype(BF16)
        acc_ref[...] += _tn(y, dx2_b)

        @pl.when(i == n_tiles - 1)
        def _():
            dw_ref[...] = acc_ref[...].reshape(N_DEV, OUT_SHARD, D_MODEL).astype(BF16)

    tile = lambda width: pl.BlockSpec((TOKEN_TILE, width), lambda i: (i, 0))
    return pl.pallas_call(
        body,
        name="outproj_loss",
        grid=(n_tiles,),
        out_shape=(
            jax.ShapeDtypeStruct((t, D_MODEL), F32),
            jax.ShapeDtypeStruct((t, POOL_WIDTH), BF16),
            jax.ShapeDtypeStruct((t, ATTN_WIDTH), BF16),
            jax.ShapeDtypeStruct((N_DEV, OUT_SHARD, D_MODEL), BF16),
            jax.ShapeDtypeStruct((1, D_MODEL), F32),
            jax.ShapeDtypeStruct((1, 1), F32),
        ),
        in_specs=[
            tile(D_MODEL), tile(D_MODEL), tile(POOL_WIDTH), tile(ATTN_WIDTH),
            pl.BlockSpec((N_DEV, OUT_SHARD, D_MODEL), lambda i: (0, 0, 0)),
            pl.BlockSpec((1, D_MODEL), lambda i: (0, 0)),
        ],
        out_specs=(
            tile(D_MODEL), tile(POOL_WIDTH), tile(ATTN_WIDTH),
            pl.BlockSpec((N_DEV, OUT_SHARD, D_MODEL), lambda i: (0, 0, 0)),
            pl.BlockSpec((1, D_MODEL), lambda i: (0, 0)),
            pl.BlockSpec((1, 1), lambda i: (0, 0)),
        ),
        scratch_shapes=[pltpu.VMEM((D_MODEL, D_MODEL), F32)],
        compiler_params=_params(("arbitrary",)),
    )(x2d, tgt2d, y_pool, y_attn, wout_g, g2)


def _dproj_specs():
    tile = lambda width: pl.BlockSpec((TOKEN_TILE, width), lambda i: (i, 0))
    return [tile(2 * POOL_WIDTH)] + [tile(ATTN_WIDTH)] * 4


def _inproj_bwd_dx(x2d, dx2, dproj, g1, wg):
    t = x2d.shape[0]

    def body(x_ref, dx2_ref, dp_ref, dq_ref, dk_ref, dv_ref, dag_ref, g_ref, wg_hbm, gx_ref, dg_ref, wfull_ref, sem):
        @pl.when(pl.program_id(0) == 0)
        def _():
            _load_w_in(wg_hbm, wfull_ref, sem)
            dg_ref[...] = jnp.zeros_like(dg_ref)

        dproj_t = jnp.concatenate([dp_ref[...], dq_ref[...], dk_ref[...], dv_ref[...], dag_ref[...]], axis=1)
        dh = _nt(dproj_t, wfull_ref[...])
        xf = x_ref[...]
        r = lax.rsqrt(jnp.mean(xf * xf, axis=-1, keepdims=True) + EPS)
        xh = xf * r
        dg_ref[...] += jnp.sum(dh * xh, axis=0, keepdims=True)
        u = dh * g_ref[...]
        gx_ref[...] = dx2_ref[...] + r * (u - xh * jnp.mean(u * xh, axis=-1, keepdims=True))

    tile = pl.BlockSpec((TOKEN_TILE, D_MODEL), lambda i: (i, 0))
    return pl.pallas_call(
        body,
        name="inproj_bwd_dx",
        grid=(t // TOKEN_TILE,),
        out_shape=(jax.ShapeDtypeStruct((t, D_MODEL), F32), jax.ShapeDtypeStruct((1, D_MODEL), F32)),
        in_specs=[tile, tile] + _dproj_specs() + [
            pl.BlockSpec((1, D_MODEL), lambda i: (0, 0)),
            pl.BlockSpec(memory_space=pl.ANY),
        ],
        out_specs=(tile, pl.BlockSpec((1, D_MODEL), lambda i: (0, 0))),
        scratch_shapes=[pltpu.VMEM((D_MODEL, IN_WIDTH), BF16), pltpu.SemaphoreType.DMA((N_DEV,))],
        compiler_params=_params(("arbitrary",)),
    )(x2d, dx2, *dproj, g1, wg)


def _inproj_bwd_dw(x2d, dproj, g1):
    t = x2d.shape[0]
    n_tiles = t // TOKEN_TILE
    widths = [2 * POOL_WIDTH] + [ATTN_WIDTH] * 4

    def body(x_ref, dp_ref, dq_ref, dk_ref, dv_ref, dag_ref, g_ref, out_ref, acc_ref):
        i = pl.program_id(0)

        @pl.when(i == 0)
        def _():
            acc_ref[...] = jnp.zeros_like(acc_ref)

        xf = x_ref[...]
        r = lax.rsqrt(jnp.mean(xf * xf, axis=-1, keepdims=True) + EPS)
        h = ((xf * r) * g_ref[...]).astype(BF16)
        col = 0
        for ref, width in zip((dp_ref, dq_ref, dk_ref, dv_ref, dag_ref), widths):
            for c0 in range(0, width, POOL_WIDTH):
                acc_ref[:, col:col + POOL_WIDTH] += _tn(h, ref[:, c0:c0 + POOL_WIDTH])
                col += POOL_WIDTH

        @pl.when(i == n_tiles - 1)
        def _():
            for d in range(N_DEV):
                out_ref[d] = acc_ref[:, d * IN_SHARD:(d + 1) * IN_SHARD].astype(BF16)

    return pl.pallas_call(
        body,
        name="inproj_bwd_dw",
        grid=(n_tiles,),
        out_shape=jax.ShapeDtypeStruct((N_DEV, D_MODEL, IN_SHARD), BF16),
        in_specs=[pl.BlockSpec((TOKEN_TILE, D_MODEL), lambda i: (i, 0))] + _dproj_specs() + [
            pl.BlockSpec((1, D_MODEL), lambda i: (0, 0)),
        ],
        out_specs=pl.BlockSpec((N_DEV, D_MODEL, IN_SHARD), lambda i: (0, 0, 0)),
        scratch_shapes=[pltpu.VMEM((D_MODEL, IN_WIDTH), F32)],
        compiler_params=_params(("arbitrary",)),
    )(x2d, *dproj, g1)


def _adamw(w, g, m, v):
    m = ADAM_B1 * m + (1.0 - ADAM_B1) * g
    v = ADAM_B2 * v + (1.0 - ADAM_B2) * (g * g)
    m_hat = m / (1.0 - ADAM_B1 ** ADAM_STEP)
    v_hat = v / (1.0 - ADAM_B2 ** ADAM_STEP)
    delta = -ADAM_LR * (m_hat / (jnp.sqrt(v_hat) + ADAM_EPS) + ADAM_WD * w)
    return delta, m, v


def _reduce_adamw(dwin_g, dwout_g, small_g, w_in, m_in, v_in, w_out, m_out, v_out, small_w, small_m, small_v):
    rows = small_g.shape[0]

    def body(dwin_ref, dwout_ref, sg_ref, win_ref, min_ref, vin_ref, wout_ref, mout_ref, vout_ref,
             sw_ref, sm_ref, sv_ref,
             gin_ref, din_ref, nmin_ref, nvin_ref, gout_ref, dout_ref, nmout_ref, nvout_ref,
             gs_ref, ds_ref, nms_ref, nvs_ref,
             rin_ref, rout_ref, rs_ref, send_sems, recv_sems):
        x, y, c = _mesh_pos()
        me = 4 * x + 2 * y + c

        def peer(r):
            return (x ^ ((r >> 2) & 1), y ^ ((r >> 1) & 1), c ^ (r & 1))

        def peer_idx(r):
            p = peer(r)
            return 4 * p[0] + 2 * p[1] + p[2]

        def copies(r):
            to = peer(r)
            args = dict(device_id=to, device_id_type=MESH_ID)
            return [
                pltpu.make_async_remote_copy(src_ref=dwin_ref.at[peer_idx(r)], dst_ref=rin_ref.at[r],
                                             send_sem=send_sems.at[r], recv_sem=recv_sems.at[r], **args),
                pltpu.make_async_remote_copy(src_ref=dwout_ref.at[peer_idx(r)], dst_ref=rout_ref.at[r],
                                             send_sem=send_sems.at[8 + r], recv_sem=recv_sems.at[8 + r], **args),
                pltpu.make_async_remote_copy(src_ref=sg_ref, dst_ref=rs_ref.at[r],
                                             send_sem=send_sems.at[16 + r], recv_sem=recv_sems.at[16 + r], **args),
            ]

        started = [cp for r in range(1, N_DEV) for cp in copies(r)]
        for cp in started:
            cp.start()
        rin_ref[0] = dwin_ref[me]
        rout_ref[0] = dwout_ref[me]
        rs_ref[0] = sg_ref[...]
        for cp in started:
            cp.wait_recv()
        for cp in started:
            cp.wait_send()

        g_in = rin_ref[0].astype(F32)
        g_out = rout_ref[0].astype(F32)
        for r in range(1, N_DEV):
            g_in = g_in + rin_ref[r].astype(F32)
            g_out = g_out + rout_ref[r].astype(F32)
        g_small = rs_ref[me]
        for s in range(1, N_DEV):
            g_small = g_small + rs_ref[me ^ s]

        for g, (w_ref, m_ref, v_ref), (g_out_ref, d_ref, nm_ref, nv_ref) in (
            (g_in, (win_ref, min_ref, vin_ref), (gin_ref, din_ref, nmin_ref, nvin_ref)),
            (g_out, (wout_ref, mout_ref, vout_ref), (gout_ref, dout_ref, nmout_ref, nvout_ref)),
            (g_small, (sw_ref, sm_ref, sv_ref), (gs_ref, ds_ref, nms_ref, nvs_ref)),
        ):
            delta, m_new, v_new = _adamw(w_ref[...], g, m_ref[...], v_ref[...])
            g_out_ref[...] = g
            d_ref[...] = delta
            nm_ref[...] = m_new
            nv_ref[...] = v_new

    vm = pl.BlockSpec(memory_space=pltpu.VMEM)
    shapes = [jax.ShapeDtypeStruct((D_MODEL, IN_SHARD), F32)] * 4
    shapes += [jax.ShapeDtypeStruct((OUT_SHARD, D_MODEL), F32)] * 4
    shapes += [jax.ShapeDtypeStruct((rows, LANES), F32)] * 4
    return pl.pallas_call(
        body,
        name="reduce_adamw",
        out_shape=tuple(shapes),
        in_specs=[vm] * 12,
        out_specs=tuple([vm] * 12),
        scratch_shapes=[
            pltpu.VMEM((N_DEV, D_MODEL, IN_SHARD), BF16),
            pltpu.VMEM((N_DEV, OUT_SHARD, D_MODEL), BF16),
            pltpu.VMEM((N_DEV, rows, LANES), F32),
            pltpu.SemaphoreType.DMA((3 * N_DEV,)),
            pltpu.SemaphoreType.DMA((3 * N_DEV,)),
        ],
        compiler_params=_params(),
    )(dwin_g, dwout_g, small_g, w_in, m_in, v_in, w_out, m_out, v_out, small_w, small_m, small_v)


_SMALL_ROWS = (8, 8, 8, 16, 512, 8)


def _pack_small(norm_gain, final_gain, pool_scale, rel_bias, pool_w, loss_row):
    rb = jnp.pad(rel_bias.reshape(N_HEADS, -1), ((0, 0), (0, 2 * LANES - rel_bias.shape[-1])))
    parts = [
        norm_gain.reshape(8, LANES), final_gain.reshape(8, LANES),
        jnp.pad(pool_scale.reshape(4, LANES), ((0, 4), (0, 0))),
        rb.reshape(16, LANES), pool_w.reshape(N_GROUPS * GROUP_DIM, GROUP_DIM), loss_row,
    ]
    return jnp.concatenate(parts, axis=0)


def _unpack_small(packed):
    offs = np.concatenate([[0], np.cumsum(_SMALL_ROWS)])
    seg = [packed[offs[k]:offs[k + 1]] for k in range(len(_SMALL_ROWS))]
    norm_gain = seg[0].reshape(1, D_MODEL)
    final_gain = seg[1].reshape(D_MODEL)
    pool_scale = seg[2][:4].reshape(1, POOL_WIDTH)
    rel_bias = seg[3].reshape(N_HEADS, 2 * LANES)[:, :N_REL].reshape(1, N_HEADS, N_REL)
    pool_w = seg[4].reshape(1, N_GROUPS, GROUP_DIM, GROUP_DIM)
    return norm_gain, pool_w, pool_scale, rel_bias, final_gain, seg[5]


def kernel(x, norm_gain, w_in, pool_w, pool_scale, rel_bias, w_out, final_norm_gain, loss_target, m_norm_gain, m_w_in, m_pool_w, m_pool_scale, m_rel_bias, m_w_out, m_final_norm_gain, v_norm_gain, v_w_in, v_pool_w, v_pool_scale, v_rel_bias, v_w_out, v_final_norm_gain):
    t = x.shape[1]
    assert x.shape[0] == 1 and t % TOKEN_TILE == 0 and t // Q_BLOCK >= 4
    x2d = x[0]
    tgt2d = loss_target[0]
    g2 = final_norm_gain.reshape(1, D_MODEL)

    wg_in, wg_out = _gather_weights(w_in[0], w_out[0])

    rb = rel_bias[0]
    rel_line = jnp.concatenate([
        jnp.broadcast_to(rb[:, :1], (N_HEADS, _REL_FIRST)), rb,
        jnp.broadcast_to(rb[:, N_REL - 1:], (N_HEADS, TOEPLITZ - _REL_FIRST - N_REL)),
    ], axis=1).reshape(N_HEADS, 1, TOEPLITZ)
    bias_tile = _bias_tile(rel_line)

    pvg, qkv, ag = _norm_inproj(x2d, norm_gain, wg_in)
    y_pool = _pool_fwd(pvg, pool_w[0], pool_scale)
    o, y_attn = _attn_fwd(qkv, ag, bias_tile)
    dx2, dy_pool, dy_attn, dwout_g, d_g2, loss_sum = _outproj_loss(x2d, tgt2d, y_pool, y_attn, wg_out, g2)
    d_pool, d_pw, d_ps = _pool_bwd(pvg, dy_pool, pool_w[0], pool_scale)
    dq, dk, dv, dag, ds_sum = _attn_bwd(qkv, ag, o, dy_attn, bias_tile)
    d_rb = _bias_grad(ds_sum).reshape(N_HEADS, 2 * LANES)[:, :N_REL]
    dproj = (d_pool, dq, dk, dv, dag)
    grad_x, d_g1 = _inproj_bwd_dx(x2d, dx2, dproj, norm_gain, wg_in)
    dwin_g = _inproj_bwd_dw(x2d, dproj, norm_gain)

    loss_row = jnp.broadcast_to(0.5 * loss_sum, (8, LANES))
    small_g = _pack_small(d_g1, d_g2, d_ps, d_rb, d_pw, loss_row)
    zeros = jnp.zeros((8, LANES), F32)
    small_w = _pack_small(norm_gain, final_norm_gain, pool_scale, rel_bias, pool_w, zeros)
    small_m = _pack_small(m_norm_gain, m_final_norm_gain, m_pool_scale, m_rel_bias, m_pool_w, zeros)
    small_v = _pack_small(v_norm_gain, v_final_norm_gain, v_pool_scale, v_rel_bias, v_pool_w, zeros)

    res = _reduce_adamw(dwin_g, dwout_g, small_g, w_in[0], m_w_in[0], v_w_in[0],
                        w_out[0], m_w_out[0], v_w_out[0], small_w, small_m, small_v)
    big_in = [a[None] for a in res[0:4]]
    big_out = [a[None] for a in res[4:8]]
    small = [_unpack_small(a) for a in res[8:12]]
    loss = small[0][5][0, 0]

    def leaves(k):
        g1_, pw_, ps_, rb_, g2_, _ = small[k]
        return [g1_, big_in[k], pw_, ps_, rb_, big_out[k], g2_]

    return (loss, grad_x[None], *leaves(0), *leaves(1), *leaves(2), *leaves(3))
```

```python
import functools
import math

import jax
import jax.numpy as jnp
import numpy as np
from jax import lax
from jax.experimental import pallas as pl
from jax.experimental.pallas import tpu as pltpu

F32 = jnp.float32
BF16 = jnp.bfloat16
MESH_ID = pl.DeviceIdType.MESH

D_MODEL = 1024
POOL_WIDTH = 512
ATTN_WIDTH = 512
POOL_WINDOWS = (2, 4, 8, 16)
N_GROUPS = 4
GROUP_DIM = 128
HEAD_DIM = 64
N_HEADS = 8
CHUNK = 64
LEFT_CHUNKS = 8
MAX_REL = 64
N_REL = 2 * MAX_REL + 1
IN_WIDTH = 2 * POOL_WIDTH + 4 * ATTN_WIDTH
EPS = 1e-6
MASK_VALUE = -1e30
ATTN_SCALE = 1.0 / math.sqrt(HEAD_DIM)
ADAM_LR = 0.001
ADAM_B1 = 0.9
ADAM_B2 = 0.999
ADAM_EPS = 1e-08
ADAM_WD = 0.01
ADAM_STEP = 10

N_DEV = 8
IN_SHARD = IN_WIDTH // N_DEV
OUT_SHARD = D_MODEL // N_DEV

LANES = 128
TOKEN_TILE = 512
HALO = 16
Q_BLOCK = 256
KV_BLOCKS = 3
KV_WINDOW = KV_BLOCKS * Q_BLOCK
PAIR = 2 * HEAD_DIM
N_PAIRS = N_HEADS // 2
TOEPLITZ = 1024
VMEM_LIMIT = 56 * 1024 * 1024


def _params(sem=None, vmem=VMEM_LIMIT):
    return pltpu.CompilerParams(dimension_semantics=sem, vmem_limit_bytes=vmem)


def _sigmoid(x):
    return 1.0 / (1.0 + jnp.exp(-x))


def _nt(a, b):
    return lax.dot_general(a, b, (((1,), (1,)), ((), ())), preferred_element_type=F32)


def _tn(a, b):
    return lax.dot_general(a, b, (((0,), (0,)), ((), ())), preferred_element_type=F32)


def _nn(a, b):
    return jnp.dot(a, b, preferred_element_type=F32)


def _mesh_pos():
    return lax.axis_index("x"), lax.axis_index("y"), lax.axis_index("c")


def _dev_index(p):
    return 4 * p[0] + 2 * p[1] + p[2]


def _peer(r):
    x, y, c = _mesh_pos()
    return (x ^ ((r >> 2) & 1), y ^ ((r >> 1) & 1), c ^ (r & 1))


def _exchange_copies(src_hbm, land_hbm, send_sems, recv_sems):
    return [
        pltpu.make_async_remote_copy(
            src_ref=src_hbm.at[_dev_index(_peer(r))], dst_ref=land_hbm.at[r - 1],
            send_sem=send_sems.at[r - 1], recv_sem=recv_sems.at[r - 1],
            device_id=_peer(r), device_id_type=MESH_ID)
        for r in range(1, N_DEV)
    ]


def _gather_weights(w_in_shard, rel_line):
    def body(win_ref, line_ref, gin_ref, bias_ref, sin_ref, send_sems, recv_sems):
        x, y, c = _mesh_pos()
        me, sibling = (x, y, c), (x, y, 1 - c)
        chips = [(1 - x, y), (x, 1 - y), (1 - x, 1 - y)]

        sin_ref[...] = win_ref[...].astype(BF16)
        gin_ref[_dev_index(me)] = sin_ref[...]

        def copy(k, block, to, from_shard=False):
            return pltpu.make_async_remote_copy(
                src_ref=sin_ref if from_shard else gin_ref.at[_dev_index(block)],
                dst_ref=gin_ref.at[_dev_index(block)],
                send_sem=send_sems.at[k],
                recv_sem=recv_sems.at[k],
                device_id=to,
                device_id_type=MESH_ID,
            )

        first = [copy(0, me, sibling, True)]
        first += [copy(1 + j, me, (*chip, c), True) for j, chip in enumerate(chips)]
        for cp in first:
            cp.start()
        passed = [copy(4 + j, (*chip, c), sibling) for j, chip in enumerate(chips)]

        def bias_heads(lo, hi):
            for h in range(lo, hi):
                bias_ref[h] = _toeplitz_bias(line_ref[h])

        bias_heads(0, N_HEADS - 3)
        for j, chip in enumerate(chips):
            copy(1 + j, (*chip, c), me).wait_recv()
            passed[j].start()
            bias_heads(N_HEADS - 3 + j, N_HEADS - 2 + j)
        copy(0, sibling, me).wait_recv()
        for j, chip in enumerate(chips):
            copy(4 + j, (*chip, 1 - c), me).wait_recv()
        for cp in first + passed:
            cp.wait_send()

    vm = pl.BlockSpec(memory_space=pltpu.VMEM)
    return pl.pallas_call(
        body,
        name="gather_weights",
        out_shape=(
            jax.ShapeDtypeStruct((N_DEV, D_MODEL, IN_SHARD), BF16),
            jax.ShapeDtypeStruct((N_HEADS, Q_BLOCK, KV_WINDOW), F32),
        ),
        in_specs=[vm, vm],
        out_specs=(vm, vm),
        scratch_shapes=[
            pltpu.VMEM((D_MODEL, IN_SHARD), BF16),
            pltpu.SemaphoreType.DMA((7,)),
            pltpu.SemaphoreType.DMA((7,)),
        ],
        compiler_params=_params(),
    )(w_in_shard, rel_line)


def _load_w_in(wg_hbm, wfull_ref, sem):
    copies = [
        pltpu.make_async_copy(wg_hbm.at[d], wfull_ref.at[:, d * IN_SHARD:(d + 1) * IN_SHARD], sem.at[d])
        for d in range(N_DEV)
    ]
    for cp in copies:
        cp.start()
    for cp in copies:
        cp.wait()


def _norm_inproj(x2d, g1, wg):
    t = x2d.shape[0]
    n_chunks = IN_WIDTH // POOL_WIDTH

    def body(x_ref, g_ref, wg_hbm, pvg_ref, qkv_ref, ag_ref, wfull_ref, sem):
        @pl.when(pl.program_id(0) == 0)
        def _():
            _load_w_in(wg_hbm, wfull_ref, sem)

        xf = x_ref[...]
        r = lax.rsqrt(jnp.mean(xf * xf, axis=-1, keepdims=True) + EPS)
        h = ((xf * r) * g_ref[...]).astype(BF16)
        for ci in range(n_chunks):
            res = _nn(h, wfull_ref[:, ci * POOL_WIDTH:(ci + 1) * POOL_WIDTH])
            if ci < 2:
                pvg_ref[:, ci * POOL_WIDTH:(ci + 1) * POOL_WIDTH] = res
            elif ci < 5:
                qkv_ref[:, (ci - 2) * POOL_WIDTH:(ci - 1) * POOL_WIDTH] = res.astype(BF16)
            else:
                ag_ref[...] = res

    return pl.pallas_call(
        body,
        name="norm_inproj",
        grid=(t // TOKEN_TILE,),
        out_shape=(
            jax.ShapeDtypeStruct((t, 2 * POOL_WIDTH), F32),
            jax.ShapeDtypeStruct((t, 3 * ATTN_WIDTH), BF16),
            jax.ShapeDtypeStruct((t, ATTN_WIDTH), F32),
        ),
        in_specs=[
            pl.BlockSpec((TOKEN_TILE, D_MODEL), lambda i: (i, 0)),
            pl.BlockSpec((1, D_MODEL), lambda i: (0, 0)),
            pl.BlockSpec(memory_space=pl.ANY),
        ],
        out_specs=(
            pl.BlockSpec((TOKEN_TILE, 2 * POOL_WIDTH), lambda i: (i, 0)),
            pl.BlockSpec((TOKEN_TILE, 3 * ATTN_WIDTH), lambda i: (i, 0)),
            pl.BlockSpec((TOKEN_TILE, ATTN_WIDTH), lambda i: (i, 0)),
        ),
        scratch_shapes=[pltpu.VMEM((D_MODEL, IN_WIDTH), BF16), pltpu.SemaphoreType.DMA((N_DEV,))],
        compiler_params=_params(("arbitrary",)),
    )(x2d, g1, wg)


def _inv_count(first_row, rows, window):
    tpos = first_row + lax.broadcasted_iota(jnp.int32, (rows, 1), 0)
    return 1.0 / jnp.minimum(tpos + 1, window).astype(F32)


def _causal_window_sum(ext, window):
    s, k = ext, 1
    while k < window:
        s = s + pltpu.roll(s, k, 0)
        k *= 2
    return s


def _pool_diffs(pv, halo, first_row, gi):
    w = POOL_WINDOWS[gi]
    sl = slice(gi * GROUP_DIM, (gi + 1) * GROUP_DIM)
    ext = jnp.concatenate([halo[:, sl], pv[:, sl]], axis=0)
    s = _causal_window_sum(ext, w)[HALO:]
    return s * _inv_count(first_row, pv.shape[0], w) - pv[:, sl]


def _pool_fwd(pvg, pool_w, pool_scale, w_out_shard):
    t = pvg.shape[0]
    n_tiles = t // TOKEN_TILE
    halo_per_tile = TOKEN_TILE // HALO

    def body(cur_ref, halo_ref, pw_ref, ps_ref, wout_ref, y_ref, gout_hbm, sout_ref, send_sems, recv_sems, local_sem):
        i = pl.program_id(0)
        x, y, c = _mesh_pos()
        me, sibling = (x, y, c), (x, y, 1 - c)
        chips = [(1 - x, y), (x, 1 - y), (1 - x, 1 - y)]

        def copy(k, block, to, from_shard=False):
            return pltpu.make_async_remote_copy(
                src_ref=sout_ref if from_shard else gout_hbm.at[_dev_index(block)],
                dst_ref=gout_hbm.at[_dev_index(block)],
                send_sem=send_sems.at[k],
                recv_sem=recv_sems.at[k],
                device_id=to,
                device_id_type=MESH_ID,
            )

        mine = pltpu.make_async_copy(sout_ref, gout_hbm.at[_dev_index(me)], local_sem)
        first = [copy(0, me, sibling, True)] + [copy(1 + j, me, (*chip, c), True) for j, chip in enumerate(chips)]
        passed = [copy(4 + j, (*chip, c), sibling) for j, chip in enumerate(chips)]

        @pl.when(i == 0)
        def _():
            sout_ref[...] = wout_ref[...].astype(BF16)
            mine.start()
            for cp in first:
                cp.start()

        @pl.when(i == n_tiles // 2)
        def _():
            for j, chip in enumerate(chips):
                copy(1 + j, (*chip, c), me).wait_recv()
                passed[j].start()

        pv = cur_ref[:, :POOL_WIDTH]
        pg = cur_ref[:, POOL_WIDTH:]
        halo = jnp.where(i > 0, halo_ref[...], 0.0)
        for gi in range(N_GROUPS):
            sl = slice(gi * GROUP_DIM, (gi + 1) * GROUP_DIM)
            d = _pool_diffs(pv, halo, i * TOKEN_TILE, gi)
            z = _nn(d.astype(BF16), pw_ref[gi].astype(BF16))
            g = pg[:, sl]
            y_ref[:, sl] = ((z * ps_ref[:, sl]) * (g * _sigmoid(g))).astype(BF16)

        @pl.when(i == n_tiles - 1)
        def _():
            copy(0, sibling, me).wait_recv()
            for j, chip in enumerate(chips):
                copy(4 + j, (*chip, 1 - c), me).wait_recv()
            for cp in first + passed:
                cp.wait_send()
            mine.wait()

    return pl.pallas_call(
        body,
        name="pool_fwd",
        grid=(n_tiles,),
        out_shape=(
            jax.ShapeDtypeStruct((t, POOL_WIDTH), BF16),
            jax.ShapeDtypeStruct((N_DEV, OUT_SHARD, D_MODEL), BF16),
        ),
        in_specs=[
            pl.BlockSpec((TOKEN_TILE, 2 * POOL_WIDTH), lambda i: (i, 0)),
            pl.BlockSpec((HALO, POOL_WIDTH), lambda i: (jnp.maximum(i * halo_per_tile - 1, 0), 0)),
            pl.BlockSpec((N_GROUPS, GROUP_DIM, GROUP_DIM), lambda i: (0, 0, 0)),
            pl.BlockSpec((1, POOL_WIDTH), lambda i: (0, 0)),
            pl.BlockSpec((OUT_SHARD, D_MODEL), lambda i: (0, 0)),
        ],
        out_specs=(
            pl.BlockSpec((TOKEN_TILE, POOL_WIDTH), lambda i: (i, 0)),
            pl.BlockSpec(memory_space=pl.ANY),
        ),
        scratch_shapes=[
            pltpu.VMEM((OUT_SHARD, D_MODEL), BF16),
            pltpu.SemaphoreType.DMA((7,)),
            pltpu.SemaphoreType.DMA((7,)),
            pltpu.SemaphoreType.DMA,
        ],
        compiler_params=_params(("arbitrary",)),
    )(pvg, pvg, pool_w, pool_scale, w_out_shard)


def _pool_bwd(pvg, dy_pool, pool_w, pool_scale, dwout_g):
    t = pvg.shape[0]
    n_tiles = t // TOKEN_TILE
    halo_per_tile = TOKEN_TILE // HALO
    last_halo = t // HALO - 1

    def body(cur_ref, prev_ref, pgn_ref, dy_ref, dyn_ref, pw_ref, ps_ref, dwout_hbm,
             dp_ref, dpw_ref, dps_ref, land_hbm, send_sems, recv_sems):
        i = pl.program_id(0)
        exchange = _exchange_copies(dwout_hbm, land_hbm, send_sems, recv_sems)

        @pl.when(i == 0)
        def _():
            for cp in exchange:
                cp.start()
            dpw_ref[...] = jnp.zeros_like(dpw_ref)
            dps_ref[...] = jnp.zeros_like(dps_ref)

        pv = cur_ref[:, :POOL_WIDTH]
        pg = cur_ref[:, POOL_WIDTH:]
        prev = jnp.where(i > 0, prev_ref[...], 0.0)
        has_next = i < n_tiles - 1
        rows = TOKEN_TILE + HALO
        for gi in range(N_GROUPS):
            w = POOL_WINDOWS[gi]
            sl = slice(gi * GROUP_DIM, (gi + 1) * GROUP_DIM)
            pw = pw_ref[gi].astype(BF16)
            ps = ps_ref[:, sl]
            d = _pool_diffs(pv, prev, i * TOKEN_TILE, gi).astype(BF16)
            z = _nn(d, pw)
            g_ext = jnp.concatenate([pg[:, sl], pgn_ref[:, sl]], axis=0)
            dy_ext = jnp.concatenate([dy_ref[:, sl], dyn_ref[:, sl]], axis=0).astype(F32)
            sig = _sigmoid(g_ext)
            gate = g_ext * sig
            dz_ext = ((dy_ext * gate) * ps).astype(BF16)
            dd_ext = _nt(dz_ext, pw)
            e = dd_ext * _inv_count(i * TOKEN_TILE, rows, w)
            row = lax.broadcasted_iota(jnp.int32, (rows, 1), 0)
            e = jnp.where(jnp.logical_or(row < TOKEN_TILE, has_next), e, 0.0)
            s, k = e, 1
            while k < w:
                s = s + pltpu.roll(s, rows - k, 0)
                k *= 2
            dp_ref[:, sl] = (s[:TOKEN_TILE] - dd_ext[:TOKEN_TILE]).astype(BF16)
            dy = dy_ext[:TOKEN_TILE]
            g = g_ext[:TOKEN_TILE]
            sg = sig[:TOKEN_TILE]
            dgate = sg * (1.0 + g * (1.0 - sg))
            dp_ref[:, POOL_WIDTH + gi * GROUP_DIM:POOL_WIDTH + (gi + 1) * GROUP_DIM] = (
                (dy * (z * ps)) * dgate).astype(BF16)
            dps_ref[:, sl] += jnp.sum((dy * gate[:TOKEN_TILE]) * z, axis=0, keepdims=True)
            dpw_ref[gi] += _tn(d, dz_ext[:TOKEN_TILE])

        @pl.when(i == n_tiles - 1)
        def _():
            for cp in exchange:
                cp.wait_recv()
            for cp in exchange:
                cp.wait_send()

    return pl.pallas_call(
        body,
        name="pool_bwd",
        grid=(n_tiles,),
        out_shape=(
            jax.ShapeDtypeStruct((t, 2 * POOL_WIDTH), BF16),
            jax.ShapeDtypeStruct((N_GROUPS, GROUP_DIM, GROUP_DIM), F32),
            jax.ShapeDtypeStruct((1, POOL_WIDTH), F32),
            jax.ShapeDtypeStruct((N_DEV - 1, OUT_SHARD, D_MODEL), BF16),
        ),
        in_specs=[
            pl.BlockSpec((TOKEN_TILE, 2 * POOL_WIDTH), lambda i: (i, 0)),
            pl.BlockSpec((HALO, POOL_WIDTH), lambda i: (jnp.maximum(i * halo_per_tile - 1, 0), 0)),
            pl.BlockSpec((HALO, POOL_WIDTH), lambda i: (jnp.minimum((i + 1) * halo_per_tile, last_halo), 1)),
            pl.BlockSpec((TOKEN_TILE, POOL_WIDTH), lambda i: (i, 0)),
            pl.BlockSpec((HALO, POOL_WIDTH), lambda i: (jnp.minimum((i + 1) * halo_per_tile, last_halo), 0)),
            pl.BlockSpec((N_GROUPS, GROUP_DIM, GROUP_DIM), lambda i: (0, 0, 0)),
            pl.BlockSpec((1, POOL_WIDTH), lambda i: (0, 0)),
            pl.BlockSpec(memory_space=pl.ANY),
        ],
        out_specs=(
            pl.BlockSpec((TOKEN_TILE, 2 * POOL_WIDTH), lambda i: (i, 0)),
            pl.BlockSpec((N_GROUPS, GROUP_DIM, GROUP_DIM), lambda i: (0, 0, 0)),
            pl.BlockSpec((1, POOL_WIDTH), lambda i: (0, 0)),
            pl.BlockSpec(memory_space=pl.ANY),
        ),
        scratch_shapes=[pltpu.SemaphoreType.DMA((N_DEV - 1,)), pltpu.SemaphoreType.DMA((N_DEV - 1,))],
        compiler_params=_params(("arbitrary",)),
    )(pvg, pvg, pvg, dy_pool, dy_pool, pool_w, pool_scale, dwout_g)


_REL_FIRST = KV_WINDOW - 1 - MAX_REL
_DIAG_FIRST = _REL_FIRST - (Q_BLOCK - 1)


def _skew_rows(a, right):
    row = lax.broadcasted_iota(jnp.int32, a.shape, 0)
    for b in range(8):
        shift = (1 << b) if right else TOEPLITZ - (1 << b)
        a = jnp.where((row >> b) & 1 == 1, pltpu.roll(a, shift, 1), a)
    return a


def _toeplitz_bias(line):
    a = jnp.broadcast_to(line, (Q_BLOCK, TOEPLITZ))
    a = _skew_rows(a, True)
    a = pltpu.roll(a, TOEPLITZ - (Q_BLOCK - 1), 1)
    a = a[:, :KV_WINDOW]
    qc = lax.broadcasted_iota(jnp.int32, a.shape, 0) // CHUNK
    kc = lax.broadcasted_iota(jnp.int32, a.shape, 1) // CHUNK
    visible = jnp.logical_and(kc >= qc, kc <= qc + LEFT_CHUNKS)
    return jnp.where(visible, a, MASK_VALUE)


def _bias_grad(ds_sum):
    def body(ds_ref, out_ref):
        a = jnp.concatenate([ds_ref[0], jnp.zeros((Q_BLOCK, TOEPLITZ - KV_WINDOW), F32)], axis=1)
        a = _skew_rows(a, False)
        diag = jnp.sum(a, axis=0, keepdims=True)
        c = lax.broadcasted_iota(jnp.int32, diag.shape, 1)
        far = jnp.logical_or(c <= _DIAG_FIRST, c > KV_WINDOW)
        near = jnp.logical_and(c >= _DIAG_FIRST + 2 * MAX_REL, c <= KV_WINDOW)
        low = jnp.sum(jnp.where(far, diag, 0.0), axis=1, keepdims=True)
        high = jnp.sum(jnp.where(near, diag, 0.0), axis=1, keepdims=True)
        line = pltpu.roll(diag, TOEPLITZ - _DIAG_FIRST, 1)[:, :2 * LANES]
        r = lax.broadcasted_iota(jnp.int32, line.shape, 1)
        line = jnp.where(r == 0, low, jnp.where(r == 2 * MAX_REL, high, jnp.where(r < 2 * MAX_REL, line, 0.0)))
        out_ref[0] = line

    return pl.pallas_call(
        body,
        name="bias_grad",
        grid=(N_HEADS,),
        out_shape=jax.ShapeDtypeStruct((N_HEADS, 1, 2 * LANES), F32),
        in_specs=[pl.BlockSpec((1, Q_BLOCK, KV_WINDOW), lambda h: (h, 0, 0))],
        out_specs=pl.BlockSpec((1, 1, 2 * LANES), lambda h: (h, 0, 0)),
        compiler_params=_params(("arbitrary",)),
    )(ds_sum)


def _head_lanes(hh):
    lane = lax.broadcasted_iota(jnp.int32, (1, PAIR), 1)
    return (lane < HEAD_DIM) if hh == 0 else (lane >= HEAD_DIM)


def _block_rows(first_block, n_blocks):
    if isinstance(first_block, int):
        return pl.ds(first_block * Q_BLOCK, n_blocks * Q_BLOCK)
    return pl.ds(pl.multiple_of(first_block * Q_BLOCK, Q_BLOCK), n_blocks * Q_BLOCK)


def _attn_fwd(qkv, ag, bias_tile):
    t = qkv.shape[0]
    nb = t // Q_BLOCK

    def body(q_ref, k_ref, v_ref, ag_ref, bias_ref, o_ref, y_ref, s_scr, p_scr, linv_scr):
        def scores(i, slot, nwin):
            q = q_ref[_block_rows(i, 1), :]
            kw = k_ref[_block_rows(i + 1 - nwin, nwin), :]
            for hh in range(2):
                q_h = jnp.where(_head_lanes(hh), q, jnp.zeros_like(q)) * ATTN_SCALE
                s_scr[slot, hh, :, :nwin * Q_BLOCK] = _nt(q_h, kw)

        def softmax(slot, nwin):
            for hh in range(2):
                s = s_scr[slot, hh, :, :nwin * Q_BLOCK] + bias_ref[hh, :, (KV_BLOCKS - nwin) * Q_BLOCK:]
                e = jnp.exp(s - jnp.max(s, axis=-1, keepdims=True))
                linv_scr[slot, hh] = 1.0 / jnp.sum(e, axis=-1, keepdims=True)
                p_scr[slot, hh, :, :nwin * Q_BLOCK] = e.astype(BF16)

        def output(i, slot, nwin):
            vw = v_ref[_block_rows(i + 1 - nwin, nwin), :]
            outs = [_nn(p_scr[slot, hh, :, :nwin * Q_BLOCK], vw) * linv_scr[slot, hh] for hh in range(2)]
            o = jnp.where(_head_lanes(0), outs[0], outs[1])
            g = ag_ref[_block_rows(i, 1), :]
            o_ref[_block_rows(i, 1), :] = o.astype(BF16)
            y_ref[_block_rows(i, 1), :] = (o * (g * _sigmoid(g))).astype(BF16)

        scores(0, 0, 1)
        scores(1, 1, 2)
        softmax(0, 1)
        scores(2, 0, 3)
        softmax(1, 2)
        output(0, 0, 1)
        scores(3, 1, 3)
        softmax(0, 3)
        output(1, 1, 2)

        def two_steps(k, carry):
            i = 3 + 2 * k
            scores(i + 1, 0, KV_BLOCKS)
            softmax(1, KV_BLOCKS)
            output(i - 1, 0, KV_BLOCKS)
            scores(i + 2, 1, KV_BLOCKS)
            softmax(0, KV_BLOCKS)
            output(i, 1, KV_BLOCKS)
            return carry

        lax.fori_loop(0, (nb - 4) // 2, two_steps, 0)
        last = (nb - 1) % 2
        softmax(last, KV_BLOCKS)
        output(nb - 2, 1 - last, KV_BLOCKS)
        output(nb - 1, last, KV_BLOCKS)

    col = lambda c0: pl.BlockSpec((t, PAIR), lambda j: (0, c0 + j))
    return pl.pallas_call(
        body,
        name="attn_fwd",
        grid=(N_PAIRS,),
        out_shape=(jax.ShapeDtypeStruct((t, ATTN_WIDTH), BF16), jax.ShapeDtypeStruct((t, ATTN_WIDTH), BF16)),
        in_specs=[col(0), col(N_PAIRS), col(2 * N_PAIRS), col(0),
                  pl.BlockSpec((2, Q_BLOCK, KV_WINDOW), lambda j: (j, 0, 0))],
        out_specs=(col(0), col(0)),
        scratch_shapes=[
            pltpu.VMEM((2, 2, Q_BLOCK, KV_WINDOW), F32),
            pltpu.VMEM((2, 2, Q_BLOCK, KV_WINDOW), BF16),
            pltpu.VMEM((2, 2, Q_BLOCK, 1), F32),
        ],
        compiler_params=_params(("arbitrary",)),
    )(qkv, qkv, qkv, ag, bias_tile)


def _attn_bwd(qkv, ag, o, dy_attn, bias_tile):
    t = qkv.shape[0]
    nb = t // Q_BLOCK

    def body(q_ref, k_ref, v_ref, ag_ref, o_ref, dy_ref, bias_ref, dq_ref, dk_ref, dv_ref, dag_ref, ds_ref,
             do_scr, s_scr, dp_scr, p_scr, dsb_scr, dq_scr, dk_acc, dv_acc):
        def gates(i, carry):
            rows = _block_rows(i, 1)
            g = ag_ref[rows, :]
            sig = _sigmoid(g)
            dy = dy_ref[rows, :].astype(F32)
            do_scr[rows, :] = (dy * (g * sig)).astype(BF16)
            dag_ref[rows, :] = ((dy * o_ref[rows, :].astype(F32)) * (sig * (1.0 + g * (1.0 - sig)))).astype(BF16)
            return carry

        lax.fori_loop(0, nb, gates, 0)
        ds_ref[...] = jnp.zeros_like(ds_ref)

        def nwin_of(i):
            return min(i + 1, KV_BLOCKS) if isinstance(i, int) else KV_BLOCKS

        def operands(i, hh):
            lanes = _head_lanes(hh)
            q = q_ref[_block_rows(i, 1), :]
            do = do_scr[_block_rows(i, 1), :]
            return (jnp.where(lanes, q, jnp.zeros_like(q)) * ATTN_SCALE, jnp.where(lanes, do, jnp.zeros_like(do)))

        def products(i, hh):
            nwin = nwin_of(i)
            win = _block_rows(i + 1 - nwin, nwin)
            q_h, do_h = operands(i, hh)
            s_scr[hh, :, :nwin * Q_BLOCK] = _nt(q_h, k_ref[win, :])
            dp_scr[hh, :, :nwin * Q_BLOCK] = _nt(do_h, v_ref[win, :])

        def grads(i, hh):
            nwin = nwin_of(i)
            w, off = nwin * Q_BLOCK, (KV_BLOCKS - nwin) * Q_BLOCK
            s = s_scr[hh, :, :w] + bias_ref[hh, :, off:]
            e = jnp.exp(s - jnp.max(s, axis=-1, keepdims=True))
            p = e * (1.0 / jnp.sum(e, axis=-1, keepdims=True))
            dp = dp_scr[hh, :, :w]
            ds = p * (dp - jnp.sum(p * dp, axis=-1, keepdims=True))
            ds_ref[hh, :, off:] += ds
            p_scr[hh, :, :w] = p.astype(BF16)
            dsb_scr[hh, :, :w] = ds.astype(BF16)

        def ring(block):
            return block % KV_BLOCKS if isinstance(block, int) else lax.rem(block, KV_BLOCKS)

        def accumulate(i, hh):
            nwin = nwin_of(i)
            w = nwin * Q_BLOCK
            win = _block_rows(i + 1 - nwin, nwin)
            q_h, do_h = operands(i, hh)
            ds_b = dsb_scr[hh, :, :w]
            dq_h = _nn(ds_b, k_ref[win, :]) * ATTN_SCALE
            dkw = _tn(ds_b, q_h)
            dvw = _tn(p_scr[hh, :, :w], do_h)
            for b in range(nwin):
                slot = ring(i + 1 - nwin + b)
                part = slice(b * Q_BLOCK, (b + 1) * Q_BLOCK)
                if hh == 0 and b == nwin - 1:
                    dk_acc[slot] = dkw[part]
                    dv_acc[slot] = dvw[part]
                else:
                    dk_acc[slot] += dkw[part]
                    dv_acc[slot] += dvw[part]
            if hh == 0:
                dq_scr[...] = dq_h
            else:
                dq_ref[_block_rows(i, 1), :] = jnp.where(_head_lanes(0), dq_scr[...], dq_h).astype(BF16)
                if not (isinstance(i, int) and i < KV_BLOCKS - 1):
                    flush(i - (KV_BLOCKS - 1))

        def flush(block):
            dk_ref[_block_rows(block, 1), :] = dk_acc[ring(block)].astype(BF16)
            dv_ref[_block_rows(block, 1), :] = dv_acc[ring(block)].astype(BF16)

        def tile(n):
            return n // 2, n % 2

        def step(n):
            if n + 1 < 2 * nb:
                products(*tile(n + 1))
            grads(*tile(n))
            if n >= 1:
                accumulate(*tile(n - 1))

        products(0, 0)
        for n in range(2 * KV_BLOCKS):
            step(n)

        def two_steps(i, carry):
            products(i, 1)
            grads(i, 0)
            accumulate(i - 1, 1)
            products(i + 1, 0)
            grads(i, 1)
            accumulate(i, 0)
            return carry

        lax.fori_loop(KV_BLOCKS, nb - 1, two_steps, 0)
        step(2 * nb - 2)
        step(2 * nb - 1)
        accumulate(nb - 1, 1)
        flush(nb - 2)
        flush(nb - 1)

    col = lambda c0: pl.BlockSpec((t, PAIR), lambda j: (0, c0 + j))
    once = pl.BlockSpec((t, PAIR), lambda j: (0, j), pipeline_mode=pl.Buffered(1))
    tile_spec = pl.BlockSpec((2, Q_BLOCK, KV_WINDOW), lambda j: (j, 0, 0))
    out = jax.ShapeDtypeStruct((t, ATTN_WIDTH), BF16)
    return pl.pallas_call(
        body,
        name="attn_bwd",
        grid=(N_PAIRS,),
        out_shape=(out, out, out, out, jax.ShapeDtypeStruct((N_HEADS, Q_BLOCK, KV_WINDOW), F32)),
        in_specs=[col(0), col(N_PAIRS), col(2 * N_PAIRS), once, col(0), col(0), tile_spec],
        out_specs=(col(0), col(0), col(0), col(0), tile_spec),
        scratch_shapes=[
            pltpu.VMEM((t, PAIR), BF16),
            pltpu.VMEM((2, Q_BLOCK, KV_WINDOW), F32),
            pltpu.VMEM((2, Q_BLOCK, KV_WINDOW), F32),
            pltpu.VMEM((2, Q_BLOCK, KV_WINDOW), BF16),
            pltpu.VMEM((2, Q_BLOCK, KV_WINDOW), BF16),
            pltpu.VMEM((Q_BLOCK, PAIR), F32),
            pltpu.VMEM((KV_BLOCKS, Q_BLOCK, PAIR), F32),
            pltpu.VMEM((KV_BLOCKS, Q_BLOCK, PAIR), F32),
        ],
        compiler_params=_params(("arbitrary",), vmem=60 * 1024 * 1024),
    )(qkv, qkv, qkv, ag, o, dy_attn, bias_tile)


def _outproj_loss(x2d, tgt2d, y_pool, y_attn, wout_g, g2):
    t = x2d.shape[0]
    n_tiles = t // TOKEN_TILE

    def body(x_ref, tgt_ref, yp_ref, ya_ref, w_ref, g_ref,
             dx2_ref, dyp_ref, dya_ref, dw_ref, dg_ref, loss_ref, acc_ref):
        i = pl.program_id(0)

        @pl.when(i == 0)
        def _():
            acc_ref[...] = jnp.zeros_like(acc_ref)
            dg_ref[...] = jnp.zeros_like(dg_ref)
            loss_ref[...] = jnp.zeros_like(loss_ref)

        w = w_ref[...].reshape(D_MODEL, D_MODEL)
        y = jnp.concatenate([yp_ref[...], ya_ref[...]], axis=1)
        x2 = x_ref[...] + _nn(y, w)
        r = lax.rsqrt(jnp.mean(x2 * x2, axis=-1, keepdims=True) + EPS)
        xh = x2 * r
        g = g_ref[...]
        diff = xh * g - tgt_ref[...]
        tok = jnp.sum(diff * diff, axis=-1, keepdims=True) * (1.0 / D_MODEL)
        loss_ref[...] += jnp.sum(tok, axis=0, keepdims=True)
        dout = diff * (1.0 / D_MODEL)
        dg_ref[...] += jnp.sum(dout * xh, axis=0, keepdims=True)
        u = dout * g
        dx2 = r * (u - xh * jnp.mean(u * xh, axis=-1, keepdims=True))
        dx2_ref[...] = dx2
        dx2_b = dx2.astype(BF16)
        dy = _nt(dx2_b, w)
        dyp_ref[...] = dy[:, :POOL_WIDTH].astype(BF16)
        dya_ref[...] = dy[:, POOL_WIDTH:].astype(BF16)
        acc_ref[...] += _tn(y, dx2_b)

        @pl.when(i == n_tiles - 1)
        def _():
            dw_ref[...] = acc_ref[...].reshape(N_DEV, OUT_SHARD, D_MODEL).astype(BF16)

    tile = lambda width: pl.BlockSpec((TOKEN_TILE, width), lambda i: (i, 0))
    return pl.pallas_call(
        body,
        name="outproj_loss",
        grid=(n_tiles,),
        out_shape=(
            jax.ShapeDtypeStruct((t, D_MODEL), F32),
            jax.ShapeDtypeStruct((t, POOL_WIDTH), BF16),
            jax.ShapeDtypeStruct((t, ATTN_WIDTH), BF16),
            jax.ShapeDtypeStruct((N_DEV, OUT_SHARD, D_MODEL), BF16),
            jax.ShapeDtypeStruct((1, D_MODEL), F32),
            jax.ShapeDtypeStruct((1, 1), F32),
        ),
        in_specs=[
            tile(D_MODEL), tile(D_MODEL), tile(POOL_WIDTH), tile(ATTN_WIDTH),
            pl.BlockSpec((N_DEV, OUT_SHARD, D_MODEL), lambda i: (0, 0, 0)),
            pl.BlockSpec((1, D_MODEL), lambda i: (0, 0)),
        ],
        out_specs=(
            tile(D_MODEL), tile(POOL_WIDTH), tile(ATTN_WIDTH),
            pl.BlockSpec((N_DEV, OUT_SHARD, D_MODEL), lambda i: (0, 0, 0)),
            pl.BlockSpec((1, D_MODEL), lambda i: (0, 0)),
            pl.BlockSpec((1, 1), lambda i: (0, 0)),
        ),
        scratch_shapes=[pltpu.VMEM((D_MODEL, D_MODEL), F32)],
        compiler_params=_params(("arbitrary",)),
    )(x2d, tgt2d, y_pool, y_attn, wout_g, g2)


def _dproj_specs():
    tile = lambda width: pl.BlockSpec((TOKEN_TILE, width), lambda i: (i, 0))
    return [tile(2 * POOL_WIDTH)] + [tile(ATTN_WIDTH)] * 4


def _inproj_bwd_dx(x2d, dx2, dproj, g1, wg, dwin_g):
    t = x2d.shape[0]
    n_tiles = t // TOKEN_TILE

    def body(x_ref, dx2_ref, dp_ref, dq_ref, dk_ref, dv_ref, dag_ref, g_ref, wg_hbm, dwin_hbm,
             gx_ref, dg_ref, land_hbm, wfull_ref, sem, send_sems, recv_sems):
        i = pl.program_id(0)
        exchange = _exchange_copies(dwin_hbm, land_hbm, send_sems, recv_sems)

        @pl.when(i == 0)
        def _():
            for cp in exchange:
                cp.start()
            _load_w_in(wg_hbm, wfull_ref, sem)
            dg_ref[...] = jnp.zeros_like(dg_ref)

        dproj_t = jnp.concatenate([dp_ref[...], dq_ref[...], dk_ref[...], dv_ref[...], dag_ref[...]], axis=1)
        dh = _nt(dproj_t, wfull_ref[...])
        xf = x_ref[...]
        r = lax.rsqrt(jnp.mean(xf * xf, axis=-1, keepdims=True) + EPS)
        xh = xf * r
        dg_ref[...] += jnp.sum(dh * xh, axis=0, keepdims=True)
        u = dh * g_ref[...]
        gx_ref[...] = dx2_ref[...] + r * (u - xh * jnp.mean(u * xh, axis=-1, keepdims=True))

        @pl.when(i == n_tiles - 1)
        def _():
            for cp in exchange:
                cp.wait_recv()
            for cp in exchange:
                cp.wait_send()

    tile = pl.BlockSpec((TOKEN_TILE, D_MODEL), lambda i: (i, 0))
    hbm = pl.BlockSpec(memory_space=pl.ANY)
    return pl.pallas_call(
        body,
        name="inproj_bwd_dx",
        grid=(n_tiles,),
        out_shape=(
            jax.ShapeDtypeStruct((t, D_MODEL), F32),
            jax.ShapeDtypeStruct((1, D_MODEL), F32),
            jax.ShapeDtypeStruct((N_DEV - 1, D_MODEL, IN_SHARD), BF16),
        ),
        in_specs=[tile, tile] + _dproj_specs() + [pl.BlockSpec((1, D_MODEL), lambda i: (0, 0)), hbm, hbm],
        out_specs=(tile, pl.BlockSpec((1, D_MODEL), lambda i: (0, 0)), hbm),
        scratch_shapes=[
            pltpu.VMEM((D_MODEL, IN_WIDTH), BF16),
            pltpu.SemaphoreType.DMA((N_DEV,)),
            pltpu.SemaphoreType.DMA((N_DEV - 1,)),
            pltpu.SemaphoreType.DMA((N_DEV - 1,)),
        ],
        compiler_params=_params(("arbitrary",)),
    )(x2d, dx2, *dproj, g1, wg, dwin_g)


def _inproj_bwd_dw(x2d, dproj, g1):
    t = x2d.shape[0]
    n_tiles = t // TOKEN_TILE
    widths = [2 * POOL_WIDTH] + [ATTN_WIDTH] * 4

    def body(x_ref, dp_ref, dq_ref, dk_ref, dv_ref, dag_ref, g_ref, out_ref, acc_ref):
        i = pl.program_id(0)

        @pl.when(i == 0)
        def _():
            acc_ref[...] = jnp.zeros_like(acc_ref)

        xf = x_ref[...]
        r = lax.rsqrt(jnp.mean(xf * xf, axis=-1, keepdims=True) + EPS)
        h = ((xf * r) * g_ref[...]).astype(BF16)
        col = 0
        for ref, width in zip((dp_ref, dq_ref, dk_ref, dv_ref, dag_ref), widths):
            for c0 in range(0, width, POOL_WIDTH):
                acc_ref[:, col:col + POOL_WIDTH] += _tn(h, ref[:, c0:c0 + POOL_WIDTH])
                col += POOL_WIDTH

        @pl.when(i == n_tiles - 1)
        def _():
            for d in range(N_DEV):
                out_ref[d] = acc_ref[:, d * IN_SHARD:(d + 1) * IN_SHARD].astype(BF16)

    return pl.pallas_call(
        body,
        name="inproj_bwd_dw",
        grid=(n_tiles,),
        out_shape=jax.ShapeDtypeStruct((N_DEV, D_MODEL, IN_SHARD), BF16),
        in_specs=[pl.BlockSpec((TOKEN_TILE, D_MODEL), lambda i: (i, 0))] + _dproj_specs() + [
            pl.BlockSpec((1, D_MODEL), lambda i: (0, 0)),
        ],
        out_specs=pl.BlockSpec((N_DEV, D_MODEL, IN_SHARD), lambda i: (0, 0, 0)),
        scratch_shapes=[pltpu.VMEM((D_MODEL, IN_WIDTH), F32)],
        compiler_params=_params(("arbitrary",)),
    )(x2d, *dproj, g1)


def _adamw(w, g, m, v):
    m = ADAM_B1 * m + (1.0 - ADAM_B1) * g
    v = ADAM_B2 * v + (1.0 - ADAM_B2) * (g * g)
    m_hat = m / (1.0 - ADAM_B1 ** ADAM_STEP)
    v_hat = v / (1.0 - ADAM_B2 ** ADAM_STEP)
    delta = -ADAM_LR * (m_hat / (jnp.sqrt(v_hat) + ADAM_EPS) + ADAM_WD * w)
    return delta, m, v


def _reduce_adamw(dwin_g, land_in, dwout_g, land_out, small_g,
                  w_in, m_in, v_in, w_out, m_out, v_out, small_w, small_m, small_v):
    rows = small_g.shape[0]

    def body(dwin_hbm, lin_ref, dwout_hbm, lout_ref, sg_ref, win_ref, min_ref, vin_ref, wout_ref, mout_ref, vout_ref,
             sw_ref, sm_ref, sv_ref,
             gin_ref, din_ref, nmin_ref, nvin_ref, gout_ref, dout_ref, nmout_ref, nvout_ref,
             gs_ref, ds_ref, nms_ref, nvs_ref,
             own_in, own_out, rs_ref, send_sems, recv_sems, local_sems):
        x, y, c = _mesh_pos()
        me = _dev_index((x, y, c))
        started = [
            pltpu.make_async_remote_copy(src_ref=sg_ref, dst_ref=rs_ref.at[r], send_sem=send_sems.at[r - 1],
                                         recv_sem=recv_sems.at[r - 1], device_id=_peer(r), device_id_type=MESH_ID)
            for r in range(1, N_DEV)
        ]
        for cp in started:
            cp.start()
        mine = [pltpu.make_async_copy(dwin_hbm.at[me], own_in, local_sems.at[0]),
                pltpu.make_async_copy(dwout_hbm.at[me], own_out, local_sems.at[1])]
        for cp in mine:
            cp.start()
        rs_ref[0] = sg_ref[...]
        for cp in mine:
            cp.wait()

        g_in = own_in[...].astype(F32)
        g_out = own_out[...].astype(F32)
        for r in range(N_DEV - 1):
            g_in = g_in + lin_ref[r].astype(F32)
            g_out = g_out + lout_ref[r].astype(F32)

        def update(g, w_ref, m_ref, v_ref, g_ref, d_ref, nm_ref, nv_ref):
            delta, m_new, v_new = _adamw(w_ref[...], g, m_ref[...], v_ref[...])
            g_ref[...] = g
            d_ref[...] = delta
            nm_ref[...] = m_new
            nv_ref[...] = v_new

        update(g_in, win_ref, min_ref, vin_ref, gin_ref, din_ref, nmin_ref, nvin_ref)
        update(g_out, wout_ref, mout_ref, vout_ref, gout_ref, dout_ref, nmout_ref, nvout_ref)

        for cp in started:
            cp.wait_recv()
        for cp in started:
            cp.wait_send()
        g_small = rs_ref[me]
        for s in range(1, N_DEV):
            g_small = g_small + rs_ref[me ^ s]
        update(g_small, sw_ref, sm_ref, sv_ref, gs_ref, ds_ref, nms_ref, nvs_ref)

    vm = pl.BlockSpec(memory_space=pltpu.VMEM)
    hbm = pl.BlockSpec(memory_space=pl.ANY)
    shapes = [jax.ShapeDtypeStruct((D_MODEL, IN_SHARD), F32)] * 4
    shapes += [jax.ShapeDtypeStruct((OUT_SHARD, D_MODEL), F32)] * 4
    shapes += [jax.ShapeDtypeStruct((rows, LANES), F32)] * 4
    return pl.pallas_call(
        body,
        name="reduce_adamw",
        out_shape=tuple(shapes),
        in_specs=[hbm, vm, hbm, vm] + [vm] * 10,
        out_specs=tuple([vm] * 12),
        scratch_shapes=[
            pltpu.VMEM((D_MODEL, IN_SHARD), BF16),
            pltpu.VMEM((OUT_SHARD, D_MODEL), BF16),
            pltpu.VMEM((N_DEV, rows, LANES), F32),
            pltpu.SemaphoreType.DMA((N_DEV - 1,)),
            pltpu.SemaphoreType.DMA((N_DEV - 1,)),
            pltpu.SemaphoreType.DMA((2,)),
        ],
        compiler_params=_params(),
    )(dwin_g, land_in, dwout_g, land_out, small_g, w_in, m_in, v_in, w_out, m_out, v_out, small_w, small_m, small_v)


_SMALL_ROWS = (8, 8, 8, 16, 512, 8)


def _pack_small(norm_gain, final_gain, pool_scale, rel_bias, pool_w, loss_row):
    rb = jnp.pad(rel_bias.reshape(N_HEADS, -1), ((0, 0), (0, 2 * LANES - rel_bias.shape[-1])))
    parts = [
        norm_gain.reshape(8, LANES), final_gain.reshape(8, LANES),
        jnp.pad(pool_scale.reshape(4, LANES), ((0, 4), (0, 0))),
        rb.reshape(16, LANES), pool_w.reshape(N_GROUPS * GROUP_DIM, GROUP_DIM), loss_row,
    ]
    return jnp.concatenate(parts, axis=0)


def _unpack_small(packed):
    offs = np.concatenate([[0], np.cumsum(_SMALL_ROWS)])
    seg = [packed[offs[k]:offs[k + 1]] for k in range(len(_SMALL_ROWS))]
    norm_gain = seg[0].reshape(1, D_MODEL)
    final_gain = seg[1].reshape(D_MODEL)
    pool_scale = seg[2][:4].reshape(1, POOL_WIDTH)
    rel_bias = seg[3].reshape(N_HEADS, 2 * LANES)[:, :N_REL].reshape(1, N_HEADS, N_REL)
    pool_w = seg[4].reshape(1, N_GROUPS, GROUP_DIM, GROUP_DIM)
    return norm_gain, pool_w, pool_scale, rel_bias, final_gain, seg[5]


def kernel(x, norm_gain, w_in, pool_w, pool_scale, rel_bias, w_out, final_norm_gain, loss_target, m_norm_gain, m_w_in, m_pool_w, m_pool_scale, m_rel_bias, m_w_out, m_final_norm_gain, v_norm_gain, v_w_in, v_pool_w, v_pool_scale, v_rel_bias, v_w_out, v_final_norm_gain):
    t = x.shape[1]
    assert x.shape[0] == 1 and t % TOKEN_TILE == 0 and t // Q_BLOCK >= 4
    x2d = x[0]
    tgt2d = loss_target[0]
    g2 = final_norm_gain.reshape(1, D_MODEL)

    rb = rel_bias[0]
    rel_line = jnp.concatenate([
        jnp.broadcast_to(rb[:, :1], (N_HEADS, _REL_FIRST)), rb,
        jnp.broadcast_to(rb[:, N_REL - 1:], (N_HEADS, TOEPLITZ - _REL_FIRST - N_REL)),
    ], axis=1).reshape(N_HEADS, 1, TOEPLITZ)
    wg_in, bias_tile = _gather_weights(w_in[0], rel_line)

    pvg, qkv, ag = _norm_inproj(x2d, norm_gain, wg_in)
    y_pool, wg_out = _pool_fwd(pvg, pool_w[0], pool_scale, w_out[0])
    o, y_attn = _attn_fwd(qkv, ag, bias_tile)
    dx2, dy_pool, dy_attn, dwout_g, d_g2, loss_sum = _outproj_loss(x2d, tgt2d, y_pool, y_attn, wg_out, g2)
    d_pool, d_pw, d_ps, land_out = _pool_bwd(pvg, dy_pool, pool_w[0], pool_scale, dwout_g)
    dq, dk, dv, dag, ds_sum = _attn_bwd(qkv, ag, o, dy_attn, bias_tile)
    d_rb = _bias_grad(ds_sum).reshape(N_HEADS, 2 * LANES)[:, :N_REL]
    dproj = (d_pool, dq, dk, dv, dag)
    dwin_g = _inproj_bwd_dw(x2d, dproj, norm_gain)
    grad_x, d_g1, land_in = _inproj_bwd_dx(x2d, dx2, dproj, norm_gain, wg_in, dwin_g)

    loss_row = jnp.broadcast_to(0.5 * loss_sum, (8, LANES))
    small_g = _pack_small(d_g1, d_g2, d_ps, d_rb, d_pw, loss_row)
    zeros = jnp.zeros((8, LANES), F32)
    small_w = _pack_small(norm_gain, final_norm_gain, pool_scale, rel_bias, pool_w, zeros)
    small_m = _pack_small(m_norm_gain, m_final_norm_gain, m_pool_scale, m_rel_bias, m_pool_w, zeros)
    small_v = _pack_small(v_norm_gain, v_final_norm_gain, v_pool_scale, v_rel_bias, v_pool_w, zeros)

    res = _reduce_adamw(dwin_g, land_in, dwout_g, land_out, small_g, w_in[0], m_w_in[0], v_w_in[0],
                        w_out[0], m_w_out[0], v_w_out[0], small_w, small_m, small_v)
    big_in = [a[None] for a in res[0:4]]
    big_out = [a[None] for a in res[4:8]]
    small = [_unpack_small(a) for a in res[8:12]]
    loss = small[0][5][0, 0]

    def leaves(k):
        g1_, pw_, ps_, rb_, g2_, _ = small[k]
        return [g1_, big_in[k], pw_, ps_, rb_, big_out[k], g2_]

    return (loss, grad_x[None], *leaves(0), *leaves(1), *leaves(2), *leaves(3))
```

```python
import math

import jax
import jax.numpy as jnp
from jax import lax
from jax.experimental import pallas as pl
from jax.experimental.pallas import tpu as pltpu

F32 = jnp.float32
BF16 = jnp.bfloat16
MESH_ID = pl.DeviceIdType.MESH

D_MODEL = 1024
POOL_WIDTH = 512
ATTN_WIDTH = 512
POOL_WINDOWS = (2, 4, 8, 16)
N_GROUPS = 4
GROUP_DIM = 128
HEAD_DIM = 64
N_HEADS = 8
CHUNK = 64
LEFT_CHUNKS = 8
MAX_REL = 64
N_REL = 2 * MAX_REL + 1
IN_WIDTH = 2 * POOL_WIDTH + 4 * ATTN_WIDTH
EPS = 1e-6
MASK_VALUE = -1e30
ATTN_SCALE = 1.0 / math.sqrt(HEAD_DIM)
ADAM_LR = 0.001
ADAM_B1 = 0.9
ADAM_B2 = 0.999
ADAM_EPS = 1e-08
ADAM_WD = 0.01
ADAM_STEP = 10

N_DEV = 8
IN_SHARD = IN_WIDTH // N_DEV
OUT_SHARD = D_MODEL // N_DEV

LANES = 128
TOKEN_TILE = 512
HALO = 16
Q_BLOCK = 256
KV_BLOCKS = 3
KV_WINDOW = KV_BLOCKS * Q_BLOCK
PAIR = 2 * HEAD_DIM
N_PAIRS = N_HEADS // 2
TOEPLITZ = 1024
VMEM_LIMIT = 56 * 1024 * 1024


def _params(sem=None, vmem=VMEM_LIMIT):
    return pltpu.CompilerParams(dimension_semantics=sem, vmem_limit_bytes=vmem)


def _sigmoid(x):
    return 1.0 / (1.0 + jnp.exp(-x))


def _nt(a, b):
    return lax.dot_general(a, b, (((1,), (1,)), ((), ())), preferred_element_type=F32)


def _tn(a, b):
    return lax.dot_general(a, b, (((0,), (0,)), ((), ())), preferred_element_type=F32)


def _nn(a, b):
    return jnp.dot(a, b, preferred_element_type=F32)


def _mesh_pos():
    return lax.axis_index("x"), lax.axis_index("y"), lax.axis_index("c")


def _dev_index(p):
    return 4 * p[0] + 2 * p[1] + p[2]


def _peer(r):
    x, y, c = _mesh_pos()
    return (x ^ ((r >> 2) & 1), y ^ ((r >> 1) & 1), c ^ (r & 1))


def _exchange_copies(src_hbm, land_hbm, send_sems, recv_sems):
    return [
        pltpu.make_async_remote_copy(
            src_ref=src_hbm.at[_dev_index(_peer(r))], dst_ref=land_hbm.at[r - 1],
            send_sem=send_sems.at[r - 1], recv_sem=recv_sems.at[r - 1],
            device_id=_peer(r), device_id_type=MESH_ID)
        for r in range(1, N_DEV)
    ]


def _gather_weights(w_in_shard, rel_line):
    def body(win_ref, line_ref, gin_ref, bias_ref, sin_ref, send_sems, recv_sems):
        x, y, c = _mesh_pos()
        me, sibling = (x, y, c), (x, y, 1 - c)
        chips = [(1 - x, y), (x, 1 - y), (1 - x, 1 - y)]

        sin_ref[...] = win_ref[...].astype(BF16)
        gin_ref[_dev_index(me)] = sin_ref[...]

        def copy(k, block, to, from_shard=False):
            return pltpu.make_async_remote_copy(
                src_ref=sin_ref if from_shard else gin_ref.at[_dev_index(block)],
                dst_ref=gin_ref.at[_dev_index(block)],
                send_sem=send_sems.at[k],
                recv_sem=recv_sems.at[k],
                device_id=to,
                device_id_type=MESH_ID,
            )

        first = [copy(0, me, sibling, True)]
        first += [copy(1 + j, me, (*chip, c), True) for j, chip in enumerate(chips)]
        for cp in first:
            cp.start()
        passed = [copy(4 + j, (*chip, c), sibling) for j, chip in enumerate(chips)]

        def bias_heads(lo, hi):
            for h in range(lo, hi):
                bias_ref[h] = _toeplitz_bias(line_ref[h])

        bias_heads(0, N_HEADS - 3)
        for j, chip in enumerate(chips):
            copy(1 + j, (*chip, c), me).wait_recv()
            passed[j].start()
            bias_heads(N_HEADS - 3 + j, N_HEADS - 2 + j)
        copy(0, sibling, me).wait_recv()
        for j, chip in enumerate(chips):
            copy(4 + j, (*chip, 1 - c), me).wait_recv()
        for cp in first + passed:
            cp.wait_send()

    vm = pl.BlockSpec(memory_space=pltpu.VMEM)
    return pl.pallas_call(
        body,
        name="gather_weights",
        out_shape=(
            jax.ShapeDtypeStruct((N_DEV, D_MODEL, IN_SHARD), BF16),
            jax.ShapeDtypeStruct((N_HEADS, Q_BLOCK, KV_WINDOW), F32),
        ),
        in_specs=[vm, vm],
        out_specs=(vm, vm),
        scratch_shapes=[
            pltpu.VMEM((D_MODEL, IN_SHARD), BF16),
            pltpu.SemaphoreType.DMA((7,)),
            pltpu.SemaphoreType.DMA((7,)),
        ],
        compiler_params=_params(),
    )(w_in_shard, rel_line)


def _load_w_in(wg_hbm, wfull_ref, sem):
    copies = [
        pltpu.make_async_copy(wg_hbm.at[d], wfull_ref.at[:, d * IN_SHARD:(d + 1) * IN_SHARD], sem.at[d])
        for d in range(N_DEV)
    ]
    for cp in copies:
        cp.start()
    for cp in copies:
        cp.wait()


def _norm_inproj(x2d, g1, wg):
    t = x2d.shape[0]
    n_chunks = IN_WIDTH // POOL_WIDTH

    def body(x_ref, g_ref, wg_hbm, pvg_ref, qkv_ref, ag_ref, wfull_ref, sem):
        @pl.when(pl.program_id(0) == 0)
        def _():
            _load_w_in(wg_hbm, wfull_ref, sem)

        xf = x_ref[...]
        r = lax.rsqrt(jnp.mean(xf * xf, axis=-1, keepdims=True) + EPS)
        h = ((xf * r) * g_ref[...]).astype(BF16)
        for ci in range(n_chunks):
            res = _nn(h, wfull_ref[:, ci * POOL_WIDTH:(ci + 1) * POOL_WIDTH])
            if ci < 2:
                pvg_ref[:, ci * POOL_WIDTH:(ci + 1) * POOL_WIDTH] = res
            elif ci < 5:
                qkv_ref[:, (ci - 2) * POOL_WIDTH:(ci - 1) * POOL_WIDTH] = res.astype(BF16)
            else:
                ag_ref[...] = res

    return pl.pallas_call(
        body,
        name="norm_inproj",
        grid=(t // TOKEN_TILE,),
        out_shape=(
            jax.ShapeDtypeStruct((t, 2 * POOL_WIDTH), F32),
            jax.ShapeDtypeStruct((t, 3 * ATTN_WIDTH), BF16),
            jax.ShapeDtypeStruct((t, ATTN_WIDTH), F32),
        ),
        in_specs=[
            pl.BlockSpec((TOKEN_TILE, D_MODEL), lambda i: (i, 0)),
            pl.BlockSpec((1, D_MODEL), lambda i: (0, 0)),
            pl.BlockSpec(memory_space=pl.ANY),
        ],
        out_specs=(
            pl.BlockSpec((TOKEN_TILE, 2 * POOL_WIDTH), lambda i: (i, 0)),
            pl.BlockSpec((TOKEN_TILE, 3 * ATTN_WIDTH), lambda i: (i, 0)),
            pl.BlockSpec((TOKEN_TILE, ATTN_WIDTH), lambda i: (i, 0)),
        ),
        scratch_shapes=[pltpu.VMEM((D_MODEL, IN_WIDTH), BF16), pltpu.SemaphoreType.DMA((N_DEV,))],
        compiler_params=_params(("arbitrary",)),
    )(x2d, g1, wg)


def _inv_count(first_row, rows, window):
    tpos = first_row + lax.broadcasted_iota(jnp.int32, (rows, 1), 0)
    return 1.0 / jnp.minimum(tpos + 1, window).astype(F32)


def _causal_window_sum(ext, window):
    s, k = ext, 1
    while k < window:
        s = s + pltpu.roll(s, k, 0)
        k *= 2
    return s


def _pool_diffs(pv, halo, first_row, gi):
    w = POOL_WINDOWS[gi]
    sl = slice(gi * GROUP_DIM, (gi + 1) * GROUP_DIM)
    ext = jnp.concatenate([halo[:, sl], pv[:, sl]], axis=0)
    s = _causal_window_sum(ext, w)[HALO:]
    return s * _inv_count(first_row, pv.shape[0], w) - pv[:, sl]


def _pool_fwd(pvg, pool_w, pool_scale):
    t = pvg.shape[0]
    n_tiles = t // TOKEN_TILE
    halo_per_tile = TOKEN_TILE // HALO

    def body(cur_ref, halo_ref, pw_ref, ps_ref, y_ref):
        i = pl.program_id(0)
        pv = cur_ref[:, :POOL_WIDTH]
        pg = cur_ref[:, POOL_WIDTH:]
        halo = jnp.where(i > 0, halo_ref[...], 0.0)
        for gi in range(N_GROUPS):
            sl = slice(gi * GROUP_DIM, (gi + 1) * GROUP_DIM)
            d = _pool_diffs(pv, halo, i * TOKEN_TILE, gi)
            z = _nn(d.astype(BF16), pw_ref[gi].astype(BF16))
            g = pg[:, sl]
            y_ref[:, sl] = ((z * ps_ref[:, sl]) * (g * _sigmoid(g))).astype(BF16)

    return pl.pallas_call(
        body,
        name="pool_fwd",
        grid=(n_tiles,),
        out_shape=jax.ShapeDtypeStruct((t, POOL_WIDTH), BF16),
        in_specs=[
            pl.BlockSpec((TOKEN_TILE, 2 * POOL_WIDTH), lambda i: (i, 0)),
            pl.BlockSpec((HALO, POOL_WIDTH), lambda i: (jnp.maximum(i * halo_per_tile - 1, 0), 0)),
            pl.BlockSpec((N_GROUPS, GROUP_DIM, GROUP_DIM), lambda i: (0, 0, 0)),
            pl.BlockSpec((1, POOL_WIDTH), lambda i: (0, 0)),
        ],
        out_specs=pl.BlockSpec((TOKEN_TILE, POOL_WIDTH), lambda i: (i, 0)),
        compiler_params=_params(("arbitrary",)),
    )(pvg, pvg, pool_w, pool_scale)


def _pool_bwd(pvg, dy_pool, pool_w, pool_scale, dwout_g):
    t = pvg.shape[0]
    n_tiles = t // TOKEN_TILE
    halo_per_tile = TOKEN_TILE // HALO
    last_halo = t // HALO - 1

    def body(cur_ref, prev_ref, pgn_ref, dy_ref, dyn_ref, pw_ref, ps_ref, dwout_hbm,
             dp_ref, dpw_ref, dps_ref, land_hbm, send_sems, recv_sems):
        i = pl.program_id(0)
        exchange = _exchange_copies(dwout_hbm, land_hbm, send_sems, recv_sems)

        @pl.when(i == 0)
        def _():
            for cp in exchange:
                cp.start()
            dpw_ref[...] = jnp.zeros_like(dpw_ref)
            dps_ref[...] = jnp.zeros_like(dps_ref)

        pv = cur_ref[:, :POOL_WIDTH]
        pg = cur_ref[:, POOL_WIDTH:]
        prev = jnp.where(i > 0, prev_ref[...], 0.0)
        has_next = i < n_tiles - 1
        rows = TOKEN_TILE + HALO
        for gi in range(N_GROUPS):
            w = POOL_WINDOWS[gi]
            sl = slice(gi * GROUP_DIM, (gi + 1) * GROUP_DIM)
            pw = pw_ref[gi].astype(BF16)
            ps = ps_ref[:, sl]
            d = _pool_diffs(pv, prev, i * TOKEN_TILE, gi).astype(BF16)
            z = _nn(d, pw)
            g_ext = jnp.concatenate([pg[:, sl], pgn_ref[:, sl]], axis=0)
            dy_ext = jnp.concatenate([dy_ref[:, sl], dyn_ref[:, sl]], axis=0).astype(F32)
            sig = _sigmoid(g_ext)
            gate = g_ext * sig
            dz_ext = ((dy_ext * gate) * ps).astype(BF16)
            dd_ext = _nt(dz_ext, pw)
            e = dd_ext * _inv_count(i * TOKEN_TILE, rows, w)
            row = lax.broadcasted_iota(jnp.int32, (rows, 1), 0)
            e = jnp.where(jnp.logical_or(row < TOKEN_TILE, has_next), e, 0.0)
            s, k = e, 1
            while k < w:
                s = s + pltpu.roll(s, rows - k, 0)
                k *= 2
            dp_ref[:, sl] = (s[:TOKEN_TILE] - dd_ext[:TOKEN_TILE]).astype(BF16)
            dy = dy_ext[:TOKEN_TILE]
            g = g_ext[:TOKEN_TILE]
            sg = sig[:TOKEN_TILE]
            dgate = sg * (1.0 + g * (1.0 - sg))
            dp_ref[:, POOL_WIDTH + gi * GROUP_DIM:POOL_WIDTH + (gi + 1) * GROUP_DIM] = (
                (dy * (z * ps)) * dgate).astype(BF16)
            dps_ref[:, sl] += jnp.sum((dy * gate[:TOKEN_TILE]) * z, axis=0, keepdims=True)
            dpw_ref[gi] += _tn(d, dz_ext[:TOKEN_TILE])

        @pl.when(i == n_tiles - 1)
        def _():
            for cp in exchange:
                cp.wait_recv()
            for cp in exchange:
                cp.wait_send()

    return pl.pallas_call(
        body,
        name="pool_bwd",
        grid=(n_tiles,),
        out_shape=(
            jax.ShapeDtypeStruct((t, 2 * POOL_WIDTH), BF16),
            jax.ShapeDtypeStruct((N_GROUPS, GROUP_DIM, GROUP_DIM), F32),
            jax.ShapeDtypeStruct((1, POOL_WIDTH), F32),
            jax.ShapeDtypeStruct((N_DEV - 1, OUT_SHARD, D_MODEL), BF16),
        ),
        in_specs=[
            pl.BlockSpec((TOKEN_TILE, 2 * POOL_WIDTH), lambda i: (i, 0)),
            pl.BlockSpec((HALO, POOL_WIDTH), lambda i: (jnp.maximum(i * halo_per_tile - 1, 0), 0)),
            pl.BlockSpec((HALO, POOL_WIDTH), lambda i: (jnp.minimum((i + 1) * halo_per_tile, last_halo), 1)),
            pl.BlockSpec((TOKEN_TILE, POOL_WIDTH), lambda i: (i, 0)),
            pl.BlockSpec((HALO, POOL_WIDTH), lambda i: (jnp.minimum((i + 1) * halo_per_tile, last_halo), 0)),
            pl.BlockSpec((N_GROUPS, GROUP_DIM, GROUP_DIM), lambda i: (0, 0, 0)),
            pl.BlockSpec((1, POOL_WIDTH), lambda i: (0, 0)),
            pl.BlockSpec(memory_space=pl.ANY),
        ],
        out_specs=(
            pl.BlockSpec((TOKEN_TILE, 2 * POOL_WIDTH), lambda i: (i, 0)),
            pl.BlockSpec((N_GROUPS, GROUP_DIM, GROUP_DIM), lambda i: (0, 0, 0)),
            pl.BlockSpec((1, POOL_WIDTH), lambda i: (0, 0)),
            pl.BlockSpec(memory_space=pl.ANY),
        ),
        scratch_shapes=[pltpu.SemaphoreType.DMA((N_DEV - 1,)), pltpu.SemaphoreType.DMA((N_DEV - 1,))],
        compiler_params=_params(("arbitrary",)),
    )(pvg, pvg, pvg, dy_pool, dy_pool, pool_w, pool_scale, dwout_g)


_REL_FIRST = KV_WINDOW - 1 - MAX_REL
_DIAG_FIRST = _REL_FIRST - (Q_BLOCK - 1)


def _skew_rows(a, right):
    row = lax.broadcasted_iota(jnp.int32, a.shape, 0)
    for b in range(8):
        shift = (1 << b) if right else TOEPLITZ - (1 << b)
        a = jnp.where((row >> b) & 1 == 1, pltpu.roll(a, shift, 1), a)
    return a


def _toeplitz_bias(line):
    a = jnp.broadcast_to(line, (Q_BLOCK, TOEPLITZ))
    a = _skew_rows(a, True)
    a = pltpu.roll(a, TOEPLITZ - (Q_BLOCK - 1), 1)
    a = a[:, :KV_WINDOW]
    qc = lax.broadcasted_iota(jnp.int32, a.shape, 0) // CHUNK
    kc = lax.broadcasted_iota(jnp.int32, a.shape, 1) // CHUNK
    visible = jnp.logical_and(kc >= qc, kc <= qc + LEFT_CHUNKS)
    return jnp.where(visible, a, MASK_VALUE)


def _bias_grad(ds_sum):
    def body(ds_ref, out_ref):
        for h in range(N_HEADS):
            a = jnp.concatenate([ds_ref[h], jnp.zeros((Q_BLOCK, TOEPLITZ - KV_WINDOW), F32)], axis=1)
            a = _skew_rows(a, False)
            diag = jnp.sum(a, axis=0, keepdims=True)
            c = lax.broadcasted_iota(jnp.int32, diag.shape, 1)
            far = jnp.logical_or(c <= _DIAG_FIRST, c > KV_WINDOW)
            near = jnp.logical_and(c >= _DIAG_FIRST + 2 * MAX_REL, c <= KV_WINDOW)
            low = jnp.sum(jnp.where(far, diag, 0.0), axis=1, keepdims=True)
            high = jnp.sum(jnp.where(near, diag, 0.0), axis=1, keepdims=True)
            line = pltpu.roll(diag, TOEPLITZ - _DIAG_FIRST, 1)[:, :2 * LANES]
            r = lax.broadcasted_iota(jnp.int32, line.shape, 1)
            line = jnp.where(r == 0, low, jnp.where(r == 2 * MAX_REL, high, jnp.where(r < 2 * MAX_REL, line, 0.0)))
            out_ref[h:h + 1, :] = line

    vm = pl.BlockSpec(memory_space=pltpu.VMEM)
    return pl.pallas_call(
        body,
        name="bias_grad",
        out_shape=jax.ShapeDtypeStruct((N_HEADS, 2 * LANES), F32),
        in_specs=[vm],
        out_specs=vm,
        compiler_params=_params(),
    )(ds_sum)


def _head_lanes(hh):
    lane = lax.broadcasted_iota(jnp.int32, (1, PAIR), 1)
    return (lane < HEAD_DIM) if hh == 0 else (lane >= HEAD_DIM)


def _block_rows(first_block, n_blocks):
    if isinstance(first_block, int):
        return pl.ds(first_block * Q_BLOCK, n_blocks * Q_BLOCK)
    return pl.ds(pl.multiple_of(first_block * Q_BLOCK, Q_BLOCK), n_blocks * Q_BLOCK)


def _attn_fwd(qkv, ag, bias_tile, w_out_shard):
    t = qkv.shape[0]
    nb = t // Q_BLOCK

    def body(q_ref, k_ref, v_ref, ag_ref, bias_ref, wout_ref, o_ref, y_ref, gout_hbm,
             s_scr, p_scr, linv_scr, sout_ref, send_sems, recv_sems, local_sem):
        pair = pl.program_id(0)
        me = _dev_index(_mesh_pos())
        mine = pltpu.make_async_copy(sout_ref, gout_hbm.at[me], local_sem)

        def shard_copy(r, block):
            return pltpu.make_async_remote_copy(
                src_ref=sout_ref, dst_ref=gout_hbm.at[block], send_sem=send_sems.at[r - 1],
                recv_sem=recv_sems.at[r - 1], device_id=_peer(r), device_id_type=MESH_ID)

        @pl.when(pair == 0)
        def _():
            sout_ref[...] = wout_ref[...].astype(BF16)
            mine.start()
            for r in range(1, N_DEV):
                shard_copy(r, me).start()

        def scores(i, slot, nwin):
            q = q_ref[_block_rows(i, 1), :]
            kw = k_ref[_block_rows(i + 1 - nwin, nwin), :]
            for hh in range(2):
                q_h = jnp.where(_head_lanes(hh), q, jnp.zeros_like(q)) * ATTN_SCALE
                s_scr[slot, hh, :, :nwin * Q_BLOCK] = _nt(q_h, kw)

        def softmax(slot, nwin):
            for hh in range(2):
                s = s_scr[slot, hh, :, :nwin * Q_BLOCK] + bias_ref[hh, :, (KV_BLOCKS - nwin) * Q_BLOCK:]
                e = jnp.exp(s - jnp.max(s, axis=-1, keepdims=True))
                linv_scr[slot, hh] = 1.0 / jnp.sum(e, axis=-1, keepdims=True)
                p_scr[slot, hh, :, :nwin * Q_BLOCK] = e.astype(BF16)

        def output(i, slot, nwin):
            vw = v_ref[_block_rows(i + 1 - nwin, nwin), :]
            outs = [_nn(p_scr[slot, hh, :, :nwin * Q_BLOCK], vw) * linv_scr[slot, hh] for hh in range(2)]
            o = jnp.where(_head_lanes(0), outs[0], outs[1])
            g = ag_ref[_block_rows(i, 1), :]
            o_ref[_block_rows(i, 1), :] = o.astype(BF16)
            y_ref[_block_rows(i, 1), :] = (o * (g * _sigmoid(g))).astype(BF16)

        scores(0, 0, 1)
        scores(1, 1, 2)
        softmax(0, 1)
        scores(2, 0, 3)
        softmax(1, 2)
        output(0, 0, 1)
        scores(3, 1, 3)
        softmax(0, 3)
        output(1, 1, 2)

        def two_steps(k, carry):
            i = 3 + 2 * k
            scores(i + 1, 0, KV_BLOCKS)
            softmax(1, KV_BLOCKS)
            output(i - 1, 0, KV_BLOCKS)
            scores(i + 2, 1, KV_BLOCKS)
            softmax(0, KV_BLOCKS)
            output(i, 1, KV_BLOCKS)
            return carry

        lax.fori_loop(0, (nb - 4) // 2, two_steps, 0)
        last = (nb - 1) % 2
        softmax(last, KV_BLOCKS)
        output(nb - 2, 1 - last, KV_BLOCKS)
        output(nb - 1, last, KV_BLOCKS)

        @pl.when(pair == N_PAIRS - 1)
        def _():
            for r in range(1, N_DEV):
                shard_copy(r, _dev_index(_peer(r))).wait_recv()
            for r in range(1, N_DEV):
                shard_copy(r, me).wait_send()
            mine.wait()

    col = lambda c0: pl.BlockSpec((t, PAIR), lambda j: (0, c0 + j))
    return pl.pallas_call(
        body,
        name="attn_fwd",
        grid=(N_PAIRS,),
        out_shape=(
            jax.ShapeDtypeStruct((t, ATTN_WIDTH), BF16),
            jax.ShapeDtypeStruct((t, ATTN_WIDTH), BF16),
            jax.ShapeDtypeStruct((N_DEV, OUT_SHARD, D_MODEL), BF16),
        ),
        in_specs=[col(0), col(N_PAIRS), col(2 * N_PAIRS), col(0),
                  pl.BlockSpec((2, Q_BLOCK, KV_WINDOW), lambda j: (j, 0, 0)),
                  pl.BlockSpec((OUT_SHARD, D_MODEL), lambda j: (0, 0))],
        out_specs=(col(0), col(0), pl.BlockSpec(memory_space=pl.ANY)),
        scratch_shapes=[
            pltpu.VMEM((2, 2, Q_BLOCK, KV_WINDOW), F32),
            pltpu.VMEM((2, 2, Q_BLOCK, KV_WINDOW), BF16),
            pltpu.VMEM((2, 2, Q_BLOCK, 1), F32),
            pltpu.VMEM((OUT_SHARD, D_MODEL), BF16),
            pltpu.SemaphoreType.DMA((N_DEV - 1,)),
            pltpu.SemaphoreType.DMA((N_DEV - 1,)),
            pltpu.SemaphoreType.DMA,
        ],
        compiler_params=_params(("arbitrary",)),
    )(qkv, qkv, qkv, ag, bias_tile, w_out_shard)


def _attn_bwd(qkv, ag, o, dy_attn, bias_tile):
    t = qkv.shape[0]
    nb = t // Q_BLOCK

    def body(q_ref, k_ref, v_ref, ag_ref, o_ref, dy_ref, bias_ref, dq_ref, dk_ref, dv_ref, dag_ref, ds_ref,
             do_scr, s_scr, dp_scr, p_scr, dsb_scr, dq_scr, dk_acc, dv_acc):
        def gates(i, carry):
            rows = _block_rows(i, 1)
            g = ag_ref[rows, :]
            sig = _sigmoid(g)
            dy = dy_ref[rows, :].astype(F32)
            do_scr[rows, :] = (dy * (g * sig)).astype(BF16)
            dag_ref[rows, :] = ((dy * o_ref[rows, :].astype(F32)) * (sig * (1.0 + g * (1.0 - sig)))).astype(BF16)
            return carry

        lax.fori_loop(0, nb, gates, 0)
        ds_ref[...] = jnp.zeros_like(ds_ref)

        def nwin_of(i):
            return min(i + 1, KV_BLOCKS) if isinstance(i, int) else KV_BLOCKS

        def operands(i, hh):
            lanes = _head_lanes(hh)
            q = q_ref[_block_rows(i, 1), :]
            do = do_scr[_block_rows(i, 1), :]
            return (jnp.where(lanes, q, jnp.zeros_like(q)) * ATTN_SCALE, jnp.where(lanes, do, jnp.zeros_like(do)))

        def products(i, hh):
            nwin = nwin_of(i)
            win = _block_rows(i + 1 - nwin, nwin)
            q_h, do_h = operands(i, hh)
            s_scr[hh, :, :nwin * Q_BLOCK] = _nt(q_h, k_ref[win, :])
            dp_scr[hh, :, :nwin * Q_BLOCK] = _nt(do_h, v_ref[win, :])

        def grads(i, hh):
            nwin = nwin_of(i)
            w, off = nwin * Q_BLOCK, (KV_BLOCKS - nwin) * Q_BLOCK
            s = s_scr[hh, :, :w] + bias_ref[hh, :, off:]
            e = jnp.exp(s - jnp.max(s, axis=-1, keepdims=True))
            p = e * (1.0 / jnp.sum(e, axis=-1, keepdims=True))
            dp = dp_scr[hh, :, :w]
            ds = p * (dp - jnp.sum(p * dp, axis=-1, keepdims=True))
            ds_ref[hh, :, off:] += ds
            p_scr[hh, :, :w] = p.astype(BF16)
            dsb_scr[hh, :, :w] = ds.astype(BF16)

        def ring(block):
            return block % KV_BLOCKS if isinstance(block, int) else lax.rem(block, KV_BLOCKS)

        def accumulate(i, hh):
            nwin = nwin_of(i)
            w = nwin * Q_BLOCK
            win = _block_rows(i + 1 - nwin, nwin)
            q_h, do_h = operands(i, hh)
            ds_b = dsb_scr[hh, :, :w]
            dq_h = _nn(ds_b, k_ref[win, :]) * ATTN_SCALE
            dkw = _tn(ds_b, q_h)
            dvw = _tn(p_scr[hh, :, :w], do_h)
            for b in range(nwin):
                slot = ring(i + 1 - nwin + b)
                part = slice(b * Q_BLOCK, (b + 1) * Q_BLOCK)
                if hh == 0 and b == nwin - 1:
                    dk_acc[slot] = dkw[part]
                    dv_acc[slot] = dvw[part]
                else:
                    dk_acc[slot] += dkw[part]
                    dv_acc[slot] += dvw[part]
            if hh == 0:
                dq_scr[...] = dq_h
            else:
                dq_ref[_block_rows(i, 1), :] = jnp.where(_head_lanes(0), dq_scr[...], dq_h).astype(BF16)
                if not (isinstance(i, int) and i < KV_BLOCKS - 1):
                    flush(i - (KV_BLOCKS - 1))

        def flush(block):
            dk_ref[_block_rows(block, 1), :] = dk_acc[ring(block)].astype(BF16)
            dv_ref[_block_rows(block, 1), :] = dv_acc[ring(block)].astype(BF16)

        def tile(n):
            return n // 2, n % 2

        def step(n):
            if n + 1 < 2 * nb:
                products(*tile(n + 1))
            grads(*tile(n))
            if n >= 1:
                accumulate(*tile(n - 1))

        products(0, 0)
        for n in range(2 * KV_BLOCKS):
            step(n)

        def two_steps(i, carry):
            products(i, 1)
            grads(i, 0)
            accumulate(i - 1, 1)
            products(i + 1, 0)
            grads(i, 1)
            accumulate(i, 0)
            return carry

        lax.fori_loop(KV_BLOCKS, nb - 1, two_steps, 0)
        step(2 * nb - 2)
        step(2 * nb - 1)
        accumulate(nb - 1, 1)
        flush(nb - 2)
        flush(nb - 1)

    col = lambda c0: pl.BlockSpec((t, PAIR), lambda j: (0, c0 + j))
    once = pl.BlockSpec((t, PAIR), lambda j: (0, j), pipeline_mode=pl.Buffered(1))
    tile_spec = pl.BlockSpec((2, Q_BLOCK, KV_WINDOW), lambda j: (j, 0, 0))
    out = jax.ShapeDtypeStruct((t, ATTN_WIDTH), BF16)
    return pl.pallas_call(
        body,
        name="attn_bwd",
        grid=(N_PAIRS,),
        out_shape=(out, out, out, out, jax.ShapeDtypeStruct((N_HEADS, Q_BLOCK, KV_WINDOW), F32)),
        in_specs=[col(0), col(N_PAIRS), col(2 * N_PAIRS), once, col(0), col(0), tile_spec],
        out_specs=(col(0), col(0), col(0), col(0), tile_spec),
        scratch_shapes=[
            pltpu.VMEM((t, PAIR), BF16),
            pltpu.VMEM((2, Q_BLOCK, KV_WINDOW), F32),
            pltpu.VMEM((2, Q_BLOCK, KV_WINDOW), F32),
            pltpu.VMEM((2, Q_BLOCK, KV_WINDOW), BF16),
            pltpu.VMEM((2, Q_BLOCK, KV_WINDOW), BF16),
            pltpu.VMEM((Q_BLOCK, PAIR), F32),
            pltpu.VMEM((KV_BLOCKS, Q_BLOCK, PAIR), F32),
            pltpu.VMEM((KV_BLOCKS, Q_BLOCK, PAIR), F32),
        ],
        compiler_params=_params(("arbitrary",), vmem=60 * 1024 * 1024),
    )(qkv, qkv, qkv, ag, o, dy_attn, bias_tile)


def _outproj_loss(x2d, tgt2d, y_pool, y_attn, wout_g, g2):
    t = x2d.shape[0]
    n_tiles = t // TOKEN_TILE

    def body(x_ref, tgt_ref, yp_ref, ya_ref, w_ref, g_ref,
             dx2_ref, dyp_ref, dya_ref, dw_ref, dg_ref, loss_ref, acc_ref):
        i = pl.program_id(0)

        @pl.when(i == 0)
        def _():
            acc_ref[...] = jnp.zeros_like(acc_ref)
            dg_ref[...] = jnp.zeros_like(dg_ref)
            loss_ref[...] = jnp.zeros_like(loss_ref)

        w = w_ref[...].reshape(D_MODEL, D_MODEL)
        y = jnp.concatenate([yp_ref[...], ya_ref[...]], axis=1)
        x2 = x_ref[...] + _nn(y, w)
        r = lax.rsqrt(jnp.mean(x2 * x2, axis=-1, keepdims=True) + EPS)
        xh = x2 * r
        g = g_ref[...]
        diff = xh * g - tgt_ref[...]
        tok = jnp.sum(diff * diff, axis=-1, keepdims=True) * (1.0 / D_MODEL)
        loss_ref[...] += jnp.sum(tok, axis=0, keepdims=True)
        dout = diff * (1.0 / D_MODEL)
        dg_ref[...] += jnp.sum(dout * xh, axis=0, keepdims=True)
        u = dout * g
        dx2 = r * (u - xh * jnp.mean(u * xh, axis=-1, keepdims=True))
        dx2_ref[...] = dx2
        dx2_b = dx2.astype(BF16)
        dy = _nt(dx2_b, w)
        dyp_ref[...] = dy[:, :POOL_WIDTH].astype(BF16)
        dya_ref[...] = dy[:, POOL_WIDTH:].astype(BF16)
        acc_ref[...] += _tn(y, dx2_b)

        @pl.when(i == n_tiles - 1)
        def _():
            dw_ref[...] = acc_ref[...].reshape(N_DEV, OUT_SHARD, D_MODEL).astype(BF16)

    tile = lambda width: pl.BlockSpec((TOKEN_TILE, width), lambda i: (i, 0))
    return pl.pallas_call(
        body,
        name="outproj_loss",
        grid=(n_tiles,),
        out_shape=(
            jax.ShapeDtypeStruct((t, D_MODEL), F32),
            jax.ShapeDtypeStruct((t, POOL_WIDTH), BF16),
            jax.ShapeDtypeStruct((t, ATTN_WIDTH), BF16),
            jax.ShapeDtypeStruct((N_DEV, OUT_SHARD, D_MODEL), BF16),
            jax.ShapeDtypeStruct((1, D_MODEL), F32),
            jax.ShapeDtypeStruct((8, LANES), F32),
        ),
        in_specs=[
            tile(D_MODEL), tile(D_MODEL), tile(POOL_WIDTH), tile(ATTN_WIDTH),
            pl.BlockSpec((N_DEV, OUT_SHARD, D_MODEL), lambda i: (0, 0, 0)),
            pl.BlockSpec((1, D_MODEL), lambda i: (0, 0)),
        ],
        out_specs=(
            tile(D_MODEL), tile(POOL_WIDTH), tile(ATTN_WIDTH),
            pl.BlockSpec((N_DEV, OUT_SHARD, D_MODEL), lambda i: (0, 0, 0)),
            pl.BlockSpec((1, D_MODEL), lambda i: (0, 0)),
            pl.BlockSpec((8, LANES), lambda i: (0, 0)),
        ),
        scratch_shapes=[pltpu.VMEM((D_MODEL, D_MODEL), F32)],
        compiler_params=_params(("arbitrary",)),
    )(x2d, tgt2d, y_pool, y_attn, wout_g, g2)


def _dproj_specs():
    tile = lambda width: pl.BlockSpec((TOKEN_TILE, width), lambda i: (i, 0))
    return [tile(2 * POOL_WIDTH)] + [tile(ATTN_WIDTH)] * 4


def _inproj_bwd_dx(x2d, dx2, dproj, g1, wg, dwin_g):
    t = x2d.shape[0]
    n_tiles = t // TOKEN_TILE

    def body(x_ref, dx2_ref, dp_ref, dq_ref, dk_ref, dv_ref, dag_ref, g_ref, wg_hbm, dwin_hbm,
             gx_ref, dg_ref, land_hbm, wfull_ref, sem, send_sems, recv_sems):
        i = pl.program_id(0)
        exchange = _exchange_copies(dwin_hbm, land_hbm, send_sems, recv_sems)

        @pl.when(i == 0)
        def _():
            for cp in exchange:
                cp.start()
            _load_w_in(wg_hbm, wfull_ref, sem)
            dg_ref[...] = jnp.zeros_like(dg_ref)

        dproj_t = jnp.concatenate([dp_ref[...], dq_ref[...], dk_ref[...], dv_ref[...], dag_ref[...]], axis=1)
        dh = _nt(dproj_t, wfull_ref[...])
        xf = x_ref[...]
        r = lax.rsqrt(jnp.mean(xf * xf, axis=-1, keepdims=True) + EPS)
        xh = xf * r
        dg_ref[...] += jnp.sum(dh * xh, axis=0, keepdims=True)
        u = dh * g_ref[...]
        gx_ref[...] = dx2_ref[...] + r * (u - xh * jnp.mean(u * xh, axis=-1, keepdims=True))

        @pl.when(i == n_tiles - 1)
        def _():
            for cp in exchange:
                cp.wait_recv()
            for cp in exchange:
                cp.wait_send()

    tile = pl.BlockSpec((TOKEN_TILE, D_MODEL), lambda i: (i, 0))
    hbm = pl.BlockSpec(memory_space=pl.ANY)
    return pl.pallas_call(
        body,
        name="inproj_bwd_dx",
        grid=(n_tiles,),
        out_shape=(
            jax.ShapeDtypeStruct((t, D_MODEL), F32),
            jax.ShapeDtypeStruct((1, D_MODEL), F32),
            jax.ShapeDtypeStruct((N_DEV - 1, D_MODEL, IN_SHARD), BF16),
        ),
        in_specs=[tile, tile] + _dproj_specs() + [pl.BlockSpec((1, D_MODEL), lambda i: (0, 0)), hbm, hbm],
        out_specs=(tile, pl.BlockSpec((1, D_MODEL), lambda i: (0, 0)), hbm),
        scratch_shapes=[
            pltpu.VMEM((D_MODEL, IN_WIDTH), BF16),
            pltpu.SemaphoreType.DMA((N_DEV,)),
            pltpu.SemaphoreType.DMA((N_DEV - 1,)),
            pltpu.SemaphoreType.DMA((N_DEV - 1,)),
        ],
        compiler_params=_params(("arbitrary",)),
    )(x2d, dx2, *dproj, g1, wg, dwin_g)


def _inproj_bwd_dw(x2d, dproj, g1):
    t = x2d.shape[0]
    n_tiles = t // TOKEN_TILE
    widths = [2 * POOL_WIDTH] + [ATTN_WIDTH] * 4

    def body(x_ref, dp_ref, dq_ref, dk_ref, dv_ref, dag_ref, g_ref, out_ref, acc_ref):
        i = pl.program_id(0)

        @pl.when(i == 0)
        def _():
            acc_ref[...] = jnp.zeros_like(acc_ref)

        xf = x_ref[...]
        r = lax.rsqrt(jnp.mean(xf * xf, axis=-1, keepdims=True) + EPS)
        h = ((xf * r) * g_ref[...]).astype(BF16)
        col = 0
        for ref, width in zip((dp_ref, dq_ref, dk_ref, dv_ref, dag_ref), widths):
            for c0 in range(0, width, POOL_WIDTH):
                acc_ref[:, col:col + POOL_WIDTH] += _tn(h, ref[:, c0:c0 + POOL_WIDTH])
                col += POOL_WIDTH

        @pl.when(i == n_tiles - 1)
        def _():
            for d in range(N_DEV):
                out_ref[d] = acc_ref[:, d * IN_SHARD:(d + 1) * IN_SHARD].astype(BF16)

    return pl.pallas_call(
        body,
        name="inproj_bwd_dw",
        grid=(n_tiles,),
        out_shape=jax.ShapeDtypeStruct((N_DEV, D_MODEL, IN_SHARD), BF16),
        in_specs=[pl.BlockSpec((TOKEN_TILE, D_MODEL), lambda i: (i, 0))] + _dproj_specs() + [
            pl.BlockSpec((1, D_MODEL), lambda i: (0, 0)),
        ],
        out_specs=pl.BlockSpec((N_DEV, D_MODEL, IN_SHARD), lambda i: (0, 0, 0)),
        scratch_shapes=[pltpu.VMEM((D_MODEL, IN_WIDTH), F32)],
        compiler_params=_params(("arbitrary",)),
    )(x2d, *dproj, g1)


def _adamw(w, g, m, v):
    m = ADAM_B1 * m + (1.0 - ADAM_B1) * g
    v = ADAM_B2 * v + (1.0 - ADAM_B2) * (g * g)
    m_hat = m / (1.0 - ADAM_B1 ** ADAM_STEP)
    v_hat = v / (1.0 - ADAM_B2 ** ADAM_STEP)
    delta = -ADAM_LR * (m_hat / (jnp.sqrt(v_hat) + ADAM_EPS) + ADAM_WD * w)
    return delta, m, v


_ROW_G1, _ROW_G2, _ROW_PS, _ROW_LOSS, _ROW_RB = 0, 1, 2, 3, 8
_VEC_ROWS = 16


def _reduce_adamw(dwin_g, land_in, dwout_g, land_out, grads, weights, big):
    small_shapes = [(1, D_MODEL), (1, D_MODEL), (1, POOL_WIDTH), (N_HEADS, 2 * LANES), (N_GROUPS, GROUP_DIM, GROUP_DIM)]

    def body(*refs):
        refs = list(refs)
        take = lambda n: [refs.pop(0) for _ in range(n)]
        dwin_hbm, lin_ref, dwout_hbm, lout_ref = take(4)
        g1_ref, g2_ref, ps_ref, rb_ref, pw_ref, loss_ref = take(6)
        small_wmv = [take(3) for _ in range(5)]
        big_wmv = [take(3) for _ in range(2)]
        big_out = [take(4) for _ in range(2)]
        small_out = [take(4) for _ in range(5)]
        (loss_out,) = take(1)
        own_in, own_out, vec_ref, rvec_ref, rpw_ref, send_sems, recv_sems, local_sems = refs

        me = _dev_index(_mesh_pos())
        vec_ref[...] = jnp.zeros_like(vec_ref)
        vec_ref[_ROW_G1:_ROW_G1 + 1, :] = g1_ref[...]
        vec_ref[_ROW_G2:_ROW_G2 + 1, :] = g2_ref[...]
        vec_ref[_ROW_PS:_ROW_PS + 1, :POOL_WIDTH] = ps_ref[...]
        vec_ref[_ROW_LOSS:_ROW_LOSS + 1, :LANES] = loss_ref[0:1, :]
        vec_ref[_ROW_RB:_ROW_RB + N_HEADS, :2 * LANES] = rb_ref[...]

        def send(r, src, land, k):
            return pltpu.make_async_remote_copy(
                src_ref=src, dst_ref=land.at[r], send_sem=send_sems.at[2 * (r - 1) + k],
                recv_sem=recv_sems.at[2 * (r - 1) + k], device_id=_peer(r), device_id_type=MESH_ID)

        started = [cp for r in range(1, N_DEV) for cp in (send(r, vec_ref, rvec_ref, 0), send(r, pw_ref, rpw_ref, 1))]
        for cp in started:
            cp.start()
        mine = [pltpu.make_async_copy(dwin_hbm.at[me], own_in, local_sems.at[0]),
                pltpu.make_async_copy(dwout_hbm.at[me], own_out, local_sems.at[1])]
        for cp in mine:
            cp.start()
        rvec_ref[0] = vec_ref[...]
        rpw_ref[0] = pw_ref[...]
        for cp in mine:
            cp.wait()

        def update(g, wmv, outs):
            delta, m_new, v_new = _adamw(wmv[0][...], g, wmv[1][...], wmv[2][...])
            for ref, val in zip(outs, (g, delta, m_new, v_new)):
                ref[...] = val

        g_in = own_in[...].astype(F32)
        g_out = own_out[...].astype(F32)
        for r in range(N_DEV - 1):
            g_in = g_in + lin_ref[r].astype(F32)
            g_out = g_out + lout_ref[r].astype(F32)
        update(g_in, big_wmv[0], big_out[0])
        update(g_out, big_wmv[1], big_out[1])

        for cp in started:
            cp.wait_recv()
        for cp in started:
            cp.wait_send()
        vec = rvec_ref[me]
        pw = rpw_ref[me]
        for s in range(1, N_DEV):
            vec = vec + rvec_ref[me ^ s]
            pw = pw + rpw_ref[me ^ s]
        vec_ref[...] = vec
        update(vec_ref[_ROW_G1:_ROW_G1 + 1, :], small_wmv[0], small_out[0])
        update(vec_ref[_ROW_G2:_ROW_G2 + 1, :], small_wmv[1], small_out[1])
        update(vec_ref[_ROW_PS:_ROW_PS + 1, :POOL_WIDTH], small_wmv[2], small_out[2])
        update(vec_ref[_ROW_RB:_ROW_RB + N_HEADS, :2 * LANES], small_wmv[3], small_out[3])
        update(pw, small_wmv[4], small_out[4])
        loss_out[...] = jnp.broadcast_to(vec_ref[_ROW_LOSS:_ROW_LOSS + 1, :LANES], loss_out.shape)

    vm = pl.BlockSpec(memory_space=pltpu.VMEM)
    hbm = pl.BlockSpec(memory_space=pl.ANY)
    f32 = lambda shape: jax.ShapeDtypeStruct(shape, F32)
    out_shapes = [f32((D_MODEL, IN_SHARD))] * 4 + [f32((OUT_SHARD, D_MODEL))] * 4
    for shape in small_shapes:
        out_shapes += [f32(shape)] * 4
    out_shapes.append(f32((8, LANES)))
    args = [dwin_g, land_in, dwout_g, land_out, *grads]
    for wmv in weights:
        args += list(wmv)
    for wmv in big:
        args += list(wmv)
    return pl.pallas_call(
        body,
        name="reduce_adamw",
        out_shape=tuple(out_shapes),
        in_specs=[hbm, vm, hbm, vm] + [vm] * (len(args) - 4),
        out_specs=tuple([vm] * len(out_shapes)),
        scratch_shapes=[
            pltpu.VMEM((D_MODEL, IN_SHARD), BF16),
            pltpu.VMEM((OUT_SHARD, D_MODEL), BF16),
            pltpu.VMEM((_VEC_ROWS, D_MODEL), F32),
            pltpu.VMEM((N_DEV, _VEC_ROWS, D_MODEL), F32),
            pltpu.VMEM((N_DEV, N_GROUPS, GROUP_DIM, GROUP_DIM), F32),
            pltpu.SemaphoreType.DMA((2 * (N_DEV - 1),)),
            pltpu.SemaphoreType.DMA((2 * (N_DEV - 1),)),
            pltpu.SemaphoreType.DMA((2,)),
        ],
        compiler_params=_params(),
    )(*args)


def kernel(x, norm_gain, w_in, pool_w, pool_scale, rel_bias, w_out, final_norm_gain, loss_target, m_norm_gain, m_w_in, m_pool_w, m_pool_scale, m_rel_bias, m_w_out, m_final_norm_gain, v_norm_gain, v_w_in, v_pool_w, v_pool_scale, v_rel_bias, v_w_out, v_final_norm_gain):
    t = x.shape[1]
    assert x.shape[0] == 1 and t % TOKEN_TILE == 0 and t // Q_BLOCK >= 4
    x2d = x[0]
    tgt2d = loss_target[0]
    g2 = final_norm_gain.reshape(1, D_MODEL)

    rb = rel_bias[0]
    rel_line = jnp.concatenate([
        jnp.broadcast_to(rb[:, :1], (N_HEADS, _REL_FIRST)), rb,
        jnp.broadcast_to(rb[:, N_REL - 1:], (N_HEADS, TOEPLITZ - _REL_FIRST - N_REL)),
    ], axis=1).reshape(N_HEADS, 1, TOEPLITZ)
    wg_in, bias_tile = _gather_weights(w_in[0], rel_line)

    pvg, qkv, ag = _norm_inproj(x2d, norm_gain, wg_in)
    y_pool = _pool_fwd(pvg, pool_w[0], pool_scale)
    o, y_attn, wg_out = _attn_fwd(qkv, ag, bias_tile, w_out[0])
    dx2, dy_pool, dy_attn, dwout_g, d_g2, loss_sum = _outproj_loss(x2d, tgt2d, y_pool, y_attn, wg_out, g2)
    d_pool, d_pw, d_ps, land_out = _pool_bwd(pvg, dy_pool, pool_w[0], pool_scale, dwout_g)
    dq, dk, dv, dag, ds_sum = _attn_bwd(qkv, ag, o, dy_attn, bias_tile)
    d_rb = _bias_grad(ds_sum)
    dproj = (d_pool, dq, dk, dv, dag)
    dwin_g = _inproj_bwd_dw(x2d, dproj, norm_gain)
    grad_x, d_g1, land_in = _inproj_bwd_dx(x2d, dx2, dproj, norm_gain, wg_in, dwin_g)

    pad_rb = lambda a: jnp.pad(a[0], ((0, 0), (0, 2 * LANES - N_REL)))
    row = lambda a: a.reshape(1, D_MODEL)
    weights = [
        (norm_gain, m_norm_gain, v_norm_gain),
        (row(final_norm_gain), row(m_final_norm_gain), row(v_final_norm_gain)),
        (pool_scale, m_pool_scale, v_pool_scale),
        (pad_rb(rel_bias), pad_rb(m_rel_bias), pad_rb(v_rel_bias)),
        (pool_w[0], m_pool_w[0], v_pool_w[0]),
    ]
    big = [(w_in[0], m_w_in[0], v_w_in[0]), (w_out[0], m_w_out[0], v_w_out[0])]
    res = _reduce_adamw(dwin_g, land_in, dwout_g, land_out, (d_g1, d_g2, d_ps, d_rb, d_pw, loss_sum), weights, big)
    loss = 0.5 * res[28][0, 0]

    def leaves(k):
        g1_, g2_, ps_, rb_, pw_ = (res[8 + 4 * leaf + k] for leaf in range(5))
        return [g1_, res[k][None], pw_[None], ps_, rb_[None, :, :N_REL], res[4 + k][None], g2_.reshape(D_MODEL)]

    return (loss, grad_x[None], *leaves(0), *leaves(1), *leaves(2), *leaves(3))
```

```python
import math

import jax
import jax.numpy as jnp
from jax import lax
from jax.experimental import pallas as pl
from jax.experimental.pallas import tpu as pltpu

F32 = jnp.float32
BF16 = jnp.bfloat16
MESH_ID = pl.DeviceIdType.MESH

D_MODEL = 1024
POOL_WIDTH = 512
ATTN_WIDTH = 512
POOL_WINDOWS = (2, 4, 8, 16)
N_GROUPS = 4
GROUP_DIM = 128
HEAD_DIM = 64
N_HEADS = 8
CHUNK = 64
LEFT_CHUNKS = 8
MAX_REL = 64
N_REL = 2 * MAX_REL + 1
IN_WIDTH = 2 * POOL_WIDTH + 4 * ATTN_WIDTH
EPS = 1e-6
MASK_VALUE = -1e30
ATTN_SCALE = 1.0 / math.sqrt(HEAD_DIM)
ADAM_LR = 0.001
ADAM_B1 = 0.9
ADAM_B2 = 0.999
ADAM_EPS = 1e-08
ADAM_WD = 0.01
ADAM_STEP = 10

N_DEV = 8
IN_SHARD = IN_WIDTH // N_DEV
OUT_SHARD = D_MODEL // N_DEV

LANES = 128
TOKEN_TILE = 512
HALO = 16
Q_BLOCK = 256
KV_BLOCKS = 3
KV_WINDOW = KV_BLOCKS * Q_BLOCK
PAIR = 2 * HEAD_DIM
N_PAIRS = N_HEADS // 2
TOEPLITZ = 1024
VMEM_LIMIT = 56 * 1024 * 1024


def _params(sem=None, vmem=VMEM_LIMIT):
    return pltpu.CompilerParams(dimension_semantics=sem, vmem_limit_bytes=vmem)


def _sigmoid(x):
    return 1.0 / (1.0 + jnp.exp(-x))


def _nt(a, b):
    return lax.dot_general(a, b, (((1,), (1,)), ((), ())), preferred_element_type=F32)


def _tn(a, b):
    return lax.dot_general(a, b, (((0,), (0,)), ((), ())), preferred_element_type=F32)


def _nn(a, b):
    return jnp.dot(a, b, preferred_element_type=F32)


def _mesh_pos():
    return lax.axis_index("x"), lax.axis_index("y"), lax.axis_index("c")


def _dev_index(p):
    return 4 * p[0] + 2 * p[1] + p[2]


def _peer(r):
    x, y, c = _mesh_pos()
    return (x ^ ((r >> 2) & 1), y ^ ((r >> 1) & 1), c ^ (r & 1))


def _exchange_copies(src_hbm, land_hbm, send_sems, recv_sems):
    return [
        pltpu.make_async_remote_copy(
            src_ref=src_hbm.at[_dev_index(_peer(r))], dst_ref=land_hbm.at[r - 1],
            send_sem=send_sems.at[r - 1], recv_sem=recv_sems.at[r - 1],
            device_id=_peer(r), device_id_type=MESH_ID)
        for r in range(1, N_DEV)
    ]


def _gather_weights(w_in_shard, rel_line):
    def body(win_ref, line_ref, gin_ref, bias_ref, sin_ref, send_sems, recv_sems):
        x, y, c = _mesh_pos()
        me, sibling = (x, y, c), (x, y, 1 - c)
        chips = [(1 - x, y), (x, 1 - y), (1 - x, 1 - y)]

        sin_ref[...] = win_ref[...].astype(BF16)
        gin_ref[_dev_index(me)] = sin_ref[...]

        def copy(k, block, to, from_shard=False):
            return pltpu.make_async_remote_copy(
                src_ref=sin_ref if from_shard else gin_ref.at[_dev_index(block)],
                dst_ref=gin_ref.at[_dev_index(block)],
                send_sem=send_sems.at[k],
                recv_sem=recv_sems.at[k],
                device_id=to,
                device_id_type=MESH_ID,
            )

        first = [copy(0, me, sibling, True)]
        first += [copy(1 + j, me, (*chip, c), True) for j, chip in enumerate(chips)]
        for cp in first:
            cp.start()
        passed = [copy(4 + j, (*chip, c), sibling) for j, chip in enumerate(chips)]

        def bias_heads(lo, hi):
            for h in range(lo, hi):
                bias_ref[h] = _toeplitz_bias(line_ref[h])

        bias_heads(0, N_HEADS - 3)
        for j, chip in enumerate(chips):
            copy(1 + j, (*chip, c), me).wait_recv()
            passed[j].start()
            bias_heads(N_HEADS - 3 + j, N_HEADS - 2 + j)
        copy(0, sibling, me).wait_recv()
        for j, chip in enumerate(chips):
            copy(4 + j, (*chip, 1 - c), me).wait_recv()
        for cp in first + passed:
            cp.wait_send()

    vm = pl.BlockSpec(memory_space=pltpu.VMEM)
    return pl.pallas_call(
        body,
        name="gather_weights",
        out_shape=(
            jax.ShapeDtypeStruct((N_DEV, D_MODEL, IN_SHARD), BF16),
            jax.ShapeDtypeStruct((N_HEADS, Q_BLOCK, KV_WINDOW), F32),
        ),
        in_specs=[vm, vm],
        out_specs=(vm, vm),
        scratch_shapes=[
            pltpu.VMEM((D_MODEL, IN_SHARD), BF16),
            pltpu.SemaphoreType.DMA((7,)),
            pltpu.SemaphoreType.DMA((7,)),
        ],
        compiler_params=_params(),
    )(w_in_shard, rel_line)


def _load_w_in(wg_hbm, wfull_ref, sem):
    copies = [
        pltpu.make_async_copy(wg_hbm.at[d], wfull_ref.at[:, d * IN_SHARD:(d + 1) * IN_SHARD], sem.at[d])
        for d in range(N_DEV)
    ]
    for cp in copies:
        cp.start()
    for cp in copies:
        cp.wait()


def _norm_inproj(x2d, g1, wg, pool_w, pool_scale):
    t = x2d.shape[0]

    def body(x_ref, g_ref, wg_hbm, pw_ref, ps_ref, pvg_ref, qkv_ref, ag_ref, yp_ref, wfull_ref, halo_ref, sem):
        i = pl.program_id(0)

        @pl.when(i == 0)
        def _():
            _load_w_in(wg_hbm, wfull_ref, sem)
            halo_ref[...] = jnp.zeros_like(halo_ref)

        xf = x_ref[...]
        r = lax.rsqrt(jnp.mean(xf * xf, axis=-1, keepdims=True) + EPS)
        h = ((xf * r) * g_ref[...]).astype(BF16)
        chunk = lambda ci: _nn(h, wfull_ref[:, ci * POOL_WIDTH:(ci + 1) * POOL_WIDTH])
        pv, pg = chunk(0), chunk(1)
        pvg_ref[:, :POOL_WIDTH] = pv
        pvg_ref[:, POOL_WIDTH:] = pg
        halo = halo_ref[...]
        halo_ref[...] = pv[TOKEN_TILE - HALO:]
        for gi in range(N_GROUPS):
            sl = slice(gi * GROUP_DIM, (gi + 1) * GROUP_DIM)
            d = _pool_diffs(pv, halo, i * TOKEN_TILE, gi)
            z = _nn(d.astype(BF16), pw_ref[gi].astype(BF16))
            g = pg[:, sl]
            yp_ref[:, sl] = ((z * ps_ref[:, sl]) * (g * _sigmoid(g))).astype(BF16)
        for ci in range(2, 5):
            qkv_ref[:, (ci - 2) * POOL_WIDTH:(ci - 1) * POOL_WIDTH] = chunk(ci).astype(BF16)
        ag_ref[...] = chunk(5)

    tile = lambda width: pl.BlockSpec((TOKEN_TILE, width), lambda i: (i, 0))
    return pl.pallas_call(
        body,
        name="norm_inproj",
        grid=(t // TOKEN_TILE,),
        out_shape=(
            jax.ShapeDtypeStruct((t, 2 * POOL_WIDTH), F32),
            jax.ShapeDtypeStruct((t, 3 * ATTN_WIDTH), BF16),
            jax.ShapeDtypeStruct((t, ATTN_WIDTH), F32),
            jax.ShapeDtypeStruct((t, POOL_WIDTH), BF16),
        ),
        in_specs=[
            tile(D_MODEL),
            pl.BlockSpec((1, D_MODEL), lambda i: (0, 0)),
            pl.BlockSpec(memory_space=pl.ANY),
            pl.BlockSpec((N_GROUPS, GROUP_DIM, GROUP_DIM), lambda i: (0, 0, 0)),
            pl.BlockSpec((1, POOL_WIDTH), lambda i: (0, 0)),
        ],
        out_specs=(tile(2 * POOL_WIDTH), tile(3 * ATTN_WIDTH), tile(ATTN_WIDTH), tile(POOL_WIDTH)),
        scratch_shapes=[
            pltpu.VMEM((D_MODEL, IN_WIDTH), BF16),
            pltpu.VMEM((HALO, POOL_WIDTH), F32),
            pltpu.SemaphoreType.DMA((N_DEV,)),
        ],
        compiler_params=_params(("arbitrary",)),
    )(x2d, g1, wg, pool_w, pool_scale)


def _inv_count(first_row, rows, window):
    tpos = first_row + lax.broadcasted_iota(jnp.int32, (rows, 1), 0)
    return 1.0 / jnp.minimum(tpos + 1, window).astype(F32)


def _causal_window_sum(ext, window):
    s, k = ext, 1
    while k < window:
        s = s + pltpu.roll(s, k, 0)
        k *= 2
    return s


def _pool_diffs(pv, halo, first_row, gi):
    w = POOL_WINDOWS[gi]
    sl = slice(gi * GROUP_DIM, (gi + 1) * GROUP_DIM)
    ext = jnp.concatenate([halo[:, sl], pv[:, sl]], axis=0)
    s = _causal_window_sum(ext, w)[HALO:]
    return s * _inv_count(first_row, pv.shape[0], w) - pv[:, sl]


def _pool_bwd_tile(i, n_tiles, cur_ref, prev_ref, pgn_ref, dy_ref, dyn_ref, pw_ref, ps_ref, dp_ref, dpw_ref, dps_ref):
    pv = cur_ref[:, :POOL_WIDTH]
    pg = cur_ref[:, POOL_WIDTH:]
    prev = jnp.where(i > 0, prev_ref[...], 0.0)
    has_next = i < n_tiles - 1
    rows = TOKEN_TILE + HALO
    for gi in range(N_GROUPS):
        w = POOL_WINDOWS[gi]
        sl = slice(gi * GROUP_DIM, (gi + 1) * GROUP_DIM)
        pw = pw_ref[gi].astype(BF16)
        ps = ps_ref[:, sl]
        d = _pool_diffs(pv, prev, i * TOKEN_TILE, gi).astype(BF16)
        z = _nn(d, pw)
        g_ext = jnp.concatenate([pg[:, sl], pgn_ref[:, sl]], axis=0)
        dy_ext = jnp.concatenate([dy_ref[:, sl], dyn_ref[:, sl]], axis=0).astype(F32)
        sig = _sigmoid(g_ext)
        gate = g_ext * sig
        dz_ext = ((dy_ext * gate) * ps).astype(BF16)
        dd_ext = _nt(dz_ext, pw)
        e = dd_ext * _inv_count(i * TOKEN_TILE, rows, w)
        row = lax.broadcasted_iota(jnp.int32, (rows, 1), 0)
        e = jnp.where(jnp.logical_or(row < TOKEN_TILE, has_next), e, 0.0)
        s, k = e, 1
        while k < w:
            s = s + pltpu.roll(s, rows - k, 0)
            k *= 2
        dp_ref[:, sl] = (s[:TOKEN_TILE] - dd_ext[:TOKEN_TILE]).astype(BF16)
        dy = dy_ext[:TOKEN_TILE]
        g = g_ext[:TOKEN_TILE]
        sg = sig[:TOKEN_TILE]
        dgate = sg * (1.0 + g * (1.0 - sg))
        dp_ref[:, POOL_WIDTH + gi * GROUP_DIM:POOL_WIDTH + (gi + 1) * GROUP_DIM] = (
            (dy * (z * ps)) * dgate).astype(BF16)
        dps_ref[:, sl] += jnp.sum((dy * gate[:TOKEN_TILE]) * z, axis=0, keepdims=True)
        dpw_ref[gi] += _tn(d, dz_ext[:TOKEN_TILE])


_REL_FIRST = KV_WINDOW - 1 - MAX_REL
_DIAG_FIRST = _REL_FIRST - (Q_BLOCK - 1)


def _skew_rows(a, right):
    row = lax.broadcasted_iota(jnp.int32, a.shape, 0)
    for b in range(8):
        shift = (1 << b) if right else TOEPLITZ - (1 << b)
        a = jnp.where((row >> b) & 1 == 1, pltpu.roll(a, shift, 1), a)
    return a


def _toeplitz_bias(line):
    a = jnp.broadcast_to(line, (Q_BLOCK, TOEPLITZ))
    a = _skew_rows(a, True)
    a = pltpu.roll(a, TOEPLITZ - (Q_BLOCK - 1), 1)
    a = a[:, :KV_WINDOW]
    qc = lax.broadcasted_iota(jnp.int32, a.shape, 0) // CHUNK
    kc = lax.broadcasted_iota(jnp.int32, a.shape, 1) // CHUNK
    visible = jnp.logical_and(kc >= qc, kc <= qc + LEFT_CHUNKS)
    return jnp.where(visible, a, MASK_VALUE)


def _bias_grad(ds_sum):
    def body(ds_ref, out_ref):
        for h in range(N_HEADS):
            a = jnp.concatenate([ds_ref[h], jnp.zeros((Q_BLOCK, TOEPLITZ - KV_WINDOW), F32)], axis=1)
            a = _skew_rows(a, False)
            diag = jnp.sum(a, axis=0, keepdims=True)
            c = lax.broadcasted_iota(jnp.int32, diag.shape, 1)
            far = jnp.logical_or(c <= _DIAG_FIRST, c > KV_WINDOW)
            near = jnp.logical_and(c >= _DIAG_FIRST + 2 * MAX_REL, c <= KV_WINDOW)
            low = jnp.sum(jnp.where(far, diag, 0.0), axis=1, keepdims=True)
            high = jnp.sum(jnp.where(near, diag, 0.0), axis=1, keepdims=True)
            line = pltpu.roll(diag, TOEPLITZ - _DIAG_FIRST, 1)[:, :2 * LANES]
            r = lax.broadcasted_iota(jnp.int32, line.shape, 1)
            line = jnp.where(r == 0, low, jnp.where(r == 2 * MAX_REL, high, jnp.where(r < 2 * MAX_REL, line, 0.0)))
            out_ref[h:h + 1, :] = line

    vm = pl.BlockSpec(memory_space=pltpu.VMEM)
    return pl.pallas_call(
        body,
        name="bias_grad",
        out_shape=jax.ShapeDtypeStruct((N_HEADS, 2 * LANES), F32),
        in_specs=[vm],
        out_specs=vm,
        compiler_params=_params(),
    )(ds_sum)


def _head_lanes(hh):
    lane = lax.broadcasted_iota(jnp.int32, (1, PAIR), 1)
    return (lane < HEAD_DIM) if hh == 0 else (lane >= HEAD_DIM)


def _block_rows(first_block, n_blocks):
    if isinstance(first_block, int):
        return pl.ds(first_block * Q_BLOCK, n_blocks * Q_BLOCK)
    return pl.ds(pl.multiple_of(first_block * Q_BLOCK, Q_BLOCK), n_blocks * Q_BLOCK)


def _attn_fwd(qkv, ag, bias_tile, w_out_shard):
    t = qkv.shape[0]
    nb = t // Q_BLOCK

    def body(q_ref, k_ref, v_ref, ag_ref, bias_ref, wout_ref, o_ref, y_ref, gout_hbm,
             s_scr, p_scr, linv_scr, sout_ref, send_sems, recv_sems, local_sem):
        pair = pl.program_id(0)
        me = _dev_index(_mesh_pos())
        mine = pltpu.make_async_copy(sout_ref, gout_hbm.at[me], local_sem)

        def shard_copy(r, block):
            return pltpu.make_async_remote_copy(
                src_ref=sout_ref, dst_ref=gout_hbm.at[block], send_sem=send_sems.at[r - 1],
                recv_sem=recv_sems.at[r - 1], device_id=_peer(r), device_id_type=MESH_ID)

        @pl.when(pair == 0)
        def _():
            sout_ref[...] = wout_ref[...].astype(BF16)
            mine.start()
            for r in range(1, N_DEV):
                shard_copy(r, me).start()

        def scores(i, slot, nwin):
            q = q_ref[_block_rows(i, 1), :]
            kw = k_ref[_block_rows(i + 1 - nwin, nwin), :]
            for hh in range(2):
                q_h = jnp.where(_head_lanes(hh), q, jnp.zeros_like(q)) * ATTN_SCALE
                s_scr[slot, hh, :, :nwin * Q_BLOCK] = _nt(q_h, kw)

        def softmax(slot, nwin):
            for hh in range(2):
                s = s_scr[slot, hh, :, :nwin * Q_BLOCK] + bias_ref[hh, :, (KV_BLOCKS - nwin) * Q_BLOCK:]
                e = jnp.exp(s - jnp.max(s, axis=-1, keepdims=True))
                linv_scr[slot, hh] = 1.0 / jnp.sum(e, axis=-1, keepdims=True)
                p_scr[slot, hh, :, :nwin * Q_BLOCK] = e.astype(BF16)

        def output(i, slot, nwin):
            vw = v_ref[_block_rows(i + 1 - nwin, nwin), :]
            outs = [_nn(p_scr[slot, hh, :, :nwin * Q_BLOCK], vw) * linv_scr[slot, hh] for hh in range(2)]
            o = jnp.where(_head_lanes(0), outs[0], outs[1])
            g = ag_ref[_block_rows(i, 1), :]
            o_ref[_block_rows(i, 1), :] = o.astype(BF16)
            y_ref[_block_rows(i, 1), :] = (o * (g * _sigmoid(g))).astype(BF16)

        scores(0, 0, 1)
        scores(1, 1, 2)
        softmax(0, 1)
        scores(2, 0, 3)
        softmax(1, 2)
        output(0, 0, 1)
        scores(3, 1, 3)
        softmax(0, 3)
        output(1, 1, 2)

        def two_steps(k, carry):
            i = 3 + 2 * k
            scores(i + 1, 0, KV_BLOCKS)
            softmax(1, KV_BLOCKS)
            output(i - 1, 0, KV_BLOCKS)
            scores(i + 2, 1, KV_BLOCKS)
            softmax(0, KV_BLOCKS)
            output(i, 1, KV_BLOCKS)
            return carry

        lax.fori_loop(0, (nb - 4) // 2, two_steps, 0)
        last = (nb - 1) % 2
        softmax(last, KV_BLOCKS)
        output(nb - 2, 1 - last, KV_BLOCKS)
        output(nb - 1, last, KV_BLOCKS)

        @pl.when(pair == N_PAIRS - 1)
        def _():
            for r in range(1, N_DEV):
                shard_copy(r, _dev_index(_peer(r))).wait_recv()
            for r in range(1, N_DEV):
                shard_copy(r, me).wait_send()
            mine.wait()

    col = lambda c0: pl.BlockSpec((t, PAIR), lambda j: (0, c0 + j))
    return pl.pallas_call(
        body,
        name="attn_fwd",
        grid=(N_PAIRS,),
        out_shape=(
            jax.ShapeDtypeStruct((t, ATTN_WIDTH), BF16),
            jax.ShapeDtypeStruct((t, ATTN_WIDTH), BF16),
            jax.ShapeDtypeStruct((N_DEV, OUT_SHARD, D_MODEL), BF16),
        ),
        in_specs=[col(0), col(N_PAIRS), col(2 * N_PAIRS), col(0),
                  pl.BlockSpec((2, Q_BLOCK, KV_WINDOW), lambda j: (j, 0, 0)),
                  pl.BlockSpec((OUT_SHARD, D_MODEL), lambda j: (0, 0))],
        out_specs=(col(0), col(0), pl.BlockSpec(memory_space=pl.ANY)),
        scratch_shapes=[
            pltpu.VMEM((2, 2, Q_BLOCK, KV_WINDOW), F32),
            pltpu.VMEM((2, 2, Q_BLOCK, KV_WINDOW), BF16),
            pltpu.VMEM((2, 2, Q_BLOCK, 1), F32),
            pltpu.VMEM((OUT_SHARD, D_MODEL), BF16),
            pltpu.SemaphoreType.DMA((N_DEV - 1,)),
            pltpu.SemaphoreType.DMA((N_DEV - 1,)),
            pltpu.SemaphoreType.DMA,
        ],
        compiler_params=_params(("arbitrary",)),
    )(qkv, qkv, qkv, ag, bias_tile, w_out_shard)


def _attn_bwd(qkv, ag, o, dy_attn, bias_tile, dwout_g):
    t = qkv.shape[0]
    nb = t // Q_BLOCK

    def body(q_ref, k_ref, v_ref, ag_ref, o_ref, dy_ref, bias_ref, dwout_hbm,
             dq_ref, dk_ref, dv_ref, dag_ref, ds_ref, land_hbm,
             do_scr, s_scr, dp_scr, p_scr, dsb_scr, dq_scr, dk_acc, dv_acc, send_sems, recv_sems):
        exchange = _exchange_copies(dwout_hbm, land_hbm, send_sems, recv_sems)

        @pl.when(pl.program_id(0) == 0)
        def _():
            for cp in exchange:
                cp.start()

        def gates(i, carry):
            rows = _block_rows(i, 1)
            g = ag_ref[rows, :]
            sig = _sigmoid(g)
            dy = dy_ref[rows, :].astype(F32)
            do_scr[rows, :] = (dy * (g * sig)).astype(BF16)
            dag_ref[rows, :] = ((dy * o_ref[rows, :].astype(F32)) * (sig * (1.0 + g * (1.0 - sig)))).astype(BF16)
            return carry

        lax.fori_loop(0, nb, gates, 0)
        ds_ref[...] = jnp.zeros_like(ds_ref)

        def nwin_of(i):
            return min(i + 1, KV_BLOCKS) if isinstance(i, int) else KV_BLOCKS

        def operands(i, hh):
            lanes = _head_lanes(hh)
            q = q_ref[_block_rows(i, 1), :]
            do = do_scr[_block_rows(i, 1), :]
            return (jnp.where(lanes, q, jnp.zeros_like(q)) * ATTN_SCALE, jnp.where(lanes, do, jnp.zeros_like(do)))

        def products(i, hh):
            nwin = nwin_of(i)
            win = _block_rows(i + 1 - nwin, nwin)
            q_h, do_h = operands(i, hh)
            s_scr[hh, :, :nwin * Q_BLOCK] = _nt(q_h, k_ref[win, :])
            dp_scr[hh, :, :nwin * Q_BLOCK] = _nt(do_h, v_ref[win, :])

        def grads(i, hh):
            nwin = nwin_of(i)
            w, off = nwin * Q_BLOCK, (KV_BLOCKS - nwin) * Q_BLOCK
            s = s_scr[hh, :, :w] + bias_ref[hh, :, off:]
            e = jnp.exp(s - jnp.max(s, axis=-1, keepdims=True))
            p = e * (1.0 / jnp.sum(e, axis=-1, keepdims=True))
            dp = dp_scr[hh, :, :w]
            ds = p * (dp - jnp.sum(p * dp, axis=-1, keepdims=True))
            ds_ref[hh, :, off:] += ds
            p_scr[hh, :, :w] = p.astype(BF16)
            dsb_scr[hh, :, :w] = ds.astype(BF16)

        def ring(block):
            return block % KV_BLOCKS if isinstance(block, int) else lax.rem(block, KV_BLOCKS)

        def accumulate(i, hh):
            nwin = nwin_of(i)
            w = nwin * Q_BLOCK
            win = _block_rows(i + 1 - nwin, nwin)
            q_h, do_h = operands(i, hh)
            ds_b = dsb_scr[hh, :, :w]
            dq_h = _nn(ds_b, k_ref[win, :]) * ATTN_SCALE
            dkw = _tn(ds_b, q_h)
            dvw = _tn(p_scr[hh, :, :w], do_h)
            for b in range(nwin):
                slot = ring(i + 1 - nwin + b)
                part = slice(b * Q_BLOCK, (b + 1) * Q_BLOCK)
                if hh == 0 and b == nwin - 1:
                    dk_acc[slot] = dkw[part]
                    dv_acc[slot] = dvw[part]
                else:
                    dk_acc[slot] += dkw[part]
                    dv_acc[slot] += dvw[part]
            if hh == 0:
                dq_scr[...] = dq_h
            else:
                dq_ref[_block_rows(i, 1), :] = jnp.where(_head_lanes(0), dq_scr[...], dq_h).astype(BF16)
                if not (isinstance(i, int) and i < KV_BLOCKS - 1):
                    flush(i - (KV_BLOCKS - 1))

        def flush(block):
            dk_ref[_block_rows(block, 1), :] = dk_acc[ring(block)].astype(BF16)
            dv_ref[_block_rows(block, 1), :] = dv_acc[ring(block)].astype(BF16)

        def tile(n):
            return n // 2, n % 2

        def step(n):
            if n + 1 < 2 * nb:
                products(*tile(n + 1))
            grads(*tile(n))
            if n >= 1:
                accumulate(*tile(n - 1))

        products(0, 0)
        for n in range(2 * KV_BLOCKS):
            step(n)

        def two_steps(i, carry):
            products(i, 1)
            grads(i, 0)
            accumulate(i - 1, 1)
            products(i + 1, 0)
            grads(i, 1)
            accumulate(i, 0)
            return carry

        lax.fori_loop(KV_BLOCKS, nb - 1, two_steps, 0)
        step(2 * nb - 2)
        step(2 * nb - 1)
        accumulate(nb - 1, 1)
        flush(nb - 2)
        flush(nb - 1)

        @pl.when(pl.program_id(0) == N_PAIRS - 1)
        def _():
            for cp in exchange:
                cp.wait_recv()
            for cp in exchange:
                cp.wait_send()

    col = lambda c0: pl.BlockSpec((t, PAIR), lambda j: (0, c0 + j))
    once = pl.BlockSpec((t, PAIR), lambda j: (0, j), pipeline_mode=pl.Buffered(1))
    tile_spec = pl.BlockSpec((2, Q_BLOCK, KV_WINDOW), lambda j: (j, 0, 0))
    out = jax.ShapeDtypeStruct((t, ATTN_WIDTH), BF16)
    return pl.pallas_call(
        body,
        name="attn_bwd",
        grid=(N_PAIRS,),
        out_shape=(out, out, out, out, jax.ShapeDtypeStruct((N_HEADS, Q_BLOCK, KV_WINDOW), F32),
                   jax.ShapeDtypeStruct((N_DEV - 1, OUT_SHARD, D_MODEL), BF16)),
        in_specs=[col(0), col(N_PAIRS), col(2 * N_PAIRS), once, col(0), col(0), tile_spec,
                  pl.BlockSpec(memory_space=pl.ANY)],
        out_specs=(col(0), col(0), col(0), col(0), tile_spec, pl.BlockSpec(memory_space=pl.ANY)),
        scratch_shapes=[
            pltpu.VMEM((t, PAIR), BF16),
            pltpu.VMEM((2, Q_BLOCK, KV_WINDOW), F32),
            pltpu.VMEM((2, Q_BLOCK, KV_WINDOW), F32),
            pltpu.VMEM((2, Q_BLOCK, KV_WINDOW), BF16),
            pltpu.VMEM((2, Q_BLOCK, KV_WINDOW), BF16),
            pltpu.VMEM((Q_BLOCK, PAIR), F32),
            pltpu.VMEM((KV_BLOCKS, Q_BLOCK, PAIR), F32),
            pltpu.VMEM((KV_BLOCKS, Q_BLOCK, PAIR), F32),
            pltpu.SemaphoreType.DMA((N_DEV - 1,)),
            pltpu.SemaphoreType.DMA((N_DEV - 1,)),
        ],
        compiler_params=_params(("arbitrary",), vmem=60 * 1024 * 1024),
    )(qkv, qkv, qkv, ag, o, dy_attn, bias_tile, dwout_g)


def _outproj_loss(x2d, tgt2d, y_pool, y_attn, wout_g, g2):
    t = x2d.shape[0]
    n_tiles = t // TOKEN_TILE

    def body(x_ref, tgt_ref, yp_ref, ya_ref, w_ref, g_ref,
             dx2_ref, dyp_ref, dya_ref, dw_ref, dg_ref, loss_ref, acc_ref):
        i = pl.program_id(0)

        @pl.when(i == 0)
        def _():
            acc_ref[...] = jnp.zeros_like(acc_ref)
            dg_ref[...] = jnp.zeros_like(dg_ref)
            loss_ref[...] = jnp.zeros_like(loss_ref)

        w = w_ref[...].reshape(D_MODEL, D_MODEL)
        y = jnp.concatenate([yp_ref[...], ya_ref[...]], axis=1)
        x2 = x_ref[...] + _nn(y, w)
        r = lax.rsqrt(jnp.mean(x2 * x2, axis=-1, keepdims=True) + EPS)
        xh = x2 * r
        g = g_ref[...]
        diff = xh * g - tgt_ref[...]
        tok = jnp.sum(diff * diff, axis=-1, keepdims=True) * (1.0 / D_MODEL)
        loss_ref[...] += jnp.sum(tok, axis=0, keepdims=True)
        dout = diff * (1.0 / D_MODEL)
        dg_ref[...] += jnp.sum(dout * xh, axis=0, keepdims=True)
        u = dout * g
        dx2 = r * (u - xh * jnp.mean(u * xh, axis=-1, keepdims=True))
        dx2_ref[...] = dx2
        dx2_b = dx2.astype(BF16)
        dy = _nt(dx2_b, w)
        dyp_ref[...] = dy[:, :POOL_WIDTH].astype(BF16)
        dya_ref[...] = dy[:, POOL_WIDTH:].astype(BF16)
        acc_ref[...] += _tn(y, dx2_b)

        @pl.when(i == n_tiles - 1)
        def _():
            dw_ref[...] = acc_ref[...].reshape(N_DEV, OUT_SHARD, D_MODEL).astype(BF16)

    tile = lambda width: pl.BlockSpec((TOKEN_TILE, width), lambda i: (i, 0))
    return pl.pallas_call(
        body,
        name="outproj_loss",
        grid=(n_tiles,),
        out_shape=(
            jax.ShapeDtypeStruct((t, D_MODEL), F32),
            jax.ShapeDtypeStruct((t, POOL_WIDTH), BF16),
            jax.ShapeDtypeStruct((t, ATTN_WIDTH), BF16),
            jax.ShapeDtypeStruct((N_DEV, OUT_SHARD, D_MODEL), BF16),
            jax.ShapeDtypeStruct((1, D_MODEL), F32),
            jax.ShapeDtypeStruct((8, LANES), F32),
        ),
        in_specs=[
            tile(D_MODEL), tile(D_MODEL), tile(POOL_WIDTH), tile(ATTN_WIDTH),
            pl.BlockSpec((N_DEV, OUT_SHARD, D_MODEL), lambda i: (0, 0, 0)),
            pl.BlockSpec((1, D_MODEL), lambda i: (0, 0)),
        ],
        out_specs=(
            tile(D_MODEL), tile(POOL_WIDTH), tile(ATTN_WIDTH),
            pl.BlockSpec((N_DEV, OUT_SHARD, D_MODEL), lambda i: (0, 0, 0)),
            pl.BlockSpec((1, D_MODEL), lambda i: (0, 0)),
            pl.BlockSpec((8, LANES), lambda i: (0, 0)),
        ),
        scratch_shapes=[pltpu.VMEM((D_MODEL, D_MODEL), F32)],
        compiler_params=_params(("arbitrary",)),
    )(x2d, tgt2d, y_pool, y_attn, wout_g, g2)


def _dproj_specs():
    tile = lambda width: pl.BlockSpec((TOKEN_TILE, width), lambda i: (i, 0))
    return [tile(2 * POOL_WIDTH)] + [tile(ATTN_WIDTH)] * 4


def _inproj_bwd_dx(x2d, dx2, dproj, g1, wg, dwin_g):
    t = x2d.shape[0]
    n_tiles = t // TOKEN_TILE

    def body(x_ref, dx2_ref, dp_ref, dq_ref, dk_ref, dv_ref, dag_ref, g_ref, wg_hbm, dwin_hbm,
             gx_ref, dg_ref, land_hbm, wfull_ref, sem, send_sems, recv_sems):
        i = pl.program_id(0)
        exchange = _exchange_copies(dwin_hbm, land_hbm, send_sems, recv_sems)

        @pl.when(i == 0)
        def _():
            for cp in exchange:
                cp.start()
            _load_w_in(wg_hbm, wfull_ref, sem)
            dg_ref[...] = jnp.zeros_like(dg_ref)

        dproj_t = jnp.concatenate([dp_ref[...], dq_ref[...], dk_ref[...], dv_ref[...], dag_ref[...]], axis=1)
        dh = _nt(dproj_t, wfull_ref[...])
        xf = x_ref[...]
        r = lax.rsqrt(jnp.mean(xf * xf, axis=-1, keepdims=True) + EPS)
        xh = xf * r
        dg_ref[...] += jnp.sum(dh * xh, axis=0, keepdims=True)
        u = dh * g_ref[...]
        gx_ref[...] = dx2_ref[...] + r * (u - xh * jnp.mean(u * xh, axis=-1, keepdims=True))

        @pl.when(i == n_tiles - 1)
        def _():
            for cp in exchange:
                cp.wait_recv()
            for cp in exchange:
                cp.wait_send()

    tile = pl.BlockSpec((TOKEN_TILE, D_MODEL), lambda i: (i, 0))
    hbm = pl.BlockSpec(memory_space=pl.ANY)
    return pl.pallas_call(
        body,
        name="inproj_bwd_dx",
        grid=(n_tiles,),
        out_shape=(
            jax.ShapeDtypeStruct((t, D_MODEL), F32),
            jax.ShapeDtypeStruct((1, D_MODEL), F32),
            jax.ShapeDtypeStruct((N_DEV - 1, D_MODEL, IN_SHARD), BF16),
        ),
        in_specs=[tile, tile] + _dproj_specs() + [pl.BlockSpec((1, D_MODEL), lambda i: (0, 0)), hbm, hbm],
        out_specs=(tile, pl.BlockSpec((1, D_MODEL), lambda i: (0, 0)), hbm),
        scratch_shapes=[
            pltpu.VMEM((D_MODEL, IN_WIDTH), BF16),
            pltpu.SemaphoreType.DMA((N_DEV,)),
            pltpu.SemaphoreType.DMA((N_DEV - 1,)),
            pltpu.SemaphoreType.DMA((N_DEV - 1,)),
        ],
        compiler_params=_params(("arbitrary",)),
    )(x2d, dx2, *dproj, g1, wg, dwin_g)


def _inproj_bwd_dw(x2d, pvg, dy_pool, dattn, g1, pool_w, pool_scale):
    t = x2d.shape[0]
    n_tiles = t // TOKEN_TILE
    halo_per_tile = TOKEN_TILE // HALO
    last_halo = t // HALO - 1

    def body(x_ref, cur_ref, prev_ref, pgn_ref, dy_ref, dyn_ref, dq_ref, dk_ref, dv_ref, dag_ref, g_ref, pw_ref, ps_ref,
             out_ref, dp_ref, dpw_ref, dps_ref, acc_ref):
        i = pl.program_id(0)

        @pl.when(i == 0)
        def _():
            acc_ref[...] = jnp.zeros_like(acc_ref)
            dpw_ref[...] = jnp.zeros_like(dpw_ref)
            dps_ref[...] = jnp.zeros_like(dps_ref)

        xf = x_ref[...]
        r = lax.rsqrt(jnp.mean(xf * xf, axis=-1, keepdims=True) + EPS)
        h = ((xf * r) * g_ref[...]).astype(BF16)
        col = 2 * POOL_WIDTH
        for ref in (dq_ref, dk_ref, dv_ref, dag_ref):
            acc_ref[:, col:col + ATTN_WIDTH] += _tn(h, ref[...])
            col += ATTN_WIDTH
        _pool_bwd_tile(i, n_tiles, cur_ref, prev_ref, pgn_ref, dy_ref, dyn_ref, pw_ref, ps_ref, dp_ref, dpw_ref, dps_ref)
        for c0 in (0, POOL_WIDTH):
            acc_ref[:, c0:c0 + POOL_WIDTH] += _tn(h, dp_ref[:, c0:c0 + POOL_WIDTH])

        @pl.when(i == n_tiles - 1)
        def _():
            for d in range(N_DEV):
                out_ref[d] = acc_ref[:, d * IN_SHARD:(d + 1) * IN_SHARD].astype(BF16)

    tile = lambda width: pl.BlockSpec((TOKEN_TILE, width), lambda i: (i, 0))
    next_halo = lambda col: pl.BlockSpec(
        (HALO, POOL_WIDTH), lambda i: (jnp.minimum((i + 1) * halo_per_tile, last_halo), col))
    return pl.pallas_call(
        body,
        name="inproj_bwd_dw",
        grid=(n_tiles,),
        out_shape=(
            jax.ShapeDtypeStruct((N_DEV, D_MODEL, IN_SHARD), BF16),
            jax.ShapeDtypeStruct((t, 2 * POOL_WIDTH), BF16),
            jax.ShapeDtypeStruct((N_GROUPS, GROUP_DIM, GROUP_DIM), F32),
            jax.ShapeDtypeStruct((1, POOL_WIDTH), F32),
        ),
        in_specs=[
            tile(D_MODEL),
            tile(2 * POOL_WIDTH),
            pl.BlockSpec((HALO, POOL_WIDTH), lambda i: (jnp.maximum(i * halo_per_tile - 1, 0), 0)),
            next_halo(1),
            tile(POOL_WIDTH),
            next_halo(0),
            tile(ATTN_WIDTH), tile(ATTN_WIDTH), tile(ATTN_WIDTH), tile(ATTN_WIDTH),
            pl.BlockSpec((1, D_MODEL), lambda i: (0, 0)),
            pl.BlockSpec((N_GROUPS, GROUP_DIM, GROUP_DIM), lambda i: (0, 0, 0)),
            pl.BlockSpec((1, POOL_WIDTH), lambda i: (0, 0)),
        ],
        out_specs=(
            pl.BlockSpec((N_DEV, D_MODEL, IN_SHARD), lambda i: (0, 0, 0)),
            tile(2 * POOL_WIDTH),
            pl.BlockSpec((N_GROUPS, GROUP_DIM, GROUP_DIM), lambda i: (0, 0, 0)),
            pl.BlockSpec((1, POOL_WIDTH), lambda i: (0, 0)),
        ),
        scratch_shapes=[pltpu.VMEM((D_MODEL, IN_WIDTH), F32)],
        compiler_params=_params(("arbitrary",)),
    )(x2d, pvg, pvg, pvg, dy_pool, dy_pool, *dattn, g1, pool_w, pool_scale)


def _adamw(w, g, m, v):
    m = ADAM_B1 * m + (1.0 - ADAM_B1) * g
    v = ADAM_B2 * v + (1.0 - ADAM_B2) * (g * g)
    m_hat = m / (1.0 - ADAM_B1 ** ADAM_STEP)
    v_hat = v / (1.0 - ADAM_B2 ** ADAM_STEP)
    delta = -ADAM_LR * (m_hat / (jnp.sqrt(v_hat) + ADAM_EPS) + ADAM_WD * w)
    return delta, m, v


_ROW_G1, _ROW_G2, _ROW_PS, _ROW_LOSS, _ROW_RB = 0, 1, 2, 3, 8
_VEC_ROWS = 16


def _reduce_adamw(dwin_g, land_in, dwout_g, land_out, grads, weights, big):
    small_shapes = [(1, D_MODEL), (1, D_MODEL), (1, POOL_WIDTH), (N_HEADS, 2 * LANES), (N_GROUPS, GROUP_DIM, GROUP_DIM)]

    def body(*refs):
        refs = list(refs)
        take = lambda n: [refs.pop(0) for _ in range(n)]
        dwin_hbm, lin_ref, dwout_hbm, lout_ref = take(4)
        g1_ref, g2_ref, ps_ref, rb_ref, pw_ref, loss_ref = take(6)
        small_wmv = [take(3) for _ in range(5)]
        big_wmv = [take(3) for _ in range(2)]
        big_out = [take(4) for _ in range(2)]
        small_out = [take(4) for _ in range(5)]
        (loss_out,) = take(1)
        own_in, own_out, vec_ref, rvec_ref, rpw_ref, send_sems, recv_sems, local_sems = refs

        me = _dev_index(_mesh_pos())
        vec_ref[...] = jnp.zeros_like(vec_ref)
        vec_ref[_ROW_G1:_ROW_G1 + 1, :] = g1_ref[...]
        vec_ref[_ROW_G2:_ROW_G2 + 1, :] = g2_ref[...]
        vec_ref[_ROW_PS:_ROW_PS + 1, :POOL_WIDTH] = ps_ref[...]
        vec_ref[_ROW_LOSS:_ROW_LOSS + 1, :LANES] = loss_ref[0:1, :]
        vec_ref[_ROW_RB:_ROW_RB + N_HEADS, :2 * LANES] = rb_ref[...]

        def send(r, src, land, k):
            return pltpu.make_async_remote_copy(
                src_ref=src, dst_ref=land.at[r], send_sem=send_sems.at[2 * (r - 1) + k],
                recv_sem=recv_sems.at[2 * (r - 1) + k], device_id=_peer(r), device_id_type=MESH_ID)

        started = [cp for r in range(1, N_DEV) for cp in (send(r, vec_ref, rvec_ref, 0), send(r, pw_ref, rpw_ref, 1))]
        for cp in started:
            cp.start()
        mine = [pltpu.make_async_copy(dwin_hbm.at[me], own_in, local_sems.at[0]),
                pltpu.make_async_copy(dwout_hbm.at[me], own_out, local_sems.at[1])]
        for cp in mine:
            cp.start()
        rvec_ref[0] = vec_ref[...]
        rpw_ref[0] = pw_ref[...]
        for cp in mine:
            cp.wait()

        def update(g, wmv, outs):
            delta, m_new, v_new = _adamw(wmv[0][...], g, wmv[1][...], wmv[2][...])
            for ref, val in zip(outs, (g, delta, m_new, v_new)):
                ref[...] = val

        g_in = own_in[...].astype(F32)
        g_out = own_out[...].astype(F32)
        for r in range(N_DEV - 1):
            g_in = g_in + lin_ref[r].astype(F32)
            g_out = g_out + lout_ref[r].astype(F32)
        update(g_in, big_wmv[0], big_out[0])
        update(g_out, big_wmv[1], big_out[1])

        for cp in started:
            cp.wait_recv()
        for cp in started:
            cp.wait_send()
        vec = rvec_ref[me]
        pw = rpw_ref[me]
        for s in range(1, N_DEV):
            vec = vec + rvec_ref[me ^ s]
            pw = pw + rpw_ref[me ^ s]
        vec_ref[...] = vec
        update(vec_ref[_ROW_G1:_ROW_G1 + 1, :], small_wmv[0], small_out[0])
        update(vec_ref[_ROW_G2:_ROW_G2 + 1, :], small_wmv[1], small_out[1])
        update(vec_ref[_ROW_PS:_ROW_PS + 1, :POOL_WIDTH], small_wmv[2], small_out[2])
        update(vec_ref[_ROW_RB:_ROW_RB + N_HEADS, :2 * LANES], small_wmv[3], small_out[3])
        update(pw, small_wmv[4], small_out[4])
        loss_out[...] = jnp.broadcast_to(vec_ref[_ROW_LOSS:_ROW_LOSS + 1, :LANES], loss_out.shape)

    vm = pl.BlockSpec(memory_space=pltpu.VMEM)
    hbm = pl.BlockSpec(memory_space=pl.ANY)
    f32 = lambda shape: jax.ShapeDtypeStruct(shape, F32)
    out_shapes = [f32((D_MODEL, IN_SHARD))] * 4 + [f32((OUT_SHARD, D_MODEL))] * 4
    for shape in small_shapes:
        out_shapes += [f32(shape)] * 4
    out_shapes.append(f32((8, LANES)))
    args = [dwin_g, land_in, dwout_g, land_out, *grads]
    for wmv in weights:
        args += list(wmv)
    for wmv in big:
        args += list(wmv)
    return pl.pallas_call(
        body,
        name="reduce_adamw",
        out_shape=tuple(out_shapes),
        in_specs=[hbm, vm, hbm, vm] + [vm] * (len(args) - 4),
        out_specs=tuple([vm] * len(out_shapes)),
        scratch_shapes=[
            pltpu.VMEM((D_MODEL, IN_SHARD), BF16),
            pltpu.VMEM((OUT_SHARD, D_MODEL), BF16),
            pltpu.VMEM((_VEC_ROWS, D_MODEL), F32),
            pltpu.VMEM((N_DEV, _VEC_ROWS, D_MODEL), F32),
            pltpu.VMEM((N_DEV, N_GROUPS, GROUP_DIM, GROUP_DIM), F32),
            pltpu.SemaphoreType.DMA((2 * (N_DEV - 1),)),
            pltpu.SemaphoreType.DMA((2 * (N_DEV - 1),)),
            pltpu.SemaphoreType.DMA((2,)),
        ],
        compiler_params=_params(),
    )(*args)


def kernel(x, norm_gain, w_in, pool_w, pool_scale, rel_bias, w_out, final_norm_gain, loss_target, m_norm_gain, m_w_in, m_pool_w, m_pool_scale, m_rel_bias, m_w_out, m_final_norm_gain, v_norm_gain, v_w_in, v_pool_w, v_pool_scale, v_rel_bias, v_w_out, v_final_norm_gain):
    t = x.shape[1]
    assert x.shape[0] == 1 and t % TOKEN_TILE == 0 and t // Q_BLOCK >= 4
    x2d = x[0]
    tgt2d = loss_target[0]
    g2 = final_norm_gain.reshape(1, D_MODEL)

    rb = rel_bias[0]
    rel_line = jnp.concatenate([
        jnp.broadcast_to(rb[:, :1], (N_HEADS, _REL_FIRST)), rb,
        jnp.broadcast_to(rb[:, N_REL - 1:], (N_HEADS, TOEPLITZ - _REL_FIRST - N_REL)),
    ], axis=1).reshape(N_HEADS, 1, TOEPLITZ)
    wg_in, bias_tile = _gather_weights(w_in[0], rel_line)

    pvg, qkv, ag, y_pool = _norm_inproj(x2d, norm_gain, wg_in, pool_w[0], pool_scale)
    o, y_attn, wg_out = _attn_fwd(qkv, ag, bias_tile, w_out[0])
    dx2, dy_pool, dy_attn, dwout_g, d_g2, loss_sum = _outproj_loss(x2d, tgt2d, y_pool, y_attn, wg_out, g2)
    dq, dk, dv, dag, ds_sum, land_out = _attn_bwd(qkv, ag, o, dy_attn, bias_tile, dwout_g)
    d_rb = _bias_grad(ds_sum)
    dwin_g, d_pool, d_pw, d_ps = _inproj_bwd_dw(x2d, pvg, dy_pool, (dq, dk, dv, dag), norm_gain, pool_w[0], pool_scale)
    dproj = (d_pool, dq, dk, dv, dag)
    grad_x, d_g1, land_in = _inproj_bwd_dx(x2d, dx2, dproj, norm_gain, wg_in, dwin_g)

    pad_rb = lambda a: jnp.pad(a[0], ((0, 0), (0, 2 * LANES - N_REL)))
    row = lambda a: a.reshape(1, D_MODEL)
    weights = [
        (norm_gain, m_norm_gain, v_norm_gain),
        (row(final_norm_gain), row(m_final_norm_gain), row(v_final_norm_gain)),
        (pool_scale, m_pool_scale, v_pool_scale),
        (pad_rb(rel_bias), pad_rb(m_rel_bias), pad_rb(v_rel_bias)),
        (pool_w[0], m_pool_w[0], v_pool_w[0]),
    ]
    big = [(w_in[0], m_w_in[0], v_w_in[0]), (w_out[0], m_w_out[0], v_w_out[0])]
    res = _reduce_adamw(dwin_g, land_in, dwout_g, land_out, (d_g1, d_g2, d_ps, d_rb, d_pw, loss_sum), weights, big)
    loss = 0.5 * res[28][0, 0]

    def leaves(k):
        g1_, g2_, ps_, rb_, pw_ = (res[8 + 4 * leaf + k] for leaf in range(5))
        return [g1_, res[k][None], pw_[None], ps_, rb_[None, :, :N_REL], res[4 + k][None], g2_.reshape(D_MODEL)]

    return (loss, grad_x[None], *leaves(0), *leaves(1), *leaves(2), *leaves(3))
```

```python
import math

import jax
import jax.numpy as jnp
from jax import lax
from jax.experimental import pallas as pl
from jax.experimental.pallas import tpu as pltpu

F32 = jnp.float32
BF16 = jnp.bfloat16
MESH_ID = pl.DeviceIdType.MESH

D_MODEL = 1024
POOL_WIDTH = 512
ATTN_WIDTH = 512
POOL_WINDOWS = (2, 4, 8, 16)
N_GROUPS = 4
GROUP_DIM = 128
HEAD_DIM = 64
N_HEADS = 8
CHUNK = 64
LEFT_CHUNKS = 8
MAX_REL = 64
N_REL = 2 * MAX_REL + 1
IN_WIDTH = 2 * POOL_WIDTH + 4 * ATTN_WIDTH
EPS = 1e-6
MASK_VALUE = -1e30
ATTN_SCALE = 1.0 / math.sqrt(HEAD_DIM)
ADAM_LR = 0.001
ADAM_B1 = 0.9
ADAM_B2 = 0.999
ADAM_EPS = 1e-08
ADAM_WD = 0.01
ADAM_STEP = 10

N_DEV = 8
IN_SHARD = IN_WIDTH // N_DEV
OUT_SHARD = D_MODEL // N_DEV

LANES = 128
TOKEN_TILE = 512
HALO = 16
Q_BLOCK = 256
KV_BLOCKS = 3
KV_WINDOW = KV_BLOCKS * Q_BLOCK
PAIR = 2 * HEAD_DIM
N_PAIRS = N_HEADS // 2
TOEPLITZ = 1024
VMEM_LIMIT = 56 * 1024 * 1024


def _params(sem=None, vmem=VMEM_LIMIT):
    return pltpu.CompilerParams(dimension_semantics=sem, vmem_limit_bytes=vmem)


def _sigmoid(x):
    return 1.0 / (1.0 + jnp.exp(-x))


def _nt(a, b):
    return lax.dot_general(a, b, (((1,), (1,)), ((), ())), preferred_element_type=F32)


def _tn(a, b):
    return lax.dot_general(a, b, (((0,), (0,)), ((), ())), preferred_element_type=F32)


def _nn(a, b):
    return jnp.dot(a, b, preferred_element_type=F32)


def _mesh_pos():
    return lax.axis_index("x"), lax.axis_index("y"), lax.axis_index("c")


def _dev_index(p):
    return 4 * p[0] + 2 * p[1] + p[2]


def _peer(r):
    x, y, c = _mesh_pos()
    return (x ^ ((r >> 2) & 1), y ^ ((r >> 1) & 1), c ^ (r & 1))


def _exchange_copies(src_hbm, land_hbm, send_sems, recv_sems):
    return [
        pltpu.make_async_remote_copy(
            src_ref=src_hbm.at[_dev_index(_peer(r))], dst_ref=land_hbm.at[r - 1],
            send_sem=send_sems.at[r - 1], recv_sem=recv_sems.at[r - 1],
            device_id=_peer(r), device_id_type=MESH_ID)
        for r in range(1, N_DEV)
    ]


def _gather_weights(w_in_shard, rel_line):
    def body(win_ref, line_ref, gin_ref, bias_ref, sin_ref, send_sems, recv_sems):
        x, y, c = _mesh_pos()
        me, sibling = (x, y, c), (x, y, 1 - c)
        chips = [(1 - x, y), (x, 1 - y), (1 - x, 1 - y)]

        sin_ref[...] = win_ref[...].astype(BF16)
        gin_ref[_dev_index(me)] = sin_ref[...]

        def copy(k, block, to, from_shard=False):
            return pltpu.make_async_remote_copy(
                src_ref=sin_ref if from_shard else gin_ref.at[_dev_index(block)],
                dst_ref=gin_ref.at[_dev_index(block)],
                send_sem=send_sems.at[k],
                recv_sem=recv_sems.at[k],
                device_id=to,
                device_id_type=MESH_ID,
            )

        first = [copy(0, me, sibling, True)]
        first += [copy(1 + j, me, (*chip, c), True) for j, chip in enumerate(chips)]
        for cp in first:
            cp.start()
        passed = [copy(4 + j, (*chip, c), sibling) for j, chip in enumerate(chips)]

        def bias_heads(lo, hi):
            for h in range(lo, hi):
                bias_ref[h] = _toeplitz_bias(line_ref[h])

        bias_heads(0, N_HEADS - 3)
        for j, chip in enumerate(chips):
            copy(1 + j, (*chip, c), me).wait_recv()
            passed[j].start()
            bias_heads(N_HEADS - 3 + j, N_HEADS - 2 + j)
        copy(0, sibling, me).wait_recv()
        for j, chip in enumerate(chips):
            copy(4 + j, (*chip, 1 - c), me).wait_recv()
        for cp in first + passed:
            cp.wait_send()

    vm = pl.BlockSpec(memory_space=pltpu.VMEM)
    return pl.pallas_call(
        body,
        name="gather_weights",
        out_shape=(
            jax.ShapeDtypeStruct((N_DEV, D_MODEL, IN_SHARD), BF16),
            jax.ShapeDtypeStruct((N_HEADS, Q_BLOCK, KV_WINDOW), F32),
        ),
        in_specs=[vm, vm],
        out_specs=(vm, vm),
        scratch_shapes=[
            pltpu.VMEM((D_MODEL, IN_SHARD), BF16),
            pltpu.SemaphoreType.DMA((7,)),
            pltpu.SemaphoreType.DMA((7,)),
        ],
        compiler_params=_params(),
    )(w_in_shard, rel_line)


def _load_w_in(wg_hbm, wfull_ref, sem):
    copies = [
        pltpu.make_async_copy(wg_hbm.at[d], wfull_ref.at[:, d * IN_SHARD:(d + 1) * IN_SHARD], sem.at[d])
        for d in range(N_DEV)
    ]
    for cp in copies:
        cp.start()
    for cp in copies:
        cp.wait()


def _norm_inproj(x2d, g1, wg, pool_w, pool_scale):
    t = x2d.shape[0]

    def body(x_ref, g_ref, wg_hbm, pw_ref, ps_ref, pvg_ref, qkv_ref, ag_ref, yp_ref, wfull_ref, halo_ref, sem):
        i = pl.program_id(0)

        @pl.when(i == 0)
        def _():
            _load_w_in(wg_hbm, wfull_ref, sem)
            halo_ref[...] = jnp.zeros_like(halo_ref)

        xf = x_ref[...]
        r = lax.rsqrt(jnp.mean(xf * xf, axis=-1, keepdims=True) + EPS)
        h = ((xf * r) * g_ref[...]).astype(BF16)
        chunk = lambda ci: _nn(h, wfull_ref[:, ci * POOL_WIDTH:(ci + 1) * POOL_WIDTH])
        pv, pg = chunk(0), chunk(1)
        pvg_ref[:, :POOL_WIDTH] = pv
        pvg_ref[:, POOL_WIDTH:] = pg
        halo = halo_ref[...]
        halo_ref[...] = pv[TOKEN_TILE - HALO:]
        for gi in range(N_GROUPS):
            sl = slice(gi * GROUP_DIM, (gi + 1) * GROUP_DIM)
            d = _pool_diffs(pv, halo, i * TOKEN_TILE, gi)
            z = _nn(d.astype(BF16), pw_ref[gi].astype(BF16))
            g = pg[:, sl]
            yp_ref[:, sl] = ((z * ps_ref[:, sl]) * (g * _sigmoid(g))).astype(BF16)
        for ci in range(2, 5):
            qkv_ref[:, (ci - 2) * POOL_WIDTH:(ci - 1) * POOL_WIDTH] = chunk(ci).astype(BF16)
        ag_ref[...] = chunk(5)

    tile = lambda width: pl.BlockSpec((TOKEN_TILE, width), lambda i: (i, 0))
    return pl.pallas_call(
        body,
        name="norm_inproj",
        grid=(t // TOKEN_TILE,),
        out_shape=(
            jax.ShapeDtypeStruct((t, 2 * POOL_WIDTH), F32),
            jax.ShapeDtypeStruct((t, 3 * ATTN_WIDTH), BF16),
            jax.ShapeDtypeStruct((t, ATTN_WIDTH), F32),
            jax.ShapeDtypeStruct((t, POOL_WIDTH), BF16),
        ),
        in_specs=[
            tile(D_MODEL),
            pl.BlockSpec((1, D_MODEL), lambda i: (0, 0)),
            pl.BlockSpec(memory_space=pl.ANY),
            pl.BlockSpec((N_GROUPS, GROUP_DIM, GROUP_DIM), lambda i: (0, 0, 0)),
            pl.BlockSpec((1, POOL_WIDTH), lambda i: (0, 0)),
        ],
        out_specs=(tile(2 * POOL_WIDTH), tile(3 * ATTN_WIDTH), tile(ATTN_WIDTH), tile(POOL_WIDTH)),
        scratch_shapes=[
            pltpu.VMEM((D_MODEL, IN_WIDTH), BF16),
            pltpu.VMEM((HALO, POOL_WIDTH), F32),
            pltpu.SemaphoreType.DMA((N_DEV,)),
        ],
        compiler_params=_params(("arbitrary",)),
    )(x2d, g1, wg, pool_w, pool_scale)


def _inv_count(first_row, rows, window):
    tpos = first_row + lax.broadcasted_iota(jnp.int32, (rows, 1), 0)
    return 1.0 / jnp.minimum(tpos + 1, window).astype(F32)


def _causal_window_sum(ext, window):
    s, k = ext, 1
    while k < window:
        s = s + pltpu.roll(s, k, 0)
        k *= 2
    return s


def _pool_diffs(pv, halo, first_row, gi):
    w = POOL_WINDOWS[gi]
    sl = slice(gi * GROUP_DIM, (gi + 1) * GROUP_DIM)
    ext = jnp.concatenate([halo[:, sl], pv[:, sl]], axis=0)
    s = _causal_window_sum(ext, w)[HALO:]
    return s * _inv_count(first_row, pv.shape[0], w) - pv[:, sl]


def _pool_bwd_tile(i, n_tiles, cur_ref, prev_ref, pgn_ref, dy_ref, dyn_ref, pw_ref, ps_ref, dp_ref, dpw_ref, dps_ref):
    pv = cur_ref[:, :POOL_WIDTH]
    pg = cur_ref[:, POOL_WIDTH:]
    prev = jnp.where(i > 0, prev_ref[...], 0.0)
    has_next = i < n_tiles - 1
    rows = TOKEN_TILE + HALO
    for gi in range(N_GROUPS):
        w = POOL_WINDOWS[gi]
        sl = slice(gi * GROUP_DIM, (gi + 1) * GROUP_DIM)
        pw = pw_ref[gi].astype(BF16)
        ps = ps_ref[:, sl]
        d = _pool_diffs(pv, prev, i * TOKEN_TILE, gi).astype(BF16)
        z = _nn(d, pw)
        g_ext = jnp.concatenate([pg[:, sl], pgn_ref[:, sl]], axis=0)
        dy_ext = jnp.concatenate([dy_ref[:, sl], dyn_ref[:, sl]], axis=0).astype(F32)
        sig = _sigmoid(g_ext)
        gate = g_ext * sig
        dz_ext = ((dy_ext * gate) * ps).astype(BF16)
        dd_ext = _nt(dz_ext, pw)
        e = dd_ext * _inv_count(i * TOKEN_TILE, rows, w)
        row = lax.broadcasted_iota(jnp.int32, (rows, 1), 0)
        e = jnp.where(jnp.logical_or(row < TOKEN_TILE, has_next), e, 0.0)
        s, k = e, 1
        while k < w:
            s = s + pltpu.roll(s, rows - k, 0)
            k *= 2
        dp_ref[:, sl] = (s[:TOKEN_TILE] - dd_ext[:TOKEN_TILE]).astype(BF16)
        dy = dy_ext[:TOKEN_TILE]
        g = g_ext[:TOKEN_TILE]
        sg = sig[:TOKEN_TILE]
        dgate = sg * (1.0 + g * (1.0 - sg))
        dp_ref[:, POOL_WIDTH + gi * GROUP_DIM:POOL_WIDTH + (gi + 1) * GROUP_DIM] = (
            (dy * (z * ps)) * dgate).astype(BF16)
        dps_ref[:, sl] += jnp.sum((dy * gate[:TOKEN_TILE]) * z, axis=0, keepdims=True)
        dpw_ref[gi] += _tn(d, dz_ext[:TOKEN_TILE])


_REL_FIRST = KV_WINDOW - 1 - MAX_REL


def _skew_rows(a, right):
    rows, lanes = a.shape
    row = lax.broadcasted_iota(jnp.int32, a.shape, 0)
    for b in range(rows.bit_length() - 1):
        shift = (1 << b) if right else lanes - (1 << b)
        a = jnp.where((row >> b) & 1 == 1, pltpu.roll(a, shift, 1), a)
    return a


def _toeplitz_bias(line):
    a = jnp.broadcast_to(line, (Q_BLOCK, TOEPLITZ))
    a = _skew_rows(a, True)
    a = pltpu.roll(a, TOEPLITZ - (Q_BLOCK - 1), 1)
    a = a[:, :KV_WINDOW]
    qc = lax.broadcasted_iota(jnp.int32, a.shape, 0) // CHUNK
    kc = lax.broadcasted_iota(jnp.int32, a.shape, 1) // CHUNK
    visible = jnp.logical_and(kc >= qc, kc <= qc + LEFT_CHUNKS)
    return jnp.where(visible, a, MASK_VALUE)


_BAND_WIDTH = 2 * LANES
_BAND_START = (384, 512, 512, 512)
_BAND_REL = tuple(a - (KV_BLOCKS - 1) * Q_BLOCK - qc * CHUNK for qc, a in enumerate(_BAND_START))


def _bias_grad(band, total):
    def body(band_ref, total_ref, out_ref):
        for h in range(N_HEADS):
            near = jnp.zeros((1, 2 * _BAND_WIDTH), F32)
            for qc in range(Q_BLOCK // CHUNK):
                a = jnp.concatenate([band_ref[h, qc], jnp.zeros((CHUNK, _BAND_WIDTH), F32)], axis=1)
                a = _skew_rows(a, False)
                diag = jnp.sum(a, axis=0, keepdims=True)
                left = -MAX_REL - _BAND_REL[qc]
                near = near + (pltpu.roll(diag, 2 * _BAND_WIDTH - left, 1) if left else diag)
            near = near[:, :2 * LANES]
            r = lax.broadcasted_iota(jnp.int32, near.shape, 1)
            near = jnp.where(jnp.logical_and(r >= 1, r < 2 * MAX_REL), near, 0.0)
            everything = jnp.sum(jnp.sum(total_ref[h], axis=0, keepdims=True), axis=1, keepdims=True)
            far = everything - jnp.sum(near, axis=1, keepdims=True)
            out_ref[h:h + 1, :] = jnp.where(r == 0, far, near)

    vm = pl.BlockSpec(memory_space=pltpu.VMEM)
    return pl.pallas_call(
        body,
        name="bias_grad",
        out_shape=jax.ShapeDtypeStruct((N_HEADS, 2 * LANES), F32),
        in_specs=[vm, vm],
        out_specs=vm,
        compiler_params=_params(),
    )(band, total)


def _head_lanes(hh):
    lane = lax.broadcasted_iota(jnp.int32, (1, PAIR), 1)
    return (lane < HEAD_DIM) if hh == 0 else (lane >= HEAD_DIM)


def _score_windows(nwin):
    if nwin < KV_BLOCKS:
        return [(slice(0, Q_BLOCK), slice(0, nwin * Q_BLOCK), None)]
    pieces = []
    for qc in range(Q_BLOCK // CHUNK):
        rows = slice(qc * CHUNK, (qc + 1) * CHUNK)
        if qc < 2:
            pieces.append((rows, slice(0, KV_WINDOW - LANES), slice(KV_WINDOW - LANES, KV_WINDOW)))
        else:
            pieces.append((rows, slice(LANES, KV_WINDOW), slice(0, LANES)))
    return pieces


def _block_rows(first_block, n_blocks):
    if isinstance(first_block, int):
        return pl.ds(first_block * Q_BLOCK, n_blocks * Q_BLOCK)
    return pl.ds(pl.multiple_of(first_block * Q_BLOCK, Q_BLOCK), n_blocks * Q_BLOCK)


def _attn_fwd(qkv, ag, bias_tile, w_out_shard):
    t = qkv.shape[0]
    nb = t // Q_BLOCK

    def body(q_ref, k_ref, v_ref, ag_ref, bias_ref, wout_ref, o_ref, y_ref, gout_hbm,
             s_scr, p_scr, linv_scr, sout_ref, send_sems, recv_sems, local_sem):
        pair = pl.program_id(0)
        me = _dev_index(_mesh_pos())
        mine = pltpu.make_async_copy(sout_ref, gout_hbm.at[me], local_sem)

        def shard_copy(r, block):
            return pltpu.make_async_remote_copy(
                src_ref=sout_ref, dst_ref=gout_hbm.at[block], send_sem=send_sems.at[r - 1],
                recv_sem=recv_sems.at[r - 1], device_id=_peer(r), device_id_type=MESH_ID)

        @pl.when(pair == 0)
        def _():
            sout_ref[...] = wout_ref[...].astype(BF16)
            mine.start()
            for r in range(1, N_DEV):
                shard_copy(r, me).start()

        def scores(i, slot, nwin):
            q = q_ref[_block_rows(i, 1), :]
            kw = k_ref[_block_rows(i + 1 - nwin, nwin), :]
            for hh in range(2):
                q_h = jnp.where(_head_lanes(hh), q, jnp.zeros_like(q)) * ATTN_SCALE
                s_scr[slot, hh, :, :nwin * Q_BLOCK] = _nt(q_h, kw)

        def softmax(slot, nwin):
            off = (KV_BLOCKS - nwin) * Q_BLOCK
            for hh in range(2):
                for rows, cols, rest in _score_windows(nwin):
                    s = s_scr[slot, hh, rows, cols] + bias_ref[hh, rows, off + cols.start:off + cols.stop]
                    e = jnp.exp(s - jnp.max(s, axis=-1, keepdims=True))
                    linv_scr[slot, hh, rows, :] = 1.0 / jnp.sum(e, axis=-1, keepdims=True)
                    p_scr[slot, hh, rows, cols] = e.astype(BF16)
                    if rest is not None:
                        p_scr[slot, hh, rows, rest] = jnp.zeros((CHUNK, LANES), BF16)

        def output(i, slot, nwin):
            vw = v_ref[_block_rows(i + 1 - nwin, nwin), :]
            outs = [_nn(p_scr[slot, hh, :, :nwin * Q_BLOCK], vw) * linv_scr[slot, hh] for hh in range(2)]
            o = jnp.where(_head_lanes(0), outs[0], outs[1])
            g = ag_ref[_block_rows(i, 1), :]
            o_ref[_block_rows(i, 1), :] = o.astype(BF16)
            y_ref[_block_rows(i, 1), :] = (o * (g * _sigmoid(g))).astype(BF16)

        scores(0, 0, 1)
        scores(1, 1, 2)
        softmax(0, 1)
        scores(2, 0, 3)
        softmax(1, 2)
        output(0, 0, 1)
        scores(3, 1, 3)
        softmax(0, 3)
        output(1, 1, 2)

        def two_steps(k, carry):
            i = 3 + 2 * k
            scores(i + 1, 0, KV_BLOCKS)
            softmax(1, KV_BLOCKS)
            output(i - 1, 0, KV_BLOCKS)
            scores(i + 2, 1, KV_BLOCKS)
            softmax(0, KV_BLOCKS)
            output(i, 1, KV_BLOCKS)
            return carry

        lax.fori_loop(0, (nb - 4) // 2, two_steps, 0)
        last = (nb - 1) % 2
        softmax(last, KV_BLOCKS)
        output(nb - 2, 1 - last, KV_BLOCKS)
        output(nb - 1, last, KV_BLOCKS)

        @pl.when(pair == N_PAIRS - 1)
        def _():
            for r in range(1, N_DEV):
                shard_copy(r, _dev_index(_peer(r))).wait_recv()
            for r in range(1, N_DEV):
                shard_copy(r, me).wait_send()
            mine.wait()

    col = lambda c0: pl.BlockSpec((t, PAIR), lambda j: (0, c0 + j))
    return pl.pallas_call(
        body,
        name="attn_fwd",
        grid=(N_PAIRS,),
        out_shape=(
            jax.ShapeDtypeStruct((t, ATTN_WIDTH), BF16),
            jax.ShapeDtypeStruct((t, ATTN_WIDTH), BF16),
            jax.ShapeDtypeStruct((N_DEV, OUT_SHARD, D_MODEL), BF16),
        ),
        in_specs=[col(0), col(N_PAIRS), col(2 * N_PAIRS), col(0),
                  pl.BlockSpec((2, Q_BLOCK, KV_WINDOW), lambda j: (j, 0, 0)),
                  pl.BlockSpec((OUT_SHARD, D_MODEL), lambda j: (0, 0))],
        out_specs=(col(0), col(0), pl.BlockSpec(memory_space=pl.ANY)),
        scratch_shapes=[
            pltpu.VMEM((2, 2, Q_BLOCK, KV_WINDOW), F32),
            pltpu.VMEM((2, 2, Q_BLOCK, KV_WINDOW), BF16),
            pltpu.VMEM((2, 2, Q_BLOCK, 1), F32),
            pltpu.VMEM((OUT_SHARD, D_MODEL), BF16),
            pltpu.SemaphoreType.DMA((N_DEV - 1,)),
            pltpu.SemaphoreType.DMA((N_DEV - 1,)),
            pltpu.SemaphoreType.DMA,
        ],
        compiler_params=_params(("arbitrary",)),
    )(qkv, qkv, qkv, ag, bias_tile, w_out_shard)


def _attn_bwd(qkv, ag, o, dy_attn, bias_tile, dwout_g):
    t = qkv.shape[0]
    nb = t // Q_BLOCK

    def body(q_ref, k_ref, v_ref, ag_ref, o_ref, dy_ref, bias_ref, dwout_hbm,
             dq_ref, dk_ref, dv_ref, dag_ref, band_ref, total_ref, land_hbm,
             do_scr, s_scr, dp_scr, p_scr, dsb_scr, dq_scr, dk_acc, dv_acc, send_sems, recv_sems):
        exchange = _exchange_copies(dwout_hbm, land_hbm, send_sems, recv_sems)

        @pl.when(pl.program_id(0) == 0)
        def _():
            for cp in exchange:
                cp.start()

        def gates(i, carry):
            rows = _block_rows(i, 1)
            g = ag_ref[rows, :]
            sig = _sigmoid(g)
            dy = dy_ref[rows, :].astype(F32)
            do_scr[rows, :] = (dy * (g * sig)).astype(BF16)
            dag_ref[rows, :] = ((dy * o_ref[rows, :].astype(F32)) * (sig * (1.0 + g * (1.0 - sig)))).astype(BF16)
            return carry

        lax.fori_loop(0, nb, gates, 0)
        band_ref[...] = jnp.zeros_like(band_ref)
        total_ref[...] = jnp.zeros_like(total_ref)

        def nwin_of(i):
            return min(i + 1, KV_BLOCKS) if isinstance(i, int) else KV_BLOCKS

        def operands(i, hh):
            lanes = _head_lanes(hh)
            q = q_ref[_block_rows(i, 1), :]
            do = do_scr[_block_rows(i, 1), :]
            return (jnp.where(lanes, q, jnp.zeros_like(q)) * ATTN_SCALE, jnp.where(lanes, do, jnp.zeros_like(do)))

        def products(i, hh):
            nwin = nwin_of(i)
            win = _block_rows(i + 1 - nwin, nwin)
            q_h, do_h = operands(i, hh)
            s_scr[hh, :, :nwin * Q_BLOCK] = _nt(q_h, k_ref[win, :])
            dp_scr[hh, :, :nwin * Q_BLOCK] = _nt(do_h, v_ref[win, :])

        def grads(i, hh):
            nwin = nwin_of(i)
            off = (KV_BLOCKS - nwin) * Q_BLOCK
            for rows, cols, rest in _score_windows(nwin):
                bias_cols = slice(off + cols.start, off + cols.stop)
                s = s_scr[hh, rows, cols] + bias_ref[hh, rows, bias_cols]
                e = jnp.exp(s - jnp.max(s, axis=-1, keepdims=True))
                p = e * (1.0 / jnp.sum(e, axis=-1, keepdims=True))
                dp = dp_scr[hh, rows, cols]
                ds = p * (dp - jnp.sum(p * dp, axis=-1, keepdims=True))
                total_ref[hh, rows, :] += sum(ds[:, c0:c0 + LANES] for c0 in range(0, ds.shape[1], LANES))
                for qc in range(rows.start // CHUNK, rows.stop // CHUNK):
                    lo = max(_BAND_START[qc], bias_cols.start)
                    hi = min(_BAND_START[qc] + _BAND_WIDTH, bias_cols.stop)
                    if lo < hi:
                        band_ref[hh, qc, :, lo - _BAND_START[qc]:hi - _BAND_START[qc]] += ds[
                            qc * CHUNK - rows.start:(qc + 1) * CHUNK - rows.start,
                            lo - bias_cols.start:hi - bias_cols.start]
                p_scr[hh, rows, cols] = p.astype(BF16)
                dsb_scr[hh, rows, cols] = ds.astype(BF16)
                if rest is not None:
                    p_scr[hh, rows, rest] = jnp.zeros((CHUNK, LANES), BF16)
                    dsb_scr[hh, rows, rest] = jnp.zeros((CHUNK, LANES), BF16)

        def ring(block):
            return block % KV_BLOCKS if isinstance(block, int) else lax.rem(block, KV_BLOCKS)

        def accumulate(i, hh):
            nwin = nwin_of(i)
            w = nwin * Q_BLOCK
            win = _block_rows(i + 1 - nwin, nwin)
            q_h, do_h = operands(i, hh)
            ds_b = dsb_scr[hh, :, :w]
            dq_h = _nn(ds_b, k_ref[win, :]) * ATTN_SCALE
            dkw = _tn(ds_b, q_h)
            dvw = _tn(p_scr[hh, :, :w], do_h)
            for b in range(nwin):
                slot = ring(i + 1 - nwin + b)
                part = slice(b * Q_BLOCK, (b + 1) * Q_BLOCK)
                if hh == 0 and b == nwin - 1:
                    dk_acc[slot] = dkw[part]
                    dv_acc[slot] = dvw[part]
                else:
                    dk_acc[slot] += dkw[part]
                    dv_acc[slot] += dvw[part]
            if hh == 0:
                dq_scr[...] = dq_h
            else:
                dq_ref[_block_rows(i, 1), :] = jnp.where(_head_lanes(0), dq_scr[...], dq_h).astype(BF16)
                if not (isinstance(i, int) and i < KV_BLOCKS - 1):
                    flush(i - (KV_BLOCKS - 1))

        def flush(block):
            dk_ref[_block_rows(block, 1), :] = dk_acc[ring(block)].astype(BF16)
            dv_ref[_block_rows(block, 1), :] = dv_acc[ring(block)].astype(BF16)

        def tile(n):
            return n // 2, n % 2

        def step(n):
            if n + 1 < 2 * nb:
                products(*tile(n + 1))
            grads(*tile(n))
            if n >= 1:
                accumulate(*tile(n - 1))

        products(0, 0)
        for n in range(2 * KV_BLOCKS):
            step(n)

        def two_steps(i, carry):
            products(i, 1)
            grads(i, 0)
            accumulate(i - 1, 1)
            products(i + 1, 0)
            grads(i, 1)
            accumulate(i, 0)
            return carry

        lax.fori_loop(KV_BLOCKS, nb - 1, two_steps, 0)
        step(2 * nb - 2)
        step(2 * nb - 1)
        accumulate(nb - 1, 1)
        flush(nb - 2)
        flush(nb - 1)

        @pl.when(pl.program_id(0) == N_PAIRS - 1)
        def _():
            for cp in exchange:
                cp.wait_recv()
            for cp in exchange:
                cp.wait_send()

    col = lambda c0: pl.BlockSpec((t, PAIR), lambda j: (0, c0 + j))
    once = pl.BlockSpec((t, PAIR), lambda j: (0, j), pipeline_mode=pl.Buffered(1))
    tile_spec = pl.BlockSpec((2, Q_BLOCK, KV_WINDOW), lambda j: (j, 0, 0))
    out = jax.ShapeDtypeStruct((t, ATTN_WIDTH), BF16)
    return pl.pallas_call(
        body,
        name="attn_bwd",
        grid=(N_PAIRS,),
        out_shape=(out, out, out, out,
                   jax.ShapeDtypeStruct((N_HEADS, Q_BLOCK // CHUNK, CHUNK, _BAND_WIDTH), F32),
                   jax.ShapeDtypeStruct((N_HEADS, Q_BLOCK, LANES), F32),
                   jax.ShapeDtypeStruct((N_DEV - 1, OUT_SHARD, D_MODEL), BF16)),
        in_specs=[col(0), col(N_PAIRS), col(2 * N_PAIRS), once, col(0), col(0), tile_spec,
                  pl.BlockSpec(memory_space=pl.ANY)],
        out_specs=(col(0), col(0), col(0), col(0),
                   pl.BlockSpec((2, Q_BLOCK // CHUNK, CHUNK, _BAND_WIDTH), lambda j: (j, 0, 0, 0)),
                   pl.BlockSpec((2, Q_BLOCK, LANES), lambda j: (j, 0, 0)),
                   pl.BlockSpec(memory_space=pl.ANY)),
        scratch_shapes=[
            pltpu.VMEM((t, PAIR), BF16),
            pltpu.VMEM((2, Q_BLOCK, KV_WINDOW), F32),
            pltpu.VMEM((2, Q_BLOCK, KV_WINDOW), F32),
            pltpu.VMEM((2, Q_BLOCK, KV_WINDOW), BF16),
            pltpu.VMEM((2, Q_BLOCK, KV_WINDOW), BF16),
            pltpu.VMEM((Q_BLOCK, PAIR), F32),
            pltpu.VMEM((KV_BLOCKS, Q_BLOCK, PAIR), F32),
            pltpu.VMEM((KV_BLOCKS, Q_BLOCK, PAIR), F32),
            pltpu.SemaphoreType.DMA((N_DEV - 1,)),
            pltpu.SemaphoreType.DMA((N_DEV - 1,)),
        ],
        compiler_params=_params(("arbitrary",), vmem=60 * 1024 * 1024),
    )(qkv, qkv, qkv, ag, o, dy_attn, bias_tile, dwout_g)


def _outproj_loss(x2d, tgt2d, y_pool, y_attn, wout_g, g2):
    t = x2d.shape[0]
    n_tiles = t // TOKEN_TILE

    def body(x_ref, tgt_ref, yp_ref, ya_ref, w_ref, g_ref,
             dx2_ref, dyp_ref, dya_ref, dw_ref, dg_ref, loss_ref, acc_ref):
        i = pl.program_id(0)

        @pl.when(i == 0)
        def _():
            acc_ref[...] = jnp.zeros_like(acc_ref)
            dg_ref[...] = jnp.zeros_like(dg_ref)
            loss_ref[...] = jnp.zeros_like(loss_ref)

        w = w_ref[...].reshape(D_MODEL, D_MODEL)
        y = jnp.concatenate([yp_ref[...], ya_ref[...]], axis=1)
        x2 = x_ref[...] + _nn(y, w)
        r = lax.rsqrt(jnp.mean(x2 * x2, axis=-1, keepdims=True) + EPS)
        xh = x2 * r
        g = g_ref[...]
        diff = xh * g - tgt_ref[...]
        tok = jnp.sum(diff * diff, axis=-1, keepdims=True) * (1.0 / D_MODEL)
        loss_ref[...] += jnp.sum(tok, axis=0, keepdims=True)
        dout = diff * (1.0 / D_MODEL)
        dg_ref[...] += jnp.sum(dout * xh, axis=0, keepdims=True)
        u = dout * g
        dx2 = r * (u - xh * jnp.mean(u * xh, axis=-1, keepdims=True))
        dx2_ref[...] = dx2
        dx2_b = dx2.astype(BF16)
        dy = _nt(dx2_b, w)
        dyp_ref[...] = dy[:, :POOL_WIDTH].astype(BF16)
        dya_ref[...] = dy[:, POOL_WIDTH:].astype(BF16)
        acc_ref[...] += _tn(y, dx2_b)

        @pl.when(i == n_tiles - 1)
        def _():
            dw_ref[...] = acc_ref[...].reshape(N_DEV, OUT_SHARD, D_MODEL).astype(BF16)

    tile = lambda width: pl.BlockSpec((TOKEN_TILE, width), lambda i: (i, 0))
    return pl.pallas_call(
        body,
        name="outproj_loss",
        grid=(n_tiles,),
        out_shape=(
            jax.ShapeDtypeStruct((t, D_MODEL), F32),
            jax.ShapeDtypeStruct((t, POOL_WIDTH), BF16),
            jax.ShapeDtypeStruct((t, ATTN_WIDTH), BF16),
            jax.ShapeDtypeStruct((N_DEV, OUT_SHARD, D_MODEL), BF16),
            jax.ShapeDtypeStruct((1, D_MODEL), F32),
            jax.ShapeDtypeStruct((8, LANES), F32),
        ),
        in_specs=[
            tile(D_MODEL), tile(D_MODEL), tile(POOL_WIDTH), tile(ATTN_WIDTH),
            pl.BlockSpec((N_DEV, OUT_SHARD, D_MODEL), lambda i: (0, 0, 0)),
            pl.BlockSpec((1, D_MODEL), lambda i: (0, 0)),
        ],
        out_specs=(
            tile(D_MODEL), tile(POOL_WIDTH), tile(ATTN_WIDTH),
            pl.BlockSpec((N_DEV, OUT_SHARD, D_MODEL), lambda i: (0, 0, 0)),
            pl.BlockSpec((1, D_MODEL), lambda i: (0, 0)),
            pl.BlockSpec((8, LANES), lambda i: (0, 0)),
        ),
        scratch_shapes=[pltpu.VMEM((D_MODEL, D_MODEL), F32)],
        compiler_params=_params(("arbitrary",)),
    )(x2d, tgt2d, y_pool, y_attn, wout_g, g2)


def _dproj_specs():
    tile = lambda width: pl.BlockSpec((TOKEN_TILE, width), lambda i: (i, 0))
    return [tile(2 * POOL_WIDTH)] + [tile(ATTN_WIDTH)] * 4


def _inproj_bwd_dx(x2d, dx2, dproj, g1, wg, dwin_g):
    t = x2d.shape[0]
    n_tiles = t // TOKEN_TILE

    def body(x_ref, dx2_ref, dp_ref, dq_ref, dk_ref, dv_ref, dag_ref, g_ref, wg_hbm, dwin_hbm,
             gx_ref, dg_ref, land_hbm, wfull_ref, sem, send_sems, recv_sems):
        i = pl.program_id(0)
        exchange = _exchange_copies(dwin_hbm, land_hbm, send_sems, recv_sems)

        @pl.when(i == 0)
        def _():
            for cp in exchange:
                cp.start()
            _load_w_in(wg_hbm, wfull_ref, sem)
            dg_ref[...] = jnp.zeros_like(dg_ref)

        dproj_t = jnp.concatenate([dp_ref[...], dq_ref[...], dk_ref[...], dv_ref[...], dag_ref[...]], axis=1)
        dh = _nt(dproj_t, wfull_ref[...])
        xf = x_ref[...]
        r = lax.rsqrt(jnp.mean(xf * xf, axis=-1, keepdims=True) + EPS)
        xh = xf * r
        dg_ref[...] += jnp.sum(dh * xh, axis=0, keepdims=True)
        u = dh * g_ref[...]
        gx_ref[...] = dx2_ref[...] + r * (u - xh * jnp.mean(u * xh, axis=-1, keepdims=True))

        @pl.when(i == n_tiles - 1)
        def _():
            for cp in exchange:
                cp.wait_recv()
            for cp in exchange:
                cp.wait_send()

    tile = pl.BlockSpec((TOKEN_TILE, D_MODEL), lambda i: (i, 0))
    hbm = pl.BlockSpec(memory_space=pl.ANY)
    return pl.pallas_call(
        body,
        name="inproj_bwd_dx",
        grid=(n_tiles,),
        out_shape=(
            jax.ShapeDtypeStruct((t, D_MODEL), F32),
            jax.ShapeDtypeStruct((1, D_MODEL), F32),
            jax.ShapeDtypeStruct((N_DEV - 1, D_MODEL, IN_SHARD), BF16),
        ),
        in_specs=[tile, tile] + _dproj_specs() + [pl.BlockSpec((1, D_MODEL), lambda i: (0, 0)), hbm, hbm],
        out_specs=(tile, pl.BlockSpec((1, D_MODEL), lambda i: (0, 0)), hbm),
        scratch_shapes=[
            pltpu.VMEM((D_MODEL, IN_WIDTH), BF16),
            pltpu.SemaphoreType.DMA((N_DEV,)),
            pltpu.SemaphoreType.DMA((N_DEV - 1,)),
            pltpu.SemaphoreType.DMA((N_DEV - 1,)),
        ],
        compiler_params=_params(("arbitrary",)),
    )(x2d, dx2, *dproj, g1, wg, dwin_g)


def _inproj_bwd_dw(x2d, pvg, dy_pool, dattn, g1, pool_w, pool_scale):
    t = x2d.shape[0]
    n_tiles = t // TOKEN_TILE
    halo_per_tile = TOKEN_TILE // HALO
    last_halo = t // HALO - 1

    def body(x_ref, cur_ref, prev_ref, pgn_ref, dy_ref, dyn_ref, dq_ref, dk_ref, dv_ref, dag_ref, g_ref, pw_ref, ps_ref,
             out_ref, dp_ref, dpw_ref, dps_ref, acc_ref):
        i = pl.program_id(0)

        @pl.when(i == 0)
        def _():
            acc_ref[...] = jnp.zeros_like(acc_ref)
            dpw_ref[...] = jnp.zeros_like(dpw_ref)
            dps_ref[...] = jnp.zeros_like(dps_ref)

        xf = x_ref[...]
        r = lax.rsqrt(jnp.mean(xf * xf, axis=-1, keepdims=True) + EPS)
        h = ((xf * r) * g_ref[...]).astype(BF16)
        col = 2 * POOL_WIDTH
        for ref in (dq_ref, dk_ref, dv_ref, dag_ref):
            acc_ref[:, col:col + ATTN_WIDTH] += _tn(h, ref[...])
            col += ATTN_WIDTH
        _pool_bwd_tile(i, n_tiles, cur_ref, prev_ref, pgn_ref, dy_ref, dyn_ref, pw_ref, ps_ref, dp_ref, dpw_ref, dps_ref)
        for c0 in (0, POOL_WIDTH):
            acc_ref[:, c0:c0 + POOL_WIDTH] += _tn(h, dp_ref[:, c0:c0 + POOL_WIDTH])

        @pl.when(i == n_tiles - 1)
        def _():
            for d in range(N_DEV):
                out_ref[d] = acc_ref[:, d * IN_SHARD:(d + 1) * IN_SHARD].astype(BF16)

    tile = lambda width: pl.BlockSpec((TOKEN_TILE, width), lambda i: (i, 0))
    next_halo = lambda col: pl.BlockSpec(
        (HALO, POOL_WIDTH), lambda i: (jnp.minimum((i + 1) * halo_per_tile, last_halo), col))
    return pl.pallas_call(
        body,
        name="inproj_bwd_dw",
        grid=(n_tiles,),
        out_shape=(
            jax.ShapeDtypeStruct((N_DEV, D_MODEL, IN_SHARD), BF16),
            jax.ShapeDtypeStruct((t, 2 * POOL_WIDTH), BF16),
            jax.ShapeDtypeStruct((N_GROUPS, GROUP_DIM, GROUP_DIM), F32),
            jax.ShapeDtypeStruct((1, POOL_WIDTH), F32),
        ),
        in_specs=[
            tile(D_MODEL),
            tile(2 * POOL_WIDTH),
            pl.BlockSpec((HALO, POOL_WIDTH), lambda i: (jnp.maximum(i * halo_per_tile - 1, 0), 0)),
            next_halo(1),
            tile(POOL_WIDTH),
            next_halo(0),
            tile(ATTN_WIDTH), tile(ATTN_WIDTH), tile(ATTN_WIDTH), tile(ATTN_WIDTH),
            pl.BlockSpec((1, D_MODEL), lambda i: (0, 0)),
            pl.BlockSpec((N_GROUPS, GROUP_DIM, GROUP_DIM), lambda i: (0, 0, 0)),
            pl.BlockSpec((1, POOL_WIDTH), lambda i: (0, 0)),
        ],
        out_specs=(
            pl.BlockSpec((N_DEV, D_MODEL, IN_SHARD), lambda i: (0, 0, 0)),
            tile(2 * POOL_WIDTH),
            pl.BlockSpec((N_GROUPS, GROUP_DIM, GROUP_DIM), lambda i: (0, 0, 0)),
            pl.BlockSpec((1, POOL_WIDTH), lambda i: (0, 0)),
        ),
        scratch_shapes=[pltpu.VMEM((D_MODEL, IN_WIDTH), F32)],
        compiler_params=_params(("arbitrary",)),
    )(x2d, pvg, pvg, pvg, dy_pool, dy_pool, *dattn, g1, pool_w, pool_scale)


def _adamw(w, g, m, v):
    m = ADAM_B1 * m + (1.0 - ADAM_B1) * g
    v = ADAM_B2 * v + (1.0 - ADAM_B2) * (g * g)
    m_hat = m / (1.0 - ADAM_B1 ** ADAM_STEP)
    v_hat = v / (1.0 - ADAM_B2 ** ADAM_STEP)
    delta = -ADAM_LR * (m_hat / (jnp.sqrt(v_hat) + ADAM_EPS) + ADAM_WD * w)
    return delta, m, v


_ROW_G1, _ROW_G2, _ROW_PS, _ROW_LOSS, _ROW_RB = 0, 1, 2, 3, 8
_VEC_ROWS = 16


def _reduce_adamw(dwin_g, land_in, dwout_g, land_out, grads, weights, big):
    small_shapes = [(1, D_MODEL), (1, D_MODEL), (1, POOL_WIDTH), (N_HEADS, 2 * LANES), (N_GROUPS, GROUP_DIM, GROUP_DIM)]

    def body(*refs):
        refs = list(refs)
        take = lambda n: [refs.pop(0) for _ in range(n)]
        dwin_hbm, lin_ref, dwout_hbm, lout_ref = take(4)
        g1_ref, g2_ref, ps_ref, rb_ref, pw_ref, loss_ref = take(6)
        small_wmv = [take(3) for _ in range(5)]
        big_wmv = [take(3) for _ in range(2)]
        big_out = [take(4) for _ in range(2)]
        small_out = [take(4) for _ in range(5)]
        (loss_out,) = take(1)
        own_in, own_out, vec_ref, rvec_ref, rpw_ref, send_sems, recv_sems, local_sems = refs

        me = _dev_index(_mesh_pos())
        vec_ref[...] = jnp.zeros_like(vec_ref)
        vec_ref[_ROW_G1:_ROW_G1 + 1, :] = g1_ref[...]
        vec_ref[_ROW_G2:_ROW_G2 + 1, :] = g2_ref[...]
        vec_ref[_ROW_PS:_ROW_PS + 1, :POOL_WIDTH] = ps_ref[...]
        vec_ref[_ROW_LOSS:_ROW_LOSS + 1, :LANES] = loss_ref[0:1, :]
        vec_ref[_ROW_RB:_ROW_RB + N_HEADS, :2 * LANES] = rb_ref[...]

        def send(r, src, land, k):
            return pltpu.make_async_remote_copy(
                src_ref=src, dst_ref=land.at[r], send_sem=send_sems.at[2 * (r - 1) + k],
                recv_sem=recv_sems.at[2 * (r - 1) + k], device_id=_peer(r), device_id_type=MESH_ID)

        started = [cp for r in range(1, N_DEV) for cp in (send(r, vec_ref, rvec_ref, 0), send(r, pw_ref, rpw_ref, 1))]
        for cp in started:
            cp.start()
        mine = [pltpu.make_async_copy(dwin_hbm.at[me], own_in, local_sems.at[0]),
                pltpu.make_async_copy(dwout_hbm.at[me], own_out, local_sems.at[1])]
        for cp in mine:
            cp.start()
        rvec_ref[0] = vec_ref[...]
        rpw_ref[0] = pw_ref[...]
        for cp in mine:
            cp.wait()

        def update(g, wmv, outs):
            delta, m_new, v_new = _adamw(wmv[0][...], g, wmv[1][...], wmv[2][...])
            for ref, val in zip(outs, (g, delta, m_new, v_new)):
                ref[...] = val

        g_in = own_in[...].astype(F32)
        g_out = own_out[...].astype(F32)
        for r in range(N_DEV - 1):
            g_in = g_in + lin_ref[r].astype(F32)
            g_out = g_out + lout_ref[r].astype(F32)
        update(g_in, big_wmv[0], big_out[0])
        update(g_out, big_wmv[1], big_out[1])

        for cp in started:
            cp.wait_recv()
        for cp in started:
            cp.wait_send()
        vec = rvec_ref[me]
        pw = rpw_ref[me]
        for s in range(1, N_DEV):
            vec = vec + rvec_ref[me ^ s]
            pw = pw + rpw_ref[me ^ s]
        vec_ref[...] = vec
        update(vec_ref[_ROW_G1:_ROW_G1 + 1, :], small_wmv[0], small_out[0])
        update(vec_ref[_ROW_G2:_ROW_G2 + 1, :], small_wmv[1], small_out[1])
        update(vec_ref[_ROW_PS:_ROW_PS + 1, :POOL_WIDTH], small_wmv[2], small_out[2])
        update(vec_ref[_ROW_RB:_ROW_RB + N_HEADS, :2 * LANES], small_wmv[3], small_out[3])
        update(pw, small_wmv[4], small_out[4])
        loss_out[...] = jnp.broadcast_to(vec_ref[_ROW_LOSS:_ROW_LOSS + 1, :LANES], loss_out.shape)

    vm = pl.BlockSpec(memory_space=pltpu.VMEM)
    hbm = pl.BlockSpec(memory_space=pl.ANY)
    f32 = lambda shape: jax.ShapeDtypeStruct(shape, F32)
    out_shapes = [f32((D_MODEL, IN_SHARD))] * 4 + [f32((OUT_SHARD, D_MODEL))] * 4
    for shape in small_shapes:
        out_shapes += [f32(shape)] * 4
    out_shapes.append(f32((8, LANES)))
    args = [dwin_g, land_in, dwout_g, land_out, *grads]
    for wmv in weights:
        args += list(wmv)
    for wmv in big:
        args += list(wmv)
    return pl.pallas_call(
        body,
        name="reduce_adamw",
        out_shape=tuple(out_shapes),
        in_specs=[hbm, vm, hbm, vm] + [vm] * (len(args) - 4),
        out_specs=tuple([vm] * len(out_shapes)),
        scratch_shapes=[
            pltpu.VMEM((D_MODEL, IN_SHARD), BF16),
            pltpu.VMEM((OUT_SHARD, D_MODEL), BF16),
            pltpu.VMEM((_VEC_ROWS, D_MODEL), F32),
            pltpu.VMEM((N_DEV, _VEC_ROWS, D_MODEL), F32),
            pltpu.VMEM((N_DEV, N_GROUPS, GROUP_DIM, GROUP_DIM), F32),
            pltpu.SemaphoreType.DMA((2 * (N_DEV - 1),)),
            pltpu.SemaphoreType.DMA((2 * (N_DEV - 1),)),
            pltpu.SemaphoreType.DMA((2,)),
        ],
        compiler_params=_params(),
    )(*args)


def kernel(x, norm_gain, w_in, pool_w, pool_scale, rel_bias, w_out, final_norm_gain, loss_target, m_norm_gain, m_w_in, m_pool_w, m_pool_scale, m_rel_bias, m_w_out, m_final_norm_gain, v_norm_gain, v_w_in, v_pool_w, v_pool_scale, v_rel_bias, v_w_out, v_final_norm_gain):
    t = x.shape[1]
    assert x.shape[0] == 1 and t % TOKEN_TILE == 0 and t // Q_BLOCK >= 4
    x2d = x[0]
    tgt2d = loss_target[0]
    g2 = final_norm_gain.reshape(1, D_MODEL)

    rb = rel_bias[0]
    rel_line = jnp.concatenate([
        jnp.broadcast_to(rb[:, :1], (N_HEADS, _REL_FIRST)), rb,
        jnp.broadcast_to(rb[:, N_REL - 1:], (N_HEADS, TOEPLITZ - _REL_FIRST - N_REL)),
    ], axis=1).reshape(N_HEADS, 1, TOEPLITZ)
    wg_in, bias_tile = _gather_weights(w_in[0], rel_line)

    pvg, qkv, ag, y_pool = _norm_inproj(x2d, norm_gain, wg_in, pool_w[0], pool_scale)
    o, y_attn, wg_out = _attn_fwd(qkv, ag, bias_tile, w_out[0])
    dx2, dy_pool, dy_attn, dwout_g, d_g2, loss_sum = _outproj_loss(x2d, tgt2d, y_pool, y_attn, wg_out, g2)
    dq, dk, dv, dag, ds_band, ds_total, land_out = _attn_bwd(qkv, ag, o, dy_attn, bias_tile, dwout_g)
    d_rb = _bias_grad(ds_band, ds_total)
    dwin_g, d_pool, d_pw, d_ps = _inproj_bwd_dw(x2d, pvg, dy_pool, (dq, dk, dv, dag), norm_gain, pool_w[0], pool_scale)
    dproj = (d_pool, dq, dk, dv, dag)
    grad_x, d_g1, land_in = _inproj_bwd_dx(x2d, dx2, dproj, norm_gain, wg_in, dwin_g)

    pad_rb = lambda a: jnp.pad(a[0], ((0, 0), (0, 2 * LANES - N_REL)))
    row = lambda a: a.reshape(1, D_MODEL)
    weights = [
        (norm_gain, m_norm_gain, v_norm_gain),
        (row(final_norm_gain), row(m_final_norm_gain), row(v_final_norm_gain)),
        (pool_scale, m_pool_scale, v_pool_scale),
        (pad_rb(rel_bias), pad_rb(m_rel_bias), pad_rb(v_rel_bias)),
        (pool_w[0], m_pool_w[0], v_pool_w[0]),
    ]
    big = [(w_in[0], m_w_in[0], v_w_in[0]), (w_out[0], m_w_out[0], v_w_out[0])]
    res = _reduce_adamw(dwin_g, land_in, dwout_g, land_out, (d_g1, d_g2, d_ps, d_rb, d_pw, loss_sum), weights, big)
    loss = 0.5 * res[28][0, 0]

    def leaves(k):
        g1_, g2_, ps_, rb_, pw_ = (res[8 + 4 * leaf + k] for leaf in range(5))
        return [g1_, res[k][None], pw_[None], ps_, rb_[None, :, :N_REL], res[4 + k][None], g2_.reshape(D_MODEL)]

    return (loss, grad_x[None], *leaves(0), *leaves(1), *leaves(2), *leaves(3))
```

```python
import math

import jax
import jax.numpy as jnp
from jax import lax
from jax.experimental import pallas as pl
from jax.experimental.pallas import tpu as pltpu

F32 = jnp.float32
BF16 = jnp.bfloat16
MESH_ID = pl.DeviceIdType.MESH

D_MODEL = 1024
POOL_WIDTH = 512
ATTN_WIDTH = 512
POOL_WINDOWS = (2, 4, 8, 16)
N_GROUPS = 4
GROUP_DIM = 128
HEAD_DIM = 64
N_HEADS = 8
CHUNK = 64
LEFT_CHUNKS = 8
MAX_REL = 64
N_REL = 2 * MAX_REL + 1
IN_WIDTH = 2 * POOL_WIDTH + 4 * ATTN_WIDTH
EPS = 1e-6
MASK_VALUE = -1e30
ATTN_SCALE = 1.0 / math.sqrt(HEAD_DIM)
ADAM_LR = 0.001
ADAM_B1 = 0.9
ADAM_B2 = 0.999
ADAM_EPS = 1e-08
ADAM_WD = 0.01
ADAM_STEP = 10

N_DEV = 8
IN_SHARD = IN_WIDTH // N_DEV
OUT_SHARD = D_MODEL // N_DEV

LANES = 128
TOKEN_TILE = 512
HALO = 16
Q_BLOCK = 256
KV_BLOCKS = 3
KV_WINDOW = KV_BLOCKS * Q_BLOCK
PAIR = 2 * HEAD_DIM
N_PAIRS = N_HEADS // 2
TOEPLITZ = 1024
VMEM_LIMIT = 56 * 1024 * 1024


def _params(sem=None, vmem=VMEM_LIMIT):
    return pltpu.CompilerParams(dimension_semantics=sem, vmem_limit_bytes=vmem)


def _sigmoid(x):
    return 1.0 / (1.0 + jnp.exp(-x))


def _nt(a, b):
    return lax.dot_general(a, b, (((1,), (1,)), ((), ())), preferred_element_type=F32)


def _tn(a, b):
    return lax.dot_general(a, b, (((0,), (0,)), ((), ())), preferred_element_type=F32)


def _nn(a, b):
    return jnp.dot(a, b, preferred_element_type=F32)


def _mesh_pos():
    return lax.axis_index("x"), lax.axis_index("y"), lax.axis_index("c")


def _dev_index(p):
    return 4 * p[0] + 2 * p[1] + p[2]


def _peer(r):
    x, y, c = _mesh_pos()
    return (x ^ ((r >> 2) & 1), y ^ ((r >> 1) & 1), c ^ (r & 1))


def _exchange_copies(src_hbm, land_hbm, send_sems, recv_sems):
    return [
        pltpu.make_async_remote_copy(
            src_ref=src_hbm.at[_dev_index(_peer(r))], dst_ref=land_hbm.at[r - 1],
            send_sem=send_sems.at[r - 1], recv_sem=recv_sems.at[r - 1],
            device_id=_peer(r), device_id_type=MESH_ID)
        for r in range(1, N_DEV)
    ]


def _gather_weights(w_in_shard, rel_line):
    def body(win_ref, line_ref, gin_ref, bias_ref, sin_ref, send_sems, recv_sems):
        x, y, c = _mesh_pos()
        me, sibling = (x, y, c), (x, y, 1 - c)
        chips = [(1 - x, y), (x, 1 - y), (1 - x, 1 - y)]

        sin_ref[...] = win_ref[...].astype(BF16)
        gin_ref[_dev_index(me)] = sin_ref[...]

        def copy(k, block, to, from_shard=False):
            return pltpu.make_async_remote_copy(
                src_ref=sin_ref if from_shard else gin_ref.at[_dev_index(block)],
                dst_ref=gin_ref.at[_dev_index(block)],
                send_sem=send_sems.at[k],
                recv_sem=recv_sems.at[k],
                device_id=to,
                device_id_type=MESH_ID,
            )

        first = [copy(0, me, sibling, True)]
        first += [copy(1 + j, me, (*chip, c), True) for j, chip in enumerate(chips)]
        for cp in first:
            cp.start()
        passed = [copy(4 + j, (*chip, c), sibling) for j, chip in enumerate(chips)]

        def bias_heads(lo, hi):
            for h in range(lo, hi):
                bias_ref[h] = _toeplitz_bias(line_ref[h])

        bias_heads(0, N_HEADS - 3)
        for j, chip in enumerate(chips):
            copy(1 + j, (*chip, c), me).wait_recv()
            passed[j].start()
            bias_heads(N_HEADS - 3 + j, N_HEADS - 2 + j)
        copy(0, sibling, me).wait_recv()
        for j, chip in enumerate(chips):
            copy(4 + j, (*chip, 1 - c), me).wait_recv()
        for cp in first + passed:
            cp.wait_send()

    vm = pl.BlockSpec(memory_space=pltpu.VMEM)
    return pl.pallas_call(
        body,
        name="gather_weights",
        out_shape=(
            jax.ShapeDtypeStruct((N_DEV, D_MODEL, IN_SHARD), BF16),
            jax.ShapeDtypeStruct((N_HEADS, Q_BLOCK, KV_WINDOW), F32),
        ),
        in_specs=[vm, vm],
        out_specs=(vm, vm),
        scratch_shapes=[
            pltpu.VMEM((D_MODEL, IN_SHARD), BF16),
            pltpu.SemaphoreType.DMA((7,)),
            pltpu.SemaphoreType.DMA((7,)),
        ],
        compiler_params=_params(),
    )(w_in_shard, rel_line)


def _load_w_in(wg_hbm, wfull_ref, sem):
    copies = [
        pltpu.make_async_copy(wg_hbm.at[d], wfull_ref.at[:, d * IN_SHARD:(d + 1) * IN_SHARD], sem.at[d])
        for d in range(N_DEV)
    ]
    for cp in copies:
        cp.start()
    for cp in copies:
        cp.wait()


def _norm_inproj(x2d, g1, wg, pool_w, pool_scale):
    t = x2d.shape[0]

    def body(x_ref, g_ref, wg_hbm, pw_ref, ps_ref, pvg_ref, qkv_ref, ag_ref, yp_ref, wfull_ref, halo_ref, sem):
        i = pl.program_id(0)

        @pl.when(i == 0)
        def _():
            _load_w_in(wg_hbm, wfull_ref, sem)
            halo_ref[...] = jnp.zeros_like(halo_ref)

        xf = x_ref[...]
        r = lax.rsqrt(jnp.mean(xf * xf, axis=-1, keepdims=True) + EPS)
        h = ((xf * r) * g_ref[...]).astype(BF16)
        chunk = lambda ci: _nn(h, wfull_ref[:, ci * POOL_WIDTH:(ci + 1) * POOL_WIDTH])
        pv, pg = chunk(0), chunk(1)
        pvg_ref[:, :POOL_WIDTH] = pv
        pvg_ref[:, POOL_WIDTH:] = pg
        halo = halo_ref[...]
        halo_ref[...] = pv[TOKEN_TILE - HALO:]
        for gi in range(N_GROUPS):
            sl = slice(gi * GROUP_DIM, (gi + 1) * GROUP_DIM)
            d = _pool_diffs(pv, halo, i * TOKEN_TILE, gi)
            z = _nn(d.astype(BF16), pw_ref[gi].astype(BF16))
            g = pg[:, sl]
            yp_ref[:, sl] = ((z * ps_ref[:, sl]) * (g * _sigmoid(g))).astype(BF16)
        for ci in range(2, 5):
            qkv_ref[:, (ci - 2) * POOL_WIDTH:(ci - 1) * POOL_WIDTH] = chunk(ci).astype(BF16)
        ag_ref[...] = chunk(5)

    tile = lambda width: pl.BlockSpec((TOKEN_TILE, width), lambda i: (i, 0))
    return pl.pallas_call(
        body,
        name="norm_inproj",
        grid=(t // TOKEN_TILE,),
        out_shape=(
            jax.ShapeDtypeStruct((t, 2 * POOL_WIDTH), F32),
            jax.ShapeDtypeStruct((t, 3 * ATTN_WIDTH), BF16),
            jax.ShapeDtypeStruct((t, ATTN_WIDTH), F32),
            jax.ShapeDtypeStruct((t, POOL_WIDTH), BF16),
        ),
        in_specs=[
            tile(D_MODEL),
            pl.BlockSpec((1, D_MODEL), lambda i: (0, 0)),
            pl.BlockSpec(memory_space=pl.ANY),
            pl.BlockSpec((N_GROUPS, GROUP_DIM, GROUP_DIM), lambda i: (0, 0, 0)),
            pl.BlockSpec((1, POOL_WIDTH), lambda i: (0, 0)),
        ],
        out_specs=(tile(2 * POOL_WIDTH), tile(3 * ATTN_WIDTH), tile(ATTN_WIDTH), tile(POOL_WIDTH)),
        scratch_shapes=[
            pltpu.VMEM((D_MODEL, IN_WIDTH), BF16),
            pltpu.VMEM((HALO, POOL_WIDTH), F32),
            pltpu.SemaphoreType.DMA((N_DEV,)),
        ],
        compiler_params=_params(("arbitrary",)),
    )(x2d, g1, wg, pool_w, pool_scale)


def _inv_count(first_row, rows, window):
    tpos = first_row + lax.broadcasted_iota(jnp.int32, (rows, 1), 0)
    return 1.0 / jnp.minimum(tpos + 1, window).astype(F32)


def _causal_window_sum(ext, window):
    s, k = ext, 1
    while k < window:
        s = s + pltpu.roll(s, k, 0)
        k *= 2
    return s


def _pool_diffs(pv, halo, first_row, gi):
    w = POOL_WINDOWS[gi]
    sl = slice(gi * GROUP_DIM, (gi + 1) * GROUP_DIM)
    ext = jnp.concatenate([halo[:, sl], pv[:, sl]], axis=0)
    s = _causal_window_sum(ext, w)[HALO:]
    return s * _inv_count(first_row, pv.shape[0], w) - pv[:, sl]


def _pool_bwd_tile(i, n_tiles, cur_ref, prev_ref, pgn_ref, dy_ref, dyn_ref, pw_ref, ps_ref, dp_ref, dpw_ref, dps_ref):
    pv = cur_ref[:, :POOL_WIDTH]
    pg = cur_ref[:, POOL_WIDTH:]
    prev = jnp.where(i > 0, prev_ref[...], 0.0)
    has_next = i < n_tiles - 1
    rows = TOKEN_TILE + HALO
    for gi in range(N_GROUPS):
        w = POOL_WINDOWS[gi]
        sl = slice(gi * GROUP_DIM, (gi + 1) * GROUP_DIM)
        pw = pw_ref[gi].astype(BF16)
        ps = ps_ref[:, sl]
        d = _pool_diffs(pv, prev, i * TOKEN_TILE, gi).astype(BF16)
        z = _nn(d, pw)
        g_ext = jnp.concatenate([pg[:, sl], pgn_ref[:, sl]], axis=0)
        dy_ext = jnp.concatenate([dy_ref[:, sl], dyn_ref[:, sl]], axis=0).astype(F32)
        sig = _sigmoid(g_ext)
        gate = g_ext * sig
        dz_ext = ((dy_ext * gate) * ps).astype(BF16)
        dd_ext = _nt(dz_ext, pw)
        e = dd_ext * _inv_count(i * TOKEN_TILE, rows, w)
        row = lax.broadcasted_iota(jnp.int32, (rows, 1), 0)
        e = jnp.where(jnp.logical_or(row < TOKEN_TILE, has_next), e, 0.0)
        s, k = e, 1
        while k < w:
            s = s + pltpu.roll(s, rows - k, 0)
            k *= 2
        dp_ref[:, sl] = (s[:TOKEN_TILE] - dd_ext[:TOKEN_TILE]).astype(BF16)
        dy = dy_ext[:TOKEN_TILE]
        g = g_ext[:TOKEN_TILE]
        sg = sig[:TOKEN_TILE]
        dgate = sg * (1.0 + g * (1.0 - sg))
        dp_ref[:, POOL_WIDTH + gi * GROUP_DIM:POOL_WIDTH + (gi + 1) * GROUP_DIM] = (
            (dy * (z * ps)) * dgate).astype(BF16)
        dps_ref[:, sl] += jnp.sum((dy * gate[:TOKEN_TILE]) * z, axis=0, keepdims=True)
        dpw_ref[gi] += _tn(d, dz_ext[:TOKEN_TILE])


_REL_FIRST = KV_WINDOW - 1 - MAX_REL


def _skew_rows(a, right):
    rows, lanes = a.shape
    row = lax.broadcasted_iota(jnp.int32, a.shape, 0)
    for b in range(rows.bit_length() - 1):
        shift = (1 << b) if right else lanes - (1 << b)
        a = jnp.where((row >> b) & 1 == 1, pltpu.roll(a, shift, 1), a)
    return a


def _toeplitz_bias(line):
    a = jnp.broadcast_to(line, (Q_BLOCK, TOEPLITZ))
    a = _skew_rows(a, True)
    a = pltpu.roll(a, TOEPLITZ - (Q_BLOCK - 1), 1)
    a = a[:, :KV_WINDOW]
    qc = lax.broadcasted_iota(jnp.int32, a.shape, 0) // CHUNK
    kc = lax.broadcasted_iota(jnp.int32, a.shape, 1) // CHUNK
    visible = jnp.logical_and(kc >= qc, kc <= qc + LEFT_CHUNKS)
    return jnp.where(visible, a, MASK_VALUE)


_BAND_WIDTH = 2 * LANES
_BAND_START = (384, 512, 512, 512)
_BAND_REL = tuple(a - (KV_BLOCKS - 1) * Q_BLOCK - qc * CHUNK for qc, a in enumerate(_BAND_START))


def _bias_grad(band, total):
    def body(band_ref, total_ref, out_ref):
        for h in range(N_HEADS):
            a = jnp.zeros((CHUNK, 2 * _BAND_WIDTH), F32)
            for qc in range(Q_BLOCK // CHUNK):
                z = jnp.concatenate([band_ref[h, qc], jnp.zeros((CHUNK, _BAND_WIDTH), F32)], axis=1)
                left = -MAX_REL - _BAND_REL[qc]
                a = a + (pltpu.roll(z, 2 * _BAND_WIDTH - left, 1) if left else z)
            a = _skew_rows(a, False)
            near = jnp.sum(a, axis=0, keepdims=True)[:, :2 * LANES]
            r = lax.broadcasted_iota(jnp.int32, near.shape, 1)
            near = jnp.where(jnp.logical_and(r >= 1, r < 2 * MAX_REL), near, 0.0)
            everything = jnp.sum(jnp.sum(total_ref[h], axis=0, keepdims=True), axis=1, keepdims=True)
            far = everything - jnp.sum(near, axis=1, keepdims=True)
            out_ref[h:h + 1, :] = jnp.where(r == 0, far, near)

    vm = pl.BlockSpec(memory_space=pltpu.VMEM)
    return pl.pallas_call(
        body,
        name="bias_grad",
        out_shape=jax.ShapeDtypeStruct((N_HEADS, 2 * LANES), F32),
        in_specs=[vm, vm],
        out_specs=vm,
        compiler_params=_params(),
    )(band, total)


def _head_lanes(hh):
    lane = lax.broadcasted_iota(jnp.int32, (1, PAIR), 1)
    return (lane < HEAD_DIM) if hh == 0 else (lane >= HEAD_DIM)


def _score_windows(nwin):
    if nwin < KV_BLOCKS:
        return [(slice(0, Q_BLOCK), slice(0, nwin * Q_BLOCK), None)]
    pieces = []
    for qc in range(Q_BLOCK // CHUNK):
        rows = slice(qc * CHUNK, (qc + 1) * CHUNK)
        if qc < 2:
            pieces.append((rows, slice(0, KV_WINDOW - LANES), slice(KV_WINDOW - LANES, KV_WINDOW)))
        else:
            pieces.append((rows, slice(LANES, KV_WINDOW), slice(0, LANES)))
    return pieces


def _block_rows(first_block, n_blocks):
    if isinstance(first_block, int):
        return pl.ds(first_block * Q_BLOCK, n_blocks * Q_BLOCK)
    return pl.ds(pl.multiple_of(first_block * Q_BLOCK, Q_BLOCK), n_blocks * Q_BLOCK)


def _attn_fwd(qkv, ag, bias_tile, w_out_shard):
    t = qkv.shape[0]
    nb = t // Q_BLOCK

    def body(q_ref, k_ref, v_ref, ag_ref, bias_ref, wout_ref, o_ref, y_ref, gout_hbm,
             s_scr, p_scr, linv_scr, sout_ref, send_sems, recv_sems, local_sem):
        pair = pl.program_id(0)
        me = _dev_index(_mesh_pos())
        mine = pltpu.make_async_copy(sout_ref, gout_hbm.at[me], local_sem)

        def shard_copy(r, block):
            return pltpu.make_async_remote_copy(
                src_ref=sout_ref, dst_ref=gout_hbm.at[block], send_sem=send_sems.at[r - 1],
                recv_sem=recv_sems.at[r - 1], device_id=_peer(r), device_id_type=MESH_ID)

        @pl.when(pair == 0)
        def _():
            sout_ref[...] = wout_ref[...].astype(BF16)
            mine.start()
            for r in range(1, N_DEV):
                shard_copy(r, me).start()

        def scores(i, slot, nwin):
            q = q_ref[_block_rows(i, 1), :]
            kw = k_ref[_block_rows(i + 1 - nwin, nwin), :]
            for hh in range(2):
                q_h = jnp.where(_head_lanes(hh), q, jnp.zeros_like(q)) * ATTN_SCALE
                s_scr[slot, hh, :, :nwin * Q_BLOCK] = _nt(q_h, kw)

        def softmax(slot, nwin):
            off = (KV_BLOCKS - nwin) * Q_BLOCK
            for hh in range(2):
                for rows, cols, rest in _score_windows(nwin):
                    s = s_scr[slot, hh, rows, cols] + bias_ref[hh, rows, off + cols.start:off + cols.stop]
                    e = jnp.exp(s - jnp.max(s, axis=-1, keepdims=True))
                    linv_scr[slot, hh, rows, :] = 1.0 / jnp.sum(e, axis=-1, keepdims=True)
                    p_scr[slot, hh, rows, cols] = e.astype(BF16)
                    if rest is not None:
                        p_scr[slot, hh, rows, rest] = jnp.zeros((CHUNK, LANES), BF16)

        def output(i, slot, nwin):
            vw = v_ref[_block_rows(i + 1 - nwin, nwin), :]
            outs = [_nn(p_scr[slot, hh, :, :nwin * Q_BLOCK], vw) * linv_scr[slot, hh] for hh in range(2)]
            o = jnp.where(_head_lanes(0), outs[0], outs[1])
            g = ag_ref[_block_rows(i, 1), :]
            o_ref[_block_rows(i, 1), :] = o.astype(BF16)
            y_ref[_block_rows(i, 1), :] = (o * (g * _sigmoid(g))).astype(BF16)

        scores(0, 0, 1)
        scores(1, 1, 2)
        softmax(0, 1)
        scores(2, 0, 3)
        softmax(1, 2)
        output(0, 0, 1)
        scores(3, 1, 3)
        softmax(0, 3)
        output(1, 1, 2)

        def two_steps(k, carry):
            i = 3 + 2 * k
            scores(i + 1, 0, KV_BLOCKS)
            softmax(1, KV_BLOCKS)
            output(i - 1, 0, KV_BLOCKS)
            scores(i + 2, 1, KV_BLOCKS)
            softmax(0, KV_BLOCKS)
            output(i, 1, KV_BLOCKS)
            return carry

        lax.fori_loop(0, (nb - 4) // 2, two_steps, 0)
        last = (nb - 1) % 2
        softmax(last, KV_BLOCKS)
        output(nb - 2, 1 - last, KV_BLOCKS)
        output(nb - 1, last, KV_BLOCKS)

        @pl.when(pair == N_PAIRS - 1)
        def _():
            for r in range(1, N_DEV):
                shard_copy(r, _dev_index(_peer(r))).wait_recv()
            for r in range(1, N_DEV):
                shard_copy(r, me).wait_send()
            mine.wait()

    col = lambda c0: pl.BlockSpec((t, PAIR), lambda j: (0, c0 + j))
    return pl.pallas_call(
        body,
        name="attn_fwd",
        grid=(N_PAIRS,),
        out_shape=(
            jax.ShapeDtypeStruct((t, ATTN_WIDTH), BF16),
            jax.ShapeDtypeStruct((t, ATTN_WIDTH), BF16),
            jax.ShapeDtypeStruct((N_DEV, OUT_SHARD, D_MODEL), BF16),
        ),
        in_specs=[col(0), col(N_PAIRS), col(2 * N_PAIRS), col(0),
                  pl.BlockSpec((2, Q_BLOCK, KV_WINDOW), lambda j: (j, 0, 0)),
                  pl.BlockSpec((OUT_SHARD, D_MODEL), lambda j: (0, 0))],
        out_specs=(col(0), col(0), pl.BlockSpec(memory_space=pl.ANY)),
        scratch_shapes=[
            pltpu.VMEM((2, 2, Q_BLOCK, KV_WINDOW), F32),
            pltpu.VMEM((2, 2, Q_BLOCK, KV_WINDOW), BF16),
            pltpu.VMEM((2, 2, Q_BLOCK, 1), F32),
            pltpu.VMEM((OUT_SHARD, D_MODEL), BF16),
            pltpu.SemaphoreType.DMA((N_DEV - 1,)),
            pltpu.SemaphoreType.DMA((N_DEV - 1,)),
            pltpu.SemaphoreType.DMA,
        ],
        compiler_params=_params(("arbitrary",)),
    )(qkv, qkv, qkv, ag, bias_tile, w_out_shard)


def _attn_bwd(qkv, ag, o, dy_attn, bias_tile, dwout_g):
    t = qkv.shape[0]
    nb = t // Q_BLOCK

    def body(q_ref, k_ref, v_ref, ag_ref, o_ref, dy_ref, bias_ref, dwout_hbm,
             dq_ref, dk_ref, dv_ref, dag_ref, band_ref, total_ref, land_hbm,
             do_scr, s_scr, dp_scr, p_scr, dsb_scr, dq_scr, dk_acc, dv_acc, send_sems, recv_sems):
        exchange = _exchange_copies(dwout_hbm, land_hbm, send_sems, recv_sems)

        @pl.when(pl.program_id(0) == 0)
        def _():
            for cp in exchange:
                cp.start()

        def gates(i, carry):
            rows = _block_rows(i, 1)
            g = ag_ref[rows, :]
            sig = _sigmoid(g)
            dy = dy_ref[rows, :].astype(F32)
            do_scr[rows, :] = (dy * (g * sig)).astype(BF16)
            dag_ref[rows, :] = ((dy * o_ref[rows, :].astype(F32)) * (sig * (1.0 + g * (1.0 - sig)))).astype(BF16)
            return carry

        lax.fori_loop(0, nb, gates, 0)
        band_ref[...] = jnp.zeros_like(band_ref)
        total_ref[...] = jnp.zeros_like(total_ref)

        def nwin_of(i):
            return min(i + 1, KV_BLOCKS) if isinstance(i, int) else KV_BLOCKS

        def operands(i, hh):
            lanes = _head_lanes(hh)
            q = q_ref[_block_rows(i, 1), :]
            do = do_scr[_block_rows(i, 1), :]
            return (jnp.where(lanes, q, jnp.zeros_like(q)) * ATTN_SCALE, jnp.where(lanes, do, jnp.zeros_like(do)))

        def products(i, hh):
            nwin = nwin_of(i)
            win = _block_rows(i + 1 - nwin, nwin)
            q_h, do_h = operands(i, hh)
            s_scr[hh, :, :nwin * Q_BLOCK] = _nt(q_h, k_ref[win, :])
            dp_scr[hh, :, :nwin * Q_BLOCK] = _nt(do_h, v_ref[win, :])

        def grads(i, hh):
            nwin = nwin_of(i)
            off = (KV_BLOCKS - nwin) * Q_BLOCK
            for rows, cols, rest in _score_windows(nwin):
                bias_cols = slice(off + cols.start, off + cols.stop)
                s = s_scr[hh, rows, cols] + bias_ref[hh, rows, bias_cols]
                e = jnp.exp(s - jnp.max(s, axis=-1, keepdims=True))
                p = e * (1.0 / jnp.sum(e, axis=-1, keepdims=True))
                dp = dp_scr[hh, rows, cols]
                ds = p * (dp - jnp.sum(p * dp, axis=-1, keepdims=True))
                total_ref[hh, rows, :] += sum(ds[:, c0:c0 + LANES] for c0 in range(0, ds.shape[1], LANES))
                for qc in range(rows.start // CHUNK, rows.stop // CHUNK):
                    lo = max(_BAND_START[qc], bias_cols.start)
                    hi = min(_BAND_START[qc] + _BAND_WIDTH, bias_cols.stop)
                    if lo < hi:
                        band_ref[hh, qc, :, lo - _BAND_START[qc]:hi - _BAND_START[qc]] += ds[
                            qc * CHUNK - rows.start:(qc + 1) * CHUNK - rows.start,
                            lo - bias_cols.start:hi - bias_cols.start]
                p_scr[hh, rows, cols] = p.astype(BF16)
                dsb_scr[hh, rows, cols] = ds.astype(BF16)
                if rest is not None:
                    p_scr[hh, rows, rest] = jnp.zeros((CHUNK, LANES), BF16)
                    dsb_scr[hh, rows, rest] = jnp.zeros((CHUNK, LANES), BF16)

        def ring(block):
            return block % KV_BLOCKS if isinstance(block, int) else lax.rem(block, KV_BLOCKS)

        def accumulate(i, hh):
            nwin = nwin_of(i)
            w = nwin * Q_BLOCK
            win = _block_rows(i + 1 - nwin, nwin)
            q_h, do_h = operands(i, hh)
            ds_b = dsb_scr[hh, :, :w]
            dq_h = _nn(ds_b, k_ref[win, :]) * ATTN_SCALE
            dkw = _tn(ds_b, q_h)
            dvw = _tn(p_scr[hh, :, :w], do_h)
            for b in range(nwin):
                slot = ring(i + 1 - nwin + b)
                part = slice(b * Q_BLOCK, (b + 1) * Q_BLOCK)
                if hh == 0 and b == nwin - 1:
                    dk_acc[slot] = dkw[part]
                    dv_acc[slot] = dvw[part]
                else:
                    dk_acc[slot] += dkw[part]
                    dv_acc[slot] += dvw[part]
            if hh == 0:
                dq_scr[...] = dq_h
            else:
                dq_ref[_block_rows(i, 1), :] = jnp.where(_head_lanes(0), dq_scr[...], dq_h).astype(BF16)
                if not (isinstance(i, int) and i < KV_BLOCKS - 1):
                    flush(i - (KV_BLOCKS - 1))

        def flush(block):
            dk_ref[_block_rows(block, 1), :] = dk_acc[ring(block)].astype(BF16)
            dv_ref[_block_rows(block, 1), :] = dv_acc[ring(block)].astype(BF16)

        def tile(n):
            return n // 2, n % 2

        def step(n):
            if n + 1 < 2 * nb:
                products(*tile(n + 1))
            grads(*tile(n))
            if n >= 1:
                accumulate(*tile(n - 1))

        products(0, 0)
        for n in range(2 * KV_BLOCKS):
            step(n)

        def two_steps(i, carry):
            products(i, 1)
            grads(i, 0)
            accumulate(i - 1, 1)
            products(i + 1, 0)
            grads(i, 1)
            accumulate(i, 0)
            return carry

        lax.fori_loop(KV_BLOCKS, nb - 1, two_steps, 0)
        step(2 * nb - 2)
        step(2 * nb - 1)
        accumulate(nb - 1, 1)
        flush(nb - 2)
        flush(nb - 1)

        @pl.when(pl.program_id(0) == N_PAIRS - 1)
        def _():
            for cp in exchange:
                cp.wait_recv()
            for cp in exchange:
                cp.wait_send()

    col = lambda c0: pl.BlockSpec((t, PAIR), lambda j: (0, c0 + j))
    once = pl.BlockSpec((t, PAIR), lambda j: (0, j), pipeline_mode=pl.Buffered(1))
    tile_spec = pl.BlockSpec((2, Q_BLOCK, KV_WINDOW), lambda j: (j, 0, 0))
    out = jax.ShapeDtypeStruct((t, ATTN_WIDTH), BF16)
    return pl.pallas_call(
        body,
        name="attn_bwd",
        grid=(N_PAIRS,),
        out_shape=(out, out, out, out,
                   jax.ShapeDtypeStruct((N_HEADS, Q_BLOCK // CHUNK, CHUNK, _BAND_WIDTH), F32),
                   jax.ShapeDtypeStruct((N_HEADS, Q_BLOCK, LANES), F32),
                   jax.ShapeDtypeStruct((N_DEV - 1, OUT_SHARD, D_MODEL), BF16)),
        in_specs=[col(0), col(N_PAIRS), col(2 * N_PAIRS), once, col(0), col(0), tile_spec,
                  pl.BlockSpec(memory_space=pl.ANY)],
        out_specs=(col(0), col(0), col(0), col(0),
                   pl.BlockSpec((2, Q_BLOCK // CHUNK, CHUNK, _BAND_WIDTH), lambda j: (j, 0, 0, 0)),
                   pl.BlockSpec((2, Q_BLOCK, LANES), lambda j: (j, 0, 0)),
                   pl.BlockSpec(memory_space=pl.ANY)),
        scratch_shapes=[
            pltpu.VMEM((t, PAIR), BF16),
            pltpu.VMEM((2, Q_BLOCK, KV_WINDOW), F32),
            pltpu.VMEM((2, Q_BLOCK, KV_WINDOW), F32),
            pltpu.VMEM((2, Q_BLOCK, KV_WINDOW), BF16),
            pltpu.VMEM((2, Q_BLOCK, KV_WINDOW), BF16),
            pltpu.VMEM((Q_BLOCK, PAIR), F32),
            pltpu.VMEM((KV_BLOCKS, Q_BLOCK, PAIR), F32),
            pltpu.VMEM((KV_BLOCKS, Q_BLOCK, PAIR), F32),
            pltpu.SemaphoreType.DMA((N_DEV - 1,)),
            pltpu.SemaphoreType.DMA((N_DEV - 1,)),
        ],
        compiler_params=_params(("arbitrary",), vmem=60 * 1024 * 1024),
    )(qkv, qkv, qkv, ag, o, dy_attn, bias_tile, dwout_g)


def _outproj_loss(x2d, tgt2d, y_pool, y_attn, wout_g, g2):
    t = x2d.shape[0]
    n_tiles = t // TOKEN_TILE

    def body(x_ref, tgt_ref, yp_ref, ya_ref, w_ref, g_ref,
             dx2_ref, dyp_ref, dya_ref, dw_ref, dg_ref, loss_ref, acc_ref):
        i = pl.program_id(0)

        @pl.when(i == 0)
        def _():
            acc_ref[...] = jnp.zeros_like(acc_ref)
            dg_ref[...] = jnp.zeros_like(dg_ref)
            loss_ref[...] = jnp.zeros_like(loss_ref)

        w = w_ref[...].reshape(D_MODEL, D_MODEL)
        y = jnp.concatenate([yp_ref[...], ya_ref[...]], axis=1)
        x2 = x_ref[...] + _nn(y, w)
        r = lax.rsqrt(jnp.mean(x2 * x2, axis=-1, keepdims=True) + EPS)
        xh = x2 * r
        g = g_ref[...]
        diff = xh * g - tgt_ref[...]
        tok = jnp.sum(diff * diff, axis=-1, keepdims=True) * (1.0 / D_MODEL)
        loss_ref[...] += jnp.sum(tok, axis=0, keepdims=True)
        dout = diff * (1.0 / D_MODEL)
        dg_ref[...] += jnp.sum(dout * xh, axis=0, keepdims=True)
        u = dout * g
        dx2 = r * (u - xh * jnp.mean(u * xh, axis=-1, keepdims=True))
        dx2_ref[...] = dx2
        dx2_b = dx2.astype(BF16)
        dy = _nt(dx2_b, w)
        dyp_ref[...] = dy[:, :POOL_WIDTH].astype(BF16)
        dya_ref[...] = dy[:, POOL_WIDTH:].astype(BF16)
        acc_ref[...] += _tn(y, dx2_b)

        @pl.when(i == n_tiles - 1)
        def _():
            dw_ref[...] = acc_ref[...].reshape(N_DEV, OUT_SHARD, D_MODEL).astype(BF16)

    tile = lambda width: pl.BlockSpec((TOKEN_TILE, width), lambda i: (i, 0))
    return pl.pallas_call(
        body,
        name="outproj_loss",
        grid=(n_tiles,),
        out_shape=(
            jax.ShapeDtypeStruct((t, D_MODEL), F32),
            jax.ShapeDtypeStruct((t, POOL_WIDTH), BF16),
            jax.ShapeDtypeStruct((t, ATTN_WIDTH), BF16),
            jax.ShapeDtypeStruct((N_DEV, OUT_SHARD, D_MODEL), BF16),
            jax.ShapeDtypeStruct((1, D_MODEL), F32),
            jax.ShapeDtypeStruct((8, LANES), F32),
        ),
        in_specs=[
            tile(D_MODEL), tile(D_MODEL), tile(POOL_WIDTH), tile(ATTN_WIDTH),
            pl.BlockSpec((N_DEV, OUT_SHARD, D_MODEL), lambda i: (0, 0, 0)),
            pl.BlockSpec((1, D_MODEL), lambda i: (0, 0)),
        ],
        out_specs=(
            tile(D_MODEL), tile(POOL_WIDTH), tile(ATTN_WIDTH),
            pl.BlockSpec((N_DEV, OUT_SHARD, D_MODEL), lambda i: (0, 0, 0)),
            pl.BlockSpec((1, D_MODEL), lambda i: (0, 0)),
            pl.BlockSpec((8, LANES), lambda i: (0, 0)),
        ),
        scratch_shapes=[pltpu.VMEM((D_MODEL, D_MODEL), F32)],
        compiler_params=_params(("arbitrary",)),
    )(x2d, tgt2d, y_pool, y_attn, wout_g, g2)


def _dproj_specs():
    tile = lambda width: pl.BlockSpec((TOKEN_TILE, width), lambda i: (i, 0))
    return [tile(2 * POOL_WIDTH)] + [tile(ATTN_WIDTH)] * 4


def _inproj_bwd_dx(x2d, dx2, dproj, g1, wg):
    t = x2d.shape[0]

    def body(x_ref, dx2_ref, dp_ref, dq_ref, dk_ref, dv_ref, dag_ref, g_ref, wg_hbm, gx_ref, dg_ref, wfull_ref, sem):
        @pl.when(pl.program_id(0) == 0)
        def _():
            _load_w_in(wg_hbm, wfull_ref, sem)
            dg_ref[...] = jnp.zeros_like(dg_ref)

        dproj_t = jnp.concatenate([dp_ref[...], dq_ref[...], dk_ref[...], dv_ref[...], dag_ref[...]], axis=1)
        dh = _nt(dproj_t, wfull_ref[...])
        xf = x_ref[...]
        r = lax.rsqrt(jnp.mean(xf * xf, axis=-1, keepdims=True) + EPS)
        xh = xf * r
        dg_ref[...] += jnp.sum(dh * xh, axis=0, keepdims=True)
        u = dh * g_ref[...]
        gx_ref[...] = dx2_ref[...] + r * (u - xh * jnp.mean(u * xh, axis=-1, keepdims=True))

    tile = pl.BlockSpec((TOKEN_TILE, D_MODEL), lambda i: (i, 0))
    return pl.pallas_call(
        body,
        name="inproj_bwd_dx",
        grid=(t // TOKEN_TILE,),
        out_shape=(jax.ShapeDtypeStruct((t, D_MODEL), F32), jax.ShapeDtypeStruct((1, D_MODEL), F32)),
        in_specs=[tile, tile] + _dproj_specs() + [
            pl.BlockSpec((1, D_MODEL), lambda i: (0, 0)),
            pl.BlockSpec(memory_space=pl.ANY),
        ],
        out_specs=(tile, pl.BlockSpec((1, D_MODEL), lambda i: (0, 0))),
        scratch_shapes=[pltpu.VMEM((D_MODEL, IN_WIDTH), BF16), pltpu.SemaphoreType.DMA((N_DEV,))],
        compiler_params=_params(("arbitrary",)),
    )(x2d, dx2, *dproj, g1, wg)


def _inproj_bwd_dw(x2d, pvg, dy_pool, dattn, g1, pool_w, pool_scale):
    t = x2d.shape[0]
    n_tiles = t // TOKEN_TILE
    halo_per_tile = TOKEN_TILE // HALO
    last_halo = t // HALO - 1

    def body(x_ref, cur_ref, prev_ref, pgn_ref, dy_ref, dyn_ref, dq_ref, dk_ref, dv_ref, dag_ref, g_ref, pw_ref, ps_ref,
             out_ref, dp_ref, dpw_ref, dps_ref, acc_ref):
        i = pl.program_id(0)

        @pl.when(i == 0)
        def _():
            acc_ref[...] = jnp.zeros_like(acc_ref)
            dpw_ref[...] = jnp.zeros_like(dpw_ref)
            dps_ref[...] = jnp.zeros_like(dps_ref)

        xf = x_ref[...]
        r = lax.rsqrt(jnp.mean(xf * xf, axis=-1, keepdims=True) + EPS)
        h = ((xf * r) * g_ref[...]).astype(BF16)
        col = 2 * POOL_WIDTH
        for ref in (dq_ref, dk_ref, dv_ref, dag_ref):
            acc_ref[:, col:col + ATTN_WIDTH] += _tn(h, ref[...])
            col += ATTN_WIDTH
        _pool_bwd_tile(i, n_tiles, cur_ref, prev_ref, pgn_ref, dy_ref, dyn_ref, pw_ref, ps_ref, dp_ref, dpw_ref, dps_ref)
        for c0 in (0, POOL_WIDTH):
            acc_ref[:, c0:c0 + POOL_WIDTH] += _tn(h, dp_ref[:, c0:c0 + POOL_WIDTH])

        @pl.when(i == n_tiles - 1)
        def _():
            for d in range(N_DEV):
                out_ref[d] = acc_ref[:, d * IN_SHARD:(d + 1) * IN_SHARD].astype(BF16)

    tile = lambda width: pl.BlockSpec((TOKEN_TILE, width), lambda i: (i, 0))
    next_halo = lambda col: pl.BlockSpec(
        (HALO, POOL_WIDTH), lambda i: (jnp.minimum((i + 1) * halo_per_tile, last_halo), col))
    return pl.pallas_call(
        body,
        name="inproj_bwd_dw",
        grid=(n_tiles,),
        out_shape=(
            jax.ShapeDtypeStruct((N_DEV, D_MODEL, IN_SHARD), BF16),
            jax.ShapeDtypeStruct((t, 2 * POOL_WIDTH), BF16),
            jax.ShapeDtypeStruct((N_GROUPS, GROUP_DIM, GROUP_DIM), F32),
            jax.ShapeDtypeStruct((1, POOL_WIDTH), F32),
        ),
        in_specs=[
            tile(D_MODEL),
            tile(2 * POOL_WIDTH),
            pl.BlockSpec((HALO, POOL_WIDTH), lambda i: (jnp.maximum(i * halo_per_tile - 1, 0), 0)),
            next_halo(1),
            tile(POOL_WIDTH),
            next_halo(0),
            tile(ATTN_WIDTH), tile(ATTN_WIDTH), tile(ATTN_WIDTH), tile(ATTN_WIDTH),
            pl.BlockSpec((1, D_MODEL), lambda i: (0, 0)),
            pl.BlockSpec((N_GROUPS, GROUP_DIM, GROUP_DIM), lambda i: (0, 0, 0)),
            pl.BlockSpec((1, POOL_WIDTH), lambda i: (0, 0)),
        ],
        out_specs=(
            pl.BlockSpec((N_DEV, D_MODEL, IN_SHARD), lambda i: (0, 0, 0)),
            tile(2 * POOL_WIDTH),
            pl.BlockSpec((N_GROUPS, GROUP_DIM, GROUP_DIM), lambda i: (0, 0, 0)),
            pl.BlockSpec((1, POOL_WIDTH), lambda i: (0, 0)),
        ),
        scratch_shapes=[pltpu.VMEM((D_MODEL, IN_WIDTH), F32)],
        compiler_params=_params(("arbitrary",)),
    )(x2d, pvg, pvg, pvg, dy_pool, dy_pool, *dattn, g1, pool_w, pool_scale)


_HBM = pl.BlockSpec(memory_space=pltpu.HBM)
_SEM = pl.BlockSpec(memory_space=pltpu.SEMAPHORE)
_DATAFLOW = pltpu.SideEffectType.DATAFLOW_SIDE_EFFECTING


def _exchange_start(blocks):
    land_shape = (N_DEV - 1,) + blocks.shape[1:]

    def body(src_hbm, land_hbm, send_sems, recv_sems, src_thru, land_thru, token):
        for cp in _exchange_copies(src_hbm, land_hbm, send_sems, recv_sems):
            cp.start()
        token[...] = jnp.zeros_like(token)

    sems = pltpu.SemaphoreType.DMA((N_DEV - 1,))
    return pl.pallas_call(
        body,
        name="exchange_start",
        out_shape=(sems, sems, pltpu.HBM(blocks.shape, blocks.dtype), pltpu.HBM(land_shape, blocks.dtype),
                   jax.ShapeDtypeStruct((8, LANES), F32)),
        in_specs=(_HBM, _HBM),
        out_specs=(_SEM, _SEM, _HBM, _HBM, pl.BlockSpec(memory_space=pltpu.VMEM)),
        input_output_aliases={0: 2, 1: 3},
        compiler_params=pltpu.CompilerParams(has_side_effects=_DATAFLOW),
    )(pltpu.with_memory_space_constraint(blocks, pltpu.HBM),
      pltpu.with_memory_space_constraint(lax.empty(land_shape, blocks.dtype), pltpu.HBM))


def _exchange_wait(send_sems, recv_sems, blocks, land, after):
    def body(src_hbm, land_hbm, send_sems, recv_sems, after_ref, src_out, land_out):
        for cp in _exchange_copies(src_hbm, land_hbm, send_sems, recv_sems):
            cp.wait_send()
            cp.wait_recv()

    return pl.pallas_call(
        body,
        name="exchange_wait",
        out_shape=(pltpu.HBM(blocks.shape, blocks.dtype), pltpu.HBM(land.shape, land.dtype)),
        in_specs=(_HBM, _HBM, _SEM, _SEM, pl.BlockSpec(memory_space=pl.ANY)),
        out_specs=(_HBM, _HBM),
        input_output_aliases={0: 0, 1: 1},
        compiler_params=pltpu.CompilerParams(has_side_effects=_DATAFLOW),
    )(blocks, land, send_sems, recv_sems, after)


def _adamw(w, g, m, v):
    m = ADAM_B1 * m + (1.0 - ADAM_B1) * g
    v = ADAM_B2 * v + (1.0 - ADAM_B2) * (g * g)
    m_hat = m / (1.0 - ADAM_B1 ** ADAM_STEP)
    v_hat = v / (1.0 - ADAM_B2 ** ADAM_STEP)
    delta = -ADAM_LR * (m_hat / (jnp.sqrt(v_hat) + ADAM_EPS) + ADAM_WD * w)
    return delta, m, v


_ROW_G1, _ROW_G2, _ROW_PS, _ROW_LOSS, _ROW_RB = 0, 1, 2, 3, 8
_VEC_ROWS = 16


def _reduce_adamw(dwin_g, land_in, dwout_g, land_out, grads, weights, big):
    small_shapes = [(1, D_MODEL), (1, D_MODEL), (1, POOL_WIDTH), (N_HEADS, 2 * LANES), (N_GROUPS, GROUP_DIM, GROUP_DIM)]

    def body(*refs):
        refs = list(refs)
        take = lambda n: [refs.pop(0) for _ in range(n)]
        dwin_hbm, lin_ref, dwout_hbm, lout_ref = take(4)
        g1_ref, g2_ref, ps_ref, rb_ref, pw_ref, loss_ref = take(6)
        small_wmv = [take(3) for _ in range(5)]
        big_wmv = [take(3) for _ in range(2)]
        big_out = [take(4) for _ in range(2)]
        small_out = [take(4) for _ in range(5)]
        (loss_out,) = take(1)
        own_in, own_out, vec_ref, rvec_ref, rpw_ref, send_sems, recv_sems, local_sems = refs

        me = _dev_index(_mesh_pos())
        vec_ref[...] = jnp.zeros_like(vec_ref)
        vec_ref[_ROW_G1:_ROW_G1 + 1, :] = g1_ref[...]
        vec_ref[_ROW_G2:_ROW_G2 + 1, :] = g2_ref[...]
        vec_ref[_ROW_PS:_ROW_PS + 1, :POOL_WIDTH] = ps_ref[...]
        vec_ref[_ROW_LOSS:_ROW_LOSS + 1, :LANES] = loss_ref[0:1, :]
        vec_ref[_ROW_RB:_ROW_RB + N_HEADS, :2 * LANES] = rb_ref[...]

        def send(r, src, land, k):
            return pltpu.make_async_remote_copy(
                src_ref=src, dst_ref=land.at[r], send_sem=send_sems.at[2 * (r - 1) + k],
                recv_sem=recv_sems.at[2 * (r - 1) + k], device_id=_peer(r), device_id_type=MESH_ID)

        started = [cp for r in range(1, N_DEV) for cp in (send(r, vec_ref, rvec_ref, 0), send(r, pw_ref, rpw_ref, 1))]
        for cp in started:
            cp.start()
        mine = [pltpu.make_async_copy(dwin_hbm.at[me], own_in, local_sems.at[0]),
                pltpu.make_async_copy(dwout_hbm.at[me], own_out, local_sems.at[1])]
        for cp in mine:
            cp.start()
        rvec_ref[0] = vec_ref[...]
        rpw_ref[0] = pw_ref[...]
        for cp in mine:
            cp.wait()

        def update(g, wmv, outs):
            delta, m_new, v_new = _adamw(wmv[0][...], g, wmv[1][...], wmv[2][...])
            for ref, val in zip(outs, (g, delta, m_new, v_new)):
                ref[...] = val

        g_in = own_in[...].astype(F32)
        g_out = own_out[...].astype(F32)
        for r in range(N_DEV - 1):
            g_in = g_in + lin_ref[r].astype(F32)
            g_out = g_out + lout_ref[r].astype(F32)
        update(g_in, big_wmv[0], big_out[0])
        update(g_out, big_wmv[1], big_out[1])

        for cp in started:
            cp.wait_recv()
        for cp in started:
            cp.wait_send()
        vec = rvec_ref[me]
        pw = rpw_ref[me]
        for s in range(1, N_DEV):
            vec = vec + rvec_ref[me ^ s]
            pw = pw + rpw_ref[me ^ s]
        vec_ref[...] = vec
        update(vec_ref[_ROW_G1:_ROW_G1 + 1, :], small_wmv[0], small_out[0])
        update(vec_ref[_ROW_G2:_ROW_G2 + 1, :], small_wmv[1], small_out[1])
        update(vec_ref[_ROW_PS:_ROW_PS + 1, :POOL_WIDTH], small_wmv[2], small_out[2])
        update(vec_ref[_ROW_RB:_ROW_RB + N_HEADS, :2 * LANES], small_wmv[3], small_out[3])
        update(pw, small_wmv[4], small_out[4])
        loss_out[...] = jnp.broadcast_to(vec_ref[_ROW_LOSS:_ROW_LOSS + 1, :LANES], loss_out.shape)

    vm = pl.BlockSpec(memory_space=pltpu.VMEM)
    hbm = pl.BlockSpec(memory_space=pl.ANY)
    f32 = lambda shape: jax.ShapeDtypeStruct(shape, F32)
    out_shapes = [f32((D_MODEL, IN_SHARD))] * 4 + [f32((OUT_SHARD, D_MODEL))] * 4
    for shape in small_shapes:
        out_shapes += [f32(shape)] * 4
    out_shapes.append(f32((8, LANES)))
    args = [dwin_g, land_in, dwout_g, land_out, *grads]
    for wmv in weights:
        args += list(wmv)
    for wmv in big:
        args += list(wmv)
    return pl.pallas_call(
        body,
        name="reduce_adamw",
        out_shape=tuple(out_shapes),
        in_specs=[hbm, vm, hbm, vm] + [vm] * (len(args) - 4),
        out_specs=tuple([vm] * len(out_shapes)),
        scratch_shapes=[
            pltpu.VMEM((D_MODEL, IN_SHARD), BF16),
            pltpu.VMEM((OUT_SHARD, D_MODEL), BF16),
            pltpu.VMEM((_VEC_ROWS, D_MODEL), F32),
            pltpu.VMEM((N_DEV, _VEC_ROWS, D_MODEL), F32),
            pltpu.VMEM((N_DEV, N_GROUPS, GROUP_DIM, GROUP_DIM), F32),
            pltpu.SemaphoreType.DMA((2 * (N_DEV - 1),)),
            pltpu.SemaphoreType.DMA((2 * (N_DEV - 1),)),
            pltpu.SemaphoreType.DMA((2,)),
        ],
        compiler_params=_params(),
    )(*args)


def kernel(x, norm_gain, w_in, pool_w, pool_scale, rel_bias, w_out, final_norm_gain, loss_target, m_norm_gain, m_w_in, m_pool_w, m_pool_scale, m_rel_bias, m_w_out, m_final_norm_gain, v_norm_gain, v_w_in, v_pool_w, v_pool_scale, v_rel_bias, v_w_out, v_final_norm_gain):
    t = x.shape[1]
    assert x.shape[0] == 1 and t % TOKEN_TILE == 0 and t // Q_BLOCK >= 4
    x2d = x[0]
    tgt2d = loss_target[0]
    g2 = final_norm_gain.reshape(1, D_MODEL)

    rb = rel_bias[0]
    rel_line = jnp.concatenate([
        jnp.broadcast_to(rb[:, :1], (N_HEADS, _REL_FIRST)), rb,
        jnp.broadcast_to(rb[:, N_REL - 1:], (N_HEADS, TOEPLITZ - _REL_FIRST - N_REL)),
    ], axis=1).reshape(N_HEADS, 1, TOEPLITZ)
    wg_in, bias_tile = _gather_weights(w_in[0], rel_line)

    pvg, qkv, ag, y_pool = _norm_inproj(x2d, norm_gain, wg_in, pool_w[0], pool_scale)
    o, y_attn, wg_out = _attn_fwd(qkv, ag, bias_tile, w_out[0])
    dx2, dy_pool, dy_attn, dwout_g, d_g2, loss_sum = _outproj_loss(x2d, tgt2d, y_pool, y_attn, wg_out, g2)
    dq, dk, dv, dag, ds_band, ds_total, land_out = _attn_bwd(qkv, ag, o, dy_attn, bias_tile, dwout_g)
    d_rb = _bias_grad(ds_band, ds_total)
    dwin_g, d_pool, d_pw, d_ps = _inproj_bwd_dw(x2d, pvg, dy_pool, (dq, dk, dv, dag), norm_gain, pool_w[0], pool_scale)
    dproj = (d_pool, dq, dk, dv, dag)
    send_sems, recv_sems, dwin_g, land_in, token = _exchange_start(dwin_g)
    grad_x, d_g1 = _inproj_bwd_dx(x2d, dx2, dproj, norm_gain + token[:1, :1], wg_in)
    dwin_g, land_in = _exchange_wait(send_sems, recv_sems, dwin_g, land_in, d_g1)

    pad_rb = lambda a: jnp.pad(a[0], ((0, 0), (0, 2 * LANES - N_REL)))
    row = lambda a: a.reshape(1, D_MODEL)
    weights = [
        (norm_gain, m_norm_gain, v_norm_gain),
        (row(final_norm_gain), row(m_final_norm_gain), row(v_final_norm_gain)),
        (pool_scale, m_pool_scale, v_pool_scale),
        (pad_rb(rel_bias), pad_rb(m_rel_bias), pad_rb(v_rel_bias)),
        (pool_w[0], m_pool_w[0], v_pool_w[0]),
    ]
    big = [(w_in[0], m_w_in[0], v_w_in[0]), (w_out[0], m_w_out[0], v_w_out[0])]
    res = _reduce_adamw(dwin_g, land_in, dwout_g, land_out, (d_g1, d_g2, d_ps, d_rb, d_pw, loss_sum), weights, big)
    loss = 0.5 * res[28][0, 0]

    def leaves(k):
        g1_, g2_, ps_, rb_, pw_ = (res[8 + 4 * leaf + k] for leaf in range(5))
        return [g1_, res[k][None], pw_[None], ps_, rb_[None, :, :N_REL], res[4 + k][None], g2_.reshape(D_MODEL)]

    return (loss, grad_x[None], *leaves(0), *leaves(1), *leaves(2), *leaves(3))
```

```python
import math

import jax
import jax.numpy as jnp
from jax import lax
from jax.experimental import pallas as pl
from jax.experimental.pallas import tpu as pltpu

F32 = jnp.float32
BF16 = jnp.bfloat16
MESH_ID = pl.DeviceIdType.MESH

D_MODEL = 1024
POOL_WIDTH = 512
ATTN_WIDTH = 512
POOL_WINDOWS = (2, 4, 8, 16)
N_GROUPS = 4
GROUP_DIM = 128
HEAD_DIM = 64
N_HEADS = 8
CHUNK = 64
LEFT_CHUNKS = 8
MAX_REL = 64
N_REL = 2 * MAX_REL + 1
IN_WIDTH = 2 * POOL_WIDTH + 4 * ATTN_WIDTH
EPS = 1e-6
MASK_VALUE = -1e30
ATTN_SCALE = 1.0 / math.sqrt(HEAD_DIM)
ADAM_LR = 0.001
ADAM_B1 = 0.9
ADAM_B2 = 0.999
ADAM_EPS = 1e-08
ADAM_WD = 0.01
ADAM_STEP = 10

N_DEV = 8
IN_SHARD = IN_WIDTH // N_DEV
OUT_SHARD = D_MODEL // N_DEV

LANES = 128
TOKEN_TILE = 512
HALO = 16
Q_BLOCK = 256
KV_BLOCKS = 3
KV_WINDOW = KV_BLOCKS * Q_BLOCK
PAIR = 2 * HEAD_DIM
N_PAIRS = N_HEADS // 2
TOEPLITZ = 1024
VMEM_LIMIT = 56 * 1024 * 1024


def _params(sem=None, vmem=VMEM_LIMIT):
    return pltpu.CompilerParams(dimension_semantics=sem, vmem_limit_bytes=vmem)


def _sigmoid(x):
    return 1.0 / (1.0 + jnp.exp(-x))


def _nt(a, b):
    return lax.dot_general(a, b, (((1,), (1,)), ((), ())), preferred_element_type=F32)


def _tn(a, b):
    return lax.dot_general(a, b, (((0,), (0,)), ((), ())), preferred_element_type=F32)


def _nn(a, b):
    return jnp.dot(a, b, preferred_element_type=F32)


def _mesh_pos():
    return lax.axis_index("x"), lax.axis_index("y"), lax.axis_index("c")


def _dev_index(p):
    return 4 * p[0] + 2 * p[1] + p[2]


def _peer(r):
    x, y, c = _mesh_pos()
    return (x ^ ((r >> 2) & 1), y ^ ((r >> 1) & 1), c ^ (r & 1))


def _exchange_copies(src_hbm, land_hbm, send_sems, recv_sems):
    return [
        pltpu.make_async_remote_copy(
            src_ref=src_hbm.at[_dev_index(_peer(r))], dst_ref=land_hbm.at[r - 1],
            send_sem=send_sems.at[r - 1], recv_sem=recv_sems.at[r - 1],
            device_id=_peer(r), device_id_type=MESH_ID)
        for r in range(1, N_DEV)
    ]


def _gather_weights(w_in_shard, rel_line):
    def body(win_ref, line_ref, gin_ref, bias_ref, sin_ref, send_sems, recv_sems):
        x, y, c = _mesh_pos()
        me, sibling = (x, y, c), (x, y, 1 - c)
        chips = [(1 - x, y), (x, 1 - y), (1 - x, 1 - y)]

        sin_ref[...] = win_ref[...].astype(BF16)
        gin_ref[_dev_index(me)] = sin_ref[...]

        def copy(k, block, to, from_shard=False):
            return pltpu.make_async_remote_copy(
                src_ref=sin_ref if from_shard else gin_ref.at[_dev_index(block)],
                dst_ref=gin_ref.at[_dev_index(block)],
                send_sem=send_sems.at[k],
                recv_sem=recv_sems.at[k],
                device_id=to,
                device_id_type=MESH_ID,
            )

        first = [copy(0, me, sibling, True)]
        first += [copy(1 + j, me, (*chip, c), True) for j, chip in enumerate(chips)]
        for cp in first:
            cp.start()
        passed = [copy(4 + j, (*chip, c), sibling) for j, chip in enumerate(chips)]

        def bias_heads(lo, hi):
            for h in range(lo, hi):
                bias_ref[h] = _toeplitz_bias(line_ref[h])

        bias_heads(0, N_HEADS - 3)
        for j, chip in enumerate(chips):
            copy(1 + j, (*chip, c), me).wait_recv()
            passed[j].start()
            bias_heads(N_HEADS - 3 + j, N_HEADS - 2 + j)
        copy(0, sibling, me).wait_recv()
        for j, chip in enumerate(chips):
            copy(4 + j, (*chip, 1 - c), me).wait_recv()
        for cp in first + passed:
            cp.wait_send()

    vm = pl.BlockSpec(memory_space=pltpu.VMEM)
    return pl.pallas_call(
        body,
        name="gather_weights",
        out_shape=(
            jax.ShapeDtypeStruct((N_DEV, D_MODEL, IN_SHARD), BF16),
            jax.ShapeDtypeStruct((N_HEADS, Q_BLOCK, KV_WINDOW), F32),
        ),
        in_specs=[vm, vm],
        out_specs=(vm, vm),
        scratch_shapes=[
            pltpu.VMEM((D_MODEL, IN_SHARD), BF16),
            pltpu.SemaphoreType.DMA((7,)),
            pltpu.SemaphoreType.DMA((7,)),
        ],
        compiler_params=_params(),
    )(w_in_shard, rel_line)


def _load_w_in(wg_hbm, wfull_ref, sem):
    copies = [
        pltpu.make_async_copy(wg_hbm.at[d], wfull_ref.at[:, d * IN_SHARD:(d + 1) * IN_SHARD], sem.at[d])
        for d in range(N_DEV)
    ]
    for cp in copies:
        cp.start()
    for cp in copies:
        cp.wait()


def _norm_inproj(x2d, g1, wg, pool_w, pool_scale):
    t = x2d.shape[0]

    def body(x_ref, g_ref, wg_hbm, pw_ref, ps_ref, pvg_ref, qkv_ref, ag_ref, yp_ref, wfull_ref, halo_ref, sem):
        i = pl.program_id(0)

        @pl.when(i == 0)
        def _():
            _load_w_in(wg_hbm, wfull_ref, sem)
            halo_ref[...] = jnp.zeros_like(halo_ref)

        xf = x_ref[...]
        r = lax.rsqrt(jnp.mean(xf * xf, axis=-1, keepdims=True) + EPS)
        h = ((xf * r) * g_ref[...]).astype(BF16)
        chunk = lambda ci: _nn(h, wfull_ref[:, ci * POOL_WIDTH:(ci + 1) * POOL_WIDTH])
        pv, pg = chunk(0), chunk(1)
        pvg_ref[:, :POOL_WIDTH] = pv
        pvg_ref[:, POOL_WIDTH:] = pg
        halo = halo_ref[...]
        halo_ref[...] = pv[TOKEN_TILE - HALO:]
        for gi in range(N_GROUPS):
            sl = slice(gi * GROUP_DIM, (gi + 1) * GROUP_DIM)
            d = _pool_diffs(pv, halo, i * TOKEN_TILE, gi)
            z = _nn(d.astype(BF16), pw_ref[gi].astype(BF16))
            g = pg[:, sl]
            yp_ref[:, sl] = ((z * ps_ref[:, sl]) * (g * _sigmoid(g))).astype(BF16)
        for ci in range(2, 5):
            qkv_ref[:, (ci - 2) * POOL_WIDTH:(ci - 1) * POOL_WIDTH] = chunk(ci).astype(BF16)
        ag_ref[...] = chunk(5)

    tile = lambda width: pl.BlockSpec((TOKEN_TILE, width), lambda i: (i, 0))
    return pl.pallas_call(
        body,
        name="norm_inproj",
        grid=(t // TOKEN_TILE,),
        out_shape=(
            jax.ShapeDtypeStruct((t, 2 * POOL_WIDTH), F32),
            jax.ShapeDtypeStruct((t, 3 * ATTN_WIDTH), BF16),
            jax.ShapeDtypeStruct((t, ATTN_WIDTH), F32),
            jax.ShapeDtypeStruct((t, POOL_WIDTH), BF16),
        ),
        in_specs=[
            tile(D_MODEL),
            pl.BlockSpec((1, D_MODEL), lambda i: (0, 0)),
            pl.BlockSpec(memory_space=pl.ANY),
            pl.BlockSpec((N_GROUPS, GROUP_DIM, GROUP_DIM), lambda i: (0, 0, 0)),
            pl.BlockSpec((1, POOL_WIDTH), lambda i: (0, 0)),
        ],
        out_specs=(tile(2 * POOL_WIDTH), tile(3 * ATTN_WIDTH), tile(ATTN_WIDTH), tile(POOL_WIDTH)),
        scratch_shapes=[
            pltpu.VMEM((D_MODEL, IN_WIDTH), BF16),
            pltpu.VMEM((HALO, POOL_WIDTH), F32),
            pltpu.SemaphoreType.DMA((N_DEV,)),
        ],
        compiler_params=_params(("arbitrary",)),
    )(x2d, g1, wg, pool_w, pool_scale)


def _inv_count(first_row, rows, window):
    tpos = first_row + lax.broadcasted_iota(jnp.int32, (rows, 1), 0)
    return 1.0 / jnp.minimum(tpos + 1, window).astype(F32)


def _causal_window_sum(ext, window):
    s, k = ext, 1
    while k < window:
        s = s + pltpu.roll(s, k, 0)
        k *= 2
    return s


def _pool_diffs(pv, halo, first_row, gi):
    w = POOL_WINDOWS[gi]
    sl = slice(gi * GROUP_DIM, (gi + 1) * GROUP_DIM)
    ext = jnp.concatenate([halo[:, sl], pv[:, sl]], axis=0)
    s = _causal_window_sum(ext, w)[HALO:]
    return s * _inv_count(first_row, pv.shape[0], w) - pv[:, sl]


def _pool_bwd_tile(i, n_tiles, cur_ref, prev_ref, pgn_ref, dy_ref, dyn_ref, pw_ref, ps_ref, dp_ref, dpw_ref, dps_ref):
    pv = cur_ref[:, :POOL_WIDTH]
    pg = cur_ref[:, POOL_WIDTH:]
    prev = jnp.where(i > 0, prev_ref[...], 0.0)
    has_next = i < n_tiles - 1
    rows = TOKEN_TILE + HALO
    for gi in range(N_GROUPS):
        w = POOL_WINDOWS[gi]
        sl = slice(gi * GROUP_DIM, (gi + 1) * GROUP_DIM)
        pw = pw_ref[gi].astype(BF16)
        ps = ps_ref[:, sl]
        d = _pool_diffs(pv, prev, i * TOKEN_TILE, gi).astype(BF16)
        z = _nn(d, pw)
        g_ext = jnp.concatenate([pg[:, sl], pgn_ref[:, sl]], axis=0)
        dy_ext = jnp.concatenate([dy_ref[:, sl], dyn_ref[:, sl]], axis=0).astype(F32)
        sig = _sigmoid(g_ext)
        gate = g_ext * sig
        dz_ext = ((dy_ext * gate) * ps).astype(BF16)
        dd_ext = _nt(dz_ext, pw)
        e = dd_ext * _inv_count(i * TOKEN_TILE, rows, w)
        row = lax.broadcasted_iota(jnp.int32, (rows, 1), 0)
        e = jnp.where(jnp.logical_or(row < TOKEN_TILE, has_next), e, 0.0)
        s, k = e, 1
        while k < w:
            s = s + pltpu.roll(s, rows - k, 0)
            k *= 2
        dp_ref[:, sl] = (s[:TOKEN_TILE] - dd_ext[:TOKEN_TILE]).astype(BF16)
        dy = dy_ext[:TOKEN_TILE]
        g = g_ext[:TOKEN_TILE]
        sg = sig[:TOKEN_TILE]
        dgate = sg * (1.0 + g * (1.0 - sg))
        dp_ref[:, POOL_WIDTH + gi * GROUP_DIM:POOL_WIDTH + (gi + 1) * GROUP_DIM] = (
            (dy * (z * ps)) * dgate).astype(BF16)
        dps_ref[:, sl] += jnp.sum((dy * gate[:TOKEN_TILE]) * z, axis=0, keepdims=True)
        dpw_ref[gi] += _tn(d, dz_ext[:TOKEN_TILE])


_REL_FIRST = KV_WINDOW - 1 - MAX_REL


def _skew_rows(a, right):
    rows, lanes = a.shape
    row = lax.broadcasted_iota(jnp.int32, a.shape, 0)
    for b in range(rows.bit_length() - 1):
        shift = (1 << b) if right else lanes - (1 << b)
        a = jnp.where((row >> b) & 1 == 1, pltpu.roll(a, shift, 1), a)
    return a


def _toeplitz_bias(line):
    a = jnp.broadcast_to(line, (Q_BLOCK, TOEPLITZ))
    a = _skew_rows(a, True)
    a = pltpu.roll(a, TOEPLITZ - (Q_BLOCK - 1), 1)
    a = a[:, :KV_WINDOW]
    qc = lax.broadcasted_iota(jnp.int32, a.shape, 0) // CHUNK
    kc = lax.broadcasted_iota(jnp.int32, a.shape, 1) // CHUNK
    visible = jnp.logical_and(kc >= qc, kc <= qc + LEFT_CHUNKS)
    return jnp.where(visible, a, MASK_VALUE)


_BAND_WIDTH = 2 * LANES
_BAND_START = (384, 512, 512, 512)
_BAND_REL = tuple(a - (KV_BLOCKS - 1) * Q_BLOCK - qc * CHUNK for qc, a in enumerate(_BAND_START))


def _bias_grad(band, total):
    def body(band_ref, total_ref, out_ref):
        for h in range(N_HEADS):
            a = jnp.zeros((CHUNK, 2 * _BAND_WIDTH), F32)
            for qc in range(Q_BLOCK // CHUNK):
                z = jnp.concatenate([band_ref[h, qc], jnp.zeros((CHUNK, _BAND_WIDTH), F32)], axis=1)
                left = -MAX_REL - _BAND_REL[qc]
                a = a + (pltpu.roll(z, 2 * _BAND_WIDTH - left, 1) if left else z)
            a = _skew_rows(a, False)
            near = jnp.sum(a, axis=0, keepdims=True)[:, :2 * LANES]
            r = lax.broadcasted_iota(jnp.int32, near.shape, 1)
            near = jnp.where(jnp.logical_and(r >= 1, r < 2 * MAX_REL), near, 0.0)
            everything = jnp.sum(jnp.sum(total_ref[h], axis=0, keepdims=True), axis=1, keepdims=True)
            far = everything - jnp.sum(near, axis=1, keepdims=True)
            out_ref[h:h + 1, :] = jnp.where(r == 0, far, near)

    vm = pl.BlockSpec(memory_space=pltpu.VMEM)
    return pl.pallas_call(
        body,
        name="bias_grad",
        out_shape=jax.ShapeDtypeStruct((N_HEADS, 2 * LANES), F32),
        in_specs=[vm, vm],
        out_specs=vm,
        compiler_params=_params(),
    )(band, total)


def _head_lanes(hh):
    lane = lax.broadcasted_iota(jnp.int32, (1, PAIR), 1)
    return (lane < HEAD_DIM) if hh == 0 else (lane >= HEAD_DIM)


def _score_windows(nwin):
    if nwin < KV_BLOCKS:
        return [(slice(0, Q_BLOCK), slice(0, nwin * Q_BLOCK), None)]
    pieces = []
    for qc in range(Q_BLOCK // CHUNK):
        rows = slice(qc * CHUNK, (qc + 1) * CHUNK)
        if qc < 2:
            pieces.append((rows, slice(0, KV_WINDOW - LANES), slice(KV_WINDOW - LANES, KV_WINDOW)))
        else:
            pieces.append((rows, slice(LANES, KV_WINDOW), slice(0, LANES)))
    return pieces


def _block_rows(first_block, n_blocks):
    if isinstance(first_block, int):
        return pl.ds(first_block * Q_BLOCK, n_blocks * Q_BLOCK)
    return pl.ds(pl.multiple_of(first_block * Q_BLOCK, Q_BLOCK), n_blocks * Q_BLOCK)


def _attn_fwd(qkv, ag, bias_tile, w_out_shard):
    t = qkv.shape[0]
    nb = t // Q_BLOCK

    def body(q_ref, k_ref, v_ref, ag_ref, bias_ref, wout_ref, o_ref, y_ref, gout_hbm,
             s_scr, p_scr, linv_scr, sout_ref, send_sems, recv_sems, local_sem):
        pair = pl.program_id(0)
        me = _dev_index(_mesh_pos())
        mine = pltpu.make_async_copy(sout_ref, gout_hbm.at[me], local_sem)

        def shard_copy(r, block):
            return pltpu.make_async_remote_copy(
                src_ref=sout_ref, dst_ref=gout_hbm.at[block], send_sem=send_sems.at[r - 1],
                recv_sem=recv_sems.at[r - 1], device_id=_peer(r), device_id_type=MESH_ID)

        @pl.when(pair == 0)
        def _():
            sout_ref[...] = wout_ref[...].astype(BF16)
            mine.start()
            for r in range(1, N_DEV):
                shard_copy(r, me).start()

        def scores(i, slot, nwin):
            q = q_ref[_block_rows(i, 1), :]
            kw = k_ref[_block_rows(i + 1 - nwin, nwin), :]
            for hh in range(2):
                q_h = jnp.where(_head_lanes(hh), q, jnp.zeros_like(q)) * ATTN_SCALE
                s_scr[slot, hh, :, :nwin * Q_BLOCK] = _nt(q_h, kw)

        def softmax(slot, nwin):
            off = (KV_BLOCKS - nwin) * Q_BLOCK
            for hh in range(2):
                for rows, cols, rest in _score_windows(nwin):
                    s = s_scr[slot, hh, rows, cols] + bias_ref[hh, rows, off + cols.start:off + cols.stop]
                    e = jnp.exp(s - jnp.max(s, axis=-1, keepdims=True))
                    linv_scr[slot, hh, rows, :] = 1.0 / jnp.sum(e, axis=-1, keepdims=True)
                    p_scr[slot, hh, rows, cols] = e.astype(BF16)
                    if rest is not None:
                        p_scr[slot, hh, rows, rest] = jnp.zeros((CHUNK, LANES), BF16)

        def output(i, slot, nwin):
            vw = v_ref[_block_rows(i + 1 - nwin, nwin), :]
            outs = [_nn(p_scr[slot, hh, :, :nwin * Q_BLOCK], vw) * linv_scr[slot, hh] for hh in range(2)]
            o = jnp.where(_head_lanes(0), outs[0], outs[1])
            g = ag_ref[_block_rows(i, 1), :]
            o_ref[_block_rows(i, 1), :] = o.astype(BF16)
            y_ref[_block_rows(i, 1), :] = (o * (g * _sigmoid(g))).astype(BF16)

        scores(0, 0, 1)
        scores(1, 1, 2)
        softmax(0, 1)
        scores(2, 0, 3)
        softmax(1, 2)
        output(0, 0, 1)
        scores(3, 1, 3)
        softmax(0, 3)
        output(1, 1, 2)

        def two_steps(k, carry):
            i = 3 + 2 * k
            scores(i + 1, 0, KV_BLOCKS)
            softmax(1, KV_BLOCKS)
            output(i - 1, 0, KV_BLOCKS)
            scores(i + 2, 1, KV_BLOCKS)
            softmax(0, KV_BLOCKS)
            output(i, 1, KV_BLOCKS)
            return carry

        lax.fori_loop(0, (nb - 4) // 2, two_steps, 0)
        last = (nb - 1) % 2
        softmax(last, KV_BLOCKS)
        output(nb - 2, 1 - last, KV_BLOCKS)
        output(nb - 1, last, KV_BLOCKS)

        @pl.when(pair == N_PAIRS - 1)
        def _():
            for r in range(1, N_DEV):
                shard_copy(r, _dev_index(_peer(r))).wait_recv()
            for r in range(1, N_DEV):
                shard_copy(r, me).wait_send()
            mine.wait()

    col = lambda c0: pl.BlockSpec((t, PAIR), lambda j: (0, c0 + j))
    return pl.pallas_call(
        body,
        name="attn_fwd",
        grid=(N_PAIRS,),
        out_shape=(
            jax.ShapeDtypeStruct((t, ATTN_WIDTH), BF16),
            jax.ShapeDtypeStruct((t, ATTN_WIDTH), BF16),
            jax.ShapeDtypeStruct((N_DEV, OUT_SHARD, D_MODEL), BF16),
        ),
        in_specs=[col(0), col(N_PAIRS), col(2 * N_PAIRS), col(0),
                  pl.BlockSpec((2, Q_BLOCK, KV_WINDOW), lambda j: (j, 0, 0)),
                  pl.BlockSpec((OUT_SHARD, D_MODEL), lambda j: (0, 0))],
        out_specs=(col(0), col(0), pl.BlockSpec(memory_space=pl.ANY)),
        scratch_shapes=[
            pltpu.VMEM((2, 2, Q_BLOCK, KV_WINDOW), F32),
            pltpu.VMEM((2, 2, Q_BLOCK, KV_WINDOW), BF16),
            pltpu.VMEM((2, 2, Q_BLOCK, 1), F32),
            pltpu.VMEM((OUT_SHARD, D_MODEL), BF16),
            pltpu.SemaphoreType.DMA((N_DEV - 1,)),
            pltpu.SemaphoreType.DMA((N_DEV - 1,)),
            pltpu.SemaphoreType.DMA,
        ],
        compiler_params=_params(("arbitrary",)),
    )(qkv, qkv, qkv, ag, bias_tile, w_out_shard)


def _attn_bwd(qkv, ag, o, dy_attn, bias_tile, dwout_g):
    t = qkv.shape[0]
    nb = t // Q_BLOCK

    def body(q_ref, k_ref, v_ref, ag_ref, o_ref, dy_ref, bias_ref, dwout_hbm,
             dq_ref, dk_ref, dv_ref, dag_ref, band_ref, total_ref, land_hbm,
             do_scr, s_scr, dp_scr, p_scr, dsb_scr, dq_scr, dk_acc, dv_acc, send_sems, recv_sems):
        exchange = _exchange_copies(dwout_hbm, land_hbm, send_sems, recv_sems)

        @pl.when(pl.program_id(0) == 0)
        def _():
            for cp in exchange:
                cp.start()

        def gates(i, carry):
            rows = _block_rows(i, 1)
            g = ag_ref[rows, :]
            sig = _sigmoid(g)
            dy = dy_ref[rows, :].astype(F32)
            do_scr[rows, :] = (dy * (g * sig)).astype(BF16)
            dag_ref[rows, :] = ((dy * o_ref[rows, :].astype(F32)) * (sig * (1.0 + g * (1.0 - sig)))).astype(BF16)
            return carry

        lax.fori_loop(0, nb, gates, 0)
        band_ref[...] = jnp.zeros_like(band_ref)
        total_ref[...] = jnp.zeros_like(total_ref)

        def nwin_of(i):
            return min(i + 1, KV_BLOCKS) if isinstance(i, int) else KV_BLOCKS

        def operands(i, hh):
            lanes = _head_lanes(hh)
            q = q_ref[_block_rows(i, 1), :]
            do = do_scr[_block_rows(i, 1), :]
            return (jnp.where(lanes, q, jnp.zeros_like(q)) * ATTN_SCALE, jnp.where(lanes, do, jnp.zeros_like(do)))

        def products(i, hh):
            nwin = nwin_of(i)
            win = _block_rows(i + 1 - nwin, nwin)
            q_h, do_h = operands(i, hh)
            s_scr[hh, :, :nwin * Q_BLOCK] = _nt(q_h, k_ref[win, :])
            dp_scr[hh, :, :nwin * Q_BLOCK] = _nt(do_h, v_ref[win, :])

        def grads(i, hh):
            nwin = nwin_of(i)
            off = (KV_BLOCKS - nwin) * Q_BLOCK
            for rows, cols, rest in _score_windows(nwin):
                bias_cols = slice(off + cols.start, off + cols.stop)
                s = s_scr[hh, rows, cols] + bias_ref[hh, rows, bias_cols]
                e = jnp.exp(s - jnp.max(s, axis=-1, keepdims=True))
                p = e * (1.0 / jnp.sum(e, axis=-1, keepdims=True))
                dp = dp_scr[hh, rows, cols]
                ds = p * (dp - jnp.sum(p * dp, axis=-1, keepdims=True))
                total_ref[hh, rows, :] += sum(ds[:, c0:c0 + LANES] for c0 in range(0, ds.shape[1], LANES))
                for qc in range(rows.start // CHUNK, rows.stop // CHUNK):
                    lo = max(_BAND_START[qc], bias_cols.start)
                    hi = min(_BAND_START[qc] + _BAND_WIDTH, bias_cols.stop)
                    if lo < hi:
                        band_ref[hh, qc, :, lo - _BAND_START[qc]:hi - _BAND_START[qc]] += ds[
                            qc * CHUNK - rows.start:(qc + 1) * CHUNK - rows.start,
                            lo - bias_cols.start:hi - bias_cols.start]
                p_scr[hh, rows, cols] = p.astype(BF16)
                dsb_scr[hh, rows, cols] = ds.astype(BF16)
                if rest is not None:
                    p_scr[hh, rows, rest] = jnp.zeros((CHUNK, LANES), BF16)
                    dsb_scr[hh, rows, rest] = jnp.zeros((CHUNK, LANES), BF16)

        def ring(block):
            return block % KV_BLOCKS if isinstance(block, int) else lax.rem(block, KV_BLOCKS)

        def accumulate(i, hh):
            nwin = nwin_of(i)
            w = nwin * Q_BLOCK
            win = _block_rows(i + 1 - nwin, nwin)
            q_h, do_h = operands(i, hh)
            ds_b = dsb_scr[hh, :, :w]
            dq_h = _nn(ds_b, k_ref[win, :]) * ATTN_SCALE
            dkw = _tn(ds_b, q_h)
            dvw = _tn(p_scr[hh, :, :w], do_h)
            for b in range(nwin):
                slot = ring(i + 1 - nwin + b)
                part = slice(b * Q_BLOCK, (b + 1) * Q_BLOCK)
                if hh == 0 and b == nwin - 1:
                    dk_acc[slot] = dkw[part]
                    dv_acc[slot] = dvw[part]
                else:
                    dk_acc[slot] += dkw[part]
                    dv_acc[slot] += dvw[part]
            if hh == 0:
                dq_scr[...] = dq_h
            else:
                dq_ref[_block_rows(i, 1), :] = jnp.where(_head_lanes(0), dq_scr[...], dq_h).astype(BF16)
                if not (isinstance(i, int) and i < KV_BLOCKS - 1):
                    flush(i - (KV_BLOCKS - 1))

        def flush(block):
            dk_ref[_block_rows(block, 1), :] = dk_acc[ring(block)].astype(BF16)
            dv_ref[_block_rows(block, 1), :] = dv_acc[ring(block)].astype(BF16)

        def tile(n):
            return n // 2, n % 2

        def step(n):
            if n + 1 < 2 * nb:
                products(*tile(n + 1))
            grads(*tile(n))
            if n >= 1:
                accumulate(*tile(n - 1))

        products(0, 0)
        for n in range(2 * KV_BLOCKS):
            step(n)

        def two_steps(i, carry):
            products(i, 1)
            grads(i, 0)
            accumulate(i - 1, 1)
            products(i + 1, 0)
            grads(i, 1)
            accumulate(i, 0)
            return carry

        lax.fori_loop(KV_BLOCKS, nb - 1, two_steps, 0)
        step(2 * nb - 2)
        step(2 * nb - 1)
        accumulate(nb - 1, 1)
        flush(nb - 2)
        flush(nb - 1)

        @pl.when(pl.program_id(0) == N_PAIRS - 1)
        def _():
            for cp in exchange:
                cp.wait_recv()
            for cp in exchange:
                cp.wait_send()

    col = lambda c0: pl.BlockSpec((t, PAIR), lambda j: (0, c0 + j))
    once = pl.BlockSpec((t, PAIR), lambda j: (0, j), pipeline_mode=pl.Buffered(1))
    tile_spec = pl.BlockSpec((2, Q_BLOCK, KV_WINDOW), lambda j: (j, 0, 0))
    out = jax.ShapeDtypeStruct((t, ATTN_WIDTH), BF16)
    return pl.pallas_call(
        body,
        name="attn_bwd",
        grid=(N_PAIRS,),
        out_shape=(out, out, out, out,
                   jax.ShapeDtypeStruct((N_HEADS, Q_BLOCK // CHUNK, CHUNK, _BAND_WIDTH), F32),
                   jax.ShapeDtypeStruct((N_HEADS, Q_BLOCK, LANES), F32),
                   jax.ShapeDtypeStruct((N_DEV - 1, OUT_SHARD, D_MODEL), BF16)),
        in_specs=[col(0), col(N_PAIRS), col(2 * N_PAIRS), once, col(0), col(0), tile_spec,
                  pl.BlockSpec(memory_space=pl.ANY)],
        out_specs=(col(0), col(0), col(0), col(0),
                   pl.BlockSpec((2, Q_BLOCK // CHUNK, CHUNK, _BAND_WIDTH), lambda j: (j, 0, 0, 0)),
                   pl.BlockSpec((2, Q_BLOCK, LANES), lambda j: (j, 0, 0)),
                   pl.BlockSpec(memory_space=pl.ANY)),
        scratch_shapes=[
            pltpu.VMEM((t, PAIR), BF16),
            pltpu.VMEM((2, Q_BLOCK, KV_WINDOW), F32),
            pltpu.VMEM((2, Q_BLOCK, KV_WINDOW), F32),
            pltpu.VMEM((2, Q_BLOCK, KV_WINDOW), BF16),
            pltpu.VMEM((2, Q_BLOCK, KV_WINDOW), BF16),
            pltpu.VMEM((Q_BLOCK, PAIR), F32),
            pltpu.VMEM((KV_BLOCKS, Q_BLOCK, PAIR), F32),
            pltpu.VMEM((KV_BLOCKS, Q_BLOCK, PAIR), F32),
            pltpu.SemaphoreType.DMA((N_DEV - 1,)),
            pltpu.SemaphoreType.DMA((N_DEV - 1,)),
        ],
        compiler_params=_params(("arbitrary",), vmem=60 * 1024 * 1024),
    )(qkv, qkv, qkv, ag, o, dy_attn, bias_tile, dwout_g)


def _outproj_loss(x2d, tgt2d, y_pool, y_attn, wout_g, g2):
    t = x2d.shape[0]
    n_tiles = t // TOKEN_TILE

    def body(x_ref, tgt_ref, yp_ref, ya_ref, w_ref, g_ref,
             dx2_ref, dyp_ref, dya_ref, dw_ref, dg_ref, loss_ref, acc_ref):
        i = pl.program_id(0)

        @pl.when(i == 0)
        def _():
            acc_ref[...] = jnp.zeros_like(acc_ref)
            dg_ref[...] = jnp.zeros_like(dg_ref)
            loss_ref[...] = jnp.zeros_like(loss_ref)

        w = w_ref[...].reshape(D_MODEL, D_MODEL)
        y = jnp.concatenate([yp_ref[...], ya_ref[...]], axis=1)
        x2 = x_ref[...] + _nn(y, w)
        r = lax.rsqrt(jnp.mean(x2 * x2, axis=-1, keepdims=True) + EPS)
        xh = x2 * r
        g = g_ref[...]
        diff = xh * g - tgt_ref[...]
        tok = jnp.sum(diff * diff, axis=-1, keepdims=True) * (1.0 / D_MODEL)
        loss_ref[...] += jnp.sum(tok, axis=0, keepdims=True)
        dout = diff * (1.0 / D_MODEL)
        dg_ref[...] += jnp.sum(dout * xh, axis=0, keepdims=True)
        u = dout * g
        dx2 = r * (u - xh * jnp.mean(u * xh, axis=-1, keepdims=True))
        dx2_ref[...] = dx2
        dx2_b = dx2.astype(BF16)
        dy = _nt(dx2_b, w)
        dyp_ref[...] = dy[:, :POOL_WIDTH].astype(BF16)
        dya_ref[...] = dy[:, POOL_WIDTH:].astype(BF16)
        acc_ref[...] += _tn(y, dx2_b)

        @pl.when(i == n_tiles - 1)
        def _():
            dw_ref[...] = acc_ref[...].reshape(N_DEV, OUT_SHARD, D_MODEL).astype(BF16)

    tile = lambda width: pl.BlockSpec((TOKEN_TILE, width), lambda i: (i, 0))
    return pl.pallas_call(
        body,
        name="outproj_loss",
        grid=(n_tiles,),
        out_shape=(
            jax.ShapeDtypeStruct((t, D_MODEL), F32),
            jax.ShapeDtypeStruct((t, POOL_WIDTH), BF16),
            jax.ShapeDtypeStruct((t, ATTN_WIDTH), BF16),
            jax.ShapeDtypeStruct((N_DEV, OUT_SHARD, D_MODEL), BF16),
            jax.ShapeDtypeStruct((1, D_MODEL), F32),
            jax.ShapeDtypeStruct((8, LANES), F32),
        ),
        in_specs=[
            tile(D_MODEL), tile(D_MODEL), tile(POOL_WIDTH), tile(ATTN_WIDTH),
            pl.BlockSpec((N_DEV, OUT_SHARD, D_MODEL), lambda i: (0, 0, 0)),
            pl.BlockSpec((1, D_MODEL), lambda i: (0, 0)),
        ],
        out_specs=(
            tile(D_MODEL), tile(POOL_WIDTH), tile(ATTN_WIDTH),
            pl.BlockSpec((N_DEV, OUT_SHARD, D_MODEL), lambda i: (0, 0, 0)),
            pl.BlockSpec((1, D_MODEL), lambda i: (0, 0)),
            pl.BlockSpec((8, LANES), lambda i: (0, 0)),
        ),
        scratch_shapes=[pltpu.VMEM((D_MODEL, D_MODEL), F32)],
        compiler_params=_params(("arbitrary",)),
    )(x2d, tgt2d, y_pool, y_attn, wout_g, g2)


def _dproj_specs():
    tile = lambda width: pl.BlockSpec((TOKEN_TILE, width), lambda i: (i, 0))
    return [tile(2 * POOL_WIDTH)] + [tile(ATTN_WIDTH)] * 4


def _inproj_bwd_dx(x2d, dx2, dproj, g1, wg):
    t = x2d.shape[0]

    def body(x_ref, dx2_ref, dp_ref, dq_ref, dk_ref, dv_ref, dag_ref, g_ref, wg_hbm, gx_ref, dg_ref, wfull_ref, sem):
        @pl.when(pl.program_id(0) == 0)
        def _():
            _load_w_in(wg_hbm, wfull_ref, sem)
            dg_ref[...] = jnp.zeros_like(dg_ref)

        dproj_t = jnp.concatenate([dp_ref[...], dq_ref[...], dk_ref[...], dv_ref[...], dag_ref[...]], axis=1)
        dh = _nt(dproj_t, wfull_ref[...])
        xf = x_ref[...]
        r = lax.rsqrt(jnp.mean(xf * xf, axis=-1, keepdims=True) + EPS)
        xh = xf * r
        dg_ref[...] += jnp.sum(dh * xh, axis=0, keepdims=True)
        u = dh * g_ref[...]
        gx_ref[...] = dx2_ref[...] + r * (u - xh * jnp.mean(u * xh, axis=-1, keepdims=True))

    tile = pl.BlockSpec((TOKEN_TILE, D_MODEL), lambda i: (i, 0))
    return pl.pallas_call(
        body,
        name="inproj_bwd_dx",
        grid=(t // TOKEN_TILE,),
        out_shape=(jax.ShapeDtypeStruct((t, D_MODEL), F32), jax.ShapeDtypeStruct((1, D_MODEL), F32)),
        in_specs=[tile, tile] + _dproj_specs() + [
            pl.BlockSpec((1, D_MODEL), lambda i: (0, 0)),
            pl.BlockSpec(memory_space=pl.ANY),
        ],
        out_specs=(tile, pl.BlockSpec((1, D_MODEL), lambda i: (0, 0))),
        scratch_shapes=[pltpu.VMEM((D_MODEL, IN_WIDTH), BF16), pltpu.SemaphoreType.DMA((N_DEV,))],
        compiler_params=_params(("arbitrary",)),
    )(x2d, dx2, *dproj, g1, wg)


def _inproj_bwd_dw(x2d, pvg, dy_pool, dattn, g1, pool_w, pool_scale):
    t = x2d.shape[0]
    n_tiles = t // TOKEN_TILE
    halo_per_tile = TOKEN_TILE // HALO
    last_halo = t // HALO - 1

    def body(x_ref, cur_ref, prev_ref, pgn_ref, dy_ref, dyn_ref, dq_ref, dk_ref, dv_ref, dag_ref, g_ref, pw_ref, ps_ref,
             out_ref, dp_ref, dpw_ref, dps_ref, acc_ref):
        i = pl.program_id(0)

        @pl.when(i == 0)
        def _():
            acc_ref[...] = jnp.zeros_like(acc_ref)
            dpw_ref[...] = jnp.zeros_like(dpw_ref)
            dps_ref[...] = jnp.zeros_like(dps_ref)

        xf = x_ref[...]
        r = lax.rsqrt(jnp.mean(xf * xf, axis=-1, keepdims=True) + EPS)
        h = ((xf * r) * g_ref[...]).astype(BF16)
        col = 2 * POOL_WIDTH
        for ref in (dq_ref, dk_ref, dv_ref, dag_ref):
            acc_ref[:, col:col + ATTN_WIDTH] += _tn(h, ref[...])
            col += ATTN_WIDTH
        _pool_bwd_tile(i, n_tiles, cur_ref, prev_ref, pgn_ref, dy_ref, dyn_ref, pw_ref, ps_ref, dp_ref, dpw_ref, dps_ref)
        for c0 in (0, POOL_WIDTH):
            acc_ref[:, c0:c0 + POOL_WIDTH] += _tn(h, dp_ref[:, c0:c0 + POOL_WIDTH])

        @pl.when(i == n_tiles - 1)
        def _():
            for d in range(N_DEV):
                out_ref[d] = acc_ref[:, d * IN_SHARD:(d + 1) * IN_SHARD].astype(BF16)

    tile = lambda width: pl.BlockSpec((TOKEN_TILE, width), lambda i: (i, 0))
    next_halo = lambda col: pl.BlockSpec(
        (HALO, POOL_WIDTH), lambda i: (jnp.minimum((i + 1) * halo_per_tile, last_halo), col))
    return pl.pallas_call(
        body,
        name="inproj_bwd_dw",
        grid=(n_tiles,),
        out_shape=(
            jax.ShapeDtypeStruct((N_DEV, D_MODEL, IN_SHARD), BF16),
            jax.ShapeDtypeStruct((t, 2 * POOL_WIDTH), BF16),
            jax.ShapeDtypeStruct((N_GROUPS, GROUP_DIM, GROUP_DIM), F32),
            jax.ShapeDtypeStruct((1, POOL_WIDTH), F32),
        ),
        in_specs=[
            tile(D_MODEL),
            tile(2 * POOL_WIDTH),
            pl.BlockSpec((HALO, POOL_WIDTH), lambda i: (jnp.maximum(i * halo_per_tile - 1, 0), 0)),
            next_halo(1),
            tile(POOL_WIDTH),
            next_halo(0),
            tile(ATTN_WIDTH), tile(ATTN_WIDTH), tile(ATTN_WIDTH), tile(ATTN_WIDTH),
            pl.BlockSpec((1, D_MODEL), lambda i: (0, 0)),
            pl.BlockSpec((N_GROUPS, GROUP_DIM, GROUP_DIM), lambda i: (0, 0, 0)),
            pl.BlockSpec((1, POOL_WIDTH), lambda i: (0, 0)),
        ],
        out_specs=(
            pl.BlockSpec((N_DEV, D_MODEL, IN_SHARD), lambda i: (0, 0, 0)),
            tile(2 * POOL_WIDTH),
            pl.BlockSpec((N_GROUPS, GROUP_DIM, GROUP_DIM), lambda i: (0, 0, 0)),
            pl.BlockSpec((1, POOL_WIDTH), lambda i: (0, 0)),
        ),
        scratch_shapes=[pltpu.VMEM((D_MODEL, IN_WIDTH), F32)],
        compiler_params=_params(("arbitrary",)),
    )(x2d, pvg, pvg, pvg, dy_pool, dy_pool, *dattn, g1, pool_w, pool_scale)


_HBM = pl.BlockSpec(memory_space=pltpu.HBM)
_SEM = pl.BlockSpec(memory_space=pltpu.SEMAPHORE)
_DATAFLOW = pltpu.SideEffectType.DATAFLOW_SIDE_EFFECTING


def _exchange_start(blocks):
    land_shape = (N_DEV - 1,) + blocks.shape[1:]

    def body(src_hbm, land_hbm, send_sems, recv_sems, src_thru, land_thru, token):
        for cp in _exchange_copies(src_hbm, land_hbm, send_sems, recv_sems):
            cp.start()
        token[...] = jnp.zeros_like(token)

    sems = pltpu.SemaphoreType.DMA((N_DEV - 1,))
    return pl.pallas_call(
        body,
        name="exchange_start",
        out_shape=(sems, sems, pltpu.HBM(blocks.shape, blocks.dtype), pltpu.HBM(land_shape, blocks.dtype),
                   jax.ShapeDtypeStruct((8, LANES), F32)),
        in_specs=(_HBM, _HBM),
        out_specs=(_SEM, _SEM, _HBM, _HBM, pl.BlockSpec(memory_space=pltpu.VMEM)),
        input_output_aliases={0: 2, 1: 3},
        compiler_params=pltpu.CompilerParams(has_side_effects=_DATAFLOW),
    )(pltpu.with_memory_space_constraint(blocks, pltpu.HBM),
      pltpu.with_memory_space_constraint(lax.empty(land_shape, blocks.dtype), pltpu.HBM))


def _exchange_wait(send_sems, recv_sems, blocks, land, after):
    def body(src_hbm, land_hbm, send_sems, recv_sems, after_ref, src_out, land_out):
        for cp in _exchange_copies(src_hbm, land_hbm, send_sems, recv_sems):
            cp.wait_send()
            cp.wait_recv()

    return pl.pallas_call(
        body,
        name="exchange_wait",
        out_shape=(pltpu.HBM(blocks.shape, blocks.dtype), pltpu.HBM(land.shape, land.dtype)),
        in_specs=(_HBM, _HBM, _SEM, _SEM, pl.BlockSpec(memory_space=pl.ANY)),
        out_specs=(_HBM, _HBM),
        input_output_aliases={0: 0, 1: 1},
        compiler_params=pltpu.CompilerParams(has_side_effects=_DATAFLOW),
    )(blocks, land, send_sems, recv_sems, after)


def _adamw(w, g, m, v):
    m = ADAM_B1 * m + (1.0 - ADAM_B1) * g
    v = ADAM_B2 * v + (1.0 - ADAM_B2) * (g * g)
    m_hat = m / (1.0 - ADAM_B1 ** ADAM_STEP)
    v_hat = v / (1.0 - ADAM_B2 ** ADAM_STEP)
    delta = -ADAM_LR * (m_hat / (jnp.sqrt(v_hat) + ADAM_EPS) + ADAM_WD * w)
    return delta, m, v


_ROW_G1, _ROW_G2, _ROW_PS, _ROW_LOSS, _ROW_RB = 0, 1, 2, 3, 8
_VEC_ROWS = 16


def _small_allreduce(d_g1, d_g2, d_ps, d_rb, d_pw, loss_row):
    def body(g1_ref, g2_ref, ps_ref, rb_ref, pw_ref, loss_ref, vec_out, pw_out, vec_ref, rvec_ref, rpw_ref,
             send_sems, recv_sems):
        me = _dev_index(_mesh_pos())
        vec_ref[...] = jnp.zeros_like(vec_ref)
        vec_ref[_ROW_G1:_ROW_G1 + 1, :] = g1_ref[...]
        vec_ref[_ROW_G2:_ROW_G2 + 1, :] = g2_ref[...]
        vec_ref[_ROW_PS:_ROW_PS + 1, :POOL_WIDTH] = ps_ref[...]
        vec_ref[_ROW_LOSS:_ROW_LOSS + 1, :LANES] = loss_ref[0:1, :]
        vec_ref[_ROW_RB:_ROW_RB + N_HEADS, :2 * LANES] = rb_ref[...]

        def send(r, src, land, k):
            return pltpu.make_async_remote_copy(
                src_ref=src, dst_ref=land.at[r], send_sem=send_sems.at[2 * (r - 1) + k],
                recv_sem=recv_sems.at[2 * (r - 1) + k], device_id=_peer(r), device_id_type=MESH_ID)

        started = [cp for r in range(1, N_DEV) for cp in (send(r, vec_ref, rvec_ref, 0), send(r, pw_ref, rpw_ref, 1))]
        for cp in started:
            cp.start()
        rvec_ref[0] = vec_ref[...]
        rpw_ref[0] = pw_ref[...]
        for cp in started:
            cp.wait_recv()
        for cp in started:
            cp.wait_send()
        vec = rvec_ref[me]
        pw = rpw_ref[me]
        for s in range(1, N_DEV):
            vec = vec + rvec_ref[me ^ s]
            pw = pw + rpw_ref[me ^ s]
        vec_out[...] = vec
        pw_out[...] = pw

    vm = pl.BlockSpec(memory_space=pltpu.VMEM)
    return pl.pallas_call(
        body,
        name="small_allreduce",
        out_shape=(jax.ShapeDtypeStruct((_VEC_ROWS, D_MODEL), F32),
                   jax.ShapeDtypeStruct((N_GROUPS, GROUP_DIM, GROUP_DIM), F32)),
        in_specs=[vm] * 6,
        out_specs=(vm, vm),
        scratch_shapes=[
            pltpu.VMEM((_VEC_ROWS, D_MODEL), F32),
            pltpu.VMEM((N_DEV, _VEC_ROWS, D_MODEL), F32),
            pltpu.VMEM((N_DEV, N_GROUPS, GROUP_DIM, GROUP_DIM), F32),
            pltpu.SemaphoreType.DMA((2 * (N_DEV - 1),)),
            pltpu.SemaphoreType.DMA((2 * (N_DEV - 1),)),
        ],
        compiler_params=_params(),
    )(d_g1, d_g2, d_ps, d_rb, d_pw, loss_row)


def _adamw_all(dwin_g, land_in, dwout_g, land_out, vec_sum, pw_sum, weights, big):
    small_shapes = [(1, D_MODEL), (1, D_MODEL), (1, POOL_WIDTH), (N_HEADS, 2 * LANES), (N_GROUPS, GROUP_DIM, GROUP_DIM)]

    def body(*refs):
        refs = list(refs)
        take = lambda n: [refs.pop(0) for _ in range(n)]
        dwin_hbm, lin_ref, dwout_hbm, lout_ref, vec_ref, pw_ref = take(6)
        small_wmv = [take(3) for _ in range(5)]
        big_wmv = [take(3) for _ in range(2)]
        big_out = [take(4) for _ in range(2)]
        small_out = [take(4) for _ in range(5)]
        own_in, own_out, local_sems = refs

        me = _dev_index(_mesh_pos())
        mine = [pltpu.make_async_copy(dwin_hbm.at[me], own_in, local_sems.at[0]),
                pltpu.make_async_copy(dwout_hbm.at[me], own_out, local_sems.at[1])]
        for cp in mine:
            cp.start()

        def update(g, wmv, outs):
            delta, m_new, v_new = _adamw(wmv[0][...], g, wmv[1][...], wmv[2][...])
            for ref, val in zip(outs, (g, delta, m_new, v_new)):
                ref[...] = val

        update(vec_ref[_ROW_G1:_ROW_G1 + 1, :], small_wmv[0], small_out[0])
        update(vec_ref[_ROW_G2:_ROW_G2 + 1, :], small_wmv[1], small_out[1])
        update(vec_ref[_ROW_PS:_ROW_PS + 1, :POOL_WIDTH], small_wmv[2], small_out[2])
        update(vec_ref[_ROW_RB:_ROW_RB + N_HEADS, :2 * LANES], small_wmv[3], small_out[3])
        update(pw_ref[...], small_wmv[4], small_out[4])
        for cp in mine:
            cp.wait()
        g_in = own_in[...].astype(F32)
        g_out = own_out[...].astype(F32)
        for r in range(N_DEV - 1):
            g_in = g_in + lin_ref[r].astype(F32)
            g_out = g_out + lout_ref[r].astype(F32)
        update(g_in, big_wmv[0], big_out[0])
        update(g_out, big_wmv[1], big_out[1])

    vm = pl.BlockSpec(memory_space=pltpu.VMEM)
    hbm = pl.BlockSpec(memory_space=pl.ANY)
    f32 = lambda shape: jax.ShapeDtypeStruct(shape, F32)
    out_shapes = [f32((D_MODEL, IN_SHARD))] * 4 + [f32((OUT_SHARD, D_MODEL))] * 4
    for shape in small_shapes:
        out_shapes += [f32(shape)] * 4
    args = [dwin_g, land_in, dwout_g, land_out, vec_sum, pw_sum]
    for wmv in weights:
        args += list(wmv)
    for wmv in big:
        args += list(wmv)
    return pl.pallas_call(
        body,
        name="adamw_all",
        out_shape=tuple(out_shapes),
        in_specs=[hbm, vm, hbm, vm] + [vm] * (len(args) - 4),
        out_specs=tuple([vm] * len(out_shapes)),
        scratch_shapes=[
            pltpu.VMEM((D_MODEL, IN_SHARD), BF16),
            pltpu.VMEM((OUT_SHARD, D_MODEL), BF16),
            pltpu.SemaphoreType.DMA((2,)),
        ],
        compiler_params=_params(),
    )(*args)


def kernel(x, norm_gain, w_in, pool_w, pool_scale, rel_bias, w_out, final_norm_gain, loss_target, m_norm_gain, m_w_in, m_pool_w, m_pool_scale, m_rel_bias, m_w_out, m_final_norm_gain, v_norm_gain, v_w_in, v_pool_w, v_pool_scale, v_rel_bias, v_w_out, v_final_norm_gain):
    t = x.shape[1]
    assert x.shape[0] == 1 and t % TOKEN_TILE == 0 and t // Q_BLOCK >= 4
    x2d = x[0]
    tgt2d = loss_target[0]
    g2 = final_norm_gain.reshape(1, D_MODEL)

    rb = rel_bias[0]
    rel_line = jnp.concatenate([
        jnp.broadcast_to(rb[:, :1], (N_HEADS, _REL_FIRST)), rb,
        jnp.broadcast_to(rb[:, N_REL - 1:], (N_HEADS, TOEPLITZ - _REL_FIRST - N_REL)),
    ], axis=1).reshape(N_HEADS, 1, TOEPLITZ)
    wg_in, bias_tile = _gather_weights(w_in[0], rel_line)

    pvg, qkv, ag, y_pool = _norm_inproj(x2d, norm_gain, wg_in, pool_w[0], pool_scale)
    o, y_attn, wg_out = _attn_fwd(qkv, ag, bias_tile, w_out[0])
    dx2, dy_pool, dy_attn, dwout_g, d_g2, loss_sum = _outproj_loss(x2d, tgt2d, y_pool, y_attn, wg_out, g2)
    dq, dk, dv, dag, ds_band, ds_total, land_out = _attn_bwd(qkv, ag, o, dy_attn, bias_tile, dwout_g)
    d_rb = _bias_grad(ds_band, ds_total)
    dwin_g, d_pool, d_pw, d_ps = _inproj_bwd_dw(x2d, pvg, dy_pool, (dq, dk, dv, dag), norm_gain, pool_w[0], pool_scale)
    dproj = (d_pool, dq, dk, dv, dag)
    send_sems, recv_sems, dwin_g, land_in, token = _exchange_start(dwin_g)
    grad_x, d_g1 = _inproj_bwd_dx(x2d, dx2, dproj, norm_gain + token[:1, :1], wg_in)
    dwin_g, land_in = _exchange_wait(send_sems, recv_sems, dwin_g, land_in, d_g1)

    pad_rb = lambda a: jnp.pad(a[0], ((0, 0), (0, 2 * LANES - N_REL)))
    row = lambda a: a.reshape(1, D_MODEL)
    weights = [
        (norm_gain, m_norm_gain, v_norm_gain),
        (row(final_norm_gain), row(m_final_norm_gain), row(v_final_norm_gain)),
        (pool_scale, m_pool_scale, v_pool_scale),
        (pad_rb(rel_bias), pad_rb(m_rel_bias), pad_rb(v_rel_bias)),
        (pool_w[0], m_pool_w[0], v_pool_w[0]),
    ]
    big = [(w_in[0], m_w_in[0], v_w_in[0]), (w_out[0], m_w_out[0], v_w_out[0])]
    vec_sum, pw_sum = _small_allreduce(d_g1, d_g2, d_ps, d_rb, d_pw, loss_sum)
    res = _adamw_all(dwin_g, land_in, dwout_g, land_out, vec_sum, pw_sum, weights, big)
    loss = 0.5 * vec_sum[_ROW_LOSS, 0]

    def leaves(k):
        g1_, g2_, ps_, rb_, pw_ = (res[8 + 4 * leaf + k] for leaf in range(5))
        return [g1_, res[k][None], pw_[None], ps_, rb_[None, :, :N_REL], res[4 + k][None], g2_.reshape(D_MODEL)]

    return (loss, grad_x[None], *leaves(0), *leaves(1), *leaves(2), *leaves(3))
```

```python
import math

import jax
import jax.numpy as jnp
from jax import lax
from jax.experimental import pallas as pl
from jax.experimental.pallas import tpu as pltpu

F32 = jnp.float32
BF16 = jnp.bfloat16
MESH_ID = pl.DeviceIdType.MESH

D_MODEL = 1024
POOL_WIDTH = 512
ATTN_WIDTH = 512
POOL_WINDOWS = (2, 4, 8, 16)
N_GROUPS = 4
GROUP_DIM = 128
HEAD_DIM = 64
N_HEADS = 8
CHUNK = 64
LEFT_CHUNKS = 8
MAX_REL = 64
N_REL = 2 * MAX_REL + 1
IN_WIDTH = 2 * POOL_WIDTH + 4 * ATTN_WIDTH
EPS = 1e-6
MASK_VALUE = -1e30
ATTN_SCALE = 1.0 / math.sqrt(HEAD_DIM)
ADAM_LR = 0.001
ADAM_B1 = 0.9
ADAM_B2 = 0.999
ADAM_EPS = 1e-08
ADAM_WD = 0.01
ADAM_STEP = 10

N_DEV = 8
IN_SHARD = IN_WIDTH // N_DEV
OUT_SHARD = D_MODEL // N_DEV

LANES = 128
TOKEN_TILE = 512
HALO = 16
Q_BLOCK = 256
KV_BLOCKS = 3
KV_WINDOW = KV_BLOCKS * Q_BLOCK
PAIR = 2 * HEAD_DIM
N_PAIRS = N_HEADS // 2
TOEPLITZ = 1024
VMEM_LIMIT = 56 * 1024 * 1024


def _params(sem=None, vmem=VMEM_LIMIT):
    return pltpu.CompilerParams(dimension_semantics=sem, vmem_limit_bytes=vmem)


def _sigmoid(x):
    return 1.0 / (1.0 + jnp.exp(-x))


def _nt(a, b):
    return lax.dot_general(a, b, (((1,), (1,)), ((), ())), preferred_element_type=F32)


def _tn(a, b):
    return lax.dot_general(a, b, (((0,), (0,)), ((), ())), preferred_element_type=F32)


def _nn(a, b):
    return jnp.dot(a, b, preferred_element_type=F32)


def _mesh_pos():
    return lax.axis_index("x"), lax.axis_index("y"), lax.axis_index("c")


def _dev_index(p):
    return 4 * p[0] + 2 * p[1] + p[2]


def _peer(r):
    x, y, c = _mesh_pos()
    return (x ^ ((r >> 2) & 1), y ^ ((r >> 1) & 1), c ^ (r & 1))


def _exchange_copies(src_hbm, land_hbm, send_sems, recv_sems, first_sem=0, same_for_all=False):
    return [
        pltpu.make_async_remote_copy(
            src_ref=src_hbm if same_for_all else src_hbm.at[_dev_index(_peer(r))], dst_ref=land_hbm.at[r - 1],
            send_sem=send_sems.at[first_sem + r - 1], recv_sem=recv_sems.at[first_sem + r - 1],
            device_id=_peer(r), device_id_type=MESH_ID)
        for r in range(1, N_DEV)
    ]


def _gather_weights(w_in_shard, rel_line):
    def body(win_ref, line_ref, gin_ref, bias_ref, sin_ref, send_sems, recv_sems):
        x, y, c = _mesh_pos()
        me, sibling = (x, y, c), (x, y, 1 - c)
        chips = [(1 - x, y), (x, 1 - y), (1 - x, 1 - y)]

        sin_ref[...] = win_ref[...].astype(BF16)
        gin_ref[_dev_index(me)] = sin_ref[...]

        def copy(k, block, to, from_shard=False):
            return pltpu.make_async_remote_copy(
                src_ref=sin_ref if from_shard else gin_ref.at[_dev_index(block)],
                dst_ref=gin_ref.at[_dev_index(block)],
                send_sem=send_sems.at[k],
                recv_sem=recv_sems.at[k],
                device_id=to,
                device_id_type=MESH_ID,
            )

        first = [copy(0, me, sibling, True)]
        first += [copy(1 + j, me, (*chip, c), True) for j, chip in enumerate(chips)]
        for cp in first:
            cp.start()
        passed = [copy(4 + j, (*chip, c), sibling) for j, chip in enumerate(chips)]

        def bias_heads(lo, hi):
            for h in range(lo, hi):
                bias_ref[h] = _toeplitz_bias(line_ref[h])

        bias_heads(0, N_HEADS - 3)
        for j, chip in enumerate(chips):
            copy(1 + j, (*chip, c), me).wait_recv()
            passed[j].start()
            bias_heads(N_HEADS - 3 + j, N_HEADS - 2 + j)
        copy(0, sibling, me).wait_recv()
        for j, chip in enumerate(chips):
            copy(4 + j, (*chip, 1 - c), me).wait_recv()
        for cp in first + passed:
            cp.wait_send()

    vm = pl.BlockSpec(memory_space=pltpu.VMEM)
    return pl.pallas_call(
        body,
        name="gather_weights",
        out_shape=(
            jax.ShapeDtypeStruct((N_DEV, D_MODEL, IN_SHARD), BF16),
            jax.ShapeDtypeStruct((N_HEADS, Q_BLOCK, KV_WINDOW), F32),
        ),
        in_specs=[vm, vm],
        out_specs=(vm, vm),
        scratch_shapes=[
            pltpu.VMEM((D_MODEL, IN_SHARD), BF16),
            pltpu.SemaphoreType.DMA((7,)),
            pltpu.SemaphoreType.DMA((7,)),
        ],
        compiler_params=_params(),
    )(w_in_shard, rel_line)


def _load_w_in(wg_hbm, wfull_ref, sem):
    copies = [
        pltpu.make_async_copy(wg_hbm.at[d], wfull_ref.at[:, d * IN_SHARD:(d + 1) * IN_SHARD], sem.at[d])
        for d in range(N_DEV)
    ]
    for cp in copies:
        cp.start()
    for cp in copies:
        cp.wait()


def _norm_inproj(x2d, g1, wg, pool_w, pool_scale):
    t = x2d.shape[0]

    def body(x_ref, g_ref, wg_hbm, pw_ref, ps_ref, pvg_ref, qkv_ref, ag_ref, yp_ref, wfull_ref, halo_ref, sem):
        i = pl.program_id(0)

        @pl.when(i == 0)
        def _():
            _load_w_in(wg_hbm, wfull_ref, sem)
            halo_ref[...] = jnp.zeros_like(halo_ref)

        xf = x_ref[...]
        r = lax.rsqrt(jnp.mean(xf * xf, axis=-1, keepdims=True) + EPS)
        h = ((xf * r) * g_ref[...]).astype(BF16)
        chunk = lambda ci: _nn(h, wfull_ref[:, ci * POOL_WIDTH:(ci + 1) * POOL_WIDTH])
        pv, pg = chunk(0), chunk(1)
        pvg_ref[:, :POOL_WIDTH] = pv
        pvg_ref[:, POOL_WIDTH:] = pg
        halo = halo_ref[...]
        halo_ref[...] = pv[TOKEN_TILE - HALO:]
        for gi in range(N_GROUPS):
            sl = slice(gi * GROUP_DIM, (gi + 1) * GROUP_DIM)
            d = _pool_diffs(pv, halo, i * TOKEN_TILE, gi)
            z = _nn(d.astype(BF16), pw_ref[gi].astype(BF16))
            g = pg[:, sl]
            yp_ref[:, sl] = ((z * ps_ref[:, sl]) * (g * _sigmoid(g))).astype(BF16)
        for ci in range(2, 5):
            qkv_ref[:, (ci - 2) * POOL_WIDTH:(ci - 1) * POOL_WIDTH] = chunk(ci).astype(BF16)
        ag_ref[...] = chunk(5)

    tile = lambda width: pl.BlockSpec((TOKEN_TILE, width), lambda i: (i, 0))
    return pl.pallas_call(
        body,
        name="norm_inproj",
        grid=(t // TOKEN_TILE,),
        out_shape=(
            jax.ShapeDtypeStruct((t, 2 * POOL_WIDTH), F32),
            jax.ShapeDtypeStruct((t, 3 * ATTN_WIDTH), BF16),
            jax.ShapeDtypeStruct((t, ATTN_WIDTH), F32),
            jax.ShapeDtypeStruct((t, POOL_WIDTH), BF16),
        ),
        in_specs=[
            tile(D_MODEL),
            pl.BlockSpec((1, D_MODEL), lambda i: (0, 0)),
            pl.BlockSpec(memory_space=pl.ANY),
            pl.BlockSpec((N_GROUPS, GROUP_DIM, GROUP_DIM), lambda i: (0, 0, 0)),
            pl.BlockSpec((1, POOL_WIDTH), lambda i: (0, 0)),
        ],
        out_specs=(tile(2 * POOL_WIDTH), tile(3 * ATTN_WIDTH), tile(ATTN_WIDTH), tile(POOL_WIDTH)),
        scratch_shapes=[
            pltpu.VMEM((D_MODEL, IN_WIDTH), BF16),
            pltpu.VMEM((HALO, POOL_WIDTH), F32),
            pltpu.SemaphoreType.DMA((N_DEV,)),
        ],
        compiler_params=_params(("arbitrary",)),
    )(x2d, g1, wg, pool_w, pool_scale)


def _inv_count(first_row, rows, window):
    tpos = first_row + lax.broadcasted_iota(jnp.int32, (rows, 1), 0)
    return 1.0 / jnp.minimum(tpos + 1, window).astype(F32)


def _causal_window_sum(ext, window):
    s, k = ext, 1
    while k < window:
        s = s + pltpu.roll(s, k, 0)
        k *= 2
    return s


def _pool_diffs(pv, halo, first_row, gi):
    w = POOL_WINDOWS[gi]
    sl = slice(gi * GROUP_DIM, (gi + 1) * GROUP_DIM)
    ext = jnp.concatenate([halo[:, sl], pv[:, sl]], axis=0)
    s = _causal_window_sum(ext, w)[HALO:]
    return s * _inv_count(first_row, pv.shape[0], w) - pv[:, sl]


def _pool_bwd_tile(i, n_tiles, cur_ref, prev_ref, pgn_ref, dy_ref, dyn_ref, pw_ref, ps_ref, dp_ref, dpw_ref, dps_ref):
    pv = cur_ref[:, :POOL_WIDTH]
    pg = cur_ref[:, POOL_WIDTH:]
    prev = jnp.where(i > 0, prev_ref[...], 0.0)
    has_next = i < n_tiles - 1
    rows = TOKEN_TILE + HALO
    for gi in range(N_GROUPS):
        w = POOL_WINDOWS[gi]
        sl = slice(gi * GROUP_DIM, (gi + 1) * GROUP_DIM)
        pw = pw_ref[gi].astype(BF16)
        ps = ps_ref[:, sl]
        d = _pool_diffs(pv, prev, i * TOKEN_TILE, gi).astype(BF16)
        z = _nn(d, pw)
        g_ext = jnp.concatenate([pg[:, sl], pgn_ref[:, sl]], axis=0)
        dy_ext = jnp.concatenate([dy_ref[:, sl], dyn_ref[:, sl]], axis=0).astype(F32)
        sig = _sigmoid(g_ext)
        gate = g_ext * sig
        dz_ext = ((dy_ext * gate) * ps).astype(BF16)
        dd_ext = _nt(dz_ext, pw)
        e = dd_ext * _inv_count(i * TOKEN_TILE, rows, w)
        row = lax.broadcasted_iota(jnp.int32, (rows, 1), 0)
        e = jnp.where(jnp.logical_or(row < TOKEN_TILE, has_next), e, 0.0)
        s, k = e, 1
        while k < w:
            s = s + pltpu.roll(s, rows - k, 0)
            k *= 2
        dp_ref[:, sl] = (s[:TOKEN_TILE] - dd_ext[:TOKEN_TILE]).astype(BF16)
        dy = dy_ext[:TOKEN_TILE]
        g = g_ext[:TOKEN_TILE]
        sg = sig[:TOKEN_TILE]
        dgate = sg * (1.0 + g * (1.0 - sg))
        dp_ref[:, POOL_WIDTH + gi * GROUP_DIM:POOL_WIDTH + (gi + 1) * GROUP_DIM] = (
            (dy * (z * ps)) * dgate).astype(BF16)
        dps_ref[:, sl] += jnp.sum((dy * gate[:TOKEN_TILE]) * z, axis=0, keepdims=True)
        dpw_ref[gi] += _tn(d, dz_ext[:TOKEN_TILE])


_REL_FIRST = KV_WINDOW - 1 - MAX_REL


def _skew_rows(a, right):
    rows, lanes = a.shape
    row = lax.broadcasted_iota(jnp.int32, a.shape, 0)
    for b in range(rows.bit_length() - 1):
        shift = (1 << b) if right else lanes - (1 << b)
        a = jnp.where((row >> b) & 1 == 1, pltpu.roll(a, shift, 1), a)
    return a


def _toeplitz_bias(line):
    a = jnp.broadcast_to(line, (Q_BLOCK, TOEPLITZ))
    a = _skew_rows(a, True)
    a = pltpu.roll(a, TOEPLITZ - (Q_BLOCK - 1), 1)
    a = a[:, :KV_WINDOW]
    qc = lax.broadcasted_iota(jnp.int32, a.shape, 0) // CHUNK
    kc = lax.broadcasted_iota(jnp.int32, a.shape, 1) // CHUNK
    visible = jnp.logical_and(kc >= qc, kc <= qc + LEFT_CHUNKS)
    return jnp.where(visible, a, MASK_VALUE)


_BAND_WIDTH = 2 * LANES
_BAND_START = (384, 512, 512, 512)
_BAND_REL = tuple(a - (KV_BLOCKS - 1) * Q_BLOCK - qc * CHUNK for qc, a in enumerate(_BAND_START))


_ROW_G1, _ROW_G2, _ROW_PS, _ROW_LOSS, _ROW_RB = 0, 1, 2, 3, 8
_VEC_ROWS = 16


def _pack_small(band, total, d_g2, d_ps, loss_row):
    def body(band_ref, total_ref, g2_ref, ps_ref, loss_ref, vec_ref):
        vec_ref[...] = jnp.zeros_like(vec_ref)
        vec_ref[_ROW_G2:_ROW_G2 + 1, :] = g2_ref[...]
        vec_ref[_ROW_PS:_ROW_PS + 1, :POOL_WIDTH] = ps_ref[...]
        vec_ref[_ROW_LOSS:_ROW_LOSS + 1, :LANES] = loss_ref[0:1, :]
        for h in range(N_HEADS):
            a = jnp.zeros((CHUNK, 2 * _BAND_WIDTH), F32)
            for qc in range(Q_BLOCK // CHUNK):
                z = jnp.concatenate([band_ref[h, qc], jnp.zeros((CHUNK, _BAND_WIDTH), F32)], axis=1)
                left = -MAX_REL - _BAND_REL[qc]
                a = a + (pltpu.roll(z, 2 * _BAND_WIDTH - left, 1) if left else z)
            a = _skew_rows(a, False)
            near = jnp.sum(a, axis=0, keepdims=True)[:, :2 * LANES]
            r = lax.broadcasted_iota(jnp.int32, near.shape, 1)
            near = jnp.where(jnp.logical_and(r >= 1, r < 2 * MAX_REL), near, 0.0)
            everything = jnp.sum(jnp.sum(total_ref[h], axis=0, keepdims=True), axis=1, keepdims=True)
            far = everything - jnp.sum(near, axis=1, keepdims=True)
            vec_ref[_ROW_RB + h:_ROW_RB + h + 1, :2 * LANES] = jnp.where(r == 0, far, near)

    vm = pl.BlockSpec(memory_space=pltpu.VMEM)
    return pl.pallas_call(
        body,
        name="pack_small",
        out_shape=jax.ShapeDtypeStruct((_VEC_ROWS, D_MODEL), F32),
        in_specs=[vm] * 5,
        out_specs=vm,
        compiler_params=_params(),
    )(band, total, d_g2, d_ps, loss_row)


def _head_lanes(hh):
    lane = lax.broadcasted_iota(jnp.int32, (1, PAIR), 1)
    return (lane < HEAD_DIM) if hh == 0 else (lane >= HEAD_DIM)


def _score_windows(nwin):
    if nwin < KV_BLOCKS:
        return [(slice(0, Q_BLOCK), slice(0, nwin * Q_BLOCK), None)]
    pieces = []
    for qc in range(Q_BLOCK // CHUNK):
        rows = slice(qc * CHUNK, (qc + 1) * CHUNK)
        if qc < 2:
            pieces.append((rows, slice(0, KV_WINDOW - LANES), slice(KV_WINDOW - LANES, KV_WINDOW)))
        else:
            pieces.append((rows, slice(LANES, KV_WINDOW), slice(0, LANES)))
    return pieces


def _block_rows(first_block, n_blocks):
    if isinstance(first_block, int):
        return pl.ds(first_block * Q_BLOCK, n_blocks * Q_BLOCK)
    return pl.ds(pl.multiple_of(first_block * Q_BLOCK, Q_BLOCK), n_blocks * Q_BLOCK)


def _attn_fwd(qkv, ag, bias_tile, w_out_shard):
    t = qkv.shape[0]
    nb = t // Q_BLOCK

    def body(q_ref, k_ref, v_ref, ag_ref, bias_ref, wout_ref, o_ref, y_ref, gout_hbm,
             s_scr, p_scr, linv_scr, sout_ref, send_sems, recv_sems, local_sem):
        pair = pl.program_id(0)
        me = _dev_index(_mesh_pos())
        mine = pltpu.make_async_copy(sout_ref, gout_hbm.at[me], local_sem)

        def shard_copy(r, block):
            return pltpu.make_async_remote_copy(
                src_ref=sout_ref, dst_ref=gout_hbm.at[block], send_sem=send_sems.at[r - 1],
                recv_sem=recv_sems.at[r - 1], device_id=_peer(r), device_id_type=MESH_ID)

        @pl.when(pair == 0)
        def _():
            sout_ref[...] = wout_ref[...].astype(BF16)
            mine.start()
            for r in range(1, N_DEV):
                shard_copy(r, me).start()

        def scores(i, slot, nwin):
            q = q_ref[_block_rows(i, 1), :]
            kw = k_ref[_block_rows(i + 1 - nwin, nwin), :]
            for hh in range(2):
                q_h = jnp.where(_head_lanes(hh), q, jnp.zeros_like(q)) * ATTN_SCALE
                s_scr[slot, hh, :, :nwin * Q_BLOCK] = _nt(q_h, kw)

        def softmax(slot, nwin):
            off = (KV_BLOCKS - nwin) * Q_BLOCK
            for hh in range(2):
                for rows, cols, rest in _score_windows(nwin):
                    s = s_scr[slot, hh, rows, cols] + bias_ref[hh, rows, off + cols.start:off + cols.stop]
                    e = jnp.exp(s - jnp.max(s, axis=-1, keepdims=True))
                    linv_scr[slot, hh, rows, :] = 1.0 / jnp.sum(e, axis=-1, keepdims=True)
                    p_scr[slot, hh, rows, cols] = e.astype(BF16)
                    if rest is not None:
                        p_scr[slot, hh, rows, rest] = jnp.zeros((CHUNK, LANES), BF16)

        def output(i, slot, nwin):
            vw = v_ref[_block_rows(i + 1 - nwin, nwin), :]
            outs = [_nn(p_scr[slot, hh, :, :nwin * Q_BLOCK], vw) * linv_scr[slot, hh] for hh in range(2)]
            o = jnp.where(_head_lanes(0), outs[0], outs[1])
            g = ag_ref[_block_rows(i, 1), :]
            o_ref[_block_rows(i, 1), :] = o.astype(BF16)
            y_ref[_block_rows(i, 1), :] = (o * (g * _sigmoid(g))).astype(BF16)

        scores(0, 0, 1)
        scores(1, 1, 2)
        softmax(0, 1)
        scores(2, 0, 3)
        softmax(1, 2)
        output(0, 0, 1)
        scores(3, 1, 3)
        softmax(0, 3)
        output(1, 1, 2)

        def two_steps(k, carry):
            i = 3 + 2 * k
            scores(i + 1, 0, KV_BLOCKS)
            softmax(1, KV_BLOCKS)
            output(i - 1, 0, KV_BLOCKS)
            scores(i + 2, 1, KV_BLOCKS)
            softmax(0, KV_BLOCKS)
            output(i, 1, KV_BLOCKS)
            return carry

        lax.fori_loop(0, (nb - 4) // 2, two_steps, 0)
        last = (nb - 1) % 2
        softmax(last, KV_BLOCKS)
        output(nb - 2, 1 - last, KV_BLOCKS)
        output(nb - 1, last, KV_BLOCKS)

        @pl.when(pair == N_PAIRS - 1)
        def _():
            for r in range(1, N_DEV):
                shard_copy(r, _dev_index(_peer(r))).wait_recv()
            for r in range(1, N_DEV):
                shard_copy(r, me).wait_send()
            mine.wait()

    col = lambda c0: pl.BlockSpec((t, PAIR), lambda j: (0, c0 + j))
    return pl.pallas_call(
        body,
        name="attn_fwd",
        grid=(N_PAIRS,),
        out_shape=(
            jax.ShapeDtypeStruct((t, ATTN_WIDTH), BF16),
            jax.ShapeDtypeStruct((t, ATTN_WIDTH), BF16),
            jax.ShapeDtypeStruct((N_DEV, OUT_SHARD, D_MODEL), BF16),
        ),
        in_specs=[col(0), col(N_PAIRS), col(2 * N_PAIRS), col(0),
                  pl.BlockSpec((2, Q_BLOCK, KV_WINDOW), lambda j: (j, 0, 0)),
                  pl.BlockSpec((OUT_SHARD, D_MODEL), lambda j: (0, 0))],
        out_specs=(col(0), col(0), pl.BlockSpec(memory_space=pl.ANY)),
        scratch_shapes=[
            pltpu.VMEM((2, 2, Q_BLOCK, KV_WINDOW), F32),
            pltpu.VMEM((2, 2, Q_BLOCK, KV_WINDOW), BF16),
            pltpu.VMEM((2, 2, Q_BLOCK, 1), F32),
            pltpu.VMEM((OUT_SHARD, D_MODEL), BF16),
            pltpu.SemaphoreType.DMA((N_DEV - 1,)),
            pltpu.SemaphoreType.DMA((N_DEV - 1,)),
            pltpu.SemaphoreType.DMA,
        ],
        compiler_params=_params(("arbitrary",)),
    )(qkv, qkv, qkv, ag, bias_tile, w_out_shard)


def _attn_bwd(qkv, ag, o, dy_attn, bias_tile, dwout_g):
    t = qkv.shape[0]
    nb = t // Q_BLOCK

    def body(q_ref, k_ref, v_ref, ag_ref, o_ref, dy_ref, bias_ref, dwout_hbm,
             dq_ref, dk_ref, dv_ref, dag_ref, band_ref, total_ref, land_hbm,
             do_scr, s_scr, dp_scr, p_scr, dsb_scr, dq_scr, dk_acc, dv_acc, send_sems, recv_sems):
        exchange = _exchange_copies(dwout_hbm, land_hbm, send_sems, recv_sems)

        @pl.when(pl.program_id(0) == 0)
        def _():
            for cp in exchange:
                cp.start()

        def gates(i, carry):
            rows = _block_rows(i, 1)
            g = ag_ref[rows, :]
            sig = _sigmoid(g)
            dy = dy_ref[rows, :].astype(F32)
            do_scr[rows, :] = (dy * (g * sig)).astype(BF16)
            dag_ref[rows, :] = ((dy * o_ref[rows, :].astype(F32)) * (sig * (1.0 + g * (1.0 - sig)))).astype(BF16)
            return carry

        lax.fori_loop(0, nb, gates, 0)
        band_ref[...] = jnp.zeros_like(band_ref)
        total_ref[...] = jnp.zeros_like(total_ref)

        def nwin_of(i):
            return min(i + 1, KV_BLOCKS) if isinstance(i, int) else KV_BLOCKS

        def operands(i, hh):
            lanes = _head_lanes(hh)
            q = q_ref[_block_rows(i, 1), :]
            do = do_scr[_block_rows(i, 1), :]
            return (jnp.where(lanes, q, jnp.zeros_like(q)) * ATTN_SCALE, jnp.where(lanes, do, jnp.zeros_like(do)))

        def products(i, hh):
            nwin = nwin_of(i)
            win = _block_rows(i + 1 - nwin, nwin)
            q_h, do_h = operands(i, hh)
            s_scr[hh, :, :nwin * Q_BLOCK] = _nt(q_h, k_ref[win, :])
            dp_scr[hh, :, :nwin * Q_BLOCK] = _nt(do_h, v_ref[win, :])

        def grads(i, hh):
            nwin = nwin_of(i)
            off = (KV_BLOCKS - nwin) * Q_BLOCK
            for rows, cols, rest in _score_windows(nwin):
                bias_cols = slice(off + cols.start, off + cols.stop)
                s = s_scr[hh, rows, cols] + bias_ref[hh, rows, bias_cols]
                e = jnp.exp(s - jnp.max(s, axis=-1, keepdims=True))
                p = e * (1.0 / jnp.sum(e, axis=-1, keepdims=True))
                dp = dp_scr[hh, rows, cols]
                ds = p * (dp - jnp.sum(p * dp, axis=-1, keepdims=True))
                total_ref[hh, rows, :] += sum(ds[:, c0:c0 + LANES] for c0 in range(0, ds.shape[1], LANES))
                for qc in range(rows.start // CHUNK, rows.stop // CHUNK):
                    lo = max(_BAND_START[qc], bias_cols.start)
                    hi = min(_BAND_START[qc] + _BAND_WIDTH, bias_cols.stop)
                    if lo < hi:
                        band_ref[hh, qc, :, lo - _BAND_START[qc]:hi - _BAND_START[qc]] += ds[
                            qc * CHUNK - rows.start:(qc + 1) * CHUNK - rows.start,
                            lo - bias_cols.start:hi - bias_cols.start]
                p_scr[hh, rows, cols] = p.astype(BF16)
                dsb_scr[hh, rows, cols] = ds.astype(BF16)
                if rest is not None:
                    p_scr[hh, rows, rest] = jnp.zeros((CHUNK, LANES), BF16)
                    dsb_scr[hh, rows, rest] = jnp.zeros((CHUNK, LANES), BF16)

        def ring(block):
            return block % KV_BLOCKS if isinstance(block, int) else lax.rem(block, KV_BLOCKS)

        def accumulate(i, hh):
            nwin = nwin_of(i)
            w = nwin * Q_BLOCK
            win = _block_rows(i + 1 - nwin, nwin)
            q_h, do_h = operands(i, hh)
            ds_b = dsb_scr[hh, :, :w]
            dq_h = _nn(ds_b, k_ref[win, :]) * ATTN_SCALE
            dkw = _tn(ds_b, q_h)
            dvw = _tn(p_scr[hh, :, :w], do_h)
            for b in range(nwin):
                slot = ring(i + 1 - nwin + b)
                part = slice(b * Q_BLOCK, (b + 1) * Q_BLOCK)
                if hh == 0 and b == nwin - 1:
                    dk_acc[slot] = dkw[part]
                    dv_acc[slot] = dvw[part]
                else:
                    dk_acc[slot] += dkw[part]
                    dv_acc[slot] += dvw[part]
            if hh == 0:
                dq_scr[...] = dq_h
            else:
                dq_ref[_block_rows(i, 1), :] = jnp.where(_head_lanes(0), dq_scr[...], dq_h).astype(BF16)
                if not (isinstance(i, int) and i < KV_BLOCKS - 1):
                    flush(i - (KV_BLOCKS - 1))

        def flush(block):
            dk_ref[_block_rows(block, 1), :] = dk_acc[ring(block)].astype(BF16)
            dv_ref[_block_rows(block, 1), :] = dv_acc[ring(block)].astype(BF16)

        def tile(n):
            return n // 2, n % 2

        def step(n):
            if n + 1 < 2 * nb:
                products(*tile(n + 1))
            grads(*tile(n))
            if n >= 1:
                accumulate(*tile(n - 1))

        products(0, 0)
        for n in range(2 * KV_BLOCKS):
            step(n)

        def two_steps(i, carry):
            products(i, 1)
            grads(i, 0)
            accumulate(i - 1, 1)
            products(i + 1, 0)
            grads(i, 1)
            accumulate(i, 0)
            return carry

        lax.fori_loop(KV_BLOCKS, nb - 1, two_steps, 0)
        step(2 * nb - 2)
        step(2 * nb - 1)
        accumulate(nb - 1, 1)
        flush(nb - 2)
        flush(nb - 1)

        @pl.when(pl.program_id(0) == N_PAIRS - 1)
        def _():
            for cp in exchange:
                cp.wait_recv()
            for cp in exchange:
                cp.wait_send()

    col = lambda c0: pl.BlockSpec((t, PAIR), lambda j: (0, c0 + j))
    once = pl.BlockSpec((t, PAIR), lambda j: (0, j), pipeline_mode=pl.Buffered(1))
    tile_spec = pl.BlockSpec((2, Q_BLOCK, KV_WINDOW), lambda j: (j, 0, 0))
    out = jax.ShapeDtypeStruct((t, ATTN_WIDTH), BF16)
    return pl.pallas_call(
        body,
        name="attn_bwd",
        grid=(N_PAIRS,),
        out_shape=(out, out, out, out,
                   jax.ShapeDtypeStruct((N_HEADS, Q_BLOCK // CHUNK, CHUNK, _BAND_WIDTH), F32),
                   jax.ShapeDtypeStruct((N_HEADS, Q_BLOCK, LANES), F32),
                   jax.ShapeDtypeStruct((N_DEV - 1, OUT_SHARD, D_MODEL), BF16)),
        in_specs=[col(0), col(N_PAIRS), col(2 * N_PAIRS), once, col(0), col(0), tile_spec,
                  pl.BlockSpec(memory_space=pl.ANY)],
        out_specs=(col(0), col(0), col(0), col(0),
                   pl.BlockSpec((2, Q_BLOCK // CHUNK, CHUNK, _BAND_WIDTH), lambda j: (j, 0, 0, 0)),
                   pl.BlockSpec((2, Q_BLOCK, LANES), lambda j: (j, 0, 0)),
                   pl.BlockSpec(memory_space=pl.ANY)),
        scratch_shapes=[
            pltpu.VMEM((t, PAIR), BF16),
            pltpu.VMEM((2, Q_BLOCK, KV_WINDOW), F32),
            pltpu.VMEM((2, Q_BLOCK, KV_WINDOW), F32),
            pltpu.VMEM((2, Q_BLOCK, KV_WINDOW), BF16),
            pltpu.VMEM((2, Q_BLOCK, KV_WINDOW), BF16),
            pltpu.VMEM((Q_BLOCK, PAIR), F32),
            pltpu.VMEM((KV_BLOCKS, Q_BLOCK, PAIR), F32),
            pltpu.VMEM((KV_BLOCKS, Q_BLOCK, PAIR), F32),
            pltpu.SemaphoreType.DMA((N_DEV - 1,)),
            pltpu.SemaphoreType.DMA((N_DEV - 1,)),
        ],
        compiler_params=_params(("arbitrary",), vmem=60 * 1024 * 1024),
    )(qkv, qkv, qkv, ag, o, dy_attn, bias_tile, dwout_g)


def _outproj_loss(x2d, tgt2d, y_pool, y_attn, wout_g, g2):
    t = x2d.shape[0]
    n_tiles = t // TOKEN_TILE

    def body(x_ref, tgt_ref, yp_ref, ya_ref, w_ref, g_ref,
             dx2_ref, dyp_ref, dya_ref, dw_ref, dg_ref, loss_ref, acc_ref):
        i = pl.program_id(0)

        @pl.when(i == 0)
        def _():
            acc_ref[...] = jnp.zeros_like(acc_ref)
            dg_ref[...] = jnp.zeros_like(dg_ref)
            loss_ref[...] = jnp.zeros_like(loss_ref)

        w = w_ref[...].reshape(D_MODEL, D_MODEL)
        y = jnp.concatenate([yp_ref[...], ya_ref[...]], axis=1)
        x2 = x_ref[...] + _nn(y, w)
        r = lax.rsqrt(jnp.mean(x2 * x2, axis=-1, keepdims=True) + EPS)
        xh = x2 * r
        g = g_ref[...]
        diff = xh * g - tgt_ref[...]
        tok = jnp.sum(diff * diff, axis=-1, keepdims=True) * (1.0 / D_MODEL)
        loss_ref[...] += jnp.sum(tok, axis=0, keepdims=True)
        dout = diff * (1.0 / D_MODEL)
        dg_ref[...] += jnp.sum(dout * xh, axis=0, keepdims=True)
        u = dout * g
        dx2 = r * (u - xh * jnp.mean(u * xh, axis=-1, keepdims=True))
        dx2_ref[...] = dx2
        dx2_b = dx2.astype(BF16)
        dy = _nt(dx2_b, w)
        dyp_ref[...] = dy[:, :POOL_WIDTH].astype(BF16)
        dya_ref[...] = dy[:, POOL_WIDTH:].astype(BF16)
        acc_ref[...] += _tn(y, dx2_b)

        @pl.when(i == n_tiles - 1)
        def _():
            dw_ref[...] = acc_ref[...].reshape(N_DEV, OUT_SHARD, D_MODEL).astype(BF16)

    tile = lambda width: pl.BlockSpec((TOKEN_TILE, width), lambda i: (i, 0))
    return pl.pallas_call(
        body,
        name="outproj_loss",
        grid=(n_tiles,),
        out_shape=(
            jax.ShapeDtypeStruct((t, D_MODEL), F32),
            jax.ShapeDtypeStruct((t, POOL_WIDTH), BF16),
            jax.ShapeDtypeStruct((t, ATTN_WIDTH), BF16),
            jax.ShapeDtypeStruct((N_DEV, OUT_SHARD, D_MODEL), BF16),
            jax.ShapeDtypeStruct((1, D_MODEL), F32),
            jax.ShapeDtypeStruct((8, LANES), F32),
        ),
        in_specs=[
            tile(D_MODEL), tile(D_MODEL), tile(POOL_WIDTH), tile(ATTN_WIDTH),
            pl.BlockSpec((N_DEV, OUT_SHARD, D_MODEL), lambda i: (0, 0, 0)),
            pl.BlockSpec((1, D_MODEL), lambda i: (0, 0)),
        ],
        out_specs=(
            tile(D_MODEL), tile(POOL_WIDTH), tile(ATTN_WIDTH),
            pl.BlockSpec((N_DEV, OUT_SHARD, D_MODEL), lambda i: (0, 0, 0)),
            pl.BlockSpec((1, D_MODEL), lambda i: (0, 0)),
            pl.BlockSpec((8, LANES), lambda i: (0, 0)),
        ),
        scratch_shapes=[pltpu.VMEM((D_MODEL, D_MODEL), F32)],
        compiler_params=_params(("arbitrary",)),
    )(x2d, tgt2d, y_pool, y_attn, wout_g, g2)


def _dproj_specs():
    tile = lambda width: pl.BlockSpec((TOKEN_TILE, width), lambda i: (i, 0))
    return [tile(2 * POOL_WIDTH)] + [tile(ATTN_WIDTH)] * 4


def _inproj_bwd_dx(x2d, dx2, dproj, g1, wg):
    t = x2d.shape[0]

    def body(x_ref, dx2_ref, dp_ref, dq_ref, dk_ref, dv_ref, dag_ref, g_ref, wg_hbm, gx_ref, dg_ref, wfull_ref, sem):
        @pl.when(pl.program_id(0) == 0)
        def _():
            _load_w_in(wg_hbm, wfull_ref, sem)
            dg_ref[...] = jnp.zeros_like(dg_ref)

        dproj_t = jnp.concatenate([dp_ref[...], dq_ref[...], dk_ref[...], dv_ref[...], dag_ref[...]], axis=1)
        dh = _nt(dproj_t, wfull_ref[...])
        xf = x_ref[...]
        r = lax.rsqrt(jnp.mean(xf * xf, axis=-1, keepdims=True) + EPS)
        xh = xf * r
        dg_ref[...] += jnp.sum(dh * xh, axis=0, keepdims=True)
        u = dh * g_ref[...]
        gx_ref[...] = dx2_ref[...] + r * (u - xh * jnp.mean(u * xh, axis=-1, keepdims=True))

    tile = pl.BlockSpec((TOKEN_TILE, D_MODEL), lambda i: (i, 0))
    return pl.pallas_call(
        body,
        name="inproj_bwd_dx",
        grid=(t // TOKEN_TILE,),
        out_shape=(jax.ShapeDtypeStruct((t, D_MODEL), F32), jax.ShapeDtypeStruct((1, D_MODEL), F32)),
        in_specs=[tile, tile] + _dproj_specs() + [
            pl.BlockSpec((1, D_MODEL), lambda i: (0, 0)),
            pl.BlockSpec(memory_space=pl.ANY),
        ],
        out_specs=(tile, pl.BlockSpec((1, D_MODEL), lambda i: (0, 0))),
        scratch_shapes=[pltpu.VMEM((D_MODEL, IN_WIDTH), BF16), pltpu.SemaphoreType.DMA((N_DEV,))],
        compiler_params=_params(("arbitrary",)),
    )(x2d, dx2, *dproj, g1, wg)


def _inproj_bwd_dw(x2d, pvg, dy_pool, dattn, g1, pool_w, pool_scale):
    t = x2d.shape[0]
    n_tiles = t // TOKEN_TILE
    halo_per_tile = TOKEN_TILE // HALO
    last_halo = t // HALO - 1

    def body(x_ref, cur_ref, prev_ref, pgn_ref, dy_ref, dyn_ref, dq_ref, dk_ref, dv_ref, dag_ref, g_ref, pw_ref, ps_ref,
             out_ref, dp_ref, dpw_ref, dps_ref, acc_ref):
        i = pl.program_id(0)

        @pl.when(i == 0)
        def _():
            acc_ref[...] = jnp.zeros_like(acc_ref)
            dpw_ref[...] = jnp.zeros_like(dpw_ref)
            dps_ref[...] = jnp.zeros_like(dps_ref)

        xf = x_ref[...]
        r = lax.rsqrt(jnp.mean(xf * xf, axis=-1, keepdims=True) + EPS)
        h = ((xf * r) * g_ref[...]).astype(BF16)
        col = 2 * POOL_WIDTH
        for ref in (dq_ref, dk_ref, dv_ref, dag_ref):
            acc_ref[:, col:col + ATTN_WIDTH] += _tn(h, ref[...])
            col += ATTN_WIDTH
        _pool_bwd_tile(i, n_tiles, cur_ref, prev_ref, pgn_ref, dy_ref, dyn_ref, pw_ref, ps_ref, dp_ref, dpw_ref, dps_ref)
        for c0 in (0, POOL_WIDTH):
            acc_ref[:, c0:c0 + POOL_WIDTH] += _tn(h, dp_ref[:, c0:c0 + POOL_WIDTH])

        @pl.when(i == n_tiles - 1)
        def _():
            for d in range(N_DEV):
                out_ref[d] = acc_ref[:, d * IN_SHARD:(d + 1) * IN_SHARD].astype(BF16)

    tile = lambda width: pl.BlockSpec((TOKEN_TILE, width), lambda i: (i, 0))
    next_halo = lambda col: pl.BlockSpec(
        (HALO, POOL_WIDTH), lambda i: (jnp.minimum((i + 1) * halo_per_tile, last_halo), col))
    return pl.pallas_call(
        body,
        name="inproj_bwd_dw",
        grid=(n_tiles,),
        out_shape=(
            jax.ShapeDtypeStruct((N_DEV, D_MODEL, IN_SHARD), BF16),
            jax.ShapeDtypeStruct((t, 2 * POOL_WIDTH), BF16),
            jax.ShapeDtypeStruct((N_GROUPS, GROUP_DIM, GROUP_DIM), F32),
            jax.ShapeDtypeStruct((1, POOL_WIDTH), F32),
        ),
        in_specs=[
            tile(D_MODEL),
            tile(2 * POOL_WIDTH),
            pl.BlockSpec((HALO, POOL_WIDTH), lambda i: (jnp.maximum(i * halo_per_tile - 1, 0), 0)),
            next_halo(1),
            tile(POOL_WIDTH),
            next_halo(0),
            tile(ATTN_WIDTH), tile(ATTN_WIDTH), tile(ATTN_WIDTH), tile(ATTN_WIDTH),
            pl.BlockSpec((1, D_MODEL), lambda i: (0, 0)),
            pl.BlockSpec((N_GROUPS, GROUP_DIM, GROUP_DIM), lambda i: (0, 0, 0)),
            pl.BlockSpec((1, POOL_WIDTH), lambda i: (0, 0)),
        ],
        out_specs=(
            pl.BlockSpec((N_DEV, D_MODEL, IN_SHARD), lambda i: (0, 0, 0)),
            tile(2 * POOL_WIDTH),
            pl.BlockSpec((N_GROUPS, GROUP_DIM, GROUP_DIM), lambda i: (0, 0, 0)),
            pl.BlockSpec((1, POOL_WIDTH), lambda i: (0, 0)),
        ),
        scratch_shapes=[pltpu.VMEM((D_MODEL, IN_WIDTH), F32)],
        compiler_params=_params(("arbitrary",)),
    )(x2d, pvg, pvg, pvg, dy_pool, dy_pool, *dattn, g1, pool_w, pool_scale)


_HBM = pl.BlockSpec(memory_space=pltpu.HBM)
_SEM = pl.BlockSpec(memory_space=pltpu.SEMAPHORE)
_DATAFLOW = pltpu.SideEffectType.DATAFLOW_SIDE_EFFECTING


_N_EXCHANGED = 3


def _exchange_all(refs, send_sems, recv_sems):
    win_hbm, win_land, pw_hbm, pw_land, vec_hbm, vec_land = refs
    return (_exchange_copies(win_hbm, win_land, send_sems, recv_sems)
            + _exchange_copies(pw_hbm, pw_land, send_sems, recv_sems, first_sem=N_DEV - 1)
            + _exchange_copies(vec_hbm, vec_land, send_sems, recv_sems, first_sem=2 * (N_DEV - 1), same_for_all=True))


def _exchange_start(win_blocks, pw_blocks, vec):
    arrays = []
    for a, land_shape in ((win_blocks, (N_DEV - 1,) + win_blocks.shape[1:]),
                          (pw_blocks, (N_DEV - 1,) + pw_blocks.shape[1:]),
                          (vec, (N_DEV - 1,) + vec.shape)):
        arrays += [pltpu.with_memory_space_constraint(a, pltpu.HBM),
                   pltpu.with_memory_space_constraint(lax.empty(land_shape, a.dtype), pltpu.HBM)]

    def body(*refs):
        ins, (send_sems, recv_sems), token = refs[:2 * _N_EXCHANGED], refs[2 * _N_EXCHANGED:2 * _N_EXCHANGED + 2], refs[-1]
        for cp in _exchange_all(ins, send_sems, recv_sems):
            cp.start()
        token[...] = jnp.zeros_like(token)

    sems = pltpu.SemaphoreType.DMA((_N_EXCHANGED * (N_DEV - 1),))
    return pl.pallas_call(
        body,
        name="exchange_start",
        out_shape=(sems, sems, *[pltpu.HBM(a.shape, a.dtype) for a in arrays], jax.ShapeDtypeStruct((8, LANES), F32)),
        in_specs=tuple([_HBM] * len(arrays)),
        out_specs=(_SEM, _SEM, *[_HBM] * len(arrays), pl.BlockSpec(memory_space=pltpu.VMEM)),
        input_output_aliases={k: 2 + k for k in range(len(arrays))},
        compiler_params=pltpu.CompilerParams(has_side_effects=_DATAFLOW),
    )(*arrays)


def _exchange_wait(send_sems, recv_sems, arrays, after):
    def body(*refs):
        ins = refs[:2 * _N_EXCHANGED]
        send_sems, recv_sems = refs[2 * _N_EXCHANGED:2 * _N_EXCHANGED + 2]
        for cp in _exchange_all(ins, send_sems, recv_sems):
            cp.wait_send()
            cp.wait_recv()

    return pl.pallas_call(
        body,
        name="exchange_wait",
        out_shape=tuple(pltpu.HBM(a.shape, a.dtype) for a in arrays),
        in_specs=(*[_HBM] * len(arrays), _SEM, _SEM, pl.BlockSpec(memory_space=pl.ANY)),
        out_specs=tuple([_HBM] * len(arrays)),
        input_output_aliases={k: k for k in range(len(arrays))},
        compiler_params=pltpu.CompilerParams(has_side_effects=_DATAFLOW),
    )(*arrays, send_sems, recv_sems, after)


def _adamw(w, g, m, v):
    m = ADAM_B1 * m + (1.0 - ADAM_B1) * g
    v = ADAM_B2 * v + (1.0 - ADAM_B2) * (g * g)
    m_hat = m / (1.0 - ADAM_B1 ** ADAM_STEP)
    v_hat = v / (1.0 - ADAM_B2 ** ADAM_STEP)
    delta = -ADAM_LR * (m_hat / (jnp.sqrt(v_hat) + ADAM_EPS) + ADAM_WD * w)
    return delta, m, v


def _small_allreduce(d_g1, vec, vec_land, pw_blocks, pw_land):
    def body(g1_ref, vec_ref, vland_ref, pwb_ref, pland_ref, vec_out, pw_out, vparts_ref, pparts_ref, rows_ref,
             slice_ref, send_sems, recv_sems):
        me = _dev_index(_mesh_pos())

        def from_devices(parts_ref, own, land_ref):
            parts_ref[0] = own
            parts_ref[1:] = land_ref[...]
            total = parts_ref[me]
            for s in range(1, N_DEV):
                total = total + parts_ref[me ^ s]
            return total

        slice_ref[...] = from_devices(pparts_ref, pwb_ref[me], pland_ref)

        def send(r, src, dst, k):
            return pltpu.make_async_remote_copy(
                src_ref=src, dst_ref=dst, send_sem=send_sems.at[2 * (r - 1) + k],
                recv_sem=recv_sems.at[2 * (r - 1) + k], device_id=_peer(r), device_id_type=MESH_ID)

        started = [cp for r in range(1, N_DEV)
                   for cp in (send(r, g1_ref, rows_ref.at[r], 0), send(r, slice_ref, pw_out.at[me], 1))]
        for cp in started:
            cp.start()
        rows_ref[0] = g1_ref[...]
        pw_out[me] = slice_ref[...]
        vec_out[...] = from_devices(vparts_ref, vec_ref[...], vland_ref)
        for cp in started:
            cp.wait_recv()
        for cp in started:
            cp.wait_send()
        g1 = rows_ref[me]
        for s in range(1, N_DEV):
            g1 = g1 + rows_ref[me ^ s]
        vec_out[_ROW_G1:_ROW_G1 + 1, :] = g1

    vm = pl.BlockSpec(memory_space=pltpu.VMEM)
    return pl.pallas_call(
        body,
        name="small_allreduce",
        out_shape=(jax.ShapeDtypeStruct((_VEC_ROWS, D_MODEL), F32),
                   jax.ShapeDtypeStruct((N_DEV, GROUP_DIM // 2, GROUP_DIM), F32)),
        in_specs=[vm] * 5,
        out_specs=(vm, vm),
        scratch_shapes=[
            pltpu.VMEM((N_DEV, _VEC_ROWS, D_MODEL), F32),
            pltpu.VMEM((N_DEV, GROUP_DIM // 2, GROUP_DIM), F32),
            pltpu.VMEM((N_DEV, 1, D_MODEL), F32),
            pltpu.VMEM((GROUP_DIM // 2, GROUP_DIM), F32),
            pltpu.SemaphoreType.DMA((2 * (N_DEV - 1),)),
            pltpu.SemaphoreType.DMA((2 * (N_DEV - 1),)),
        ],
        compiler_params=_params(),
    )(d_g1, vec, vec_land, pw_blocks, pw_land)


def _adamw_all(dwin_g, land_in, dwout_g, land_out, vec_sum, pw_sum, weights, big):
    small_shapes = [(1, D_MODEL), (1, D_MODEL), (1, POOL_WIDTH), (N_HEADS, 2 * LANES), (N_GROUPS, GROUP_DIM, GROUP_DIM)]

    def body(*refs):
        refs = list(refs)
        take = lambda n: [refs.pop(0) for _ in range(n)]
        dwin_hbm, lin_ref, dwout_hbm, lout_ref, vec_ref, pw_ref = take(6)
        small_wmv = [take(3) for _ in range(5)]
        big_wmv = [take(3) for _ in range(2)]
        big_out = [take(4) for _ in range(2)]
        small_out = [take(4) for _ in range(5)]
        own_in, own_out, local_sems = refs

        me = _dev_index(_mesh_pos())
        mine = [pltpu.make_async_copy(dwin_hbm.at[me], own_in, local_sems.at[0]),
                pltpu.make_async_copy(dwout_hbm.at[me], own_out, local_sems.at[1])]
        for cp in mine:
            cp.start()

        def update(g, wmv, outs):
            delta, m_new, v_new = _adamw(wmv[0][...], g, wmv[1][...], wmv[2][...])
            for ref, val in zip(outs, (g, delta, m_new, v_new)):
                ref[...] = val

        update(vec_ref[_ROW_G1:_ROW_G1 + 1, :], small_wmv[0], small_out[0])
        update(vec_ref[_ROW_G2:_ROW_G2 + 1, :], small_wmv[1], small_out[1])
        update(vec_ref[_ROW_PS:_ROW_PS + 1, :POOL_WIDTH], small_wmv[2], small_out[2])
        update(vec_ref[_ROW_RB:_ROW_RB + N_HEADS, :2 * LANES], small_wmv[3], small_out[3])
        update(pw_ref[...], small_wmv[4], small_out[4])
        for cp in mine:
            cp.wait()
        g_in = own_in[...].astype(F32)
        g_out = own_out[...].astype(F32)
        for r in range(N_DEV - 1):
            g_in = g_in + lin_ref[r].astype(F32)
            g_out = g_out + lout_ref[r].astype(F32)
        update(g_in, big_wmv[0], big_out[0])
        update(g_out, big_wmv[1], big_out[1])

    vm = pl.BlockSpec(memory_space=pltpu.VMEM)
    hbm = pl.BlockSpec(memory_space=pl.ANY)
    f32 = lambda shape: jax.ShapeDtypeStruct(shape, F32)
    out_shapes = [f32((D_MODEL, IN_SHARD))] * 4 + [f32((OUT_SHARD, D_MODEL))] * 4
    for shape in small_shapes:
        out_shapes += [f32(shape)] * 4
    args = [dwin_g, land_in, dwout_g, land_out, vec_sum, pw_sum]
    for wmv in weights:
        args += list(wmv)
    for wmv in big:
        args += list(wmv)
    return pl.pallas_call(
        body,
        name="adamw_all",
        out_shape=tuple(out_shapes),
        in_specs=[hbm, vm, hbm, vm] + [vm] * (len(args) - 4),
        out_specs=tuple([vm] * len(out_shapes)),
        scratch_shapes=[
            pltpu.VMEM((D_MODEL, IN_SHARD), BF16),
            pltpu.VMEM((OUT_SHARD, D_MODEL), BF16),
            pltpu.SemaphoreType.DMA((2,)),
        ],
        compiler_params=_params(),
    )(*args)


def kernel(x, norm_gain, w_in, pool_w, pool_scale, rel_bias, w_out, final_norm_gain, loss_target, m_norm_gain, m_w_in, m_pool_w, m_pool_scale, m_rel_bias, m_w_out, m_final_norm_gain, v_norm_gain, v_w_in, v_pool_w, v_pool_scale, v_rel_bias, v_w_out, v_final_norm_gain):
    t = x.shape[1]
    assert x.shape[0] == 1 and t % TOKEN_TILE == 0 and t // Q_BLOCK >= 4
    x2d = x[0]
    tgt2d = loss_target[0]
    g2 = final_norm_gain.reshape(1, D_MODEL)

    rb = rel_bias[0]
    rel_line = jnp.concatenate([
        jnp.broadcast_to(rb[:, :1], (N_HEADS, _REL_FIRST)), rb,
        jnp.broadcast_to(rb[:, N_REL - 1:], (N_HEADS, TOEPLITZ - _REL_FIRST - N_REL)),
    ], axis=1).reshape(N_HEADS, 1, TOEPLITZ)
    wg_in, bias_tile = _gather_weights(w_in[0], rel_line)

    pvg, qkv, ag, y_pool = _norm_inproj(x2d, norm_gain, wg_in, pool_w[0], pool_scale)
    o, y_attn, wg_out = _attn_fwd(qkv, ag, bias_tile, w_out[0])
    dx2, dy_pool, dy_attn, dwout_g, d_g2, loss_sum = _outproj_loss(x2d, tgt2d, y_pool, y_attn, wg_out, g2)
    dq, dk, dv, dag, ds_band, ds_total, land_out = _attn_bwd(qkv, ag, o, dy_attn, bias_tile, dwout_g)
    dwin_g, d_pool, d_pw, d_ps = _inproj_bwd_dw(x2d, pvg, dy_pool, (dq, dk, dv, dag), norm_gain, pool_w[0], pool_scale)
    vec = _pack_small(ds_band, ds_total, d_g2, d_ps, loss_sum)
    dproj = (d_pool, dq, dk, dv, dag)
    pw_blocks = d_pw.reshape(N_DEV, GROUP_DIM // 2, GROUP_DIM)
    send_sems, recv_sems, *exchanged, token = _exchange_start(dwin_g, pw_blocks, vec)
    grad_x, d_g1 = _inproj_bwd_dx(x2d, dx2, dproj, norm_gain + token[:1, :1], wg_in)
    dwin_g, land_in, pw_blocks, pw_land, vec, vec_land = _exchange_wait(send_sems, recv_sems, exchanged, d_g1)

    pad_rb = lambda a: jnp.pad(a[0], ((0, 0), (0, 2 * LANES - N_REL)))
    row = lambda a: a.reshape(1, D_MODEL)
    weights = [
        (norm_gain, m_norm_gain, v_norm_gain),
        (row(final_norm_gain), row(m_final_norm_gain), row(v_final_norm_gain)),
        (pool_scale, m_pool_scale, v_pool_scale),
        (pad_rb(rel_bias), pad_rb(m_rel_bias), pad_rb(v_rel_bias)),
        (pool_w[0], m_pool_w[0], v_pool_w[0]),
    ]
    big = [(w_in[0], m_w_in[0], v_w_in[0]), (w_out[0], m_w_out[0], v_w_out[0])]
    vec_sum, pw_sum = _small_allreduce(d_g1, vec, vec_land, pw_blocks, pw_land)
    res = _adamw_all(dwin_g, land_in, dwout_g, land_out, vec_sum, pw_sum.reshape(N_GROUPS, GROUP_DIM, GROUP_DIM),
                     weights, big)
    loss = 0.5 * vec_sum[_ROW_LOSS, 0]

    def leaves(k):
        g1_, g2_, ps_, rb_, pw_ = (res[8 + 4 * leaf + k] for leaf in range(5))
        return [g1_, res[k][None], pw_[None], ps_, rb_[None, :, :N_REL], res[4 + k][None], g2_.reshape(D_MODEL)]

    return (loss, grad_x[None], *leaves(0), *leaves(1), *leaves(2), *leaves(3))
```

```python
import math

import jax
import jax.numpy as jnp
from jax import lax
from jax.experimental import pallas as pl
from jax.experimental.pallas import tpu as pltpu

F32 = jnp.float32
BF16 = jnp.bfloat16
MESH_ID = pl.DeviceIdType.MESH

D_MODEL = 1024
POOL_WIDTH = 512
ATTN_WIDTH = 512
POOL_WINDOWS = (2, 4, 8, 16)
N_GROUPS = 4
GROUP_DIM = 128
HEAD_DIM = 64
N_HEADS = 8
CHUNK = 64
LEFT_CHUNKS = 8
MAX_REL = 64
N_REL = 2 * MAX_REL + 1
IN_WIDTH = 2 * POOL_WIDTH + 4 * ATTN_WIDTH
EPS = 1e-6
MASK_VALUE = -1e30
ATTN_SCALE = 1.0 / math.sqrt(HEAD_DIM)
ADAM_LR = 0.001
ADAM_B1 = 0.9
ADAM_B2 = 0.999
ADAM_EPS = 1e-08
ADAM_WD = 0.01
ADAM_STEP = 10

N_DEV = 8
IN_SHARD = IN_WIDTH // N_DEV
OUT_SHARD = D_MODEL // N_DEV

LANES = 128
TOKEN_TILE = 512
HALO = 16
Q_BLOCK = 256
KV_BLOCKS = 3
KV_WINDOW = KV_BLOCKS * Q_BLOCK
PAIR = 2 * HEAD_DIM
N_PAIRS = N_HEADS // 2
TOEPLITZ = 1024
VMEM_LIMIT = 56 * 1024 * 1024


def _params(sem=None, vmem=VMEM_LIMIT):
    return pltpu.CompilerParams(dimension_semantics=sem, vmem_limit_bytes=vmem)


def _sigmoid(x):
    return 1.0 / (1.0 + jnp.exp(-x))


def _nt(a, b):
    return lax.dot_general(a, b, (((1,), (1,)), ((), ())), preferred_element_type=F32)


def _tn(a, b):
    return lax.dot_general(a, b, (((0,), (0,)), ((), ())), preferred_element_type=F32)


def _nn(a, b):
    return jnp.dot(a, b, preferred_element_type=F32)


def _mesh_pos():
    return lax.axis_index("x"), lax.axis_index("y"), lax.axis_index("c")


def _dev_index(p):
    return 4 * p[0] + 2 * p[1] + p[2]


def _peer(r):
    x, y, c = _mesh_pos()
    return (x ^ ((r >> 2) & 1), y ^ ((r >> 1) & 1), c ^ (r & 1))


def _exchange_copies(src_hbm, land_hbm, send_sems, recv_sems, first_sem=0, same_for_all=False):
    return [
        pltpu.make_async_remote_copy(
            src_ref=src_hbm if same_for_all else src_hbm.at[_dev_index(_peer(r))], dst_ref=land_hbm.at[r - 1],
            send_sem=send_sems.at[first_sem + r - 1], recv_sem=recv_sems.at[first_sem + r - 1],
            device_id=_peer(r), device_id_type=MESH_ID)
        for r in range(1, N_DEV)
    ]


def _gather_weights(w_in_shard, rel_line):
    def body(win_ref, line_ref, gin_ref, bias_ref, sin_ref, send_sems, recv_sems):
        x, y, c = _mesh_pos()
        me, sibling = (x, y, c), (x, y, 1 - c)
        chips = [(1 - x, y), (x, 1 - y), (1 - x, 1 - y)]

        sin_ref[...] = win_ref[...].astype(BF16)
        gin_ref[_dev_index(me)] = sin_ref[...]

        def copy(k, block, to, from_shard=False):
            return pltpu.make_async_remote_copy(
                src_ref=sin_ref if from_shard else gin_ref.at[_dev_index(block)],
                dst_ref=gin_ref.at[_dev_index(block)],
                send_sem=send_sems.at[k],
                recv_sem=recv_sems.at[k],
                device_id=to,
                device_id_type=MESH_ID,
            )

        first = [copy(0, me, sibling, True)]
        first += [copy(1 + j, me, (*chip, c), True) for j, chip in enumerate(chips)]
        for cp in first:
            cp.start()
        passed = [copy(4 + j, (*chip, c), sibling) for j, chip in enumerate(chips)]

        def bias_heads(lo, hi):
            for h in range(lo, hi):
                bias_ref[h] = _toeplitz_bias(line_ref[h])

        bias_heads(0, N_HEADS - 3)
        for j, chip in enumerate(chips):
            copy(1 + j, (*chip, c), me).wait_recv()
            passed[j].start()
            bias_heads(N_HEADS - 3 + j, N_HEADS - 2 + j)
        copy(0, sibling, me).wait_recv()
        for j, chip in enumerate(chips):
            copy(4 + j, (*chip, 1 - c), me).wait_recv()
        for cp in first + passed:
            cp.wait_send()

    vm = pl.BlockSpec(memory_space=pltpu.VMEM)
    return pl.pallas_call(
        body,
        name="gather_weights",
        out_shape=(
            jax.ShapeDtypeStruct((N_DEV, D_MODEL, IN_SHARD), BF16),
            jax.ShapeDtypeStruct((N_HEADS, Q_BLOCK, KV_WINDOW), F32),
        ),
        in_specs=[vm, vm],
        out_specs=(vm, vm),
        scratch_shapes=[
            pltpu.VMEM((D_MODEL, IN_SHARD), BF16),
            pltpu.SemaphoreType.DMA((7,)),
            pltpu.SemaphoreType.DMA((7,)),
        ],
        compiler_params=_params(),
    )(w_in_shard, rel_line)


def _load_w_in(wg_hbm, wfull_ref, sem):
    copies = [
        pltpu.make_async_copy(wg_hbm.at[d], wfull_ref.at[:, d * IN_SHARD:(d + 1) * IN_SHARD], sem.at[d])
        for d in range(N_DEV)
    ]
    for cp in copies:
        cp.start()
    for cp in copies:
        cp.wait()


def _norm_inproj(x2d, g1, wg, pool_w, pool_scale):
    t = x2d.shape[0]

    def body(x_ref, g_ref, wg_hbm, pw_ref, ps_ref, pvg_ref, qkv_ref, ag_ref, yp_ref, wfull_ref, halo_ref, sem):
        i = pl.program_id(0)

        @pl.when(i == 0)
        def _():
            _load_w_in(wg_hbm, wfull_ref, sem)
            halo_ref[...] = jnp.zeros_like(halo_ref)

        xf = x_ref[...]
        r = lax.rsqrt(jnp.mean(xf * xf, axis=-1, keepdims=True) + EPS)
        h = ((xf * r) * g_ref[...]).astype(BF16)
        chunk = lambda ci: _nn(h, wfull_ref[:, ci * POOL_WIDTH:(ci + 1) * POOL_WIDTH])
        pv, pg = chunk(0), chunk(1)
        pvg_ref[:, :POOL_WIDTH] = pv
        pvg_ref[:, POOL_WIDTH:] = pg
        halo = halo_ref[...]
        halo_ref[...] = pv[TOKEN_TILE - HALO:]
        for gi in range(N_GROUPS):
            sl = slice(gi * GROUP_DIM, (gi + 1) * GROUP_DIM)
            d = _pool_diffs(pv, halo, i * TOKEN_TILE, gi)
            z = _nn(d.astype(BF16), pw_ref[gi].astype(BF16))
            g = pg[:, sl]
            yp_ref[:, sl] = ((z * ps_ref[:, sl]) * (g * _sigmoid(g))).astype(BF16)
        for ci in range(2, 5):
            qkv_ref[:, (ci - 2) * POOL_WIDTH:(ci - 1) * POOL_WIDTH] = chunk(ci).astype(BF16)
        ag_ref[...] = chunk(5)

    tile = lambda width: pl.BlockSpec((TOKEN_TILE, width), lambda i: (i, 0))
    return pl.pallas_call(
        body,
        name="norm_inproj",
        grid=(t // TOKEN_TILE,),
        out_shape=(
            jax.ShapeDtypeStruct((t, 2 * POOL_WIDTH), F32),
            jax.ShapeDtypeStruct((t, 3 * ATTN_WIDTH), BF16),
            jax.ShapeDtypeStruct((t, ATTN_WIDTH), F32),
            jax.ShapeDtypeStruct((t, POOL_WIDTH), BF16),
        ),
        in_specs=[
            tile(D_MODEL),
            pl.BlockSpec((1, D_MODEL), lambda i: (0, 0)),
            pl.BlockSpec(memory_space=pl.ANY),
            pl.BlockSpec((N_GROUPS, GROUP_DIM, GROUP_DIM), lambda i: (0, 0, 0)),
            pl.BlockSpec((1, POOL_WIDTH), lambda i: (0, 0)),
        ],
        out_specs=(tile(2 * POOL_WIDTH), tile(3 * ATTN_WIDTH), tile(ATTN_WIDTH), tile(POOL_WIDTH)),
        scratch_shapes=[
            pltpu.VMEM((D_MODEL, IN_WIDTH), BF16),
            pltpu.VMEM((HALO, POOL_WIDTH), F32),
            pltpu.SemaphoreType.DMA((N_DEV,)),
        ],
        compiler_params=_params(("arbitrary",)),
    )(x2d, g1, wg, pool_w, pool_scale)


def _inv_count(first_row, rows, window):
    tpos = first_row + lax.broadcasted_iota(jnp.int32, (rows, 1), 0)
    return 1.0 / jnp.minimum(tpos + 1, window).astype(F32)


def _causal_window_sum(ext, window):
    s, k = ext, 1
    while k < window:
        s = s + pltpu.roll(s, k, 0)
        k *= 2
    return s


def _pool_diffs(pv, halo, first_row, gi):
    w = POOL_WINDOWS[gi]
    sl = slice(gi * GROUP_DIM, (gi + 1) * GROUP_DIM)
    ext = jnp.concatenate([halo[:, sl], pv[:, sl]], axis=0)
    s = _causal_window_sum(ext, w)[HALO:]
    return s * _inv_count(first_row, pv.shape[0], w) - pv[:, sl]


def _pool_bwd_tile(i, n_tiles, cur_ref, prev_ref, pgn_ref, dy_ref, dyn_ref, pw_ref, ps_ref, dp_ref, dpw_ref, dps_ref):
    pv = cur_ref[:, :POOL_WIDTH]
    pg = cur_ref[:, POOL_WIDTH:]
    prev = jnp.where(i > 0, prev_ref[...], 0.0)
    has_next = i < n_tiles - 1
    rows = TOKEN_TILE + HALO
    for gi in range(N_GROUPS):
        w = POOL_WINDOWS[gi]
        sl = slice(gi * GROUP_DIM, (gi + 1) * GROUP_DIM)
        pw = pw_ref[gi].astype(BF16)
        ps = ps_ref[:, sl]
        d = _pool_diffs(pv, prev, i * TOKEN_TILE, gi).astype(BF16)
        z = _nn(d, pw)
        g_ext = jnp.concatenate([pg[:, sl], pgn_ref[:, sl]], axis=0)
        dy_ext = jnp.concatenate([dy_ref[:, sl], dyn_ref[:, sl]], axis=0).astype(F32)
        sig = _sigmoid(g_ext)
        gate = g_ext * sig
        dz_ext = ((dy_ext * gate) * ps).astype(BF16)
        dd_ext = _nt(dz_ext, pw)
        e = dd_ext * _inv_count(i * TOKEN_TILE, rows, w)
        row = lax.broadcasted_iota(jnp.int32, (rows, 1), 0)
        e = jnp.where(jnp.logical_or(row < TOKEN_TILE, has_next), e, 0.0)
        s, k = e, 1
        while k < w:
            s = s + pltpu.roll(s, rows - k, 0)
            k *= 2
        dp_ref[:, sl] = (s[:TOKEN_TILE] - dd_ext[:TOKEN_TILE]).astype(BF16)
        dy = dy_ext[:TOKEN_TILE]
        g = g_ext[:TOKEN_TILE]
        sg = sig[:TOKEN_TILE]
        dgate = sg * (1.0 + g * (1.0 - sg))
        dp_ref[:, POOL_WIDTH + gi * GROUP_DIM:POOL_WIDTH + (gi + 1) * GROUP_DIM] = (
            (dy * (z * ps)) * dgate).astype(BF16)
        dps_ref[:, sl] += jnp.sum((dy * gate[:TOKEN_TILE]) * z, axis=0, keepdims=True)
        dpw_ref[gi] += _tn(d, dz_ext[:TOKEN_TILE])


_REL_FIRST = KV_WINDOW - 1 - MAX_REL


def _skew_rows(a, right):
    rows, lanes = a.shape
    row = lax.broadcasted_iota(jnp.int32, a.shape, 0)
    for b in range(rows.bit_length() - 1):
        shift = (1 << b) if right else lanes - (1 << b)
        a = jnp.where((row >> b) & 1 == 1, pltpu.roll(a, shift, 1), a)
    return a


def _toeplitz_bias(line):
    a = jnp.broadcast_to(line, (Q_BLOCK, TOEPLITZ))
    a = _skew_rows(a, True)
    a = pltpu.roll(a, TOEPLITZ - (Q_BLOCK - 1), 1)
    a = a[:, :KV_WINDOW]
    qc = lax.broadcasted_iota(jnp.int32, a.shape, 0) // CHUNK
    kc = lax.broadcasted_iota(jnp.int32, a.shape, 1) // CHUNK
    visible = jnp.logical_and(kc >= qc, kc <= qc + LEFT_CHUNKS)
    return jnp.where(visible, a, MASK_VALUE)


_BAND_WIDTH = 2 * LANES
_BAND_START = (384, 512, 512, 512)
_BAND_REL = tuple(a - (KV_BLOCKS - 1) * Q_BLOCK - qc * CHUNK for qc, a in enumerate(_BAND_START))


_ROW_G1, _ROW_G2, _ROW_PS, _ROW_LOSS, _ROW_RB = 0, 1, 2, 3, 8
_VEC_ROWS = 16


def _pack_small(band, total, d_g2, d_ps, loss_row):
    def body(band_ref, total_ref, g2_ref, ps_ref, loss_ref, vec_ref):
        vec_ref[...] = jnp.zeros_like(vec_ref)
        vec_ref[_ROW_G2:_ROW_G2 + 1, :] = g2_ref[...]
        vec_ref[_ROW_PS:_ROW_PS + 1, :POOL_WIDTH] = ps_ref[...]
        vec_ref[_ROW_LOSS:_ROW_LOSS + 1, :LANES] = loss_ref[0:1, :]
        for h in range(N_HEADS):
            a = jnp.zeros((CHUNK, 2 * _BAND_WIDTH), F32)
            for qc in range(Q_BLOCK // CHUNK):
                z = jnp.concatenate([band_ref[h, qc], jnp.zeros((CHUNK, _BAND_WIDTH), F32)], axis=1)
                left = -MAX_REL - _BAND_REL[qc]
                a = a + (pltpu.roll(z, 2 * _BAND_WIDTH - left, 1) if left else z)
            a = _skew_rows(a, False)
            near = jnp.sum(a, axis=0, keepdims=True)[:, :2 * LANES]
            r = lax.broadcasted_iota(jnp.int32, near.shape, 1)
            near = jnp.where(jnp.logical_and(r >= 1, r < 2 * MAX_REL), near, 0.0)
            everything = jnp.sum(jnp.sum(total_ref[h], axis=0, keepdims=True), axis=1, keepdims=True)
            far = everything - jnp.sum(near, axis=1, keepdims=True)
            vec_ref[_ROW_RB + h:_ROW_RB + h + 1, :2 * LANES] = jnp.where(r == 0, far, near)

    vm = pl.BlockSpec(memory_space=pltpu.VMEM)
    return pl.pallas_call(
        body,
        name="pack_small",
        out_shape=jax.ShapeDtypeStruct((_VEC_ROWS, D_MODEL), F32),
        in_specs=[vm] * 5,
        out_specs=vm,
        compiler_params=_params(),
    )(band, total, d_g2, d_ps, loss_row)


def _head_lanes(hh):
    lane = lax.broadcasted_iota(jnp.int32, (1, PAIR), 1)
    return (lane < HEAD_DIM) if hh == 0 else (lane >= HEAD_DIM)


def _score_windows(nwin):
    if nwin < KV_BLOCKS:
        return [(slice(0, Q_BLOCK), slice(0, nwin * Q_BLOCK), None)]
    pieces = []
    for qc in range(Q_BLOCK // CHUNK):
        rows = slice(qc * CHUNK, (qc + 1) * CHUNK)
        if qc < 2:
            pieces.append((rows, slice(0, KV_WINDOW - LANES), slice(KV_WINDOW - LANES, KV_WINDOW)))
        else:
            pieces.append((rows, slice(LANES, KV_WINDOW), slice(0, LANES)))
    return pieces


def _block_rows(first_block, n_blocks):
    if isinstance(first_block, int):
        return pl.ds(first_block * Q_BLOCK, n_blocks * Q_BLOCK)
    return pl.ds(pl.multiple_of(first_block * Q_BLOCK, Q_BLOCK), n_blocks * Q_BLOCK)


def _attn_fwd(qkv, ag, bias_tile, w_out_shard):
    t = qkv.shape[0]
    nb = t // Q_BLOCK

    def body(q_ref, k_ref, v_ref, ag_ref, bias_ref, wout_ref, o_ref, y_ref, gout_hbm,
             s_scr, p_scr, linv_scr, sout_ref, send_sems, recv_sems, local_sem):
        pair = pl.program_id(0)
        me = _dev_index(_mesh_pos())
        mine = pltpu.make_async_copy(sout_ref, gout_hbm.at[me], local_sem)

        def shard_copy(r, block):
            return pltpu.make_async_remote_copy(
                src_ref=sout_ref, dst_ref=gout_hbm.at[block], send_sem=send_sems.at[r - 1],
                recv_sem=recv_sems.at[r - 1], device_id=_peer(r), device_id_type=MESH_ID)

        @pl.when(pair == 0)
        def _():
            sout_ref[...] = wout_ref[...].astype(BF16)
            mine.start()
            for r in range(1, N_DEV):
                shard_copy(r, me).start()

        def scores(i, slot, nwin):
            q = q_ref[_block_rows(i, 1), :]
            kw = k_ref[_block_rows(i + 1 - nwin, nwin), :]
            for hh in range(2):
                q_h = jnp.where(_head_lanes(hh), q, jnp.zeros_like(q)) * ATTN_SCALE
                s_scr[slot, hh, :, :nwin * Q_BLOCK] = _nt(q_h, kw)

        def softmax(slot, nwin):
            off = (KV_BLOCKS - nwin) * Q_BLOCK
            for hh in range(2):
                for rows, cols, rest in _score_windows(nwin):
                    s = s_scr[slot, hh, rows, cols] + bias_ref[hh, rows, off + cols.start:off + cols.stop]
                    e = jnp.exp(s - jnp.max(s, axis=-1, keepdims=True))
                    linv_scr[slot, hh, rows, :] = 1.0 / jnp.sum(e, axis=-1, keepdims=True)
                    p_scr[slot, hh, rows, cols] = e.astype(BF16)
                    if rest is not None:
                        p_scr[slot, hh, rows, rest] = jnp.zeros((CHUNK, LANES), BF16)

        def output(i, slot, nwin):
            vw = v_ref[_block_rows(i + 1 - nwin, nwin), :]
            outs = [_nn(p_scr[slot, hh, :, :nwin * Q_BLOCK], vw) * linv_scr[slot, hh] for hh in range(2)]
            o = jnp.where(_head_lanes(0), outs[0], outs[1])
            g = ag_ref[_block_rows(i, 1), :]
            o_ref[_block_rows(i, 1), :] = o.astype(BF16)
            y_ref[_block_rows(i, 1), :] = (o * (g * _sigmoid(g))).astype(BF16)

        scores(0, 0, 1)
        scores(1, 1, 2)
        softmax(0, 1)
        scores(2, 0, 3)
        softmax(1, 2)
        output(0, 0, 1)
        scores(3, 1, 3)
        softmax(0, 3)
        output(1, 1, 2)

        def two_steps(k, carry):
            i = 3 + 2 * k
            scores(i + 1, 0, KV_BLOCKS)
            softmax(1, KV_BLOCKS)
            output(i - 1, 0, KV_BLOCKS)
            scores(i + 2, 1, KV_BLOCKS)
            softmax(0, KV_BLOCKS)
            output(i, 1, KV_BLOCKS)
            return carry

        lax.fori_loop(0, (nb - 4) // 2, two_steps, 0)
        last = (nb - 1) % 2
        softmax(last, KV_BLOCKS)
        output(nb - 2, 1 - last, KV_BLOCKS)
        output(nb - 1, last, KV_BLOCKS)

        @pl.when(pair == N_PAIRS - 1)
        def _():
            for r in range(1, N_DEV):
                shard_copy(r, _dev_index(_peer(r))).wait_recv()
            for r in range(1, N_DEV):
                shard_copy(r, me).wait_send()
            mine.wait()

    col = lambda c0: pl.BlockSpec((t, PAIR), lambda j: (0, c0 + j))
    return pl.pallas_call(
        body,
        name="attn_fwd",
        grid=(N_PAIRS,),
        out_shape=(
            jax.ShapeDtypeStruct((t, ATTN_WIDTH), BF16),
            jax.ShapeDtypeStruct((t, ATTN_WIDTH), BF16),
            jax.ShapeDtypeStruct((N_DEV, OUT_SHARD, D_MODEL), BF16),
        ),
        in_specs=[col(0), col(N_PAIRS), col(2 * N_PAIRS), col(0),
                  pl.BlockSpec((2, Q_BLOCK, KV_WINDOW), lambda j: (j, 0, 0)),
                  pl.BlockSpec((OUT_SHARD, D_MODEL), lambda j: (0, 0))],
        out_specs=(col(0), col(0), pl.BlockSpec(memory_space=pl.ANY)),
        scratch_shapes=[
            pltpu.VMEM((2, 2, Q_BLOCK, KV_WINDOW), F32),
            pltpu.VMEM((2, 2, Q_BLOCK, KV_WINDOW), BF16),
            pltpu.VMEM((2, 2, Q_BLOCK, 1), F32),
            pltpu.VMEM((OUT_SHARD, D_MODEL), BF16),
            pltpu.SemaphoreType.DMA((N_DEV - 1,)),
            pltpu.SemaphoreType.DMA((N_DEV - 1,)),
            pltpu.SemaphoreType.DMA,
        ],
        compiler_params=_params(("arbitrary",)),
    )(qkv, qkv, qkv, ag, bias_tile, w_out_shard)


def _attn_bwd(qkv, do, bias_tile, dwout_g):
    t = qkv.shape[0]
    nb = t // Q_BLOCK

    def body(q_ref, k_ref, v_ref, do_ref, bias_ref, dwout_hbm,
             dq_ref, dk_ref, dv_ref, band_ref, total_ref, land_hbm,
             s_scr, dp_scr, p_scr, dsb_scr, dq_scr, dk_acc, dv_acc, send_sems, recv_sems):
        exchange = _exchange_copies(dwout_hbm, land_hbm, send_sems, recv_sems)

        @pl.when(pl.program_id(0) == 0)
        def _():
            for cp in exchange:
                cp.start()

        band_ref[...] = jnp.zeros_like(band_ref)
        total_ref[...] = jnp.zeros_like(total_ref)

        def nwin_of(i):
            return min(i + 1, KV_BLOCKS) if isinstance(i, int) else KV_BLOCKS

        def operands(i, hh):
            lanes = _head_lanes(hh)
            q = q_ref[_block_rows(i, 1), :]
            do = do_ref[_block_rows(i, 1), :]
            return (jnp.where(lanes, q, jnp.zeros_like(q)) * ATTN_SCALE, jnp.where(lanes, do, jnp.zeros_like(do)))

        def products(i, hh):
            nwin = nwin_of(i)
            win = _block_rows(i + 1 - nwin, nwin)
            q_h, do_h = operands(i, hh)
            s_scr[hh, :, :nwin * Q_BLOCK] = _nt(q_h, k_ref[win, :])
            dp_scr[hh, :, :nwin * Q_BLOCK] = _nt(do_h, v_ref[win, :])

        def grads(i, hh):
            nwin = nwin_of(i)
            off = (KV_BLOCKS - nwin) * Q_BLOCK
            for rows, cols, rest in _score_windows(nwin):
                bias_cols = slice(off + cols.start, off + cols.stop)
                s = s_scr[hh, rows, cols] + bias_ref[hh, rows, bias_cols]
                e = jnp.exp(s - jnp.max(s, axis=-1, keepdims=True))
                p = e * (1.0 / jnp.sum(e, axis=-1, keepdims=True))
                dp = dp_scr[hh, rows, cols]
                ds = p * (dp - jnp.sum(p * dp, axis=-1, keepdims=True))
                total_ref[hh, rows, :] += sum(ds[:, c0:c0 + LANES] for c0 in range(0, ds.shape[1], LANES))
                for qc in range(rows.start // CHUNK, rows.stop // CHUNK):
                    lo = max(_BAND_START[qc], bias_cols.start)
                    hi = min(_BAND_START[qc] + _BAND_WIDTH, bias_cols.stop)
                    if lo < hi:
                        band_ref[hh, qc, :, lo - _BAND_START[qc]:hi - _BAND_START[qc]] += ds[
                            qc * CHUNK - rows.start:(qc + 1) * CHUNK - rows.start,
                            lo - bias_cols.start:hi - bias_cols.start]
                p_scr[hh, rows, cols] = p.astype(BF16)
                dsb_scr[hh, rows, cols] = ds.astype(BF16)
                if rest is not None:
                    p_scr[hh, rows, rest] = jnp.zeros((CHUNK, LANES), BF16)
                    dsb_scr[hh, rows, rest] = jnp.zeros((CHUNK, LANES), BF16)

        def ring(block):
            return block % KV_BLOCKS if isinstance(block, int) else lax.rem(block, KV_BLOCKS)

        def accumulate(i, hh):
            nwin = nwin_of(i)
            w = nwin * Q_BLOCK
            win = _block_rows(i + 1 - nwin, nwin)
            q_h, do_h = operands(i, hh)
            ds_b = dsb_scr[hh, :, :w]
            dq_h = _nn(ds_b, k_ref[win, :]) * ATTN_SCALE
            dkw = _tn(ds_b, q_h)
            dvw = _tn(p_scr[hh, :, :w], do_h)
            for b in range(nwin):
                slot = ring(i + 1 - nwin + b)
                part = slice(b * Q_BLOCK, (b + 1) * Q_BLOCK)
                if hh == 0 and b == nwin - 1:
                    dk_acc[slot] = dkw[part]
                    dv_acc[slot] = dvw[part]
                else:
                    dk_acc[slot] += dkw[part]
                    dv_acc[slot] += dvw[part]
            if hh == 0:
                dq_scr[...] = dq_h
            else:
                dq_ref[_block_rows(i, 1), :] = jnp.where(_head_lanes(0), dq_scr[...], dq_h).astype(BF16)
                if not (isinstance(i, int) and i < KV_BLOCKS - 1):
                    flush(i - (KV_BLOCKS - 1))

        def flush(block):
            dk_ref[_block_rows(block, 1), :] = dk_acc[ring(block)].astype(BF16)
            dv_ref[_block_rows(block, 1), :] = dv_acc[ring(block)].astype(BF16)

        def tile(n):
            return n // 2, n % 2

        def step(n):
            if n + 1 < 2 * nb:
                products(*tile(n + 1))
            grads(*tile(n))
            if n >= 1:
                accumulate(*tile(n - 1))

        products(0, 0)
        for n in range(2 * KV_BLOCKS):
            step(n)

        def two_steps(i, carry):
            products(i, 1)
            grads(i, 0)
            accumulate(i - 1, 1)
            products(i + 1, 0)
            grads(i, 1)
            accumulate(i, 0)
            return carry

        lax.fori_loop(KV_BLOCKS, nb - 1, two_steps, 0)
        step(2 * nb - 2)
        step(2 * nb - 1)
        accumulate(nb - 1, 1)
        flush(nb - 2)
        flush(nb - 1)

        @pl.when(pl.program_id(0) == N_PAIRS - 1)
        def _():
            for cp in exchange:
                cp.wait_recv()
            for cp in exchange:
                cp.wait_send()

    col = lambda c0: pl.BlockSpec((t, PAIR), lambda j: (0, c0 + j))
    tile_spec = pl.BlockSpec((2, Q_BLOCK, KV_WINDOW), lambda j: (j, 0, 0))
    out = jax.ShapeDtypeStruct((t, ATTN_WIDTH), BF16)
    return pl.pallas_call(
        body,
        name="attn_bwd",
        grid=(N_PAIRS,),
        out_shape=(out, out, out,
                   jax.ShapeDtypeStruct((N_HEADS, Q_BLOCK // CHUNK, CHUNK, _BAND_WIDTH), F32),
                   jax.ShapeDtypeStruct((N_HEADS, Q_BLOCK, LANES), F32),
                   jax.ShapeDtypeStruct((N_DEV - 1, OUT_SHARD, D_MODEL), BF16)),
        in_specs=[col(0), col(N_PAIRS), col(2 * N_PAIRS), col(0), tile_spec, pl.BlockSpec(memory_space=pl.ANY)],
        out_specs=(col(0), col(0), col(0),
                   pl.BlockSpec((2, Q_BLOCK // CHUNK, CHUNK, _BAND_WIDTH), lambda j: (j, 0, 0, 0)),
                   pl.BlockSpec((2, Q_BLOCK, LANES), lambda j: (j, 0, 0)),
                   pl.BlockSpec(memory_space=pl.ANY)),
        scratch_shapes=[
            pltpu.VMEM((2, Q_BLOCK, KV_WINDOW), F32),
            pltpu.VMEM((2, Q_BLOCK, KV_WINDOW), F32),
            pltpu.VMEM((2, Q_BLOCK, KV_WINDOW), BF16),
            pltpu.VMEM((2, Q_BLOCK, KV_WINDOW), BF16),
            pltpu.VMEM((Q_BLOCK, PAIR), F32),
            pltpu.VMEM((KV_BLOCKS, Q_BLOCK, PAIR), F32),
            pltpu.VMEM((KV_BLOCKS, Q_BLOCK, PAIR), F32),
            pltpu.SemaphoreType.DMA((N_DEV - 1,)),
            pltpu.SemaphoreType.DMA((N_DEV - 1,)),
        ],
        compiler_params=_params(("arbitrary",)),
    )(qkv, qkv, qkv, do, bias_tile, dwout_g)


def _outproj_loss(x2d, tgt2d, y_pool, y_attn, wout_g, g2, ag, o):
    t = x2d.shape[0]
    n_tiles = t // TOKEN_TILE

    def body(x_ref, tgt_ref, yp_ref, ya_ref, w_ref, g_ref, ag_ref, o_ref,
             dx2_ref, dyp_ref, do_ref, dag_ref, dw_ref, dg_ref, loss_ref, acc_ref):
        i = pl.program_id(0)

        @pl.when(i == 0)
        def _():
            acc_ref[...] = jnp.zeros_like(acc_ref)
            dg_ref[...] = jnp.zeros_like(dg_ref)
            loss_ref[...] = jnp.zeros_like(loss_ref)

        w = w_ref[...].reshape(D_MODEL, D_MODEL)
        y = jnp.concatenate([yp_ref[...], ya_ref[...]], axis=1)
        x2 = x_ref[...] + _nn(y, w)
        r = lax.rsqrt(jnp.mean(x2 * x2, axis=-1, keepdims=True) + EPS)
        xh = x2 * r
        g = g_ref[...]
        diff = xh * g - tgt_ref[...]
        tok = jnp.sum(diff * diff, axis=-1, keepdims=True) * (1.0 / D_MODEL)
        loss_ref[...] += jnp.sum(tok, axis=0, keepdims=True)
        dout = diff * (1.0 / D_MODEL)
        dg_ref[...] += jnp.sum(dout * xh, axis=0, keepdims=True)
        u = dout * g
        dx2 = r * (u - xh * jnp.mean(u * xh, axis=-1, keepdims=True))
        dx2_ref[...] = dx2
        dx2_b = dx2.astype(BF16)
        dy = _nt(dx2_b, w)
        dyp_ref[...] = dy[:, :POOL_WIDTH].astype(BF16)
        dya = dy[:, POOL_WIDTH:]
        ga = ag_ref[...]
        sig = _sigmoid(ga)
        do_ref[...] = (dya * (ga * sig)).astype(BF16)
        dag_ref[...] = ((dya * o_ref[...].astype(F32)) * (sig * (1.0 + ga * (1.0 - sig)))).astype(BF16)
        acc_ref[...] += _tn(y, dx2_b)

        @pl.when(i == n_tiles - 1)
        def _():
            dw_ref[...] = acc_ref[...].reshape(N_DEV, OUT_SHARD, D_MODEL).astype(BF16)

    tile = lambda width: pl.BlockSpec((TOKEN_TILE, width), lambda i: (i, 0))
    return pl.pallas_call(
        body,
        name="outproj_loss",
        grid=(n_tiles,),
        out_shape=(
            jax.ShapeDtypeStruct((t, D_MODEL), F32),
            jax.ShapeDtypeStruct((t, POOL_WIDTH), BF16),
            jax.ShapeDtypeStruct((t, ATTN_WIDTH), BF16),
            jax.ShapeDtypeStruct((t, ATTN_WIDTH), BF16),
            jax.ShapeDtypeStruct((N_DEV, OUT_SHARD, D_MODEL), BF16),
            jax.ShapeDtypeStruct((1, D_MODEL), F32),
            jax.ShapeDtypeStruct((8, LANES), F32),
        ),
        in_specs=[
            tile(D_MODEL), tile(D_MODEL), tile(POOL_WIDTH), tile(ATTN_WIDTH),
            pl.BlockSpec((N_DEV, OUT_SHARD, D_MODEL), lambda i: (0, 0, 0)),
            pl.BlockSpec((1, D_MODEL), lambda i: (0, 0)),
            tile(ATTN_WIDTH), tile(ATTN_WIDTH),
        ],
        out_specs=(
            tile(D_MODEL), tile(POOL_WIDTH), tile(ATTN_WIDTH), tile(ATTN_WIDTH),
            pl.BlockSpec((N_DEV, OUT_SHARD, D_MODEL), lambda i: (0, 0, 0)),
            pl.BlockSpec((1, D_MODEL), lambda i: (0, 0)),
            pl.BlockSpec((8, LANES), lambda i: (0, 0)),
        ),
        scratch_shapes=[pltpu.VMEM((D_MODEL, D_MODEL), F32)],
        compiler_params=_params(("arbitrary",)),
    )(x2d, tgt2d, y_pool, y_attn, wout_g, g2, ag, o)


def _dproj_specs():
    tile = lambda width: pl.BlockSpec((TOKEN_TILE, width), lambda i: (i, 0))
    return [tile(2 * POOL_WIDTH)] + [tile(ATTN_WIDTH)] * 4


def _inproj_bwd_dx(x2d, dx2, dproj, g1, wg):
    t = x2d.shape[0]

    def body(x_ref, dx2_ref, dp_ref, dq_ref, dk_ref, dv_ref, dag_ref, g_ref, wg_hbm, gx_ref, dg_ref, wfull_ref, sem):
        @pl.when(pl.program_id(0) == 0)
        def _():
            _load_w_in(wg_hbm, wfull_ref, sem)
            dg_ref[...] = jnp.zeros_like(dg_ref)

        dproj_t = jnp.concatenate([dp_ref[...], dq_ref[...], dk_ref[...], dv_ref[...], dag_ref[...]], axis=1)
        dh = _nt(dproj_t, wfull_ref[...])
        xf = x_ref[...]
        r = lax.rsqrt(jnp.mean(xf * xf, axis=-1, keepdims=True) + EPS)
        xh = xf * r
        dg_ref[...] += jnp.sum(dh * xh, axis=0, keepdims=True)
        u = dh * g_ref[...]
        gx_ref[...] = dx2_ref[...] + r * (u - xh * jnp.mean(u * xh, axis=-1, keepdims=True))

    tile = pl.BlockSpec((TOKEN_TILE, D_MODEL), lambda i: (i, 0))
    return pl.pallas_call(
        body,
        name="inproj_bwd_dx",
        grid=(t // TOKEN_TILE,),
        out_shape=(jax.ShapeDtypeStruct((t, D_MODEL), F32), jax.ShapeDtypeStruct((1, D_MODEL), F32)),
        in_specs=[tile, tile] + _dproj_specs() + [
            pl.BlockSpec((1, D_MODEL), lambda i: (0, 0)),
            pl.BlockSpec(memory_space=pl.ANY),
        ],
        out_specs=(tile, pl.BlockSpec((1, D_MODEL), lambda i: (0, 0))),
        scratch_shapes=[pltpu.VMEM((D_MODEL, IN_WIDTH), BF16), pltpu.SemaphoreType.DMA((N_DEV,))],
        compiler_params=_params(("arbitrary",)),
    )(x2d, dx2, *dproj, g1, wg)


def _inproj_bwd_dw(x2d, pvg, dy_pool, dattn, g1, pool_w, pool_scale):
    t = x2d.shape[0]
    n_tiles = t // TOKEN_TILE
    halo_per_tile = TOKEN_TILE // HALO
    last_halo = t // HALO - 1

    def body(x_ref, cur_ref, prev_ref, pgn_ref, dy_ref, dyn_ref, dq_ref, dk_ref, dv_ref, dag_ref, g_ref, pw_ref, ps_ref,
             out_ref, dp_ref, dpw_ref, dps_ref, acc_ref):
        i = pl.program_id(0)

        @pl.when(i == 0)
        def _():
            acc_ref[...] = jnp.zeros_like(acc_ref)
            dpw_ref[...] = jnp.zeros_like(dpw_ref)
            dps_ref[...] = jnp.zeros_like(dps_ref)

        xf = x_ref[...]
        r = lax.rsqrt(jnp.mean(xf * xf, axis=-1, keepdims=True) + EPS)
        h = ((xf * r) * g_ref[...]).astype(BF16)
        col = 2 * POOL_WIDTH
        for ref in (dq_ref, dk_ref, dv_ref, dag_ref):
            acc_ref[:, col:col + ATTN_WIDTH] += _tn(h, ref[...])
            col += ATTN_WIDTH
        _pool_bwd_tile(i, n_tiles, cur_ref, prev_ref, pgn_ref, dy_ref, dyn_ref, pw_ref, ps_ref, dp_ref, dpw_ref, dps_ref)
        for c0 in (0, POOL_WIDTH):
            acc_ref[:, c0:c0 + POOL_WIDTH] += _tn(h, dp_ref[:, c0:c0 + POOL_WIDTH])

        @pl.when(i == n_tiles - 1)
        def _():
            for d in range(N_DEV):
                out_ref[d] = acc_ref[:, d * IN_SHARD:(d + 1) * IN_SHARD].astype(BF16)

    tile = lambda width: pl.BlockSpec((TOKEN_TILE, width), lambda i: (i, 0))
    next_halo = lambda col: pl.BlockSpec(
        (HALO, POOL_WIDTH), lambda i: (jnp.minimum((i + 1) * halo_per_tile, last_halo), col))
    return pl.pallas_call(
        body,
        name="inproj_bwd_dw",
        grid=(n_tiles,),
        out_shape=(
            jax.ShapeDtypeStruct((N_DEV, D_MODEL, IN_SHARD), BF16),
            jax.ShapeDtypeStruct((t, 2 * POOL_WIDTH), BF16),
            jax.ShapeDtypeStruct((N_GROUPS, GROUP_DIM, GROUP_DIM), F32),
            jax.ShapeDtypeStruct((1, POOL_WIDTH), F32),
        ),
        in_specs=[
            tile(D_MODEL),
            tile(2 * POOL_WIDTH),
            pl.BlockSpec((HALO, POOL_WIDTH), lambda i: (jnp.maximum(i * halo_per_tile - 1, 0), 0)),
            next_halo(1),
            tile(POOL_WIDTH),
            next_halo(0),
            tile(ATTN_WIDTH), tile(ATTN_WIDTH), tile(ATTN_WIDTH), tile(ATTN_WIDTH),
            pl.BlockSpec((1, D_MODEL), lambda i: (0, 0)),
            pl.BlockSpec((N_GROUPS, GROUP_DIM, GROUP_DIM), lambda i: (0, 0, 0)),
            pl.BlockSpec((1, POOL_WIDTH), lambda i: (0, 0)),
        ],
        out_specs=(
            pl.BlockSpec((N_DEV, D_MODEL, IN_SHARD), lambda i: (0, 0, 0)),
            tile(2 * POOL_WIDTH),
            pl.BlockSpec((N_GROUPS, GROUP_DIM, GROUP_DIM), lambda i: (0, 0, 0)),
            pl.BlockSpec((1, POOL_WIDTH), lambda i: (0, 0)),
        ),
        scratch_shapes=[pltpu.VMEM((D_MODEL, IN_WIDTH), F32)],
        compiler_params=_params(("arbitrary",)),
    )(x2d, pvg, pvg, pvg, dy_pool, dy_pool, *dattn, g1, pool_w, pool_scale)


_HBM = pl.BlockSpec(memory_space=pltpu.HBM)
_SEM = pl.BlockSpec(memory_space=pltpu.SEMAPHORE)
_DATAFLOW = pltpu.SideEffectType.DATAFLOW_SIDE_EFFECTING


_N_EXCHANGED = 3


def _exchange_all(refs, send_sems, recv_sems):
    win_hbm, win_land, pw_hbm, pw_land, vec_hbm, vec_land = refs
    return (_exchange_copies(win_hbm, win_land, send_sems, recv_sems)
            + _exchange_copies(pw_hbm, pw_land, send_sems, recv_sems, first_sem=N_DEV - 1)
            + _exchange_copies(vec_hbm, vec_land, send_sems, recv_sems, first_sem=2 * (N_DEV - 1), same_for_all=True))


def _exchange_start(win_blocks, pw_blocks, vec):
    arrays = []
    for a, land_shape in ((win_blocks, (N_DEV - 1,) + win_blocks.shape[1:]),
                          (pw_blocks, (N_DEV - 1,) + pw_blocks.shape[1:]),
                          (vec, (N_DEV - 1,) + vec.shape)):
        arrays += [pltpu.with_memory_space_constraint(a, pltpu.HBM),
                   pltpu.with_memory_space_constraint(lax.empty(land_shape, a.dtype), pltpu.HBM)]

    def body(*refs):
        ins, (send_sems, recv_sems), token = refs[:2 * _N_EXCHANGED], refs[2 * _N_EXCHANGED:2 * _N_EXCHANGED + 2], refs[-1]
        for cp in _exchange_all(ins, send_sems, recv_sems):
            cp.start()
        token[...] = jnp.zeros_like(token)

    sems = pltpu.SemaphoreType.DMA((_N_EXCHANGED * (N_DEV - 1),))
    return pl.pallas_call(
        body,
        name="exchange_start",
        out_shape=(sems, sems, *[pltpu.HBM(a.shape, a.dtype) for a in arrays], jax.ShapeDtypeStruct((8, LANES), F32)),
        in_specs=tuple([_HBM] * len(arrays)),
        out_specs=(_SEM, _SEM, *[_HBM] * len(arrays), pl.BlockSpec(memory_space=pltpu.VMEM)),
        input_output_aliases={k: 2 + k for k in range(len(arrays))},
        compiler_params=pltpu.CompilerParams(has_side_effects=_DATAFLOW),
    )(*arrays)


def _exchange_wait(send_sems, recv_sems, arrays, after):
    def body(*refs):
        ins = refs[:2 * _N_EXCHANGED]
        send_sems, recv_sems = refs[2 * _N_EXCHANGED:2 * _N_EXCHANGED + 2]
        for cp in _exchange_all(ins, send_sems, recv_sems):
            cp.wait_send()
            cp.wait_recv()

    return pl.pallas_call(
        body,
        name="exchange_wait",
        out_shape=tuple(pltpu.HBM(a.shape, a.dtype) for a in arrays),
        in_specs=(*[_HBM] * len(arrays), _SEM, _SEM, pl.BlockSpec(memory_space=pl.ANY)),
        out_specs=tuple([_HBM] * len(arrays)),
        input_output_aliases={k: k for k in range(len(arrays))},
        compiler_params=pltpu.CompilerParams(has_side_effects=_DATAFLOW),
    )(*arrays, send_sems, recv_sems, after)


def _adamw(w, g, m, v):
    m = ADAM_B1 * m + (1.0 - ADAM_B1) * g
    v = ADAM_B2 * v + (1.0 - ADAM_B2) * (g * g)
    m_hat = m / (1.0 - ADAM_B1 ** ADAM_STEP)
    v_hat = v / (1.0 - ADAM_B2 ** ADAM_STEP)
    delta = -ADAM_LR * (m_hat / (jnp.sqrt(v_hat) + ADAM_EPS) + ADAM_WD * w)
    return delta, m, v


def _small_allreduce(d_g1, vec, vec_land, pw_blocks, pw_land):
    def body(g1_ref, vec_ref, vland_ref, pwb_ref, pland_ref, vec_out, pw_out, vparts_ref, pparts_ref, rows_ref,
             slice_ref, send_sems, recv_sems):
        me = _dev_index(_mesh_pos())

        def from_devices(parts_ref, own, land_ref):
            parts_ref[0] = own
            parts_ref[1:] = land_ref[...]
            total = parts_ref[me]
            for s in range(1, N_DEV):
                total = total + parts_ref[me ^ s]
            return total

        slice_ref[...] = from_devices(pparts_ref, pwb_ref[me], pland_ref)

        def send(r, src, dst, k):
            return pltpu.make_async_remote_copy(
                src_ref=src, dst_ref=dst, send_sem=send_sems.at[2 * (r - 1) + k],
                recv_sem=recv_sems.at[2 * (r - 1) + k], device_id=_peer(r), device_id_type=MESH_ID)

        started = [cp for r in range(1, N_DEV)
                   for cp in (send(r, g1_ref, rows_ref.at[r], 0), send(r, slice_ref, pw_out.at[me], 1))]
        for cp in started:
            cp.start()
        rows_ref[0] = g1_ref[...]
        pw_out[me] = slice_ref[...]
        vec_out[...] = from_devices(vparts_ref, vec_ref[...], vland_ref)
        for cp in started:
            cp.wait_recv()
        for cp in started:
            cp.wait_send()
        g1 = rows_ref[me]
        for s in range(1, N_DEV):
            g1 = g1 + rows_ref[me ^ s]
        vec_out[_ROW_G1:_ROW_G1 + 1, :] = g1

    vm = pl.BlockSpec(memory_space=pltpu.VMEM)
    return pl.pallas_call(
        body,
        name="small_allreduce",
        out_shape=(jax.ShapeDtypeStruct((_VEC_ROWS, D_MODEL), F32),
                   jax.ShapeDtypeStruct((N_DEV, GROUP_DIM // 2, GROUP_DIM), F32)),
        in_specs=[vm] * 5,
        out_specs=(vm, vm),
        scratch_shapes=[
            pltpu.VMEM((N_DEV, _VEC_ROWS, D_MODEL), F32),
            pltpu.VMEM((N_DEV, GROUP_DIM // 2, GROUP_DIM), F32),
            pltpu.VMEM((N_DEV, 1, D_MODEL), F32),
            pltpu.VMEM((GROUP_DIM // 2, GROUP_DIM), F32),
            pltpu.SemaphoreType.DMA((2 * (N_DEV - 1),)),
            pltpu.SemaphoreType.DMA((2 * (N_DEV - 1),)),
        ],
        compiler_params=_params(),
    )(d_g1, vec, vec_land, pw_blocks, pw_land)


def _adamw_all(dwin_g, land_in, dwout_g, land_out, vec_sum, pw_sum, weights, big):
    small_shapes = [(1, D_MODEL), (1, D_MODEL), (1, POOL_WIDTH), (N_HEADS, 2 * LANES), (N_GROUPS, GROUP_DIM, GROUP_DIM)]

    def body(*refs):
        refs = list(refs)
        take = lambda n: [refs.pop(0) for _ in range(n)]
        dwin_hbm, lin_ref, dwout_hbm, lout_ref, vec_ref, pw_ref = take(6)
        small_wmv = [take(3) for _ in range(5)]
        big_wmv = [take(3) for _ in range(2)]
        big_out = [take(4) for _ in range(2)]
        small_out = [take(4) for _ in range(5)]
        own_in, own_out, local_sems = refs

        me = _dev_index(_mesh_pos())
        mine = [pltpu.make_async_copy(dwin_hbm.at[me], own_in, local_sems.at[0]),
                pltpu.make_async_copy(dwout_hbm.at[me], own_out, local_sems.at[1])]
        for cp in mine:
            cp.start()

        def update(g, wmv, outs):
            delta, m_new, v_new = _adamw(wmv[0][...], g, wmv[1][...], wmv[2][...])
            for ref, val in zip(outs, (g, delta, m_new, v_new)):
                ref[...] = val

        update(vec_ref[_ROW_G1:_ROW_G1 + 1, :], small_wmv[0], small_out[0])
        update(vec_ref[_ROW_G2:_ROW_G2 + 1, :], small_wmv[1], small_out[1])
        update(vec_ref[_ROW_PS:_ROW_PS + 1, :POOL_WIDTH], small_wmv[2], small_out[2])
        update(vec_ref[_ROW_RB:_ROW_RB + N_HEADS, :2 * LANES], small_wmv[3], small_out[3])
        update(pw_ref[...], small_wmv[4], small_out[4])
        for cp in mine:
            cp.wait()
        g_in = own_in[...].astype(F32)
        g_out = own_out[...].astype(F32)
        for r in range(N_DEV - 1):
            g_in = g_in + lin_ref[r].astype(F32)
            g_out = g_out + lout_ref[r].astype(F32)
        update(g_in, big_wmv[0], big_out[0])
        update(g_out, big_wmv[1], big_out[1])

    vm = pl.BlockSpec(memory_space=pltpu.VMEM)
    hbm = pl.BlockSpec(memory_space=pl.ANY)
    f32 = lambda shape: jax.ShapeDtypeStruct(shape, F32)
    out_shapes = [f32((D_MODEL, IN_SHARD))] * 4 + [f32((OUT_SHARD, D_MODEL))] * 4
    for shape in small_shapes:
        out_shapes += [f32(shape)] * 4
    args = [dwin_g, land_in, dwout_g, land_out, vec_sum, pw_sum]
    for wmv in weights:
        args += list(wmv)
    for wmv in big:
        args += list(wmv)
    return pl.pallas_call(
        body,
        name="adamw_all",
        out_shape=tuple(out_shapes),
        in_specs=[hbm, vm, hbm, vm] + [vm] * (len(args) - 4),
        out_specs=tuple([vm] * len(out_shapes)),
        scratch_shapes=[
            pltpu.VMEM((D_MODEL, IN_SHARD), BF16),
            pltpu.VMEM((OUT_SHARD, D_MODEL), BF16),
            pltpu.SemaphoreType.DMA((2,)),
        ],
        compiler_params=_params(),
    )(*args)


def kernel(x, norm_gain, w_in, pool_w, pool_scale, rel_bias, w_out, final_norm_gain, loss_target, m_norm_gain, m_w_in, m_pool_w, m_pool_scale, m_rel_bias, m_w_out, m_final_norm_gain, v_norm_gain, v_w_in, v_pool_w, v_pool_scale, v_rel_bias, v_w_out, v_final_norm_gain):
    t = x.shape[1]
    assert x.shape[0] == 1 and t % TOKEN_TILE == 0 and t // Q_BLOCK >= 4
    x2d = x[0]
    tgt2d = loss_target[0]
    g2 = final_norm_gain.reshape(1, D_MODEL)

    rb = rel_bias[0]
    rel_line = jnp.concatenate([
        jnp.broadcast_to(rb[:, :1], (N_HEADS, _REL_FIRST)), rb,
        jnp.broadcast_to(rb[:, N_REL - 1:], (N_HEADS, TOEPLITZ - _REL_FIRST - N_REL)),
    ], axis=1).reshape(N_HEADS, 1, TOEPLITZ)
    wg_in, bias_tile = _gather_weights(w_in[0], rel_line)

    pvg, qkv, ag, y_pool = _norm_inproj(x2d, norm_gain, wg_in, pool_w[0], pool_scale)
    o, y_attn, wg_out = _attn_fwd(qkv, ag, bias_tile, w_out[0])
    dx2, dy_pool, do, dag, dwout_g, d_g2, loss_sum = _outproj_loss(x2d, tgt2d, y_pool, y_attn, wg_out, g2, ag, o)
    dq, dk, dv, ds_band, ds_total, land_out = _attn_bwd(qkv, do, bias_tile, dwout_g)
    dwin_g, d_pool, d_pw, d_ps = _inproj_bwd_dw(x2d, pvg, dy_pool, (dq, dk, dv, dag), norm_gain, pool_w[0], pool_scale)
    vec = _pack_small(ds_band, ds_total, d_g2, d_ps, loss_sum)
    dproj = (d_pool, dq, dk, dv, dag)
    pw_blocks = d_pw.reshape(N_DEV, GROUP_DIM // 2, GROUP_DIM)
    send_sems, recv_sems, *exchanged, token = _exchange_start(dwin_g, pw_blocks, vec)
    grad_x, d_g1 = _inproj_bwd_dx(x2d, dx2, dproj, norm_gain + token[:1, :1], wg_in)
    dwin_g, land_in, pw_blocks, pw_land, vec, vec_land = _exchange_wait(send_sems, recv_sems, exchanged, d_g1)

    pad_rb = lambda a: jnp.pad(a[0], ((0, 0), (0, 2 * LANES - N_REL)))
    row = lambda a: a.reshape(1, D_MODEL)
    weights = [
        (norm_gain, m_norm_gain, v_norm_gain),
        (row(final_norm_gain), row(m_final_norm_gain), row(v_final_norm_gain)),
        (pool_scale, m_pool_scale, v_pool_scale),
        (pad_rb(rel_bias), pad_rb(m_rel_bias), pad_rb(v_rel_bias)),
        (pool_w[0], m_pool_w[0], v_pool_w[0]),
    ]
    big = [(w_in[0], m_w_in[0], v_w_in[0]), (w_out[0], m_w_out[0], v_w_out[0])]
    vec_sum, pw_sum = _small_allreduce(d_g1, vec, vec_land, pw_blocks, pw_land)
    res = _adamw_all(dwin_g, land_in, dwout_g, land_out, vec_sum, pw_sum.reshape(N_GROUPS, GROUP_DIM, GROUP_DIM),
                     weights, big)
    loss = 0.5 * vec_sum[_ROW_LOSS, 0]

    def leaves(k):
        g1_, g2_, ps_, rb_, pw_ = (res[8 + 4 * leaf + k] for leaf in range(5))
        return [g1_, res[k][None], pw_[None], ps_, rb_[None, :, :N_REL], res[4 + k][None], g2_.reshape(D_MODEL)]

    return (loss, grad_x[None], *leaves(0), *leaves(1), *leaves(2), *leaves(3))
```

```python
import math

import jax
import jax.numpy as jnp
from jax import lax
from jax.experimental import pallas as pl
from jax.experimental.pallas import tpu as pltpu

F32 = jnp.float32
BF16 = jnp.bfloat16
MESH_ID = pl.DeviceIdType.MESH

D_MODEL = 1024
POOL_WIDTH = 512
ATTN_WIDTH = 512
POOL_WINDOWS = (2, 4, 8, 16)
N_GROUPS = 4
GROUP_DIM = 128
HEAD_DIM = 64
N_HEADS = 8
CHUNK = 64
LEFT_CHUNKS = 8
MAX_REL = 64
N_REL = 2 * MAX_REL + 1
IN_WIDTH = 2 * POOL_WIDTH + 4 * ATTN_WIDTH
EPS = 1e-6
MASK_VALUE = -1e30
ATTN_SCALE = 1.0 / math.sqrt(HEAD_DIM)
ADAM_LR = 0.001
ADAM_B1 = 0.9
ADAM_B2 = 0.999
ADAM_EPS = 1e-08
ADAM_WD = 0.01
ADAM_STEP = 10

N_DEV = 8
IN_SHARD = IN_WIDTH // N_DEV
OUT_SHARD = D_MODEL // N_DEV

LANES = 128
TOKEN_TILE = 512
HALO = 16
Q_BLOCK = 256
KV_BLOCKS = 3
KV_WINDOW = KV_BLOCKS * Q_BLOCK
PAIR = 2 * HEAD_DIM
N_PAIRS = N_HEADS // 2
TOEPLITZ = 1024
VMEM_LIMIT = 56 * 1024 * 1024


def _params(sem=None, vmem=VMEM_LIMIT):
    return pltpu.CompilerParams(dimension_semantics=sem, vmem_limit_bytes=vmem)


def _sigmoid(x):
    return 1.0 / (1.0 + jnp.exp(-x))


def _nt(a, b):
    return lax.dot_general(a, b, (((1,), (1,)), ((), ())), preferred_element_type=F32)


def _tn(a, b):
    return lax.dot_general(a, b, (((0,), (0,)), ((), ())), preferred_element_type=F32)


def _nn(a, b):
    return jnp.dot(a, b, preferred_element_type=F32)


def _mesh_pos():
    return lax.axis_index("x"), lax.axis_index("y"), lax.axis_index("c")


def _dev_index(p):
    return 4 * p[0] + 2 * p[1] + p[2]


def _peer(r):
    x, y, c = _mesh_pos()
    return (x ^ ((r >> 2) & 1), y ^ ((r >> 1) & 1), c ^ (r & 1))


def _exchange_copies(src_hbm, land_hbm, send_sems, recv_sems, first_sem=0, same_for_all=False):
    return [
        pltpu.make_async_remote_copy(
            src_ref=src_hbm if same_for_all else src_hbm.at[_dev_index(_peer(r))], dst_ref=land_hbm.at[r - 1],
            send_sem=send_sems.at[first_sem + r - 1], recv_sem=recv_sems.at[first_sem + r - 1],
            device_id=_peer(r), device_id_type=MESH_ID)
        for r in range(1, N_DEV)
    ]


def _gather_weights(w_in_shard, rel_line):
    def body(win_ref, line_ref, gin_ref, bias_ref, sin_ref, send_sems, recv_sems):
        x, y, c = _mesh_pos()
        me, sibling = (x, y, c), (x, y, 1 - c)
        chips = [(1 - x, y), (x, 1 - y), (1 - x, 1 - y)]

        sin_ref[...] = win_ref[...].astype(BF16)
        gin_ref[_dev_index(me)] = sin_ref[...]

        def copy(k, block, to, from_shard=False):
            return pltpu.make_async_remote_copy(
                src_ref=sin_ref if from_shard else gin_ref.at[_dev_index(block)],
                dst_ref=gin_ref.at[_dev_index(block)],
                send_sem=send_sems.at[k],
                recv_sem=recv_sems.at[k],
                device_id=to,
                device_id_type=MESH_ID,
            )

        first = [copy(0, me, sibling, True)]
        first += [copy(1 + j, me, (*chip, c), True) for j, chip in enumerate(chips)]
        for cp in first:
            cp.start()
        passed = [copy(4 + j, (*chip, c), sibling) for j, chip in enumerate(chips)]

        def bias_heads(lo, hi):
            for h in range(lo, hi):
                bias_ref[h] = _toeplitz_bias(line_ref[h])

        bias_heads(0, N_HEADS - 3)
        for j, chip in enumerate(chips):
            copy(1 + j, (*chip, c), me).wait_recv()
            passed[j].start()
            bias_heads(N_HEADS - 3 + j, N_HEADS - 2 + j)
        copy(0, sibling, me).wait_recv()
        for j, chip in enumerate(chips):
            copy(4 + j, (*chip, 1 - c), me).wait_recv()
        for cp in first + passed:
            cp.wait_send()

    vm = pl.BlockSpec(memory_space=pltpu.VMEM)
    return pl.pallas_call(
        body,
        name="gather_weights",
        out_shape=(
            jax.ShapeDtypeStruct((N_DEV, D_MODEL, IN_SHARD), BF16),
            jax.ShapeDtypeStruct((N_HEADS, Q_BLOCK, KV_WINDOW), F32),
        ),
        in_specs=[vm, vm],
        out_specs=(vm, vm),
        scratch_shapes=[
            pltpu.VMEM((D_MODEL, IN_SHARD), BF16),
            pltpu.SemaphoreType.DMA((7,)),
            pltpu.SemaphoreType.DMA((7,)),
        ],
        compiler_params=_params(),
    )(w_in_shard, rel_line)


def _load_w_in(wg_hbm, wfull_ref, sem):
    copies = [
        pltpu.make_async_copy(wg_hbm.at[d], wfull_ref.at[:, d * IN_SHARD:(d + 1) * IN_SHARD], sem.at[d])
        for d in range(N_DEV)
    ]
    for cp in copies:
        cp.start()
    for cp in copies:
        cp.wait()


def _norm_inproj(x2d, g1, wg, pool_w, pool_scale):
    t = x2d.shape[0]

    def body(x_ref, g_ref, wg_hbm, pw_ref, ps_ref, pvg_ref, qkv_ref, ag_ref, yp_ref, wfull_ref, halo_ref, sem):
        i = pl.program_id(0)

        @pl.when(i == 0)
        def _():
            _load_w_in(wg_hbm, wfull_ref, sem)
            halo_ref[...] = jnp.zeros_like(halo_ref)

        xf = x_ref[...]
        r = lax.rsqrt(jnp.mean(xf * xf, axis=-1, keepdims=True) + EPS)
        h = ((xf * r) * g_ref[...]).astype(BF16)
        chunk = lambda ci: _nn(h, wfull_ref[:, ci * POOL_WIDTH:(ci + 1) * POOL_WIDTH])
        pv, pg = chunk(0), chunk(1)
        pvg_ref[:, :POOL_WIDTH] = pv
        pvg_ref[:, POOL_WIDTH:] = pg
        halo = halo_ref[...]
        halo_ref[...] = pv[TOKEN_TILE - HALO:]
        for gi in range(N_GROUPS):
            sl = slice(gi * GROUP_DIM, (gi + 1) * GROUP_DIM)
            d = _pool_diffs(pv, halo, i * TOKEN_TILE, gi)
            z = _nn(d.astype(BF16), pw_ref[gi].astype(BF16))
            g = pg[:, sl]
            yp_ref[:, sl] = ((z * ps_ref[:, sl]) * (g * _sigmoid(g))).astype(BF16)
        for ci in range(2, 5):
            qkv_ref[:, (ci - 2) * POOL_WIDTH:(ci - 1) * POOL_WIDTH] = chunk(ci).astype(BF16)
        ag_ref[...] = chunk(5)

    tile = lambda width: pl.BlockSpec((TOKEN_TILE, width), lambda i: (i, 0))
    return pl.pallas_call(
        body,
        name="norm_inproj",
        grid=(t // TOKEN_TILE,),
        out_shape=(
            jax.ShapeDtypeStruct((t, 2 * POOL_WIDTH), F32),
            jax.ShapeDtypeStruct((t, 3 * ATTN_WIDTH), BF16),
            jax.ShapeDtypeStruct((t, ATTN_WIDTH), F32),
            jax.ShapeDtypeStruct((t, POOL_WIDTH), BF16),
        ),
        in_specs=[
            tile(D_MODEL),
            pl.BlockSpec((1, D_MODEL), lambda i: (0, 0)),
            pl.BlockSpec(memory_space=pl.ANY),
            pl.BlockSpec((N_GROUPS, GROUP_DIM, GROUP_DIM), lambda i: (0, 0, 0)),
            pl.BlockSpec((1, POOL_WIDTH), lambda i: (0, 0)),
        ],
        out_specs=(tile(2 * POOL_WIDTH), tile(3 * ATTN_WIDTH), tile(ATTN_WIDTH), tile(POOL_WIDTH)),
        scratch_shapes=[
            pltpu.VMEM((D_MODEL, IN_WIDTH), BF16),
            pltpu.VMEM((HALO, POOL_WIDTH), F32),
            pltpu.SemaphoreType.DMA((N_DEV,)),
        ],
        compiler_params=_params(("arbitrary",)),
    )(x2d, g1, wg, pool_w, pool_scale)


def _inv_count(first_row, rows, window):
    tpos = first_row + lax.broadcasted_iota(jnp.int32, (rows, 1), 0)
    return 1.0 / jnp.minimum(tpos + 1, window).astype(F32)


def _causal_window_sum(ext, window):
    s, k = ext, 1
    while k < window:
        s = s + pltpu.roll(s, k, 0)
        k *= 2
    return s


def _pool_diffs(pv, halo, first_row, gi):
    w = POOL_WINDOWS[gi]
    sl = slice(gi * GROUP_DIM, (gi + 1) * GROUP_DIM)
    ext = jnp.concatenate([halo[:, sl], pv[:, sl]], axis=0)
    s = _causal_window_sum(ext, w)[HALO:]
    return s * _inv_count(first_row, pv.shape[0], w) - pv[:, sl]


def _pool_bwd_tile(i, n_tiles, cur_ref, prev_ref, pgn_ref, dy_ref, dyn_ref, pw_ref, ps_ref, dp_ref, dpw_ref, dps_ref):
    pv = cur_ref[:, :POOL_WIDTH]
    pg = cur_ref[:, POOL_WIDTH:]
    prev = jnp.where(i > 0, prev_ref[...], 0.0)
    has_next = i < n_tiles - 1
    rows = TOKEN_TILE + HALO
    for gi in range(N_GROUPS):
        w = POOL_WINDOWS[gi]
        sl = slice(gi * GROUP_DIM, (gi + 1) * GROUP_DIM)
        pw = pw_ref[gi].astype(BF16)
        ps = ps_ref[:, sl]
        d = _pool_diffs(pv, prev, i * TOKEN_TILE, gi).astype(BF16)
        z = _nn(d, pw)
        g_ext = jnp.concatenate([pg[:, sl], pgn_ref[:, sl]], axis=0)
        dy_ext = jnp.concatenate([dy_ref[:, sl], dyn_ref[:, sl]], axis=0).astype(F32)
        sig = _sigmoid(g_ext)
        gate = g_ext * sig
        dz_ext = ((dy_ext * gate) * ps).astype(BF16)
        dd_ext = _nt(dz_ext, pw)
        e = dd_ext * _inv_count(i * TOKEN_TILE, rows, w)
        row = lax.broadcasted_iota(jnp.int32, (rows, 1), 0)
        e = jnp.where(jnp.logical_or(row < TOKEN_TILE, has_next), e, 0.0)
        s, k = e, 1
        while k < w:
            s = s + pltpu.roll(s, rows - k, 0)
            k *= 2
        dp_ref[:, sl] = (s[:TOKEN_TILE] - dd_ext[:TOKEN_TILE]).astype(BF16)
        dy = dy_ext[:TOKEN_TILE]
        g = g_ext[:TOKEN_TILE]
        sg = sig[:TOKEN_TILE]
        dgate = sg * (1.0 + g * (1.0 - sg))
        dp_ref[:, POOL_WIDTH + gi * GROUP_DIM:POOL_WIDTH + (gi + 1) * GROUP_DIM] = (
            (dy * (z * ps)) * dgate).astype(BF16)
        dps_ref[:, sl] += jnp.sum((dy * gate[:TOKEN_TILE]) * z, axis=0, keepdims=True)
        dpw_ref[gi] += _tn(d, dz_ext[:TOKEN_TILE])


_REL_FIRST = KV_WINDOW - 1 - MAX_REL


def _skew_rows(a, right):
    rows, lanes = a.shape
    row = lax.broadcasted_iota(jnp.int32, a.shape, 0)
    for b in range(rows.bit_length() - 1):
        shift = (1 << b) if right else lanes - (1 << b)
        a = jnp.where((row >> b) & 1 == 1, pltpu.roll(a, shift, 1), a)
    return a


def _toeplitz_bias(line):
    a = jnp.broadcast_to(line, (Q_BLOCK, TOEPLITZ))
    a = _skew_rows(a, True)
    a = pltpu.roll(a, TOEPLITZ - (Q_BLOCK - 1), 1)
    a = a[:, :KV_WINDOW]
    qc = lax.broadcasted_iota(jnp.int32, a.shape, 0) // CHUNK
    kc = lax.broadcasted_iota(jnp.int32, a.shape, 1) // CHUNK
    visible = jnp.logical_and(kc >= qc, kc <= qc + LEFT_CHUNKS)
    return jnp.where(visible, a, MASK_VALUE)


_BAND_WIDTH = 2 * LANES
_BAND_START = (384, 512, 512, 512)
_BAND_REL = tuple(a - (KV_BLOCKS - 1) * Q_BLOCK - qc * CHUNK for qc, a in enumerate(_BAND_START))


_ROW_G1, _ROW_G2, _ROW_PS, _ROW_LOSS, _ROW_RB = 0, 1, 2, 3, 8
_VEC_ROWS = 16


def _pack_small(band, total, d_g2, d_ps, loss_row):
    def body(band_ref, total_ref, g2_ref, ps_ref, loss_ref, vec_ref):
        vec_ref[...] = jnp.zeros_like(vec_ref)
        vec_ref[_ROW_G2:_ROW_G2 + 1, :] = g2_ref[...]
        vec_ref[_ROW_PS:_ROW_PS + 1, :POOL_WIDTH] = ps_ref[...]
        vec_ref[_ROW_LOSS:_ROW_LOSS + 1, :LANES] = loss_ref[0:1, :]
        for h in range(N_HEADS):
            a = jnp.zeros((CHUNK, 2 * _BAND_WIDTH), F32)
            for qc in range(Q_BLOCK // CHUNK):
                z = jnp.concatenate([band_ref[h, qc], jnp.zeros((CHUNK, _BAND_WIDTH), F32)], axis=1)
                left = -MAX_REL - _BAND_REL[qc]
                a = a + (pltpu.roll(z, 2 * _BAND_WIDTH - left, 1) if left else z)
            a = _skew_rows(a, False)
            near = jnp.sum(a, axis=0, keepdims=True)[:, :2 * LANES]
            r = lax.broadcasted_iota(jnp.int32, near.shape, 1)
            near = jnp.where(jnp.logical_and(r >= 1, r < 2 * MAX_REL), near, 0.0)
            everything = jnp.sum(jnp.sum(total_ref[h], axis=0, keepdims=True), axis=1, keepdims=True)
            far = everything - jnp.sum(near, axis=1, keepdims=True)
            vec_ref[_ROW_RB + h:_ROW_RB + h + 1, :2 * LANES] = jnp.where(r == 0, far, near)

    vm = pl.BlockSpec(memory_space=pltpu.VMEM)
    return pl.pallas_call(
        body,
        name="pack_small",
        out_shape=jax.ShapeDtypeStruct((_VEC_ROWS, D_MODEL), F32),
        in_specs=[vm] * 5,
        out_specs=vm,
        compiler_params=_params(),
    )(band, total, d_g2, d_ps, loss_row)


def _head_lanes(hh):
    lane = lax.broadcasted_iota(jnp.int32, (1, PAIR), 1)
    return (lane < HEAD_DIM) if hh == 0 else (lane >= HEAD_DIM)


def _score_windows(nwin):
    if nwin < KV_BLOCKS:
        return [(slice(0, Q_BLOCK), slice(0, nwin * Q_BLOCK), None)]
    pieces = []
    for qc in range(Q_BLOCK // CHUNK):
        rows = slice(qc * CHUNK, (qc + 1) * CHUNK)
        if qc < 2:
            pieces.append((rows, slice(0, KV_WINDOW - LANES), slice(KV_WINDOW - LANES, KV_WINDOW)))
        else:
            pieces.append((rows, slice(LANES, KV_WINDOW), slice(0, LANES)))
    return pieces


def _block_rows(first_block, n_blocks):
    if isinstance(first_block, int):
        return pl.ds(first_block * Q_BLOCK, n_blocks * Q_BLOCK)
    return pl.ds(pl.multiple_of(first_block * Q_BLOCK, Q_BLOCK), n_blocks * Q_BLOCK)


def _attn_fwd(qkv, bias_tile, w_out_shard):
    t = qkv.shape[0]
    nb = t // Q_BLOCK

    def body(q_ref, k_ref, v_ref, bias_ref, wout_ref, o_ref, gout_hbm,
             s_scr, p_scr, linv_scr, sout_ref, send_sems, recv_sems, local_sem):
        pair = pl.program_id(0)
        me = _dev_index(_mesh_pos())
        mine = pltpu.make_async_copy(sout_ref, gout_hbm.at[me], local_sem)

        def shard_copy(r, block):
            return pltpu.make_async_remote_copy(
                src_ref=sout_ref, dst_ref=gout_hbm.at[block], send_sem=send_sems.at[r - 1],
                recv_sem=recv_sems.at[r - 1], device_id=_peer(r), device_id_type=MESH_ID)

        @pl.when(pair == 0)
        def _():
            sout_ref[...] = wout_ref[...].astype(BF16)
            mine.start()
            for r in range(1, N_DEV):
                shard_copy(r, me).start()

        def scores(i, slot, nwin):
            q = q_ref[_block_rows(i, 1), :]
            kw = k_ref[_block_rows(i + 1 - nwin, nwin), :]
            for hh in range(2):
                q_h = jnp.where(_head_lanes(hh), q, jnp.zeros_like(q)) * ATTN_SCALE
                s_scr[slot, hh, :, :nwin * Q_BLOCK] = _nt(q_h, kw)

        def softmax(slot, nwin):
            off = (KV_BLOCKS - nwin) * Q_BLOCK
            for hh in range(2):
                for rows, cols, rest in _score_windows(nwin):
                    s = s_scr[slot, hh, rows, cols] + bias_ref[hh, rows, off + cols.start:off + cols.stop]
                    e = jnp.exp(s - jnp.max(s, axis=-1, keepdims=True))
                    linv_scr[slot, hh, rows, :] = 1.0 / jnp.sum(e, axis=-1, keepdims=True)
                    p_scr[slot, hh, rows, cols] = e.astype(BF16)
                    if rest is not None:
                        p_scr[slot, hh, rows, rest] = jnp.zeros((CHUNK, LANES), BF16)

        def output(i, slot, nwin):
            vw = v_ref[_block_rows(i + 1 - nwin, nwin), :]
            outs = [_nn(p_scr[slot, hh, :, :nwin * Q_BLOCK], vw) * linv_scr[slot, hh] for hh in range(2)]
            o_ref[_block_rows(i, 1), :] = jnp.where(_head_lanes(0), outs[0], outs[1]).astype(BF16)

        scores(0, 0, 1)
        scores(1, 1, 2)
        softmax(0, 1)
        scores(2, 0, 3)
        softmax(1, 2)
        output(0, 0, 1)
        scores(3, 1, 3)
        softmax(0, 3)
        output(1, 1, 2)

        def two_steps(k, carry):
            i = 3 + 2 * k
            scores(i + 1, 0, KV_BLOCKS)
            softmax(1, KV_BLOCKS)
            output(i - 1, 0, KV_BLOCKS)
            scores(i + 2, 1, KV_BLOCKS)
            softmax(0, KV_BLOCKS)
            output(i, 1, KV_BLOCKS)
            return carry

        lax.fori_loop(0, (nb - 4) // 2, two_steps, 0)
        last = (nb - 1) % 2
        softmax(last, KV_BLOCKS)
        output(nb - 2, 1 - last, KV_BLOCKS)
        output(nb - 1, last, KV_BLOCKS)

        @pl.when(pair == N_PAIRS - 1)
        def _():
            for r in range(1, N_DEV):
                shard_copy(r, _dev_index(_peer(r))).wait_recv()
            for r in range(1, N_DEV):
                shard_copy(r, me).wait_send()
            mine.wait()

    col = lambda c0: pl.BlockSpec((t, PAIR), lambda j: (0, c0 + j))
    return pl.pallas_call(
        body,
        name="attn_fwd",
        grid=(N_PAIRS,),
        out_shape=(
            jax.ShapeDtypeStruct((t, ATTN_WIDTH), BF16),
            jax.ShapeDtypeStruct((N_DEV, OUT_SHARD, D_MODEL), BF16),
        ),
        in_specs=[col(0), col(N_PAIRS), col(2 * N_PAIRS),
                  pl.BlockSpec((2, Q_BLOCK, KV_WINDOW), lambda j: (j, 0, 0)),
                  pl.BlockSpec((OUT_SHARD, D_MODEL), lambda j: (0, 0))],
        out_specs=(col(0), pl.BlockSpec(memory_space=pl.ANY)),
        scratch_shapes=[
            pltpu.VMEM((2, 2, Q_BLOCK, KV_WINDOW), F32),
            pltpu.VMEM((2, 2, Q_BLOCK, KV_WINDOW), BF16),
            pltpu.VMEM((2, 2, Q_BLOCK, 1), F32),
            pltpu.VMEM((OUT_SHARD, D_MODEL), BF16),
            pltpu.SemaphoreType.DMA((N_DEV - 1,)),
            pltpu.SemaphoreType.DMA((N_DEV - 1,)),
            pltpu.SemaphoreType.DMA,
        ],
        compiler_params=_params(("arbitrary",)),
    )(qkv, qkv, qkv, bias_tile, w_out_shard)


def _attn_bwd(qkv, do, bias_tile, dwout_g):
    t = qkv.shape[0]
    nb = t // Q_BLOCK

    def body(q_ref, k_ref, v_ref, do_ref, bias_ref, dwout_hbm,
             dq_ref, dk_ref, dv_ref, band_ref, total_ref, land_hbm,
             s_scr, dp_scr, p_scr, dsb_scr, dq_scr, dk_acc, dv_acc, send_sems, recv_sems):
        exchange = _exchange_copies(dwout_hbm, land_hbm, send_sems, recv_sems)

        @pl.when(pl.program_id(0) == 0)
        def _():
            for cp in exchange:
                cp.start()

        band_ref[...] = jnp.zeros_like(band_ref)
        total_ref[...] = jnp.zeros_like(total_ref)

        def nwin_of(i):
            return min(i + 1, KV_BLOCKS) if isinstance(i, int) else KV_BLOCKS

        def operands(i, hh):
            lanes = _head_lanes(hh)
            q = q_ref[_block_rows(i, 1), :]
            do = do_ref[_block_rows(i, 1), :]
            return (jnp.where(lanes, q, jnp.zeros_like(q)) * ATTN_SCALE, jnp.where(lanes, do, jnp.zeros_like(do)))

        def products(i, hh):
            nwin = nwin_of(i)
            win = _block_rows(i + 1 - nwin, nwin)
            q_h, do_h = operands(i, hh)
            s_scr[hh, :, :nwin * Q_BLOCK] = _nt(q_h, k_ref[win, :])
            dp_scr[hh, :, :nwin * Q_BLOCK] = _nt(do_h, v_ref[win, :])

        def grads(i, hh):
            nwin = nwin_of(i)
            off = (KV_BLOCKS - nwin) * Q_BLOCK
            for rows, cols, rest in _score_windows(nwin):
                bias_cols = slice(off + cols.start, off + cols.stop)
                s = s_scr[hh, rows, cols] + bias_ref[hh, rows, bias_cols]
                e = jnp.exp(s - jnp.max(s, axis=-1, keepdims=True))
                p = e * (1.0 / jnp.sum(e, axis=-1, keepdims=True))
                dp = dp_scr[hh, rows, cols]
                ds = p * (dp - jnp.sum(p * dp, axis=-1, keepdims=True))
                total_ref[hh, rows, :] += sum(ds[:, c0:c0 + LANES] for c0 in range(0, ds.shape[1], LANES))
                for qc in range(rows.start // CHUNK, rows.stop // CHUNK):
                    lo = max(_BAND_START[qc], bias_cols.start)
                    hi = min(_BAND_START[qc] + _BAND_WIDTH, bias_cols.stop)
                    if lo < hi:
                        band_ref[hh, qc, :, lo - _BAND_START[qc]:hi - _BAND_START[qc]] += ds[
                            qc * CHUNK - rows.start:(qc + 1) * CHUNK - rows.start,
                            lo - bias_cols.start:hi - bias_cols.start]
                p_scr[hh, rows, cols] = p.astype(BF16)
                dsb_scr[hh, rows, cols] = ds.astype(BF16)
                if rest is not None:
                    p_scr[hh, rows, rest] = jnp.zeros((CHUNK, LANES), BF16)
                    dsb_scr[hh, rows, rest] = jnp.zeros((CHUNK, LANES), BF16)

        def ring(block):
            return block % KV_BLOCKS if isinstance(block, int) else lax.rem(block, KV_BLOCKS)

        def accumulate(i, hh):
            nwin = nwin_of(i)
            w = nwin * Q_BLOCK
            win = _block_rows(i + 1 - nwin, nwin)
            q_h, do_h = operands(i, hh)
            ds_b = dsb_scr[hh, :, :w]
            dq_h = _nn(ds_b, k_ref[win, :]) * ATTN_SCALE
            dkw = _tn(ds_b, q_h)
            dvw = _tn(p_scr[hh, :, :w], do_h)
            for b in range(nwin):
                slot = ring(i + 1 - nwin + b)
                part = slice(b * Q_BLOCK, (b + 1) * Q_BLOCK)
                if hh == 0 and b == nwin - 1:
                    dk_acc[slot] = dkw[part]
                    dv_acc[slot] = dvw[part]
                else:
                    dk_acc[slot] += dkw[part]
                    dv_acc[slot] += dvw[part]
            if hh == 0:
                dq_scr[...] = dq_h
            else:
                dq_ref[_block_rows(i, 1), :] = jnp.where(_head_lanes(0), dq_scr[...], dq_h).astype(BF16)
                if not (isinstance(i, int) and i < KV_BLOCKS - 1):
                    flush(i - (KV_BLOCKS - 1))

        def flush(block):
            dk_ref[_block_rows(block, 1), :] = dk_acc[ring(block)].astype(BF16)
            dv_ref[_block_rows(block, 1), :] = dv_acc[ring(block)].astype(BF16)

        def tile(n):
            return n // 2, n % 2

        def step(n):
            if n + 1 < 2 * nb:
                products(*tile(n + 1))
            grads(*tile(n))
            if n >= 1:
                accumulate(*tile(n - 1))

        products(0, 0)
        for n in range(2 * KV_BLOCKS):
            step(n)

        def two_steps(i, carry):
            products(i, 1)
            grads(i, 0)
            accumulate(i - 1, 1)
            products(i + 1, 0)
            grads(i, 1)
            accumulate(i, 0)
            return carry

        lax.fori_loop(KV_BLOCKS, nb - 1, two_steps, 0)
        step(2 * nb - 2)
        step(2 * nb - 1)
        accumulate(nb - 1, 1)
        flush(nb - 2)
        flush(nb - 1)

        @pl.when(pl.program_id(0) == N_PAIRS - 1)
        def _():
            for cp in exchange:
                cp.wait_recv()
            for cp in exchange:
                cp.wait_send()

    col = lambda c0: pl.BlockSpec((t, PAIR), lambda j: (0, c0 + j))
    tile_spec = pl.BlockSpec((2, Q_BLOCK, KV_WINDOW), lambda j: (j, 0, 0))
    out = jax.ShapeDtypeStruct((t, ATTN_WIDTH), BF16)
    return pl.pallas_call(
        body,
        name="attn_bwd",
        grid=(N_PAIRS,),
        out_shape=(out, out, out,
                   jax.ShapeDtypeStruct((N_HEADS, Q_BLOCK // CHUNK, CHUNK, _BAND_WIDTH), F32),
                   jax.ShapeDtypeStruct((N_HEADS, Q_BLOCK, LANES), F32),
                   jax.ShapeDtypeStruct((N_DEV - 1, OUT_SHARD, D_MODEL), BF16)),
        in_specs=[col(0), col(N_PAIRS), col(2 * N_PAIRS), col(0), tile_spec, pl.BlockSpec(memory_space=pl.ANY)],
        out_specs=(col(0), col(0), col(0),
                   pl.BlockSpec((2, Q_BLOCK // CHUNK, CHUNK, _BAND_WIDTH), lambda j: (j, 0, 0, 0)),
                   pl.BlockSpec((2, Q_BLOCK, LANES), lambda j: (j, 0, 0)),
                   pl.BlockSpec(memory_space=pl.ANY)),
        scratch_shapes=[
            pltpu.VMEM((2, Q_BLOCK, KV_WINDOW), F32),
            pltpu.VMEM((2, Q_BLOCK, KV_WINDOW), F32),
            pltpu.VMEM((2, Q_BLOCK, KV_WINDOW), BF16),
            pltpu.VMEM((2, Q_BLOCK, KV_WINDOW), BF16),
            pltpu.VMEM((Q_BLOCK, PAIR), F32),
            pltpu.VMEM((KV_BLOCKS, Q_BLOCK, PAIR), F32),
            pltpu.VMEM((KV_BLOCKS, Q_BLOCK, PAIR), F32),
            pltpu.SemaphoreType.DMA((N_DEV - 1,)),
            pltpu.SemaphoreType.DMA((N_DEV - 1,)),
        ],
        compiler_params=_params(("arbitrary",)),
    )(qkv, qkv, qkv, do, bias_tile, dwout_g)


def _outproj_loss(x2d, tgt2d, y_pool, o, ag, wout_g, g2):
    t = x2d.shape[0]
    n_tiles = t // TOKEN_TILE

    def body(x_ref, tgt_ref, yp_ref, o_ref, ag_ref, w_ref, g_ref,
             dx2_ref, dyp_ref, do_ref, dag_ref, dw_ref, dg_ref, loss_ref, acc_ref):
        i = pl.program_id(0)

        @pl.when(i == 0)
        def _():
            acc_ref[...] = jnp.zeros_like(acc_ref)
            dg_ref[...] = jnp.zeros_like(dg_ref)
            loss_ref[...] = jnp.zeros_like(loss_ref)

        w = w_ref[...].reshape(D_MODEL, D_MODEL)
        ga = ag_ref[...]
        sig = _sigmoid(ga)
        gate = ga * sig
        of = o_ref[...].astype(F32)
        y = jnp.concatenate([yp_ref[...], (of * gate).astype(BF16)], axis=1)
        x2 = x_ref[...] + _nn(y, w)
        r = lax.rsqrt(jnp.mean(x2 * x2, axis=-1, keepdims=True) + EPS)
        xh = x2 * r
        g = g_ref[...]
        diff = xh * g - tgt_ref[...]
        tok = jnp.sum(diff * diff, axis=-1, keepdims=True) * (1.0 / D_MODEL)
        loss_ref[...] += jnp.sum(tok, axis=0, keepdims=True)
        dout = diff * (1.0 / D_MODEL)
        dg_ref[...] += jnp.sum(dout * xh, axis=0, keepdims=True)
        u = dout * g
        dx2 = r * (u - xh * jnp.mean(u * xh, axis=-1, keepdims=True))
        dx2_ref[...] = dx2
        dx2_b = dx2.astype(BF16)
        dy = _nt(dx2_b, w)
        dyp_ref[...] = dy[:, :POOL_WIDTH].astype(BF16)
        dya = dy[:, POOL_WIDTH:]
        do_ref[...] = (dya * gate).astype(BF16)
        dag_ref[...] = ((dya * of) * (sig * (1.0 + ga * (1.0 - sig)))).astype(BF16)
        acc_ref[...] += _tn(y, dx2_b)

        @pl.when(i == n_tiles - 1)
        def _():
            dw_ref[...] = acc_ref[...].reshape(N_DEV, OUT_SHARD, D_MODEL).astype(BF16)

    tile = lambda width: pl.BlockSpec((TOKEN_TILE, width), lambda i: (i, 0))
    return pl.pallas_call(
        body,
        name="outproj_loss",
        grid=(n_tiles,),
        out_shape=(
            jax.ShapeDtypeStruct((t, D_MODEL), F32),
            jax.ShapeDtypeStruct((t, POOL_WIDTH), BF16),
            jax.ShapeDtypeStruct((t, ATTN_WIDTH), BF16),
            jax.ShapeDtypeStruct((t, ATTN_WIDTH), BF16),
            jax.ShapeDtypeStruct((N_DEV, OUT_SHARD, D_MODEL), BF16),
            jax.ShapeDtypeStruct((1, D_MODEL), F32),
            jax.ShapeDtypeStruct((8, LANES), F32),
        ),
        in_specs=[
            tile(D_MODEL), tile(D_MODEL), tile(POOL_WIDTH), tile(ATTN_WIDTH), tile(ATTN_WIDTH),
            pl.BlockSpec((N_DEV, OUT_SHARD, D_MODEL), lambda i: (0, 0, 0)),
            pl.BlockSpec((1, D_MODEL), lambda i: (0, 0)),
        ],
        out_specs=(
            tile(D_MODEL), tile(POOL_WIDTH), tile(ATTN_WIDTH), tile(ATTN_WIDTH),
            pl.BlockSpec((N_DEV, OUT_SHARD, D_MODEL), lambda i: (0, 0, 0)),
            pl.BlockSpec((1, D_MODEL), lambda i: (0, 0)),
            pl.BlockSpec((8, LANES), lambda i: (0, 0)),
        ),
        scratch_shapes=[pltpu.VMEM((D_MODEL, D_MODEL), F32)],
        compiler_params=_params(("arbitrary",)),
    )(x2d, tgt2d, y_pool, o, ag, wout_g, g2)


def _dproj_specs():
    tile = lambda width: pl.BlockSpec((TOKEN_TILE, width), lambda i: (i, 0))
    return [tile(2 * POOL_WIDTH)] + [tile(ATTN_WIDTH)] * 4


def _inproj_bwd_dx(x2d, dx2, dproj, g1, wg):
    t = x2d.shape[0]

    def body(x_ref, dx2_ref, dp_ref, dq_ref, dk_ref, dv_ref, dag_ref, g_ref, wg_hbm, gx_ref, dg_ref, wfull_ref, sem):
        @pl.when(pl.program_id(0) == 0)
        def _():
            _load_w_in(wg_hbm, wfull_ref, sem)
            dg_ref[...] = jnp.zeros_like(dg_ref)

        dproj_t = jnp.concatenate([dp_ref[...], dq_ref[...], dk_ref[...], dv_ref[...], dag_ref[...]], axis=1)
        dh = _nt(dproj_t, wfull_ref[...])
        xf = x_ref[...]
        r = lax.rsqrt(jnp.mean(xf * xf, axis=-1, keepdims=True) + EPS)
        xh = xf * r
        dg_ref[...] += jnp.sum(dh * xh, axis=0, keepdims=True)
        u = dh * g_ref[...]
        gx_ref[...] = dx2_ref[...] + r * (u - xh * jnp.mean(u * xh, axis=-1, keepdims=True))

    tile = pl.BlockSpec((TOKEN_TILE, D_MODEL), lambda i: (i, 0))
    return pl.pallas_call(
        body,
        name="inproj_bwd_dx",
        grid=(t // TOKEN_TILE,),
        out_shape=(jax.ShapeDtypeStruct((t, D_MODEL), F32), jax.ShapeDtypeStruct((1, D_MODEL), F32)),
        in_specs=[tile, tile] + _dproj_specs() + [
            pl.BlockSpec((1, D_MODEL), lambda i: (0, 0)),
            pl.BlockSpec(memory_space=pl.ANY),
        ],
        out_specs=(tile, pl.BlockSpec((1, D_MODEL), lambda i: (0, 0))),
        scratch_shapes=[pltpu.VMEM((D_MODEL, IN_WIDTH), BF16), pltpu.SemaphoreType.DMA((N_DEV,))],
        compiler_params=_params(("arbitrary",)),
    )(x2d, dx2, *dproj, g1, wg)


def _inproj_bwd_dw(x2d, pvg, dy_pool, dattn, g1, pool_w, pool_scale):
    t = x2d.shape[0]
    n_tiles = t // TOKEN_TILE
    halo_per_tile = TOKEN_TILE // HALO
    last_halo = t // HALO - 1

    def body(x_ref, cur_ref, prev_ref, pgn_ref, dy_ref, dyn_ref, dq_ref, dk_ref, dv_ref, dag_ref, g_ref, pw_ref, ps_ref,
             out_ref, dp_ref, dpw_ref, dps_ref, acc_ref):
        i = pl.program_id(0)

        @pl.when(i == 0)
        def _():
            acc_ref[...] = jnp.zeros_like(acc_ref)
            dpw_ref[...] = jnp.zeros_like(dpw_ref)
            dps_ref[...] = jnp.zeros_like(dps_ref)

        xf = x_ref[...]
        r = lax.rsqrt(jnp.mean(xf * xf, axis=-1, keepdims=True) + EPS)
        h = ((xf * r) * g_ref[...]).astype(BF16)
        col = 2 * POOL_WIDTH
        for ref in (dq_ref, dk_ref, dv_ref, dag_ref):
            acc_ref[:, col:col + ATTN_WIDTH] += _tn(h, ref[...])
            col += ATTN_WIDTH
        _pool_bwd_tile(i, n_tiles, cur_ref, prev_ref, pgn_ref, dy_ref, dyn_ref, pw_ref, ps_ref, dp_ref, dpw_ref, dps_ref)
        for c0 in (0, POOL_WIDTH):
            acc_ref[:, c0:c0 + POOL_WIDTH] += _tn(h, dp_ref[:, c0:c0 + POOL_WIDTH])

        @pl.when(i == n_tiles - 1)
        def _():
            for d in range(N_DEV):
                out_ref[d] = acc_ref[:, d * IN_SHARD:(d + 1) * IN_SHARD].astype(BF16)

    tile = lambda width: pl.BlockSpec((TOKEN_TILE, width), lambda i: (i, 0))
    next_halo = lambda col: pl.BlockSpec(
        (HALO, POOL_WIDTH), lambda i: (jnp.minimum((i + 1) * halo_per_tile, last_halo), col))
    return pl.pallas_call(
        body,
        name="inproj_bwd_dw",
        grid=(n_tiles,),
        out_shape=(
            jax.ShapeDtypeStruct((N_DEV, D_MODEL, IN_SHARD), BF16),
            jax.ShapeDtypeStruct((t, 2 * POOL_WIDTH), BF16),
            jax.ShapeDtypeStruct((N_GROUPS, GROUP_DIM, GROUP_DIM), F32),
            jax.ShapeDtypeStruct((1, POOL_WIDTH), F32),
        ),
        in_specs=[
            tile(D_MODEL),
            tile(2 * POOL_WIDTH),
            pl.BlockSpec((HALO, POOL_WIDTH), lambda i: (jnp.maximum(i * halo_per_tile - 1, 0), 0)),
            next_halo(1),
            tile(POOL_WIDTH),
            next_halo(0),
            tile(ATTN_WIDTH), tile(ATTN_WIDTH), tile(ATTN_WIDTH), tile(ATTN_WIDTH),
            pl.BlockSpec((1, D_MODEL), lambda i: (0, 0)),
            pl.BlockSpec((N_GROUPS, GROUP_DIM, GROUP_DIM), lambda i: (0, 0, 0)),
            pl.BlockSpec((1, POOL_WIDTH), lambda i: (0, 0)),
        ],
        out_specs=(
            pl.BlockSpec((N_DEV, D_MODEL, IN_SHARD), lambda i: (0, 0, 0)),
            tile(2 * POOL_WIDTH),
            pl.BlockSpec((N_GROUPS, GROUP_DIM, GROUP_DIM), lambda i: (0, 0, 0)),
            pl.BlockSpec((1, POOL_WIDTH), lambda i: (0, 0)),
        ),
        scratch_shapes=[pltpu.VMEM((D_MODEL, IN_WIDTH), F32)],
        compiler_params=_params(("arbitrary",)),
    )(x2d, pvg, pvg, pvg, dy_pool, dy_pool, *dattn, g1, pool_w, pool_scale)


_HBM = pl.BlockSpec(memory_space=pltpu.HBM)
_SEM = pl.BlockSpec(memory_space=pltpu.SEMAPHORE)
_DATAFLOW = pltpu.SideEffectType.DATAFLOW_SIDE_EFFECTING


_N_EXCHANGED = 3


def _exchange_all(refs, send_sems, recv_sems):
    win_hbm, win_land, pw_hbm, pw_land, vec_hbm, vec_land = refs
    return (_exchange_copies(win_hbm, win_land, send_sems, recv_sems)
            + _exchange_copies(pw_hbm, pw_land, send_sems, recv_sems, first_sem=N_DEV - 1)
            + _exchange_copies(vec_hbm, vec_land, send_sems, recv_sems, first_sem=2 * (N_DEV - 1), same_for_all=True))


def _exchange_start(win_blocks, pw_blocks, vec):
    arrays = []
    for a, land_shape in ((win_blocks, (N_DEV - 1,) + win_blocks.shape[1:]),
                          (pw_blocks, (N_DEV - 1,) + pw_blocks.shape[1:]),
                          (vec, (N_DEV - 1,) + vec.shape)):
        arrays += [pltpu.with_memory_space_constraint(a, pltpu.HBM),
                   pltpu.with_memory_space_constraint(lax.empty(land_shape, a.dtype), pltpu.HBM)]

    def body(*refs):
        ins, (send_sems, recv_sems), token = refs[:2 * _N_EXCHANGED], refs[2 * _N_EXCHANGED:2 * _N_EXCHANGED + 2], refs[-1]
        for cp in _exchange_all(ins, send_sems, recv_sems):
            cp.start()
        token[...] = jnp.zeros_like(token)

    sems = pltpu.SemaphoreType.DMA((_N_EXCHANGED * (N_DEV - 1),))
    return pl.pallas_call(
        body,
        name="exchange_start",
        out_shape=(sems, sems, *[pltpu.HBM(a.shape, a.dtype) for a in arrays], jax.ShapeDtypeStruct((8, LANES), F32)),
        in_specs=tuple([_HBM] * len(arrays)),
        out_specs=(_SEM, _SEM, *[_HBM] * len(arrays), pl.BlockSpec(memory_space=pltpu.VMEM)),
        input_output_aliases={k: 2 + k for k in range(len(arrays))},
        compiler_params=pltpu.CompilerParams(has_side_effects=_DATAFLOW),
    )(*arrays)


def _exchange_wait(send_sems, recv_sems, arrays, after):
    def body(*refs):
        ins = refs[:2 * _N_EXCHANGED]
        send_sems, recv_sems = refs[2 * _N_EXCHANGED:2 * _N_EXCHANGED + 2]
        for cp in _exchange_all(ins, send_sems, recv_sems):
            cp.wait_send()
            cp.wait_recv()

    return pl.pallas_call(
        body,
        name="exchange_wait",
        out_shape=tuple(pltpu.HBM(a.shape, a.dtype) for a in arrays),
        in_specs=(*[_HBM] * len(arrays), _SEM, _SEM, pl.BlockSpec(memory_space=pl.ANY)),
        out_specs=tuple([_HBM] * len(arrays)),
        input_output_aliases={k: k for k in range(len(arrays))},
        compiler_params=pltpu.CompilerParams(has_side_effects=_DATAFLOW),
    )(*arrays, send_sems, recv_sems, after)


def _adamw(w, g, m, v):
    m = ADAM_B1 * m + (1.0 - ADAM_B1) * g
    v = ADAM_B2 * v + (1.0 - ADAM_B2) * (g * g)
    m_hat = m / (1.0 - ADAM_B1 ** ADAM_STEP)
    v_hat = v / (1.0 - ADAM_B2 ** ADAM_STEP)
    delta = -ADAM_LR * (m_hat / (jnp.sqrt(v_hat) + ADAM_EPS) + ADAM_WD * w)
    return delta, m, v


def _small_allreduce(d_g1, vec, vec_land, pw_blocks, pw_land):
    def body(g1_ref, vec_ref, vland_ref, pwb_ref, pland_ref, vec_out, pw_out, vparts_ref, pparts_ref, rows_ref,
             slice_ref, send_sems, recv_sems):
        me = _dev_index(_mesh_pos())

        def from_devices(parts_ref, own, land_ref):
            parts_ref[0] = own
            parts_ref[1:] = land_ref[...]
            total = parts_ref[me]
            for s in range(1, N_DEV):
                total = total + parts_ref[me ^ s]
            return total

        slice_ref[...] = from_devices(pparts_ref, pwb_ref[me], pland_ref)

        def send(r, src, dst, k):
            return pltpu.make_async_remote_copy(
                src_ref=src, dst_ref=dst, send_sem=send_sems.at[2 * (r - 1) + k],
                recv_sem=recv_sems.at[2 * (r - 1) + k], device_id=_peer(r), device_id_type=MESH_ID)

        started = [cp for r in range(1, N_DEV)
                   for cp in (send(r, g1_ref, rows_ref.at[r], 0), send(r, slice_ref, pw_out.at[me], 1))]
        for cp in started:
            cp.start()
        rows_ref[0] = g1_ref[...]
        pw_out[me] = slice_ref[...]
        vec_out[...] = from_devices(vparts_ref, vec_ref[...], vland_ref)
        for cp in started:
            cp.wait_recv()
        for cp in started:
            cp.wait_send()
        g1 = rows_ref[me]
        for s in range(1, N_DEV):
            g1 = g1 + rows_ref[me ^ s]
        vec_out[_ROW_G1:_ROW_G1 + 1, :] = g1

    vm = pl.BlockSpec(memory_space=pltpu.VMEM)
    return pl.pallas_call(
        body,
        name="small_allreduce",
        out_shape=(jax.ShapeDtypeStruct((_VEC_ROWS, D_MODEL), F32),
                   jax.ShapeDtypeStruct((N_DEV, GROUP_DIM // 2, GROUP_DIM), F32)),
        in_specs=[vm] * 5,
        out_specs=(vm, vm),
        scratch_shapes=[
            pltpu.VMEM((N_DEV, _VEC_ROWS, D_MODEL), F32),
            pltpu.VMEM((N_DEV, GROUP_DIM // 2, GROUP_DIM), F32),
            pltpu.VMEM((N_DEV, 1, D_MODEL), F32),
            pltpu.VMEM((GROUP_DIM // 2, GROUP_DIM), F32),
            pltpu.SemaphoreType.DMA((2 * (N_DEV - 1),)),
            pltpu.SemaphoreType.DMA((2 * (N_DEV - 1),)),
        ],
        compiler_params=_params(),
    )(d_g1, vec, vec_land, pw_blocks, pw_land)


def _adamw_all(dwin_g, land_in, dwout_g, land_out, vec_sum, pw_sum, weights, big):
    small_shapes = [(1, D_MODEL), (1, D_MODEL), (1, POOL_WIDTH), (N_HEADS, 2 * LANES), (N_GROUPS, GROUP_DIM, GROUP_DIM)]

    def body(*refs):
        refs = list(refs)
        take = lambda n: [refs.pop(0) for _ in range(n)]
        dwin_hbm, lin_ref, dwout_hbm, lout_ref, vec_ref, pw_ref = take(6)
        small_wmv = [take(3) for _ in range(5)]
        big_wmv = [take(3) for _ in range(2)]
        big_out = [take(4) for _ in range(2)]
        small_out = [take(4) for _ in range(5)]
        own_in, own_out, local_sems = refs

        me = _dev_index(_mesh_pos())
        mine = [pltpu.make_async_copy(dwin_hbm.at[me], own_in, local_sems.at[0]),
                pltpu.make_async_copy(dwout_hbm.at[me], own_out, local_sems.at[1])]
        for cp in mine:
            cp.start()

        def update(g, wmv, outs):
            delta, m_new, v_new = _adamw(wmv[0][...], g, wmv[1][...], wmv[2][...])
            for ref, val in zip(outs, (g, delta, m_new, v_new)):
                ref[...] = val

        update(vec_ref[_ROW_G1:_ROW_G1 + 1, :], small_wmv[0], small_out[0])
        update(vec_ref[_ROW_G2:_ROW_G2 + 1, :], small_wmv[1], small_out[1])
        update(vec_ref[_ROW_PS:_ROW_PS + 1, :POOL_WIDTH], small_wmv[2], small_out[2])
        update(vec_ref[_ROW_RB:_ROW_RB + N_HEADS, :2 * LANES], small_wmv[3], small_out[3])
        update(pw_ref[...], small_wmv[4], small_out[4])
        for cp in mine:
            cp.wait()
        g_in = own_in[...].astype(F32)
        g_out = own_out[...].astype(F32)
        for r in range(N_DEV - 1):
            g_in = g_in + lin_ref[r].astype(F32)
            g_out = g_out + lout_ref[r].astype(F32)
        update(g_in, big_wmv[0], big_out[0])
        update(g_out, big_wmv[1], big_out[1])

    vm = pl.BlockSpec(memory_space=pltpu.VMEM)
    hbm = pl.BlockSpec(memory_space=pl.ANY)
    f32 = lambda shape: jax.ShapeDtypeStruct(shape, F32)
    out_shapes = [f32((D_MODEL, IN_SHARD))] * 4 + [f32((OUT_SHARD, D_MODEL))] * 4
    for shape in small_shapes:
        out_shapes += [f32(shape)] * 4
    args = [dwin_g, land_in, dwout_g, land_out, vec_sum, pw_sum]
    for wmv in weights:
        args += list(wmv)
    for wmv in big:
        args += list(wmv)
    return pl.pallas_call(
        body,
        name="adamw_all",
        out_shape=tuple(out_shapes),
        in_specs=[hbm, vm, hbm, vm] + [vm] * (len(args) - 4),
        out_specs=tuple([vm] * len(out_shapes)),
        scratch_shapes=[
            pltpu.VMEM((D_MODEL, IN_SHARD), BF16),
            pltpu.VMEM((OUT_SHARD, D_MODEL), BF16),
            pltpu.SemaphoreType.DMA((2,)),
        ],
        compiler_params=_params(),
    )(*args)


def kernel(x, norm_gain, w_in, pool_w, pool_scale, rel_bias, w_out, final_norm_gain, loss_target, m_norm_gain, m_w_in, m_pool_w, m_pool_scale, m_rel_bias, m_w_out, m_final_norm_gain, v_norm_gain, v_w_in, v_pool_w, v_pool_scale, v_rel_bias, v_w_out, v_final_norm_gain):
    t = x.shape[1]
    assert x.shape[0] == 1 and t % TOKEN_TILE == 0 and t // Q_BLOCK >= 4
    x2d = x[0]
    tgt2d = loss_target[0]
    g2 = final_norm_gain.reshape(1, D_MODEL)

    rb = rel_bias[0]
    rel_line = jnp.concatenate([
        jnp.broadcast_to(rb[:, :1], (N_HEADS, _REL_FIRST)), rb,
        jnp.broadcast_to(rb[:, N_REL - 1:], (N_HEADS, TOEPLITZ - _REL_FIRST - N_REL)),
    ], axis=1).reshape(N_HEADS, 1, TOEPLITZ)
    wg_in, bias_tile = _gather_weights(w_in[0], rel_line)

    pvg, qkv, ag, y_pool = _norm_inproj(x2d, norm_gain, wg_in, pool_w[0], pool_scale)
    o, wg_out = _attn_fwd(qkv, bias_tile, w_out[0])
    dx2, dy_pool, do, dag, dwout_g, d_g2, loss_sum = _outproj_loss(x2d, tgt2d, y_pool, o, ag, wg_out, g2)
    dq, dk, dv, ds_band, ds_total, land_out = _attn_bwd(qkv, do, bias_tile, dwout_g)
    dwin_g, d_pool, d_pw, d_ps = _inproj_bwd_dw(x2d, pvg, dy_pool, (dq, dk, dv, dag), norm_gain, pool_w[0], pool_scale)
    vec = _pack_small(ds_band, ds_total, d_g2, d_ps, loss_sum)
    dproj = (d_pool, dq, dk, dv, dag)
    pw_blocks = d_pw.reshape(N_DEV, GROUP_DIM // 2, GROUP_DIM)
    send_sems, recv_sems, *exchanged, token = _exchange_start(dwin_g, pw_blocks, vec)
    grad_x, d_g1 = _inproj_bwd_dx(x2d, dx2, dproj, norm_gain + token[:1, :1], wg_in)
    dwin_g, land_in, pw_blocks, pw_land, vec, vec_land = _exchange_wait(send_sems, recv_sems, exchanged, d_g1)

    pad_rb = lambda a: jnp.pad(a[0], ((0, 0), (0, 2 * LANES - N_REL)))
    row = lambda a: a.reshape(1, D_MODEL)
    weights = [
        (norm_gain, m_norm_gain, v_norm_gain),
        (row(final_norm_gain), row(m_final_norm_gain), row(v_final_norm_gain)),
        (pool_scale, m_pool_scale, v_pool_scale),
        (pad_rb(rel_bias), pad_rb(m_rel_bias), pad_rb(v_rel_bias)),
        (pool_w[0], m_pool_w[0], v_pool_w[0]),
    ]
    big = [(w_in[0], m_w_in[0], v_w_in[0]), (w_out[0], m_w_out[0], v_w_out[0])]
    vec_sum, pw_sum = _small_allreduce(d_g1, vec, vec_land, pw_blocks, pw_land)
    res = _adamw_all(dwin_g, land_in, dwout_g, land_out, vec_sum, pw_sum.reshape(N_GROUPS, GROUP_DIM, GROUP_DIM),
                     weights, big)
    loss = 0.5 * vec_sum[_ROW_LOSS, 0]

    def leaves(k):
        g1_, g2_, ps_, rb_, pw_ = (res[8 + 4 * leaf + k] for leaf in range(5))
        return [g1_, res[k][None], pw_[None], ps_, rb_[None, :, :N_REL], res[4 + k][None], g2_.reshape(D_MODEL)]

    return (loss, grad_x[None], *leaves(0), *leaves(1), *leaves(2), *leaves(3))
```

```python
import math

import jax
import jax.numpy as jnp
from jax import lax
from jax.experimental import pallas as pl
from jax.experimental.pallas import tpu as pltpu

F32 = jnp.float32
BF16 = jnp.bfloat16
MESH_ID = pl.DeviceIdType.MESH

D_MODEL = 1024
POOL_WIDTH = 512
ATTN_WIDTH = 512
POOL_WINDOWS = (2, 4, 8, 16)
N_GROUPS = 4
GROUP_DIM = 128
HEAD_DIM = 64
N_HEADS = 8
CHUNK = 64
LEFT_CHUNKS = 8
MAX_REL = 64
N_REL = 2 * MAX_REL + 1
IN_WIDTH = 2 * POOL_WIDTH + 4 * ATTN_WIDTH
EPS = 1e-6
MASK_VALUE = -1e30
ATTN_SCALE = 1.0 / math.sqrt(HEAD_DIM)
ADAM_LR = 0.001
ADAM_B1 = 0.9
ADAM_B2 = 0.999
ADAM_EPS = 1e-08
ADAM_WD = 0.01
ADAM_STEP = 10

N_DEV = 8
IN_SHARD = IN_WIDTH // N_DEV
OUT_SHARD = D_MODEL // N_DEV

LANES = 128
TOKEN_TILE = 512
HALO = 16
Q_BLOCK = 256
KV_BLOCKS = 3
KV_WINDOW = KV_BLOCKS * Q_BLOCK
PAIR = 2 * HEAD_DIM
N_PAIRS = N_HEADS // 2
TOEPLITZ = 1024
VMEM_LIMIT = 56 * 1024 * 1024


def _params(sem=None, vmem=VMEM_LIMIT):
    return pltpu.CompilerParams(dimension_semantics=sem, vmem_limit_bytes=vmem)


def _sigmoid(x):
    return 1.0 / (1.0 + jnp.exp(-x))


def _nt(a, b):
    return lax.dot_general(a, b, (((1,), (1,)), ((), ())), preferred_element_type=F32)


def _tn(a, b):
    return lax.dot_general(a, b, (((0,), (0,)), ((), ())), preferred_element_type=F32)


def _nn(a, b):
    return jnp.dot(a, b, preferred_element_type=F32)


def _mesh_pos():
    return lax.axis_index("x"), lax.axis_index("y"), lax.axis_index("c")


def _dev_index(p):
    return 4 * p[0] + 2 * p[1] + p[2]


def _peer(r):
    x, y, c = _mesh_pos()
    return (x ^ ((r >> 2) & 1), y ^ ((r >> 1) & 1), c ^ (r & 1))


def _exchange_copies(src_hbm, land_hbm, send_sems, recv_sems, first_sem=0, same_for_all=False):
    return [
        pltpu.make_async_remote_copy(
            src_ref=src_hbm if same_for_all else src_hbm.at[_dev_index(_peer(r))], dst_ref=land_hbm.at[r - 1],
            send_sem=send_sems.at[first_sem + r - 1], recv_sem=recv_sems.at[first_sem + r - 1],
            device_id=_peer(r), device_id_type=MESH_ID)
        for r in range(1, N_DEV)
    ]


def _gather_weights(w_in_shard, rel_line):
    def body(win_ref, line_ref, gin_ref, bias_ref, sin_ref, send_sems, recv_sems):
        x, y, c = _mesh_pos()
        me, sibling = (x, y, c), (x, y, 1 - c)
        chips = [(1 - x, y), (x, 1 - y), (1 - x, 1 - y)]

        sin_ref[...] = win_ref[...].astype(BF16)
        gin_ref[_dev_index(me)] = sin_ref[...]

        def copy(k, block, to, from_shard=False):
            return pltpu.make_async_remote_copy(
                src_ref=sin_ref if from_shard else gin_ref.at[_dev_index(block)],
                dst_ref=gin_ref.at[_dev_index(block)],
                send_sem=send_sems.at[k],
                recv_sem=recv_sems.at[k],
                device_id=to,
                device_id_type=MESH_ID,
            )

        first = [copy(0, me, sibling, True)]
        first += [copy(1 + j, me, (*chip, c), True) for j, chip in enumerate(chips)]
        for cp in first:
            cp.start()
        passed = [copy(4 + j, (*chip, c), sibling) for j, chip in enumerate(chips)]

        def bias_heads(lo, hi):
            for h in range(lo, hi):
                bias_ref[h] = _toeplitz_bias(line_ref[h])

        bias_heads(0, N_HEADS - 3)
        for j, chip in enumerate(chips):
            copy(1 + j, (*chip, c), me).wait_recv()
            passed[j].start()
            bias_heads(N_HEADS - 3 + j, N_HEADS - 2 + j)
        copy(0, sibling, me).wait_recv()
        for j, chip in enumerate(chips):
            copy(4 + j, (*chip, 1 - c), me).wait_recv()
        for cp in first + passed:
            cp.wait_send()

    vm = pl.BlockSpec(memory_space=pltpu.VMEM)
    return pl.pallas_call(
        body,
        name="gather_weights",
        out_shape=(
            jax.ShapeDtypeStruct((N_DEV, D_MODEL, IN_SHARD), BF16),
            jax.ShapeDtypeStruct((N_HEADS, Q_BLOCK, KV_WINDOW), F32),
        ),
        in_specs=[vm, vm],
        out_specs=(vm, vm),
        scratch_shapes=[
            pltpu.VMEM((D_MODEL, IN_SHARD), BF16),
            pltpu.SemaphoreType.DMA((7,)),
            pltpu.SemaphoreType.DMA((7,)),
        ],
        compiler_params=_params(),
    )(w_in_shard, rel_line)


def _load_w_in(wg_hbm, wfull_ref, sem):
    copies = [
        pltpu.make_async_copy(wg_hbm.at[d], wfull_ref.at[:, d * IN_SHARD:(d + 1) * IN_SHARD], sem.at[d])
        for d in range(N_DEV)
    ]
    for cp in copies:
        cp.start()
    for cp in copies:
        cp.wait()


def _norm_inproj(x2d, g1, wg, pool_w, pool_scale):
    t = x2d.shape[0]

    def body(x_ref, g_ref, wg_hbm, pw_ref, ps_ref, pvg_ref, qkv_ref, ag_ref, yp_ref, wfull_ref, halo_ref, sem):
        i = pl.program_id(0)

        @pl.when(i == 0)
        def _():
            _load_w_in(wg_hbm, wfull_ref, sem)
            halo_ref[...] = jnp.zeros_like(halo_ref)

        xf = x_ref[...]
        r = lax.rsqrt(jnp.mean(xf * xf, axis=-1, keepdims=True) + EPS)
        h = ((xf * r) * g_ref[...]).astype(BF16)
        chunk = lambda ci: _nn(h, wfull_ref[:, ci * POOL_WIDTH:(ci + 1) * POOL_WIDTH])
        pv, pg = chunk(0), chunk(1)
        pvg_ref[:, :POOL_WIDTH] = pv
        pvg_ref[:, POOL_WIDTH:] = pg
        halo = halo_ref[...]
        halo_ref[...] = pv[TOKEN_TILE - HALO:]
        for gi in range(N_GROUPS):
            sl = slice(gi * GROUP_DIM, (gi + 1) * GROUP_DIM)
            d = _pool_diffs(pv, halo, i * TOKEN_TILE, gi)
            z = _nn(d.astype(BF16), pw_ref[gi].astype(BF16))
            g = pg[:, sl]
            yp_ref[:, sl] = ((z * ps_ref[:, sl]) * (g * _sigmoid(g))).astype(BF16)
        for ci in range(2, 5):
            qkv_ref[:, (ci - 2) * POOL_WIDTH:(ci - 1) * POOL_WIDTH] = chunk(ci).astype(BF16)
        ag_ref[...] = chunk(5)

    tile = lambda width: pl.BlockSpec((TOKEN_TILE, width), lambda i: (i, 0))
    return pl.pallas_call(
        body,
        name="norm_inproj",
        grid=(t // TOKEN_TILE,),
        out_shape=(
            jax.ShapeDtypeStruct((t, 2 * POOL_WIDTH), F32),
            jax.ShapeDtypeStruct((t, 3 * ATTN_WIDTH), BF16),
            jax.ShapeDtypeStruct((t, ATTN_WIDTH), F32),
            jax.ShapeDtypeStruct((t, POOL_WIDTH), BF16),
        ),
        in_specs=[
            tile(D_MODEL),
            pl.BlockSpec((1, D_MODEL), lambda i: (0, 0)),
            pl.BlockSpec(memory_space=pl.ANY),
            pl.BlockSpec((N_GROUPS, GROUP_DIM, GROUP_DIM), lambda i: (0, 0, 0)),
            pl.BlockSpec((1, POOL_WIDTH), lambda i: (0, 0)),
        ],
        out_specs=(tile(2 * POOL_WIDTH), tile(3 * ATTN_WIDTH), tile(ATTN_WIDTH), tile(POOL_WIDTH)),
        scratch_shapes=[
            pltpu.VMEM((D_MODEL, IN_WIDTH), BF16),
            pltpu.VMEM((HALO, POOL_WIDTH), F32),
            pltpu.SemaphoreType.DMA((N_DEV,)),
        ],
        compiler_params=_params(("arbitrary",)),
    )(x2d, g1, wg, pool_w, pool_scale)


def _inv_count(first_row, rows, window):
    tpos = first_row + lax.broadcasted_iota(jnp.int32, (rows, 1), 0)
    return 1.0 / jnp.minimum(tpos + 1, window).astype(F32)


def _causal_window_sum(ext, window):
    s, k = ext, 1
    while k < window:
        s = s + pltpu.roll(s, k, 0)
        k *= 2
    return s


def _pool_diffs(pv, halo, first_row, gi):
    w = POOL_WINDOWS[gi]
    sl = slice(gi * GROUP_DIM, (gi + 1) * GROUP_DIM)
    ext = jnp.concatenate([halo[:, sl], pv[:, sl]], axis=0)
    s = _causal_window_sum(ext, w)[HALO:]
    return s * _inv_count(first_row, pv.shape[0], w) - pv[:, sl]


def _pool_bwd_tile(i, n_tiles, cur_ref, prev_ref, pgn_ref, dy_ref, dyn_ref, pw_ref, ps_ref, dp_ref, dpw_ref, dps_ref):
    pv = cur_ref[:, :POOL_WIDTH]
    pg = cur_ref[:, POOL_WIDTH:]
    prev = jnp.where(i > 0, prev_ref[...], 0.0)
    has_next = i < n_tiles - 1
    rows = TOKEN_TILE + HALO
    for gi in range(N_GROUPS):
        w = POOL_WINDOWS[gi]
        sl = slice(gi * GROUP_DIM, (gi + 1) * GROUP_DIM)
        pw = pw_ref[gi].astype(BF16)
        ps = ps_ref[:, sl]
        d = _pool_diffs(pv, prev, i * TOKEN_TILE, gi).astype(BF16)
        z = _nn(d, pw)
        g_ext = jnp.concatenate([pg[:, sl], pgn_ref[:, sl]], axis=0)
        dy_ext = jnp.concatenate([dy_ref[:, sl], dyn_ref[:, sl]], axis=0).astype(F32)
        sig = _sigmoid(g_ext)
        gate = g_ext * sig
        dz_ext = ((dy_ext * gate) * ps).astype(BF16)
        dd_ext = _nt(dz_ext, pw)
        e = dd_ext * _inv_count(i * TOKEN_TILE, rows, w)
        row = lax.broadcasted_iota(jnp.int32, (rows, 1), 0)
        e = jnp.where(jnp.logical_or(row < TOKEN_TILE, has_next), e, 0.0)
        s, k = e, 1
        while k < w:
            s = s + pltpu.roll(s, rows - k, 0)
            k *= 2
        dp_ref[:, sl] = (s[:TOKEN_TILE] - dd_ext[:TOKEN_TILE]).astype(BF16)
        dy = dy_ext[:TOKEN_TILE]
        g = g_ext[:TOKEN_TILE]
        sg = sig[:TOKEN_TILE]
        dgate = sg * (1.0 + g * (1.0 - sg))
        dp_ref[:, POOL_WIDTH + gi * GROUP_DIM:POOL_WIDTH + (gi + 1) * GROUP_DIM] = (
            (dy * (z * ps)) * dgate).astype(BF16)
        dps_ref[:, sl] += jnp.sum((dy * gate[:TOKEN_TILE]) * z, axis=0, keepdims=True)
        dpw_ref[gi] += _tn(d, dz_ext[:TOKEN_TILE])


_REL_FIRST = KV_WINDOW - 1 - MAX_REL


def _skew_rows(a, right):
    rows, lanes = a.shape
    row = lax.broadcasted_iota(jnp.int32, a.shape, 0)
    for b in range(rows.bit_length() - 1):
        shift = (1 << b) if right else lanes - (1 << b)
        a = jnp.where((row >> b) & 1 == 1, pltpu.roll(a, shift, 1), a)
    return a


def _toeplitz_bias(line):
    a = jnp.broadcast_to(line, (Q_BLOCK, TOEPLITZ))
    a = _skew_rows(a, True)
    a = pltpu.roll(a, TOEPLITZ - (Q_BLOCK - 1), 1)
    a = a[:, :KV_WINDOW]
    qc = lax.broadcasted_iota(jnp.int32, a.shape, 0) // CHUNK
    kc = lax.broadcasted_iota(jnp.int32, a.shape, 1) // CHUNK
    visible = jnp.logical_and(kc >= qc, kc <= qc + LEFT_CHUNKS)
    return jnp.where(visible, a, MASK_VALUE)


_BAND_WIDTH = 2 * LANES
_BAND_START = (384, 512, 512, 512)
_BAND_REL = tuple(a - (KV_BLOCKS - 1) * Q_BLOCK - qc * CHUNK for qc, a in enumerate(_BAND_START))


_ROW_G1, _ROW_G2, _ROW_PS, _ROW_LOSS, _ROW_RB = 0, 1, 2, 3, 8
_VEC_ROWS = 16


def _pack_small(band, total, d_g2, d_ps, loss_row):
    def body(band_ref, total_ref, g2_ref, ps_ref, loss_ref, vec_ref):
        vec_ref[...] = jnp.zeros_like(vec_ref)
        vec_ref[_ROW_G2:_ROW_G2 + 1, :] = g2_ref[...]
        vec_ref[_ROW_PS:_ROW_PS + 1, :POOL_WIDTH] = ps_ref[...]
        vec_ref[_ROW_LOSS:_ROW_LOSS + 1, :LANES] = loss_ref[0:1, :]
        for h in range(N_HEADS):
            a = jnp.zeros((CHUNK, 2 * _BAND_WIDTH), F32)
            for qc in range(Q_BLOCK // CHUNK):
                z = jnp.concatenate([band_ref[h, qc], jnp.zeros((CHUNK, _BAND_WIDTH), F32)], axis=1)
                left = -MAX_REL - _BAND_REL[qc]
                a = a + (pltpu.roll(z, 2 * _BAND_WIDTH - left, 1) if left else z)
            a = _skew_rows(a, False)
            near = jnp.sum(a, axis=0, keepdims=True)[:, :2 * LANES]
            r = lax.broadcasted_iota(jnp.int32, near.shape, 1)
            near = jnp.where(jnp.logical_and(r >= 1, r < 2 * MAX_REL), near, 0.0)
            everything = jnp.sum(jnp.sum(total_ref[h], axis=0, keepdims=True), axis=1, keepdims=True)
            far = everything - jnp.sum(near, axis=1, keepdims=True)
            vec_ref[_ROW_RB + h:_ROW_RB + h + 1, :2 * LANES] = jnp.where(r == 0, far, near)

    vm = pl.BlockSpec(memory_space=pltpu.VMEM)
    return pl.pallas_call(
        body,
        name="pack_small",
        out_shape=jax.ShapeDtypeStruct((_VEC_ROWS, D_MODEL), F32),
        in_specs=[vm] * 5,
        out_specs=vm,
        compiler_params=_params(),
    )(band, total, d_g2, d_ps, loss_row)


def _head_lanes(hh):
    lane = lax.broadcasted_iota(jnp.int32, (1, PAIR), 1)
    return (lane < HEAD_DIM) if hh == 0 else (lane >= HEAD_DIM)


def _score_windows(nwin):
    if nwin < KV_BLOCKS:
        return [(slice(0, Q_BLOCK), slice(0, nwin * Q_BLOCK), None)]
    pieces = []
    for qc in range(Q_BLOCK // CHUNK):
        rows = slice(qc * CHUNK, (qc + 1) * CHUNK)
        if qc < 2:
            pieces.append((rows, slice(0, KV_WINDOW - LANES), slice(KV_WINDOW - LANES, KV_WINDOW)))
        else:
            pieces.append((rows, slice(LANES, KV_WINDOW), slice(0, LANES)))
    return pieces


def _block_rows(first_block, n_blocks):
    if isinstance(first_block, int):
        return pl.ds(first_block * Q_BLOCK, n_blocks * Q_BLOCK)
    return pl.ds(pl.multiple_of(first_block * Q_BLOCK, Q_BLOCK), n_blocks * Q_BLOCK)


def _attn_fwd(qkv, bias_tile, w_out_shard):
    t = qkv.shape[0]
    nb = t // Q_BLOCK

    def body(q_ref, k_ref, v_ref, bias_ref, wout_ref, o_ref, gout_hbm,
             s_scr, p_scr, sout_ref, send_sems, recv_sems, local_sem):
        pair = pl.program_id(0)
        me = _dev_index(_mesh_pos())
        mine = pltpu.make_async_copy(sout_ref, gout_hbm.at[me], local_sem)

        def shard_copy(r, block):
            return pltpu.make_async_remote_copy(
                src_ref=sout_ref, dst_ref=gout_hbm.at[block], send_sem=send_sems.at[r - 1],
                recv_sem=recv_sems.at[r - 1], device_id=_peer(r), device_id_type=MESH_ID)

        @pl.when(pair == 0)
        def _():
            sout_ref[...] = wout_ref[...].astype(BF16)
            mine.start()
            for r in range(1, N_DEV):
                shard_copy(r, me).start()

        def scores(i, slot, nwin):
            q = q_ref[_block_rows(i, 1), :]
            kw = k_ref[_block_rows(i + 1 - nwin, nwin), :]
            for hh in range(2):
                q_h = jnp.where(_head_lanes(hh), q, jnp.zeros_like(q)) * ATTN_SCALE
                s_scr[slot, hh, :, :nwin * Q_BLOCK] = _nt(q_h, kw)

        def softmax(slot, nwin):
            off = (KV_BLOCKS - nwin) * Q_BLOCK
            for hh in range(2):
                for rows, cols, rest in _score_windows(nwin):
                    s = s_scr[slot, hh, rows, cols] + bias_ref[hh, rows, off + cols.start:off + cols.stop]
                    p_scr[slot, hh, rows, cols] = jnp.exp(s - jnp.max(s, axis=-1, keepdims=True)).astype(BF16)
                    if rest is not None:
                        p_scr[slot, hh, rows, rest] = jnp.zeros((CHUNK, LANES), BF16)

        def output(i, slot, nwin):
            vw = v_ref[_block_rows(i + 1 - nwin, nwin), :]
            outs = [_nn(p_scr[slot, hh, :, :nwin * Q_BLOCK], jnp.where(_head_lanes(hh), vw, jnp.ones_like(vw)))
                    for hh in range(2)]
            sums = pltpu.roll(jnp.where(_head_lanes(0), outs[1], outs[0]), HEAD_DIM, axis=1)
            o_ref[_block_rows(i, 1), :] = (jnp.where(_head_lanes(0), outs[0], outs[1]) * (1.0 / sums)).astype(BF16)

        scores(0, 0, 1)
        scores(1, 1, 2)
        softmax(0, 1)
        scores(2, 0, 3)
        softmax(1, 2)
        output(0, 0, 1)
        scores(3, 1, 3)
        softmax(0, 3)
        output(1, 1, 2)

        def two_steps(k, carry):
            i = 3 + 2 * k
            scores(i + 1, 0, KV_BLOCKS)
            softmax(1, KV_BLOCKS)
            output(i - 1, 0, KV_BLOCKS)
            scores(i + 2, 1, KV_BLOCKS)
            softmax(0, KV_BLOCKS)
            output(i, 1, KV_BLOCKS)
            return carry

        lax.fori_loop(0, (nb - 4) // 2, two_steps, 0)
        last = (nb - 1) % 2
        softmax(last, KV_BLOCKS)
        output(nb - 2, 1 - last, KV_BLOCKS)
        output(nb - 1, last, KV_BLOCKS)

        @pl.when(pair == N_PAIRS - 1)
        def _():
            for r in range(1, N_DEV):
                shard_copy(r, _dev_index(_peer(r))).wait_recv()
            for r in range(1, N_DEV):
                shard_copy(r, me).wait_send()
            mine.wait()

    col = lambda c0: pl.BlockSpec((t, PAIR), lambda j: (0, c0 + j))
    return pl.pallas_call(
        body,
        name="attn_fwd",
        grid=(N_PAIRS,),
        out_shape=(
            jax.ShapeDtypeStruct((t, ATTN_WIDTH), BF16),
            jax.ShapeDtypeStruct((N_DEV, OUT_SHARD, D_MODEL), BF16),
        ),
        in_specs=[col(0), col(N_PAIRS), col(2 * N_PAIRS),
                  pl.BlockSpec((2, Q_BLOCK, KV_WINDOW), lambda j: (j, 0, 0)),
                  pl.BlockSpec((OUT_SHARD, D_MODEL), lambda j: (0, 0))],
        out_specs=(col(0), pl.BlockSpec(memory_space=pl.ANY)),
        scratch_shapes=[
            pltpu.VMEM((2, 2, Q_BLOCK, KV_WINDOW), F32),
            pltpu.VMEM((2, 2, Q_BLOCK, KV_WINDOW), BF16),
            pltpu.VMEM((OUT_SHARD, D_MODEL), BF16),
            pltpu.SemaphoreType.DMA((N_DEV - 1,)),
            pltpu.SemaphoreType.DMA((N_DEV - 1,)),
            pltpu.SemaphoreType.DMA,
        ],
        compiler_params=_params(("arbitrary",)),
    )(qkv, qkv, qkv, bias_tile, w_out_shard)


def _attn_bwd(qkv, do, bias_tile, dwout_g):
    t = qkv.shape[0]
    nb = t // Q_BLOCK

    def body(q_ref, k_ref, v_ref, do_ref, bias_ref, dwout_hbm,
             dq_ref, dk_ref, dv_ref, band_ref, total_ref, land_hbm,
             s_scr, dp_scr, p_scr, dsb_scr, dq_scr, dk_acc, dv_acc, send_sems, recv_sems):
        exchange = _exchange_copies(dwout_hbm, land_hbm, send_sems, recv_sems)

        @pl.when(pl.program_id(0) == 0)
        def _():
            for cp in exchange:
                cp.start()

        band_ref[...] = jnp.zeros_like(band_ref)
        total_ref[...] = jnp.zeros_like(total_ref)

        def nwin_of(i):
            return min(i + 1, KV_BLOCKS) if isinstance(i, int) else KV_BLOCKS

        def operands(i, hh):
            lanes = _head_lanes(hh)
            q = q_ref[_block_rows(i, 1), :]
            do = do_ref[_block_rows(i, 1), :]
            return (jnp.where(lanes, q, jnp.zeros_like(q)) * ATTN_SCALE, jnp.where(lanes, do, jnp.zeros_like(do)))

        def products(i, hh):
            nwin = nwin_of(i)
            win = _block_rows(i + 1 - nwin, nwin)
            q_h, do_h = operands(i, hh)
            s_scr[hh, :, :nwin * Q_BLOCK] = _nt(q_h, k_ref[win, :])
            dp_scr[hh, :, :nwin * Q_BLOCK] = _nt(do_h, v_ref[win, :])

        def grads(i, hh):
            nwin = nwin_of(i)
            off = (KV_BLOCKS - nwin) * Q_BLOCK
            for rows, cols, rest in _score_windows(nwin):
                bias_cols = slice(off + cols.start, off + cols.stop)
                s = s_scr[hh, rows, cols] + bias_ref[hh, rows, bias_cols]
                e = jnp.exp(s - jnp.max(s, axis=-1, keepdims=True))
                p = e * (1.0 / jnp.sum(e, axis=-1, keepdims=True))
                dp = dp_scr[hh, rows, cols]
                ds = p * (dp - jnp.sum(p * dp, axis=-1, keepdims=True))
                total_ref[hh, rows, :] += sum(ds[:, c0:c0 + LANES] for c0 in range(0, ds.shape[1], LANES))
                for qc in range(rows.start // CHUNK, rows.stop // CHUNK):
                    lo = max(_BAND_START[qc], bias_cols.start)
                    hi = min(_BAND_START[qc] + _BAND_WIDTH, bias_cols.stop)
                    if lo < hi:
                        band_ref[hh, qc, :, lo - _BAND_START[qc]:hi - _BAND_START[qc]] += ds[
                            qc * CHUNK - rows.start:(qc + 1) * CHUNK - rows.start,
                            lo - bias_cols.start:hi - bias_cols.start]
                p_scr[hh, rows, cols] = p.astype(BF16)
                dsb_scr[hh, rows, cols] = ds.astype(BF16)
                if rest is not None:
                    p_scr[hh, rows, rest] = jnp.zeros((CHUNK, LANES), BF16)
                    dsb_scr[hh, rows, rest] = jnp.zeros((CHUNK, LANES), BF16)

        def ring(block):
            return block % KV_BLOCKS if isinstance(block, int) else lax.rem(block, KV_BLOCKS)

        def accumulate(i, hh):
            nwin = nwin_of(i)
            w = nwin * Q_BLOCK
            win = _block_rows(i + 1 - nwin, nwin)
            q_h, do_h = operands(i, hh)
            ds_b = dsb_scr[hh, :, :w]
            dq_h = _nn(ds_b, k_ref[win, :]) * ATTN_SCALE
            dkw = _tn(ds_b, q_h)
            dvw = _tn(p_scr[hh, :, :w], do_h)
            for b in range(nwin):
                slot = ring(i + 1 - nwin + b)
                part = slice(b * Q_BLOCK, (b + 1) * Q_BLOCK)
                if hh == 0 and b == nwin - 1:
                    dk_acc[slot] = dkw[part]
                    dv_acc[slot] = dvw[part]
                else:
                    dk_acc[slot] += dkw[part]
                    dv_acc[slot] += dvw[part]
            if hh == 0:
                dq_scr[...] = dq_h
            else:
                dq_ref[_block_rows(i, 1), :] = jnp.where(_head_lanes(0), dq_scr[...], dq_h).astype(BF16)
                if not (isinstance(i, int) and i < KV_BLOCKS - 1):
                    flush(i - (KV_BLOCKS - 1))

        def flush(block):
            dk_ref[_block_rows(block, 1), :] = dk_acc[ring(block)].astype(BF16)
            dv_ref[_block_rows(block, 1), :] = dv_acc[ring(block)].astype(BF16)

        def tile(n):
            return n // 2, n % 2

        def step(n):
            if n + 1 < 2 * nb:
                products(*tile(n + 1))
            grads(*tile(n))
            if n >= 1:
                accumulate(*tile(n - 1))

        products(0, 0)
        for n in range(2 * KV_BLOCKS):
            step(n)

        def two_steps(i, carry):
            products(i, 1)
            grads(i, 0)
            accumulate(i - 1, 1)
            products(i + 1, 0)
            grads(i, 1)
            accumulate(i, 0)
            return carry

        lax.fori_loop(KV_BLOCKS, nb - 1, two_steps, 0)
        step(2 * nb - 2)
        step(2 * nb - 1)
        accumulate(nb - 1, 1)
        flush(nb - 2)
        flush(nb - 1)

        @pl.when(pl.program_id(0) == N_PAIRS - 1)
        def _():
            for cp in exchange:
                cp.wait_recv()
            for cp in exchange:
                cp.wait_send()

    col = lambda c0: pl.BlockSpec((t, PAIR), lambda j: (0, c0 + j))
    tile_spec = pl.BlockSpec((2, Q_BLOCK, KV_WINDOW), lambda j: (j, 0, 0))
    out = jax.ShapeDtypeStruct((t, ATTN_WIDTH), BF16)
    return pl.pallas_call(
        body,
        name="attn_bwd",
        grid=(N_PAIRS,),
        out_shape=(out, out, out,
                   jax.ShapeDtypeStruct((N_HEADS, Q_BLOCK // CHUNK, CHUNK, _BAND_WIDTH), F32),
                   jax.ShapeDtypeStruct((N_HEADS, Q_BLOCK, LANES), F32),
                   jax.ShapeDtypeStruct((N_DEV - 1, OUT_SHARD, D_MODEL), BF16)),
        in_specs=[col(0), col(N_PAIRS), col(2 * N_PAIRS), col(0), tile_spec, pl.BlockSpec(memory_space=pl.ANY)],
        out_specs=(col(0), col(0), col(0),
                   pl.BlockSpec((2, Q_BLOCK // CHUNK, CHUNK, _BAND_WIDTH), lambda j: (j, 0, 0, 0)),
                   pl.BlockSpec((2, Q_BLOCK, LANES), lambda j: (j, 0, 0)),
                   pl.BlockSpec(memory_space=pl.ANY)),
        scratch_shapes=[
            pltpu.VMEM((2, Q_BLOCK, KV_WINDOW), F32),
            pltpu.VMEM((2, Q_BLOCK, KV_WINDOW), F32),
            pltpu.VMEM((2, Q_BLOCK, KV_WINDOW), BF16),
            pltpu.VMEM((2, Q_BLOCK, KV_WINDOW), BF16),
            pltpu.VMEM((Q_BLOCK, PAIR), F32),
            pltpu.VMEM((KV_BLOCKS, Q_BLOCK, PAIR), F32),
            pltpu.VMEM((KV_BLOCKS, Q_BLOCK, PAIR), F32),
            pltpu.SemaphoreType.DMA((N_DEV - 1,)),
            pltpu.SemaphoreType.DMA((N_DEV - 1,)),
        ],
        compiler_params=_params(("arbitrary",)),
    )(qkv, qkv, qkv, do, bias_tile, dwout_g)


def _outproj_loss(x2d, tgt2d, y_pool, o, ag, wout_g, g2):
    t = x2d.shape[0]
    n_tiles = t // TOKEN_TILE

    def body(x_ref, tgt_ref, yp_ref, o_ref, ag_ref, w_ref, g_ref,
             dx2_ref, dyp_ref, do_ref, dag_ref, dw_ref, dg_ref, loss_ref, acc_ref):
        i = pl.program_id(0)

        @pl.when(i == 0)
        def _():
            acc_ref[...] = jnp.zeros_like(acc_ref)
            dg_ref[...] = jnp.zeros_like(dg_ref)
            loss_ref[...] = jnp.zeros_like(loss_ref)

        w = w_ref[...].reshape(D_MODEL, D_MODEL)
        ga = ag_ref[...]
        sig = _sigmoid(ga)
        gate = ga * sig
        of = o_ref[...].astype(F32)
        y = jnp.concatenate([yp_ref[...], (of * gate).astype(BF16)], axis=1)
        x2 = x_ref[...] + _nn(y, w)
        r = lax.rsqrt(jnp.mean(x2 * x2, axis=-1, keepdims=True) + EPS)
        xh = x2 * r
        g = g_ref[...]
        diff = xh * g - tgt_ref[...]
        tok = jnp.sum(diff * diff, axis=-1, keepdims=True) * (1.0 / D_MODEL)
        loss_ref[...] += jnp.sum(tok, axis=0, keepdims=True)
        dout = diff * (1.0 / D_MODEL)
        dg_ref[...] += jnp.sum(dout * xh, axis=0, keepdims=True)
        u = dout * g
        dx2 = r * (u - xh * jnp.mean(u * xh, axis=-1, keepdims=True))
        dx2_ref[...] = dx2
        dx2_b = dx2.astype(BF16)
        dy = _nt(dx2_b, w)
        dyp_ref[...] = dy[:, :POOL_WIDTH].astype(BF16)
        dya = dy[:, POOL_WIDTH:]
        do_ref[...] = (dya * gate).astype(BF16)
        dag_ref[...] = ((dya * of) * (sig * (1.0 + ga * (1.0 - sig)))).astype(BF16)
        acc_ref[...] += _tn(y, dx2_b)

        @pl.when(i == n_tiles - 1)
        def _():
            dw_ref[...] = acc_ref[...].reshape(N_DEV, OUT_SHARD, D_MODEL).astype(BF16)

    tile = lambda width: pl.BlockSpec((TOKEN_TILE, width), lambda i: (i, 0))
    return pl.pallas_call(
        body,
        name="outproj_loss",
        grid=(n_tiles,),
        out_shape=(
            jax.ShapeDtypeStruct((t, D_MODEL), F32),
            jax.ShapeDtypeStruct((t, POOL_WIDTH), BF16),
            jax.ShapeDtypeStruct((t, ATTN_WIDTH), BF16),
            jax.ShapeDtypeStruct((t, ATTN_WIDTH), BF16),
            jax.ShapeDtypeStruct((N_DEV, OUT_SHARD, D_MODEL), BF16),
            jax.ShapeDtypeStruct((1, D_MODEL), F32),
            jax.ShapeDtypeStruct((8, LANES), F32),
        ),
        in_specs=[
            tile(D_MODEL), tile(D_MODEL), tile(POOL_WIDTH), tile(ATTN_WIDTH), tile(ATTN_WIDTH),
            pl.BlockSpec((N_DEV, OUT_SHARD, D_MODEL), lambda i: (0, 0, 0)),
            pl.BlockSpec((1, D_MODEL), lambda i: (0, 0)),
        ],
        out_specs=(
            tile(D_MODEL), tile(POOL_WIDTH), tile(ATTN_WIDTH), tile(ATTN_WIDTH),
            pl.BlockSpec((N_DEV, OUT_SHARD, D_MODEL), lambda i: (0, 0, 0)),
            pl.BlockSpec((1, D_MODEL), lambda i: (0, 0)),
            pl.BlockSpec((8, LANES), lambda i: (0, 0)),
        ),
        scratch_shapes=[pltpu.VMEM((D_MODEL, D_MODEL), F32)],
        compiler_params=_params(("arbitrary",)),
    )(x2d, tgt2d, y_pool, o, ag, wout_g, g2)


def _dproj_specs():
    tile = lambda width: pl.BlockSpec((TOKEN_TILE, width), lambda i: (i, 0))
    return [tile(2 * POOL_WIDTH)] + [tile(ATTN_WIDTH)] * 4


def _inproj_bwd_dx(x2d, dx2, dproj, g1, wg):
    t = x2d.shape[0]

    def body(x_ref, dx2_ref, dp_ref, dq_ref, dk_ref, dv_ref, dag_ref, g_ref, wg_hbm, gx_ref, dg_ref, wfull_ref, sem):
        @pl.when(pl.program_id(0) == 0)
        def _():
            _load_w_in(wg_hbm, wfull_ref, sem)
            dg_ref[...] = jnp.zeros_like(dg_ref)

        dproj_t = jnp.concatenate([dp_ref[...], dq_ref[...], dk_ref[...], dv_ref[...], dag_ref[...]], axis=1)
        dh = _nt(dproj_t, wfull_ref[...])
        xf = x_ref[...]
        r = lax.rsqrt(jnp.mean(xf * xf, axis=-1, keepdims=True) + EPS)
        xh = xf * r
        dg_ref[...] += jnp.sum(dh * xh, axis=0, keepdims=True)
        u = dh * g_ref[...]
        gx_ref[...] = dx2_ref[...] + r * (u - xh * jnp.mean(u * xh, axis=-1, keepdims=True))

    tile = pl.BlockSpec((TOKEN_TILE, D_MODEL), lambda i: (i, 0))
    return pl.pallas_call(
        body,
        name="inproj_bwd_dx",
        grid=(t // TOKEN_TILE,),
        out_shape=(jax.ShapeDtypeStruct((t, D_MODEL), F32), jax.ShapeDtypeStruct((1, D_MODEL), F32)),
        in_specs=[tile, tile] + _dproj_specs() + [
            pl.BlockSpec((1, D_MODEL), lambda i: (0, 0)),
            pl.BlockSpec(memory_space=pl.ANY),
        ],
        out_specs=(tile, pl.BlockSpec((1, D_MODEL), lambda i: (0, 0))),
        scratch_shapes=[pltpu.VMEM((D_MODEL, IN_WIDTH), BF16), pltpu.SemaphoreType.DMA((N_DEV,))],
        compiler_params=_params(("arbitrary",)),
    )(x2d, dx2, *dproj, g1, wg)


def _inproj_bwd_dw(x2d, pvg, dy_pool, dattn, g1, pool_w, pool_scale):
    t = x2d.shape[0]
    n_tiles = t // TOKEN_TILE
    halo_per_tile = TOKEN_TILE // HALO
    last_halo = t // HALO - 1

    def body(x_ref, cur_ref, prev_ref, pgn_ref, dy_ref, dyn_ref, dq_ref, dk_ref, dv_ref, dag_ref, g_ref, pw_ref, ps_ref,
             out_ref, dp_ref, dpw_ref, dps_ref, acc_ref):
        i = pl.program_id(0)

        @pl.when(i == 0)
        def _():
            acc_ref[...] = jnp.zeros_like(acc_ref)
            dpw_ref[...] = jnp.zeros_like(dpw_ref)
            dps_ref[...] = jnp.zeros_like(dps_ref)

        xf = x_ref[...]
        r = lax.rsqrt(jnp.mean(xf * xf, axis=-1, keepdims=True) + EPS)
        h = ((xf * r) * g_ref[...]).astype(BF16)
        col = 2 * POOL_WIDTH
        for ref in (dq_ref, dk_ref, dv_ref, dag_ref):
            acc_ref[:, col:col + ATTN_WIDTH] += _tn(h, ref[...])
            col += ATTN_WIDTH
        _pool_bwd_tile(i, n_tiles, cur_ref, prev_ref, pgn_ref, dy_ref, dyn_ref, pw_ref, ps_ref, dp_ref, dpw_ref, dps_ref)
        for c0 in (0, POOL_WIDTH):
            acc_ref[:, c0:c0 + POOL_WIDTH] += _tn(h, dp_ref[:, c0:c0 + POOL_WIDTH])

        @pl.when(i == n_tiles - 1)
        def _():
            for d in range(N_DEV):
                out_ref[d] = acc_ref[:, d * IN_SHARD:(d + 1) * IN_SHARD].astype(BF16)

    tile = lambda width: pl.BlockSpec((TOKEN_TILE, width), lambda i: (i, 0))
    next_halo = lambda col: pl.BlockSpec(
        (HALO, POOL_WIDTH), lambda i: (jnp.minimum((i + 1) * halo_per_tile, last_halo), col))
    return pl.pallas_call(
        body,
        name="inproj_bwd_dw",
        grid=(n_tiles,),
        out_shape=(
            jax.ShapeDtypeStruct((N_DEV, D_MODEL, IN_SHARD), BF16),
            jax.ShapeDtypeStruct((t, 2 * POOL_WIDTH), BF16),
            jax.ShapeDtypeStruct((N_GROUPS, GROUP_DIM, GROUP_DIM), F32),
            jax.ShapeDtypeStruct((1, POOL_WIDTH), F32),
        ),
        in_specs=[
            tile(D_MODEL),
            tile(2 * POOL_WIDTH),
            pl.BlockSpec((HALO, POOL_WIDTH), lambda i: (jnp.maximum(i * halo_per_tile - 1, 0), 0)),
            next_halo(1),
            tile(POOL_WIDTH),
            next_halo(0),
            tile(ATTN_WIDTH), tile(ATTN_WIDTH), tile(ATTN_WIDTH), tile(ATTN_WIDTH),
            pl.BlockSpec((1, D_MODEL), lambda i: (0, 0)),
            pl.BlockSpec((N_GROUPS, GROUP_DIM, GROUP_DIM), lambda i: (0, 0, 0)),
            pl.BlockSpec((1, POOL_WIDTH), lambda i: (0, 0)),
        ],
        out_specs=(
            pl.BlockSpec((N_DEV, D_MODEL, IN_SHARD), lambda i: (0, 0, 0)),
            tile(2 * POOL_WIDTH),
            pl.BlockSpec((N_GROUPS, GROUP_DIM, GROUP_DIM), lambda i: (0, 0, 0)),
            pl.BlockSpec((1, POOL_WIDTH), lambda i: (0, 0)),
        ),
        scratch_shapes=[pltpu.VMEM((D_MODEL, IN_WIDTH), F32)],
        compiler_params=_params(("arbitrary",)),
    )(x2d, pvg, pvg, pvg, dy_pool, dy_pool, *dattn, g1, pool_w, pool_scale)


_HBM = pl.BlockSpec(memory_space=pltpu.HBM)
_SEM = pl.BlockSpec(memory_space=pltpu.SEMAPHORE)
_DATAFLOW = pltpu.SideEffectType.DATAFLOW_SIDE_EFFECTING


_N_EXCHANGED = 3


def _exchange_all(refs, send_sems, recv_sems):
    win_hbm, win_land, pw_hbm, pw_land, vec_hbm, vec_land = refs
    return (_exchange_copies(win_hbm, win_land, send_sems, recv_sems)
            + _exchange_copies(pw_hbm, pw_land, send_sems, recv_sems, first_sem=N_DEV - 1)
            + _exchange_copies(vec_hbm, vec_land, send_sems, recv_sems, first_sem=2 * (N_DEV - 1), same_for_all=True))


def _exchange_start(win_blocks, pw_blocks, vec):
    arrays = []
    for a, land_shape in ((win_blocks, (N_DEV - 1,) + win_blocks.shape[1:]),
                          (pw_blocks, (N_DEV - 1,) + pw_blocks.shape[1:]),
                          (vec, (N_DEV - 1,) + vec.shape)):
        arrays += [pltpu.with_memory_space_constraint(a, pltpu.HBM),
                   pltpu.with_memory_space_constraint(lax.empty(land_shape, a.dtype), pltpu.HBM)]

    def body(*refs):
        ins, (send_sems, recv_sems), token = refs[:2 * _N_EXCHANGED], refs[2 * _N_EXCHANGED:2 * _N_EXCHANGED + 2], refs[-1]
        for cp in _exchange_all(ins, send_sems, recv_sems):
            cp.start()
        token[...] = jnp.zeros_like(token)

    sems = pltpu.SemaphoreType.DMA((_N_EXCHANGED * (N_DEV - 1),))
    return pl.pallas_call(
        body,
        name="exchange_start",
        out_shape=(sems, sems, *[pltpu.HBM(a.shape, a.dtype) for a in arrays], jax.ShapeDtypeStruct((8, LANES), F32)),
        in_specs=tuple([_HBM] * len(arrays)),
        out_specs=(_SEM, _SEM, *[_HBM] * len(arrays), pl.BlockSpec(memory_space=pltpu.VMEM)),
        input_output_aliases={k: 2 + k for k in range(len(arrays))},
        compiler_params=pltpu.CompilerParams(has_side_effects=_DATAFLOW),
    )(*arrays)


def _exchange_wait(send_sems, recv_sems, arrays, after):
    def body(*refs):
        ins = refs[:2 * _N_EXCHANGED]
        send_sems, recv_sems = refs[2 * _N_EXCHANGED:2 * _N_EXCHANGED + 2]
        for cp in _exchange_all(ins, send_sems, recv_sems):
            cp.wait_send()
            cp.wait_recv()

    return pl.pallas_call(
        body,
        name="exchange_wait",
        out_shape=tuple(pltpu.HBM(a.shape, a.dtype) for a in arrays),
        in_specs=(*[_HBM] * len(arrays), _SEM, _SEM, pl.BlockSpec(memory_space=pl.ANY)),
        out_specs=tuple([_HBM] * len(arrays)),
        input_output_aliases={k: k for k in range(len(arrays))},
        compiler_params=pltpu.CompilerParams(has_side_effects=_DATAFLOW),
    )(*arrays, send_sems, recv_sems, after)


def _adamw(w, g, m, v):
    m = ADAM_B1 * m + (1.0 - ADAM_B1) * g
    v = ADAM_B2 * v + (1.0 - ADAM_B2) * (g * g)
    m_hat = m / (1.0 - ADAM_B1 ** ADAM_STEP)
    v_hat = v / (1.0 - ADAM_B2 ** ADAM_STEP)
    delta = -ADAM_LR * (m_hat / (jnp.sqrt(v_hat) + ADAM_EPS) + ADAM_WD * w)
    return delta, m, v


def _small_allreduce(d_g1, vec, vec_land, pw_blocks, pw_land):
    def body(g1_ref, vec_ref, vland_ref, pwb_ref, pland_ref, vec_out, pw_out, vparts_ref, pparts_ref, rows_ref,
             slice_ref, send_sems, recv_sems):
        me = _dev_index(_mesh_pos())

        def from_devices(parts_ref, own, land_ref):
            parts_ref[0] = own
            parts_ref[1:] = land_ref[...]
            total = parts_ref[me]
            for s in range(1, N_DEV):
                total = total + parts_ref[me ^ s]
            return total

        slice_ref[...] = from_devices(pparts_ref, pwb_ref[me], pland_ref)

        def send(r, src, dst, k):
            return pltpu.make_async_remote_copy(
                src_ref=src, dst_ref=dst, send_sem=send_sems.at[2 * (r - 1) + k],
                recv_sem=recv_sems.at[2 * (r - 1) + k], device_id=_peer(r), device_id_type=MESH_ID)

        started = [cp for r in range(1, N_DEV)
                   for cp in (send(r, g1_ref, rows_ref.at[r], 0), send(r, slice_ref, pw_out.at[me], 1))]
        for cp in started:
            cp.start()
        rows_ref[0] = g1_ref[...]
        pw_out[me] = slice_ref[...]
        vec_out[...] = from_devices(vparts_ref, vec_ref[...], vland_ref)
        for cp in started:
            cp.wait_recv()
        for cp in started:
            cp.wait_send()
        g1 = rows_ref[me]
        for s in range(1, N_DEV):
            g1 = g1 + rows_ref[me ^ s]
        vec_out[_ROW_G1:_ROW_G1 + 1, :] = g1

    vm = pl.BlockSpec(memory_space=pltpu.VMEM)
    return pl.pallas_call(
        body,
        name="small_allreduce",
        out_shape=(jax.ShapeDtypeStruct((_VEC_ROWS, D_MODEL), F32),
                   jax.ShapeDtypeStruct((N_DEV, GROUP_DIM // 2, GROUP_DIM), F32)),
        in_specs=[vm] * 5,
        out_specs=(vm, vm),
        scratch_shapes=[
            pltpu.VMEM((N_DEV, _VEC_ROWS, D_MODEL), F32),
            pltpu.VMEM((N_DEV, GROUP_DIM // 2, GROUP_DIM), F32),
            pltpu.VMEM((N_DEV, 1, D_MODEL), F32),
            pltpu.VMEM((GROUP_DIM // 2, GROUP_DIM), F32),
            pltpu.SemaphoreType.DMA((2 * (N_DEV - 1),)),
            pltpu.SemaphoreType.DMA((2 * (N_DEV - 1),)),
        ],
        compiler_params=_params(),
    )(d_g1, vec, vec_land, pw_blocks, pw_land)


def _adamw_all(dwin_g, land_in, dwout_g, land_out, vec_sum, pw_sum, weights, big):
    small_shapes = [(1, D_MODEL), (1, D_MODEL), (1, POOL_WIDTH), (N_HEADS, 2 * LANES), (N_GROUPS, GROUP_DIM, GROUP_DIM)]

    def body(*refs):
        refs = list(refs)
        take = lambda n: [refs.pop(0) for _ in range(n)]
        dwin_hbm, lin_ref, dwout_hbm, lout_ref, vec_ref, pw_ref = take(6)
        small_wmv = [take(3) for _ in range(5)]
        big_wmv = [take(3) for _ in range(2)]
        big_out = [take(4) for _ in range(2)]
        small_out = [take(4) for _ in range(5)]
        own_in, own_out, local_sems = refs

        me = _dev_index(_mesh_pos())
        mine = [pltpu.make_async_copy(dwin_hbm.at[me], own_in, local_sems.at[0]),
                pltpu.make_async_copy(dwout_hbm.at[me], own_out, local_sems.at[1])]
        for cp in mine:
            cp.start()

        def update(g, wmv, outs):
            delta, m_new, v_new = _adamw(wmv[0][...], g, wmv[1][...], wmv[2][...])
            for ref, val in zip(outs, (g, delta, m_new, v_new)):
                ref[...] = val

        update(vec_ref[_ROW_G1:_ROW_G1 + 1, :], small_wmv[0], small_out[0])
        update(vec_ref[_ROW_G2:_ROW_G2 + 1, :], small_wmv[1], small_out[1])
        update(vec_ref[_ROW_PS:_ROW_PS + 1, :POOL_WIDTH], small_wmv[2], small_out[2])
        update(vec_ref[_ROW_RB:_ROW_RB + N_HEADS, :2 * LANES], small_wmv[3], small_out[3])
        update(pw_ref[...], small_wmv[4], small_out[4])
        for cp in mine:
            cp.wait()
        g_in = own_in[...].astype(F32)
        g_out = own_out[...].astype(F32)
        for r in range(N_DEV - 1):
            g_in = g_in + lin_ref[r].astype(F32)
            g_out = g_out + lout_ref[r].astype(F32)
        update(g_in, big_wmv[0], big_out[0])
        update(g_out, big_wmv[1], big_out[1])

    vm = pl.BlockSpec(memory_space=pltpu.VMEM)
    hbm = pl.BlockSpec(memory_space=pl.ANY)
    f32 = lambda shape: jax.ShapeDtypeStruct(shape, F32)
    out_shapes = [f32((D_MODEL, IN_SHARD))] * 4 + [f32((OUT_SHARD, D_MODEL))] * 4
    for shape in small_shapes:
        out_shapes += [f32(shape)] * 4
    args = [dwin_g, land_in, dwout_g, land_out, vec_sum, pw_sum]
    for wmv in weights:
        args += list(wmv)
    for wmv in big:
        args += list(wmv)
    return pl.pallas_call(
        body,
        name="adamw_all",
        out_shape=tuple(out_shapes),
        in_specs=[hbm, vm, hbm, vm] + [vm] * (len(args) - 4),
        out_specs=tuple([vm] * len(out_shapes)),
        scratch_shapes=[
            pltpu.VMEM((D_MODEL, IN_SHARD), BF16),
            pltpu.VMEM((OUT_SHARD, D_MODEL), BF16),
            pltpu.SemaphoreType.DMA((2,)),
        ],
        compiler_params=_params(),
    )(*args)


def kernel(x, norm_gain, w_in, pool_w, pool_scale, rel_bias, w_out, final_norm_gain, loss_target, m_norm_gain, m_w_in, m_pool_w, m_pool_scale, m_rel_bias, m_w_out, m_final_norm_gain, v_norm_gain, v_w_in, v_pool_w, v_pool_scale, v_rel_bias, v_w_out, v_final_norm_gain):
    t = x.shape[1]
    assert x.shape[0] == 1 and t % TOKEN_TILE == 0 and t // Q_BLOCK >= 4
    x2d = x[0]
    tgt2d = loss_target[0]
    g2 = final_norm_gain.reshape(1, D_MODEL)

    rb = rel_bias[0]
    rel_line = jnp.concatenate([
        jnp.broadcast_to(rb[:, :1], (N_HEADS, _REL_FIRST)), rb,
        jnp.broadcast_to(rb[:, N_REL - 1:], (N_HEADS, TOEPLITZ - _REL_FIRST - N_REL)),
    ], axis=1).reshape(N_HEADS, 1, TOEPLITZ)
    wg_in, bias_tile = _gather_weights(w_in[0], rel_line)

    pvg, qkv, ag, y_pool = _norm_inproj(x2d, norm_gain, wg_in, pool_w[0], pool_scale)
    o, wg_out = _attn_fwd(qkv, bias_tile, w_out[0])
    dx2, dy_pool, do, dag, dwout_g, d_g2, loss_sum = _outproj_loss(x2d, tgt2d, y_pool, o, ag, wg_out, g2)
    dq, dk, dv, ds_band, ds_total, land_out = _attn_bwd(qkv, do, bias_tile, dwout_g)
    dwin_g, d_pool, d_pw, d_ps = _inproj_bwd_dw(x2d, pvg, dy_pool, (dq, dk, dv, dag), norm_gain, pool_w[0], pool_scale)
    vec = _pack_small(ds_band, ds_total, d_g2, d_ps, loss_sum)
    dproj = (d_pool, dq, dk, dv, dag)
    pw_blocks = d_pw.reshape(N_DEV, GROUP_DIM // 2, GROUP_DIM)
    send_sems, recv_sems, *exchanged, token = _exchange_start(dwin_g, pw_blocks, vec)
    grad_x, d_g1 = _inproj_bwd_dx(x2d, dx2, dproj, norm_gain + token[:1, :1], wg_in)
    dwin_g, land_in, pw_blocks, pw_land, vec, vec_land = _exchange_wait(send_sems, recv_sems, exchanged, d_g1)

    pad_rb = lambda a: jnp.pad(a[0], ((0, 0), (0, 2 * LANES - N_REL)))
    row = lambda a: a.reshape(1, D_MODEL)
    weights = [
        (norm_gain, m_norm_gain, v_norm_gain),
        (row(final_norm_gain), row(m_final_norm_gain), row(v_final_norm_gain)),
        (pool_scale, m_pool_scale, v_pool_scale),
        (pad_rb(rel_bias), pad_rb(m_rel_bias), pad_rb(v_rel_bias)),
        (pool_w[0], m_pool_w[0], v_pool_w[0]),
    ]
    big = [(w_in[0], m_w_in[0], v_w_in[0]), (w_out[0], m_w_out[0], v_w_out[0])]
    vec_sum, pw_sum = _small_allreduce(d_g1, vec, vec_land, pw_blocks, pw_land)
    res = _adamw_all(dwin_g, land_in, dwout_g, land_out, vec_sum, pw_sum.reshape(N_GROUPS, GROUP_DIM, GROUP_DIM),
                     weights, big)
    loss = 0.5 * vec_sum[_ROW_LOSS, 0]

    def leaves(k):
        g1_, g2_, ps_, rb_, pw_ = (res[8 + 4 * leaf + k] for leaf in range(5))
        return [g1_, res[k][None], pw_[None], ps_, rb_[None, :, :N_REL], res[4 + k][None], g2_.reshape(D_MODEL)]

    return (loss, grad_x[None], *leaves(0), *leaves(1), *leaves(2), *leaves(3))
```

```python
import math

import jax
import jax.numpy as jnp
from jax import lax
from jax.experimental import pallas as pl
from jax.experimental.pallas import tpu as pltpu

F32 = jnp.float32
BF16 = jnp.bfloat16
MESH_ID = pl.DeviceIdType.MESH

D_MODEL = 1024
POOL_WIDTH = 512
ATTN_WIDTH = 512
POOL_WINDOWS = (2, 4, 8, 16)
N_GROUPS = 4
GROUP_DIM = 128
HEAD_DIM = 64
N_HEADS = 8
CHUNK = 64
LEFT_CHUNKS = 8
MAX_REL = 64
N_REL = 2 * MAX_REL + 1
IN_WIDTH = 2 * POOL_WIDTH + 4 * ATTN_WIDTH
EPS = 1e-6
MASK_VALUE = -1e30
ATTN_SCALE = 1.0 / math.sqrt(HEAD_DIM)
ADAM_LR = 0.001
ADAM_B1 = 0.9
ADAM_B2 = 0.999
ADAM_EPS = 1e-08
ADAM_WD = 0.01
ADAM_STEP = 10

N_DEV = 8
IN_SHARD = IN_WIDTH // N_DEV
OUT_SHARD = D_MODEL // N_DEV

LANES = 128
TOKEN_TILE = 512
HALO = 16
Q_BLOCK = 256
KV_BLOCKS = 3
KV_WINDOW = KV_BLOCKS * Q_BLOCK
PAIR = 2 * HEAD_DIM
N_PAIRS = N_HEADS // 2
TOEPLITZ = 1024
VMEM_LIMIT = 56 * 1024 * 1024


def _params(sem=None, vmem=VMEM_LIMIT):
    return pltpu.CompilerParams(dimension_semantics=sem, vmem_limit_bytes=vmem)


def _sigmoid(x):
    return 1.0 / (1.0 + jnp.exp(-x))


def _nt(a, b):
    return lax.dot_general(a, b, (((1,), (1,)), ((), ())), preferred_element_type=F32)


def _tn(a, b):
    return lax.dot_general(a, b, (((0,), (0,)), ((), ())), preferred_element_type=F32)


def _nn(a, b):
    return jnp.dot(a, b, preferred_element_type=F32)


def _mesh_pos():
    return lax.axis_index("x"), lax.axis_index("y"), lax.axis_index("c")


def _dev_index(p):
    return 4 * p[0] + 2 * p[1] + p[2]


def _peer(r):
    x, y, c = _mesh_pos()
    return (x ^ ((r >> 2) & 1), y ^ ((r >> 1) & 1), c ^ (r & 1))


def _exchange_copies(src_hbm, land_hbm, send_sems, recv_sems, first_sem=0, same_for_all=False):
    return [
        pltpu.make_async_remote_copy(
            src_ref=src_hbm if same_for_all else src_hbm.at[_dev_index(_peer(r))], dst_ref=land_hbm.at[r - 1],
            send_sem=send_sems.at[first_sem + r - 1], recv_sem=recv_sems.at[first_sem + r - 1],
            device_id=_peer(r), device_id_type=MESH_ID)
        for r in range(1, N_DEV)
    ]


def _gather_weights(w_in_shard, rel_line):
    def body(win_ref, line_ref, gin_ref, bias_ref, sin_ref, send_sems, recv_sems):
        x, y, c = _mesh_pos()
        me, sibling = (x, y, c), (x, y, 1 - c)
        chips = [(1 - x, y), (x, 1 - y), (1 - x, 1 - y)]

        sin_ref[...] = win_ref[...].astype(BF16)
        gin_ref[_dev_index(me)] = sin_ref[...]

        def copy(k, block, to, from_shard=False):
            return pltpu.make_async_remote_copy(
                src_ref=sin_ref if from_shard else gin_ref.at[_dev_index(block)],
                dst_ref=gin_ref.at[_dev_index(block)],
                send_sem=send_sems.at[k],
                recv_sem=recv_sems.at[k],
                device_id=to,
                device_id_type=MESH_ID,
            )

        first = [copy(0, me, sibling, True)]
        first += [copy(1 + j, me, (*chip, c), True) for j, chip in enumerate(chips)]
        for cp in first:
            cp.start()
        passed = [copy(4 + j, (*chip, c), sibling) for j, chip in enumerate(chips)]

        def bias_heads(lo, hi):
            for h in range(lo, hi):
                bias_ref[h] = _toeplitz_bias(line_ref[h])

        bias_heads(0, N_HEADS - 3)
        for j, chip in enumerate(chips):
            copy(1 + j, (*chip, c), me).wait_recv()
            passed[j].start()
            bias_heads(N_HEADS - 3 + j, N_HEADS - 2 + j)
        copy(0, sibling, me).wait_recv()
        for j, chip in enumerate(chips):
            copy(4 + j, (*chip, 1 - c), me).wait_recv()
        for cp in first + passed:
            cp.wait_send()

    vm = pl.BlockSpec(memory_space=pltpu.VMEM)
    return pl.pallas_call(
        body,
        name="gather_weights",
        out_shape=(
            jax.ShapeDtypeStruct((N_DEV, D_MODEL, IN_SHARD), BF16),
            jax.ShapeDtypeStruct((N_HEADS, Q_BLOCK, KV_WINDOW), F32),
        ),
        in_specs=[vm, vm],
        out_specs=(vm, vm),
        scratch_shapes=[
            pltpu.VMEM((D_MODEL, IN_SHARD), BF16),
            pltpu.SemaphoreType.DMA((7,)),
            pltpu.SemaphoreType.DMA((7,)),
        ],
        compiler_params=_params(),
    )(w_in_shard, rel_line)


def _load_w_in(wg_hbm, wfull_ref, sem):
    copies = [
        pltpu.make_async_copy(wg_hbm.at[d], wfull_ref.at[:, d * IN_SHARD:(d + 1) * IN_SHARD], sem.at[d])
        for d in range(N_DEV)
    ]
    for cp in copies:
        cp.start()
    for cp in copies:
        cp.wait()


def _norm_inproj(x2d, g1, wg, pool_w, pool_scale):
    t = x2d.shape[0]

    def body(x_ref, g_ref, wg_hbm, pw_ref, ps_ref, pvg_ref, qkv_ref, ag_ref, yp_ref, wfull_ref, halo_ref, sem):
        i = pl.program_id(0)

        @pl.when(i == 0)
        def _():
            _load_w_in(wg_hbm, wfull_ref, sem)
            halo_ref[...] = jnp.zeros_like(halo_ref)

        xf = x_ref[...]
        r = lax.rsqrt(jnp.mean(xf * xf, axis=-1, keepdims=True) + EPS)
        h = ((xf * r) * g_ref[...]).astype(BF16)
        chunk = lambda ci: _nn(h, wfull_ref[:, ci * POOL_WIDTH:(ci + 1) * POOL_WIDTH])
        pv, pg = chunk(0), chunk(1)
        pvg_ref[:, :POOL_WIDTH] = pv
        pvg_ref[:, POOL_WIDTH:] = pg
        halo = halo_ref[...]
        halo_ref[...] = pv[TOKEN_TILE - HALO:]
        for gi in range(N_GROUPS):
            sl = slice(gi * GROUP_DIM, (gi + 1) * GROUP_DIM)
            d = _pool_diffs(pv[:, sl], halo[:, sl], i * TOKEN_TILE, POOL_WINDOWS[gi])
            z = _nn(d.astype(BF16), pw_ref[gi].astype(BF16))
            g = pg[:, sl]
            yp_ref[:, sl] = ((z * ps_ref[:, sl]) * (g * _sigmoid(g))).astype(BF16)
        for ci in range(2, 5):
            qkv_ref[:, (ci - 2) * POOL_WIDTH:(ci - 1) * POOL_WIDTH] = chunk(ci).astype(BF16)
        ag_ref[...] = chunk(5)

    tile = lambda width: pl.BlockSpec((TOKEN_TILE, width), lambda i: (i, 0))
    return pl.pallas_call(
        body,
        name="norm_inproj",
        grid=(t // TOKEN_TILE,),
        out_shape=(
            jax.ShapeDtypeStruct((t, 2 * POOL_WIDTH), F32),
            jax.ShapeDtypeStruct((t, 3 * ATTN_WIDTH), BF16),
            jax.ShapeDtypeStruct((t, ATTN_WIDTH), F32),
            jax.ShapeDtypeStruct((t, POOL_WIDTH), BF16),
        ),
        in_specs=[
            tile(D_MODEL),
            pl.BlockSpec((1, D_MODEL), lambda i: (0, 0)),
            pl.BlockSpec(memory_space=pl.ANY),
            pl.BlockSpec((N_GROUPS, GROUP_DIM, GROUP_DIM), lambda i: (0, 0, 0)),
            pl.BlockSpec((1, POOL_WIDTH), lambda i: (0, 0)),
        ],
        out_specs=(tile(2 * POOL_WIDTH), tile(3 * ATTN_WIDTH), tile(ATTN_WIDTH), tile(POOL_WIDTH)),
        scratch_shapes=[
            pltpu.VMEM((D_MODEL, IN_WIDTH), BF16),
            pltpu.VMEM((HALO, POOL_WIDTH), F32),
            pltpu.SemaphoreType.DMA((N_DEV,)),
        ],
        compiler_params=_params(("arbitrary",)),
    )(x2d, g1, wg, pool_w, pool_scale)


def _inv_count(first_row, rows, window):
    tpos = first_row + lax.broadcasted_iota(jnp.int32, (rows, 1), 0)
    return 1.0 / jnp.minimum(tpos + 1, window).astype(F32)


def _causal_window_sum(ext, window):
    s, k = ext, 1
    while k < window:
        s = s + pltpu.roll(s, k, 0)
        k *= 2
    return s


def _pool_diffs(pv, halo, first_row, window):
    s = _causal_window_sum(jnp.concatenate([halo, pv], axis=0), window)[HALO:]
    return s * _inv_count(first_row, pv.shape[0], window) - pv


def _pool_bwd_group(gi, i, n_tiles, cur_ref, prev_ref, pgn_ref, dy_ref, dyn_ref, pw_ref, ps_ref,
                    dp_ref, dpw_ref, dps_ref):
    w = POOL_WINDOWS[gi]
    sl = slice(gi * GROUP_DIM, (gi + 1) * GROUP_DIM)
    gate_sl = slice(POOL_WIDTH + gi * GROUP_DIM, POOL_WIDTH + (gi + 1) * GROUP_DIM)
    rows = TOKEN_TILE + HALO
    pw = pw_ref[gi].astype(BF16)
    ps = ps_ref[:, sl]
    d = _pool_diffs(cur_ref[:, sl], jnp.where(i > 0, prev_ref[:, sl], 0.0), i * TOKEN_TILE, w).astype(BF16)
    z = _nn(d, pw)
    g_ext = jnp.concatenate([cur_ref[:, gate_sl], pgn_ref[:, sl]], axis=0)
    dy_ext = jnp.concatenate([dy_ref[:, sl], dyn_ref[:, sl]], axis=0).astype(F32)
    sig = _sigmoid(g_ext)
    gate = g_ext * sig
    dz_ext = ((dy_ext * gate) * ps).astype(BF16)
    dd_ext = _nt(dz_ext, pw)
    yield
    e = dd_ext * _inv_count(i * TOKEN_TILE, rows, w)
    row = lax.broadcasted_iota(jnp.int32, (rows, 1), 0)
    e = jnp.where(jnp.logical_or(row < TOKEN_TILE, i < n_tiles - 1), e, 0.0)
    s, k = e, 1
    while k < w:
        s = s + pltpu.roll(s, rows - k, 0)
        k *= 2
    dp_ref[:, sl] = (s[:TOKEN_TILE] - dd_ext[:TOKEN_TILE]).astype(BF16)
    dy = dy_ext[:TOKEN_TILE]
    g = g_ext[:TOKEN_TILE]
    sg = sig[:TOKEN_TILE]
    dgate = sg * (1.0 + g * (1.0 - sg))
    dp_ref[:, gate_sl] = ((dy * (z * ps)) * dgate).astype(BF16)
    dps_ref[:, sl] += jnp.sum((dy * gate[:TOKEN_TILE]) * z, axis=0, keepdims=True)
    dpw_ref[gi] += _tn(d, dz_ext[:TOKEN_TILE])


_REL_FIRST = KV_WINDOW - 1 - MAX_REL


def _skew_rows(a, right):
    rows, lanes = a.shape
    row = lax.broadcasted_iota(jnp.int32, a.shape, 0)
    for b in range(rows.bit_length() - 1):
        shift = (1 << b) if right else lanes - (1 << b)
        a = jnp.where((row >> b) & 1 == 1, pltpu.roll(a, shift, 1), a)
    return a


def _toeplitz_bias(line):
    a = jnp.broadcast_to(line, (Q_BLOCK, TOEPLITZ))
    a = _skew_rows(a, True)
    a = pltpu.roll(a, TOEPLITZ - (Q_BLOCK - 1), 1)
    a = a[:, :KV_WINDOW]
    qc = lax.broadcasted_iota(jnp.int32, a.shape, 0) // CHUNK
    kc = lax.broadcasted_iota(jnp.int32, a.shape, 1) // CHUNK
    visible = jnp.logical_and(kc >= qc, kc <= qc + LEFT_CHUNKS)
    return jnp.where(visible, a, MASK_VALUE)


_BAND_WIDTH = 2 * LANES
_BAND_START = (384, 512, 512, 512)
_BAND_REL = tuple(a - (KV_BLOCKS - 1) * Q_BLOCK - qc * CHUNK for qc, a in enumerate(_BAND_START))


_ROW_G1, _ROW_G2, _ROW_PS, _ROW_LOSS, _ROW_RB = 0, 1, 2, 3, 8
_VEC_ROWS = 16


def _pack_small(band, total, d_g2, d_ps, loss_row):
    def body(band_ref, total_ref, g2_ref, ps_ref, loss_ref, vec_ref):
        vec_ref[...] = jnp.zeros_like(vec_ref)
        vec_ref[_ROW_G2:_ROW_G2 + 1, :] = g2_ref[...]
        vec_ref[_ROW_PS:_ROW_PS + 1, :POOL_WIDTH] = ps_ref[...]
        vec_ref[_ROW_LOSS:_ROW_LOSS + 1, :LANES] = loss_ref[0:1, :]
        for h in range(N_HEADS):
            a = jnp.zeros((CHUNK, 2 * _BAND_WIDTH), F32)
            for qc in range(Q_BLOCK // CHUNK):
                z = jnp.concatenate([band_ref[h, qc], jnp.zeros((CHUNK, _BAND_WIDTH), F32)], axis=1)
                left = -MAX_REL - _BAND_REL[qc]
                a = a + (pltpu.roll(z, 2 * _BAND_WIDTH - left, 1) if left else z)
            a = _skew_rows(a, False)
            near = jnp.sum(a, axis=0, keepdims=True)[:, :2 * LANES]
            r = lax.broadcasted_iota(jnp.int32, near.shape, 1)
            near = jnp.where(jnp.logical_and(r >= 1, r < 2 * MAX_REL), near, 0.0)
            everything = jnp.sum(jnp.sum(total_ref[h], axis=0, keepdims=True), axis=1, keepdims=True)
            far = everything - jnp.sum(near, axis=1, keepdims=True)
            vec_ref[_ROW_RB + h:_ROW_RB + h + 1, :2 * LANES] = jnp.where(r == 0, far, near)

    vm = pl.BlockSpec(memory_space=pltpu.VMEM)
    return pl.pallas_call(
        body,
        name="pack_small",
        out_shape=jax.ShapeDtypeStruct((_VEC_ROWS, D_MODEL), F32),
        in_specs=[vm] * 5,
        out_specs=vm,
        compiler_params=_params(),
    )(band, total, d_g2, d_ps, loss_row)


def _head_lanes(hh):
    lane = lax.broadcasted_iota(jnp.int32, (1, PAIR), 1)
    return (lane < HEAD_DIM) if hh == 0 else (lane >= HEAD_DIM)


def _score_windows(nwin):
    if nwin < KV_BLOCKS:
        return [(slice(0, Q_BLOCK), slice(0, nwin * Q_BLOCK), None)]
    pieces = []
    for qc in range(Q_BLOCK // CHUNK):
        rows = slice(qc * CHUNK, (qc + 1) * CHUNK)
        if qc < 2:
            pieces.append((rows, slice(0, KV_WINDOW - LANES), slice(KV_WINDOW - LANES, KV_WINDOW)))
        else:
            pieces.append((rows, slice(LANES, KV_WINDOW), slice(0, LANES)))
    return pieces


def _block_rows(first_block, n_blocks):
    if isinstance(first_block, int):
        return pl.ds(first_block * Q_BLOCK, n_blocks * Q_BLOCK)
    return pl.ds(pl.multiple_of(first_block * Q_BLOCK, Q_BLOCK), n_blocks * Q_BLOCK)


def _attn_fwd(qkv, bias_tile, w_out_shard):
    t = qkv.shape[0]
    nb = t // Q_BLOCK

    def body(q_ref, k_ref, v_ref, bias_ref, wout_ref, o_ref, gout_hbm,
             s_scr, p_scr, sout_ref, send_sems, recv_sems, local_sem):
        pair = pl.program_id(0)
        me = _dev_index(_mesh_pos())
        mine = pltpu.make_async_copy(sout_ref, gout_hbm.at[me], local_sem)

        def shard_copy(r, block):
            return pltpu.make_async_remote_copy(
                src_ref=sout_ref, dst_ref=gout_hbm.at[block], send_sem=send_sems.at[r - 1],
                recv_sem=recv_sems.at[r - 1], device_id=_peer(r), device_id_type=MESH_ID)

        @pl.when(pair == 0)
        def _():
            sout_ref[...] = wout_ref[...].astype(BF16)
            mine.start()
            for r in range(1, N_DEV):
                shard_copy(r, me).start()

        def scores(i, slot, nwin):
            q = q_ref[_block_rows(i, 1), :]
            kw = k_ref[_block_rows(i + 1 - nwin, nwin), :]
            for hh in range(2):
                q_h = jnp.where(_head_lanes(hh), q, jnp.zeros_like(q)) * ATTN_SCALE
                s_scr[slot, hh, :, :nwin * Q_BLOCK] = _nt(q_h, kw)

        def softmax(slot, nwin):
            off = (KV_BLOCKS - nwin) * Q_BLOCK
            for hh in range(2):
                for rows, cols, rest in _score_windows(nwin):
                    s = s_scr[slot, hh, rows, cols] + bias_ref[hh, rows, off + cols.start:off + cols.stop]
                    p_scr[slot, hh, rows, cols] = jnp.exp(s - jnp.max(s, axis=-1, keepdims=True)).astype(BF16)
                    if rest is not None:
                        p_scr[slot, hh, rows, rest] = jnp.zeros((CHUNK, LANES), BF16)

        def output(i, slot, nwin):
            vw = v_ref[_block_rows(i + 1 - nwin, nwin), :]
            outs = [_nn(p_scr[slot, hh, :, :nwin * Q_BLOCK], jnp.where(_head_lanes(hh), vw, jnp.ones_like(vw)))
                    for hh in range(2)]
            sums = pltpu.roll(jnp.where(_head_lanes(0), outs[1], outs[0]), HEAD_DIM, axis=1)
            o_ref[_block_rows(i, 1), :] = (jnp.where(_head_lanes(0), outs[0], outs[1]) * (1.0 / sums)).astype(BF16)

        scores(0, 0, 1)
        scores(1, 1, 2)
        softmax(0, 1)
        scores(2, 0, 3)
        softmax(1, 2)
        output(0, 0, 1)
        scores(3, 1, 3)
        softmax(0, 3)
        output(1, 1, 2)

        def two_steps(k, carry):
            i = 3 + 2 * k
            scores(i + 1, 0, KV_BLOCKS)
            softmax(1, KV_BLOCKS)
            output(i - 1, 0, KV_BLOCKS)
            scores(i + 2, 1, KV_BLOCKS)
            softmax(0, KV_BLOCKS)
            output(i, 1, KV_BLOCKS)
            return carry

        lax.fori_loop(0, (nb - 4) // 2, two_steps, 0)
        last = (nb - 1) % 2
        softmax(last, KV_BLOCKS)
        output(nb - 2, 1 - last, KV_BLOCKS)
        output(nb - 1, last, KV_BLOCKS)

        @pl.when(pair == N_PAIRS - 1)
        def _():
            for r in range(1, N_DEV):
                shard_copy(r, _dev_index(_peer(r))).wait_recv()
            for r in range(1, N_DEV):
                shard_copy(r, me).wait_send()
            mine.wait()

    col = lambda c0: pl.BlockSpec((t, PAIR), lambda j: (0, c0 + j))
    return pl.pallas_call(
        body,
        name="attn_fwd",
        grid=(N_PAIRS,),
        out_shape=(
            jax.ShapeDtypeStruct((t, ATTN_WIDTH), BF16),
            jax.ShapeDtypeStruct((N_DEV, OUT_SHARD, D_MODEL), BF16),
        ),
        in_specs=[col(0), col(N_PAIRS), col(2 * N_PAIRS),
                  pl.BlockSpec((2, Q_BLOCK, KV_WINDOW), lambda j: (j, 0, 0)),
                  pl.BlockSpec((OUT_SHARD, D_MODEL), lambda j: (0, 0))],
        out_specs=(col(0), pl.BlockSpec(memory_space=pl.ANY)),
        scratch_shapes=[
            pltpu.VMEM((2, 2, Q_BLOCK, KV_WINDOW), F32),
            pltpu.VMEM((2, 2, Q_BLOCK, KV_WINDOW), BF16),
            pltpu.VMEM((OUT_SHARD, D_MODEL), BF16),
            pltpu.SemaphoreType.DMA((N_DEV - 1,)),
            pltpu.SemaphoreType.DMA((N_DEV - 1,)),
            pltpu.SemaphoreType.DMA,
        ],
        compiler_params=_params(("arbitrary",)),
    )(qkv, qkv, qkv, bias_tile, w_out_shard)


def _attn_bwd(qkv, do, bias_tile, dwout_g):
    t = qkv.shape[0]
    nb = t // Q_BLOCK

    def body(q_ref, k_ref, v_ref, do_ref, bias_ref, dwout_hbm,
             dq_ref, dk_ref, dv_ref, band_ref, total_ref, land_hbm,
             s_scr, dp_scr, p_scr, dsb_scr, dq_scr, dk_acc, dv_acc, send_sems, recv_sems):
        exchange = _exchange_copies(dwout_hbm, land_hbm, send_sems, recv_sems)

        @pl.when(pl.program_id(0) == 0)
        def _():
            for cp in exchange:
                cp.start()

        band_ref[...] = jnp.zeros_like(band_ref)
        total_ref[...] = jnp.zeros_like(total_ref)

        def nwin_of(i):
            return min(i + 1, KV_BLOCKS) if isinstance(i, int) else KV_BLOCKS

        def operands(i, hh):
            lanes = _head_lanes(hh)
            q = q_ref[_block_rows(i, 1), :]
            do = do_ref[_block_rows(i, 1), :]
            return (jnp.where(lanes, q, jnp.zeros_like(q)) * ATTN_SCALE, jnp.where(lanes, do, jnp.zeros_like(do)))

        def products(i, hh):
            nwin = nwin_of(i)
            win = _block_rows(i + 1 - nwin, nwin)
            q_h, do_h = operands(i, hh)
            s_scr[hh, :, :nwin * Q_BLOCK] = _nt(q_h, k_ref[win, :])
            dp_scr[hh, :, :nwin * Q_BLOCK] = _nt(do_h, v_ref[win, :])

        def grads(i, hh):
            nwin = nwin_of(i)
            off = (KV_BLOCKS - nwin) * Q_BLOCK
            for rows, cols, rest in _score_windows(nwin):
                bias_cols = slice(off + cols.start, off + cols.stop)
                s = s_scr[hh, rows, cols] + bias_ref[hh, rows, bias_cols]
                e = jnp.exp(s - jnp.max(s, axis=-1, keepdims=True))
                p = e * (1.0 / jnp.sum(e, axis=-1, keepdims=True))
                dp = dp_scr[hh, rows, cols]
                ds = p * (dp - jnp.sum(p * dp, axis=-1, keepdims=True))
                total_ref[hh, rows, :] += sum(ds[:, c0:c0 + LANES] for c0 in range(0, ds.shape[1], LANES))
                for qc in range(rows.start // CHUNK, rows.stop // CHUNK):
                    lo = max(_BAND_START[qc], bias_cols.start)
                    hi = min(_BAND_START[qc] + _BAND_WIDTH, bias_cols.stop)
                    if lo < hi:
                        band_ref[hh, qc, :, lo - _BAND_START[qc]:hi - _BAND_START[qc]] += ds[
                            qc * CHUNK - rows.start:(qc + 1) * CHUNK - rows.start,
                            lo - bias_cols.start:hi - bias_cols.start]
                p_scr[hh, rows, cols] = p.astype(BF16)
                dsb_scr[hh, rows, cols] = ds.astype(BF16)
                if rest is not None:
                    p_scr[hh, rows, rest] = jnp.zeros((CHUNK, LANES), BF16)
                    dsb_scr[hh, rows, rest] = jnp.zeros((CHUNK, LANES), BF16)

        def ring(block):
            return block % KV_BLOCKS if isinstance(block, int) else lax.rem(block, KV_BLOCKS)

        def accumulate(i, hh):
            nwin = nwin_of(i)
            w = nwin * Q_BLOCK
            win = _block_rows(i + 1 - nwin, nwin)
            q_h, do_h = operands(i, hh)
            ds_b = dsb_scr[hh, :, :w]
            dq_h = _nn(ds_b, k_ref[win, :]) * ATTN_SCALE
            dkw = _tn(ds_b, q_h)
            dvw = _tn(p_scr[hh, :, :w], do_h)
            for b in range(nwin):
                slot = ring(i + 1 - nwin + b)
                part = slice(b * Q_BLOCK, (b + 1) * Q_BLOCK)
                if hh == 0 and b == nwin - 1:
                    dk_acc[slot] = dkw[part]
                    dv_acc[slot] = dvw[part]
                else:
                    dk_acc[slot] += dkw[part]
                    dv_acc[slot] += dvw[part]
            if hh == 0:
                dq_scr[...] = dq_h
            else:
                dq_ref[_block_rows(i, 1), :] = jnp.where(_head_lanes(0), dq_scr[...], dq_h).astype(BF16)
                if not (isinstance(i, int) and i < KV_BLOCKS - 1):
                    flush(i - (KV_BLOCKS - 1))

        def flush(block):
            dk_ref[_block_rows(block, 1), :] = dk_acc[ring(block)].astype(BF16)
            dv_ref[_block_rows(block, 1), :] = dv_acc[ring(block)].astype(BF16)

        def tile(n):
            return n // 2, n % 2

        def step(n):
            if n + 1 < 2 * nb:
                products(*tile(n + 1))
            grads(*tile(n))
            if n >= 1:
                accumulate(*tile(n - 1))

        products(0, 0)
        for n in range(2 * KV_BLOCKS):
            step(n)

        def two_steps(i, carry):
            products(i, 1)
            grads(i, 0)
            accumulate(i - 1, 1)
            products(i + 1, 0)
            grads(i, 1)
            accumulate(i, 0)
            return carry

        lax.fori_loop(KV_BLOCKS, nb - 1, two_steps, 0)
        step(2 * nb - 2)
        step(2 * nb - 1)
        accumulate(nb - 1, 1)
        flush(nb - 2)
        flush(nb - 1)

        @pl.when(pl.program_id(0) == N_PAIRS - 1)
        def _():
            for cp in exchange:
                cp.wait_recv()
            for cp in exchange:
                cp.wait_send()

    col = lambda c0: pl.BlockSpec((t, PAIR), lambda j: (0, c0 + j))
    tile_spec = pl.BlockSpec((2, Q_BLOCK, KV_WINDOW), lambda j: (j, 0, 0))
    out = jax.ShapeDtypeStruct((t, ATTN_WIDTH), BF16)
    return pl.pallas_call(
        body,
        name="attn_bwd",
        grid=(N_PAIRS,),
        out_shape=(out, out, out,
                   jax.ShapeDtypeStruct((N_HEADS, Q_BLOCK // CHUNK, CHUNK, _BAND_WIDTH), F32),
                   jax.ShapeDtypeStruct((N_HEADS, Q_BLOCK, LANES), F32),
                   jax.ShapeDtypeStruct((N_DEV - 1, OUT_SHARD, D_MODEL), BF16)),
        in_specs=[col(0), col(N_PAIRS), col(2 * N_PAIRS), col(0), tile_spec, pl.BlockSpec(memory_space=pl.ANY)],
        out_specs=(col(0), col(0), col(0),
                   pl.BlockSpec((2, Q_BLOCK // CHUNK, CHUNK, _BAND_WIDTH), lambda j: (j, 0, 0, 0)),
                   pl.BlockSpec((2, Q_BLOCK, LANES), lambda j: (j, 0, 0)),
                   pl.BlockSpec(memory_space=pl.ANY)),
        scratch_shapes=[
            pltpu.VMEM((2, Q_BLOCK, KV_WINDOW), F32),
            pltpu.VMEM((2, Q_BLOCK, KV_WINDOW), F32),
            pltpu.VMEM((2, Q_BLOCK, KV_WINDOW), BF16),
            pltpu.VMEM((2, Q_BLOCK, KV_WINDOW), BF16),
            pltpu.VMEM((Q_BLOCK, PAIR), F32),
            pltpu.VMEM((KV_BLOCKS, Q_BLOCK, PAIR), F32),
            pltpu.VMEM((KV_BLOCKS, Q_BLOCK, PAIR), F32),
            pltpu.SemaphoreType.DMA((N_DEV - 1,)),
            pltpu.SemaphoreType.DMA((N_DEV - 1,)),
        ],
        compiler_params=_params(("arbitrary",)),
    )(qkv, qkv, qkv, do, bias_tile, dwout_g)


def _outproj_loss(x2d, tgt2d, y_pool, o, ag, wout_g, g2):
    t = x2d.shape[0]
    n_tiles = t // TOKEN_TILE

    def body(x_ref, tgt_ref, yp_ref, o_ref, ag_ref, w_ref, g_ref,
             dx2_ref, dyp_ref, do_ref, dag_ref, dw_ref, dg_ref, loss_ref, acc_ref):
        i = pl.program_id(0)

        @pl.when(i == 0)
        def _():
            acc_ref[...] = jnp.zeros_like(acc_ref)
            dg_ref[...] = jnp.zeros_like(dg_ref)
            loss_ref[...] = jnp.zeros_like(loss_ref)

        w = w_ref[...].reshape(D_MODEL, D_MODEL)
        ga = ag_ref[...]
        sig = _sigmoid(ga)
        gate = ga * sig
        of = o_ref[...].astype(F32)
        y = jnp.concatenate([yp_ref[...], (of * gate).astype(BF16)], axis=1)
        x2 = x_ref[...] + _nn(y, w)
        r = lax.rsqrt(jnp.mean(x2 * x2, axis=-1, keepdims=True) + EPS)
        xh = x2 * r
        g = g_ref[...]
        diff = xh * g - tgt_ref[...]
        tok = jnp.sum(diff * diff, axis=-1, keepdims=True) * (1.0 / D_MODEL)
        loss_ref[...] += jnp.sum(tok, axis=0, keepdims=True)
        dout = diff * (1.0 / D_MODEL)
        dg_ref[...] += jnp.sum(dout * xh, axis=0, keepdims=True)
        u = dout * g
        dx2 = r * (u - xh * jnp.mean(u * xh, axis=-1, keepdims=True))
        dx2_ref[...] = dx2
        dx2_b = dx2.astype(BF16)
        dy = _nt(dx2_b, w)
        dyp_ref[...] = dy[:, :POOL_WIDTH].astype(BF16)
        dya = dy[:, POOL_WIDTH:]
        do_ref[...] = (dya * gate).astype(BF16)
        dag_ref[...] = ((dya * of) * (sig * (1.0 + ga * (1.0 - sig)))).astype(BF16)
        acc_ref[...] += _tn(y, dx2_b)

        @pl.when(i == n_tiles - 1)
        def _():
            dw_ref[...] = acc_ref[...].reshape(N_DEV, OUT_SHARD, D_MODEL).astype(BF16)

    tile = lambda width: pl.BlockSpec((TOKEN_TILE, width), lambda i: (i, 0))
    return pl.pallas_call(
        body,
        name="outproj_loss",
        grid=(n_tiles,),
        out_shape=(
            jax.ShapeDtypeStruct((t, D_MODEL), F32),
            jax.ShapeDtypeStruct((t, POOL_WIDTH), BF16),
            jax.ShapeDtypeStruct((t, ATTN_WIDTH), BF16),
            jax.ShapeDtypeStruct((t, ATTN_WIDTH), BF16),
            jax.ShapeDtypeStruct((N_DEV, OUT_SHARD, D_MODEL), BF16),
            jax.ShapeDtypeStruct((1, D_MODEL), F32),
            jax.ShapeDtypeStruct((8, LANES), F32),
        ),
        in_specs=[
            tile(D_MODEL), tile(D_MODEL), tile(POOL_WIDTH), tile(ATTN_WIDTH), tile(ATTN_WIDTH),
            pl.BlockSpec((N_DEV, OUT_SHARD, D_MODEL), lambda i: (0, 0, 0)),
            pl.BlockSpec((1, D_MODEL), lambda i: (0, 0)),
        ],
        out_specs=(
            tile(D_MODEL), tile(POOL_WIDTH), tile(ATTN_WIDTH), tile(ATTN_WIDTH),
            pl.BlockSpec((N_DEV, OUT_SHARD, D_MODEL), lambda i: (0, 0, 0)),
            pl.BlockSpec((1, D_MODEL), lambda i: (0, 0)),
            pl.BlockSpec((8, LANES), lambda i: (0, 0)),
        ),
        scratch_shapes=[pltpu.VMEM((D_MODEL, D_MODEL), F32)],
        compiler_params=_params(("arbitrary",)),
    )(x2d, tgt2d, y_pool, o, ag, wout_g, g2)


def _dproj_specs():
    tile = lambda width: pl.BlockSpec((TOKEN_TILE, width), lambda i: (i, 0))
    return [tile(2 * POOL_WIDTH)] + [tile(ATTN_WIDTH)] * 4


def _inproj_bwd_dx(x2d, dx2, dproj, g1, wg):
    t = x2d.shape[0]

    def body(x_ref, dx2_ref, dp_ref, dq_ref, dk_ref, dv_ref, dag_ref, g_ref, wg_hbm, gx_ref, dg_ref, wfull_ref, sem):
        @pl.when(pl.program_id(0) == 0)
        def _():
            _load_w_in(wg_hbm, wfull_ref, sem)
            dg_ref[...] = jnp.zeros_like(dg_ref)

        dproj_t = jnp.concatenate([dp_ref[...], dq_ref[...], dk_ref[...], dv_ref[...], dag_ref[...]], axis=1)
        dh = _nt(dproj_t, wfull_ref[...])
        xf = x_ref[...]
        r = lax.rsqrt(jnp.mean(xf * xf, axis=-1, keepdims=True) + EPS)
        xh = xf * r
        dg_ref[...] += jnp.sum(dh * xh, axis=0, keepdims=True)
        u = dh * g_ref[...]
        gx_ref[...] = dx2_ref[...] + r * (u - xh * jnp.mean(u * xh, axis=-1, keepdims=True))

    tile = pl.BlockSpec((TOKEN_TILE, D_MODEL), lambda i: (i, 0))
    return pl.pallas_call(
        body,
        name="inproj_bwd_dx",
        grid=(t // TOKEN_TILE,),
        out_shape=(jax.ShapeDtypeStruct((t, D_MODEL), F32), jax.ShapeDtypeStruct((1, D_MODEL), F32)),
        in_specs=[tile, tile] + _dproj_specs() + [
            pl.BlockSpec((1, D_MODEL), lambda i: (0, 0)),
            pl.BlockSpec(memory_space=pl.ANY),
        ],
        out_specs=(tile, pl.BlockSpec((1, D_MODEL), lambda i: (0, 0))),
        scratch_shapes=[pltpu.VMEM((D_MODEL, IN_WIDTH), BF16), pltpu.SemaphoreType.DMA((N_DEV,))],
        compiler_params=_params(("arbitrary",)),
    )(x2d, dx2, *dproj, g1, wg)


def _inproj_bwd_dw(x2d, pvg, dy_pool, dattn, g1, pool_w, pool_scale):
    t = x2d.shape[0]
    n_tiles = t // TOKEN_TILE
    halo_per_tile = TOKEN_TILE // HALO
    last_halo = t // HALO - 1

    def body(x_ref, cur_ref, prev_ref, pgn_ref, dy_ref, dyn_ref, dq_ref, dk_ref, dv_ref, dag_ref, g_ref, pw_ref, ps_ref,
             out_ref, dp_ref, dpw_ref, dps_ref, acc_ref):
        i = pl.program_id(0)

        @pl.when(i == 0)
        def _():
            acc_ref[...] = jnp.zeros_like(acc_ref)
            dpw_ref[...] = jnp.zeros_like(dpw_ref)
            dps_ref[...] = jnp.zeros_like(dps_ref)

        xf = x_ref[...]
        r = lax.rsqrt(jnp.mean(xf * xf, axis=-1, keepdims=True) + EPS)
        h = ((xf * r) * g_ref[...]).astype(BF16)
        half = ATTN_WIDTH // 2
        for gi, ref in enumerate((dq_ref, dk_ref, dv_ref, dag_ref)):
            pool_halves = _pool_bwd_group(gi, i, n_tiles, cur_ref, prev_ref, pgn_ref, dy_ref, dyn_ref, pw_ref, ps_ref,
                                          dp_ref, dpw_ref, dps_ref)
            for c0 in (0, half):
                col = 2 * POOL_WIDTH + gi * ATTN_WIDTH + c0
                acc_ref[:, col:col + half] += _tn(h, ref[:, c0:c0 + half])
                next(pool_halves, None)
        for c0 in (0, POOL_WIDTH):
            acc_ref[:, c0:c0 + POOL_WIDTH] += _tn(h, dp_ref[:, c0:c0 + POOL_WIDTH])

        @pl.when(i == n_tiles - 1)
        def _():
            for d in range(N_DEV):
                out_ref[d] = acc_ref[:, d * IN_SHARD:(d + 1) * IN_SHARD].astype(BF16)

    tile = lambda width: pl.BlockSpec((TOKEN_TILE, width), lambda i: (i, 0))
    next_halo = lambda col: pl.BlockSpec(
        (HALO, POOL_WIDTH), lambda i: (jnp.minimum((i + 1) * halo_per_tile, last_halo), col))
    return pl.pallas_call(
        body,
        name="inproj_bwd_dw",
        grid=(n_tiles,),
        out_shape=(
            jax.ShapeDtypeStruct((N_DEV, D_MODEL, IN_SHARD), BF16),
            jax.ShapeDtypeStruct((t, 2 * POOL_WIDTH), BF16),
            jax.ShapeDtypeStruct((N_GROUPS, GROUP_DIM, GROUP_DIM), F32),
            jax.ShapeDtypeStruct((1, POOL_WIDTH), F32),
        ),
        in_specs=[
            tile(D_MODEL),
            tile(2 * POOL_WIDTH),
            pl.BlockSpec((HALO, POOL_WIDTH), lambda i: (jnp.maximum(i * halo_per_tile - 1, 0), 0)),
            next_halo(1),
            tile(POOL_WIDTH),
            next_halo(0),
            tile(ATTN_WIDTH), tile(ATTN_WIDTH), tile(ATTN_WIDTH), tile(ATTN_WIDTH),
            pl.BlockSpec((1, D_MODEL), lambda i: (0, 0)),
            pl.BlockSpec((N_GROUPS, GROUP_DIM, GROUP_DIM), lambda i: (0, 0, 0)),
            pl.BlockSpec((1, POOL_WIDTH), lambda i: (0, 0)),
        ],
        out_specs=(
            pl.BlockSpec((N_DEV, D_MODEL, IN_SHARD), lambda i: (0, 0, 0)),
            tile(2 * POOL_WIDTH),
            pl.BlockSpec((N_GROUPS, GROUP_DIM, GROUP_DIM), lambda i: (0, 0, 0)),
            pl.BlockSpec((1, POOL_WIDTH), lambda i: (0, 0)),
        ),
        scratch_shapes=[pltpu.VMEM((D_MODEL, IN_WIDTH), F32)],
        compiler_params=_params(("arbitrary",)),
    )(x2d, pvg, pvg, pvg, dy_pool, dy_pool, *dattn, g1, pool_w, pool_scale)


_HBM = pl.BlockSpec(memory_space=pltpu.HBM)
_SEM = pl.BlockSpec(memory_space=pltpu.SEMAPHORE)
_DATAFLOW = pltpu.SideEffectType.DATAFLOW_SIDE_EFFECTING


_N_EXCHANGED = 3


def _exchange_all(refs, send_sems, recv_sems):
    win_hbm, win_land, pw_hbm, pw_land, vec_hbm, vec_land = refs
    return (_exchange_copies(win_hbm, win_land, send_sems, recv_sems)
            + _exchange_copies(pw_hbm, pw_land, send_sems, recv_sems, first_sem=N_DEV - 1)
            + _exchange_copies(vec_hbm, vec_land, send_sems, recv_sems, first_sem=2 * (N_DEV - 1), same_for_all=True))


def _exchange_start(win_blocks, pw_blocks, vec):
    arrays = []
    for a, land_shape in ((win_blocks, (N_DEV - 1,) + win_blocks.shape[1:]),
                          (pw_blocks, (N_DEV - 1,) + pw_blocks.shape[1:]),
                          (vec, (N_DEV - 1,) + vec.shape)):
        arrays += [pltpu.with_memory_space_constraint(a, pltpu.HBM),
                   pltpu.with_memory_space_constraint(lax.empty(land_shape, a.dtype), pltpu.HBM)]

    def body(*refs):
        ins, (send_sems, recv_sems), token = refs[:2 * _N_EXCHANGED], refs[2 * _N_EXCHANGED:2 * _N_EXCHANGED + 2], refs[-1]
        for cp in _exchange_all(ins, send_sems, recv_sems):
            cp.start()
        token[...] = jnp.zeros_like(token)

    sems = pltpu.SemaphoreType.DMA((_N_EXCHANGED * (N_DEV - 1),))
    return pl.pallas_call(
        body,
        name="exchange_start",
        out_shape=(sems, sems, *[pltpu.HBM(a.shape, a.dtype) for a in arrays], jax.ShapeDtypeStruct((8, LANES), F32)),
        in_specs=tuple([_HBM] * len(arrays)),
        out_specs=(_SEM, _SEM, *[_HBM] * len(arrays), pl.BlockSpec(memory_space=pltpu.VMEM)),
        input_output_aliases={k: 2 + k for k in range(len(arrays))},
        compiler_params=pltpu.CompilerParams(has_side_effects=_DATAFLOW),
    )(*arrays)


def _exchange_wait(send_sems, recv_sems, arrays, after):
    def body(*refs):
        ins = refs[:2 * _N_EXCHANGED]
        send_sems, recv_sems = refs[2 * _N_EXCHANGED:2 * _N_EXCHANGED + 2]
        for cp in _exchange_all(ins, send_sems, recv_sems):
            cp.wait_send()
            cp.wait_recv()

    return pl.pallas_call(
        body,
        name="exchange_wait",
        out_shape=tuple(pltpu.HBM(a.shape, a.dtype) for a in arrays),
        in_specs=(*[_HBM] * len(arrays), _SEM, _SEM, pl.BlockSpec(memory_space=pl.ANY)),
        out_specs=tuple([_HBM] * len(arrays)),
        input_output_aliases={k: k for k in range(len(arrays))},
        compiler_params=pltpu.CompilerParams(has_side_effects=_DATAFLOW),
    )(*arrays, send_sems, recv_sems, after)


def _adamw(w, g, m, v):
    m = ADAM_B1 * m + (1.0 - ADAM_B1) * g
    v = ADAM_B2 * v + (1.0 - ADAM_B2) * (g * g)
    m_hat = m / (1.0 - ADAM_B1 ** ADAM_STEP)
    v_hat = v / (1.0 - ADAM_B2 ** ADAM_STEP)
    delta = -ADAM_LR * (m_hat / (jnp.sqrt(v_hat) + ADAM_EPS) + ADAM_WD * w)
    return delta, m, v


def _small_allreduce(d_g1, vec, vec_land, pw_blocks, pw_land):
    def body(g1_ref, vec_ref, vland_ref, pwb_ref, pland_ref, vec_out, pw_out, vparts_ref, pparts_ref, rows_ref,
             slice_ref, send_sems, recv_sems):
        me = _dev_index(_mesh_pos())

        def from_devices(parts_ref, own, land_ref):
            parts_ref[0] = own
            parts_ref[1:] = land_ref[...]
            total = parts_ref[me]
            for s in range(1, N_DEV):
                total = total + parts_ref[me ^ s]
            return total

        slice_ref[...] = from_devices(pparts_ref, pwb_ref[me], pland_ref)

        def send(r, src, dst, k):
            return pltpu.make_async_remote_copy(
                src_ref=src, dst_ref=dst, send_sem=send_sems.at[2 * (r - 1) + k],
                recv_sem=recv_sems.at[2 * (r - 1) + k], device_id=_peer(r), device_id_type=MESH_ID)

        started = [cp for r in range(1, N_DEV)
                   for cp in (send(r, g1_ref, rows_ref.at[r], 0), send(r, slice_ref, pw_out.at[me], 1))]
        for cp in started:
            cp.start()
        rows_ref[0] = g1_ref[...]
        pw_out[me] = slice_ref[...]
        vec_out[...] = from_devices(vparts_ref, vec_ref[...], vland_ref)
        for cp in started:
            cp.wait_recv()
        for cp in started:
            cp.wait_send()
        g1 = rows_ref[me]
        for s in range(1, N_DEV):
            g1 = g1 + rows_ref[me ^ s]
        vec_out[_ROW_G1:_ROW_G1 + 1, :] = g1

    vm = pl.BlockSpec(memory_space=pltpu.VMEM)
    return pl.pallas_call(
        body,
        name="small_allreduce",
        out_shape=(jax.ShapeDtypeStruct((_VEC_ROWS, D_MODEL), F32),
                   jax.ShapeDtypeStruct((N_DEV, GROUP_DIM // 2, GROUP_DIM), F32)),
        in_specs=[vm] * 5,
        out_specs=(vm, vm),
        scratch_shapes=[
            pltpu.VMEM((N_DEV, _VEC_ROWS, D_MODEL), F32),
            pltpu.VMEM((N_DEV, GROUP_DIM // 2, GROUP_DIM), F32),
            pltpu.VMEM((N_DEV, 1, D_MODEL), F32),
            pltpu.VMEM((GROUP_DIM // 2, GROUP_DIM), F32),
            pltpu.SemaphoreType.DMA((2 * (N_DEV - 1),)),
            pltpu.SemaphoreType.DMA((2 * (N_DEV - 1),)),
        ],
        compiler_params=_params(),
    )(d_g1, vec, vec_land, pw_blocks, pw_land)


def _adamw_all(dwin_g, land_in, dwout_g, land_out, vec_sum, pw_sum, weights, big):
    small_shapes = [(1, D_MODEL), (1, D_MODEL), (1, POOL_WIDTH), (N_HEADS, 2 * LANES), (N_GROUPS, GROUP_DIM, GROUP_DIM)]

    def body(*refs):
        refs = list(refs)
        take = lambda n: [refs.pop(0) for _ in range(n)]
        dwin_hbm, lin_ref, dwout_hbm, lout_ref, vec_ref, pw_ref = take(6)
        small_wmv = [take(3) for _ in range(5)]
        big_wmv = [take(3) for _ in range(2)]
        big_out = [take(4) for _ in range(2)]
        small_out = [take(4) for _ in range(5)]
        own_in, own_out, local_sems = refs

        me = _dev_index(_mesh_pos())
        mine = [pltpu.make_async_copy(dwin_hbm.at[me], own_in, local_sems.at[0]),
                pltpu.make_async_copy(dwout_hbm.at[me], own_out, local_sems.at[1])]
        for cp in mine:
            cp.start()

        def update(g, wmv, outs):
            delta, m_new, v_new = _adamw(wmv[0][...], g, wmv[1][...], wmv[2][...])
            for ref, val in zip(outs, (g, delta, m_new, v_new)):
                ref[...] = val

        update(vec_ref[_ROW_G1:_ROW_G1 + 1, :], small_wmv[0], small_out[0])
        update(vec_ref[_ROW_G2:_ROW_G2 + 1, :], small_wmv[1], small_out[1])
        update(vec_ref[_ROW_PS:_ROW_PS + 1, :POOL_WIDTH], small_wmv[2], small_out[2])
        update(vec_ref[_ROW_RB:_ROW_RB + N_HEADS, :2 * LANES], small_wmv[3], small_out[3])
        update(pw_ref[...], small_wmv[4], small_out[4])
        for cp in mine:
            cp.wait()
        g_in = own_in[...].astype(F32)
        g_out = own_out[...].astype(F32)
        for r in range(N_DEV - 1):
            g_in = g_in + lin_ref[r].astype(F32)
            g_out = g_out + lout_ref[r].astype(F32)
        update(g_in, big_wmv[0], big_out[0])
        update(g_out, big_wmv[1], big_out[1])

    vm = pl.BlockSpec(memory_space=pltpu.VMEM)
    hbm = pl.BlockSpec(memory_space=pl.ANY)
    f32 = lambda shape: jax.ShapeDtypeStruct(shape, F32)
    out_shapes = [f32((D_MODEL, IN_SHARD))] * 4 + [f32((OUT_SHARD, D_MODEL))] * 4
    for shape in small_shapes:
        out_shapes += [f32(shape)] * 4
    args = [dwin_g, land_in, dwout_g, land_out, vec_sum, pw_sum]
    for wmv in weights:
        args += list(wmv)
    for wmv in big:
        args += list(wmv)
    return pl.pallas_call(
        body,
        name="adamw_all",
        out_shape=tuple(out_shapes),
        in_specs=[hbm, vm, hbm, vm] + [vm] * (len(args) - 4),
        out_specs=tuple([vm] * len(out_shapes)),
        scratch_shapes=[
            pltpu.VMEM((D_MODEL, IN_SHARD), BF16),
            pltpu.VMEM((OUT_SHARD, D_MODEL), BF16),
            pltpu.SemaphoreType.DMA((2,)),
        ],
        compiler_params=_params(),
    )(*args)


def kernel(x, norm_gain, w_in, pool_w, pool_scale, rel_bias, w_out, final_norm_gain, loss_target, m_norm_gain, m_w_in, m_pool_w, m_pool_scale, m_rel_bias, m_w_out, m_final_norm_gain, v_norm_gain, v_w_in, v_pool_w, v_pool_scale, v_rel_bias, v_w_out, v_final_norm_gain):
    t = x.shape[1]
    assert x.shape[0] == 1 and t % TOKEN_TILE == 0 and t // Q_BLOCK >= 4
    x2d = x[0]
    tgt2d = loss_target[0]
    g2 = final_norm_gain.reshape(1, D_MODEL)

    rb = rel_bias[0]
    rel_line = jnp.concatenate([
        jnp.broadcast_to(rb[:, :1], (N_HEADS, _REL_FIRST)), rb,
        jnp.broadcast_to(rb[:, N_REL - 1:], (N_HEADS, TOEPLITZ - _REL_FIRST - N_REL)),
    ], axis=1).reshape(N_HEADS, 1, TOEPLITZ)
    wg_in, bias_tile = _gather_weights(w_in[0], rel_line)

    pvg, qkv, ag, y_pool = _norm_inproj(x2d, norm_gain, wg_in, pool_w[0], pool_scale)
    o, wg_out = _attn_fwd(qkv, bias_tile, w_out[0])
    dx2, dy_pool, do, dag, dwout_g, d_g2, loss_sum = _outproj_loss(x2d, tgt2d, y_pool, o, ag, wg_out, g2)
    dq, dk, dv, ds_band, ds_total, land_out = _attn_bwd(qkv, do, bias_tile, dwout_g)
    dwin_g, d_pool, d_pw, d_ps = _inproj_bwd_dw(x2d, pvg, dy_pool, (dq, dk, dv, dag), norm_gain, pool_w[0], pool_scale)
    vec = _pack_small(ds_band, ds_total, d_g2, d_ps, loss_sum)
    dproj = (d_pool, dq, dk, dv, dag)
    pw_blocks = d_pw.reshape(N_DEV, GROUP_DIM // 2, GROUP_DIM)
    send_sems, recv_sems, *exchanged, token = _exchange_start(dwin_g, pw_blocks, vec)
    grad_x, d_g1 = _inproj_bwd_dx(x2d, dx2, dproj, norm_gain + token[:1, :1], wg_in)
    dwin_g, land_in, pw_blocks, pw_land, vec, vec_land = _exchange_wait(send_sems, recv_sems, exchanged, d_g1)

    pad_rb = lambda a: jnp.pad(a[0], ((0, 0), (0, 2 * LANES - N_REL)))
    row = lambda a: a.reshape(1, D_MODEL)
    weights = [
        (norm_gain, m_norm_gain, v_norm_gain),
        (row(final_norm_gain), row(m_final_norm_gain), row(v_final_norm_gain)),
        (pool_scale, m_pool_scale, v_pool_scale),
        (pad_rb(rel_bias), pad_rb(m_rel_bias), pad_rb(v_rel_bias)),
        (pool_w[0], m_pool_w[0], v_pool_w[0]),
    ]
    big = [(w_in[0], m_w_in[0], v_w_in[0]), (w_out[0], m_w_out[0], v_w_out[0])]
    vec_sum, pw_sum = _small_allreduce(d_g1, vec, vec_land, pw_blocks, pw_land)
    res = _adamw_all(dwin_g, land_in, dwout_g, land_out, vec_sum, pw_sum.reshape(N_GROUPS, GROUP_DIM, GROUP_DIM),
                     weights, big)
    loss = 0.5 * vec_sum[_ROW_LOSS, 0]

    def leaves(k):
        g1_, g2_, ps_, rb_, pw_ = (res[8 + 4 * leaf + k] for leaf in range(5))
        return [g1_, res[k][None], pw_[None], ps_, rb_[None, :, :N_REL], res[4 + k][None], g2_.reshape(D_MODEL)]

    return (loss, grad_x[None], *leaves(0), *leaves(1), *leaves(2), *leaves(3))
```

```python
import math

import jax
import jax.numpy as jnp
from jax import lax
from jax.experimental import pallas as pl
from jax.experimental.pallas import tpu as pltpu

F32 = jnp.float32
BF16 = jnp.bfloat16
MESH_ID = pl.DeviceIdType.MESH

D_MODEL = 1024
POOL_WIDTH = 512
ATTN_WIDTH = 512
POOL_WINDOWS = (2, 4, 8, 16)
N_GROUPS = 4
GROUP_DIM = 128
HEAD_DIM = 64
N_HEADS = 8
CHUNK = 64
LEFT_CHUNKS = 8
MAX_REL = 64
N_REL = 2 * MAX_REL + 1
IN_WIDTH = 2 * POOL_WIDTH + 4 * ATTN_WIDTH
EPS = 1e-6
MASK_VALUE = -1e30
ATTN_SCALE = 1.0 / math.sqrt(HEAD_DIM)
ADAM_LR = 0.001
ADAM_B1 = 0.9
ADAM_B2 = 0.999
ADAM_EPS = 1e-08
ADAM_WD = 0.01
ADAM_STEP = 10

N_DEV = 8
IN_SHARD = IN_WIDTH // N_DEV
OUT_SHARD = D_MODEL // N_DEV

LANES = 128
TOKEN_TILE = 512
HALO = 16
Q_BLOCK = 256
KV_BLOCKS = 3
KV_WINDOW = KV_BLOCKS * Q_BLOCK
PAIR = 2 * HEAD_DIM
N_PAIRS = N_HEADS // 2
TOEPLITZ = 1024
VMEM_LIMIT = 56 * 1024 * 1024


def _params(sem=None, vmem=VMEM_LIMIT):
    return pltpu.CompilerParams(dimension_semantics=sem, vmem_limit_bytes=vmem)


def _sigmoid(x):
    return 1.0 / (1.0 + jnp.exp(-x))


def _nt(a, b):
    return lax.dot_general(a, b, (((1,), (1,)), ((), ())), preferred_element_type=F32)


def _tn(a, b):
    return lax.dot_general(a, b, (((0,), (0,)), ((), ())), preferred_element_type=F32)


def _nn(a, b):
    return jnp.dot(a, b, preferred_element_type=F32)


def _mesh_pos():
    return lax.axis_index("x"), lax.axis_index("y"), lax.axis_index("c")


def _dev_index(p):
    return 4 * p[0] + 2 * p[1] + p[2]


def _peer(r):
    x, y, c = _mesh_pos()
    return (x ^ ((r >> 2) & 1), y ^ ((r >> 1) & 1), c ^ (r & 1))


def _exchange_copies(src_hbm, land_hbm, send_sems, recv_sems, first_sem=0, same_for_all=False):
    return [
        pltpu.make_async_remote_copy(
            src_ref=src_hbm if same_for_all else src_hbm.at[_dev_index(_peer(r))], dst_ref=land_hbm.at[r - 1],
            send_sem=send_sems.at[first_sem + r - 1], recv_sem=recv_sems.at[first_sem + r - 1],
            device_id=_peer(r), device_id_type=MESH_ID)
        for r in range(1, N_DEV)
    ]


def _gather_weights(w_in_shard, rel_line):
    def body(win_ref, line_ref, gin_ref, bias_ref, sin_ref, send_sems, recv_sems):
        x, y, c = _mesh_pos()
        me, sibling = (x, y, c), (x, y, 1 - c)
        chips = [(1 - x, y), (x, 1 - y), (1 - x, 1 - y)]

        sin_ref[...] = win_ref[...].astype(BF16)
        gin_ref[_dev_index(me)] = sin_ref[...]

        def copy(k, block, to, from_shard=False):
            return pltpu.make_async_remote_copy(
                src_ref=sin_ref if from_shard else gin_ref.at[_dev_index(block)],
                dst_ref=gin_ref.at[_dev_index(block)],
                send_sem=send_sems.at[k],
                recv_sem=recv_sems.at[k],
                device_id=to,
                device_id_type=MESH_ID,
            )

        first = [copy(0, me, sibling, True)]
        first += [copy(1 + j, me, (*chip, c), True) for j, chip in enumerate(chips)]
        for cp in first:
            cp.start()
        passed = [copy(4 + j, (*chip, c), sibling) for j, chip in enumerate(chips)]

        def bias_heads(lo, hi):
            for h in range(lo, hi):
                bias_ref[h] = _toeplitz_bias(line_ref[h])

        bias_heads(0, N_HEADS - 3)
        for j, chip in enumerate(chips):
            copy(1 + j, (*chip, c), me).wait_recv()
            passed[j].start()
            bias_heads(N_HEADS - 3 + j, N_HEADS - 2 + j)
        copy(0, sibling, me).wait_recv()
        for j, chip in enumerate(chips):
            copy(4 + j, (*chip, 1 - c), me).wait_recv()
        for cp in first + passed:
            cp.wait_send()

    vm = pl.BlockSpec(memory_space=pltpu.VMEM)
    return pl.pallas_call(
        body,
        name="gather_weights",
        out_shape=(
            jax.ShapeDtypeStruct((N_DEV, D_MODEL, IN_SHARD), BF16),
            jax.ShapeDtypeStruct((N_HEADS, Q_BLOCK, KV_WINDOW), F32),
        ),
        in_specs=[vm, vm],
        out_specs=(vm, vm),
        scratch_shapes=[
            pltpu.VMEM((D_MODEL, IN_SHARD), BF16),
            pltpu.SemaphoreType.DMA((7,)),
            pltpu.SemaphoreType.DMA((7,)),
        ],
        compiler_params=_params(),
    )(w_in_shard, rel_line)


def _load_w_in(wg_hbm, wfull_ref, sem):
    copies = [
        pltpu.make_async_copy(wg_hbm.at[d], wfull_ref.at[:, d * IN_SHARD:(d + 1) * IN_SHARD], sem.at[d])
        for d in range(N_DEV)
    ]
    for cp in copies:
        cp.start()
    for cp in copies:
        cp.wait()


def _norm_inproj(x2d, g1, wg, pool_w, pool_scale):
    t = x2d.shape[0]

    def body(x_ref, g_ref, wg_hbm, pw_ref, ps_ref, pvg_ref, qkv_ref, ag_ref, yp_ref, wfull_ref, halo_ref, sem):
        i = pl.program_id(0)

        @pl.when(i == 0)
        def _():
            _load_w_in(wg_hbm, wfull_ref, sem)
            halo_ref[...] = jnp.zeros_like(halo_ref)

        xf = x_ref[...]
        r = lax.rsqrt(jnp.mean(xf * xf, axis=-1, keepdims=True) + EPS)
        h = ((xf * r) * g_ref[...]).astype(BF16)
        chunk = lambda ci: _nn(h, wfull_ref[:, ci * POOL_WIDTH:(ci + 1) * POOL_WIDTH])
        pv, pg = chunk(0), chunk(1)
        pvg_ref[:, :POOL_WIDTH] = pv
        pvg_ref[:, POOL_WIDTH:] = pg
        halo = halo_ref[...]
        halo_ref[...] = pv[TOKEN_TILE - HALO:]
        for gi in range(N_GROUPS):
            sl = slice(gi * GROUP_DIM, (gi + 1) * GROUP_DIM)
            d = _pool_diffs(pv[:, sl], halo[:, sl], i * TOKEN_TILE, POOL_WINDOWS[gi])
            z = _nn(d.astype(BF16), pw_ref[gi].astype(BF16))
            g = pg[:, sl]
            yp_ref[:, sl] = ((z * ps_ref[:, sl]) * (g * _sigmoid(g))).astype(BF16)
        for ci in range(2, 5):
            qkv_ref[:, (ci - 2) * POOL_WIDTH:(ci - 1) * POOL_WIDTH] = chunk(ci).astype(BF16)
        ag_ref[...] = chunk(5)

    tile = lambda width: pl.BlockSpec((TOKEN_TILE, width), lambda i: (i, 0))
    return pl.pallas_call(
        body,
        name="norm_inproj",
        grid=(t // TOKEN_TILE,),
        out_shape=(
            jax.ShapeDtypeStruct((t, 2 * POOL_WIDTH), F32),
            jax.ShapeDtypeStruct((t, 3 * ATTN_WIDTH), BF16),
            jax.ShapeDtypeStruct((t, ATTN_WIDTH), F32),
            jax.ShapeDtypeStruct((t, POOL_WIDTH), BF16),
        ),
        in_specs=[
            tile(D_MODEL),
            pl.BlockSpec((1, D_MODEL), lambda i: (0, 0)),
            pl.BlockSpec(memory_space=pl.ANY),
            pl.BlockSpec((N_GROUPS, GROUP_DIM, GROUP_DIM), lambda i: (0, 0, 0)),
            pl.BlockSpec((1, POOL_WIDTH), lambda i: (0, 0)),
        ],
        out_specs=(tile(2 * POOL_WIDTH), tile(3 * ATTN_WIDTH), tile(ATTN_WIDTH), tile(POOL_WIDTH)),
        scratch_shapes=[
            pltpu.VMEM((D_MODEL, IN_WIDTH), BF16),
            pltpu.VMEM((HALO, POOL_WIDTH), F32),
            pltpu.SemaphoreType.DMA((N_DEV,)),
        ],
        compiler_params=_params(("arbitrary",)),
    )(x2d, g1, wg, pool_w, pool_scale)


def _inv_count(first_row, rows, window):
    tpos = first_row + lax.broadcasted_iota(jnp.int32, (rows, 1), 0)
    return 1.0 / jnp.minimum(tpos + 1, window).astype(F32)


def _causal_window_sum(ext, window):
    s, k = ext, 1
    while k < window:
        s = s + pltpu.roll(s, k, 0)
        k *= 2
    return s


def _pool_diffs(pv, halo, first_row, window):
    s = _causal_window_sum(jnp.concatenate([halo, pv], axis=0), window)[HALO:]
    return s * _inv_count(first_row, pv.shape[0], window) - pv


def _pool_bwd_group(gi, i, n_tiles, cur_ref, prev_ref, pgn_ref, dy_ref, dyn_ref, pw_ref, ps_ref,
                    dp_ref, dpw_ref, dps_ref):
    w = POOL_WINDOWS[gi]
    sl = slice(gi * GROUP_DIM, (gi + 1) * GROUP_DIM)
    gate_sl = slice(POOL_WIDTH + gi * GROUP_DIM, POOL_WIDTH + (gi + 1) * GROUP_DIM)
    rows = TOKEN_TILE + HALO
    pw = pw_ref[gi].astype(BF16)
    ps = ps_ref[:, sl]
    d = _pool_diffs(cur_ref[:, sl], jnp.where(i > 0, prev_ref[:, sl], 0.0), i * TOKEN_TILE, w).astype(BF16)
    z = _nn(d, pw)
    g_ext = jnp.concatenate([cur_ref[:, gate_sl], pgn_ref[:, sl]], axis=0)
    dy_ext = jnp.concatenate([dy_ref[:, sl], dyn_ref[:, sl]], axis=0).astype(F32)
    sig = _sigmoid(g_ext)
    gate = g_ext * sig
    dz_ext = ((dy_ext * gate) * ps).astype(BF16)
    dd_ext = _nt(dz_ext, pw)
    yield
    e = dd_ext * _inv_count(i * TOKEN_TILE, rows, w)
    row = lax.broadcasted_iota(jnp.int32, (rows, 1), 0)
    e = jnp.where(jnp.logical_or(row < TOKEN_TILE, i < n_tiles - 1), e, 0.0)
    s, k = e, 1
    while k < w:
        s = s + pltpu.roll(s, rows - k, 0)
        k *= 2
    dp_ref[:, sl] = (s[:TOKEN_TILE] - dd_ext[:TOKEN_TILE]).astype(BF16)
    dy = dy_ext[:TOKEN_TILE]
    g = g_ext[:TOKEN_TILE]
    sg = sig[:TOKEN_TILE]
    dgate = sg * (1.0 + g * (1.0 - sg))
    dp_ref[:, gate_sl] = ((dy * (z * ps)) * dgate).astype(BF16)
    dps_ref[:, sl] += jnp.sum((dy * gate[:TOKEN_TILE]) * z, axis=0, keepdims=True)
    dpw_ref[gi] += _tn(d, dz_ext[:TOKEN_TILE])


_REL_FIRST = KV_WINDOW - 1 - MAX_REL


def _skew_rows(a, right):
    rows, lanes = a.shape
    row = lax.broadcasted_iota(jnp.int32, a.shape, 0)
    for b in range(rows.bit_length() - 1):
        shift = (1 << b) if right else lanes - (1 << b)
        a = jnp.where((row >> b) & 1 == 1, pltpu.roll(a, shift, 1), a)
    return a


def _toeplitz_bias(line):
    a = jnp.broadcast_to(line, (Q_BLOCK, TOEPLITZ))
    a = _skew_rows(a, True)
    a = pltpu.roll(a, TOEPLITZ - (Q_BLOCK - 1), 1)
    a = a[:, :KV_WINDOW]
    qc = lax.broadcasted_iota(jnp.int32, a.shape, 0) // CHUNK
    kc = lax.broadcasted_iota(jnp.int32, a.shape, 1) // CHUNK
    visible = jnp.logical_and(kc >= qc, kc <= qc + LEFT_CHUNKS)
    return jnp.where(visible, a, MASK_VALUE)


_BAND_WIDTH = 2 * LANES
_BAND_START = (384, 512, 512, 512)
_BAND_REL = tuple(a - (KV_BLOCKS - 1) * Q_BLOCK - qc * CHUNK for qc, a in enumerate(_BAND_START))


_ROW_G1, _ROW_G2, _ROW_PS, _ROW_LOSS, _ROW_RB = 0, 1, 2, 3, 8
_VEC_ROWS = 16


def _pack_small(band, total, d_g2, d_ps, loss_row):
    def body(band_ref, total_ref, g2_ref, ps_ref, loss_ref, vec_ref):
        vec_ref[...] = jnp.zeros_like(vec_ref)
        vec_ref[_ROW_G2:_ROW_G2 + 1, :] = g2_ref[...]
        vec_ref[_ROW_PS:_ROW_PS + 1, :POOL_WIDTH] = ps_ref[...]
        vec_ref[_ROW_LOSS:_ROW_LOSS + 1, :LANES] = loss_ref[0:1, :]
        for h in range(N_HEADS):
            a = jnp.zeros((CHUNK, 2 * _BAND_WIDTH), F32)
            for qc in range(Q_BLOCK // CHUNK):
                z = jnp.concatenate([band_ref[h, qc], jnp.zeros((CHUNK, _BAND_WIDTH), F32)], axis=1)
                left = -MAX_REL - _BAND_REL[qc]
                a = a + (pltpu.roll(z, 2 * _BAND_WIDTH - left, 1) if left else z)
            a = _skew_rows(a, False)
            near = jnp.sum(a, axis=0, keepdims=True)[:, :2 * LANES]
            r = lax.broadcasted_iota(jnp.int32, near.shape, 1)
            near = jnp.where(jnp.logical_and(r >= 1, r < 2 * MAX_REL), near, 0.0)
            everything = jnp.sum(jnp.sum(total_ref[h], axis=0, keepdims=True), axis=1, keepdims=True)
            far = everything - jnp.sum(near, axis=1, keepdims=True)
            vec_ref[_ROW_RB + h:_ROW_RB + h + 1, :2 * LANES] = jnp.where(r == 0, far, near)

    vm = pl.BlockSpec(memory_space=pltpu.VMEM)
    return pl.pallas_call(
        body,
        name="pack_small",
        out_shape=jax.ShapeDtypeStruct((_VEC_ROWS, D_MODEL), F32),
        in_specs=[vm] * 5,
        out_specs=vm,
        compiler_params=_params(),
    )(band, total, d_g2, d_ps, loss_row)


def _head_lanes(hh):
    lane = lax.broadcasted_iota(jnp.int32, (1, PAIR), 1)
    return (lane < HEAD_DIM) if hh == 0 else (lane >= HEAD_DIM)


def _score_windows(nwin):
    if nwin < KV_BLOCKS:
        return [(slice(0, Q_BLOCK), slice(0, nwin * Q_BLOCK), None)]
    pieces = []
    for qc in range(Q_BLOCK // CHUNK):
        rows = slice(qc * CHUNK, (qc + 1) * CHUNK)
        if qc < 2:
            pieces.append((rows, slice(0, KV_WINDOW - LANES), slice(KV_WINDOW - LANES, KV_WINDOW)))
        else:
            pieces.append((rows, slice(LANES, KV_WINDOW), slice(0, LANES)))
    return pieces


def _block_rows(first_block, n_blocks):
    if isinstance(first_block, int):
        return pl.ds(first_block * Q_BLOCK, n_blocks * Q_BLOCK)
    return pl.ds(pl.multiple_of(first_block * Q_BLOCK, Q_BLOCK), n_blocks * Q_BLOCK)


def _attn_fwd(qkv, bias_tile, w_out_shard):
    t = qkv.shape[0]
    nb = t // Q_BLOCK

    def body(q_ref, k_ref, v_ref, bias_ref, wout_ref, o_ref, gout_hbm,
             s_scr, p_scr, sout_ref, send_sems, recv_sems, local_sem):
        pair = pl.program_id(0)
        me = _dev_index(_mesh_pos())
        mine = pltpu.make_async_copy(sout_ref, gout_hbm.at[me], local_sem)

        def shard_copy(r, block):
            return pltpu.make_async_remote_copy(
                src_ref=sout_ref, dst_ref=gout_hbm.at[block], send_sem=send_sems.at[r - 1],
                recv_sem=recv_sems.at[r - 1], device_id=_peer(r), device_id_type=MESH_ID)

        @pl.when(pair == 0)
        def _():
            sout_ref[...] = wout_ref[...].astype(BF16)
            mine.start()
            for r in range(1, N_DEV):
                shard_copy(r, me).start()

        def scores(i, slot, nwin):
            q = q_ref[_block_rows(i, 1), :]
            kw = k_ref[_block_rows(i + 1 - nwin, nwin), :]
            for hh in range(2):
                q_h = jnp.where(_head_lanes(hh), q, jnp.zeros_like(q)) * ATTN_SCALE
                s_scr[slot, hh, :, :nwin * Q_BLOCK] = _nt(q_h, kw)

        def softmax(slot, nwin):
            off = (KV_BLOCKS - nwin) * Q_BLOCK
            for hh in range(2):
                for rows, cols, rest in _score_windows(nwin):
                    s = s_scr[slot, hh, rows, cols] + bias_ref[hh, rows, off + cols.start:off + cols.stop]
                    p_scr[slot, hh, rows, cols] = jnp.exp(s - jnp.max(s, axis=-1, keepdims=True)).astype(BF16)
                    if rest is not None:
                        p_scr[slot, hh, rows, rest] = jnp.zeros((CHUNK, LANES), BF16)

        def output(i, slot, nwin):
            vw = v_ref[_block_rows(i + 1 - nwin, nwin), :]
            outs = [_nn(p_scr[slot, hh, :, :nwin * Q_BLOCK], jnp.where(_head_lanes(hh), vw, jnp.ones_like(vw)))
                    for hh in range(2)]
            sums = pltpu.roll(jnp.where(_head_lanes(0), outs[1], outs[0]), HEAD_DIM, axis=1)
            o_ref[_block_rows(i, 1), :] = (jnp.where(_head_lanes(0), outs[0], outs[1]) * (1.0 / sums)).astype(BF16)

        scores(0, 0, 1)
        scores(1, 1, 2)
        softmax(0, 1)
        scores(2, 0, 3)
        softmax(1, 2)
        output(0, 0, 1)
        scores(3, 1, 3)
        softmax(0, 3)
        output(1, 1, 2)

        def two_steps(k, carry):
            i = 3 + 2 * k
            scores(i + 1, 0, KV_BLOCKS)
            softmax(1, KV_BLOCKS)
            output(i - 1, 0, KV_BLOCKS)
            scores(i + 2, 1, KV_BLOCKS)
            softmax(0, KV_BLOCKS)
            output(i, 1, KV_BLOCKS)
            return carry

        lax.fori_loop(0, (nb - 4) // 2, two_steps, 0, unroll=2)
        last = (nb - 1) % 2
        softmax(last, KV_BLOCKS)
        output(nb - 2, 1 - last, KV_BLOCKS)
        output(nb - 1, last, KV_BLOCKS)

        @pl.when(pair == N_PAIRS - 1)
        def _():
            for r in range(1, N_DEV):
                shard_copy(r, _dev_index(_peer(r))).wait_recv()
            for r in range(1, N_DEV):
                shard_copy(r, me).wait_send()
            mine.wait()

    col = lambda c0: pl.BlockSpec((t, PAIR), lambda j: (0, c0 + j))
    return pl.pallas_call(
        body,
        name="attn_fwd",
        grid=(N_PAIRS,),
        out_shape=(
            jax.ShapeDtypeStruct((t, ATTN_WIDTH), BF16),
            jax.ShapeDtypeStruct((N_DEV, OUT_SHARD, D_MODEL), BF16),
        ),
        in_specs=[col(0), col(N_PAIRS), col(2 * N_PAIRS),
                  pl.BlockSpec((2, Q_BLOCK, KV_WINDOW), lambda j: (j, 0, 0)),
                  pl.BlockSpec((OUT_SHARD, D_MODEL), lambda j: (0, 0))],
        out_specs=(col(0), pl.BlockSpec(memory_space=pl.ANY)),
        scratch_shapes=[
            pltpu.VMEM((2, 2, Q_BLOCK, KV_WINDOW), F32),
            pltpu.VMEM((2, 2, Q_BLOCK, KV_WINDOW), BF16),
            pltpu.VMEM((OUT_SHARD, D_MODEL), BF16),
            pltpu.SemaphoreType.DMA((N_DEV - 1,)),
            pltpu.SemaphoreType.DMA((N_DEV - 1,)),
            pltpu.SemaphoreType.DMA,
        ],
        compiler_params=_params(("arbitrary",)),
    )(qkv, qkv, qkv, bias_tile, w_out_shard)


def _attn_bwd(qkv, do, bias_tile, dwout_g):
    t = qkv.shape[0]
    nb = t // Q_BLOCK

    def body(q_ref, k_ref, v_ref, do_ref, bias_ref, dwout_hbm,
             dq_ref, dk_ref, dv_ref, band_ref, total_ref, land_hbm,
             s_scr, dp_scr, p_scr, dsb_scr, dq_scr, dk_acc, dv_acc, send_sems, recv_sems):
        exchange = _exchange_copies(dwout_hbm, land_hbm, send_sems, recv_sems)

        @pl.when(pl.program_id(0) == 0)
        def _():
            for cp in exchange:
                cp.start()

        band_ref[...] = jnp.zeros_like(band_ref)
        total_ref[...] = jnp.zeros_like(total_ref)

        def nwin_of(i):
            return min(i + 1, KV_BLOCKS) if isinstance(i, int) else KV_BLOCKS

        def operands(i, hh):
            lanes = _head_lanes(hh)
            q = q_ref[_block_rows(i, 1), :]
            do = do_ref[_block_rows(i, 1), :]
            return (jnp.where(lanes, q, jnp.zeros_like(q)) * ATTN_SCALE, jnp.where(lanes, do, jnp.zeros_like(do)))

        def products(i, hh):
            nwin = nwin_of(i)
            win = _block_rows(i + 1 - nwin, nwin)
            q_h, do_h = operands(i, hh)
            s_scr[hh, :, :nwin * Q_BLOCK] = _nt(q_h, k_ref[win, :])
            dp_scr[hh, :, :nwin * Q_BLOCK] = _nt(do_h, v_ref[win, :])

        def grads(i, hh):
            nwin = nwin_of(i)
            off = (KV_BLOCKS - nwin) * Q_BLOCK
            for rows, cols, rest in _score_windows(nwin):
                bias_cols = slice(off + cols.start, off + cols.stop)
                s = s_scr[hh, rows, cols] + bias_ref[hh, rows, bias_cols]
                e = jnp.exp(s - jnp.max(s, axis=-1, keepdims=True))
                p = e * (1.0 / jnp.sum(e, axis=-1, keepdims=True))
                dp = dp_scr[hh, rows, cols]
                ds = p * (dp - jnp.sum(p * dp, axis=-1, keepdims=True))
                total_ref[hh, rows, :] += sum(ds[:, c0:c0 + LANES] for c0 in range(0, ds.shape[1], LANES))
                for qc in range(rows.start // CHUNK, rows.stop // CHUNK):
                    lo = max(_BAND_START[qc], bias_cols.start)
                    hi = min(_BAND_START[qc] + _BAND_WIDTH, bias_cols.stop)
                    if lo < hi:
                        band_ref[hh, qc, :, lo - _BAND_START[qc]:hi - _BAND_START[qc]] += ds[
                            qc * CHUNK - rows.start:(qc + 1) * CHUNK - rows.start,
                            lo - bias_cols.start:hi - bias_cols.start]
                p_scr[hh, rows, cols] = p.astype(BF16)
                dsb_scr[hh, rows, cols] = ds.astype(BF16)
                if rest is not None:
                    p_scr[hh, rows, rest] = jnp.zeros((CHUNK, LANES), BF16)
                    dsb_scr[hh, rows, rest] = jnp.zeros((CHUNK, LANES), BF16)

        def ring(block):
            return block % KV_BLOCKS if isinstance(block, int) else lax.rem(block, KV_BLOCKS)

        def accumulate(i, hh):
            nwin = nwin_of(i)
            w = nwin * Q_BLOCK
            win = _block_rows(i + 1 - nwin, nwin)
            q_h, do_h = operands(i, hh)
            ds_b = dsb_scr[hh, :, :w]
            dq_h = _nn(ds_b, k_ref[win, :]) * ATTN_SCALE
            dkw = _tn(ds_b, q_h)
            dvw = _tn(p_scr[hh, :, :w], do_h)
            for b in range(nwin):
                slot = ring(i + 1 - nwin + b)
                part = slice(b * Q_BLOCK, (b + 1) * Q_BLOCK)
                if hh == 0 and b == nwin - 1:
                    dk_acc[slot] = dkw[part]
                    dv_acc[slot] = dvw[part]
                else:
                    dk_acc[slot] += dkw[part]
                    dv_acc[slot] += dvw[part]
            if hh == 0:
                dq_scr[...] = dq_h
            else:
                dq_ref[_block_rows(i, 1), :] = jnp.where(_head_lanes(0), dq_scr[...], dq_h).astype(BF16)
                if not (isinstance(i, int) and i < KV_BLOCKS - 1):
                    flush(i - (KV_BLOCKS - 1))

        def flush(block):
            dk_ref[_block_rows(block, 1), :] = dk_acc[ring(block)].astype(BF16)
            dv_ref[_block_rows(block, 1), :] = dv_acc[ring(block)].astype(BF16)

        def tile(n):
            return n // 2, n % 2

        def step(n):
            if n + 1 < 2 * nb:
                products(*tile(n + 1))
            grads(*tile(n))
            if n >= 1:
                accumulate(*tile(n - 1))

        products(0, 0)
        for n in range(2 * KV_BLOCKS):
            step(n)

        def two_steps(i, carry):
            products(i, 1)
            grads(i, 0)
            accumulate(i - 1, 1)
            products(i + 1, 0)
            grads(i, 1)
            accumulate(i, 0)
            return carry

        lax.fori_loop(KV_BLOCKS, nb - 1, two_steps, 0, unroll=2)
        step(2 * nb - 2)
        step(2 * nb - 1)
        accumulate(nb - 1, 1)
        flush(nb - 2)
        flush(nb - 1)

        @pl.when(pl.program_id(0) == N_PAIRS - 1)
        def _():
            for cp in exchange:
                cp.wait_recv()
            for cp in exchange:
                cp.wait_send()

    col = lambda c0: pl.BlockSpec((t, PAIR), lambda j: (0, c0 + j))
    tile_spec = pl.BlockSpec((2, Q_BLOCK, KV_WINDOW), lambda j: (j, 0, 0))
    out = jax.ShapeDtypeStruct((t, ATTN_WIDTH), BF16)
    return pl.pallas_call(
        body,
        name="attn_bwd",
        grid=(N_PAIRS,),
        out_shape=(out, out, out,
                   jax.ShapeDtypeStruct((N_HEADS, Q_BLOCK // CHUNK, CHUNK, _BAND_WIDTH), F32),
                   jax.ShapeDtypeStruct((N_HEADS, Q_BLOCK, LANES), F32),
                   jax.ShapeDtypeStruct((N_DEV - 1, OUT_SHARD, D_MODEL), BF16)),
        in_specs=[col(0), col(N_PAIRS), col(2 * N_PAIRS), col(0), tile_spec, pl.BlockSpec(memory_space=pl.ANY)],
        out_specs=(col(0), col(0), col(0),
                   pl.BlockSpec((2, Q_BLOCK // CHUNK, CHUNK, _BAND_WIDTH), lambda j: (j, 0, 0, 0)),
                   pl.BlockSpec((2, Q_BLOCK, LANES), lambda j: (j, 0, 0)),
                   pl.BlockSpec(memory_space=pl.ANY)),
        scratch_shapes=[
            pltpu.VMEM((2, Q_BLOCK, KV_WINDOW), F32),
            pltpu.VMEM((2, Q_BLOCK, KV_WINDOW), F32),
            pltpu.VMEM((2, Q_BLOCK, KV_WINDOW), BF16),
            pltpu.VMEM((2, Q_BLOCK, KV_WINDOW), BF16),
            pltpu.VMEM((Q_BLOCK, PAIR), F32),
            pltpu.VMEM((KV_BLOCKS, Q_BLOCK, PAIR), F32),
            pltpu.VMEM((KV_BLOCKS, Q_BLOCK, PAIR), F32),
            pltpu.SemaphoreType.DMA((N_DEV - 1,)),
            pltpu.SemaphoreType.DMA((N_DEV - 1,)),
        ],
        compiler_params=_params(("arbitrary",)),
    )(qkv, qkv, qkv, do, bias_tile, dwout_g)


def _outproj_loss(x2d, tgt2d, y_pool, o, ag, wout_g, g2):
    t = x2d.shape[0]
    n_tiles = t // TOKEN_TILE

    def body(x_ref, tgt_ref, yp_ref, o_ref, ag_ref, w_ref, g_ref,
             dx2_ref, dyp_ref, do_ref, dag_ref, dw_ref, dg_ref, loss_ref, acc_ref):
        i = pl.program_id(0)

        @pl.when(i == 0)
        def _():
            acc_ref[...] = jnp.zeros_like(acc_ref)
            dg_ref[...] = jnp.zeros_like(dg_ref)
            loss_ref[...] = jnp.zeros_like(loss_ref)

        w = w_ref[...].reshape(D_MODEL, D_MODEL)
        ga = ag_ref[...]
        sig = _sigmoid(ga)
        gate = ga * sig
        of = o_ref[...].astype(F32)
        y = jnp.concatenate([yp_ref[...], (of * gate).astype(BF16)], axis=1)
        x2 = x_ref[...] + _nn(y, w)
        r = lax.rsqrt(jnp.mean(x2 * x2, axis=-1, keepdims=True) + EPS)
        xh = x2 * r
        g = g_ref[...]
        diff = xh * g - tgt_ref[...]
        tok = jnp.sum(diff * diff, axis=-1, keepdims=True) * (1.0 / D_MODEL)
        loss_ref[...] += jnp.sum(tok, axis=0, keepdims=True)
        dout = diff * (1.0 / D_MODEL)
        dg_ref[...] += jnp.sum(dout * xh, axis=0, keepdims=True)
        u = dout * g
        dx2 = r * (u - xh * jnp.mean(u * xh, axis=-1, keepdims=True))
        dx2_ref[...] = dx2
        dx2_b = dx2.astype(BF16)
        dy = _nt(dx2_b, w)
        dyp_ref[...] = dy[:, :POOL_WIDTH].astype(BF16)
        dya = dy[:, POOL_WIDTH:]
        do_ref[...] = (dya * gate).astype(BF16)
        dag_ref[...] = ((dya * of) * (sig * (1.0 + ga * (1.0 - sig)))).astype(BF16)
        acc_ref[...] += _tn(y, dx2_b)

        @pl.when(i == n_tiles - 1)
        def _():
            dw_ref[...] = acc_ref[...].reshape(N_DEV, OUT_SHARD, D_MODEL).astype(BF16)

    tile = lambda width: pl.BlockSpec((TOKEN_TILE, width), lambda i: (i, 0))
    return pl.pallas_call(
        body,
        name="outproj_loss",
        grid=(n_tiles,),
        out_shape=(
            jax.ShapeDtypeStruct((t, D_MODEL), F32),
            jax.ShapeDtypeStruct((t, POOL_WIDTH), BF16),
            jax.ShapeDtypeStruct((t, ATTN_WIDTH), BF16),
            jax.ShapeDtypeStruct((t, ATTN_WIDTH), BF16),
            jax.ShapeDtypeStruct((N_DEV, OUT_SHARD, D_MODEL), BF16),
            jax.ShapeDtypeStruct((1, D_MODEL), F32),
            jax.ShapeDtypeStruct((8, LANES), F32),
        ),
        in_specs=[
            tile(D_MODEL), tile(D_MODEL), tile(POOL_WIDTH), tile(ATTN_WIDTH), tile(ATTN_WIDTH),
            pl.BlockSpec((N_DEV, OUT_SHARD, D_MODEL), lambda i: (0, 0, 0)),
            pl.BlockSpec((1, D_MODEL), lambda i: (0, 0)),
        ],
        out_specs=(
            tile(D_MODEL), tile(POOL_WIDTH), tile(ATTN_WIDTH), tile(ATTN_WIDTH),
            pl.BlockSpec((N_DEV, OUT_SHARD, D_MODEL), lambda i: (0, 0, 0)),
            pl.BlockSpec((1, D_MODEL), lambda i: (0, 0)),
            pl.BlockSpec((8, LANES), lambda i: (0, 0)),
        ),
        scratch_shapes=[pltpu.VMEM((D_MODEL, D_MODEL), F32)],
        compiler_params=_params(("arbitrary",)),
    )(x2d, tgt2d, y_pool, o, ag, wout_g, g2)


def _dproj_specs():
    tile = lambda width: pl.BlockSpec((TOKEN_TILE, width), lambda i: (i, 0))
    return [tile(2 * POOL_WIDTH)] + [tile(ATTN_WIDTH)] * 4


def _inproj_bwd_dx(x2d, dx2, dproj, g1, wg):
    t = x2d.shape[0]

    def body(x_ref, dx2_ref, dp_ref, dq_ref, dk_ref, dv_ref, dag_ref, g_ref, wg_hbm, gx_ref, dg_ref, wfull_ref, sem):
        @pl.when(pl.program_id(0) == 0)
        def _():
            _load_w_in(wg_hbm, wfull_ref, sem)
            dg_ref[...] = jnp.zeros_like(dg_ref)

        dproj_t = jnp.concatenate([dp_ref[...], dq_ref[...], dk_ref[...], dv_ref[...], dag_ref[...]], axis=1)
        dh = _nt(dproj_t, wfull_ref[...])
        xf = x_ref[...]
        r = lax.rsqrt(jnp.mean(xf * xf, axis=-1, keepdims=True) + EPS)
        xh = xf * r
        dg_ref[...] += jnp.sum(dh * xh, axis=0, keepdims=True)
        u = dh * g_ref[...]
        gx_ref[...] = dx2_ref[...] + r * (u - xh * jnp.mean(u * xh, axis=-1, keepdims=True))

    tile = pl.BlockSpec((TOKEN_TILE, D_MODEL), lambda i: (i, 0))
    return pl.pallas_call(
        body,
        name="inproj_bwd_dx",
        grid=(t // TOKEN_TILE,),
        out_shape=(jax.ShapeDtypeStruct((t, D_MODEL), F32), jax.ShapeDtypeStruct((1, D_MODEL), F32)),
        in_specs=[tile, tile] + _dproj_specs() + [
            pl.BlockSpec((1, D_MODEL), lambda i: (0, 0)),
            pl.BlockSpec(memory_space=pl.ANY),
        ],
        out_specs=(tile, pl.BlockSpec((1, D_MODEL), lambda i: (0, 0))),
        scratch_shapes=[pltpu.VMEM((D_MODEL, IN_WIDTH), BF16), pltpu.SemaphoreType.DMA((N_DEV,))],
        compiler_params=_params(("arbitrary",)),
    )(x2d, dx2, *dproj, g1, wg)


def _inproj_bwd_dw(x2d, pvg, dy_pool, dattn, g1, pool_w, pool_scale):
    t = x2d.shape[0]
    n_tiles = t // TOKEN_TILE
    halo_per_tile = TOKEN_TILE // HALO
    last_halo = t // HALO - 1

    def body(x_ref, cur_ref, prev_ref, pgn_ref, dy_ref, dyn_ref, dq_ref, dk_ref, dv_ref, dag_ref, g_ref, pw_ref, ps_ref,
             out_ref, dp_ref, dpw_ref, dps_ref, acc_ref):
        i = pl.program_id(0)

        @pl.when(i == 0)
        def _():
            acc_ref[...] = jnp.zeros_like(acc_ref)
            dpw_ref[...] = jnp.zeros_like(dpw_ref)
            dps_ref[...] = jnp.zeros_like(dps_ref)

        xf = x_ref[...]
        r = lax.rsqrt(jnp.mean(xf * xf, axis=-1, keepdims=True) + EPS)
        h = ((xf * r) * g_ref[...]).astype(BF16)
        half = ATTN_WIDTH // 2
        for gi, ref in enumerate((dq_ref, dk_ref, dv_ref, dag_ref)):
            pool_halves = _pool_bwd_group(gi, i, n_tiles, cur_ref, prev_ref, pgn_ref, dy_ref, dyn_ref, pw_ref, ps_ref,
                                          dp_ref, dpw_ref, dps_ref)
            for c0 in (0, half):
                col = 2 * POOL_WIDTH + gi * ATTN_WIDTH + c0
                acc_ref[:, col:col + half] += _tn(h, ref[:, c0:c0 + half])
                next(pool_halves, None)
        for c0 in (0, POOL_WIDTH):
            acc_ref[:, c0:c0 + POOL_WIDTH] += _tn(h, dp_ref[:, c0:c0 + POOL_WIDTH])

        @pl.when(i == n_tiles - 1)
        def _():
            for d in range(N_DEV):
                out_ref[d] = acc_ref[:, d * IN_SHARD:(d + 1) * IN_SHARD].astype(BF16)

    tile = lambda width: pl.BlockSpec((TOKEN_TILE, width), lambda i: (i, 0))
    next_halo = lambda col: pl.BlockSpec(
        (HALO, POOL_WIDTH), lambda i: (jnp.minimum((i + 1) * halo_per_tile, last_halo), col))
    return pl.pallas_call(
        body,
        name="inproj_bwd_dw",
        grid=(n_tiles,),
        out_shape=(
            jax.ShapeDtypeStruct((N_DEV, D_MODEL, IN_SHARD), BF16),
            jax.ShapeDtypeStruct((t, 2 * POOL_WIDTH), BF16),
            jax.ShapeDtypeStruct((N_GROUPS, GROUP_DIM, GROUP_DIM), F32),
            jax.ShapeDtypeStruct((1, POOL_WIDTH), F32),
        ),
        in_specs=[
            tile(D_MODEL),
            tile(2 * POOL_WIDTH),
            pl.BlockSpec((HALO, POOL_WIDTH), lambda i: (jnp.maximum(i * halo_per_tile - 1, 0), 0)),
            next_halo(1),
            tile(POOL_WIDTH),
            next_halo(0),
            tile(ATTN_WIDTH), tile(ATTN_WIDTH), tile(ATTN_WIDTH), tile(ATTN_WIDTH),
            pl.BlockSpec((1, D_MODEL), lambda i: (0, 0)),
            pl.BlockSpec((N_GROUPS, GROUP_DIM, GROUP_DIM), lambda i: (0, 0, 0)),
            pl.BlockSpec((1, POOL_WIDTH), lambda i: (0, 0)),
        ],
        out_specs=(
            pl.BlockSpec((N_DEV, D_MODEL, IN_SHARD), lambda i: (0, 0, 0)),
            tile(2 * POOL_WIDTH),
            pl.BlockSpec((N_GROUPS, GROUP_DIM, GROUP_DIM), lambda i: (0, 0, 0)),
            pl.BlockSpec((1, POOL_WIDTH), lambda i: (0, 0)),
        ),
        scratch_shapes=[pltpu.VMEM((D_MODEL, IN_WIDTH), F32)],
        compiler_params=_params(("arbitrary",)),
    )(x2d, pvg, pvg, pvg, dy_pool, dy_pool, *dattn, g1, pool_w, pool_scale)


_HBM = pl.BlockSpec(memory_space=pltpu.HBM)
_SEM = pl.BlockSpec(memory_space=pltpu.SEMAPHORE)
_DATAFLOW = pltpu.SideEffectType.DATAFLOW_SIDE_EFFECTING


_N_EXCHANGED = 3


def _exchange_all(refs, send_sems, recv_sems):
    win_hbm, win_land, pw_hbm, pw_land, vec_hbm, vec_land = refs
    return (_exchange_copies(win_hbm, win_land, send_sems, recv_sems)
            + _exchange_copies(pw_hbm, pw_land, send_sems, recv_sems, first_sem=N_DEV - 1)
            + _exchange_copies(vec_hbm, vec_land, send_sems, recv_sems, first_sem=2 * (N_DEV - 1), same_for_all=True))


def _exchange_start(win_blocks, pw_blocks, vec):
    arrays = []
    for a, land_shape in ((win_blocks, (N_DEV - 1,) + win_blocks.shape[1:]),
                          (pw_blocks, (N_DEV - 1,) + pw_blocks.shape[1:]),
                          (vec, (N_DEV - 1,) + vec.shape)):
        arrays += [pltpu.with_memory_space_constraint(a, pltpu.HBM),
                   pltpu.with_memory_space_constraint(lax.empty(land_shape, a.dtype), pltpu.HBM)]

    def body(*refs):
        ins, (send_sems, recv_sems), token = refs[:2 * _N_EXCHANGED], refs[2 * _N_EXCHANGED:2 * _N_EXCHANGED + 2], refs[-1]
        for cp in _exchange_all(ins, send_sems, recv_sems):
            cp.start()
        token[...] = jnp.zeros_like(token)

    sems = pltpu.SemaphoreType.DMA((_N_EXCHANGED * (N_DEV - 1),))
    return pl.pallas_call(
        body,
        name="exchange_start",
        out_shape=(sems, sems, *[pltpu.HBM(a.shape, a.dtype) for a in arrays], jax.ShapeDtypeStruct((8, LANES), F32)),
        in_specs=tuple([_HBM] * len(arrays)),
        out_specs=(_SEM, _SEM, *[_HBM] * len(arrays), pl.BlockSpec(memory_space=pltpu.VMEM)),
        input_output_aliases={k: 2 + k for k in range(len(arrays))},
        compiler_params=pltpu.CompilerParams(has_side_effects=_DATAFLOW),
    )(*arrays)


def _exchange_wait(send_sems, recv_sems, arrays, after):
    def body(*refs):
        ins = refs[:2 * _N_EXCHANGED]
        send_sems, recv_sems = refs[2 * _N_EXCHANGED:2 * _N_EXCHANGED + 2]
        for cp in _exchange_all(ins, send_sems, recv_sems):
            cp.wait_send()
            cp.wait_recv()

    return pl.pallas_call(
        body,
        name="exchange_wait",
        out_shape=tuple(pltpu.HBM(a.shape, a.dtype) for a in arrays),
        in_specs=(*[_HBM] * len(arrays), _SEM, _SEM, pl.BlockSpec(memory_space=pl.ANY)),
        out_specs=tuple([_HBM] * len(arrays)),
        input_output_aliases={k: k for k in range(len(arrays))},
        compiler_params=pltpu.CompilerParams(has_side_effects=_DATAFLOW),
    )(*arrays, send_sems, recv_sems, after)


def _adamw(w, g, m, v):
    m = ADAM_B1 * m + (1.0 - ADAM_B1) * g
    v = ADAM_B2 * v + (1.0 - ADAM_B2) * (g * g)
    m_hat = m / (1.0 - ADAM_B1 ** ADAM_STEP)
    v_hat = v / (1.0 - ADAM_B2 ** ADAM_STEP)
    delta = -ADAM_LR * (m_hat / (jnp.sqrt(v_hat) + ADAM_EPS) + ADAM_WD * w)
    return delta, m, v


def _small_allreduce(d_g1, vec, vec_land, pw_blocks, pw_land):
    def body(g1_ref, vec_ref, vland_ref, pwb_ref, pland_ref, vec_out, pw_out, vparts_ref, pparts_ref, rows_ref,
             slice_ref, send_sems, recv_sems):
        me = _dev_index(_mesh_pos())

        def from_devices(parts_ref, own, land_ref):
            parts_ref[0] = own
            parts_ref[1:] = land_ref[...]
            total = parts_ref[me]
            for s in range(1, N_DEV):
                total = total + parts_ref[me ^ s]
            return total

        slice_ref[...] = from_devices(pparts_ref, pwb_ref[me], pland_ref)

        def send(r, src, dst, k):
            return pltpu.make_async_remote_copy(
                src_ref=src, dst_ref=dst, send_sem=send_sems.at[2 * (r - 1) + k],
                recv_sem=recv_sems.at[2 * (r - 1) + k], device_id=_peer(r), device_id_type=MESH_ID)

        started = [cp for r in range(1, N_DEV)
                   for cp in (send(r, g1_ref, rows_ref.at[r], 0), send(r, slice_ref, pw_out.at[me], 1))]
        for cp in started:
            cp.start()
        rows_ref[0] = g1_ref[...]
        pw_out[me] = slice_ref[...]
        vec_out[...] = from_devices(vparts_ref, vec_ref[...], vland_ref)
        for cp in started:
            cp.wait_recv()
        for cp in started:
            cp.wait_send()
        g1 = rows_ref[me]
        for s in range(1, N_DEV):
            g1 = g1 + rows_ref[me ^ s]
        vec_out[_ROW_G1:_ROW_G1 + 1, :] = g1

    vm = pl.BlockSpec(memory_space=pltpu.VMEM)
    return pl.pallas_call(
        body,
        name="small_allreduce",
        out_shape=(jax.ShapeDtypeStruct((_VEC_ROWS, D_MODEL), F32),
                   jax.ShapeDtypeStruct((N_DEV, GROUP_DIM // 2, GROUP_DIM), F32)),
        in_specs=[vm] * 5,
        out_specs=(vm, vm),
        scratch_shapes=[
            pltpu.VMEM((N_DEV, _VEC_ROWS, D_MODEL), F32),
            pltpu.VMEM((N_DEV, GROUP_DIM // 2, GROUP_DIM), F32),
            pltpu.VMEM((N_DEV, 1, D_MODEL), F32),
            pltpu.VMEM((GROUP_DIM // 2, GROUP_DIM), F32),
            pltpu.SemaphoreType.DMA((2 * (N_DEV - 1),)),
            pltpu.SemaphoreType.DMA((2 * (N_DEV - 1),)),
        ],
        compiler_params=_params(),
    )(d_g1, vec, vec_land, pw_blocks, pw_land)


def _adamw_all(dwin_g, land_in, dwout_g, land_out, vec_sum, pw_sum, weights, big):
    small_shapes = [(1, D_MODEL), (1, D_MODEL), (1, POOL_WIDTH), (N_HEADS, 2 * LANES), (N_GROUPS, GROUP_DIM, GROUP_DIM)]

    def body(*refs):
        refs = list(refs)
        take = lambda n: [refs.pop(0) for _ in range(n)]
        dwin_hbm, lin_ref, dwout_hbm, lout_ref, vec_ref, pw_ref = take(6)
        small_wmv = [take(3) for _ in range(5)]
        big_wmv = [take(3) for _ in range(2)]
        big_out = [take(4) for _ in range(2)]
        small_out = [take(4) for _ in range(5)]
        own_in, own_out, local_sems = refs

        me = _dev_index(_mesh_pos())
        mine = [pltpu.make_async_copy(dwin_hbm.at[me], own_in, local_sems.at[0]),
                pltpu.make_async_copy(dwout_hbm.at[me], own_out, local_sems.at[1])]
        for cp in mine:
            cp.start()

        def update(g, wmv, outs):
            delta, m_new, v_new = _adamw(wmv[0][...], g, wmv[1][...], wmv[2][...])
            for ref, val in zip(outs, (g, delta, m_new, v_new)):
                ref[...] = val

        update(vec_ref[_ROW_G1:_ROW_G1 + 1, :], small_wmv[0], small_out[0])
        update(vec_ref[_ROW_G2:_ROW_G2 + 1, :], small_wmv[1], small_out[1])
        update(vec_ref[_ROW_PS:_ROW_PS + 1, :POOL_WIDTH], small_wmv[2], small_out[2])
        update(vec_ref[_ROW_RB:_ROW_RB + N_HEADS, :2 * LANES], small_wmv[3], small_out[3])
        update(pw_ref[...], small_wmv[4], small_out[4])
        for cp in mine:
            cp.wait()
        g_in = own_in[...].astype(F32)
        g_out = own_out[...].astype(F32)
        for r in range(N_DEV - 1):
            g_in = g_in + lin_ref[r].astype(F32)
            g_out = g_out + lout_ref[r].astype(F32)
        update(g_in, big_wmv[0], big_out[0])
        update(g_out, big_wmv[1], big_out[1])

    vm = pl.BlockSpec(memory_space=pltpu.VMEM)
    hbm = pl.BlockSpec(memory_space=pl.ANY)
    f32 = lambda shape: jax.ShapeDtypeStruct(shape, F32)
    out_shapes = [f32((D_MODEL, IN_SHARD))] * 4 + [f32((OUT_SHARD, D_MODEL))] * 4
    for shape in small_shapes:
        out_shapes += [f32(shape)] * 4
    args = [dwin_g, land_in, dwout_g, land_out, vec_sum, pw_sum]
    for wmv in weights:
        args += list(wmv)
    for wmv in big:
        args += list(wmv)
    return pl.pallas_call(
        body,
        name="adamw_all",
        out_shape=tuple(out_shapes),
        in_specs=[hbm, vm, hbm, vm] + [vm] * (len(args) - 4),
        out_specs=tuple([vm] * len(out_shapes)),
        scratch_shapes=[
            pltpu.VMEM((D_MODEL, IN_SHARD), BF16),
            pltpu.VMEM((OUT_SHARD, D_MODEL), BF16),
            pltpu.SemaphoreType.DMA((2,)),
        ],
        compiler_params=_params(),
    )(*args)


def kernel(x, norm_gain, w_in, pool_w, pool_scale, rel_bias, w_out, final_norm_gain, loss_target, m_norm_gain, m_w_in, m_pool_w, m_pool_scale, m_rel_bias, m_w_out, m_final_norm_gain, v_norm_gain, v_w_in, v_pool_w, v_pool_scale, v_rel_bias, v_w_out, v_final_norm_gain):
    t = x.shape[1]
    assert x.shape[0] == 1 and t % TOKEN_TILE == 0 and t // Q_BLOCK >= 4
    x2d = x[0]
    tgt2d = loss_target[0]
    g2 = final_norm_gain.reshape(1, D_MODEL)

    rb = rel_bias[0]
    rel_line = jnp.concatenate([
        jnp.broadcast_to(rb[:, :1], (N_HEADS, _REL_FIRST)), rb,
        jnp.broadcast_to(rb[:, N_REL - 1:], (N_HEADS, TOEPLITZ - _REL_FIRST - N_REL)),
    ], axis=1).reshape(N_HEADS, 1, TOEPLITZ)
    wg_in, bias_tile = _gather_weights(w_in[0], rel_line)

    pvg, qkv, ag, y_pool = _norm_inproj(x2d, norm_gain, wg_in, pool_w[0], pool_scale)
    o, wg_out = _attn_fwd(qkv, bias_tile, w_out[0])
    dx2, dy_pool, do, dag, dwout_g, d_g2, loss_sum = _outproj_loss(x2d, tgt2d, y_pool, o, ag, wg_out, g2)
    dq, dk, dv, ds_band, ds_total, land_out = _attn_bwd(qkv, do, bias_tile, dwout_g)
    dwin_g, d_pool, d_pw, d_ps = _inproj_bwd_dw(x2d, pvg, dy_pool, (dq, dk, dv, dag), norm_gain, pool_w[0], pool_scale)
    vec = _pack_small(ds_band, ds_total, d_g2, d_ps, loss_sum)
    dproj = (d_pool, dq, dk, dv, dag)
    pw_blocks = d_pw.reshape(N_DEV, GROUP_DIM // 2, GROUP_DIM)
    send_sems, recv_sems, *exchanged, token = _exchange_start(dwin_g, pw_blocks, vec)
    grad_x, d_g1 = _inproj_bwd_dx(x2d, dx2, dproj, norm_gain + token[:1, :1], wg_in)
    dwin_g, land_in, pw_blocks, pw_land, vec, vec_land = _exchange_wait(send_sems, recv_sems, exchanged, d_g1)

    pad_rb = lambda a: jnp.pad(a[0], ((0, 0), (0, 2 * LANES - N_REL)))
    row = lambda a: a.reshape(1, D_MODEL)
    weights = [
        (norm_gain, m_norm_gain, v_norm_gain),
        (row(final_norm_gain), row(m_final_norm_gain), row(v_final_norm_gain)),
        (pool_scale, m_pool_scale, v_pool_scale),
        (pad_rb(rel_bias), pad_rb(m_rel_bias), pad_rb(v_rel_bias)),
        (pool_w[0], m_pool_w[0], v_pool_w[0]),
    ]
    big = [(w_in[0], m_w_in[0], v_w_in[0]), (w_out[0], m_w_out[0], v_w_out[0])]
    vec_sum, pw_sum = _small_allreduce(d_g1, vec, vec_land, pw_blocks, pw_land)
    res = _adamw_all(dwin_g, land_in, dwout_g, land_out, vec_sum, pw_sum.reshape(N_GROUPS, GROUP_DIM, GROUP_DIM),
                     weights, big)
    loss = 0.5 * vec_sum[_ROW_LOSS, 0]

    def leaves(k):
        g1_, g2_, ps_, rb_, pw_ = (res[8 + 4 * leaf + k] for leaf in range(5))
        return [g1_, res[k][None], pw_[None], ps_, rb_[None, :, :N_REL], res[4 + k][None], g2_.reshape(D_MODEL)]

    return (loss, grad_x[None], *leaves(0), *leaves(1), *leaves(2), *leaves(3))
```

```python
import math

import jax
import jax.numpy as jnp
from jax import lax
from jax.experimental import pallas as pl
from jax.experimental.pallas import tpu as pltpu

F32 = jnp.float32
BF16 = jnp.bfloat16
MESH_ID = pl.DeviceIdType.MESH

D_MODEL = 1024
POOL_WIDTH = 512
ATTN_WIDTH = 512
POOL_WINDOWS = (2, 4, 8, 16)
N_GROUPS = 4
GROUP_DIM = 128
HEAD_DIM = 64
N_HEADS = 8
CHUNK = 64
LEFT_CHUNKS = 8
MAX_REL = 64
N_REL = 2 * MAX_REL + 1
IN_WIDTH = 2 * POOL_WIDTH + 4 * ATTN_WIDTH
EPS = 1e-6
MASK_VALUE = -1e30
ATTN_SCALE = 1.0 / math.sqrt(HEAD_DIM)
ADAM_LR = 0.001
ADAM_B1 = 0.9
ADAM_B2 = 0.999
ADAM_EPS = 1e-08
ADAM_WD = 0.01
ADAM_STEP = 10

N_DEV = 8
IN_SHARD = IN_WIDTH // N_DEV
OUT_SHARD = D_MODEL // N_DEV

LANES = 128
TOKEN_TILE = 512
HALO = 16
Q_BLOCK = 256
KV_BLOCKS = 3
KV_WINDOW = KV_BLOCKS * Q_BLOCK
PAIR = 2 * HEAD_DIM
N_PAIRS = N_HEADS // 2
TOEPLITZ = 1024
VMEM_LIMIT = 56 * 1024 * 1024


def _params(sem=None, vmem=VMEM_LIMIT):
    return pltpu.CompilerParams(dimension_semantics=sem, vmem_limit_bytes=vmem)


def _sigmoid(x):
    return 1.0 / (1.0 + jnp.exp(-x))


def _nt(a, b):
    return lax.dot_general(a, b, (((1,), (1,)), ((), ())), preferred_element_type=F32)


def _tn(a, b):
    return lax.dot_general(a, b, (((0,), (0,)), ((), ())), preferred_element_type=F32)


def _nn(a, b):
    return jnp.dot(a, b, preferred_element_type=F32)


def _mesh_pos():
    return lax.axis_index("x"), lax.axis_index("y"), lax.axis_index("c")


def _dev_index(p):
    return 4 * p[0] + 2 * p[1] + p[2]


def _peer(r):
    x, y, c = _mesh_pos()
    return (x ^ ((r >> 2) & 1), y ^ ((r >> 1) & 1), c ^ (r & 1))


def _exchange_copies(src_hbm, land_hbm, send_sems, recv_sems, first_sem=0, same_for_all=False):
    return [
        pltpu.make_async_remote_copy(
            src_ref=src_hbm if same_for_all else src_hbm.at[_dev_index(_peer(r))], dst_ref=land_hbm.at[r - 1],
            send_sem=send_sems.at[first_sem + r - 1], recv_sem=recv_sems.at[first_sem + r - 1],
            device_id=_peer(r), device_id_type=MESH_ID)
        for r in range(1, N_DEV)
    ]


def _gather_weights(w_in_shard, rel_line):
    def body(win_ref, line_ref, gin_ref, bias_ref, sin_ref, send_sems, recv_sems):
        x, y, c = _mesh_pos()
        me, sibling = (x, y, c), (x, y, 1 - c)
        chips = [(1 - x, y), (x, 1 - y), (1 - x, 1 - y)]

        sin_ref[...] = win_ref[...].astype(BF16)
        gin_ref[_dev_index(me)] = sin_ref[...]

        def copy(k, block, to, from_shard=False):
            return pltpu.make_async_remote_copy(
                src_ref=sin_ref if from_shard else gin_ref.at[_dev_index(block)],
                dst_ref=gin_ref.at[_dev_index(block)],
                send_sem=send_sems.at[k],
                recv_sem=recv_sems.at[k],
                device_id=to,
                device_id_type=MESH_ID,
            )

        first = [copy(0, me, sibling, True)]
        first += [copy(1 + j, me, (*chip, c), True) for j, chip in enumerate(chips)]
        for cp in first:
            cp.start()
        passed = [copy(4 + j, (*chip, c), sibling) for j, chip in enumerate(chips)]

        def bias_heads(lo, hi):
            for h in range(lo, hi):
                bias_ref[h] = _toeplitz_bias(line_ref[h])

        bias_heads(0, N_HEADS - 3)
        for j, chip in enumerate(chips):
            copy(1 + j, (*chip, c), me).wait_recv()
            passed[j].start()
            bias_heads(N_HEADS - 3 + j, N_HEADS - 2 + j)
        copy(0, sibling, me).wait_recv()
        for j, chip in enumerate(chips):
            copy(4 + j, (*chip, 1 - c), me).wait_recv()
        for cp in first + passed:
            cp.wait_send()

    vm = pl.BlockSpec(memory_space=pltpu.VMEM)
    return pl.pallas_call(
        body,
        name="gather_weights",
        out_shape=(
            jax.ShapeDtypeStruct((N_DEV, D_MODEL, IN_SHARD), BF16),
            jax.ShapeDtypeStruct((N_HEADS, Q_BLOCK, KV_WINDOW), F32),
        ),
        in_specs=[vm, vm],
        out_specs=(vm, vm),
        scratch_shapes=[
            pltpu.VMEM((D_MODEL, IN_SHARD), BF16),
            pltpu.SemaphoreType.DMA((7,)),
            pltpu.SemaphoreType.DMA((7,)),
        ],
        compiler_params=_params(),
    )(w_in_shard, rel_line)


def _load_w_in(wg_hbm, wfull_ref, sem):
    copies = [
        pltpu.make_async_copy(wg_hbm.at[d], wfull_ref.at[:, d * IN_SHARD:(d + 1) * IN_SHARD], sem.at[d])
        for d in range(N_DEV)
    ]
    for cp in copies:
        cp.start()
    for cp in copies:
        cp.wait()


def _norm_inproj(x2d, g1, wg, pool_w, pool_scale):
    t = x2d.shape[0]

    def body(x_ref, g_ref, wg_hbm, pw_ref, ps_ref, pvg_ref, qkv_ref, ag_ref, yp_ref, wfull_ref, halo_ref, sem):
        i = pl.program_id(0)

        @pl.when(i == 0)
        def _():
            _load_w_in(wg_hbm, wfull_ref, sem)
            halo_ref[...] = jnp.zeros_like(halo_ref)

        xf = x_ref[...]
        r = lax.rsqrt(jnp.mean(xf * xf, axis=-1, keepdims=True) + EPS)
        h = ((xf * r) * g_ref[...]).astype(BF16)
        chunk = lambda ci: _nn(h, wfull_ref[:, ci * POOL_WIDTH:(ci + 1) * POOL_WIDTH])
        pv, pg = chunk(0), chunk(1)
        pvg_ref[:, :POOL_WIDTH] = pv
        pvg_ref[:, POOL_WIDTH:] = pg
        halo = halo_ref[...]
        halo_ref[...] = pv[TOKEN_TILE - HALO:]
        for gi in range(N_GROUPS):
            if gi < 3:
                qkv_ref[:, gi * ATTN_WIDTH:(gi + 1) * ATTN_WIDTH] = chunk(2 + gi).astype(BF16)
            else:
                ag_ref[...] = chunk(5)
            sl = slice(gi * GROUP_DIM, (gi + 1) * GROUP_DIM)
            d = _pool_diffs(pv[:, sl], halo[:, sl], i * TOKEN_TILE, POOL_WINDOWS[gi])
            z = _nn(d.astype(BF16), pw_ref[gi].astype(BF16))
            g = pg[:, sl]
            yp_ref[:, sl] = ((z * ps_ref[:, sl]) * (g * _sigmoid(g))).astype(BF16)

    tile = lambda width: pl.BlockSpec((TOKEN_TILE, width), lambda i: (i, 0))
    return pl.pallas_call(
        body,
        name="norm_inproj",
        grid=(t // TOKEN_TILE,),
        out_shape=(
            jax.ShapeDtypeStruct((t, 2 * POOL_WIDTH), F32),
            jax.ShapeDtypeStruct((t, 3 * ATTN_WIDTH), BF16),
            jax.ShapeDtypeStruct((t, ATTN_WIDTH), F32),
            jax.ShapeDtypeStruct((t, POOL_WIDTH), BF16),
        ),
        in_specs=[
            tile(D_MODEL),
            pl.BlockSpec((1, D_MODEL), lambda i: (0, 0)),
            pl.BlockSpec(memory_space=pl.ANY),
            pl.BlockSpec((N_GROUPS, GROUP_DIM, GROUP_DIM), lambda i: (0, 0, 0)),
            pl.BlockSpec((1, POOL_WIDTH), lambda i: (0, 0)),
        ],
        out_specs=(tile(2 * POOL_WIDTH), tile(3 * ATTN_WIDTH), tile(ATTN_WIDTH), tile(POOL_WIDTH)),
        scratch_shapes=[
            pltpu.VMEM((D_MODEL, IN_WIDTH), BF16),
            pltpu.VMEM((HALO, POOL_WIDTH), F32),
            pltpu.SemaphoreType.DMA((N_DEV,)),
        ],
        compiler_params=_params(("arbitrary",)),
    )(x2d, g1, wg, pool_w, pool_scale)


def _inv_count(first_row, rows, window):
    tpos = first_row + lax.broadcasted_iota(jnp.int32, (rows, 1), 0)
    return 1.0 / jnp.minimum(tpos + 1, window).astype(F32)


def _causal_window_sum(ext, window):
    s, k = ext, 1
    while k < window:
        s = s + pltpu.roll(s, k, 0)
        k *= 2
    return s


def _pool_diffs(pv, halo, first_row, window):
    s = _causal_window_sum(jnp.concatenate([halo, pv], axis=0), window)[HALO:]
    return s * _inv_count(first_row, pv.shape[0], window) - pv


def _pool_bwd_group(gi, i, n_tiles, cur_ref, prev_ref, pgn_ref, dy_ref, dyn_ref, pw_ref, ps_ref,
                    dp_ref, dpw_ref, dps_ref):
    w = POOL_WINDOWS[gi]
    sl = slice(gi * GROUP_DIM, (gi + 1) * GROUP_DIM)
    gate_sl = slice(POOL_WIDTH + gi * GROUP_DIM, POOL_WIDTH + (gi + 1) * GROUP_DIM)
    rows = TOKEN_TILE + HALO
    pw = pw_ref[gi].astype(BF16)
    ps = ps_ref[:, sl]
    d = _pool_diffs(cur_ref[:, sl], jnp.where(i > 0, prev_ref[:, sl], 0.0), i * TOKEN_TILE, w).astype(BF16)
    z = _nn(d, pw)
    g_ext = jnp.concatenate([cur_ref[:, gate_sl], pgn_ref[:, sl]], axis=0)
    dy_ext = jnp.concatenate([dy_ref[:, sl], dyn_ref[:, sl]], axis=0).astype(F32)
    sig = _sigmoid(g_ext)
    gate = g_ext * sig
    dz_ext = ((dy_ext * gate) * ps).astype(BF16)
    dd_ext = _nt(dz_ext, pw)
    yield
    e = dd_ext * _inv_count(i * TOKEN_TILE, rows, w)
    row = lax.broadcasted_iota(jnp.int32, (rows, 1), 0)
    e = jnp.where(jnp.logical_or(row < TOKEN_TILE, i < n_tiles - 1), e, 0.0)
    s, k = e, 1
    while k < w:
        s = s + pltpu.roll(s, rows - k, 0)
        k *= 2
    dp_ref[:, sl] = (s[:TOKEN_TILE] - dd_ext[:TOKEN_TILE]).astype(BF16)
    dy = dy_ext[:TOKEN_TILE]
    g = g_ext[:TOKEN_TILE]
    sg = sig[:TOKEN_TILE]
    dgate = sg * (1.0 + g * (1.0 - sg))
    dp_ref[:, gate_sl] = ((dy * (z * ps)) * dgate).astype(BF16)
    dps_ref[:, sl] += jnp.sum((dy * gate[:TOKEN_TILE]) * z, axis=0, keepdims=True)
    dpw_ref[gi] += _tn(d, dz_ext[:TOKEN_TILE])


_REL_FIRST = KV_WINDOW - 1 - MAX_REL


def _skew_rows(a, right):
    rows, lanes = a.shape
    row = lax.broadcasted_iota(jnp.int32, a.shape, 0)
    for b in range(rows.bit_length() - 1):
        shift = (1 << b) if right else lanes - (1 << b)
        a = jnp.where((row >> b) & 1 == 1, pltpu.roll(a, shift, 1), a)
    return a


def _toeplitz_bias(line):
    a = jnp.broadcast_to(line, (Q_BLOCK, TOEPLITZ))
    a = _skew_rows(a, True)
    a = pltpu.roll(a, TOEPLITZ - (Q_BLOCK - 1), 1)
    a = a[:, :KV_WINDOW]
    qc = lax.broadcasted_iota(jnp.int32, a.shape, 0) // CHUNK
    kc = lax.broadcasted_iota(jnp.int32, a.shape, 1) // CHUNK
    visible = jnp.logical_and(kc >= qc, kc <= qc + LEFT_CHUNKS)
    return jnp.where(visible, a, MASK_VALUE)


_BAND_WIDTH = 2 * LANES
_BAND_START = (384, 512, 512, 512)
_BAND_REL = tuple(a - (KV_BLOCKS - 1) * Q_BLOCK - qc * CHUNK for qc, a in enumerate(_BAND_START))


_ROW_G1, _ROW_G2, _ROW_PS, _ROW_LOSS, _ROW_RB = 0, 1, 2, 3, 8
_VEC_ROWS = 16


def _pack_small(band, total, d_g2, d_ps, loss_row):
    def body(band_ref, total_ref, g2_ref, ps_ref, loss_ref, vec_ref):
        vec_ref[...] = jnp.zeros_like(vec_ref)
        vec_ref[_ROW_G2:_ROW_G2 + 1, :] = g2_ref[...]
        vec_ref[_ROW_PS:_ROW_PS + 1, :POOL_WIDTH] = ps_ref[...]
        vec_ref[_ROW_LOSS:_ROW_LOSS + 1, :LANES] = loss_ref[0:1, :]
        for h in range(N_HEADS):
            a = jnp.zeros((CHUNK, 2 * _BAND_WIDTH), F32)
            for qc in range(Q_BLOCK // CHUNK):
                z = jnp.concatenate([band_ref[h, qc], jnp.zeros((CHUNK, _BAND_WIDTH), F32)], axis=1)
                left = -MAX_REL - _BAND_REL[qc]
                a = a + (pltpu.roll(z, 2 * _BAND_WIDTH - left, 1) if left else z)
            a = _skew_rows(a, False)
            near = jnp.sum(a, axis=0, keepdims=True)[:, :2 * LANES]
            r = lax.broadcasted_iota(jnp.int32, near.shape, 1)
            near = jnp.where(jnp.logical_and(r >= 1, r < 2 * MAX_REL), near, 0.0)
            everything = jnp.sum(jnp.sum(total_ref[h], axis=0, keepdims=True), axis=1, keepdims=True)
            far = everything - jnp.sum(near, axis=1, keepdims=True)
            vec_ref[_ROW_RB + h:_ROW_RB + h + 1, :2 * LANES] = jnp.where(r == 0, far, near)

    vm = pl.BlockSpec(memory_space=pltpu.VMEM)
    return pl.pallas_call(
        body,
        name="pack_small",
        out_shape=jax.ShapeDtypeStruct((_VEC_ROWS, D_MODEL), F32),
        in_specs=[vm] * 5,
        out_specs=vm,
        compiler_params=_params(),
    )(band, total, d_g2, d_ps, loss_row)


def _head_lanes(hh):
    lane = lax.broadcasted_iota(jnp.int32, (1, PAIR), 1)
    return (lane < HEAD_DIM) if hh == 0 else (lane >= HEAD_DIM)


def _score_windows(nwin):
    if nwin < KV_BLOCKS:
        return [(slice(0, Q_BLOCK), slice(0, nwin * Q_BLOCK), None)]
    pieces = []
    for qc in range(Q_BLOCK // CHUNK):
        rows = slice(qc * CHUNK, (qc + 1) * CHUNK)
        if qc < 2:
            pieces.append((rows, slice(0, KV_WINDOW - LANES), slice(KV_WINDOW - LANES, KV_WINDOW)))
        else:
            pieces.append((rows, slice(LANES, KV_WINDOW), slice(0, LANES)))
    return pieces


def _block_rows(first_block, n_blocks):
    if isinstance(first_block, int):
        return pl.ds(first_block * Q_BLOCK, n_blocks * Q_BLOCK)
    return pl.ds(pl.multiple_of(first_block * Q_BLOCK, Q_BLOCK), n_blocks * Q_BLOCK)


def _attn_fwd(qkv, bias_tile, w_out_shard):
    t = qkv.shape[0]
    nb = t // Q_BLOCK

    def body(q_ref, k_ref, v_ref, bias_ref, wout_ref, o_ref, gout_hbm,
             s_scr, p_scr, sout_ref, send_sems, recv_sems, local_sem):
        pair = pl.program_id(0)
        me = _dev_index(_mesh_pos())
        mine = pltpu.make_async_copy(sout_ref, gout_hbm.at[me], local_sem)

        def shard_copy(r, block):
            return pltpu.make_async_remote_copy(
                src_ref=sout_ref, dst_ref=gout_hbm.at[block], send_sem=send_sems.at[r - 1],
                recv_sem=recv_sems.at[r - 1], device_id=_peer(r), device_id_type=MESH_ID)

        @pl.when(pair == 0)
        def _():
            sout_ref[...] = wout_ref[...].astype(BF16)
            mine.start()
            for r in range(1, N_DEV):
                shard_copy(r, me).start()

        def scores(i, slot, nwin):
            q = q_ref[_block_rows(i, 1), :]
            kw = k_ref[_block_rows(i + 1 - nwin, nwin), :]
            for hh in range(2):
                q_h = jnp.where(_head_lanes(hh), q, jnp.zeros_like(q)) * ATTN_SCALE
                s_scr[slot, hh, :, :nwin * Q_BLOCK] = _nt(q_h, kw)

        def softmax(slot, nwin):
            off = (KV_BLOCKS - nwin) * Q_BLOCK
            for hh in range(2):
                for rows, cols, rest in _score_windows(nwin):
                    s = s_scr[slot, hh, rows, cols] + bias_ref[hh, rows, off + cols.start:off + cols.stop]
                    p_scr[slot, hh, rows, cols] = jnp.exp(s - jnp.max(s, axis=-1, keepdims=True)).astype(BF16)
                    if rest is not None:
                        p_scr[slot, hh, rows, rest] = jnp.zeros((CHUNK, LANES), BF16)

        def output(i, slot, nwin):
            vw = v_ref[_block_rows(i + 1 - nwin, nwin), :]
            outs = [_nn(p_scr[slot, hh, :, :nwin * Q_BLOCK], jnp.where(_head_lanes(hh), vw, jnp.ones_like(vw)))
                    for hh in range(2)]
            sums = pltpu.roll(jnp.where(_head_lanes(0), outs[1], outs[0]), HEAD_DIM, axis=1)
            o_ref[_block_rows(i, 1), :] = (jnp.where(_head_lanes(0), outs[0], outs[1]) * (1.0 / sums)).astype(BF16)

        scores(0, 0, 1)
        scores(1, 1, 2)
        softmax(0, 1)
        scores(2, 0, 3)
        softmax(1, 2)
        output(0, 0, 1)
        scores(3, 1, 3)
        softmax(0, 3)
        output(1, 1, 2)

        def two_steps(k, carry):
            i = 3 + 2 * k
            scores(i + 1, 0, KV_BLOCKS)
            softmax(1, KV_BLOCKS)
            output(i - 1, 0, KV_BLOCKS)
            scores(i + 2, 1, KV_BLOCKS)
            softmax(0, KV_BLOCKS)
            output(i, 1, KV_BLOCKS)
            return carry

        lax.fori_loop(0, (nb - 4) // 2, two_steps, 0, unroll=7)
        last = (nb - 1) % 2
        softmax(last, KV_BLOCKS)
        output(nb - 2, 1 - last, KV_BLOCKS)
        output(nb - 1, last, KV_BLOCKS)

        @pl.when(pair == N_PAIRS - 1)
        def _():
            for r in range(1, N_DEV):
                shard_copy(r, _dev_index(_peer(r))).wait_recv()
            for r in range(1, N_DEV):
                shard_copy(r, me).wait_send()
            mine.wait()

    col = lambda c0: pl.BlockSpec((t, PAIR), lambda j: (0, c0 + j))
    return pl.pallas_call(
        body,
        name="attn_fwd",
        grid=(N_PAIRS,),
        out_shape=(
            jax.ShapeDtypeStruct((t, ATTN_WIDTH), BF16),
            jax.ShapeDtypeStruct((N_DEV, OUT_SHARD, D_MODEL), BF16),
        ),
        in_specs=[col(0), col(N_PAIRS), col(2 * N_PAIRS),
                  pl.BlockSpec((2, Q_BLOCK, KV_WINDOW), lambda j: (j, 0, 0)),
                  pl.BlockSpec((OUT_SHARD, D_MODEL), lambda j: (0, 0))],
        out_specs=(col(0), pl.BlockSpec(memory_space=pl.ANY)),
        scratch_shapes=[
            pltpu.VMEM((2, 2, Q_BLOCK, KV_WINDOW), F32),
            pltpu.VMEM((2, 2, Q_BLOCK, KV_WINDOW), BF16),
            pltpu.VMEM((OUT_SHARD, D_MODEL), BF16),
            pltpu.SemaphoreType.DMA((N_DEV - 1,)),
            pltpu.SemaphoreType.DMA((N_DEV - 1,)),
            pltpu.SemaphoreType.DMA,
        ],
        compiler_params=_params(("arbitrary",)),
    )(qkv, qkv, qkv, bias_tile, w_out_shard)


def _attn_bwd(qkv, do, bias_tile, dwout_g):
    t = qkv.shape[0]
    nb = t // Q_BLOCK

    def body(q_ref, k_ref, v_ref, do_ref, bias_ref, dwout_hbm,
             dq_ref, dk_ref, dv_ref, band_ref, total_ref, land_hbm,
             s_scr, dp_scr, p_scr, dsb_scr, dq_scr, dk_acc, dv_acc, send_sems, recv_sems):
        exchange = _exchange_copies(dwout_hbm, land_hbm, send_sems, recv_sems)

        @pl.when(pl.program_id(0) == 0)
        def _():
            for cp in exchange:
                cp.start()

        band_ref[...] = jnp.zeros_like(band_ref)
        total_ref[...] = jnp.zeros_like(total_ref)

        def nwin_of(i):
            return min(i + 1, KV_BLOCKS) if isinstance(i, int) else KV_BLOCKS

        def operands(i, hh):
            lanes = _head_lanes(hh)
            q = q_ref[_block_rows(i, 1), :]
            do = do_ref[_block_rows(i, 1), :]
            return (jnp.where(lanes, q, jnp.zeros_like(q)) * ATTN_SCALE, jnp.where(lanes, do, jnp.zeros_like(do)))

        def products(i, hh):
            nwin = nwin_of(i)
            win = _block_rows(i + 1 - nwin, nwin)
            q_h, do_h = operands(i, hh)
            s_scr[hh, :, :nwin * Q_BLOCK] = _nt(q_h, k_ref[win, :])
            dp_scr[hh, :, :nwin * Q_BLOCK] = _nt(do_h, v_ref[win, :])

        def grads(i, hh):
            nwin = nwin_of(i)
            off = (KV_BLOCKS - nwin) * Q_BLOCK
            for rows, cols, rest in _score_windows(nwin):
                bias_cols = slice(off + cols.start, off + cols.stop)
                s = s_scr[hh, rows, cols] + bias_ref[hh, rows, bias_cols]
                e = jnp.exp(s - jnp.max(s, axis=-1, keepdims=True))
                p = e * (1.0 / jnp.sum(e, axis=-1, keepdims=True))
                dp = dp_scr[hh, rows, cols]
                ds = p * (dp - jnp.sum(p * dp, axis=-1, keepdims=True))
                total_ref[hh, rows, :] += sum(ds[:, c0:c0 + LANES] for c0 in range(0, ds.shape[1], LANES))
                for qc in range(rows.start // CHUNK, rows.stop // CHUNK):
                    lo = max(_BAND_START[qc], bias_cols.start)
                    hi = min(_BAND_START[qc] + _BAND_WIDTH, bias_cols.stop)
                    if lo < hi:
                        band_ref[hh, qc, :, lo - _BAND_START[qc]:hi - _BAND_START[qc]] += ds[
                            qc * CHUNK - rows.start:(qc + 1) * CHUNK - rows.start,
                            lo - bias_cols.start:hi - bias_cols.start]
                p_scr[hh, rows, cols] = p.astype(BF16)
                dsb_scr[hh, rows, cols] = ds.astype(BF16)
                if rest is not None:
                    p_scr[hh, rows, rest] = jnp.zeros((CHUNK, LANES), BF16)
                    dsb_scr[hh, rows, rest] = jnp.zeros((CHUNK, LANES), BF16)

        def ring(block):
            return block % KV_BLOCKS if isinstance(block, int) else lax.rem(block, KV_BLOCKS)

        def accumulate(i, hh):
            nwin = nwin_of(i)
            w = nwin * Q_BLOCK
            win = _block_rows(i + 1 - nwin, nwin)
            q_h, do_h = operands(i, hh)
            ds_b = dsb_scr[hh, :, :w]
            dq_h = _nn(ds_b, k_ref[win, :]) * ATTN_SCALE
            dkw = _tn(ds_b, q_h)
            dvw = _tn(p_scr[hh, :, :w], do_h)
            for b in range(nwin):
                slot = ring(i + 1 - nwin + b)
                part = slice(b * Q_BLOCK, (b + 1) * Q_BLOCK)
                if hh == 0 and b == nwin - 1:
                    dk_acc[slot] = dkw[part]
                    dv_acc[slot] = dvw[part]
                else:
                    dk_acc[slot] += dkw[part]
                    dv_acc[slot] += dvw[part]
            if hh == 0:
                dq_scr[...] = dq_h
            else:
                dq_ref[_block_rows(i, 1), :] = jnp.where(_head_lanes(0), dq_scr[...], dq_h).astype(BF16)
                if not (isinstance(i, int) and i < KV_BLOCKS - 1):
                    flush(i - (KV_BLOCKS - 1))

        def flush(block):
            dk_ref[_block_rows(block, 1), :] = dk_acc[ring(block)].astype(BF16)
            dv_ref[_block_rows(block, 1), :] = dv_acc[ring(block)].astype(BF16)

        def tile(n):
            return n // 2, n % 2

        def step(n):
            if n + 1 < 2 * nb:
                products(*tile(n + 1))
            grads(*tile(n))
            if n >= 1:
                accumulate(*tile(n - 1))

        products(0, 0)
        for n in range(2 * KV_BLOCKS):
            step(n)

        def two_steps(i, carry):
            products(i, 1)
            grads(i, 0)
            accumulate(i - 1, 1)
            products(i + 1, 0)
            grads(i, 1)
            accumulate(i, 0)
            return carry

        lax.fori_loop(KV_BLOCKS, nb - 1, two_steps, 0, unroll=2)
        step(2 * nb - 2)
        step(2 * nb - 1)
        accumulate(nb - 1, 1)
        flush(nb - 2)
        flush(nb - 1)

        @pl.when(pl.program_id(0) == N_PAIRS - 1)
        def _():
            for cp in exchange:
                cp.wait_recv()
            for cp in exchange:
                cp.wait_send()

    col = lambda c0: pl.BlockSpec((t, PAIR), lambda j: (0, c0 + j))
    tile_spec = pl.BlockSpec((2, Q_BLOCK, KV_WINDOW), lambda j: (j, 0, 0))
    out = jax.ShapeDtypeStruct((t, ATTN_WIDTH), BF16)
    return pl.pallas_call(
        body,
        name="attn_bwd",
        grid=(N_PAIRS,),
        out_shape=(out, out, out,
                   jax.ShapeDtypeStruct((N_HEADS, Q_BLOCK // CHUNK, CHUNK, _BAND_WIDTH), F32),
                   jax.ShapeDtypeStruct((N_HEADS, Q_BLOCK, LANES), F32),
                   jax.ShapeDtypeStruct((N_DEV - 1, OUT_SHARD, D_MODEL), BF16)),
        in_specs=[col(0), col(N_PAIRS), col(2 * N_PAIRS), col(0), tile_spec, pl.BlockSpec(memory_space=pl.ANY)],
        out_specs=(col(0), col(0), col(0),
                   pl.BlockSpec((2, Q_BLOCK // CHUNK, CHUNK, _BAND_WIDTH), lambda j: (j, 0, 0, 0)),
                   pl.BlockSpec((2, Q_BLOCK, LANES), lambda j: (j, 0, 0)),
                   pl.BlockSpec(memory_space=pl.ANY)),
        scratch_shapes=[
            pltpu.VMEM((2, Q_BLOCK, KV_WINDOW), F32),
            pltpu.VMEM((2, Q_BLOCK, KV_WINDOW), F32),
            pltpu.VMEM((2, Q_BLOCK, KV_WINDOW), BF16),
            pltpu.VMEM((2, Q_BLOCK, KV_WINDOW), BF16),
            pltpu.VMEM((Q_BLOCK, PAIR), F32),
            pltpu.VMEM((KV_BLOCKS, Q_BLOCK, PAIR), F32),
            pltpu.VMEM((KV_BLOCKS, Q_BLOCK, PAIR), F32),
            pltpu.SemaphoreType.DMA((N_DEV - 1,)),
            pltpu.SemaphoreType.DMA((N_DEV - 1,)),
        ],
        compiler_params=_params(("arbitrary",)),
    )(qkv, qkv, qkv, do, bias_tile, dwout_g)


def _outproj_loss(x2d, tgt2d, y_pool, o, ag, wout_g, g2):
    t = x2d.shape[0]
    n_tiles = t // TOKEN_TILE

    def body(x_ref, tgt_ref, yp_ref, o_ref, ag_ref, w_ref, g_ref,
             dx2_ref, dyp_ref, do_ref, dag_ref, dw_ref, dg_ref, loss_ref, acc_ref):
        i = pl.program_id(0)

        @pl.when(i == 0)
        def _():
            acc_ref[...] = jnp.zeros_like(acc_ref)
            dg_ref[...] = jnp.zeros_like(dg_ref)
            loss_ref[...] = jnp.zeros_like(loss_ref)

        w = w_ref[...].reshape(D_MODEL, D_MODEL)
        ga = ag_ref[...]
        sig = _sigmoid(ga)
        gate = ga * sig
        of = o_ref[...].astype(F32)
        y = jnp.concatenate([yp_ref[...], (of * gate).astype(BF16)], axis=1)
        x2 = x_ref[...] + _nn(y, w)
        r = lax.rsqrt(jnp.mean(x2 * x2, axis=-1, keepdims=True) + EPS)
        xh = x2 * r
        g = g_ref[...]
        diff = xh * g - tgt_ref[...]
        tok = jnp.sum(diff * diff, axis=-1, keepdims=True) * (1.0 / D_MODEL)
        loss_ref[...] += jnp.sum(tok, axis=0, keepdims=True)
        dout = diff * (1.0 / D_MODEL)
        dg_ref[...] += jnp.sum(dout * xh, axis=0, keepdims=True)
        u = dout * g
        dx2 = r * (u - xh * jnp.mean(u * xh, axis=-1, keepdims=True))
        dx2_ref[...] = dx2
        dx2_b = dx2.astype(BF16)
        dy = _nt(dx2_b, w)
        dyp_ref[...] = dy[:, :POOL_WIDTH].astype(BF16)
        dya = dy[:, POOL_WIDTH:]
        do_ref[...] = (dya * gate).astype(BF16)
        dag_ref[...] = ((dya * of) * (sig * (1.0 + ga * (1.0 - sig)))).astype(BF16)
        acc_ref[...] += _tn(y, dx2_b)

        @pl.when(i == n_tiles - 1)
        def _():
            dw_ref[...] = acc_ref[...].reshape(N_DEV, OUT_SHARD, D_MODEL).astype(BF16)

    tile = lambda width: pl.BlockSpec((TOKEN_TILE, width), lambda i: (i, 0))
    return pl.pallas_call(
        body,
        name="outproj_loss",
        grid=(n_tiles,),
        out_shape=(
            jax.ShapeDtypeStruct((t, D_MODEL), F32),
            jax.ShapeDtypeStruct((t, POOL_WIDTH), BF16),
            jax.ShapeDtypeStruct((t, ATTN_WIDTH), BF16),
            jax.ShapeDtypeStruct((t, ATTN_WIDTH), BF16),
            jax.ShapeDtypeStruct((N_DEV, OUT_SHARD, D_MODEL), BF16),
            jax.ShapeDtypeStruct((1, D_MODEL), F32),
            jax.ShapeDtypeStruct((8, LANES), F32),
        ),
        in_specs=[
            tile(D_MODEL), tile(D_MODEL), tile(POOL_WIDTH), tile(ATTN_WIDTH), tile(ATTN_WIDTH),
            pl.BlockSpec((N_DEV, OUT_SHARD, D_MODEL), lambda i: (0, 0, 0)),
            pl.BlockSpec((1, D_MODEL), lambda i: (0, 0)),
        ],
        out_specs=(
            tile(D_MODEL), tile(POOL_WIDTH), tile(ATTN_WIDTH), tile(ATTN_WIDTH),
            pl.BlockSpec((N_DEV, OUT_SHARD, D_MODEL), lambda i: (0, 0, 0)),
            pl.BlockSpec((1, D_MODEL), lambda i: (0, 0)),
            pl.BlockSpec((8, LANES), lambda i: (0, 0)),
        ),
        scratch_shapes=[pltpu.VMEM((D_MODEL, D_MODEL), F32)],
        compiler_params=_params(("arbitrary",)),
    )(x2d, tgt2d, y_pool, o, ag, wout_g, g2)


def _dproj_specs():
    tile = lambda width: pl.BlockSpec((TOKEN_TILE, width), lambda i: (i, 0))
    return [tile(2 * POOL_WIDTH)] + [tile(ATTN_WIDTH)] * 4


def _inproj_bwd_dx(x2d, dx2, dproj, g1, wg):
    t = x2d.shape[0]

    def body(x_ref, dx2_ref, dp_ref, dq_ref, dk_ref, dv_ref, dag_ref, g_ref, wg_hbm, gx_ref, dg_ref, wfull_ref, sem):
        @pl.when(pl.program_id(0) == 0)
        def _():
            _load_w_in(wg_hbm, wfull_ref, sem)
            dg_ref[...] = jnp.zeros_like(dg_ref)

        dproj_t = jnp.concatenate([dp_ref[...], dq_ref[...], dk_ref[...], dv_ref[...], dag_ref[...]], axis=1)
        dh = _nt(dproj_t, wfull_ref[...])
        xf = x_ref[...]
        r = lax.rsqrt(jnp.mean(xf * xf, axis=-1, keepdims=True) + EPS)
        xh = xf * r
        dg_ref[...] += jnp.sum(dh * xh, axis=0, keepdims=True)
        u = dh * g_ref[...]
        gx_ref[...] = dx2_ref[...] + r * (u - xh * jnp.mean(u * xh, axis=-1, keepdims=True))

    tile = pl.BlockSpec((TOKEN_TILE, D_MODEL), lambda i: (i, 0))
    return pl.pallas_call(
        body,
        name="inproj_bwd_dx",
        grid=(t // TOKEN_TILE,),
        out_shape=(jax.ShapeDtypeStruct((t, D_MODEL), F32), jax.ShapeDtypeStruct((1, D_MODEL), F32)),
        in_specs=[tile, tile] + _dproj_specs() + [
            pl.BlockSpec((1, D_MODEL), lambda i: (0, 0)),
            pl.BlockSpec(memory_space=pl.ANY),
        ],
        out_specs=(tile, pl.BlockSpec((1, D_MODEL), lambda i: (0, 0))),
        scratch_shapes=[pltpu.VMEM((D_MODEL, IN_WIDTH), BF16), pltpu.SemaphoreType.DMA((N_DEV,))],
        compiler_params=_params(("arbitrary",)),
    )(x2d, dx2, *dproj, g1, wg)


def _inproj_bwd_dw(x2d, pvg, dy_pool, dattn, g1, pool_w, pool_scale):
    t = x2d.shape[0]
    n_tiles = t // TOKEN_TILE
    halo_per_tile = TOKEN_TILE // HALO
    last_halo = t // HALO - 1

    def body(x_ref, cur_ref, prev_ref, pgn_ref, dy_ref, dyn_ref, dq_ref, dk_ref, dv_ref, dag_ref, g_ref, pw_ref, ps_ref,
             out_ref, dp_ref, dpw_ref, dps_ref, acc_ref):
        i = pl.program_id(0)

        @pl.when(i == 0)
        def _():
            acc_ref[...] = jnp.zeros_like(acc_ref)
            dpw_ref[...] = jnp.zeros_like(dpw_ref)
            dps_ref[...] = jnp.zeros_like(dps_ref)

        xf = x_ref[...]
        r = lax.rsqrt(jnp.mean(xf * xf, axis=-1, keepdims=True) + EPS)
        h = ((xf * r) * g_ref[...]).astype(BF16)
        half = ATTN_WIDTH // 2
        for gi, ref in enumerate((dq_ref, dk_ref, dv_ref, dag_ref)):
            pool_halves = _pool_bwd_group(gi, i, n_tiles, cur_ref, prev_ref, pgn_ref, dy_ref, dyn_ref, pw_ref, ps_ref,
                                          dp_ref, dpw_ref, dps_ref)
            for c0 in (0, half):
                col = 2 * POOL_WIDTH + gi * ATTN_WIDTH + c0
                acc_ref[:, col:col + half] += _tn(h, ref[:, c0:c0 + half])
                next(pool_halves, None)
        for c0 in (0, POOL_WIDTH):
            acc_ref[:, c0:c0 + POOL_WIDTH] += _tn(h, dp_ref[:, c0:c0 + POOL_WIDTH])

        @pl.when(i == n_tiles - 1)
        def _():
            for d in range(N_DEV):
                out_ref[d] = acc_ref[:, d * IN_SHARD:(d + 1) * IN_SHARD].astype(BF16)

    tile = lambda width: pl.BlockSpec((TOKEN_TILE, width), lambda i: (i, 0))
    next_halo = lambda col: pl.BlockSpec(
        (HALO, POOL_WIDTH), lambda i: (jnp.minimum((i + 1) * halo_per_tile, last_halo), col))
    return pl.pallas_call(
        body,
        name="inproj_bwd_dw",
        grid=(n_tiles,),
        out_shape=(
            jax.ShapeDtypeStruct((N_DEV, D_MODEL, IN_SHARD), BF16),
            jax.ShapeDtypeStruct((t, 2 * POOL_WIDTH), BF16),
            jax.ShapeDtypeStruct((N_GROUPS, GROUP_DIM, GROUP_DIM), F32),
            jax.ShapeDtypeStruct((1, POOL_WIDTH), F32),
        ),
        in_specs=[
            tile(D_MODEL),
            tile(2 * POOL_WIDTH),
            pl.BlockSpec((HALO, POOL_WIDTH), lambda i: (jnp.maximum(i * halo_per_tile - 1, 0), 0)),
            next_halo(1),
            tile(POOL_WIDTH),
            next_halo(0),
            tile(ATTN_WIDTH), tile(ATTN_WIDTH), tile(ATTN_WIDTH), tile(ATTN_WIDTH),
            pl.BlockSpec((1, D_MODEL), lambda i: (0, 0)),
            pl.BlockSpec((N_GROUPS, GROUP_DIM, GROUP_DIM), lambda i: (0, 0, 0)),
            pl.BlockSpec((1, POOL_WIDTH), lambda i: (0, 0)),
        ],
        out_specs=(
            pl.BlockSpec((N_DEV, D_MODEL, IN_SHARD), lambda i: (0, 0, 0)),
            tile(2 * POOL_WIDTH),
            pl.BlockSpec((N_GROUPS, GROUP_DIM, GROUP_DIM), lambda i: (0, 0, 0)),
            pl.BlockSpec((1, POOL_WIDTH), lambda i: (0, 0)),
        ),
        scratch_shapes=[pltpu.VMEM((D_MODEL, IN_WIDTH), F32)],
        compiler_params=_params(("arbitrary",)),
    )(x2d, pvg, pvg, pvg, dy_pool, dy_pool, *dattn, g1, pool_w, pool_scale)


_HBM = pl.BlockSpec(memory_space=pltpu.HBM)
_SEM = pl.BlockSpec(memory_space=pltpu.SEMAPHORE)
_DATAFLOW = pltpu.SideEffectType.DATAFLOW_SIDE_EFFECTING


_N_EXCHANGED = 3


def _exchange_all(refs, send_sems, recv_sems):
    win_hbm, win_land, pw_hbm, pw_land, vec_hbm, vec_land = refs
    return (_exchange_copies(win_hbm, win_land, send_sems, recv_sems)
            + _exchange_copies(pw_hbm, pw_land, send_sems, recv_sems, first_sem=N_DEV - 1)
            + _exchange_copies(vec_hbm, vec_land, send_sems, recv_sems, first_sem=2 * (N_DEV - 1), same_for_all=True))


def _exchange_start(win_blocks, pw_blocks, vec):
    arrays = []
    for a, land_shape in ((win_blocks, (N_DEV - 1,) + win_blocks.shape[1:]),
                          (pw_blocks, (N_DEV - 1,) + pw_blocks.shape[1:]),
                          (vec, (N_DEV - 1,) + vec.shape)):
        arrays += [pltpu.with_memory_space_constraint(a, pltpu.HBM),
                   pltpu.with_memory_space_constraint(lax.empty(land_shape, a.dtype), pltpu.HBM)]

    def body(*refs):
        ins, (send_sems, recv_sems), token = refs[:2 * _N_EXCHANGED], refs[2 * _N_EXCHANGED:2 * _N_EXCHANGED + 2], refs[-1]
        for cp in _exchange_all(ins, send_sems, recv_sems):
            cp.start()
        token[...] = jnp.zeros_like(token)

    sems = pltpu.SemaphoreType.DMA((_N_EXCHANGED * (N_DEV - 1),))
    return pl.pallas_call(
        body,
        name="exchange_start",
        out_shape=(sems, sems, *[pltpu.HBM(a.shape, a.dtype) for a in arrays], jax.ShapeDtypeStruct((8, LANES), F32)),
        in_specs=tuple([_HBM] * len(arrays)),
        out_specs=(_SEM, _SEM, *[_HBM] * len(arrays), pl.BlockSpec(memory_space=pltpu.VMEM)),
        input_output_aliases={k: 2 + k for k in range(len(arrays))},
        compiler_params=pltpu.CompilerParams(has_side_effects=_DATAFLOW),
    )(*arrays)


def _exchange_wait(send_sems, recv_sems, arrays, after):
    def body(*refs):
        ins = refs[:2 * _N_EXCHANGED]
        send_sems, recv_sems = refs[2 * _N_EXCHANGED:2 * _N_EXCHANGED + 2]
        for cp in _exchange_all(ins, send_sems, recv_sems):
            cp.wait_send()
            cp.wait_recv()

    return pl.pallas_call(
        body,
        name="exchange_wait",
        out_shape=tuple(pltpu.HBM(a.shape, a.dtype) for a in arrays),
        in_specs=(*[_HBM] * len(arrays), _SEM, _SEM, pl.BlockSpec(memory_space=pl.ANY)),
        out_specs=tuple([_HBM] * len(arrays)),
        input_output_aliases={k: k for k in range(len(arrays))},
        compiler_params=pltpu.CompilerParams(has_side_effects=_DATAFLOW),
    )(*arrays, send_sems, recv_sems, after)


def _adamw(w, g, m, v):
    m = ADAM_B1 * m + (1.0 - ADAM_B1) * g
    v = ADAM_B2 * v + (1.0 - ADAM_B2) * (g * g)
    m_hat = m / (1.0 - ADAM_B1 ** ADAM_STEP)
    v_hat = v / (1.0 - ADAM_B2 ** ADAM_STEP)
    delta = -ADAM_LR * (m_hat / (jnp.sqrt(v_hat) + ADAM_EPS) + ADAM_WD * w)
    return delta, m, v


def _small_allreduce(d_g1, vec, vec_land, pw_blocks, pw_land):
    def body(g1_ref, vec_ref, vland_ref, pwb_ref, pland_ref, vec_out, pw_out, vparts_ref, pparts_ref, rows_ref,
             slice_ref, send_sems, recv_sems):
        me = _dev_index(_mesh_pos())

        def from_devices(parts_ref, own, land_ref):
            parts_ref[0] = own
            parts_ref[1:] = land_ref[...]
            total = parts_ref[me]
            for s in range(1, N_DEV):
                total = total + parts_ref[me ^ s]
            return total

        slice_ref[...] = from_devices(pparts_ref, pwb_ref[me], pland_ref)

        def send(r, src, dst, k):
            return pltpu.make_async_remote_copy(
                src_ref=src, dst_ref=dst, send_sem=send_sems.at[2 * (r - 1) + k],
                recv_sem=recv_sems.at[2 * (r - 1) + k], device_id=_peer(r), device_id_type=MESH_ID)

        started = [cp for r in range(1, N_DEV)
                   for cp in (send(r, g1_ref, rows_ref.at[r], 0), send(r, slice_ref, pw_out.at[me], 1))]
        for cp in started:
            cp.start()
        rows_ref[0] = g1_ref[...]
        pw_out[me] = slice_ref[...]
        vec_out[...] = from_devices(vparts_ref, vec_ref[...], vland_ref)
        for cp in started:
            cp.wait_recv()
        for cp in started:
            cp.wait_send()
        g1 = rows_ref[me]
        for s in range(1, N_DEV):
            g1 = g1 + rows_ref[me ^ s]
        vec_out[_ROW_G1:_ROW_G1 + 1, :] = g1

    vm = pl.BlockSpec(memory_space=pltpu.VMEM)
    return pl.pallas_call(
        body,
        name="small_allreduce",
        out_shape=(jax.ShapeDtypeStruct((_VEC_ROWS, D_MODEL), F32),
                   jax.ShapeDtypeStruct((N_DEV, GROUP_DIM // 2, GROUP_DIM), F32)),
        in_specs=[vm] * 5,
        out_specs=(vm, vm),
        scratch_shapes=[
            pltpu.VMEM((N_DEV, _VEC_ROWS, D_MODEL), F32),
            pltpu.VMEM((N_DEV, GROUP_DIM // 2, GROUP_DIM), F32),
            pltpu.VMEM((N_DEV, 1, D_MODEL), F32),
            pltpu.VMEM((GROUP_DIM // 2, GROUP_DIM), F32),
            pltpu.SemaphoreType.DMA((2 * (N_DEV - 1),)),
            pltpu.SemaphoreType.DMA((2 * (N_DEV - 1),)),
        ],
        compiler_params=_params(),
    )(d_g1, vec, vec_land, pw_blocks, pw_land)


def _adamw_all(dwin_g, land_in, dwout_g, land_out, vec_sum, pw_sum, weights, big):
    small_shapes = [(1, D_MODEL), (1, D_MODEL), (1, POOL_WIDTH), (N_HEADS, 2 * LANES), (N_GROUPS, GROUP_DIM, GROUP_DIM)]

    def body(*refs):
        refs = list(refs)
        take = lambda n: [refs.pop(0) for _ in range(n)]
        dwin_hbm, lin_ref, dwout_hbm, lout_ref, vec_ref, pw_ref = take(6)
        small_wmv = [take(3) for _ in range(5)]
        big_wmv = [take(3) for _ in range(2)]
        big_out = [take(4) for _ in range(2)]
        small_out = [take(4) for _ in range(5)]
        own_in, own_out, local_sems = refs

        me = _dev_index(_mesh_pos())
        mine = [pltpu.make_async_copy(dwin_hbm.at[me], own_in, local_sems.at[0]),
                pltpu.make_async_copy(dwout_hbm.at[me], own_out, local_sems.at[1])]
        for cp in mine:
            cp.start()

        def update(g, wmv, outs):
            delta, m_new, v_new = _adamw(wmv[0][...], g, wmv[1][...], wmv[2][...])
            for ref, val in zip(outs, (g, delta, m_new, v_new)):
                ref[...] = val

        update(vec_ref[_ROW_G1:_ROW_G1 + 1, :], small_wmv[0], small_out[0])
        update(vec_ref[_ROW_G2:_ROW_G2 + 1, :], small_wmv[1], small_out[1])
        update(vec_ref[_ROW_PS:_ROW_PS + 1, :POOL_WIDTH], small_wmv[2], small_out[2])
        update(vec_ref[_ROW_RB:_ROW_RB + N_HEADS, :2 * LANES], small_wmv[3], small_out[3])
        update(pw_ref[...], small_wmv[4], small_out[4])
        for cp in mine:
            cp.wait()
        g_in = own_in[...].astype(F32)
        g_out = own_out[...].astype(F32)
        for r in range(N_DEV - 1):
            g_in = g_in + lin_ref[r].astype(F32)
            g_out = g_out + lout_ref[r].astype(F32)
        update(g_in, big_wmv[0], big_out[0])
        update(g_out, big_wmv[1], big_out[1])

    vm = pl.BlockSpec(memory_space=pltpu.VMEM)
    hbm = pl.BlockSpec(memory_space=pl.ANY)
    f32 = lambda shape: jax.ShapeDtypeStruct(shape, F32)
    out_shapes = [f32((D_MODEL, IN_SHARD))] * 4 + [f32((OUT_SHARD, D_MODEL))] * 4
    for shape in small_shapes:
        out_shapes += [f32(shape)] * 4
    args = [dwin_g, land_in, dwout_g, land_out, vec_sum, pw_sum]
    for wmv in weights:
        args += list(wmv)
    for wmv in big:
        args += list(wmv)
    return pl.pallas_call(
        body,
        name="adamw_all",
        out_shape=tuple(out_shapes),
        in_specs=[hbm, vm, hbm, vm] + [vm] * (len(args) - 4),
        out_specs=tuple([vm] * len(out_shapes)),
        scratch_shapes=[
            pltpu.VMEM((D_MODEL, IN_SHARD), BF16),
            pltpu.VMEM((OUT_SHARD, D_MODEL), BF16),
            pltpu.SemaphoreType.DMA((2,)),
        ],
        compiler_params=_params(),
    )(*args)


def kernel(x, norm_gain, w_in, pool_w, pool_scale, rel_bias, w_out, final_norm_gain, loss_target, m_norm_gain, m_w_in, m_pool_w, m_pool_scale, m_rel_bias, m_w_out, m_final_norm_gain, v_norm_gain, v_w_in, v_pool_w, v_pool_scale, v_rel_bias, v_w_out, v_final_norm_gain):
    t = x.shape[1]
    assert x.shape[0] == 1 and t % TOKEN_TILE == 0 and t // Q_BLOCK >= 4
    x2d = x[0]
    tgt2d = loss_target[0]
    g2 = final_norm_gain.reshape(1, D_MODEL)

    rb = rel_bias[0]
    rel_line = jnp.concatenate([
        jnp.broadcast_to(rb[:, :1], (N_HEADS, _REL_FIRST)), rb,
        jnp.broadcast_to(rb[:, N_REL - 1:], (N_HEADS, TOEPLITZ - _REL_FIRST - N_REL)),
    ], axis=1).reshape(N_HEADS, 1, TOEPLITZ)
    wg_in, bias_tile = _gather_weights(w_in[0], rel_line)

    pvg, qkv, ag, y_pool = _norm_inproj(x2d, norm_gain, wg_in, pool_w[0], pool_scale)
    o, wg_out = _attn_fwd(qkv, bias_tile, w_out[0])
    dx2, dy_pool, do, dag, dwout_g, d_g2, loss_sum = _outproj_loss(x2d, tgt2d, y_pool, o, ag, wg_out, g2)
    dq, dk, dv, ds_band, ds_total, land_out = _attn_bwd(qkv, do, bias_tile, dwout_g)
    dwin_g, d_pool, d_pw, d_ps = _inproj_bwd_dw(x2d, pvg, dy_pool, (dq, dk, dv, dag), norm_gain, pool_w[0], pool_scale)
    vec = _pack_small(ds_band, ds_total, d_g2, d_ps, loss_sum)
    dproj = (d_pool, dq, dk, dv, dag)
    pw_blocks = d_pw.reshape(N_DEV, GROUP_DIM // 2, GROUP_DIM)
    send_sems, recv_sems, *exchanged, token = _exchange_start(dwin_g, pw_blocks, vec)
    grad_x, d_g1 = _inproj_bwd_dx(x2d, dx2, dproj, norm_gain + token[:1, :1], wg_in)
    dwin_g, land_in, pw_blocks, pw_land, vec, vec_land = _exchange_wait(send_sems, recv_sems, exchanged, d_g1)

    pad_rb = lambda a: jnp.pad(a[0], ((0, 0), (0, 2 * LANES - N_REL)))
    row = lambda a: a.reshape(1, D_MODEL)
    weights = [
        (norm_gain, m_norm_gain, v_norm_gain),
        (row(final_norm_gain), row(m_final_norm_gain), row(v_final_norm_gain)),
        (pool_scale, m_pool_scale, v_pool_scale),
        (pad_rb(rel_bias), pad_rb(m_rel_bias), pad_rb(v_rel_bias)),
        (pool_w[0], m_pool_w[0], v_pool_w[0]),
    ]
    big = [(w_in[0], m_w_in[0], v_w_in[0]), (w_out[0], m_w_out[0], v_w_out[0])]
    vec_sum, pw_sum = _small_allreduce(d_g1, vec, vec_land, pw_blocks, pw_land)
    res = _adamw_all(dwin_g, land_in, dwout_g, land_out, vec_sum, pw_sum.reshape(N_GROUPS, GROUP_DIM, GROUP_DIM),
                     weights, big)
    loss = 0.5 * vec_sum[_ROW_LOSS, 0]

    def leaves(k):
        g1_, g2_, ps_, rb_, pw_ = (res[8 + 4 * leaf + k] for leaf in range(5))
        return [g1_, res[k][None], pw_[None], ps_, rb_[None, :, :N_REL], res[4 + k][None], g2_.reshape(D_MODEL)]

    return (loss, grad_x[None], *leaves(0), *leaves(1), *leaves(2), *leaves(3))
```

```python
import math

import jax
import jax.numpy as jnp
from jax import lax
from jax.experimental import pallas as pl
from jax.experimental.pallas import tpu as pltpu

F32 = jnp.float32
BF16 = jnp.bfloat16
MESH_ID = pl.DeviceIdType.MESH

D_MODEL = 1024
POOL_WIDTH = 512
ATTN_WIDTH = 512
POOL_WINDOWS = (2, 4, 8, 16)
N_GROUPS = 4
GROUP_DIM = 128
HEAD_DIM = 64
N_HEADS = 8
CHUNK = 64
LEFT_CHUNKS = 8
MAX_REL = 64
N_REL = 2 * MAX_REL + 1
IN_WIDTH = 2 * POOL_WIDTH + 4 * ATTN_WIDTH
EPS = 1e-6
MASK_VALUE = -1e30
ATTN_SCALE = 1.0 / math.sqrt(HEAD_DIM)
ADAM_LR = 0.001
ADAM_B1 = 0.9
ADAM_B2 = 0.999
ADAM_EPS = 1e-08
ADAM_WD = 0.01
ADAM_STEP = 10

N_DEV = 8
IN_SHARD = IN_WIDTH // N_DEV
OUT_SHARD = D_MODEL // N_DEV

LANES = 128
TOKEN_TILE = 512
HALO = 16
Q_BLOCK = 256
KV_BLOCKS = 3
KV_WINDOW = KV_BLOCKS * Q_BLOCK
PAIR = 2 * HEAD_DIM
N_PAIRS = N_HEADS // 2
TOEPLITZ = 1024
VMEM_LIMIT = 56 * 1024 * 1024


def _params(sem=None, vmem=VMEM_LIMIT):
    return pltpu.CompilerParams(dimension_semantics=sem, vmem_limit_bytes=vmem)


def _sigmoid(x):
    return 1.0 / (1.0 + jnp.exp(-x))


def _nt(a, b):
    return lax.dot_general(a, b, (((1,), (1,)), ((), ())), preferred_element_type=F32)


def _tn(a, b):
    return lax.dot_general(a, b, (((0,), (0,)), ((), ())), preferred_element_type=F32)


def _nn(a, b):
    return jnp.dot(a, b, preferred_element_type=F32)


def _mesh_pos():
    return lax.axis_index("x"), lax.axis_index("y"), lax.axis_index("c")


def _dev_index(p):
    return 4 * p[0] + 2 * p[1] + p[2]


def _peer(r):
    x, y, c = _mesh_pos()
    return (x ^ ((r >> 2) & 1), y ^ ((r >> 1) & 1), c ^ (r & 1))


def _exchange_copies(src_hbm, land_hbm, send_sems, recv_sems, first_sem=0, same_for_all=False):
    return [
        pltpu.make_async_remote_copy(
            src_ref=src_hbm if same_for_all else src_hbm.at[_dev_index(_peer(r))], dst_ref=land_hbm.at[r - 1],
            send_sem=send_sems.at[first_sem + r - 1], recv_sem=recv_sems.at[first_sem + r - 1],
            device_id=_peer(r), device_id_type=MESH_ID)
        for r in range(1, N_DEV)
    ]


def _gather_weights(w_in_shard, rel_line):
    def body(win_ref, line_ref, gin_ref, bias_ref, sin_ref, send_sems, recv_sems):
        x, y, c = _mesh_pos()
        me, sibling = (x, y, c), (x, y, 1 - c)
        chips = [(1 - x, y), (x, 1 - y), (1 - x, 1 - y)]

        sin_ref[...] = win_ref[...].astype(BF16)
        gin_ref[_dev_index(me)] = sin_ref[...]

        def copy(k, block, to, from_shard=False):
            return pltpu.make_async_remote_copy(
                src_ref=sin_ref if from_shard else gin_ref.at[_dev_index(block)],
                dst_ref=gin_ref.at[_dev_index(block)],
                send_sem=send_sems.at[k],
                recv_sem=recv_sems.at[k],
                device_id=to,
                device_id_type=MESH_ID,
            )

        first = [copy(0, me, sibling, True)]
        first += [copy(1 + j, me, (*chip, c), True) for j, chip in enumerate(chips)]
        for cp in first:
            cp.start()
        passed = [copy(4 + j, (*chip, c), sibling) for j, chip in enumerate(chips)]

        def bias_heads(lo, hi):
            for h in range(lo, hi):
                bias_ref[h] = _toeplitz_bias(line_ref[h])

        bias_heads(0, N_HEADS - 3)
        for j, chip in enumerate(chips):
            copy(1 + j, (*chip, c), me).wait_recv()
            passed[j].start()
            bias_heads(N_HEADS - 3 + j, N_HEADS - 2 + j)
        copy(0, sibling, me).wait_recv()
        for j, chip in enumerate(chips):
            copy(4 + j, (*chip, 1 - c), me).wait_recv()
        for cp in first + passed:
            cp.wait_send()

    vm = pl.BlockSpec(memory_space=pltpu.VMEM)
    return pl.pallas_call(
        body,
        name="gather_weights",
        out_shape=(
            jax.ShapeDtypeStruct((N_DEV, D_MODEL, IN_SHARD), BF16),
            jax.ShapeDtypeStruct((N_HEADS, Q_BLOCK, KV_WINDOW), F32),
        ),
        in_specs=[vm, vm],
        out_specs=(vm, vm),
        scratch_shapes=[
            pltpu.VMEM((D_MODEL, IN_SHARD), BF16),
            pltpu.SemaphoreType.DMA((7,)),
            pltpu.SemaphoreType.DMA((7,)),
        ],
        compiler_params=_params(),
    )(w_in_shard, rel_line)


def _load_w_in(wg_hbm, wfull_ref, sem):
    copies = [
        pltpu.make_async_copy(wg_hbm.at[d], wfull_ref.at[:, d * IN_SHARD:(d + 1) * IN_SHARD], sem.at[d])
        for d in range(N_DEV)
    ]
    for cp in copies:
        cp.start()
    for cp in copies:
        cp.wait()


def _norm_inproj(x2d, g1, wg, pool_w, pool_scale):
    t = x2d.shape[0]

    def body(x_ref, g_ref, wg_hbm, pw_ref, ps_ref, pvg_ref, qkv_ref, ag_ref, yp_ref, wfull_ref, halo_ref, sem):
        i = pl.program_id(0)

        @pl.when(i == 0)
        def _():
            _load_w_in(wg_hbm, wfull_ref, sem)
            halo_ref[...] = jnp.zeros_like(halo_ref)

        xf = x_ref[...]
        r = lax.rsqrt(jnp.mean(xf * xf, axis=-1, keepdims=True) + EPS)
        h = ((xf * r) * g_ref[...]).astype(BF16)
        chunk = lambda ci: _nn(h, wfull_ref[:, ci * POOL_WIDTH:(ci + 1) * POOL_WIDTH])
        pv, pg = chunk(0), chunk(1)
        pvg_ref[:, :POOL_WIDTH] = pv
        pvg_ref[:, POOL_WIDTH:] = pg
        halo = halo_ref[...]
        halo_ref[...] = pv[TOKEN_TILE - HALO:]
        for gi in range(N_GROUPS):
            if gi < 3:
                qkv_ref[:, gi * ATTN_WIDTH:(gi + 1) * ATTN_WIDTH] = chunk(2 + gi).astype(BF16)
            else:
                ag_ref[...] = chunk(5)
            sl = slice(gi * GROUP_DIM, (gi + 1) * GROUP_DIM)
            d = _pool_diffs(pv[:, sl], halo[:, sl], i * TOKEN_TILE, POOL_WINDOWS[gi])
            z = _nn(d.astype(BF16), pw_ref[gi].astype(BF16))
            g = pg[:, sl]
            yp_ref[:, sl] = ((z * ps_ref[:, sl]) * (g * _sigmoid(g))).astype(BF16)

    tile = lambda width: pl.BlockSpec((TOKEN_TILE, width), lambda i: (i, 0))
    return pl.pallas_call(
        body,
        name="norm_inproj",
        grid=(t // TOKEN_TILE,),
        out_shape=(
            jax.ShapeDtypeStruct((t, 2 * POOL_WIDTH), F32),
            jax.ShapeDtypeStruct((t, 3 * ATTN_WIDTH), BF16),
            jax.ShapeDtypeStruct((t, ATTN_WIDTH), F32),
            jax.ShapeDtypeStruct((t, POOL_WIDTH), BF16),
        ),
        in_specs=[
            tile(D_MODEL),
            pl.BlockSpec((1, D_MODEL), lambda i: (0, 0)),
            pl.BlockSpec(memory_space=pl.ANY),
            pl.BlockSpec((N_GROUPS, GROUP_DIM, GROUP_DIM), lambda i: (0, 0, 0)),
            pl.BlockSpec((1, POOL_WIDTH), lambda i: (0, 0)),
        ],
        out_specs=(tile(2 * POOL_WIDTH), tile(3 * ATTN_WIDTH), tile(ATTN_WIDTH), tile(POOL_WIDTH)),
        scratch_shapes=[
            pltpu.VMEM((D_MODEL, IN_WIDTH), BF16),
            pltpu.VMEM((HALO, POOL_WIDTH), F32),
            pltpu.SemaphoreType.DMA((N_DEV,)),
        ],
        compiler_params=_params(("arbitrary",)),
    )(x2d, g1, wg, pool_w, pool_scale)


def _inv_count(first_row, rows, window):
    tpos = first_row + lax.broadcasted_iota(jnp.int32, (rows, 1), 0)
    return 1.0 / jnp.minimum(tpos + 1, window).astype(F32)


def _causal_window_sum(ext, window):
    s, k = ext, 1
    while k < window:
        s = s + pltpu.roll(s, k, 0)
        k *= 2
    return s


def _pool_diffs(pv, halo, first_row, window):
    s = _causal_window_sum(jnp.concatenate([halo, pv], axis=0), window)[HALO:]
    return s * _inv_count(first_row, pv.shape[0], window) - pv


def _pool_bwd_group(gi, i, n_tiles, cur_ref, prev_ref, pgn_ref, dy_ref, dyn_ref, pw_ref, ps_ref,
                    dp_ref, dpw_ref, dps_ref):
    w = POOL_WINDOWS[gi]
    sl = slice(gi * GROUP_DIM, (gi + 1) * GROUP_DIM)
    gate_sl = slice(POOL_WIDTH + gi * GROUP_DIM, POOL_WIDTH + (gi + 1) * GROUP_DIM)
    rows = TOKEN_TILE + HALO
    pw = pw_ref[gi].astype(BF16)
    ps = ps_ref[:, sl]
    d = _pool_diffs(cur_ref[:, sl], jnp.where(i > 0, prev_ref[:, sl], 0.0), i * TOKEN_TILE, w).astype(BF16)
    z = _nn(d, pw)
    g_ext = jnp.concatenate([cur_ref[:, gate_sl], pgn_ref[:, sl]], axis=0)
    dy_ext = jnp.concatenate([dy_ref[:, sl], dyn_ref[:, sl]], axis=0).astype(F32)
    sig = _sigmoid(g_ext)
    gate = g_ext * sig
    dz_ext = ((dy_ext * gate) * ps).astype(BF16)
    dd_ext = _nt(dz_ext, pw)
    yield
    e = dd_ext * _inv_count(i * TOKEN_TILE, rows, w)
    row = lax.broadcasted_iota(jnp.int32, (rows, 1), 0)
    e = jnp.where(jnp.logical_or(row < TOKEN_TILE, i < n_tiles - 1), e, 0.0)
    s, k = e, 1
    while k < w:
        s = s + pltpu.roll(s, rows - k, 0)
        k *= 2
    dp_ref[:, sl] = (s[:TOKEN_TILE] - dd_ext[:TOKEN_TILE]).astype(BF16)
    dy = dy_ext[:TOKEN_TILE]
    g = g_ext[:TOKEN_TILE]
    sg = sig[:TOKEN_TILE]
    dgate = sg * (1.0 + g * (1.0 - sg))
    dp_ref[:, gate_sl] = ((dy * (z * ps)) * dgate).astype(BF16)
    dps_ref[:, sl] += jnp.sum((dy * gate[:TOKEN_TILE]) * z, axis=0, keepdims=True)
    dpw_ref[gi] += _tn(d, dz_ext[:TOKEN_TILE])


_REL_FIRST = KV_WINDOW - 1 - MAX_REL


def _skew_rows(a, right):
    rows, lanes = a.shape
    row = lax.broadcasted_iota(jnp.int32, a.shape, 0)
    for b in range(rows.bit_length() - 1):
        shift = (1 << b) if right else lanes - (1 << b)
        a = jnp.where((row >> b) & 1 == 1, pltpu.roll(a, shift, 1), a)
    return a


def _toeplitz_bias(line):
    a = jnp.broadcast_to(line, (Q_BLOCK, TOEPLITZ))
    a = _skew_rows(a, True)
    a = pltpu.roll(a, TOEPLITZ - (Q_BLOCK - 1), 1)
    a = a[:, :KV_WINDOW]
    qc = lax.broadcasted_iota(jnp.int32, a.shape, 0) // CHUNK
    kc = lax.broadcasted_iota(jnp.int32, a.shape, 1) // CHUNK
    visible = jnp.logical_and(kc >= qc, kc <= qc + LEFT_CHUNKS)
    return jnp.where(visible, a, MASK_VALUE)


_BAND_WIDTH = 2 * LANES
_BAND_START = (384, 512, 512, 512)
_BAND_REL = tuple(a - (KV_BLOCKS - 1) * Q_BLOCK - qc * CHUNK for qc, a in enumerate(_BAND_START))


_ROW_G1, _ROW_G2, _ROW_PS, _ROW_LOSS, _ROW_RB = 0, 1, 2, 3, 8
_VEC_ROWS = 16


def _pack_small(band, total, d_g2, d_ps, loss_row):
    def body(band_ref, total_ref, g2_ref, ps_ref, loss_ref, vec_ref):
        vec_ref[...] = jnp.zeros_like(vec_ref)
        vec_ref[_ROW_G2:_ROW_G2 + 1, :] = g2_ref[...]
        vec_ref[_ROW_PS:_ROW_PS + 1, :POOL_WIDTH] = ps_ref[...]
        vec_ref[_ROW_LOSS:_ROW_LOSS + 1, :LANES] = loss_ref[0:1, :]
        for h in range(N_HEADS):
            a = jnp.zeros((CHUNK, 2 * _BAND_WIDTH), F32)
            for qc in range(Q_BLOCK // CHUNK):
                z = jnp.concatenate([band_ref[h, qc], jnp.zeros((CHUNK, _BAND_WIDTH), F32)], axis=1)
                left = -MAX_REL - _BAND_REL[qc]
                a = a + (pltpu.roll(z, 2 * _BAND_WIDTH - left, 1) if left else z)
            a = _skew_rows(a, False)
            near = jnp.sum(a, axis=0, keepdims=True)[:, :2 * LANES]
            r = lax.broadcasted_iota(jnp.int32, near.shape, 1)
            near = jnp.where(jnp.logical_and(r >= 1, r < 2 * MAX_REL), near, 0.0)
            everything = jnp.sum(jnp.sum(total_ref[h], axis=0, keepdims=True), axis=1, keepdims=True)
            far = everything - jnp.sum(near, axis=1, keepdims=True)
            vec_ref[_ROW_RB + h:_ROW_RB + h + 1, :2 * LANES] = jnp.where(r == 0, far, near)

    vm = pl.BlockSpec(memory_space=pltpu.VMEM)
    return pl.pallas_call(
        body,
        name="pack_small",
        out_shape=jax.ShapeDtypeStruct((_VEC_ROWS, D_MODEL), F32),
        in_specs=[vm] * 5,
        out_specs=vm,
        compiler_params=_params(),
    )(band, total, d_g2, d_ps, loss_row)


def _head_lanes(hh):
    lane = lax.broadcasted_iota(jnp.int32, (1, PAIR), 1)
    return (lane < HEAD_DIM) if hh == 0 else (lane >= HEAD_DIM)


def _score_windows(nwin):
    if nwin < KV_BLOCKS:
        return [(slice(0, Q_BLOCK), slice(0, nwin * Q_BLOCK), None)]
    pieces = []
    for qc in range(Q_BLOCK // CHUNK):
        rows = slice(qc * CHUNK, (qc + 1) * CHUNK)
        if qc < 2:
            pieces.append((rows, slice(0, KV_WINDOW - LANES), slice(KV_WINDOW - LANES, KV_WINDOW)))
        else:
            pieces.append((rows, slice(LANES, KV_WINDOW), slice(0, LANES)))
    return pieces


def _block_rows(first_block, n_blocks):
    if isinstance(first_block, int):
        return pl.ds(first_block * Q_BLOCK, n_blocks * Q_BLOCK)
    return pl.ds(pl.multiple_of(first_block * Q_BLOCK, Q_BLOCK), n_blocks * Q_BLOCK)


def _attn_fwd(qkv, bias_tile, w_out_shard):
    t = qkv.shape[0]
    nb = t // Q_BLOCK

    def body(q_ref, k_ref, v_ref, bias_ref, wout_ref, o_ref, gout_hbm,
             s_scr, p_scr, sout_ref, send_sems, recv_sems, local_sem):
        pair = pl.program_id(0)
        me = _dev_index(_mesh_pos())
        mine = pltpu.make_async_copy(sout_ref, gout_hbm.at[me], local_sem)

        def shard_copy(r, block):
            return pltpu.make_async_remote_copy(
                src_ref=sout_ref, dst_ref=gout_hbm.at[block], send_sem=send_sems.at[r - 1],
                recv_sem=recv_sems.at[r - 1], device_id=_peer(r), device_id_type=MESH_ID)

        @pl.when(pair == 0)
        def _():
            sout_ref[...] = wout_ref[...].astype(BF16)
            mine.start()
            for r in range(1, N_DEV):
                shard_copy(r, me).start()

        def scores(i, slot, nwin):
            q = q_ref[_block_rows(i, 1), :]
            kw = k_ref[_block_rows(i + 1 - nwin, nwin), :]
            for hh in range(2):
                q_h = jnp.where(_head_lanes(hh), q, jnp.zeros_like(q)) * ATTN_SCALE
                s_scr[slot, hh, :, :nwin * Q_BLOCK] = _nt(q_h, kw)

        def softmax(slot, nwin):
            off = (KV_BLOCKS - nwin) * Q_BLOCK
            for hh in range(2):
                for rows, cols, rest in _score_windows(nwin):
                    s = s_scr[slot, hh, rows, cols] + bias_ref[hh, rows, off + cols.start:off + cols.stop]
                    p_scr[slot, hh, rows, cols] = jnp.exp(s - jnp.max(s, axis=-1, keepdims=True)).astype(BF16)
                    if rest is not None:
                        p_scr[slot, hh, rows, rest] = jnp.zeros((CHUNK, LANES), BF16)

        def output(i, slot, nwin):
            vw = v_ref[_block_rows(i + 1 - nwin, nwin), :]
            outs = [_nn(p_scr[slot, hh, :, :nwin * Q_BLOCK], jnp.where(_head_lanes(hh), vw, jnp.ones_like(vw)))
                    for hh in range(2)]
            sums = pltpu.roll(jnp.where(_head_lanes(0), outs[1], outs[0]), HEAD_DIM, axis=1)
            o_ref[_block_rows(i, 1), :] = (jnp.where(_head_lanes(0), outs[0], outs[1]) * (1.0 / sums)).astype(BF16)

        scores(0, 0, 1)
        scores(1, 1, 2)
        softmax(0, 1)
        scores(2, 0, 3)
        softmax(1, 2)
        output(0, 0, 1)
        scores(3, 1, 3)
        softmax(0, 3)
        output(1, 1, 2)

        def two_steps(k, carry):
            i = 3 + 2 * k
            scores(i + 1, 0, KV_BLOCKS)
            softmax(1, KV_BLOCKS)
            output(i - 1, 0, KV_BLOCKS)
            scores(i + 2, 1, KV_BLOCKS)
            softmax(0, KV_BLOCKS)
            output(i, 1, KV_BLOCKS)
            return carry

        lax.fori_loop(0, (nb - 4) // 2, two_steps, 0, unroll=7)
        last = (nb - 1) % 2
        softmax(last, KV_BLOCKS)
        output(nb - 2, 1 - last, KV_BLOCKS)
        output(nb - 1, last, KV_BLOCKS)

        @pl.when(pair == N_PAIRS - 1)
        def _():
            for r in range(1, N_DEV):
                shard_copy(r, _dev_index(_peer(r))).wait_recv()
            for r in range(1, N_DEV):
                shard_copy(r, me).wait_send()
            mine.wait()

    col = lambda c0: pl.BlockSpec((t, PAIR), lambda j: (0, c0 + j))
    return pl.pallas_call(
        body,
        name="attn_fwd",
        grid=(N_PAIRS,),
        out_shape=(
            jax.ShapeDtypeStruct((t, ATTN_WIDTH), BF16),
            jax.ShapeDtypeStruct((N_DEV, OUT_SHARD, D_MODEL), BF16),
        ),
        in_specs=[col(0), col(N_PAIRS), col(2 * N_PAIRS),
                  pl.BlockSpec((2, Q_BLOCK, KV_WINDOW), lambda j: (j, 0, 0)),
                  pl.BlockSpec((OUT_SHARD, D_MODEL), lambda j: (0, 0))],
        out_specs=(col(0), pl.BlockSpec(memory_space=pl.ANY)),
        scratch_shapes=[
            pltpu.VMEM((2, 2, Q_BLOCK, KV_WINDOW), F32),
            pltpu.VMEM((2, 2, Q_BLOCK, KV_WINDOW), BF16),
            pltpu.VMEM((OUT_SHARD, D_MODEL), BF16),
            pltpu.SemaphoreType.DMA((N_DEV - 1,)),
            pltpu.SemaphoreType.DMA((N_DEV - 1,)),
            pltpu.SemaphoreType.DMA,
        ],
        compiler_params=_params(("arbitrary",)),
    )(qkv, qkv, qkv, bias_tile, w_out_shard)


def _attn_bwd(qkv, do, bias_tile, dwout_g):
    t = qkv.shape[0]
    nb = t // Q_BLOCK

    def body(q_ref, k_ref, v_ref, do_ref, bias_ref, dwout_hbm,
             dq_ref, dk_ref, dv_ref, band_ref, total_ref, land_hbm,
             s_scr, dp_scr, p_scr, dsb_scr, dq_scr, dk_acc, dv_acc, send_sems, recv_sems):
        exchange = _exchange_copies(dwout_hbm, land_hbm, send_sems, recv_sems)

        @pl.when(pl.program_id(0) == 0)
        def _():
            for cp in exchange:
                cp.start()

        band_ref[...] = jnp.zeros_like(band_ref)
        total_ref[...] = jnp.zeros_like(total_ref)

        def nwin_of(i):
            return min(i + 1, KV_BLOCKS) if isinstance(i, int) else KV_BLOCKS

        def operands(i, hh):
            lanes = _head_lanes(hh)
            q = q_ref[_block_rows(i, 1), :]
            do = do_ref[_block_rows(i, 1), :]
            return (jnp.where(lanes, q, jnp.zeros_like(q)) * ATTN_SCALE, jnp.where(lanes, do, jnp.zeros_like(do)))

        def products(i, hh):
            nwin = nwin_of(i)
            win = _block_rows(i + 1 - nwin, nwin)
            q_h, do_h = operands(i, hh)
            s_scr[hh, :, :nwin * Q_BLOCK] = _nt(q_h, k_ref[win, :])
            dp_scr[hh, :, :nwin * Q_BLOCK] = _nt(do_h, v_ref[win, :])

        def grads(i, hh):
            nwin = nwin_of(i)
            off = (KV_BLOCKS - nwin) * Q_BLOCK
            for rows, cols, rest in _score_windows(nwin):
                bias_cols = slice(off + cols.start, off + cols.stop)
                s = s_scr[hh, rows, cols] + bias_ref[hh, rows, bias_cols]
                e = jnp.exp(s - jnp.max(s, axis=-1, keepdims=True))
                p = e * (1.0 / jnp.sum(e, axis=-1, keepdims=True))
                dp = dp_scr[hh, rows, cols]
                ds = p * (dp - jnp.sum(p * dp, axis=-1, keepdims=True))
                total_ref[hh, rows, :] += sum(ds[:, c0:c0 + LANES] for c0 in range(0, ds.shape[1], LANES))
                for qc in range(rows.start // CHUNK, rows.stop // CHUNK):
                    lo = max(_BAND_START[qc], bias_cols.start)
                    hi = min(_BAND_START[qc] + _BAND_WIDTH, bias_cols.stop)
                    if lo < hi:
                        band_ref[hh, qc, :, lo - _BAND_START[qc]:hi - _BAND_START[qc]] += ds[
                            qc * CHUNK - rows.start:(qc + 1) * CHUNK - rows.start,
                            lo - bias_cols.start:hi - bias_cols.start]
                p_scr[hh, rows, cols] = p.astype(BF16)
                dsb_scr[hh, rows, cols] = ds.astype(BF16)
                if rest is not None:
                    p_scr[hh, rows, rest] = jnp.zeros((CHUNK, LANES), BF16)
                    dsb_scr[hh, rows, rest] = jnp.zeros((CHUNK, LANES), BF16)

        def ring(block):
            return block % KV_BLOCKS if isinstance(block, int) else lax.rem(block, KV_BLOCKS)

        def accumulate(i, hh):
            nwin = nwin_of(i)
            w = nwin * Q_BLOCK
            win = _block_rows(i + 1 - nwin, nwin)
            q_h, do_h = operands(i, hh)
            ds_b = dsb_scr[hh, :, :w]
            dq_h = _nn(ds_b, k_ref[win, :]) * ATTN_SCALE
            dkw = _tn(ds_b, q_h)
            dvw = _tn(p_scr[hh, :, :w], do_h)
            for b in range(nwin):
                slot = ring(i + 1 - nwin + b)
                part = slice(b * Q_BLOCK, (b + 1) * Q_BLOCK)
                if hh == 0 and b == nwin - 1:
                    dk_acc[slot] = dkw[part]
                    dv_acc[slot] = dvw[part]
                else:
                    dk_acc[slot] += dkw[part]
                    dv_acc[slot] += dvw[part]
            if hh == 0:
                dq_scr[...] = dq_h
            else:
                dq_ref[_block_rows(i, 1), :] = jnp.where(_head_lanes(0), dq_scr[...], dq_h).astype(BF16)
                if not (isinstance(i, int) and i < KV_BLOCKS - 1):
                    flush(i - (KV_BLOCKS - 1))

        def flush(block):
            dk_ref[_block_rows(block, 1), :] = dk_acc[ring(block)].astype(BF16)
            dv_ref[_block_rows(block, 1), :] = dv_acc[ring(block)].astype(BF16)

        def tile(n):
            return n // 2, n % 2

        def step(n):
            if n + 1 < 2 * nb:
                products(*tile(n + 1))
            grads(*tile(n))
            if n >= 1:
                accumulate(*tile(n - 1))

        products(0, 0)
        for n in range(2 * KV_BLOCKS):
            step(n)

        def two_steps(i, carry):
            products(i, 1)
            grads(i, 0)
            accumulate(i - 1, 1)
            products(i + 1, 0)
            grads(i, 1)
            accumulate(i, 0)
            return carry

        lax.fori_loop(KV_BLOCKS, nb - 1, two_steps, 0, unroll=7)
        step(2 * nb - 2)
        step(2 * nb - 1)
        accumulate(nb - 1, 1)
        flush(nb - 2)
        flush(nb - 1)

        @pl.when(pl.program_id(0) == N_PAIRS - 1)
        def _():
            for cp in exchange:
                cp.wait_recv()
            for cp in exchange:
                cp.wait_send()

    col = lambda c0: pl.BlockSpec((t, PAIR), lambda j: (0, c0 + j))
    tile_spec = pl.BlockSpec((2, Q_BLOCK, KV_WINDOW), lambda j: (j, 0, 0))
    out = jax.ShapeDtypeStruct((t, ATTN_WIDTH), BF16)
    return pl.pallas_call(
        body,
        name="attn_bwd",
        grid=(N_PAIRS,),
        out_shape=(out, out, out,
                   jax.ShapeDtypeStruct((N_HEADS, Q_BLOCK // CHUNK, CHUNK, _BAND_WIDTH), F32),
                   jax.ShapeDtypeStruct((N_HEADS, Q_BLOCK, LANES), F32),
                   jax.ShapeDtypeStruct((N_DEV - 1, OUT_SHARD, D_MODEL), BF16)),
        in_specs=[col(0), col(N_PAIRS), col(2 * N_PAIRS), col(0), tile_spec, pl.BlockSpec(memory_space=pl.ANY)],
        out_specs=(col(0), col(0), col(0),
                   pl.BlockSpec((2, Q_BLOCK // CHUNK, CHUNK, _BAND_WIDTH), lambda j: (j, 0, 0, 0)),
                   pl.BlockSpec((2, Q_BLOCK, LANES), lambda j: (j, 0, 0)),
                   pl.BlockSpec(memory_space=pl.ANY)),
        scratch_shapes=[
            pltpu.VMEM((2, Q_BLOCK, KV_WINDOW), F32),
            pltpu.VMEM((2, Q_BLOCK, KV_WINDOW), F32),
            pltpu.VMEM((2, Q_BLOCK, KV_WINDOW), BF16),
            pltpu.VMEM((2, Q_BLOCK, KV_WINDOW), BF16),
            pltpu.VMEM((Q_BLOCK, PAIR), F32),
            pltpu.VMEM((KV_BLOCKS, Q_BLOCK, PAIR), F32),
            pltpu.VMEM((KV_BLOCKS, Q_BLOCK, PAIR), F32),
            pltpu.SemaphoreType.DMA((N_DEV - 1,)),
            pltpu.SemaphoreType.DMA((N_DEV - 1,)),
        ],
        compiler_params=_params(("arbitrary",)),
    )(qkv, qkv, qkv, do, bias_tile, dwout_g)


def _outproj_loss(x2d, tgt2d, y_pool, o, ag, wout_g, g2):
    t = x2d.shape[0]
    n_tiles = t // TOKEN_TILE

    def body(x_ref, tgt_ref, yp_ref, o_ref, ag_ref, w_ref, g_ref,
             dx2_ref, dyp_ref, do_ref, dag_ref, dw_ref, dg_ref, loss_ref, acc_ref):
        i = pl.program_id(0)

        @pl.when(i == 0)
        def _():
            acc_ref[...] = jnp.zeros_like(acc_ref)
            dg_ref[...] = jnp.zeros_like(dg_ref)
            loss_ref[...] = jnp.zeros_like(loss_ref)

        w = w_ref[...].reshape(D_MODEL, D_MODEL)
        ga = ag_ref[...]
        sig = _sigmoid(ga)
        gate = ga * sig
        of = o_ref[...].astype(F32)
        y = jnp.concatenate([yp_ref[...], (of * gate).astype(BF16)], axis=1)
        x2 = x_ref[...] + _nn(y, w)
        r = lax.rsqrt(jnp.mean(x2 * x2, axis=-1, keepdims=True) + EPS)
        xh = x2 * r
        g = g_ref[...]
        diff = xh * g - tgt_ref[...]
        tok = jnp.sum(diff * diff, axis=-1, keepdims=True) * (1.0 / D_MODEL)
        loss_ref[...] += jnp.sum(tok, axis=0, keepdims=True)
        dout = diff * (1.0 / D_MODEL)
        dg_ref[...] += jnp.sum(dout * xh, axis=0, keepdims=True)
        u = dout * g
        dx2 = r * (u - xh * jnp.mean(u * xh, axis=-1, keepdims=True))
        dx2_ref[...] = dx2
        dx2_b = dx2.astype(BF16)
        dy = _nt(dx2_b, w)
        dyp_ref[...] = dy[:, :POOL_WIDTH].astype(BF16)
        dya = dy[:, POOL_WIDTH:]
        do_ref[...] = (dya * gate).astype(BF16)
        dag_ref[...] = ((dya * of) * (sig * (1.0 + ga * (1.0 - sig)))).astype(BF16)
        acc_ref[...] += _tn(y, dx2_b)

        @pl.when(i == n_tiles - 1)
        def _():
            dw_ref[...] = acc_ref[...].reshape(N_DEV, OUT_SHARD, D_MODEL).astype(BF16)

    tile = lambda width: pl.BlockSpec((TOKEN_TILE, width), lambda i: (i, 0))
    return pl.pallas_call(
        body,
        name="outproj_loss",
        grid=(n_tiles,),
        out_shape=(
            jax.ShapeDtypeStruct((t, D_MODEL), F32),
            jax.ShapeDtypeStruct((t, POOL_WIDTH), BF16),
            jax.ShapeDtypeStruct((t, ATTN_WIDTH), BF16),
            jax.ShapeDtypeStruct((t, ATTN_WIDTH), BF16),
            jax.ShapeDtypeStruct((N_DEV, OUT_SHARD, D_MODEL), BF16),
            jax.ShapeDtypeStruct((1, D_MODEL), F32),
            jax.ShapeDtypeStruct((8, LANES), F32),
        ),
        in_specs=[
            tile(D_MODEL), tile(D_MODEL), tile(POOL_WIDTH), tile(ATTN_WIDTH), tile(ATTN_WIDTH),
            pl.BlockSpec((N_DEV, OUT_SHARD, D_MODEL), lambda i: (0, 0, 0)),
            pl.BlockSpec((1, D_MODEL), lambda i: (0, 0)),
        ],
        out_specs=(
            tile(D_MODEL), tile(POOL_WIDTH), tile(ATTN_WIDTH), tile(ATTN_WIDTH),
            pl.BlockSpec((N_DEV, OUT_SHARD, D_MODEL), lambda i: (0, 0, 0)),
            pl.BlockSpec((1, D_MODEL), lambda i: (0, 0)),
            pl.BlockSpec((8, LANES), lambda i: (0, 0)),
        ),
        scratch_shapes=[pltpu.VMEM((D_MODEL, D_MODEL), F32)],
        compiler_params=_params(("arbitrary",)),
    )(x2d, tgt2d, y_pool, o, ag, wout_g, g2)


def _dproj_specs():
    tile = lambda width: pl.BlockSpec((TOKEN_TILE, width), lambda i: (i, 0))
    return [tile(2 * POOL_WIDTH)] + [tile(ATTN_WIDTH)] * 4


def _inproj_bwd_dx(x2d, dx2, dproj, g1, wg):
    t = x2d.shape[0]

    def body(x_ref, dx2_ref, dp_ref, dq_ref, dk_ref, dv_ref, dag_ref, g_ref, wg_hbm, gx_ref, dg_ref, wfull_ref, sem):
        @pl.when(pl.program_id(0) == 0)
        def _():
            _load_w_in(wg_hbm, wfull_ref, sem)
            dg_ref[...] = jnp.zeros_like(dg_ref)

        dproj_t = jnp.concatenate([dp_ref[...], dq_ref[...], dk_ref[...], dv_ref[...], dag_ref[...]], axis=1)
        dh = _nt(dproj_t, wfull_ref[...])
        xf = x_ref[...]
        r = lax.rsqrt(jnp.mean(xf * xf, axis=-1, keepdims=True) + EPS)
        xh = xf * r
        dg_ref[...] += jnp.sum(dh * xh, axis=0, keepdims=True)
        u = dh * g_ref[...]
        gx_ref[...] = dx2_ref[...] + r * (u - xh * jnp.mean(u * xh, axis=-1, keepdims=True))

    tile = pl.BlockSpec((TOKEN_TILE, D_MODEL), lambda i: (i, 0))
    return pl.pallas_call(
        body,
        name="inproj_bwd_dx",
        grid=(t // TOKEN_TILE,),
        out_shape=(jax.ShapeDtypeStruct((t, D_MODEL), F32), jax.ShapeDtypeStruct((1, D_MODEL), F32)),
        in_specs=[tile, tile] + _dproj_specs() + [
            pl.BlockSpec((1, D_MODEL), lambda i: (0, 0)),
            pl.BlockSpec(memory_space=pl.ANY),
        ],
        out_specs=(tile, pl.BlockSpec((1, D_MODEL), lambda i: (0, 0))),
        scratch_shapes=[pltpu.VMEM((D_MODEL, IN_WIDTH), BF16), pltpu.SemaphoreType.DMA((N_DEV,))],
        compiler_params=_params(("arbitrary",)),
    )(x2d, dx2, *dproj, g1, wg)


def _inproj_bwd_dw(x2d, pvg, dy_pool, dattn, g1, pool_w, pool_scale):
    t = x2d.shape[0]
    n_tiles = t // TOKEN_TILE
    halo_per_tile = TOKEN_TILE // HALO
    last_halo = t // HALO - 1

    def body(x_ref, cur_ref, prev_ref, pgn_ref, dy_ref, dyn_ref, dq_ref, dk_ref, dv_ref, dag_ref, g_ref, pw_ref, ps_ref,
             out_ref, dp_ref, dpw_ref, dps_ref, acc_ref):
        i = pl.program_id(0)

        @pl.when(i == 0)
        def _():
            acc_ref[...] = jnp.zeros_like(acc_ref)
            dpw_ref[...] = jnp.zeros_like(dpw_ref)
            dps_ref[...] = jnp.zeros_like(dps_ref)

        xf = x_ref[...]
        r = lax.rsqrt(jnp.mean(xf * xf, axis=-1, keepdims=True) + EPS)
        h = ((xf * r) * g_ref[...]).astype(BF16)
        half = ATTN_WIDTH // 2
        for gi, ref in enumerate((dq_ref, dk_ref, dv_ref, dag_ref)):
            pool_halves = _pool_bwd_group(gi, i, n_tiles, cur_ref, prev_ref, pgn_ref, dy_ref, dyn_ref, pw_ref, ps_ref,
                                          dp_ref, dpw_ref, dps_ref)
            for c0 in (0, half):
                col = 2 * POOL_WIDTH + gi * ATTN_WIDTH + c0
                acc_ref[:, col:col + half] += _tn(h, ref[:, c0:c0 + half])
                next(pool_halves, None)
        for c0 in (0, POOL_WIDTH):
            acc_ref[:, c0:c0 + POOL_WIDTH] += _tn(h, dp_ref[:, c0:c0 + POOL_WIDTH])

        @pl.when(i == n_tiles - 1)
        def _():
            for d in range(N_DEV):
                out_ref[d] = acc_ref[:, d * IN_SHARD:(d + 1) * IN_SHARD].astype(BF16)

    tile = lambda width: pl.BlockSpec((TOKEN_TILE, width), lambda i: (i, 0))
    next_halo = lambda col: pl.BlockSpec(
        (HALO, POOL_WIDTH), lambda i: (jnp.minimum((i + 1) * halo_per_tile, last_halo), col))
    return pl.pallas_call(
        body,
        name="inproj_bwd_dw",
        grid=(n_tiles,),
        out_shape=(
            jax.ShapeDtypeStruct((N_DEV, D_MODEL, IN_SHARD), BF16),
            jax.ShapeDtypeStruct((t, 2 * POOL_WIDTH), BF16),
            jax.ShapeDtypeStruct((N_GROUPS, GROUP_DIM, GROUP_DIM), F32),
            jax.ShapeDtypeStruct((1, POOL_WIDTH), F32),
        ),
        in_specs=[
            tile(D_MODEL),
            tile(2 * POOL_WIDTH),
            pl.BlockSpec((HALO, POOL_WIDTH), lambda i: (jnp.maximum(i * halo_per_tile - 1, 0), 0)),
            next_halo(1),
            tile(POOL_WIDTH),
            next_halo(0),
            tile(ATTN_WIDTH), tile(ATTN_WIDTH), tile(ATTN_WIDTH), tile(ATTN_WIDTH),
            pl.BlockSpec((1, D_MODEL), lambda i: (0, 0)),
            pl.BlockSpec((N_GROUPS, GROUP_DIM, GROUP_DIM), lambda i: (0, 0, 0)),
            pl.BlockSpec((1, POOL_WIDTH), lambda i: (0, 0)),
        ],
        out_specs=(
            pl.BlockSpec((N_DEV, D_MODEL, IN_SHARD), lambda i: (0, 0, 0)),
            tile(2 * POOL_WIDTH),
            pl.BlockSpec((N_GROUPS, GROUP_DIM, GROUP_DIM), lambda i: (0, 0, 0)),
            pl.BlockSpec((1, POOL_WIDTH), lambda i: (0, 0)),
        ),
        scratch_shapes=[pltpu.VMEM((D_MODEL, IN_WIDTH), F32)],
        compiler_params=_params(("arbitrary",)),
    )(x2d, pvg, pvg, pvg, dy_pool, dy_pool, *dattn, g1, pool_w, pool_scale)


_HBM = pl.BlockSpec(memory_space=pltpu.HBM)
_SEM = pl.BlockSpec(memory_space=pltpu.SEMAPHORE)
_DATAFLOW = pltpu.SideEffectType.DATAFLOW_SIDE_EFFECTING


_N_EXCHANGED = 3


def _exchange_all(refs, send_sems, recv_sems):
    win_hbm, win_land, pw_hbm, pw_land, vec_hbm, vec_land = refs
    return (_exchange_copies(win_hbm, win_land, send_sems, recv_sems)
            + _exchange_copies(pw_hbm, pw_land, send_sems, recv_sems, first_sem=N_DEV - 1)
            + _exchange_copies(vec_hbm, vec_land, send_sems, recv_sems, first_sem=2 * (N_DEV - 1), same_for_all=True))


def _exchange_start(win_blocks, pw_blocks, vec):
    arrays = []
    for a, land_shape in ((win_blocks, (N_DEV - 1,) + win_blocks.shape[1:]),
                          (pw_blocks, (N_DEV - 1,) + pw_blocks.shape[1:]),
                          (vec, (N_DEV - 1,) + vec.shape)):
        arrays += [pltpu.with_memory_space_constraint(a, pltpu.HBM),
                   pltpu.with_memory_space_constraint(lax.empty(land_shape, a.dtype), pltpu.HBM)]

    def body(*refs):
        ins, (send_sems, recv_sems), token = refs[:2 * _N_EXCHANGED], refs[2 * _N_EXCHANGED:2 * _N_EXCHANGED + 2], refs[-1]
        for cp in _exchange_all(ins, send_sems, recv_sems):
            cp.start()
        token[...] = jnp.zeros_like(token)

    sems = pltpu.SemaphoreType.DMA((_N_EXCHANGED * (N_DEV - 1),))
    return pl.pallas_call(
        body,
        name="exchange_start",
        out_shape=(sems, sems, *[pltpu.HBM(a.shape, a.dtype) for a in arrays], jax.ShapeDtypeStruct((8, LANES), F32)),
        in_specs=tuple([_HBM] * len(arrays)),
        out_specs=(_SEM, _SEM, *[_HBM] * len(arrays), pl.BlockSpec(memory_space=pltpu.VMEM)),
        input_output_aliases={k: 2 + k for k in range(len(arrays))},
        compiler_params=pltpu.CompilerParams(has_side_effects=_DATAFLOW),
    )(*arrays)


def _exchange_wait(send_sems, recv_sems, arrays, after):
    def body(*refs):
        ins = refs[:2 * _N_EXCHANGED]
        send_sems, recv_sems = refs[2 * _N_EXCHANGED:2 * _N_EXCHANGED + 2]
        for cp in _exchange_all(ins, send_sems, recv_sems):
            cp.wait_send()
            cp.wait_recv()

    return pl.pallas_call(
        body,
        name="exchange_wait",
        out_shape=tuple(pltpu.HBM(a.shape, a.dtype) for a in arrays),
        in_specs=(*[_HBM] * len(arrays), _SEM, _SEM, pl.BlockSpec(memory_space=pl.ANY)),
        out_specs=tuple([_HBM] * len(arrays)),
        input_output_aliases={k: k for k in range(len(arrays))},
        compiler_params=pltpu.CompilerParams(has_side_effects=_DATAFLOW),
    )(*arrays, send_sems, recv_sems, after)


def _adamw(w, g, m, v):
    m = ADAM_B1 * m + (1.0 - ADAM_B1) * g
    v = ADAM_B2 * v + (1.0 - ADAM_B2) * (g * g)
    m_hat = m / (1.0 - ADAM_B1 ** ADAM_STEP)
    v_hat = v / (1.0 - ADAM_B2 ** ADAM_STEP)
    delta = -ADAM_LR * (m_hat / (jnp.sqrt(v_hat) + ADAM_EPS) + ADAM_WD * w)
    return delta, m, v


def _small_allreduce(d_g1, vec, vec_land, pw_blocks, pw_land):
    def body(g1_ref, vec_ref, vland_ref, pwb_ref, pland_ref, vec_out, pw_out, vparts_ref, pparts_ref, rows_ref,
             slice_ref, send_sems, recv_sems):
        me = _dev_index(_mesh_pos())

        def from_devices(parts_ref, own, land_ref):
            parts_ref[0] = own
            parts_ref[1:] = land_ref[...]
            total = parts_ref[me]
            for s in range(1, N_DEV):
                total = total + parts_ref[me ^ s]
            return total

        slice_ref[...] = from_devices(pparts_ref, pwb_ref[me], pland_ref)

        def send(r, src, dst, k):
            return pltpu.make_async_remote_copy(
                src_ref=src, dst_ref=dst, send_sem=send_sems.at[2 * (r - 1) + k],
                recv_sem=recv_sems.at[2 * (r - 1) + k], device_id=_peer(r), device_id_type=MESH_ID)

        started = [cp for r in range(1, N_DEV)
                   for cp in (send(r, g1_ref, rows_ref.at[r], 0), send(r, slice_ref, pw_out.at[me], 1))]
        for cp in started:
            cp.start()
        rows_ref[0] = g1_ref[...]
        pw_out[me] = slice_ref[...]
        vec_out[...] = from_devices(vparts_ref, vec_ref[...], vland_ref)
        for cp in started:
            cp.wait_recv()
        for cp in started:
            cp.wait_send()
        g1 = rows_ref[me]
        for s in range(1, N_DEV):
            g1 = g1 + rows_ref[me ^ s]
        vec_out[_ROW_G1:_ROW_G1 + 1, :] = g1

    vm = pl.BlockSpec(memory_space=pltpu.VMEM)
    return pl.pallas_call(
        body,
        name="small_allreduce",
        out_shape=(jax.ShapeDtypeStruct((_VEC_ROWS, D_MODEL), F32),
                   jax.ShapeDtypeStruct((N_DEV, GROUP_DIM // 2, GROUP_DIM), F32)),
        in_specs=[vm] * 5,
        out_specs=(vm, vm),
        scratch_shapes=[
            pltpu.VMEM((N_DEV, _VEC_ROWS, D_MODEL), F32),
            pltpu.VMEM((N_DEV, GROUP_DIM // 2, GROUP_DIM), F32),
            pltpu.VMEM((N_DEV, 1, D_MODEL), F32),
            pltpu.VMEM((GROUP_DIM // 2, GROUP_DIM), F32),
            pltpu.SemaphoreType.DMA((2 * (N_DEV - 1),)),
            pltpu.SemaphoreType.DMA((2 * (N_DEV - 1),)),
        ],
        compiler_params=_params(),
    )(d_g1, vec, vec_land, pw_blocks, pw_land)


def _adamw_all(dwin_g, land_in, dwout_g, land_out, vec_sum, pw_sum, weights, big):
    small_shapes = [(1, D_MODEL), (1, D_MODEL), (1, POOL_WIDTH), (N_HEADS, 2 * LANES), (N_GROUPS, GROUP_DIM, GROUP_DIM)]

    def body(*refs):
        refs = list(refs)
        take = lambda n: [refs.pop(0) for _ in range(n)]
        dwin_hbm, lin_ref, dwout_hbm, lout_ref, vec_ref, pw_ref = take(6)
        small_wmv = [take(3) for _ in range(5)]
        big_wmv = [take(3) for _ in range(2)]
        big_out = [take(4) for _ in range(2)]
        small_out = [take(4) for _ in range(5)]
        own_in, own_out, local_sems = refs

        me = _dev_index(_mesh_pos())
        mine = [pltpu.make_async_copy(dwin_hbm.at[me], own_in, local_sems.at[0]),
                pltpu.make_async_copy(dwout_hbm.at[me], own_out, local_sems.at[1])]
        for cp in mine:
            cp.start()

        def update(g, wmv, outs):
            delta, m_new, v_new = _adamw(wmv[0][...], g, wmv[1][...], wmv[2][...])
            for ref, val in zip(outs, (g, delta, m_new, v_new)):
                ref[...] = val

        update(vec_ref[_ROW_G1:_ROW_G1 + 1, :], small_wmv[0], small_out[0])
        update(vec_ref[_ROW_G2:_ROW_G2 + 1, :], small_wmv[1], small_out[1])
        update(vec_ref[_ROW_PS:_ROW_PS + 1, :POOL_WIDTH], small_wmv[2], small_out[2])
        update(vec_ref[_ROW_RB:_ROW_RB + N_HEADS, :2 * LANES], small_wmv[3], small_out[3])
        update(pw_ref[...], small_wmv[4], small_out[4])
        for cp in mine:
            cp.wait()
        g_in = own_in[...].astype(F32)
        g_out = own_out[...].astype(F32)
        for r in range(N_DEV - 1):
            g_in = g_in + lin_ref[r].astype(F32)
            g_out = g_out + lout_ref[r].astype(F32)
        update(g_in, big_wmv[0], big_out[0])
        update(g_out, big_wmv[1], big_out[1])

    vm = pl.BlockSpec(memory_space=pltpu.VMEM)
    hbm = pl.BlockSpec(memory_space=pl.ANY)
    f32 = lambda shape: jax.ShapeDtypeStruct(shape, F32)
    out_shapes = [f32((D_MODEL, IN_SHARD))] * 4 + [f32((OUT_SHARD, D_MODEL))] * 4
    for shape in small_shapes:
        out_shapes += [f32(shape)] * 4
    args = [dwin_g, land_in, dwout_g, land_out, vec_sum, pw_sum]
    for wmv in weights:
        args += list(wmv)
    for wmv in big:
        args += list(wmv)
    return pl.pallas_call(
        body,
        name="adamw_all",
        out_shape=tuple(out_shapes),
        in_specs=[hbm, vm, hbm, vm] + [vm] * (len(args) - 4),
        out_specs=tuple([vm] * len(out_shapes)),
        scratch_shapes=[
            pltpu.VMEM((D_MODEL, IN_SHARD), BF16),
            pltpu.VMEM((OUT_SHARD, D_MODEL), BF16),
            pltpu.SemaphoreType.DMA((2,)),
        ],
        compiler_params=_params(),
    )(*args)


def kernel(x, norm_gain, w_in, pool_w, pool_scale, rel_bias, w_out, final_norm_gain, loss_target, m_norm_gain, m_w_in, m_pool_w, m_pool_scale, m_rel_bias, m_w_out, m_final_norm_gain, v_norm_gain, v_w_in, v_pool_w, v_pool_scale, v_rel_bias, v_w_out, v_final_norm_gain):
    t = x.shape[1]
    assert x.shape[0] == 1 and t % TOKEN_TILE == 0 and t // Q_BLOCK >= 4
    x2d = x[0]
    tgt2d = loss_target[0]
    g2 = final_norm_gain.reshape(1, D_MODEL)

    rb = rel_bias[0]
    rel_line = jnp.concatenate([
        jnp.broadcast_to(rb[:, :1], (N_HEADS, _REL_FIRST)), rb,
        jnp.broadcast_to(rb[:, N_REL - 1:], (N_HEADS, TOEPLITZ - _REL_FIRST - N_REL)),
    ], axis=1).reshape(N_HEADS, 1, TOEPLITZ)
    wg_in, bias_tile = _gather_weights(w_in[0], rel_line)

    pvg, qkv, ag, y_pool = _norm_inproj(x2d, norm_gain, wg_in, pool_w[0], pool_scale)
    o, wg_out = _attn_fwd(qkv, bias_tile, w_out[0])
    dx2, dy_pool, do, dag, dwout_g, d_g2, loss_sum = _outproj_loss(x2d, tgt2d, y_pool, o, ag, wg_out, g2)
    dq, dk, dv, ds_band, ds_total, land_out = _attn_bwd(qkv, do, bias_tile, dwout_g)
    dwin_g, d_pool, d_pw, d_ps = _inproj_bwd_dw(x2d, pvg, dy_pool, (dq, dk, dv, dag), norm_gain, pool_w[0], pool_scale)
    vec = _pack_small(ds_band, ds_total, d_g2, d_ps, loss_sum)
    dproj = (d_pool, dq, dk, dv, dag)
    pw_blocks = d_pw.reshape(N_DEV, GROUP_DIM // 2, GROUP_DIM)
    send_sems, recv_sems, *exchanged, token = _exchange_start(dwin_g, pw_blocks, vec)
    grad_x, d_g1 = _inproj_bwd_dx(x2d, dx2, dproj, norm_gain + token[:1, :1], wg_in)
    dwin_g, land_in, pw_blocks, pw_land, vec, vec_land = _exchange_wait(send_sems, recv_sems, exchanged, d_g1)

    pad_rb = lambda a: jnp.pad(a[0], ((0, 0), (0, 2 * LANES - N_REL)))
    row = lambda a: a.reshape(1, D_MODEL)
    weights = [
        (norm_gain, m_norm_gain, v_norm_gain),
        (row(final_norm_gain), row(m_final_norm_gain), row(v_final_norm_gain)),
        (pool_scale, m_pool_scale, v_pool_scale),
        (pad_rb(rel_bias), pad_rb(m_rel_bias), pad_rb(v_rel_bias)),
        (pool_w[0], m_pool_w[0], v_pool_w[0]),
    ]
    big = [(w_in[0], m_w_in[0], v_w_in[0]), (w_out[0], m_w_out[0], v_w_out[0])]
    vec_sum, pw_sum = _small_allreduce(d_g1, vec, vec_land, pw_blocks, pw_land)
    res = _adamw_all(dwin_g, land_in, dwout_g, land_out, vec_sum, pw_sum.reshape(N_GROUPS, GROUP_DIM, GROUP_DIM),
                     weights, big)
    loss = 0.5 * vec_sum[_ROW_LOSS, 0]

    def leaves(k):
        g1_, g2_, ps_, rb_, pw_ = (res[8 + 4 * leaf + k] for leaf in range(5))
        return [g1_, res[k][None], pw_[None], ps_, rb_[None, :, :N_REL], res[4 + k][None], g2_.reshape(D_MODEL)]

    return (loss, grad_x[None], *leaves(0), *leaves(1), *leaves(2), *leaves(3))
```

```python
import math

import jax
import jax.numpy as jnp
from jax import lax
from jax.experimental import pallas as pl
from jax.experimental.pallas import tpu as pltpu

F32 = jnp.float32
BF16 = jnp.bfloat16
MESH_ID = pl.DeviceIdType.MESH

D_MODEL = 1024
POOL_WIDTH = 512
ATTN_WIDTH = 512
POOL_WINDOWS = (2, 4, 8, 16)
N_GROUPS = 4
GROUP_DIM = 128
HEAD_DIM = 64
N_HEADS = 8
CHUNK = 64
LEFT_CHUNKS = 8
MAX_REL = 64
N_REL = 2 * MAX_REL + 1
IN_WIDTH = 2 * POOL_WIDTH + 4 * ATTN_WIDTH
EPS = 1e-6
MASK_VALUE = -1e30
ATTN_SCALE = 1.0 / math.sqrt(HEAD_DIM)
ADAM_LR = 0.001
ADAM_B1 = 0.9
ADAM_B2 = 0.999
ADAM_EPS = 1e-08
ADAM_WD = 0.01
ADAM_STEP = 10

N_DEV = 8
IN_SHARD = IN_WIDTH // N_DEV
OUT_SHARD = D_MODEL // N_DEV

LANES = 128
TOKEN_TILE = 512
HALO = 16
Q_BLOCK = 256
KV_BLOCKS = 3
KV_WINDOW = KV_BLOCKS * Q_BLOCK
PAIR = 2 * HEAD_DIM
N_PAIRS = N_HEADS // 2
BLOCK_LOOP_UNROLL = 7
TOEPLITZ = 1024
VMEM_LIMIT = 56 * 1024 * 1024


def _params(sem=None, vmem=VMEM_LIMIT):
    return pltpu.CompilerParams(dimension_semantics=sem, vmem_limit_bytes=vmem)


def _sigmoid(x):
    return 1.0 / (1.0 + jnp.exp(-x))


def _nt(a, b):
    return lax.dot_general(a, b, (((1,), (1,)), ((), ())), preferred_element_type=F32)


def _tn(a, b):
    return lax.dot_general(a, b, (((0,), (0,)), ((), ())), preferred_element_type=F32)


def _nn(a, b):
    return jnp.dot(a, b, preferred_element_type=F32)


def _mesh_pos():
    return lax.axis_index("x"), lax.axis_index("y"), lax.axis_index("c")


def _dev_index(p):
    return 4 * p[0] + 2 * p[1] + p[2]


def _peer(r):
    x, y, c = _mesh_pos()
    return (x ^ ((r >> 2) & 1), y ^ ((r >> 1) & 1), c ^ (r & 1))


def _exchange_copies(src_hbm, land_hbm, send_sems, recv_sems, first_sem=0, same_for_all=False):
    return [
        pltpu.make_async_remote_copy(
            src_ref=src_hbm if same_for_all else src_hbm.at[_dev_index(_peer(r))], dst_ref=land_hbm.at[r - 1],
            send_sem=send_sems.at[first_sem + r - 1], recv_sem=recv_sems.at[first_sem + r - 1],
            device_id=_peer(r), device_id_type=MESH_ID)
        for r in range(1, N_DEV)
    ]


def _gather_weights(w_in_shard, rel_line):
    def body(win_ref, line_ref, gin_ref, bias_ref, sin_ref, send_sems, recv_sems):
        x, y, c = _mesh_pos()
        me, sibling = (x, y, c), (x, y, 1 - c)
        chips = [(1 - x, y), (x, 1 - y), (1 - x, 1 - y)]

        sin_ref[...] = win_ref[...].astype(BF16)
        gin_ref[_dev_index(me)] = sin_ref[...]

        def copy(k, block, to, from_shard=False):
            return pltpu.make_async_remote_copy(
                src_ref=sin_ref if from_shard else gin_ref.at[_dev_index(block)],
                dst_ref=gin_ref.at[_dev_index(block)],
                send_sem=send_sems.at[k],
                recv_sem=recv_sems.at[k],
                device_id=to,
                device_id_type=MESH_ID,
            )

        first = [copy(0, me, sibling, True)]
        first += [copy(1 + j, me, (*chip, c), True) for j, chip in enumerate(chips)]
        for cp in first:
            cp.start()
        passed = [copy(4 + j, (*chip, c), sibling) for j, chip in enumerate(chips)]

        def bias_heads(lo, hi):
            for h in range(lo, hi):
                bias_ref[h] = _toeplitz_bias(line_ref[h])

        bias_heads(0, N_HEADS - 3)
        for j, chip in enumerate(chips):
            copy(1 + j, (*chip, c), me).wait_recv()
            passed[j].start()
            bias_heads(N_HEADS - 3 + j, N_HEADS - 2 + j)
        copy(0, sibling, me).wait_recv()
        for j, chip in enumerate(chips):
            copy(4 + j, (*chip, 1 - c), me).wait_recv()
        for cp in first + passed:
            cp.wait_send()

    vm = pl.BlockSpec(memory_space=pltpu.VMEM)
    return pl.pallas_call(
        body,
        name="gather_weights",
        out_shape=(
            jax.ShapeDtypeStruct((N_DEV, D_MODEL, IN_SHARD), BF16),
            jax.ShapeDtypeStruct((N_HEADS, Q_BLOCK, KV_WINDOW), F32),
        ),
        in_specs=[vm, vm],
        out_specs=(vm, vm),
        scratch_shapes=[
            pltpu.VMEM((D_MODEL, IN_SHARD), BF16),
            pltpu.SemaphoreType.DMA((7,)),
            pltpu.SemaphoreType.DMA((7,)),
        ],
        compiler_params=_params(),
    )(w_in_shard, rel_line)


def _load_w_in(wg_hbm, wfull_ref, sem):
    copies = [
        pltpu.make_async_copy(wg_hbm.at[d], wfull_ref.at[:, d * IN_SHARD:(d + 1) * IN_SHARD], sem.at[d])
        for d in range(N_DEV)
    ]
    for cp in copies:
        cp.start()
    for cp in copies:
        cp.wait()


def _norm_inproj(x2d, g1, wg, pool_w, pool_scale):
    t = x2d.shape[0]

    def body(x_ref, g_ref, wg_hbm, pw_ref, ps_ref, pvg_ref, qkv_ref, ag_ref, yp_ref, wfull_ref, halo_ref, sem):
        i = pl.program_id(0)

        @pl.when(i == 0)
        def _():
            _load_w_in(wg_hbm, wfull_ref, sem)
            halo_ref[...] = jnp.zeros_like(halo_ref)

        xf = x_ref[...]
        r = lax.rsqrt(jnp.mean(xf * xf, axis=-1, keepdims=True) + EPS)
        h = ((xf * r) * g_ref[...]).astype(BF16)
        chunk = lambda ci: _nn(h, wfull_ref[:, ci * POOL_WIDTH:(ci + 1) * POOL_WIDTH])
        pv, pg = chunk(0), chunk(1)
        pvg_ref[:, :POOL_WIDTH] = pv
        pvg_ref[:, POOL_WIDTH:] = pg
        halo = halo_ref[...]
        halo_ref[...] = pv[TOKEN_TILE - HALO:]
        for gi in range(N_GROUPS):
            if gi < 3:
                qkv_ref[:, gi * ATTN_WIDTH:(gi + 1) * ATTN_WIDTH] = chunk(2 + gi).astype(BF16)
            else:
                ag_ref[...] = chunk(5)
            sl = slice(gi * GROUP_DIM, (gi + 1) * GROUP_DIM)
            d = _pool_diffs(pv[:, sl], halo[:, sl], i * TOKEN_TILE, POOL_WINDOWS[gi])
            z = _nn(d.astype(BF16), pw_ref[gi].astype(BF16))
            g = pg[:, sl]
            yp_ref[:, sl] = ((z * ps_ref[:, sl]) * (g * _sigmoid(g))).astype(BF16)

    tile = lambda width: pl.BlockSpec((TOKEN_TILE, width), lambda i: (i, 0))
    return pl.pallas_call(
        body,
        name="norm_inproj",
        grid=(t // TOKEN_TILE,),
        out_shape=(
            jax.ShapeDtypeStruct((t, 2 * POOL_WIDTH), F32),
            jax.ShapeDtypeStruct((t, 3 * ATTN_WIDTH), BF16),
            jax.ShapeDtypeStruct((t, ATTN_WIDTH), F32),
            jax.ShapeDtypeStruct((t, POOL_WIDTH), BF16),
        ),
        in_specs=[
            tile(D_MODEL),
            pl.BlockSpec((1, D_MODEL), lambda i: (0, 0)),
            pl.BlockSpec(memory_space=pl.ANY),
            pl.BlockSpec((N_GROUPS, GROUP_DIM, GROUP_DIM), lambda i: (0, 0, 0)),
            pl.BlockSpec((1, POOL_WIDTH), lambda i: (0, 0)),
        ],
        out_specs=(tile(2 * POOL_WIDTH), tile(3 * ATTN_WIDTH), tile(ATTN_WIDTH), tile(POOL_WIDTH)),
        scratch_shapes=[
            pltpu.VMEM((D_MODEL, IN_WIDTH), BF16),
            pltpu.VMEM((HALO, POOL_WIDTH), F32),
            pltpu.SemaphoreType.DMA((N_DEV,)),
        ],
        compiler_params=_params(("arbitrary",)),
    )(x2d, g1, wg, pool_w, pool_scale)


def _inv_count(first_row, rows, window):
    tpos = first_row + lax.broadcasted_iota(jnp.int32, (rows, 1), 0)
    return 1.0 / jnp.minimum(tpos + 1, window).astype(F32)


def _causal_window_sum(ext, window):
    s, k = ext, 1
    while k < window:
        s = s + pltpu.roll(s, k, 0)
        k *= 2
    return s


def _pool_diffs(pv, halo, first_row, window):
    s = _causal_window_sum(jnp.concatenate([halo, pv], axis=0), window)[HALO:]
    return s * _inv_count(first_row, pv.shape[0], window) - pv


def _pool_bwd_group(gi, i, n_tiles, cur_ref, prev_ref, pgn_ref, dy_ref, dyn_ref, pw_ref, ps_ref,
                    dp_ref, dpw_ref, dps_ref):
    w = POOL_WINDOWS[gi]
    sl = slice(gi * GROUP_DIM, (gi + 1) * GROUP_DIM)
    gate_sl = slice(POOL_WIDTH + gi * GROUP_DIM, POOL_WIDTH + (gi + 1) * GROUP_DIM)
    rows = TOKEN_TILE + HALO
    pw = pw_ref[gi].astype(BF16)
    ps = ps_ref[:, sl]
    d = _pool_diffs(cur_ref[:, sl], jnp.where(i > 0, prev_ref[:, sl], 0.0), i * TOKEN_TILE, w).astype(BF16)
    z = _nn(d, pw)
    g_ext = jnp.concatenate([cur_ref[:, gate_sl], pgn_ref[:, sl]], axis=0)
    dy_ext = jnp.concatenate([dy_ref[:, sl], dyn_ref[:, sl]], axis=0).astype(F32)
    sig = _sigmoid(g_ext)
    gate = g_ext * sig
    dz_ext = ((dy_ext * gate) * ps).astype(BF16)
    dd_ext = _nt(dz_ext, pw)
    yield
    e = dd_ext * _inv_count(i * TOKEN_TILE, rows, w)
    row = lax.broadcasted_iota(jnp.int32, (rows, 1), 0)
    e = jnp.where(jnp.logical_or(row < TOKEN_TILE, i < n_tiles - 1), e, 0.0)
    s, k = e, 1
    while k < w:
        s = s + pltpu.roll(s, rows - k, 0)
        k *= 2
    dp_ref[:, sl] = (s[:TOKEN_TILE] - dd_ext[:TOKEN_TILE]).astype(BF16)
    dy = dy_ext[:TOKEN_TILE]
    g = g_ext[:TOKEN_TILE]
    sg = sig[:TOKEN_TILE]
    dgate = sg * (1.0 + g * (1.0 - sg))
    dp_ref[:, gate_sl] = ((dy * (z * ps)) * dgate).astype(BF16)
    dps_ref[:, sl] += jnp.sum((dy * gate[:TOKEN_TILE]) * z, axis=0, keepdims=True)
    dpw_ref[gi] += _tn(d, dz_ext[:TOKEN_TILE])


_REL_FIRST = KV_WINDOW - 1 - MAX_REL


def _skew_rows(a, right):
    rows, lanes = a.shape
    row = lax.broadcasted_iota(jnp.int32, a.shape, 0)
    for b in range(rows.bit_length() - 1):
        shift = (1 << b) if right else lanes - (1 << b)
        a = jnp.where((row >> b) & 1 == 1, pltpu.roll(a, shift, 1), a)
    return a


def _toeplitz_bias(line):
    a = jnp.broadcast_to(line, (Q_BLOCK, TOEPLITZ))
    a = _skew_rows(a, True)
    a = pltpu.roll(a, TOEPLITZ - (Q_BLOCK - 1), 1)
    a = a[:, :KV_WINDOW]
    qc = lax.broadcasted_iota(jnp.int32, a.shape, 0) // CHUNK
    kc = lax.broadcasted_iota(jnp.int32, a.shape, 1) // CHUNK
    visible = jnp.logical_and(kc >= qc, kc <= qc + LEFT_CHUNKS)
    return jnp.where(visible, a, MASK_VALUE)


_BAND_WIDTH = 2 * LANES
_BAND_START = (384, 512, 512, 512)
_BAND_REL = tuple(a - (KV_BLOCKS - 1) * Q_BLOCK - qc * CHUNK for qc, a in enumerate(_BAND_START))


_ROW_G1, _ROW_G2, _ROW_PS, _ROW_LOSS, _ROW_RB = 0, 1, 2, 3, 8
_VEC_ROWS = 16


def _pack_small(band, total, d_g2, d_ps, loss_row):
    def body(band_ref, total_ref, g2_ref, ps_ref, loss_ref, vec_ref):
        vec_ref[...] = jnp.zeros_like(vec_ref)
        vec_ref[_ROW_G2:_ROW_G2 + 1, :] = g2_ref[...]
        vec_ref[_ROW_PS:_ROW_PS + 1, :POOL_WIDTH] = ps_ref[...]
        vec_ref[_ROW_LOSS:_ROW_LOSS + 1, :LANES] = loss_ref[0:1, :]
        for h in range(N_HEADS):
            a = jnp.zeros((CHUNK, 2 * _BAND_WIDTH), F32)
            for qc in range(Q_BLOCK // CHUNK):
                z = jnp.concatenate([band_ref[h, qc], jnp.zeros((CHUNK, _BAND_WIDTH), F32)], axis=1)
                left = -MAX_REL - _BAND_REL[qc]
                a = a + (pltpu.roll(z, 2 * _BAND_WIDTH - left, 1) if left else z)
            a = _skew_rows(a, False)
            near = jnp.sum(a, axis=0, keepdims=True)[:, :2 * LANES]
            r = lax.broadcasted_iota(jnp.int32, near.shape, 1)
            near = jnp.where(jnp.logical_and(r >= 1, r < 2 * MAX_REL), near, 0.0)
            everything = jnp.sum(jnp.sum(total_ref[h], axis=0, keepdims=True), axis=1, keepdims=True)
            far = everything - jnp.sum(near, axis=1, keepdims=True)
            vec_ref[_ROW_RB + h:_ROW_RB + h + 1, :2 * LANES] = jnp.where(r == 0, far, near)

    vm = pl.BlockSpec(memory_space=pltpu.VMEM)
    return pl.pallas_call(
        body,
        name="pack_small",
        out_shape=jax.ShapeDtypeStruct((_VEC_ROWS, D_MODEL), F32),
        in_specs=[vm] * 5,
        out_specs=vm,
        compiler_params=_params(),
    )(band, total, d_g2, d_ps, loss_row)


def _head_lanes(hh):
    lane = lax.broadcasted_iota(jnp.int32, (1, PAIR), 1)
    return (lane < HEAD_DIM) if hh == 0 else (lane >= HEAD_DIM)


def _score_windows(nwin):
    if nwin < KV_BLOCKS:
        return [(slice(0, Q_BLOCK), slice(0, nwin * Q_BLOCK), None)]
    pieces = []
    for qc in range(Q_BLOCK // CHUNK):
        rows = slice(qc * CHUNK, (qc + 1) * CHUNK)
        if qc < 2:
            pieces.append((rows, slice(0, KV_WINDOW - LANES), slice(KV_WINDOW - LANES, KV_WINDOW)))
        else:
            pieces.append((rows, slice(LANES, KV_WINDOW), slice(0, LANES)))
    return pieces


def _block_rows(first_block, n_blocks):
    if isinstance(first_block, int):
        return pl.ds(first_block * Q_BLOCK, n_blocks * Q_BLOCK)
    return pl.ds(pl.multiple_of(first_block * Q_BLOCK, Q_BLOCK), n_blocks * Q_BLOCK)


def _attn_fwd(qkv, bias_tile, w_out_shard):
    t = qkv.shape[0]
    nb = t // Q_BLOCK

    def body(q_ref, k_ref, v_ref, bias_ref, wout_ref, o_ref, gout_hbm,
             s_scr, p_scr, sout_ref, send_sems, recv_sems, local_sem):
        pair = pl.program_id(0)
        me = _dev_index(_mesh_pos())
        mine = pltpu.make_async_copy(sout_ref, gout_hbm.at[me], local_sem)

        def shard_copy(r, block):
            return pltpu.make_async_remote_copy(
                src_ref=sout_ref, dst_ref=gout_hbm.at[block], send_sem=send_sems.at[r - 1],
                recv_sem=recv_sems.at[r - 1], device_id=_peer(r), device_id_type=MESH_ID)

        @pl.when(pair == 0)
        def _():
            sout_ref[...] = wout_ref[...].astype(BF16)
            mine.start()
            for r in range(1, N_DEV):
                shard_copy(r, me).start()

        def scores(i, slot, nwin):
            q = q_ref[_block_rows(i, 1), :]
            kw = k_ref[_block_rows(i + 1 - nwin, nwin), :]
            for hh in range(2):
                q_h = jnp.where(_head_lanes(hh), q, jnp.zeros_like(q)) * ATTN_SCALE
                s_scr[slot, hh, :, :nwin * Q_BLOCK] = _nt(q_h, kw)

        def softmax(slot, nwin):
            off = (KV_BLOCKS - nwin) * Q_BLOCK
            for hh in range(2):
                for rows, cols, rest in _score_windows(nwin):
                    s = s_scr[slot, hh, rows, cols] + bias_ref[hh, rows, off + cols.start:off + cols.stop]
                    p_scr[slot, hh, rows, cols] = jnp.exp(s - jnp.max(s, axis=-1, keepdims=True)).astype(BF16)
                    if rest is not None:
                        p_scr[slot, hh, rows, rest] = jnp.zeros((CHUNK, LANES), BF16)

        def output(i, slot, nwin):
            vw = v_ref[_block_rows(i + 1 - nwin, nwin), :]
            outs = [_nn(p_scr[slot, hh, :, :nwin * Q_BLOCK], jnp.where(_head_lanes(hh), vw, jnp.ones_like(vw)))
                    for hh in range(2)]
            sums = pltpu.roll(jnp.where(_head_lanes(0), outs[1], outs[0]), HEAD_DIM, axis=1)
            o_ref[_block_rows(i, 1), :] = (jnp.where(_head_lanes(0), outs[0], outs[1]) * (1.0 / sums)).astype(BF16)

        scores(0, 0, 1)
        scores(1, 1, 2)
        softmax(0, 1)
        scores(2, 0, 3)
        softmax(1, 2)
        output(0, 0, 1)
        scores(3, 1, 3)
        softmax(0, 3)
        output(1, 1, 2)

        def two_steps(k, carry):
            i = 3 + 2 * k
            scores(i + 1, 0, KV_BLOCKS)
            softmax(1, KV_BLOCKS)
            output(i - 1, 0, KV_BLOCKS)
            scores(i + 2, 1, KV_BLOCKS)
            softmax(0, KV_BLOCKS)
            output(i, 1, KV_BLOCKS)
            return carry

        lax.fori_loop(0, (nb - 4) // 2, two_steps, 0, unroll=BLOCK_LOOP_UNROLL)
        last = (nb - 1) % 2
        softmax(last, KV_BLOCKS)
        output(nb - 2, 1 - last, KV_BLOCKS)
        output(nb - 1, last, KV_BLOCKS)

        @pl.when(pair == N_PAIRS - 1)
        def _():
            for r in range(1, N_DEV):
                shard_copy(r, _dev_index(_peer(r))).wait_recv()
            for r in range(1, N_DEV):
                shard_copy(r, me).wait_send()
            mine.wait()

    col = lambda c0: pl.BlockSpec((t, PAIR), lambda j: (0, c0 + j))
    return pl.pallas_call(
        body,
        name="attn_fwd",
        grid=(N_PAIRS,),
        out_shape=(
            jax.ShapeDtypeStruct((t, ATTN_WIDTH), BF16),
            jax.ShapeDtypeStruct((N_DEV, OUT_SHARD, D_MODEL), BF16),
        ),
        in_specs=[col(0), col(N_PAIRS), col(2 * N_PAIRS),
                  pl.BlockSpec((2, Q_BLOCK, KV_WINDOW), lambda j: (j, 0, 0)),
                  pl.BlockSpec((OUT_SHARD, D_MODEL), lambda j: (0, 0))],
        out_specs=(col(0), pl.BlockSpec(memory_space=pl.ANY)),
        scratch_shapes=[
            pltpu.VMEM((2, 2, Q_BLOCK, KV_WINDOW), F32),
            pltpu.VMEM((2, 2, Q_BLOCK, KV_WINDOW), BF16),
            pltpu.VMEM((OUT_SHARD, D_MODEL), BF16),
            pltpu.SemaphoreType.DMA((N_DEV - 1,)),
            pltpu.SemaphoreType.DMA((N_DEV - 1,)),
            pltpu.SemaphoreType.DMA,
        ],
        compiler_params=_params(("arbitrary",)),
    )(qkv, qkv, qkv, bias_tile, w_out_shard)


def _attn_bwd(qkv, do, bias_tile, dwout_g):
    t = qkv.shape[0]
    nb = t // Q_BLOCK

    def body(q_ref, k_ref, v_ref, do_ref, bias_ref, dwout_hbm,
             dq_ref, dk_ref, dv_ref, band_ref, total_ref, land_hbm,
             s_scr, dp_scr, p_scr, dsb_scr, dq_scr, dk_acc, dv_acc, send_sems, recv_sems):
        exchange = _exchange_copies(dwout_hbm, land_hbm, send_sems, recv_sems)

        @pl.when(pl.program_id(0) == 0)
        def _():
            for cp in exchange:
                cp.start()

        band_ref[...] = jnp.zeros_like(band_ref)
        total_ref[...] = jnp.zeros_like(total_ref)

        def nwin_of(i):
            return min(i + 1, KV_BLOCKS) if isinstance(i, int) else KV_BLOCKS

        def operands(i, hh):
            lanes = _head_lanes(hh)
            q = q_ref[_block_rows(i, 1), :]
            do = do_ref[_block_rows(i, 1), :]
            return (jnp.where(lanes, q, jnp.zeros_like(q)) * ATTN_SCALE, jnp.where(lanes, do, jnp.zeros_like(do)))

        def products(i, hh):
            nwin = nwin_of(i)
            win = _block_rows(i + 1 - nwin, nwin)
            q_h, do_h = operands(i, hh)
            s_scr[hh, :, :nwin * Q_BLOCK] = _nt(q_h, k_ref[win, :])
            dp_scr[hh, :, :nwin * Q_BLOCK] = _nt(do_h, v_ref[win, :])

        def grads(i, hh):
            nwin = nwin_of(i)
            off = (KV_BLOCKS - nwin) * Q_BLOCK
            for rows, cols, rest in _score_windows(nwin):
                bias_cols = slice(off + cols.start, off + cols.stop)
                s = s_scr[hh, rows, cols] + bias_ref[hh, rows, bias_cols]
                e = jnp.exp(s - jnp.max(s, axis=-1, keepdims=True))
                p = e * (1.0 / jnp.sum(e, axis=-1, keepdims=True))
                dp = dp_scr[hh, rows, cols]
                ds = p * (dp - jnp.sum(p * dp, axis=-1, keepdims=True))
                total_ref[hh, rows, :] += sum(ds[:, c0:c0 + LANES] for c0 in range(0, ds.shape[1], LANES))
                for qc in range(rows.start // CHUNK, rows.stop // CHUNK):
                    lo = max(_BAND_START[qc], bias_cols.start)
                    hi = min(_BAND_START[qc] + _BAND_WIDTH, bias_cols.stop)
                    if lo < hi:
                        band_ref[hh, qc, :, lo - _BAND_START[qc]:hi - _BAND_START[qc]] += ds[
                            qc * CHUNK - rows.start:(qc + 1) * CHUNK - rows.start,
                            lo - bias_cols.start:hi - bias_cols.start]
                p_scr[hh, rows, cols] = p.astype(BF16)
                dsb_scr[hh, rows, cols] = ds.astype(BF16)
                if rest is not None:
                    p_scr[hh, rows, rest] = jnp.zeros((CHUNK, LANES), BF16)
                    dsb_scr[hh, rows, rest] = jnp.zeros((CHUNK, LANES), BF16)

        def ring(block):
            return block % KV_BLOCKS if isinstance(block, int) else lax.rem(block, KV_BLOCKS)

        def accumulate(i, hh):
            nwin = nwin_of(i)
            w = nwin * Q_BLOCK
            win = _block_rows(i + 1 - nwin, nwin)
            q_h, do_h = operands(i, hh)
            ds_b = dsb_scr[hh, :, :w]
            dq_h = _nn(ds_b, k_ref[win, :]) * ATTN_SCALE
            dkw = _tn(ds_b, q_h)
            dvw = _tn(p_scr[hh, :, :w], do_h)
            for b in range(nwin):
                slot = ring(i + 1 - nwin + b)
                part = slice(b * Q_BLOCK, (b + 1) * Q_BLOCK)
                if hh == 0 and b == nwin - 1:
                    dk_acc[slot] = dkw[part]
                    dv_acc[slot] = dvw[part]
                else:
                    dk_acc[slot] += dkw[part]
                    dv_acc[slot] += dvw[part]
            if hh == 0:
                dq_scr[...] = dq_h
            else:
                dq_ref[_block_rows(i, 1), :] = jnp.where(_head_lanes(0), dq_scr[...], dq_h).astype(BF16)
                if not (isinstance(i, int) and i < KV_BLOCKS - 1):
                    flush(i - (KV_BLOCKS - 1))

        def flush(block):
            dk_ref[_block_rows(block, 1), :] = dk_acc[ring(block)].astype(BF16)
            dv_ref[_block_rows(block, 1), :] = dv_acc[ring(block)].astype(BF16)

        def tile(n):
            return n // 2, n % 2

        def step(n):
            if n + 1 < 2 * nb:
                products(*tile(n + 1))
            grads(*tile(n))
            if n >= 1:
                accumulate(*tile(n - 1))

        products(0, 0)
        for n in range(2 * KV_BLOCKS):
            step(n)

        def two_steps(i, carry):
            products(i, 1)
            grads(i, 0)
            accumulate(i - 1, 1)
            products(i + 1, 0)
            grads(i, 1)
            accumulate(i, 0)
            return carry

        lax.fori_loop(KV_BLOCKS, nb - 1, two_steps, 0, unroll=BLOCK_LOOP_UNROLL)
        step(2 * nb - 2)
        step(2 * nb - 1)
        accumulate(nb - 1, 1)
        flush(nb - 2)
        flush(nb - 1)

        @pl.when(pl.program_id(0) == N_PAIRS - 1)
        def _():
            for cp in exchange:
                cp.wait_recv()
            for cp in exchange:
                cp.wait_send()

    col = lambda c0: pl.BlockSpec((t, PAIR), lambda j: (0, c0 + j))
    tile_spec = pl.BlockSpec((2, Q_BLOCK, KV_WINDOW), lambda j: (j, 0, 0))
    out = jax.ShapeDtypeStruct((t, ATTN_WIDTH), BF16)
    return pl.pallas_call(
        body,
        name="attn_bwd",
        grid=(N_PAIRS,),
        out_shape=(out, out, out,
                   jax.ShapeDtypeStruct((N_HEADS, Q_BLOCK // CHUNK, CHUNK, _BAND_WIDTH), F32),
                   jax.ShapeDtypeStruct((N_HEADS, Q_BLOCK, LANES), F32),
                   jax.ShapeDtypeStruct((N_DEV - 1, OUT_SHARD, D_MODEL), BF16)),
        in_specs=[col(0), col(N_PAIRS), col(2 * N_PAIRS), col(0), tile_spec, pl.BlockSpec(memory_space=pl.ANY)],
        out_specs=(col(0), col(0), col(0),
                   pl.BlockSpec((2, Q_BLOCK // CHUNK, CHUNK, _BAND_WIDTH), lambda j: (j, 0, 0, 0)),
                   pl.BlockSpec((2, Q_BLOCK, LANES), lambda j: (j, 0, 0)),
                   pl.BlockSpec(memory_space=pl.ANY)),
        scratch_shapes=[
            pltpu.VMEM((2, Q_BLOCK, KV_WINDOW), F32),
            pltpu.VMEM((2, Q_BLOCK, KV_WINDOW), F32),
            pltpu.VMEM((2, Q_BLOCK, KV_WINDOW), BF16),
            pltpu.VMEM((2, Q_BLOCK, KV_WINDOW), BF16),
            pltpu.VMEM((Q_BLOCK, PAIR), F32),
            pltpu.VMEM((KV_BLOCKS, Q_BLOCK, PAIR), F32),
            pltpu.VMEM((KV_BLOCKS, Q_BLOCK, PAIR), F32),
            pltpu.SemaphoreType.DMA((N_DEV - 1,)),
            pltpu.SemaphoreType.DMA((N_DEV - 1,)),
        ],
        compiler_params=_params(("arbitrary",)),
    )(qkv, qkv, qkv, do, bias_tile, dwout_g)


def _outproj_loss(x2d, tgt2d, y_pool, o, ag, wout_g, g2):
    t = x2d.shape[0]
    n_tiles = t // TOKEN_TILE

    def body(x_ref, tgt_ref, yp_ref, o_ref, ag_ref, w_ref, g_ref,
             dx2_ref, dyp_ref, do_ref, dag_ref, dw_ref, dg_ref, loss_ref, acc_ref):
        i = pl.program_id(0)

        @pl.when(i == 0)
        def _():
            acc_ref[...] = jnp.zeros_like(acc_ref)
            dg_ref[...] = jnp.zeros_like(dg_ref)
            loss_ref[...] = jnp.zeros_like(loss_ref)

        w = w_ref[...].reshape(D_MODEL, D_MODEL)
        ga = ag_ref[...]
        sig = _sigmoid(ga)
        gate = ga * sig
        of = o_ref[...].astype(F32)
        y = jnp.concatenate([yp_ref[...], (of * gate).astype(BF16)], axis=1)
        x2 = x_ref[...] + _nn(y, w)
        r = lax.rsqrt(jnp.mean(x2 * x2, axis=-1, keepdims=True) + EPS)
        xh = x2 * r
        g = g_ref[...]
        diff = xh * g - tgt_ref[...]
        tok = jnp.sum(diff * diff, axis=-1, keepdims=True) * (1.0 / D_MODEL)
        loss_ref[...] += jnp.sum(tok, axis=0, keepdims=True)
        dout = diff * (1.0 / D_MODEL)
        dg_ref[...] += jnp.sum(dout * xh, axis=0, keepdims=True)
        u = dout * g
        dx2 = r * (u - xh * jnp.mean(u * xh, axis=-1, keepdims=True))
        dx2_ref[...] = dx2
        dx2_b = dx2.astype(BF16)
        dy = _nt(dx2_b, w)
        dyp_ref[...] = dy[:, :POOL_WIDTH].astype(BF16)
        dya = dy[:, POOL_WIDTH:]
        do_ref[...] = (dya * gate).astype(BF16)
        dag_ref[...] = ((dya * of) * (sig * (1.0 + ga * (1.0 - sig)))).astype(BF16)
        acc_ref[...] += _tn(y, dx2_b)

        @pl.when(i == n_tiles - 1)
        def _():
            dw_ref[...] = acc_ref[...].reshape(N_DEV, OUT_SHARD, D_MODEL).astype(BF16)

    tile = lambda width: pl.BlockSpec((TOKEN_TILE, width), lambda i: (i, 0))
    return pl.pallas_call(
        body,
        name="outproj_loss",
        grid=(n_tiles,),
        out_shape=(
            jax.ShapeDtypeStruct((t, D_MODEL), F32),
            jax.ShapeDtypeStruct((t, POOL_WIDTH), BF16),
            jax.ShapeDtypeStruct((t, ATTN_WIDTH), BF16),
            jax.ShapeDtypeStruct((t, ATTN_WIDTH), BF16),
            jax.ShapeDtypeStruct((N_DEV, OUT_SHARD, D_MODEL), BF16),
            jax.ShapeDtypeStruct((1, D_MODEL), F32),
            jax.ShapeDtypeStruct((8, LANES), F32),
        ),
        in_specs=[
            tile(D_MODEL), tile(D_MODEL), tile(POOL_WIDTH), tile(ATTN_WIDTH), tile(ATTN_WIDTH),
            pl.BlockSpec((N_DEV, OUT_SHARD, D_MODEL), lambda i: (0, 0, 0)),
            pl.BlockSpec((1, D_MODEL), lambda i: (0, 0)),
        ],
        out_specs=(
            tile(D_MODEL), tile(POOL_WIDTH), tile(ATTN_WIDTH), tile(ATTN_WIDTH),
            pl.BlockSpec((N_DEV, OUT_SHARD, D_MODEL), lambda i: (0, 0, 0)),
            pl.BlockSpec((1, D_MODEL), lambda i: (0, 0)),
            pl.BlockSpec((8, LANES), lambda i: (0, 0)),
        ),
        scratch_shapes=[pltpu.VMEM((D_MODEL, D_MODEL), F32)],
        compiler_params=_params(("arbitrary",)),
    )(x2d, tgt2d, y_pool, o, ag, wout_g, g2)


def _dproj_specs():
    tile = lambda width: pl.BlockSpec((TOKEN_TILE, width), lambda i: (i, 0))
    return [tile(2 * POOL_WIDTH)] + [tile(ATTN_WIDTH)] * 4


def _inproj_bwd_dx(x2d, dx2, dproj, g1, wg):
    t = x2d.shape[0]

    def body(x_ref, dx2_ref, dp_ref, dq_ref, dk_ref, dv_ref, dag_ref, g_ref, wg_hbm, gx_ref, dg_ref, wfull_ref, sem):
        @pl.when(pl.program_id(0) == 0)
        def _():
            _load_w_in(wg_hbm, wfull_ref, sem)
            dg_ref[...] = jnp.zeros_like(dg_ref)

        dproj_t = jnp.concatenate([dp_ref[...], dq_ref[...], dk_ref[...], dv_ref[...], dag_ref[...]], axis=1)
        dh = _nt(dproj_t, wfull_ref[...])
        xf = x_ref[...]
        r = lax.rsqrt(jnp.mean(xf * xf, axis=-1, keepdims=True) + EPS)
        xh = xf * r
        dg_ref[...] += jnp.sum(dh * xh, axis=0, keepdims=True)
        u = dh * g_ref[...]
        gx_ref[...] = dx2_ref[...] + r * (u - xh * jnp.mean(u * xh, axis=-1, keepdims=True))

    tile = pl.BlockSpec((TOKEN_TILE, D_MODEL), lambda i: (i, 0))
    return pl.pallas_call(
        body,
        name="inproj_bwd_dx",
        grid=(t // TOKEN_TILE,),
        out_shape=(jax.ShapeDtypeStruct((t, D_MODEL), F32), jax.ShapeDtypeStruct((1, D_MODEL), F32)),
        in_specs=[tile, tile] + _dproj_specs() + [
            pl.BlockSpec((1, D_MODEL), lambda i: (0, 0)),
            pl.BlockSpec(memory_space=pl.ANY),
        ],
        out_specs=(tile, pl.BlockSpec((1, D_MODEL), lambda i: (0, 0))),
        scratch_shapes=[pltpu.VMEM((D_MODEL, IN_WIDTH), BF16), pltpu.SemaphoreType.DMA((N_DEV,))],
        compiler_params=_params(("arbitrary",)),
    )(x2d, dx2, *dproj, g1, wg)


def _inproj_bwd_dw(x2d, pvg, dy_pool, dattn, g1, pool_w, pool_scale):
    t = x2d.shape[0]
    n_tiles = t // TOKEN_TILE
    halo_per_tile = TOKEN_TILE // HALO
    last_halo = t // HALO - 1

    def body(x_ref, cur_ref, prev_ref, pgn_ref, dy_ref, dyn_ref, dq_ref, dk_ref, dv_ref, dag_ref, g_ref, pw_ref, ps_ref,
             out_ref, dp_ref, dpw_ref, dps_ref, acc_ref):
        i = pl.program_id(0)

        @pl.when(i == 0)
        def _():
            acc_ref[...] = jnp.zeros_like(acc_ref)
            dpw_ref[...] = jnp.zeros_like(dpw_ref)
            dps_ref[...] = jnp.zeros_like(dps_ref)

        xf = x_ref[...]
        r = lax.rsqrt(jnp.mean(xf * xf, axis=-1, keepdims=True) + EPS)
        h = ((xf * r) * g_ref[...]).astype(BF16)
        half = ATTN_WIDTH // 2
        for gi, ref in enumerate((dq_ref, dk_ref, dv_ref, dag_ref)):
            pool_halves = _pool_bwd_group(gi, i, n_tiles, cur_ref, prev_ref, pgn_ref, dy_ref, dyn_ref, pw_ref, ps_ref,
                                          dp_ref, dpw_ref, dps_ref)
            for c0 in (0, half):
                col = 2 * POOL_WIDTH + gi * ATTN_WIDTH + c0
                acc_ref[:, col:col + half] += _tn(h, ref[:, c0:c0 + half])
                next(pool_halves, None)
        for c0 in (0, POOL_WIDTH):
            acc_ref[:, c0:c0 + POOL_WIDTH] += _tn(h, dp_ref[:, c0:c0 + POOL_WIDTH])

        @pl.when(i == n_tiles - 1)
        def _():
            for d in range(N_DEV):
                out_ref[d] = acc_ref[:, d * IN_SHARD:(d + 1) * IN_SHARD].astype(BF16)

    tile = lambda width: pl.BlockSpec((TOKEN_TILE, width), lambda i: (i, 0))
    next_halo = lambda col: pl.BlockSpec(
        (HALO, POOL_WIDTH), lambda i: (jnp.minimum((i + 1) * halo_per_tile, last_halo), col))
    return pl.pallas_call(
        body,
        name="inproj_bwd_dw",
        grid=(n_tiles,),
        out_shape=(
            jax.ShapeDtypeStruct((N_DEV, D_MODEL, IN_SHARD), BF16),
            jax.ShapeDtypeStruct((t, 2 * POOL_WIDTH), BF16),
            jax.ShapeDtypeStruct((N_GROUPS, GROUP_DIM, GROUP_DIM), F32),
            jax.ShapeDtypeStruct((1, POOL_WIDTH), F32),
        ),
        in_specs=[
            tile(D_MODEL),
            tile(2 * POOL_WIDTH),
            pl.BlockSpec((HALO, POOL_WIDTH), lambda i: (jnp.maximum(i * halo_per_tile - 1, 0), 0)),
            next_halo(1),
            tile(POOL_WIDTH),
            next_halo(0),
            tile(ATTN_WIDTH), tile(ATTN_WIDTH), tile(ATTN_WIDTH), tile(ATTN_WIDTH),
            pl.BlockSpec((1, D_MODEL), lambda i: (0, 0)),
            pl.BlockSpec((N_GROUPS, GROUP_DIM, GROUP_DIM), lambda i: (0, 0, 0)),
            pl.BlockSpec((1, POOL_WIDTH), lambda i: (0, 0)),
        ],
        out_specs=(
            pl.BlockSpec((N_DEV, D_MODEL, IN_SHARD), lambda i: (0, 0, 0)),
            tile(2 * POOL_WIDTH),
            pl.BlockSpec((N_GROUPS, GROUP_DIM, GROUP_DIM), lambda i: (0, 0, 0)),
            pl.BlockSpec((1, POOL_WIDTH), lambda i: (0, 0)),
        ),
        scratch_shapes=[pltpu.VMEM((D_MODEL, IN_WIDTH), F32)],
        compiler_params=_params(("arbitrary",)),
    )(x2d, pvg, pvg, pvg, dy_pool, dy_pool, *dattn, g1, pool_w, pool_scale)


_HBM = pl.BlockSpec(memory_space=pltpu.HBM)
_SEM = pl.BlockSpec(memory_space=pltpu.SEMAPHORE)
_DATAFLOW = pltpu.SideEffectType.DATAFLOW_SIDE_EFFECTING


_N_EXCHANGED = 3


def _exchange_all(refs, send_sems, recv_sems):
    win_hbm, win_land, pw_hbm, pw_land, vec_hbm, vec_land = refs
    return (_exchange_copies(win_hbm, win_land, send_sems, recv_sems)
            + _exchange_copies(pw_hbm, pw_land, send_sems, recv_sems, first_sem=N_DEV - 1)
            + _exchange_copies(vec_hbm, vec_land, send_sems, recv_sems, first_sem=2 * (N_DEV - 1), same_for_all=True))


def _exchange_start(win_blocks, pw_blocks, vec):
    arrays = []
    for a, land_shape in ((win_blocks, (N_DEV - 1,) + win_blocks.shape[1:]),
                          (pw_blocks, (N_DEV - 1,) + pw_blocks.shape[1:]),
                          (vec, (N_DEV - 1,) + vec.shape)):
        arrays += [pltpu.with_memory_space_constraint(a, pltpu.HBM),
                   pltpu.with_memory_space_constraint(lax.empty(land_shape, a.dtype), pltpu.HBM)]

    def body(*refs):
        ins, (send_sems, recv_sems), token = refs[:2 * _N_EXCHANGED], refs[2 * _N_EXCHANGED:2 * _N_EXCHANGED + 2], refs[-1]
        for cp in _exchange_all(ins, send_sems, recv_sems):
            cp.start()
        token[...] = jnp.zeros_like(token)

    sems = pltpu.SemaphoreType.DMA((_N_EXCHANGED * (N_DEV - 1),))
    return pl.pallas_call(
        body,
        name="exchange_start",
        out_shape=(sems, sems, *[pltpu.HBM(a.shape, a.dtype) for a in arrays], jax.ShapeDtypeStruct((8, LANES), F32)),
        in_specs=tuple([_HBM] * len(arrays)),
        out_specs=(_SEM, _SEM, *[_HBM] * len(arrays), pl.BlockSpec(memory_space=pltpu.VMEM)),
        input_output_aliases={k: 2 + k for k in range(len(arrays))},
        compiler_params=pltpu.CompilerParams(has_side_effects=_DATAFLOW),
    )(*arrays)


def _exchange_wait(send_sems, recv_sems, arrays, after):
    def body(*refs):
        ins = refs[:2 * _N_EXCHANGED]
        send_sems, recv_sems = refs[2 * _N_EXCHANGED:2 * _N_EXCHANGED + 2]
        for cp in _exchange_all(ins, send_sems, recv_sems):
            cp.wait_send()
            cp.wait_recv()

    return pl.pallas_call(
        body,
        name="exchange_wait",
        out_shape=tuple(pltpu.HBM(a.shape, a.dtype) for a in arrays),
        in_specs=(*[_HBM] * len(arrays), _SEM, _SEM, pl.BlockSpec(memory_space=pl.ANY)),
        out_specs=tuple([_HBM] * len(arrays)),
        input_output_aliases={k: k for k in range(len(arrays))},
        compiler_params=pltpu.CompilerParams(has_side_effects=_DATAFLOW),
    )(*arrays, send_sems, recv_sems, after)


def _adamw(w, g, m, v):
    m = ADAM_B1 * m + (1.0 - ADAM_B1) * g
    v = ADAM_B2 * v + (1.0 - ADAM_B2) * (g * g)
    m_hat = m / (1.0 - ADAM_B1 ** ADAM_STEP)
    v_hat = v / (1.0 - ADAM_B2 ** ADAM_STEP)
    delta = -ADAM_LR * (m_hat / (jnp.sqrt(v_hat) + ADAM_EPS) + ADAM_WD * w)
    return delta, m, v


def _small_allreduce(d_g1, vec, vec_land, pw_blocks, pw_land):
    def body(g1_ref, vec_ref, vland_ref, pwb_ref, pland_ref, vec_out, pw_out, vparts_ref, pparts_ref, rows_ref,
             slice_ref, send_sems, recv_sems):
        me = _dev_index(_mesh_pos())

        def from_devices(parts_ref, own, land_ref):
            parts_ref[0] = own
            parts_ref[1:] = land_ref[...]
            total = parts_ref[me]
            for s in range(1, N_DEV):
                total = total + parts_ref[me ^ s]
            return total

        slice_ref[...] = from_devices(pparts_ref, pwb_ref[me], pland_ref)

        def send(r, src, dst, k):
            return pltpu.make_async_remote_copy(
                src_ref=src, dst_ref=dst, send_sem=send_sems.at[2 * (r - 1) + k],
                recv_sem=recv_sems.at[2 * (r - 1) + k], device_id=_peer(r), device_id_type=MESH_ID)

        started = [cp for r in range(1, N_DEV)
                   for cp in (send(r, g1_ref, rows_ref.at[r], 0), send(r, slice_ref, pw_out.at[me], 1))]
        for cp in started:
            cp.start()
        rows_ref[0] = g1_ref[...]
        pw_out[me] = slice_ref[...]
        vec_out[...] = from_devices(vparts_ref, vec_ref[...], vland_ref)
        for cp in started:
            cp.wait_recv()
        for cp in started:
            cp.wait_send()
        g1 = rows_ref[me]
        for s in range(1, N_DEV):
            g1 = g1 + rows_ref[me ^ s]
        vec_out[_ROW_G1:_ROW_G1 + 1, :] = g1

    vm = pl.BlockSpec(memory_space=pltpu.VMEM)
    return pl.pallas_call(
        body,
        name="small_allreduce",
        out_shape=(jax.ShapeDtypeStruct((_VEC_ROWS, D_MODEL), F32),
                   jax.ShapeDtypeStruct((N_DEV, GROUP_DIM // 2, GROUP_DIM), F32)),
        in_specs=[vm] * 5,
        out_specs=(vm, vm),
        scratch_shapes=[
            pltpu.VMEM((N_DEV, _VEC_ROWS, D_MODEL), F32),
            pltpu.VMEM((N_DEV, GROUP_DIM // 2, GROUP_DIM), F32),
            pltpu.VMEM((N_DEV, 1, D_MODEL), F32),
            pltpu.VMEM((GROUP_DIM // 2, GROUP_DIM), F32),
            pltpu.SemaphoreType.DMA((2 * (N_DEV - 1),)),
            pltpu.SemaphoreType.DMA((2 * (N_DEV - 1),)),
        ],
        compiler_params=_params(),
    )(d_g1, vec, vec_land, pw_blocks, pw_land)


def _adamw_all(dwin_g, land_in, dwout_g, land_out, vec_sum, pw_sum, weights, big):
    small_shapes = [(1, D_MODEL), (1, D_MODEL), (1, POOL_WIDTH), (N_HEADS, N_REL), (N_GROUPS, GROUP_DIM, GROUP_DIM)]

    def body(*refs):
        refs = list(refs)
        take = lambda n: [refs.pop(0) for _ in range(n)]
        dwin_hbm, lin_ref, dwout_hbm, lout_ref, vec_ref, pw_ref = take(6)
        small_wmv = [take(3) for _ in range(5)]
        big_wmv = [take(3) for _ in range(2)]
        big_out = [take(4) for _ in range(2)]
        small_out = [take(4) for _ in range(5)]
        own_in, own_out, local_sems = refs

        me = _dev_index(_mesh_pos())
        mine = [pltpu.make_async_copy(dwin_hbm.at[me], own_in, local_sems.at[0]),
                pltpu.make_async_copy(dwout_hbm.at[me], own_out, local_sems.at[1])]
        for cp in mine:
            cp.start()

        def update(g, wmv, outs):
            delta, m_new, v_new = _adamw(wmv[0][...], g, wmv[1][...], wmv[2][...])
            for ref, val in zip(outs, (g, delta, m_new, v_new)):
                ref[...] = val

        update(vec_ref[_ROW_G1:_ROW_G1 + 1, :], small_wmv[0], small_out[0])
        update(vec_ref[_ROW_G2:_ROW_G2 + 1, :], small_wmv[1], small_out[1])
        update(vec_ref[_ROW_PS:_ROW_PS + 1, :POOL_WIDTH], small_wmv[2], small_out[2])
        update(vec_ref[_ROW_RB:_ROW_RB + N_HEADS, :N_REL], small_wmv[3], small_out[3])
        update(pw_ref[...], small_wmv[4], small_out[4])
        for cp in mine:
            cp.wait()
        g_in = own_in[...].astype(F32)
        g_out = own_out[...].astype(F32)
        for r in range(N_DEV - 1):
            g_in = g_in + lin_ref[r].astype(F32)
            g_out = g_out + lout_ref[r].astype(F32)
        update(g_in, big_wmv[0], big_out[0])
        update(g_out, big_wmv[1], big_out[1])

    vm = pl.BlockSpec(memory_space=pltpu.VMEM)
    hbm = pl.BlockSpec(memory_space=pl.ANY)
    f32 = lambda shape: jax.ShapeDtypeStruct(shape, F32)
    out_shapes = [f32((D_MODEL, IN_SHARD))] * 4 + [f32((OUT_SHARD, D_MODEL))] * 4
    for shape in small_shapes:
        out_shapes += [f32(shape)] * 4
    args = [dwin_g, land_in, dwout_g, land_out, vec_sum, pw_sum]
    for wmv in weights:
        args += list(wmv)
    for wmv in big:
        args += list(wmv)
    return pl.pallas_call(
        body,
        name="adamw_all",
        out_shape=tuple(out_shapes),
        in_specs=[hbm, vm, hbm, vm] + [vm] * (len(args) - 4),
        out_specs=tuple([vm] * len(out_shapes)),
        scratch_shapes=[
            pltpu.VMEM((D_MODEL, IN_SHARD), BF16),
            pltpu.VMEM((OUT_SHARD, D_MODEL), BF16),
            pltpu.SemaphoreType.DMA((2,)),
        ],
        compiler_params=_params(),
    )(*args)


def kernel(x, norm_gain, w_in, pool_w, pool_scale, rel_bias, w_out, final_norm_gain, loss_target, m_norm_gain, m_w_in, m_pool_w, m_pool_scale, m_rel_bias, m_w_out, m_final_norm_gain, v_norm_gain, v_w_in, v_pool_w, v_pool_scale, v_rel_bias, v_w_out, v_final_norm_gain):
    t = x.shape[1]
    assert x.shape[0] == 1 and t % TOKEN_TILE == 0 and t // Q_BLOCK >= 4
    x2d = x[0]
    tgt2d = loss_target[0]
    g2 = final_norm_gain.reshape(1, D_MODEL)

    rb = rel_bias[0]
    rel_line = jnp.concatenate([
        jnp.broadcast_to(rb[:, :1], (N_HEADS, _REL_FIRST)), rb,
        jnp.broadcast_to(rb[:, N_REL - 1:], (N_HEADS, TOEPLITZ - _REL_FIRST - N_REL)),
    ], axis=1).reshape(N_HEADS, 1, TOEPLITZ)
    wg_in, bias_tile = _gather_weights(w_in[0], rel_line)

    pvg, qkv, ag, y_pool = _norm_inproj(x2d, norm_gain, wg_in, pool_w[0], pool_scale)
    o, wg_out = _attn_fwd(qkv, bias_tile, w_out[0])
    dx2, dy_pool, do, dag, dwout_g, d_g2, loss_sum = _outproj_loss(x2d, tgt2d, y_pool, o, ag, wg_out, g2)
    dq, dk, dv, ds_band, ds_total, land_out = _attn_bwd(qkv, do, bias_tile, dwout_g)
    dwin_g, d_pool, d_pw, d_ps = _inproj_bwd_dw(x2d, pvg, dy_pool, (dq, dk, dv, dag), norm_gain, pool_w[0], pool_scale)
    vec = _pack_small(ds_band, ds_total, d_g2, d_ps, loss_sum)
    dproj = (d_pool, dq, dk, dv, dag)
    pw_blocks = d_pw.reshape(N_DEV, GROUP_DIM // 2, GROUP_DIM)
    send_sems, recv_sems, *exchanged, token = _exchange_start(dwin_g, pw_blocks, vec)
    grad_x, d_g1 = _inproj_bwd_dx(x2d, dx2, dproj, norm_gain + token[:1, :1], wg_in)
    dwin_g, land_in, pw_blocks, pw_land, vec, vec_land = _exchange_wait(send_sems, recv_sems, exchanged, d_g1)

    row = lambda a: a.reshape(1, D_MODEL)
    weights = [
        (norm_gain, m_norm_gain, v_norm_gain),
        (row(final_norm_gain), row(m_final_norm_gain), row(v_final_norm_gain)),
        (pool_scale, m_pool_scale, v_pool_scale),
        (rel_bias[0], m_rel_bias[0], v_rel_bias[0]),
        (pool_w[0], m_pool_w[0], v_pool_w[0]),
    ]
    big = [(w_in[0], m_w_in[0], v_w_in[0]), (w_out[0], m_w_out[0], v_w_out[0])]
    vec_sum, pw_sum = _small_allreduce(d_g1, vec, vec_land, pw_blocks, pw_land)
    res = _adamw_all(dwin_g, land_in, dwout_g, land_out, vec_sum, pw_sum.reshape(N_GROUPS, GROUP_DIM, GROUP_DIM),
                     weights, big)
    loss = 0.5 * vec_sum[_ROW_LOSS, 0]

    def leaves(k):
        g1_, g2_, ps_, rb_, pw_ = (res[8 + 4 * leaf + k] for leaf in range(5))
        return [g1_, res[k][None], pw_[None], ps_, rb_[None], res[4 + k][None], g2_.reshape(D_MODEL)]

    return (loss, grad_x[None], *leaves(0), *leaves(1), *leaves(2), *leaves(3))
```

```python
import math

import jax
import jax.numpy as jnp
from jax import lax
from jax.experimental import pallas as pl
from jax.experimental.pallas import tpu as pltpu

F32 = jnp.float32
BF16 = jnp.bfloat16
MESH_ID = pl.DeviceIdType.MESH

D_MODEL = 1024
POOL_WIDTH = 512
ATTN_WIDTH = 512
POOL_WINDOWS = (2, 4, 8, 16)
N_GROUPS = 4
GROUP_DIM = 128
HEAD_DIM = 64
N_HEADS = 8
CHUNK = 64
LEFT_CHUNKS = 8
MAX_REL = 64
N_REL = 2 * MAX_REL + 1
IN_WIDTH = 2 * POOL_WIDTH + 4 * ATTN_WIDTH
EPS = 1e-6
MASK_VALUE = -1e30
ATTN_SCALE = 1.0 / math.sqrt(HEAD_DIM)
ADAM_LR = 0.001
ADAM_B1 = 0.9
ADAM_B2 = 0.999
ADAM_EPS = 1e-08
ADAM_WD = 0.01
ADAM_STEP = 10

N_DEV = 8
IN_SHARD = IN_WIDTH // N_DEV
OUT_SHARD = D_MODEL // N_DEV

LANES = 128
TOKEN_TILE = 512
HALO = 16
Q_BLOCK = 256
KV_BLOCKS = 3
KV_WINDOW = KV_BLOCKS * Q_BLOCK
PAIR = 2 * HEAD_DIM
N_PAIRS = N_HEADS // 2
BLOCK_LOOP_UNROLL = 7
TOEPLITZ = 1024
VMEM_LIMIT = 56 * 1024 * 1024


def _params(sem=None, vmem=VMEM_LIMIT):
    return pltpu.CompilerParams(dimension_semantics=sem, vmem_limit_bytes=vmem)


def _sigmoid(x):
    return 1.0 / (1.0 + jnp.exp(-x))


def _nt(a, b):
    return lax.dot_general(a, b, (((1,), (1,)), ((), ())), preferred_element_type=F32)


def _tn(a, b):
    return lax.dot_general(a, b, (((0,), (0,)), ((), ())), preferred_element_type=F32)


def _nn(a, b):
    return jnp.dot(a, b, preferred_element_type=F32)


def _mesh_pos():
    return lax.axis_index("x"), lax.axis_index("y"), lax.axis_index("c")


def _dev_index(p):
    return 4 * p[0] + 2 * p[1] + p[2]


def _peer(r):
    x, y, c = _mesh_pos()
    return (x ^ ((r >> 2) & 1), y ^ ((r >> 1) & 1), c ^ (r & 1))


def _exchange_copies(src_hbm, land_hbm, send_sems, recv_sems, first_sem=0, same_for_all=False):
    return [
        pltpu.make_async_remote_copy(
            src_ref=src_hbm if same_for_all else src_hbm.at[_dev_index(_peer(r))], dst_ref=land_hbm.at[r - 1],
            send_sem=send_sems.at[first_sem + r - 1], recv_sem=recv_sems.at[first_sem + r - 1],
            device_id=_peer(r), device_id_type=MESH_ID)
        for r in range(1, N_DEV)
    ]


def _gather_weights(w_in_shard, rel_line):
    def body(win_ref, line_ref, gin_ref, bias_ref, sin_ref, send_sems, recv_sems):
        x, y, c = _mesh_pos()
        me, sibling = (x, y, c), (x, y, 1 - c)
        chips = [(1 - x, y), (x, 1 - y), (1 - x, 1 - y)]

        sin_ref[...] = win_ref[...].astype(BF16)
        gin_ref[_dev_index(me)] = sin_ref[...]

        def copy(k, block, to, from_shard=False):
            return pltpu.make_async_remote_copy(
                src_ref=sin_ref if from_shard else gin_ref.at[_dev_index(block)],
                dst_ref=gin_ref.at[_dev_index(block)],
                send_sem=send_sems.at[k],
                recv_sem=recv_sems.at[k],
                device_id=to,
                device_id_type=MESH_ID,
            )

        first = [copy(0, me, sibling, True)]
        first += [copy(1 + j, me, (*chip, c), True) for j, chip in enumerate(chips)]
        for cp in first:
            cp.start()
        passed = [copy(4 + j, (*chip, c), sibling) for j, chip in enumerate(chips)]

        def bias_heads(lo, hi):
            for h in range(lo, hi):
                bias_ref[h] = _toeplitz_bias(line_ref[h])

        bias_heads(0, N_HEADS - 3)
        for j, chip in enumerate(chips):
            copy(1 + j, (*chip, c), me).wait_recv()
            passed[j].start()
            bias_heads(N_HEADS - 3 + j, N_HEADS - 2 + j)
        copy(0, sibling, me).wait_recv()
        for j, chip in enumerate(chips):
            copy(4 + j, (*chip, 1 - c), me).wait_recv()
        for cp in first + passed:
            cp.wait_send()

    vm = pl.BlockSpec(memory_space=pltpu.VMEM)
    return pl.pallas_call(
        body,
        name="gather_weights",
        out_shape=(
            jax.ShapeDtypeStruct((N_DEV, D_MODEL, IN_SHARD), BF16),
            jax.ShapeDtypeStruct((N_HEADS, Q_BLOCK, KV_WINDOW), F32),
        ),
        in_specs=[vm, vm],
        out_specs=(vm, vm),
        scratch_shapes=[
            pltpu.VMEM((D_MODEL, IN_SHARD), BF16),
            pltpu.SemaphoreType.DMA((7,)),
            pltpu.SemaphoreType.DMA((7,)),
        ],
        compiler_params=_params(),
    )(w_in_shard, rel_line)


def _load_w_in(wg_hbm, wfull_ref, sem):
    copies = [
        pltpu.make_async_copy(wg_hbm.at[d], wfull_ref.at[:, d * IN_SHARD:(d + 1) * IN_SHARD], sem.at[d])
        for d in range(N_DEV)
    ]
    for cp in copies:
        cp.start()
    for cp in copies:
        cp.wait()


def _norm_inproj(x2d, g1, wg, pool_w, pool_scale):
    t = x2d.shape[0]

    def body(x_ref, g_ref, wg_hbm, pw_ref, ps_ref, pvg_ref, qkv_ref, ag_ref, yp_ref, wfull_ref, halo_ref, sem):
        i = pl.program_id(0)

        @pl.when(i == 0)
        def _():
            _load_w_in(wg_hbm, wfull_ref, sem)
            halo_ref[...] = jnp.zeros_like(halo_ref)

        xf = x_ref[...]
        r = lax.rsqrt(jnp.mean(xf * xf, axis=-1, keepdims=True) + EPS)
        h = ((xf * r) * g_ref[...]).astype(BF16)
        chunk = lambda ci: _nn(h, wfull_ref[:, ci * POOL_WIDTH:(ci + 1) * POOL_WIDTH])
        pv, pg = chunk(0), chunk(1)
        pvg_ref[:, :POOL_WIDTH] = pv
        pvg_ref[:, POOL_WIDTH:] = pg
        halo = halo_ref[...]
        halo_ref[...] = pv[TOKEN_TILE - HALO:]
        for gi in range(N_GROUPS):
            if gi < 3:
                qkv_ref[:, gi * ATTN_WIDTH:(gi + 1) * ATTN_WIDTH] = chunk(2 + gi).astype(BF16)
            else:
                ag_ref[...] = chunk(5)
            sl = slice(gi * GROUP_DIM, (gi + 1) * GROUP_DIM)
            d = _pool_diffs(pv[:, sl], halo[:, sl], i * TOKEN_TILE, POOL_WINDOWS[gi])
            z = _nn(d.astype(BF16), pw_ref[gi].astype(BF16))
            g = pg[:, sl]
            yp_ref[:, sl] = ((z * ps_ref[:, sl]) * (g * _sigmoid(g))).astype(BF16)

    tile = lambda width: pl.BlockSpec((TOKEN_TILE, width), lambda i: (i, 0))
    return pl.pallas_call(
        body,
        name="norm_inproj",
        grid=(t // TOKEN_TILE,),
        out_shape=(
            jax.ShapeDtypeStruct((t, 2 * POOL_WIDTH), F32),
            jax.ShapeDtypeStruct((t, 3 * ATTN_WIDTH), BF16),
            jax.ShapeDtypeStruct((t, ATTN_WIDTH), F32),
            jax.ShapeDtypeStruct((t, POOL_WIDTH), BF16),
        ),
        in_specs=[
            tile(D_MODEL),
            pl.BlockSpec((1, D_MODEL), lambda i: (0, 0)),
            pl.BlockSpec(memory_space=pl.ANY),
            pl.BlockSpec((N_GROUPS, GROUP_DIM, GROUP_DIM), lambda i: (0, 0, 0)),
            pl.BlockSpec((1, POOL_WIDTH), lambda i: (0, 0)),
        ],
        out_specs=(tile(2 * POOL_WIDTH), tile(3 * ATTN_WIDTH), tile(ATTN_WIDTH), tile(POOL_WIDTH)),
        scratch_shapes=[
            pltpu.VMEM((D_MODEL, IN_WIDTH), BF16),
            pltpu.VMEM((HALO, POOL_WIDTH), F32),
            pltpu.SemaphoreType.DMA((N_DEV,)),
        ],
        compiler_params=_params(("arbitrary",)),
    )(x2d, g1, wg, pool_w, pool_scale)


def _inv_count(first_row, rows, window):
    tpos = first_row + lax.broadcasted_iota(jnp.int32, (rows, 1), 0)
    return 1.0 / jnp.minimum(tpos + 1, window).astype(F32)


def _causal_window_sum(ext, window):
    s, k = ext, 1
    while k < window:
        s = s + pltpu.roll(s, k, 0)
        k *= 2
    return s


def _pool_diffs(pv, halo, first_row, window):
    s = _causal_window_sum(jnp.concatenate([halo, pv], axis=0), window)[HALO:]
    return s * _inv_count(first_row, pv.shape[0], window) - pv


def _pool_bwd_group(gi, i, n_tiles, cur_ref, prev_ref, pgn_ref, dy_ref, dyn_ref, pw_ref, ps_ref,
                    dp_ref, dpw_ref, dps_ref):
    w = POOL_WINDOWS[gi]
    sl = slice(gi * GROUP_DIM, (gi + 1) * GROUP_DIM)
    gate_sl = slice(POOL_WIDTH + gi * GROUP_DIM, POOL_WIDTH + (gi + 1) * GROUP_DIM)
    rows = TOKEN_TILE + HALO
    pw = pw_ref[gi].astype(BF16)
    ps = ps_ref[:, sl]
    d = _pool_diffs(cur_ref[:, sl], jnp.where(i > 0, prev_ref[:, sl], 0.0), i * TOKEN_TILE, w).astype(BF16)
    z = _nn(d, pw)
    g_ext = jnp.concatenate([cur_ref[:, gate_sl], pgn_ref[:, sl]], axis=0)
    dy_ext = jnp.concatenate([dy_ref[:, sl], dyn_ref[:, sl]], axis=0).astype(F32)
    sig = _sigmoid(g_ext)
    gate = g_ext * sig
    dz_ext = ((dy_ext * gate) * ps).astype(BF16)
    dd_ext = _nt(dz_ext, pw)
    yield
    e = dd_ext * _inv_count(i * TOKEN_TILE, rows, w)
    row = lax.broadcasted_iota(jnp.int32, (rows, 1), 0)
    e = jnp.where(jnp.logical_or(row < TOKEN_TILE, i < n_tiles - 1), e, 0.0)
    s, k = e, 1
    while k < w:
        s = s + pltpu.roll(s, rows - k, 0)
        k *= 2
    dp_ref[:, sl] = (s[:TOKEN_TILE] - dd_ext[:TOKEN_TILE]).astype(BF16)
    dy = dy_ext[:TOKEN_TILE]
    g = g_ext[:TOKEN_TILE]
    sg = sig[:TOKEN_TILE]
    dgate = sg * (1.0 + g * (1.0 - sg))
    dp_ref[:, gate_sl] = ((dy * (z * ps)) * dgate).astype(BF16)
    dps_ref[:, sl] += jnp.sum((dy * gate[:TOKEN_TILE]) * z, axis=0, keepdims=True)
    dpw_ref[gi] += _tn(d, dz_ext[:TOKEN_TILE])


_REL_FIRST = KV_WINDOW - 1 - MAX_REL


def _skew_rows(a, right):
    rows, lanes = a.shape
    row = lax.broadcasted_iota(jnp.int32, a.shape, 0)
    for b in range(rows.bit_length() - 1):
        shift = (1 << b) if right else lanes - (1 << b)
        a = jnp.where((row >> b) & 1 == 1, pltpu.roll(a, shift, 1), a)
    return a


def _toeplitz_bias(line):
    a = jnp.broadcast_to(line, (Q_BLOCK, TOEPLITZ))
    a = _skew_rows(a, True)
    a = pltpu.roll(a, TOEPLITZ - (Q_BLOCK - 1), 1)
    a = a[:, :KV_WINDOW]
    qc = lax.broadcasted_iota(jnp.int32, a.shape, 0) // CHUNK
    kc = lax.broadcasted_iota(jnp.int32, a.shape, 1) // CHUNK
    visible = jnp.logical_and(kc >= qc, kc <= qc + LEFT_CHUNKS)
    return jnp.where(visible, a, MASK_VALUE)


_BAND_WIDTH = 2 * LANES
_BAND_START = (384, 512, 512, 512)
_BAND_REL = tuple(a - (KV_BLOCKS - 1) * Q_BLOCK - qc * CHUNK for qc, a in enumerate(_BAND_START))


_ROW_G1, _ROW_G2, _ROW_PS, _ROW_LOSS, _ROW_RB = 0, 1, 2, 3, 8
_VEC_ROWS = 16


def _pack_small(band, total, d_g2, d_ps, loss_row):
    def body(band_ref, total_ref, g2_ref, ps_ref, loss_ref, vec_ref):
        vec_ref[...] = jnp.zeros_like(vec_ref)
        vec_ref[_ROW_G2:_ROW_G2 + 1, :] = g2_ref[...]
        vec_ref[_ROW_PS:_ROW_PS + 1, :POOL_WIDTH] = ps_ref[...]
        vec_ref[_ROW_LOSS:_ROW_LOSS + 1, :LANES] = loss_ref[0:1, :]
        for h in range(N_HEADS):
            a = jnp.zeros((CHUNK, 2 * _BAND_WIDTH), F32)
            for qc in range(Q_BLOCK // CHUNK):
                z = jnp.concatenate([band_ref[h, qc], jnp.zeros((CHUNK, _BAND_WIDTH), F32)], axis=1)
                left = -MAX_REL - _BAND_REL[qc]
                a = a + (pltpu.roll(z, 2 * _BAND_WIDTH - left, 1) if left else z)
            a = _skew_rows(a, False)
            near = jnp.sum(a, axis=0, keepdims=True)[:, :2 * LANES]
            r = lax.broadcasted_iota(jnp.int32, near.shape, 1)
            near = jnp.where(jnp.logical_and(r >= 1, r < 2 * MAX_REL), near, 0.0)
            everything = jnp.sum(jnp.sum(total_ref[h], axis=0, keepdims=True), axis=1, keepdims=True)
            far = everything - jnp.sum(near, axis=1, keepdims=True)
            vec_ref[_ROW_RB + h:_ROW_RB + h + 1, :2 * LANES] = jnp.where(r == 0, far, near)

    vm = pl.BlockSpec(memory_space=pltpu.VMEM)
    return pl.pallas_call(
        body,
        name="pack_small",
        out_shape=jax.ShapeDtypeStruct((_VEC_ROWS, D_MODEL), F32),
        in_specs=[vm] * 5,
        out_specs=vm,
        compiler_params=_params(),
    )(band, total, d_g2, d_ps, loss_row)


def _head_lanes(hh):
    lane = lax.broadcasted_iota(jnp.int32, (1, PAIR), 1)
    return (lane < HEAD_DIM) if hh == 0 else (lane >= HEAD_DIM)


def _score_windows(nwin):
    if nwin < KV_BLOCKS:
        return [(slice(0, Q_BLOCK), slice(0, nwin * Q_BLOCK), None)]
    pieces = []
    for qc in range(Q_BLOCK // CHUNK):
        rows = slice(qc * CHUNK, (qc + 1) * CHUNK)
        if qc < 2:
            pieces.append((rows, slice(0, KV_WINDOW - LANES), slice(KV_WINDOW - LANES, KV_WINDOW)))
        else:
            pieces.append((rows, slice(LANES, KV_WINDOW), slice(0, LANES)))
    return pieces


def _block_rows(first_block, n_blocks):
    if isinstance(first_block, int):
        return pl.ds(first_block * Q_BLOCK, n_blocks * Q_BLOCK)
    return pl.ds(pl.multiple_of(first_block * Q_BLOCK, Q_BLOCK), n_blocks * Q_BLOCK)


def _attn_fwd(qkv, bias_tile, w_out_shard):
    t = qkv.shape[0]
    nb = t // Q_BLOCK

    def body(q_ref, k_ref, v_ref, bias_ref, wout_ref, o_ref, gout_hbm,
             s_scr, p_scr, sout_ref, send_sems, recv_sems, local_sem):
        pair = pl.program_id(0)
        me = _dev_index(_mesh_pos())
        mine = pltpu.make_async_copy(sout_ref, gout_hbm.at[me], local_sem)

        def shard_copy(r, block):
            return pltpu.make_async_remote_copy(
                src_ref=sout_ref, dst_ref=gout_hbm.at[block], send_sem=send_sems.at[r - 1],
                recv_sem=recv_sems.at[r - 1], device_id=_peer(r), device_id_type=MESH_ID)

        @pl.when(pair == 0)
        def _():
            sout_ref[...] = wout_ref[...].astype(BF16)
            mine.start()
            for r in range(1, N_DEV):
                shard_copy(r, me).start()

        def scores(i, slot, nwin):
            q = q_ref[_block_rows(i, 1), :]
            kw = k_ref[_block_rows(i + 1 - nwin, nwin), :]
            for hh in range(2):
                q_h = jnp.where(_head_lanes(hh), q, jnp.zeros_like(q)) * ATTN_SCALE
                s_scr[slot, hh, :, :nwin * Q_BLOCK] = _nt(q_h, kw)

        def softmax(slot, nwin):
            off = (KV_BLOCKS - nwin) * Q_BLOCK
            for hh in range(2):
                for rows, cols, rest in _score_windows(nwin):
                    s = s_scr[slot, hh, rows, cols] + bias_ref[hh, rows, off + cols.start:off + cols.stop]
                    p_scr[slot, hh, rows, cols] = jnp.exp(s - jnp.max(s, axis=-1, keepdims=True)).astype(BF16)
                    if rest is not None:
                        p_scr[slot, hh, rows, rest] = jnp.zeros((CHUNK, LANES), BF16)

        def output(i, slot, nwin):
            vw = v_ref[_block_rows(i + 1 - nwin, nwin), :]
            outs = [_nn(p_scr[slot, hh, :, :nwin * Q_BLOCK], jnp.where(_head_lanes(hh), vw, jnp.ones_like(vw)))
                    for hh in range(2)]
            sums = pltpu.roll(jnp.where(_head_lanes(0), outs[1], outs[0]), HEAD_DIM, axis=1)
            o_ref[_block_rows(i, 1), :] = (jnp.where(_head_lanes(0), outs[0], outs[1]) * (1.0 / sums)).astype(BF16)

        scores(0, 0, 1)
        scores(1, 1, 2)
        softmax(0, 1)
        scores(2, 0, 3)
        softmax(1, 2)
        output(0, 0, 1)
        scores(3, 1, 3)
        softmax(0, 3)
        output(1, 1, 2)

        def two_steps(k, carry):
            i = 3 + 2 * k
            scores(i + 1, 0, KV_BLOCKS)
            softmax(1, KV_BLOCKS)
            output(i - 1, 0, KV_BLOCKS)
            scores(i + 2, 1, KV_BLOCKS)
            softmax(0, KV_BLOCKS)
            output(i, 1, KV_BLOCKS)
            return carry

        lax.fori_loop(0, (nb - 4) // 2, two_steps, 0, unroll=BLOCK_LOOP_UNROLL)
        last = (nb - 1) % 2
        softmax(last, KV_BLOCKS)
        output(nb - 2, 1 - last, KV_BLOCKS)
        output(nb - 1, last, KV_BLOCKS)

        @pl.when(pair == N_PAIRS - 1)
        def _():
            for r in range(1, N_DEV):
                shard_copy(r, _dev_index(_peer(r))).wait_recv()
            for r in range(1, N_DEV):
                shard_copy(r, me).wait_send()
            mine.wait()

    col = lambda c0: pl.BlockSpec((t, PAIR), lambda j: (0, c0 + j))
    return pl.pallas_call(
        body,
        name="attn_fwd",
        grid=(N_PAIRS,),
        out_shape=(
            jax.ShapeDtypeStruct((t, ATTN_WIDTH), BF16),
            jax.ShapeDtypeStruct((N_DEV, OUT_SHARD, D_MODEL), BF16),
        ),
        in_specs=[col(0), col(N_PAIRS), col(2 * N_PAIRS),
                  pl.BlockSpec((2, Q_BLOCK, KV_WINDOW), lambda j: (j, 0, 0)),
                  pl.BlockSpec((OUT_SHARD, D_MODEL), lambda j: (0, 0))],
        out_specs=(col(0), pl.BlockSpec(memory_space=pl.ANY)),
        scratch_shapes=[
            pltpu.VMEM((2, 2, Q_BLOCK, KV_WINDOW), F32),
            pltpu.VMEM((2, 2, Q_BLOCK, KV_WINDOW), BF16),
            pltpu.VMEM((OUT_SHARD, D_MODEL), BF16),
            pltpu.SemaphoreType.DMA((N_DEV - 1,)),
            pltpu.SemaphoreType.DMA((N_DEV - 1,)),
            pltpu.SemaphoreType.DMA,
        ],
        compiler_params=_params(("arbitrary",)),
    )(qkv, qkv, qkv, bias_tile, w_out_shard)


def _attn_bwd(qkv, do, bias_tile, dwout_g):
    t = qkv.shape[0]
    nb = t // Q_BLOCK

    def body(q_ref, k_ref, v_ref, do_ref, bias_ref, dwout_hbm,
             dq_ref, dk_ref, dv_ref, band_ref, total_ref, land_hbm,
             s_scr, dp_scr, p_scr, dsb_scr, dq_scr, dk_acc, dv_acc, send_sems, recv_sems):
        exchange = _exchange_copies(dwout_hbm, land_hbm, send_sems, recv_sems)

        @pl.when(pl.program_id(0) == 0)
        def _():
            for cp in exchange:
                cp.start()

        band_ref[...] = jnp.zeros_like(band_ref)
        total_ref[...] = jnp.zeros_like(total_ref)

        def nwin_of(i):
            return min(i + 1, KV_BLOCKS) if isinstance(i, int) else KV_BLOCKS

        def operands(i, hh):
            lanes = _head_lanes(hh)
            q = q_ref[_block_rows(i, 1), :]
            do = do_ref[_block_rows(i, 1), :]
            return (jnp.where(lanes, q, jnp.zeros_like(q)) * ATTN_SCALE, jnp.where(lanes, do, jnp.zeros_like(do)))

        def products(i, hh):
            nwin = nwin_of(i)
            win = _block_rows(i + 1 - nwin, nwin)
            q_h, do_h = operands(i, hh)
            s_scr[hh, :, :nwin * Q_BLOCK] = _nt(q_h, k_ref[win, :])
            dp_scr[hh, :, :nwin * Q_BLOCK] = _nt(do_h, v_ref[win, :])

        def grads(i, hh):
            nwin = nwin_of(i)
            off = (KV_BLOCKS - nwin) * Q_BLOCK
            for rows, cols, rest in _score_windows(nwin):
                bias_cols = slice(off + cols.start, off + cols.stop)
                s = s_scr[hh, rows, cols] + bias_ref[hh, rows, bias_cols]
                e = jnp.exp(s - jnp.max(s, axis=-1, keepdims=True))
                p = e * (1.0 / jnp.sum(e, axis=-1, keepdims=True))
                dp = dp_scr[hh, rows, cols]
                ds = p * (dp - jnp.sum(p * dp, axis=-1, keepdims=True))
                total_ref[hh, rows, :] += sum(ds[:, c0:c0 + LANES] for c0 in range(0, ds.shape[1], LANES))
                for qc in range(rows.start // CHUNK, rows.stop // CHUNK):
                    lo = max(_BAND_START[qc], bias_cols.start)
                    hi = min(_BAND_START[qc] + _BAND_WIDTH, bias_cols.stop)
                    if lo < hi:
                        band_ref[hh, qc, :, lo - _BAND_START[qc]:hi - _BAND_START[qc]] += ds[
                            qc * CHUNK - rows.start:(qc + 1) * CHUNK - rows.start,
                            lo - bias_cols.start:hi - bias_cols.start]
                p_scr[hh, rows, cols] = p.astype(BF16)
                dsb_scr[hh, rows, cols] = ds.astype(BF16)
                if rest is not None:
                    p_scr[hh, rows, rest] = jnp.zeros((CHUNK, LANES), BF16)
                    dsb_scr[hh, rows, rest] = jnp.zeros((CHUNK, LANES), BF16)

        def ring(block):
            return block % KV_BLOCKS if isinstance(block, int) else lax.rem(block, KV_BLOCKS)

        def accumulate(i, hh):
            nwin = nwin_of(i)
            w = nwin * Q_BLOCK
            win = _block_rows(i + 1 - nwin, nwin)
            q_h, do_h = operands(i, hh)
            ds_b = dsb_scr[hh, :, :w]
            dq_h = _nn(ds_b, k_ref[win, :]) * ATTN_SCALE
            dkw = _tn(ds_b, q_h)
            dvw = _tn(p_scr[hh, :, :w], do_h)
            for b in range(nwin):
                slot = ring(i + 1 - nwin + b)
                part = slice(b * Q_BLOCK, (b + 1) * Q_BLOCK)
                if hh == 0 and b == nwin - 1:
                    dk_acc[slot] = dkw[part]
                    dv_acc[slot] = dvw[part]
                else:
                    dk_acc[slot] += dkw[part]
                    dv_acc[slot] += dvw[part]
            if hh == 0:
                dq_scr[...] = dq_h
            else:
                dq_ref[_block_rows(i, 1), :] = jnp.where(_head_lanes(0), dq_scr[...], dq_h).astype(BF16)
                if not (isinstance(i, int) and i < KV_BLOCKS - 1):
                    flush(i - (KV_BLOCKS - 1))

        def flush(block):
            dk_ref[_block_rows(block, 1), :] = dk_acc[ring(block)].astype(BF16)
            dv_ref[_block_rows(block, 1), :] = dv_acc[ring(block)].astype(BF16)

        def tile(n):
            return n // 2, n % 2

        def step(n):
            if n + 1 < 2 * nb:
                products(*tile(n + 1))
            grads(*tile(n))
            if n >= 1:
                accumulate(*tile(n - 1))

        products(0, 0)
        for n in range(2 * KV_BLOCKS):
            step(n)

        def two_steps(i, carry):
            products(i, 1)
            grads(i, 0)
            accumulate(i - 1, 1)
            products(i + 1, 0)
            grads(i, 1)
            accumulate(i, 0)
            return carry

        lax.fori_loop(KV_BLOCKS, nb - 1, two_steps, 0, unroll=BLOCK_LOOP_UNROLL)
        step(2 * nb - 2)
        step(2 * nb - 1)
        accumulate(nb - 1, 1)
        flush(nb - 2)
        flush(nb - 1)

        @pl.when(pl.program_id(0) == N_PAIRS - 1)
        def _():
            for cp in exchange:
                cp.wait_recv()
            for cp in exchange:
                cp.wait_send()

    col = lambda c0: pl.BlockSpec((t, PAIR), lambda j: (0, c0 + j))
    tile_spec = pl.BlockSpec((2, Q_BLOCK, KV_WINDOW), lambda j: (j, 0, 0))
    out = jax.ShapeDtypeStruct((t, ATTN_WIDTH), BF16)
    return pl.pallas_call(
        body,
        name="attn_bwd",
        grid=(N_PAIRS,),
        out_shape=(out, out, out,
                   jax.ShapeDtypeStruct((N_HEADS, Q_BLOCK // CHUNK, CHUNK, _BAND_WIDTH), F32),
                   jax.ShapeDtypeStruct((N_HEADS, Q_BLOCK, LANES), F32),
                   jax.ShapeDtypeStruct((N_DEV - 1, OUT_SHARD, D_MODEL), BF16)),
        in_specs=[col(0), col(N_PAIRS), col(2 * N_PAIRS), col(0), tile_spec, pl.BlockSpec(memory_space=pl.ANY)],
        out_specs=(col(0), col(0), col(0),
                   pl.BlockSpec((2, Q_BLOCK // CHUNK, CHUNK, _BAND_WIDTH), lambda j: (j, 0, 0, 0)),
                   pl.BlockSpec((2, Q_BLOCK, LANES), lambda j: (j, 0, 0)),
                   pl.BlockSpec(memory_space=pl.ANY)),
        scratch_shapes=[
            pltpu.VMEM((2, Q_BLOCK, KV_WINDOW), F32),
            pltpu.VMEM((2, Q_BLOCK, KV_WINDOW), F32),
            pltpu.VMEM((2, Q_BLOCK, KV_WINDOW), BF16),
            pltpu.VMEM((2, Q_BLOCK, KV_WINDOW), BF16),
            pltpu.VMEM((Q_BLOCK, PAIR), F32),
            pltpu.VMEM((KV_BLOCKS, Q_BLOCK, PAIR), F32),
            pltpu.VMEM((KV_BLOCKS, Q_BLOCK, PAIR), F32),
            pltpu.SemaphoreType.DMA((N_DEV - 1,)),
            pltpu.SemaphoreType.DMA((N_DEV - 1,)),
        ],
        compiler_params=_params(("arbitrary",)),
    )(qkv, qkv, qkv, do, bias_tile, dwout_g)


def _outproj_loss(x2d, tgt2d, y_pool, o, ag, wout_g, g2):
    t = x2d.shape[0]
    n_tiles = t // TOKEN_TILE

    def body(x_ref, tgt_ref, yp_ref, o_ref, ag_ref, w_ref, g_ref,
             dx2_ref, dyp_ref, do_ref, dag_ref, dw_ref, dg_ref, loss_ref, acc_ref):
        i = pl.program_id(0)

        @pl.when(i == 0)
        def _():
            acc_ref[...] = jnp.zeros_like(acc_ref)
            dg_ref[...] = jnp.zeros_like(dg_ref)
            loss_ref[...] = jnp.zeros_like(loss_ref)

        w = w_ref[...].reshape(D_MODEL, D_MODEL)
        ga = ag_ref[...]
        sig = _sigmoid(ga)
        gate = ga * sig
        of = o_ref[...].astype(F32)
        y = jnp.concatenate([yp_ref[...], (of * gate).astype(BF16)], axis=1)
        x2 = x_ref[...] + _nn(y, w)
        r = lax.rsqrt(jnp.mean(x2 * x2, axis=-1, keepdims=True) + EPS)
        xh = x2 * r
        g = g_ref[...]
        diff = xh * g - tgt_ref[...]
        tok = jnp.sum(diff * diff, axis=-1, keepdims=True) * (1.0 / D_MODEL)
        loss_ref[...] += jnp.sum(tok, axis=0, keepdims=True)
        dout = diff * (1.0 / D_MODEL)
        dg_ref[...] += jnp.sum(dout * xh, axis=0, keepdims=True)
        u = dout * g
        dx2 = r * (u - xh * jnp.mean(u * xh, axis=-1, keepdims=True))
        dx2_ref[...] = dx2
        dx2_b = dx2.astype(BF16)
        dy = _nt(dx2_b, w)
        dyp_ref[...] = dy[:, :POOL_WIDTH].astype(BF16)
        dya = dy[:, POOL_WIDTH:]
        do_ref[...] = (dya * gate).astype(BF16)
        dag_ref[...] = ((dya * of) * (sig * (1.0 + ga * (1.0 - sig)))).astype(BF16)
        acc_ref[...] += _tn(y, dx2_b)

        @pl.when(i == n_tiles - 1)
        def _():
            dw_ref[...] = acc_ref[...].reshape(N_DEV, OUT_SHARD, D_MODEL).astype(BF16)

    tile = lambda width: pl.BlockSpec((TOKEN_TILE, width), lambda i: (i, 0))
    return pl.pallas_call(
        body,
        name="outproj_loss",
        grid=(n_tiles,),
        out_shape=(
            jax.ShapeDtypeStruct((t, D_MODEL), F32),
            jax.ShapeDtypeStruct((t, POOL_WIDTH), BF16),
            jax.ShapeDtypeStruct((t, ATTN_WIDTH), BF16),
            jax.ShapeDtypeStruct((t, ATTN_WIDTH), BF16),
            jax.ShapeDtypeStruct((N_DEV, OUT_SHARD, D_MODEL), BF16),
            jax.ShapeDtypeStruct((1, D_MODEL), F32),
            jax.ShapeDtypeStruct((8, LANES), F32),
        ),
        in_specs=[
            tile(D_MODEL), tile(D_MODEL), tile(POOL_WIDTH), tile(ATTN_WIDTH), tile(ATTN_WIDTH),
            pl.BlockSpec((N_DEV, OUT_SHARD, D_MODEL), lambda i: (0, 0, 0)),
            pl.BlockSpec((1, D_MODEL), lambda i: (0, 0)),
        ],
        out_specs=(
            tile(D_MODEL), tile(POOL_WIDTH), tile(ATTN_WIDTH), tile(ATTN_WIDTH),
            pl.BlockSpec((N_DEV, OUT_SHARD, D_MODEL), lambda i: (0, 0, 0)),
            pl.BlockSpec((1, D_MODEL), lambda i: (0, 0)),
            pl.BlockSpec((8, LANES), lambda i: (0, 0)),
        ),
        scratch_shapes=[pltpu.VMEM((D_MODEL, D_MODEL), F32)],
        compiler_params=_params(("arbitrary",)),
    )(x2d, tgt2d, y_pool, o, ag, wout_g, g2)


def _dproj_specs():
    tile = lambda width: pl.BlockSpec((TOKEN_TILE, width), lambda i: (i, 0))
    return [tile(2 * POOL_WIDTH)] + [tile(ATTN_WIDTH)] * 4


RING_SLOTS = 3


def _inproj_bwd_dx(x2d, dx2, dproj, g1, wg):
    t = x2d.shape[0]

    n_tiles = t // TOKEN_TILE

    def body(x_hbm, dx2_hbm, dp_ref, dq_ref, dk_ref, dv_ref, dag_ref, g_ref, wg_hbm, gx_ref, dg_ref,
             wfull_ref, sem, xbuf, dbuf, ring_sems):
        i = pl.program_id(0)

        def fetch(tile_idx, slot):
            rows = pl.ds(pl.multiple_of(tile_idx * TOKEN_TILE, TOKEN_TILE), TOKEN_TILE)
            return (pltpu.make_async_copy(x_hbm.at[rows, :], xbuf.at[slot], ring_sems.at[0, slot]),
                    pltpu.make_async_copy(dx2_hbm.at[rows, :], dbuf.at[slot], ring_sems.at[1, slot]))

        @pl.when(i == 0)
        def _():
            for s in range(RING_SLOTS - 1):
                for cp in fetch(s, s):
                    cp.start()
            _load_w_in(wg_hbm, wfull_ref, sem)
            dg_ref[...] = jnp.zeros_like(dg_ref)

        ahead = i + RING_SLOTS - 1

        @pl.when(ahead < n_tiles)
        def _():
            for cp in fetch(ahead, lax.rem(ahead, RING_SLOTS)):
                cp.start()

        slot = lax.rem(i, RING_SLOTS)
        for cp in fetch(i, slot):
            cp.wait()
        dproj_t = jnp.concatenate([dp_ref[...], dq_ref[...], dk_ref[...], dv_ref[...], dag_ref[...]], axis=1)
        dh = _nt(dproj_t, wfull_ref[...])
        xf = xbuf[slot]
        r = lax.rsqrt(jnp.mean(xf * xf, axis=-1, keepdims=True) + EPS)
        xh = xf * r
        dg_ref[...] += jnp.sum(dh * xh, axis=0, keepdims=True)
        u = dh * g_ref[...]
        gx_ref[...] = dbuf[slot] + r * (u - xh * jnp.mean(u * xh, axis=-1, keepdims=True))

    tile = pl.BlockSpec((TOKEN_TILE, D_MODEL), lambda i: (i, 0))
    return pl.pallas_call(
        body,
        name="inproj_bwd_dx",
        grid=(t // TOKEN_TILE,),
        out_shape=(jax.ShapeDtypeStruct((t, D_MODEL), F32), jax.ShapeDtypeStruct((1, D_MODEL), F32)),
        in_specs=[pl.BlockSpec(memory_space=pl.ANY)] * 2 + _dproj_specs() + [
            pl.BlockSpec((1, D_MODEL), lambda i: (0, 0)),
            pl.BlockSpec(memory_space=pl.ANY),
        ],
        out_specs=(tile, pl.BlockSpec((1, D_MODEL), lambda i: (0, 0))),
        scratch_shapes=[pltpu.VMEM((D_MODEL, IN_WIDTH), BF16), pltpu.SemaphoreType.DMA((N_DEV,)),
                        pltpu.VMEM((RING_SLOTS, TOKEN_TILE, D_MODEL), F32),
                        pltpu.VMEM((RING_SLOTS, TOKEN_TILE, D_MODEL), F32),
                        pltpu.SemaphoreType.DMA((2, RING_SLOTS))],
        compiler_params=_params(("arbitrary",)),
    )(x2d, dx2, *dproj, g1, wg)


def _inproj_bwd_dw(x2d, pvg, dy_pool, dattn, g1, pool_w, pool_scale):
    t = x2d.shape[0]
    n_tiles = t // TOKEN_TILE
    halo_per_tile = TOKEN_TILE // HALO
    last_halo = t // HALO - 1

    def body(x_ref, cur_ref, prev_ref, pgn_ref, dy_ref, dyn_ref, dq_ref, dk_ref, dv_ref, dag_ref, g_ref, pw_ref, ps_ref,
             out_ref, dp_ref, dpw_ref, dps_ref, acc_ref):
        i = pl.program_id(0)

        @pl.when(i == 0)
        def _():
            acc_ref[...] = jnp.zeros_like(acc_ref)
            dpw_ref[...] = jnp.zeros_like(dpw_ref)
            dps_ref[...] = jnp.zeros_like(dps_ref)

        xf = x_ref[...]
        r = lax.rsqrt(jnp.mean(xf * xf, axis=-1, keepdims=True) + EPS)
        h = ((xf * r) * g_ref[...]).astype(BF16)
        half = ATTN_WIDTH // 2
        for gi, ref in enumerate((dq_ref, dk_ref, dv_ref, dag_ref)):
            pool_halves = _pool_bwd_group(gi, i, n_tiles, cur_ref, prev_ref, pgn_ref, dy_ref, dyn_ref, pw_ref, ps_ref,
                                          dp_ref, dpw_ref, dps_ref)
            for c0 in (0, half):
                col = 2 * POOL_WIDTH + gi * ATTN_WIDTH + c0
                acc_ref[:, col:col + half] += _tn(h, ref[:, c0:c0 + half])
                next(pool_halves, None)
        for c0 in (0, POOL_WIDTH):
            acc_ref[:, c0:c0 + POOL_WIDTH] += _tn(h, dp_ref[:, c0:c0 + POOL_WIDTH])

        @pl.when(i == n_tiles - 1)
        def _():
            for d in range(N_DEV):
                out_ref[d] = acc_ref[:, d * IN_SHARD:(d + 1) * IN_SHARD].astype(BF16)

    tile = lambda width: pl.BlockSpec((TOKEN_TILE, width), lambda i: (i, 0))
    next_halo = lambda col: pl.BlockSpec(
        (HALO, POOL_WIDTH), lambda i: (jnp.minimum((i + 1) * halo_per_tile, last_halo), col))
    return pl.pallas_call(
        body,
        name="inproj_bwd_dw",
        grid=(n_tiles,),
        out_shape=(
            jax.ShapeDtypeStruct((N_DEV, D_MODEL, IN_SHARD), BF16),
            jax.ShapeDtypeStruct((t, 2 * POOL_WIDTH), BF16),
            jax.ShapeDtypeStruct((N_GROUPS, GROUP_DIM, GROUP_DIM), F32),
            jax.ShapeDtypeStruct((1, POOL_WIDTH), F32),
        ),
        in_specs=[
            tile(D_MODEL),
            tile(2 * POOL_WIDTH),
            pl.BlockSpec((HALO, POOL_WIDTH), lambda i: (jnp.maximum(i * halo_per_tile - 1, 0), 0)),
            next_halo(1),
            tile(POOL_WIDTH),
            next_halo(0),
            tile(ATTN_WIDTH), tile(ATTN_WIDTH), tile(ATTN_WIDTH), tile(ATTN_WIDTH),
            pl.BlockSpec((1, D_MODEL), lambda i: (0, 0)),
            pl.BlockSpec((N_GROUPS, GROUP_DIM, GROUP_DIM), lambda i: (0, 0, 0)),
            pl.BlockSpec((1, POOL_WIDTH), lambda i: (0, 0)),
        ],
        out_specs=(
            pl.BlockSpec((N_DEV, D_MODEL, IN_SHARD), lambda i: (0, 0, 0)),
            tile(2 * POOL_WIDTH),
            pl.BlockSpec((N_GROUPS, GROUP_DIM, GROUP_DIM), lambda i: (0, 0, 0)),
            pl.BlockSpec((1, POOL_WIDTH), lambda i: (0, 0)),
        ),
        scratch_shapes=[pltpu.VMEM((D_MODEL, IN_WIDTH), F32)],
        compiler_params=_params(("arbitrary",)),
    )(x2d, pvg, pvg, pvg, dy_pool, dy_pool, *dattn, g1, pool_w, pool_scale)


_HBM = pl.BlockSpec(memory_space=pltpu.HBM)
_SEM = pl.BlockSpec(memory_space=pltpu.SEMAPHORE)
_DATAFLOW = pltpu.SideEffectType.DATAFLOW_SIDE_EFFECTING


_N_EXCHANGED = 3


def _exchange_all(refs, send_sems, recv_sems):
    win_hbm, win_land, pw_hbm, pw_land, vec_hbm, vec_land = refs
    return (_exchange_copies(win_hbm, win_land, send_sems, recv_sems)
            + _exchange_copies(pw_hbm, pw_land, send_sems, recv_sems, first_sem=N_DEV - 1)
            + _exchange_copies(vec_hbm, vec_land, send_sems, recv_sems, first_sem=2 * (N_DEV - 1), same_for_all=True))


def _exchange_start(win_blocks, pw_blocks, vec):
    arrays = []
    for a, land_shape in ((win_blocks, (N_DEV - 1,) + win_blocks.shape[1:]),
                          (pw_blocks, (N_DEV - 1,) + pw_blocks.shape[1:]),
                          (vec, (N_DEV - 1,) + vec.shape)):
        arrays += [pltpu.with_memory_space_constraint(a, pltpu.HBM),
                   pltpu.with_memory_space_constraint(lax.empty(land_shape, a.dtype), pltpu.HBM)]

    def body(*refs):
        ins, (send_sems, recv_sems), token = refs[:2 * _N_EXCHANGED], refs[2 * _N_EXCHANGED:2 * _N_EXCHANGED + 2], refs[-1]
        for cp in _exchange_all(ins, send_sems, recv_sems):
            cp.start()
        token[...] = jnp.zeros_like(token)

    sems = pltpu.SemaphoreType.DMA((_N_EXCHANGED * (N_DEV - 1),))
    return pl.pallas_call(
        body,
        name="exchange_start",
        out_shape=(sems, sems, *[pltpu.HBM(a.shape, a.dtype) for a in arrays], jax.ShapeDtypeStruct((8, LANES), F32)),
        in_specs=tuple([_HBM] * len(arrays)),
        out_specs=(_SEM, _SEM, *[_HBM] * len(arrays), pl.BlockSpec(memory_space=pltpu.VMEM)),
        input_output_aliases={k: 2 + k for k in range(len(arrays))},
        compiler_params=pltpu.CompilerParams(has_side_effects=_DATAFLOW),
    )(*arrays)


def _exchange_wait(send_sems, recv_sems, arrays, after):
    def body(*refs):
        ins = refs[:2 * _N_EXCHANGED]
        send_sems, recv_sems = refs[2 * _N_EXCHANGED:2 * _N_EXCHANGED + 2]
        for cp in _exchange_all(ins, send_sems, recv_sems):
            cp.wait_send()
            cp.wait_recv()

    return pl.pallas_call(
        body,
        name="exchange_wait",
        out_shape=tuple(pltpu.HBM(a.shape, a.dtype) for a in arrays),
        in_specs=(*[_HBM] * len(arrays), _SEM, _SEM, pl.BlockSpec(memory_space=pl.ANY)),
        out_specs=tuple([_HBM] * len(arrays)),
        input_output_aliases={k: k for k in range(len(arrays))},
        compiler_params=pltpu.CompilerParams(has_side_effects=_DATAFLOW),
    )(*arrays, send_sems, recv_sems, after)


def _adamw(w, g, m, v):
    m = ADAM_B1 * m + (1.0 - ADAM_B1) * g
    v = ADAM_B2 * v + (1.0 - ADAM_B2) * (g * g)
    m_hat = m / (1.0 - ADAM_B1 ** ADAM_STEP)
    v_hat = v / (1.0 - ADAM_B2 ** ADAM_STEP)
    delta = -ADAM_LR * (m_hat / (jnp.sqrt(v_hat) + ADAM_EPS) + ADAM_WD * w)
    return delta, m, v


def _small_allreduce(d_g1, vec, vec_land, pw_blocks, pw_land):
    def body(g1_ref, vec_ref, vland_ref, pwb_ref, pland_ref, vec_out, pw_out, vparts_ref, pparts_ref, rows_ref,
             slice_ref, send_sems, recv_sems):
        me = _dev_index(_mesh_pos())

        def from_devices(parts_ref, own, land_ref):
            parts_ref[0] = own
            parts_ref[1:] = land_ref[...]
            total = parts_ref[me]
            for s in range(1, N_DEV):
                total = total + parts_ref[me ^ s]
            return total

        slice_ref[...] = from_devices(pparts_ref, pwb_ref[me], pland_ref)

        def send(r, src, dst, k):
            return pltpu.make_async_remote_copy(
                src_ref=src, dst_ref=dst, send_sem=send_sems.at[2 * (r - 1) + k],
                recv_sem=recv_sems.at[2 * (r - 1) + k], device_id=_peer(r), device_id_type=MESH_ID)

        started = [cp for r in range(1, N_DEV)
                   for cp in (send(r, g1_ref, rows_ref.at[r], 0), send(r, slice_ref, pw_out.at[me], 1))]
        for cp in started:
            cp.start()
        rows_ref[0] = g1_ref[...]
        pw_out[me] = slice_ref[...]
        vec_out[...] = from_devices(vparts_ref, vec_ref[...], vland_ref)
        for cp in started:
            cp.wait_recv()
        for cp in started:
            cp.wait_send()
        g1 = rows_ref[me]
        for s in range(1, N_DEV):
            g1 = g1 + rows_ref[me ^ s]
        vec_out[_ROW_G1:_ROW_G1 + 1, :] = g1

    vm = pl.BlockSpec(memory_space=pltpu.VMEM)
    return pl.pallas_call(
        body,
        name="small_allreduce",
        out_shape=(jax.ShapeDtypeStruct((_VEC_ROWS, D_MODEL), F32),
                   jax.ShapeDtypeStruct((N_DEV, GROUP_DIM // 2, GROUP_DIM), F32)),
        in_specs=[vm] * 5,
        out_specs=(vm, vm),
        scratch_shapes=[
            pltpu.VMEM((N_DEV, _VEC_ROWS, D_MODEL), F32),
            pltpu.VMEM((N_DEV, GROUP_DIM // 2, GROUP_DIM), F32),
            pltpu.VMEM((N_DEV, 1, D_MODEL), F32),
            pltpu.VMEM((GROUP_DIM // 2, GROUP_DIM), F32),
            pltpu.SemaphoreType.DMA((2 * (N_DEV - 1),)),
            pltpu.SemaphoreType.DMA((2 * (N_DEV - 1),)),
        ],
        compiler_params=_params(),
    )(d_g1, vec, vec_land, pw_blocks, pw_land)


def _adamw_all(dwin_g, land_in, dwout_g, land_out, vec_sum, pw_sum, weights, big):
    small_shapes = [(1, D_MODEL), (1, D_MODEL), (1, POOL_WIDTH), (N_HEADS, N_REL), (N_GROUPS, GROUP_DIM, GROUP_DIM)]

    def body(*refs):
        refs = list(refs)
        take = lambda n: [refs.pop(0) for _ in range(n)]
        dwin_hbm, lin_ref, dwout_hbm, lout_ref, vec_ref, pw_ref = take(6)
        small_wmv = [take(3) for _ in range(5)]
        big_wmv = [take(3) for _ in range(2)]
        big_out = [take(4) for _ in range(2)]
        small_out = [take(4) for _ in range(5)]
        own_in, own_out, local_sems = refs

        me = _dev_index(_mesh_pos())
        mine = [pltpu.make_async_copy(dwin_hbm.at[me], own_in, local_sems.at[0]),
                pltpu.make_async_copy(dwout_hbm.at[me], own_out, local_sems.at[1])]
        for cp in mine:
            cp.start()

        def update(g, wmv, outs):
            delta, m_new, v_new = _adamw(wmv[0][...], g, wmv[1][...], wmv[2][...])
            for ref, val in zip(outs, (g, delta, m_new, v_new)):
                ref[...] = val

        update(vec_ref[_ROW_G1:_ROW_G1 + 1, :], small_wmv[0], small_out[0])
        update(vec_ref[_ROW_G2:_ROW_G2 + 1, :], small_wmv[1], small_out[1])
        update(vec_ref[_ROW_PS:_ROW_PS + 1, :POOL_WIDTH], small_wmv[2], small_out[2])
        update(vec_ref[_ROW_RB:_ROW_RB + N_HEADS, :N_REL], small_wmv[3], small_out[3])
        update(pw_ref[...], small_wmv[4], small_out[4])
        for cp in mine:
            cp.wait()
        g_in = own_in[...].astype(F32)
        g_out = own_out[...].astype(F32)
        for r in range(N_DEV - 1):
            g_in = g_in + lin_ref[r].astype(F32)
            g_out = g_out + lout_ref[r].astype(F32)
        update(g_in, big_wmv[0], big_out[0])
        update(g_out, big_wmv[1], big_out[1])

    vm = pl.BlockSpec(memory_space=pltpu.VMEM)
    hbm = pl.BlockSpec(memory_space=pl.ANY)
    f32 = lambda shape: jax.ShapeDtypeStruct(shape, F32)
    out_shapes = [f32((D_MODEL, IN_SHARD))] * 4 + [f32((OUT_SHARD, D_MODEL))] * 4
    for shape in small_shapes:
        out_shapes += [f32(shape)] * 4
    args = [dwin_g, land_in, dwout_g, land_out, vec_sum, pw_sum]
    for wmv in weights:
        args += list(wmv)
    for wmv in big:
        args += list(wmv)
    return pl.pallas_call(
        body,
        name="adamw_all",
        out_shape=tuple(out_shapes),
        in_specs=[hbm, vm, hbm, vm] + [vm] * (len(args) - 4),
        out_specs=tuple([vm] * len(out_shapes)),
        scratch_shapes=[
            pltpu.VMEM((D_MODEL, IN_SHARD), BF16),
            pltpu.VMEM((OUT_SHARD, D_MODEL), BF16),
            pltpu.SemaphoreType.DMA((2,)),
        ],
        compiler_params=_params(),
    )(*args)


def kernel(x, norm_gain, w_in, pool_w, pool_scale, rel_bias, w_out, final_norm_gain, loss_target, m_norm_gain, m_w_in, m_pool_w, m_pool_scale, m_rel_bias, m_w_out, m_final_norm_gain, v_norm_gain, v_w_in, v_pool_w, v_pool_scale, v_rel_bias, v_w_out, v_final_norm_gain):
    t = x.shape[1]
    assert x.shape[0] == 1 and t % TOKEN_TILE == 0 and t // Q_BLOCK >= 4
    x2d = x[0]
    tgt2d = loss_target[0]
    g2 = final_norm_gain.reshape(1, D_MODEL)

    rb = rel_bias[0]
    rel_line = jnp.concatenate([
        jnp.broadcast_to(rb[:, :1], (N_HEADS, _REL_FIRST)), rb,
        jnp.broadcast_to(rb[:, N_REL - 1:], (N_HEADS, TOEPLITZ - _REL_FIRST - N_REL)),
    ], axis=1).reshape(N_HEADS, 1, TOEPLITZ)
    wg_in, bias_tile = _gather_weights(w_in[0], rel_line)

    pvg, qkv, ag, y_pool = _norm_inproj(x2d, norm_gain, wg_in, pool_w[0], pool_scale)
    o, wg_out = _attn_fwd(qkv, bias_tile, w_out[0])
    dx2, dy_pool, do, dag, dwout_g, d_g2, loss_sum = _outproj_loss(x2d, tgt2d, y_pool, o, ag, wg_out, g2)
    dq, dk, dv, ds_band, ds_total, land_out = _attn_bwd(qkv, do, bias_tile, dwout_g)
    dwin_g, d_pool, d_pw, d_ps = _inproj_bwd_dw(x2d, pvg, dy_pool, (dq, dk, dv, dag), norm_gain, pool_w[0], pool_scale)
    vec = _pack_small(ds_band, ds_total, d_g2, d_ps, loss_sum)
    dproj = (d_pool, dq, dk, dv, dag)
    pw_blocks = d_pw.reshape(N_DEV, GROUP_DIM // 2, GROUP_DIM)
    send_sems, recv_sems, *exchanged, token = _exchange_start(dwin_g, pw_blocks, vec)
    grad_x, d_g1 = _inproj_bwd_dx(x2d, dx2, dproj, norm_gain + token[:1, :1], wg_in)
    dwin_g, land_in, pw_blocks, pw_land, vec, vec_land = _exchange_wait(send_sems, recv_sems, exchanged, d_g1)

    row = lambda a: a.reshape(1, D_MODEL)
    weights = [
        (norm_gain, m_norm_gain, v_norm_gain),
        (row(final_norm_gain), row(m_final_norm_gain), row(v_final_norm_gain)),
        (pool_scale, m_pool_scale, v_pool_scale),
        (rel_bias[0], m_rel_bias[0], v_rel_bias[0]),
        (pool_w[0], m_pool_w[0], v_pool_w[0]),
    ]
    big = [(w_in[0], m_w_in[0], v_w_in[0]), (w_out[0], m_w_out[0], v_w_out[0])]
    vec_sum, pw_sum = _small_allreduce(d_g1, vec, vec_land, pw_blocks, pw_land)
    res = _adamw_all(dwin_g, land_in, dwout_g, land_out, vec_sum, pw_sum.reshape(N_GROUPS, GROUP_DIM, GROUP_DIM),
                     weights, big)
    loss = 0.5 * vec_sum[_ROW_LOSS, 0]

    def leaves(k):
        g1_, g2_, ps_, rb_, pw_ = (res[8 + 4 * leaf + k] for leaf in range(5))
        return [g1_, res[k][None], pw_[None], ps_, rb_[None], res[4 + k][None], g2_.reshape(D_MODEL)]

    return (loss, grad_x[None], *leaves(0), *leaves(1), *leaves(2), *leaves(3))
```
